```python
import math
import jax, jax.numpy as jnp
from jax import lax
import numpy as np

D_MODEL = 1024
BATCH = 8
SEQ = 4096
DEPTH = 4

N_MIXERS = 2
D_FF = 4 * D_MODEL
RMS_EPS = 1e-6
NEG_INF = -1e30

GDN_HEADS = D_MODEL // 128
GDN_DK = 128
GDN_DV = 128
GDN_CONV = 5
GDN_CHUNK = 64
GDN_QKV = GDN_HEADS * (2 * GDN_DK + GDN_DV)
GDN_IN = GDN_QKV + GDN_HEADS * GDN_DV + 4 * GDN_HEADS

DSWA_CONFIGS = ((128, 1), (512, 4), (2048, 16))
DSWA_HEADS_PER_GROUP = 6
DSWA_HEAD_DIM = 64
DSWA_HEADS = len(DSWA_CONFIGS) * DSWA_HEADS_PER_GROUP
DSWA_WIDTH = DSWA_HEADS * DSWA_HEAD_DIM

REL_BUCKETS = 32
REL_MAX_DIST = 1024

N_LAYERS_A = (DEPTH + 1) // 2
N_LAYERS_B = DEPTH // 2

kernel_name = 'hybrid_gdn_dilated_swa_encoder'


def rmsnorm(x, g):
    xf = x.astype(jnp.float32)
    y = xf * lax.rsqrt(jnp.mean(xf * xf, axis=-1, keepdims=True) + RMS_EPS)
    return (y * g.astype(jnp.float32)).astype(x.dtype)


def l2norm(t):
    return t * lax.rsqrt(jnp.sum(t * t, axis=-1, keepdims=True) + 1e-6)


def rel_bucket(rel):
    nb = REL_BUCKETS // 2
    max_exact = nb // 2
    ret = jnp.where(rel > 0, nb, 0)
    n = jnp.abs(rel)
    nf = jnp.maximum(n, 1).astype(jnp.float32)
    large = max_exact + (jnp.log(nf / max_exact) / math.log(REL_MAX_DIST / max_exact)
                         * (nb - max_exact)).astype(jnp.int32)
    large = jnp.minimum(large, nb - 1)
    return ret + jnp.where(n < max_exact, n, large)


def gated_delta_chunked(q, k, v, g, beta):
    B, H, S, DK = q.shape
    DV = v.shape[-1]
    C = GDN_CHUNK
    nc = S // C
    q = q.reshape(B, H, nc, C, DK)
    k = k.reshape(B, H, nc, C, DK)
    v = v.reshape(B, H, nc, C, DV)
    g = g.reshape(B, H, nc, C)
    beta = beta.reshape(B, H, nc, C)
    G = jnp.cumsum(g, axis=-1)
    incl = jnp.tril(jnp.ones((C, C), bool))
    strict = jnp.tril(jnp.ones((C, C), bool), -1)
    diff = jnp.where(incl, G[..., :, None] - G[..., None, :], 0.0)
    decay = jnp.where(incl, jnp.exp(diff), 0.0)
    kb = k * beta[..., None]
    a = jnp.where(strict, jnp.einsum('bhnid,bhnjd->bhnij', kb, k) * decay, 0.0)
    rhs = jnp.concatenate([v * beta[..., None], kb * jnp.exp(G)[..., None]], axis=-1)
    sol = lax.linalg.triangular_solve(a, rhs, left_side=True, lower=True, unit_diagonal=True)
    u, w = sol[..., :DV], sol[..., DV:]
    intra = jnp.einsum('bhnid,bhnjd->bhnij', q, k) * decay
    g_last = G[..., -1]
    q_dec = q * jnp.exp(G)[..., None]
    k_dec = k * jnp.exp(g_last[..., None] - G)[..., None]
    xs = tuple(jnp.moveaxis(t, 2, 0) for t in (u, w, intra, q_dec, k_dec, g_last))

    def step(state, inp):
        u_c, w_c, intra_c, qd_c, kd_c, gl_c = inp
        v_new = u_c - jnp.einsum('bhck,bhkv->bhcv', w_c, state)
        o_c = (jnp.einsum('bhck,bhkv->bhcv', qd_c, state)
               + jnp.einsum('bhij,bhjv->bhiv', intra_c, v_new))
        state = (state * jnp.exp(gl_c)[..., None, None]
                 + jnp.einsum('bhck,bhcv->bhkv', kd_c, v_new))
        return state, o_c

    s0 = jnp.zeros((B, H, DK, DV), jnp.float32)
    _, o = lax.scan(step, s0, xs)
    return jnp.moveaxis(o, 0, 2).reshape(B, H, S, DV)


def gated_deltanet_bidir(h, w_in, conv_w, a_log, dt_bias, norm_w, w_out):
    B, S, _ = h.shape
    H, DK, DV = GDN_HEADS, GDN_DK, GDN_DV
    f32 = jnp.float32
    proj = h @ w_in
    qkv = proj[..., :GDN_QKV]
    z = proj[..., GDN_QKV:GDN_QKV + H * DV].reshape(B, S, H, DV)
    ab = proj[..., GDN_QKV + H * DV:]
    pad = GDN_CONV // 2
    qkv = jax.nn.silu(lax.conv_general_dilated(
        qkv, conv_w, window_strides=(1,), padding=[(pad, pad)],
        dimension_numbers=('NWC', 'WIO', 'NWC'), feature_group_count=GDN_QKV))
    qkv = qkv.astype(f32)
    q = qkv[..., :H * DK].reshape(B, S, H, DK)
    k = qkv[..., H * DK:2 * H * DK].reshape(B, S, H, DK)
    v = qkv[..., 2 * H * DK:].reshape(B, S, H, DV)
    q = l2norm(q) * (DK ** -0.5)
    k = l2norm(k)
    a = ab[..., :2 * H].reshape(B, S, 2, H).astype(f32)
    b = ab[..., 2 * H:].reshape(B, S, 2, H).astype(f32)
    g = -jnp.exp(a_log.astype(f32)) * jax.nn.softplus(a + dt_bias.astype(f32))
    beta = jax.nn.sigmoid(b)
    qh, kh, vh = (jnp.transpose(t, (0, 2, 1, 3)) for t in (q, k, v))
    gh = jnp.transpose(g, (2, 0, 3, 1))
    bh = jnp.transpose(beta, (2, 0, 3, 1))
    o_fwd = gated_delta_chunked(qh, kh, vh, gh[0], bh[0])
    flip = lambda t: jnp.flip(t, axis=2)
    o_bwd = flip(gated_delta_chunked(flip(qh), flip(kh), flip(vh), flip(gh[1]), flip(bh[1])))
    o = jnp.transpose(o_fwd + o_bwd, (0, 2, 1, 3))
    o = o * lax.rsqrt(jnp.mean(o * o, axis=-1, keepdims=True) + RMS_EPS)
    o = o * norm_w.astype(f32) * jax.nn.silu(z.astype(f32))
    return o.reshape(B, S, H * DV).astype(h.dtype) @ w_out


def banded_attention(q, k, v, bias, half):
    N, L, H, E = q.shape
    blk = half
    nb = -(-L // blk)
    lp = nb * blk
    qb = jnp.pad(q, ((0, 0), (0, lp - L), (0, 0), (0, 0))).reshape(N, nb, blk, H, E)

    def windows(t):
        tb = jnp.pad(t, ((0, 0), (blk, lp - L + blk), (0, 0), (0, 0))).reshape(N, nb + 2, blk, H, E)
        return jnp.concatenate([tb[:, :-2], tb[:, 1:-1], tb[:, 2:]], axis=2)

    kw, vw = windows(k), windows(v)
    s = jnp.einsum('nbqhe,nbkhe->nbhqk', qb, kw).astype(jnp.float32) * (E ** -0.5)
    s = s + bias[None, None]
    q_idx = jnp.arange(lp).reshape(nb, blk)
    k_idx = jnp.arange(nb)[:, None] * blk - blk + jnp.arange(3 * blk)[None, :]
    off = k_idx[:, None, :] - q_idx[:, :, None]
    valid = (jnp.abs(off) <= half) & (k_idx[:, None, :] >= 0) & (k_idx[:, None, :] < L)
    s = jnp.where(valid[None, :, None], s, NEG_INF)
    m = jnp.max(s, axis=-1, keepdims=True)
    lse = m + jnp.log(jnp.sum(jnp.exp(s - m), axis=-1, keepdims=True))
    p = jnp.exp(s - lse)
    o = jnp.einsum('nbhqk,nbkhe->nbqhe', p.astype(v.dtype), vw).reshape(N, lp, H, E)[:, :L]
    lse = jnp.transpose(lse[..., 0], (0, 1, 3, 2)).reshape(N, lp, H)[:, :L]
    return o, lse


def to_sub(t, d):
    B, S, Hh, E = t.shape
    return jnp.swapaxes(t.reshape(B, S // d, d, Hh, E), 1, 2).reshape(B * d, S // d, Hh, E)


def from_sub(t, d, B):
    L = t.shape[1]
    rest = t.shape[2:]
    return jnp.swapaxes(t.reshape(B, d, L, *rest), 1, 2).reshape(B, L * d, *rest)


def dilated_window_attention(h, w_in, w_out, rel_table):
    B, S, _ = h.shape
    Hg, E = DSWA_HEADS_PER_GROUP, DSWA_HEAD_DIM
    qkv = (h @ w_in).reshape(B, S, 3, DSWA_HEADS, E)
    q, k, v = qkv[:, :, 0], qkv[:, :, 1], qkv[:, :, 2]
    outs, lses = [], []
    for gi, (window, dil) in enumerate(DSWA_CONFIGS):
        half = window // (2 * dil)
        hs = slice(gi * Hg, (gi + 1) * Hg)
        rel = (jnp.arange(3 * half)[None, :] - half - jnp.arange(half)[:, None]) * dil
        bias = jnp.take(rel_table, rel_bucket(rel), axis=0)[..., hs]
        bias = jnp.transpose(bias, (2, 0, 1)).astype(jnp.float32)
        o, lse = banded_attention(to_sub(q[:, :, hs], dil), to_sub(k[:, :, hs], dil),
                                  to_sub(v[:, :, hs], dil), bias, half)
        outs.append(from_sub(o, dil, B))
        lses.append(from_sub(lse, dil, B))
    o = jnp.stack(outs, axis=2)
    alpha = jax.nn.softmax(jnp.stack(lses, axis=2), axis=2)
    o = (o * alpha[..., None].astype(o.dtype)).reshape(B, S, DSWA_WIDTH)
    return o @ w_out


def squared_relu_mlp(h, w1, w2):
    return jnp.square(jax.nn.relu(h @ w1)) @ w2


def _fwd_setup_inputs(seed: int = 0) -> dict:
    key = jax.random.key(seed)
    ks = jax.random.split(key, 16)
    f32 = jnp.float32
    nrm = lambda kk, shape, scale: jax.random.normal(kk, shape, f32) * scale
    x = nrm(ks[0], (BATCH, SEQ, D_MODEL), 1.0)
    norm_mix = 1.0 + nrm(ks[1], (DEPTH, D_MODEL), 0.02)
    norm_mlp = 1.0 + nrm(ks[2], (DEPTH, D_MODEL), 0.02)
    norm_final = 1.0 + nrm(ks[3], (D_MODEL,), 0.02)
    rel_bias = nrm(ks[4], (REL_BUCKETS, DSWA_HEADS), 0.2)
    gdn_w_in = nrm(ks[5], (N_LAYERS_A, D_MODEL, GDN_IN), D_MODEL ** -0.5)
    gdn_conv_w = nrm(ks[6], (N_LAYERS_A, GDN_CONV, 1, GDN_QKV), GDN_CONV ** -0.5)
    gdn_a_log = jnp.log(jax.random.uniform(ks[7], (N_LAYERS_A, 2, GDN_HEADS), f32, 1.0, 16.0))
    dt = jnp.exp(jax.random.uniform(ks[8], (N_LAYERS_A, 2, GDN_HEADS), f32,
                                    math.log(1e-3), math.log(1e-1)))
    gdn_dt_bias = dt + jnp.log(-jnp.expm1(-dt))
    gdn_norm_w = 1.0 + nrm(ks[9], (N_LAYERS_A, GDN_DV), 0.02)
    gdn_w_out = nrm(ks[10], (N_LAYERS_A, GDN_HEADS * GDN_DV, D_MODEL), (GDN_HEADS * GDN_DV) ** -0.5)
    dswa_w_in = nrm(ks[11], (N_LAYERS_B, D_MODEL, 3 * DSWA_WIDTH), D_MODEL ** -0.5)
    dswa_w_out = nrm(ks[12], (N_LAYERS_B, DSWA_WIDTH, D_MODEL), DSWA_WIDTH ** -0.5)
    mlp_w1 = nrm(ks[13], (DEPTH, D_MODEL, D_FF), D_MODEL ** -0.5)
    mlp_w2 = nrm(ks[14], (DEPTH, D_FF, D_MODEL), D_FF ** -0.5)
    return {'x': x, 'norm_mix': norm_mix, 'norm_mlp': norm_mlp, 'norm_final': norm_final,
            'rel_bias': rel_bias, 'gdn_w_in': gdn_w_in, 'gdn_conv_w': gdn_conv_w,
            'gdn_a_log': gdn_a_log, 'gdn_dt_bias': gdn_dt_bias, 'gdn_norm_w': gdn_norm_w,
            'gdn_w_out': gdn_w_out, 'dswa_w_in': dswa_w_in, 'dswa_w_out': dswa_w_out,
            'mlp_w1': mlp_w1, 'mlp_w2': mlp_w2}


def _fwd_reference(x, norm_mix, norm_mlp, norm_final, rel_bias, gdn_w_in, gdn_conv_w,
              gdn_a_log, gdn_dt_bias, gdn_norm_w, gdn_w_out, dswa_w_in, dswa_w_out,
              mlp_w1, mlp_w2):
    for i in range(DEPTH):
        h = rmsnorm(x, norm_mix[i])
        j = i // N_MIXERS
        if i % N_MIXERS == 0:
            y = gated_deltanet_bidir(h, gdn_w_in[j], gdn_conv_w[j], gdn_a_log[j],
                                     gdn_dt_bias[j], gdn_norm_w[j], gdn_w_out[j])
        else:
            y = dilated_window_attention(h, dswa_w_in[j], dswa_w_out[j], rel_bias)
        x = x + y
        h = rmsnorm(x, norm_mlp[i])
        x = x + squared_relu_mlp(h, mlp_w1[i], mlp_w2[i])
    return rmsnorm(x, norm_final)


import jax as _jax
import jax.numpy as _jnp

TWIN_FORMAT = 'train_step'
FWD_PARAMS = ['x', 'norm_mix', 'norm_mlp', 'norm_final', 'rel_bias', 'gdn_w_in', 'gdn_conv_w', 'gdn_a_log', 'gdn_dt_bias', 'gdn_norm_w', 'gdn_w_out', 'dswa_w_in', 'dswa_w_out', 'mlp_w1', 'mlp_w2']
TWIN_WEIGHTS = ['norm_mix', 'norm_mlp', 'norm_final', 'rel_bias', 'gdn_w_in', 'gdn_conv_w', 'gdn_a_log', 'gdn_dt_bias', 'gdn_norm_w', 'gdn_w_out', 'dswa_w_in', 'dswa_w_out', 'mlp_w1', 'mlp_w2']
TWIN_DIFF_INPUT = 'x'
TWIN_INPUTS = ['x', 'norm_mix', 'norm_mlp', 'norm_final', 'rel_bias', 'gdn_w_in', 'gdn_conv_w', 'gdn_a_log', 'gdn_dt_bias', 'gdn_norm_w', 'gdn_w_out', 'dswa_w_in', 'dswa_w_out', 'mlp_w1', 'mlp_w2', 'loss_target', 'm_norm_mix', 'm_norm_mlp', 'm_norm_final', 'm_rel_bias', 'm_gdn_w_in', 'm_gdn_conv_w', 'm_gdn_a_log', 'm_gdn_dt_bias', 'm_gdn_norm_w', 'm_gdn_w_out', 'm_dswa_w_in', 'm_dswa_w_out', 'm_mlp_w1', 'm_mlp_w2', 'v_norm_mix', 'v_norm_mlp', 'v_norm_final', 'v_rel_bias', 'v_gdn_w_in', 'v_gdn_conv_w', 'v_gdn_a_log', 'v_gdn_dt_bias', 'v_gdn_norm_w', 'v_gdn_w_out', 'v_dswa_w_in', 'v_dswa_w_out', 'v_mlp_w1', 'v_mlp_w2']
TWIN_OUTPUTS = ['loss', 'grad_x', 'grad_norm_mix', 'grad_norm_mlp', 'grad_norm_final', 'grad_rel_bias', 'grad_gdn_w_in', 'grad_gdn_conv_w', 'grad_gdn_a_log', 'grad_gdn_dt_bias', 'grad_gdn_norm_w', 'grad_gdn_w_out', 'grad_dswa_w_in', 'grad_dswa_w_out', 'grad_mlp_w1', 'grad_mlp_w2', 'delta_norm_mix', 'delta_norm_mlp', 'delta_norm_final', 'delta_rel_bias', 'delta_gdn_w_in', 'delta_gdn_conv_w', 'delta_gdn_a_log', 'delta_gdn_dt_bias', 'delta_gdn_norm_w', 'delta_gdn_w_out', 'delta_dswa_w_in', 'delta_dswa_w_out', 'delta_mlp_w1', 'delta_mlp_w2', 'new_m_norm_mix', 'new_m_norm_mlp', 'new_m_norm_final', 'new_m_rel_bias', 'new_m_gdn_w_in', 'new_m_gdn_conv_w', 'new_m_gdn_a_log', 'new_m_gdn_dt_bias', 'new_m_gdn_norm_w', 'new_m_gdn_w_out', 'new_m_dswa_w_in', 'new_m_dswa_w_out', 'new_m_mlp_w1', 'new_m_mlp_w2', 'new_v_norm_mix', 'new_v_norm_mlp', 'new_v_norm_final', 'new_v_rel_bias', 'new_v_gdn_w_in', 'new_v_gdn_conv_w', 'new_v_gdn_a_log', 'new_v_gdn_dt_bias', 'new_v_gdn_norm_w', 'new_v_gdn_w_out', 'new_v_dswa_w_in', 'new_v_dswa_w_out', 'new_v_mlp_w1', 'new_v_mlp_w2']
TWIN_LEAF_KINDS = {'loss': 'loss', 'grad_x': 'grad_x', 'grad_norm_mix': 'grad_w', 'grad_norm_mlp': 'grad_w', 'grad_norm_final': 'grad_w', 'grad_rel_bias': 'grad_w', 'grad_gdn_w_in': 'grad_w', 'grad_gdn_conv_w': 'grad_w', 'grad_gdn_a_log': 'grad_w', 'grad_gdn_dt_bias': 'grad_w', 'grad_gdn_norm_w': 'grad_w', 'grad_gdn_w_out': 'grad_w', 'grad_dswa_w_in': 'grad_w', 'grad_dswa_w_out': 'grad_w', 'grad_mlp_w1': 'grad_w', 'grad_mlp_w2': 'grad_w', 'delta_norm_mix': 'delta_w', 'delta_norm_mlp': 'delta_w', 'delta_norm_final': 'delta_w', 'delta_rel_bias': 'delta_w', 'delta_gdn_w_in': 'delta_w', 'delta_gdn_conv_w': 'delta_w', 'delta_gdn_a_log': 'delta_w', 'delta_gdn_dt_bias': 'delta_w', 'delta_gdn_norm_w': 'delta_w', 'delta_gdn_w_out': 'delta_w', 'delta_dswa_w_in': 'delta_w', 'delta_dswa_w_out': 'delta_w', 'delta_mlp_w1': 'delta_w', 'delta_mlp_w2': 'delta_w', 'new_m_norm_mix': 'new_m', 'new_m_norm_mlp': 'new_m', 'new_m_norm_final': 'new_m', 'new_m_rel_bias': 'new_m', 'new_m_gdn_w_in': 'new_m', 'new_m_gdn_conv_w': 'new_m', 'new_m_gdn_a_log': 'new_m', 'new_m_gdn_dt_bias': 'new_m', 'new_m_gdn_norm_w': 'new_m', 'new_m_gdn_w_out': 'new_m', 'new_m_dswa_w_in': 'new_m', 'new_m_dswa_w_out': 'new_m', 'new_m_mlp_w1': 'new_m', 'new_m_mlp_w2': 'new_m', 'new_v_norm_mix': 'new_v', 'new_v_norm_mlp': 'new_v', 'new_v_norm_final': 'new_v', 'new_v_rel_bias': 'new_v', 'new_v_gdn_w_in': 'new_v', 'new_v_gdn_conv_w': 'new_v', 'new_v_gdn_a_log': 'new_v', 'new_v_gdn_dt_bias': 'new_v', 'new_v_gdn_norm_w': 'new_v', 'new_v_gdn_w_out': 'new_v', 'new_v_dswa_w_in': 'new_v', 'new_v_dswa_w_out': 'new_v', 'new_v_mlp_w1': 'new_v', 'new_v_mlp_w2': 'new_v'}


def _forward(args):
    return _fwd_reference(*[args[k] for k in FWD_PARAMS])


def _output_shape():
    out = _jax.eval_shape(lambda: _forward(_fwd_setup_inputs(0)))
    return out.shape, out.dtype

N_MICROBATCH = 1
ADAM_LR = 0.001
ADAM_B1 = 0.9
ADAM_B2 = 0.999
ADAM_EPS = 1e-08
ADAM_WD = 0.01
ADAM_STEP = 10
PER_EXAMPLE_BATCH_AXIS = {'x': 0, 'loss_target': 0}
SHARED_INPUTS = []
_WEIGHT_DTYPES = {'norm_mix': _jnp.float32, 'norm_mlp': _jnp.float32, 'norm_final': _jnp.float32, 'rel_bias': _jnp.float32, 'gdn_w_in': _jnp.float32, 'gdn_conv_w': _jnp.float32, 'gdn_a_log': _jnp.float32, 'gdn_dt_bias': _jnp.float32, 'gdn_norm_w': _jnp.float32, 'gdn_w_out': _jnp.float32, 'dswa_w_in': _jnp.float32, 'dswa_w_out': _jnp.float32, 'mlp_w1': _jnp.float32, 'mlp_w2': _jnp.float32}
MOMENT_SCALE = {'norm_mix': 1.037448e-01, 'norm_mlp': 1.368608e-01, 'norm_final': 3.297229e+01, 'rel_bias': 1.280323e-02, 'gdn_w_in': 7.097488e-02, 'gdn_conv_w': 6.357729e-02, 'gdn_a_log': 2.315371e-01, 'gdn_dt_bias': 2.287872e-01, 'gdn_norm_w': 2.582919e-01, 'gdn_w_out': 9.415103e-02, 'dswa_w_in': 1.058573e-02, 'dswa_w_out': 1.737591e-02, 'mlp_w1': 6.743461e-02, 'mlp_w2': 1.297542e-01}


def _to_microbatches(a, axis):
    t = _jnp.moveaxis(a, axis, 0)
    t = t.reshape((N_MICROBATCH, t.shape[0] // N_MICROBATCH) + t.shape[1:])
    return _jnp.moveaxis(t, 1, axis + 1)


def setup_inputs(seed: int = 0) -> dict:
    inp = _fwd_setup_inputs(seed)
    key = _jax.random.fold_in(_jax.random.key(seed), 7919)
    shape, _ = _output_shape()
    out = dict(inp)
    out["loss_target"] = _jax.random.normal(_jax.random.fold_in(key, 0), shape, _jnp.float32)
    for i, name in enumerate(TWIN_WEIGHTS):
        w = inp[name].astype(_jnp.float32)
        if MOMENT_SCALE is None:
            s = _jnp.sqrt(_jnp.mean(_jnp.square(w)) + 1e-30)
        else:
            s = MOMENT_SCALE[name]
        km, kv = _jax.random.split(_jax.random.fold_in(key, i + 1))
        out[name] = w
        out["m_" + name] = s * _jax.random.normal(km, w.shape, _jnp.float32)
        out["v_" + name] = (s * s) * _jax.random.uniform(kv, w.shape, _jnp.float32, 0.5, 1.5)
    if N_MICROBATCH > 1:
        for name, axis in PER_EXAMPLE_BATCH_AXIS.items():
            out[name] = _to_microbatches(out[name], axis)
    return {'x': out['x'], 'norm_mix': out['norm_mix'], 'norm_mlp': out['norm_mlp'], 'norm_final': out['norm_final'], 'rel_bias': out['rel_bias'], 'gdn_w_in': out['gdn_w_in'], 'gdn_conv_w': out['gdn_conv_w'], 'gdn_a_log': out['gdn_a_log'], 'gdn_dt_bias': out['gdn_dt_bias'], 'gdn_norm_w': out['gdn_norm_w'], 'gdn_w_out': out['gdn_w_out'], 'dswa_w_in': out['dswa_w_in'], 'dswa_w_out': out['dswa_w_out'], 'mlp_w1': out['mlp_w1'], 'mlp_w2': out['mlp_w2'], 'loss_target': out['loss_target'], 'm_norm_mix': out['m_norm_mix'], 'm_norm_mlp': out['m_norm_mlp'], 'm_norm_final': out['m_norm_final'], 'm_rel_bias': out['m_rel_bias'], 'm_gdn_w_in': out['m_gdn_w_in'], 'm_gdn_conv_w': out['m_gdn_conv_w'], 'm_gdn_a_log': out['m_gdn_a_log'], 'm_gdn_dt_bias': out['m_gdn_dt_bias'], 'm_gdn_norm_w': out['m_gdn_norm_w'], 'm_gdn_w_out': out['m_gdn_w_out'], 'm_dswa_w_in': out['m_dswa_w_in'], 'm_dswa_w_out': out['m_dswa_w_out'], 'm_mlp_w1': out['m_mlp_w1'], 'm_mlp_w2': out['m_mlp_w2'], 'v_norm_mix': out['v_norm_mix'], 'v_norm_mlp': out['v_norm_mlp'], 'v_norm_final': out['v_norm_final'], 'v_rel_bias': out['v_rel_bias'], 'v_gdn_w_in': out['v_gdn_w_in'], 'v_gdn_conv_w': out['v_gdn_conv_w'], 'v_gdn_a_log': out['v_gdn_a_log'], 'v_gdn_dt_bias': out['v_gdn_dt_bias'], 'v_gdn_norm_w': out['v_gdn_norm_w'], 'v_gdn_w_out': out['v_gdn_w_out'], 'v_dswa_w_in': out['v_dswa_w_in'], 'v_dswa_w_out': out['v_dswa_w_out'], 'v_mlp_w1': out['v_mlp_w1'], 'v_mlp_w2': out['v_mlp_w2']}


def _loss(weights, diff, rest, loss_target):
    with _jax.named_scope("forward"):
        args = {**rest, TWIN_DIFF_INPUT: diff, **{k: w.astype(_WEIGHT_DTYPES[k]) for k, w in weights.items()}}
        y = _forward(args)
    with _jax.named_scope("loss_head"):
        err = _jnp.square(y.astype(_jnp.float32) - loss_target)
        return 0.5 * _jnp.sum(_jnp.mean(err, axis=-1)) if err.ndim else 0.5 * err


def _adamw(w, g, m, v):
    m = ADAM_B1 * m + (1.0 - ADAM_B1) * g
    v = ADAM_B2 * v + (1.0 - ADAM_B2) * _jnp.square(g)
    m_hat = m / (1.0 - ADAM_B1 ** ADAM_STEP)
    v_hat = v / (1.0 - ADAM_B2 ** ADAM_STEP)
    delta = -ADAM_LR * (m_hat / (_jnp.sqrt(v_hat) + ADAM_EPS) + ADAM_WD * w)
    return delta, m, v


def reference(x, norm_mix, norm_mlp, norm_final, rel_bias, gdn_w_in, gdn_conv_w, gdn_a_log, gdn_dt_bias, gdn_norm_w, gdn_w_out, dswa_w_in, dswa_w_out, mlp_w1, mlp_w2, loss_target, m_norm_mix, m_norm_mlp, m_norm_final, m_rel_bias, m_gdn_w_in, m_gdn_conv_w, m_gdn_a_log, m_gdn_dt_bias, m_gdn_norm_w, m_gdn_w_out, m_dswa_w_in, m_dswa_w_out, m_mlp_w1, m_mlp_w2, v_norm_mix, v_norm_mlp, v_norm_final, v_rel_bias, v_gdn_w_in, v_gdn_conv_w, v_gdn_a_log, v_gdn_dt_bias, v_gdn_norm_w, v_gdn_w_out, v_dswa_w_in, v_dswa_w_out, v_mlp_w1, v_mlp_w2):
    given = dict(x=x, norm_mix=norm_mix, norm_mlp=norm_mlp, norm_final=norm_final, rel_bias=rel_bias, gdn_w_in=gdn_w_in, gdn_conv_w=gdn_conv_w, gdn_a_log=gdn_a_log, gdn_dt_bias=gdn_dt_bias, gdn_norm_w=gdn_norm_w, gdn_w_out=gdn_w_out, dswa_w_in=dswa_w_in, dswa_w_out=dswa_w_out, mlp_w1=mlp_w1, mlp_w2=mlp_w2, loss_target=loss_target, m_norm_mix=m_norm_mix, m_norm_mlp=m_norm_mlp, m_norm_final=m_norm_final, m_rel_bias=m_rel_bias, m_gdn_w_in=m_gdn_w_in, m_gdn_conv_w=m_gdn_conv_w, m_gdn_a_log=m_gdn_a_log, m_gdn_dt_bias=m_gdn_dt_bias, m_gdn_norm_w=m_gdn_norm_w, m_gdn_w_out=m_gdn_w_out, m_dswa_w_in=m_dswa_w_in, m_dswa_w_out=m_dswa_w_out, m_mlp_w1=m_mlp_w1, m_mlp_w2=m_mlp_w2, v_norm_mix=v_norm_mix, v_norm_mlp=v_norm_mlp, v_norm_final=v_norm_final, v_rel_bias=v_rel_bias, v_gdn_w_in=v_gdn_w_in, v_gdn_conv_w=v_gdn_conv_w, v_gdn_a_log=v_gdn_a_log, v_gdn_dt_bias=v_gdn_dt_bias, v_gdn_norm_w=v_gdn_norm_w, v_gdn_w_out=v_gdn_w_out, v_dswa_w_in=v_dswa_w_in, v_dswa_w_out=v_dswa_w_out, v_mlp_w1=v_mlp_w1, v_mlp_w2=v_mlp_w2)
    weights = {n: given[n] for n in TWIN_WEIGHTS}
    shared = {n: given[n] for n in SHARED_INPUTS}
    per_example = {n: given[n] for n in ['x']}
    grad_fn = _jax.value_and_grad(_loss, argnums=(0, 1))

    def one_microbatch(ex, loss_target):
        ex = dict(ex)
        diff = ex.pop(TWIN_DIFF_INPUT)
        return grad_fn(weights, diff, {**shared, **ex}, loss_target)

    if N_MICROBATCH == 1:
        loss, (grad_w, grad_x) = one_microbatch(per_example, given["loss_target"])
    else:
        def body(carry, xs):
            loss_sum, grad_sum = carry
            l_k, (gw_k, gx_k) = one_microbatch(xs[0], xs[1])
            with _jax.named_scope("update"):
                return (loss_sum + l_k, _jax.tree.map(_jnp.add, grad_sum, gw_k)), gx_k

        init = (_jnp.zeros((), _jnp.float32), _jax.tree.map(_jnp.zeros_like, weights))
        (loss, grad_w), grad_x = _jax.lax.scan(body, init, (per_example, given["loss_target"]))
    with _jax.named_scope("update"):
        delta_w, new_m, new_v = {}, {}, {}
        for n in TWIN_WEIGHTS:
            delta_w[n], new_m[n], new_v[n] = _adamw(weights[n], grad_w[n], given["m_" + n], given["v_" + n])
    return (loss, grad_x, *[grad_w[n] for n in TWIN_WEIGHTS], *[delta_w[n] for n in TWIN_WEIGHTS],
            *[new_m[n] for n in TWIN_WEIGHTS], *[new_v[n] for n in TWIN_WEIGHTS])
```

```python
import functools
import math

import numpy as np
import jax
import jax.numpy as jnp
from jax import lax
from jax.experimental import pallas as pl
from jax.experimental.pallas import tpu as pltpu

F32 = jnp.float32
BF16 = jnp.bfloat16
HI = lax.Precision.HIGHEST
BS = pl.BlockSpec
SDS = jax.ShapeDtypeStruct
MESH = pl.DeviceIdType.MESH

D_MODEL = 1024
D_FF = 4096
DEPTH = 4
RMS_EPS = 1e-6
NEG_INF = -1e30
LANES = 128
VMEM_LIMIT = 56 << 20

GDN_H = 8
GDN_DK = 128
GDN_CONV = 5
GDN_C = 64
GDN_GC = 8
GDN_QKV = 3 * GDN_H * GDN_DK
GDN_IN = GDN_QKV + GDN_H * GDN_DK + 4 * GDN_H
GDN_INP = 4224

DSWA_CFG = ((128, 1), (512, 4), (2048, 16))
DSWA_HG = 6
DSWA_E = 64
DSWA_HEADS = 18
DSWA_W = DSWA_HEADS * DSWA_E
DSWA_HALF = 64
REL_BUCKETS = 32
REL_MAX_DIST = 1024

ADAM_LR = 0.001
ADAM_B1 = 0.9
ADAM_B2 = 0.999
ADAM_EPS = 1e-08
ADAM_WD = 0.01
ADAM_STEP = 10


def _cp(sem=None):
    return pltpu.CompilerParams(dimension_semantics=sem, vmem_limit_bytes=VMEM_LIMIT)


def _dot(a, b, prec=None):
    return jnp.dot(a, b, precision=prec, preferred_element_type=F32)


def _dot_nt(a, b, prec=None):
    return lax.dot_general(a, b, (((1,), (1,)), ((), ())), precision=prec, preferred_element_type=F32)


def _dot_tn(a, b, prec=None):
    return lax.dot_general(a, b, (((0,), (0,)), ((), ())), precision=prec, preferred_element_type=F32)


def _bf(a):
    return a.astype(BF16)


def _sigmoid(x):
    return 1.0 / (1.0 + jnp.exp(-x))


def rms_fwd(x, g, name):
    S, Dm = x.shape
    tm = min(512, S)

    def body(x_ref, g_ref, o_ref):
        xv = x_ref[...]
        r = lax.rsqrt(jnp.mean(xv * xv, axis=-1, keepdims=True) + RMS_EPS)
        o_ref[...] = (xv * r * g_ref[...]).astype(o_ref.dtype)

    return pl.pallas_call(
        body, grid=(S // tm,),
        in_specs=[BS((tm, Dm), lambda i: (i, 0)), BS((1, Dm), lambda i: (0, 0))],
        out_specs=BS((tm, Dm), lambda i: (i, 0)),
        out_shape=SDS((S, Dm), BF16), name=name, compiler_params=_cp(("parallel",)))(x, g)


def rms_bwd(x, g, dh, dres, name):
    S, Dm = x.shape
    tm = min(512, S)

    def body(x_ref, g_ref, dh_ref, dres_ref, dx_ref, dxb_ref, dg_ref):
        i = pl.program_id(0)
        xv = x_ref[...]
        r = lax.rsqrt(jnp.mean(xv * xv, axis=-1, keepdims=True) + RMS_EPS)
        n = xv * r
        dhv = dh_ref[...]
        t = dhv * g_ref[...]
        dx = dres_ref[...] + r * (t - n * jnp.mean(n * t, axis=-1, keepdims=True))
        dx_ref[...] = dx
        dxb_ref[...] = dx.astype(BF16)
        part = jnp.sum(dhv * n, axis=0, keepdims=True)

        @pl.when(i == 0)
        def _():
            dg_ref[...] = part

        @pl.when(i > 0)
        def _():
            dg_ref[...] += part

    row = BS((tm, Dm), lambda i: (i, 0))
    vec = BS((1, Dm), lambda i: (0, 0))
    return pl.pallas_call(
        body, grid=(S // tm,), in_specs=[row, vec, row, row], out_specs=[row, row, vec],
        out_shape=[SDS((S, Dm), F32), SDS((S, Dm), BF16), SDS((1, Dm), F32)],
        name=name, compiler_params=_cp(("arbitrary",)))(x, g, dh, dres)


def loss_head(x, g, tgt, name):
    S, Dm = x.shape
    tm = min(512, S)

    def body(x_ref, g_ref, t_ref, loss_ref, dx_ref, dxb_ref, dg_ref):
        i = pl.program_id(0)
        xv = x_ref[...]
        gv = g_ref[...]
        r = lax.rsqrt(jnp.mean(xv * xv, axis=-1, keepdims=True) + RMS_EPS)
        n = xv * r
        err = n * gv - t_ref[...]
        lpart = 0.5 * jnp.sum(jnp.mean(err * err, axis=-1, keepdims=True), axis=0, keepdims=True)
        dout = err * (1.0 / Dm)
        t = dout * gv
        dx = r * (t - n * jnp.mean(n * t, axis=-1, keepdims=True))
        dx_ref[...] = dx
        dxb_ref[...] = dx.astype(BF16)
        part = jnp.sum(dout * n, axis=0, keepdims=True)

        @pl.when(i == 0)
        def _():
            dg_ref[...] = part
            loss_ref[...] = lpart

        @pl.when(i > 0)
        def _():
            dg_ref[...] += part
            loss_ref[...] += lpart

    row = BS((tm, Dm), lambda i: (i, 0))
    vec = BS((1, Dm), lambda i: (0, 0))
    one = BS((1, 1), lambda i: (0, 0))
    return pl.pallas_call(
        body, grid=(S // tm,), in_specs=[row, vec, row], out_specs=[one, row, row, vec],
        out_shape=[SDS((1, 1), F32), SDS((S, Dm), F32), SDS((S, Dm), BF16), SDS((1, Dm), F32)],
        name=name, compiler_params=_cp(("arbitrary",)))(x, g, tgt)


def mm(a, b, *, name, ta=False, tb=False, tm=512, tn=512, tk=None, out_dtype=F32, pre_a=None, epi=None,
       extras=()):
    M, K = (a.shape[1], a.shape[0]) if ta else a.shape
    N = b.shape[0] if tb else b.shape[1]
    tm, tn = min(tm, M), min(tn, N)
    tk = K if tk is None else min(tk, K)
    assert M % tm == 0 and N % tn == 0 and K % tk == 0, (name, M, N, K, tm, tn, tk)
    nk = K // tk
    ne = len(extras)
    a_spec = BS((tk, tm), lambda i, j, k: (k, i)) if ta else BS((tm, tk), lambda i, j, k: (i, k))
    b_spec = BS((tn, tk), lambda i, j, k: (j, k)) if tb else BS((tk, tn), lambda i, j, k: (k, j))
    o_spec = BS((tm, tn), lambda i, j, k: (i, j))
    dims = (((0 if ta else 1,), (1 if tb else 0,)), ((), ()))

    def body(a_ref, b_ref, *rest):
        e_refs, o_ref = rest[:ne], rest[ne]
        av = a_ref[...]
        if pre_a is not None:
            av = pre_a(av)
        p = lax.dot_general(_bf(av), _bf(b_ref[...]), dims, preferred_element_type=F32)

        def finish(acc):
            res = epi(acc, *[e[...] for e in e_refs]) if epi is not None else acc
            o_ref[...] = res.astype(o_ref.dtype)

        if nk == 1:
            finish(p)
        else:
            acc_ref = rest[ne + 1]
            k = pl.program_id(2)

            @pl.when(k == 0)
            def _():
                acc_ref[...] = p

            @pl.when(k > 0)
            def _():
                acc_ref[...] += p

            @pl.when(k == nk - 1)
            def _():
                finish(acc_ref[...])

    return pl.pallas_call(
        body, grid=(M // tm, N // tn, nk), in_specs=[a_spec, b_spec] + [o_spec] * ne, out_specs=o_spec,
        out_shape=SDS((M, N), out_dtype),
        scratch_shapes=[pltpu.VMEM((tm, tn), F32)] if nk > 1 else [],
        name=name, compiler_params=_cp(("parallel", "parallel", "arbitrary")))(a, b, *extras)


def _relu(acc):
    return jnp.maximum(acc, 0.0)


def _add(acc, res):
    return acc + res


def _sq(av):
    af = av.astype(F32)
    return af * af


def _times_2r(acc, r):
    return acc * (2.0 * r.astype(F32))


def mlp_fwd(x, g, w1, w2, tag):
    h = rms_fwd(x, g, f"{tag}_rms")
    r = mm(h, w1, name=f"{tag}_up", tn=1024, out_dtype=BF16, epi=_relu)
    xn = mm(r, w2, name=f"{tag}_down", pre_a=_sq, epi=_add, extras=(x,))
    return xn, (h, r)


def mlp_bwd(x, g, w1, w2, saved, dx, dxb, tag):
    h, r = saved
    da = mm(dxb, w2, name=f"{tag}_dact", tb=True, tn=1024, out_dtype=BF16, epi=_times_2r, extras=(r,))
    dw2 = mm(r, dxb, name=f"{tag}_dw2", ta=True, pre_a=_sq)
    dw1 = mm(h, da, name=f"{tag}_dw1", ta=True)
    dh = mm(da, w1, name=f"{tag}_dh", tb=True)
    dx, dxb, dg = rms_bwd(x, g, dh, dx, f"{tag}_rmsb")
    return dx, dxb, dg, dw1, dw2


def _conv_taps(x, S):
    t = lax.broadcasted_iota(jnp.int32, x.shape, 0)
    taps = []
    for j in range(GDN_CONV):
        sh = j - GDN_CONV // 2
        xs = x if sh == 0 else pltpu.roll(x, (-sh) % S, 0)
        taps.append(jnp.where((t + sh >= 0) & (t + sh < S), xs, 0.0))
    return taps


def _qkv_scale(c):
    is_norm = c < 2 * GDN_H
    scale = jnp.where(c < GDN_H, GDN_DK ** -0.5, 1.0)
    return is_norm, scale


def gdn_pre_fwd(proj, convw, name):
    S = proj.shape[0]

    def body(p_ref, w_ref, o_ref):
        c = pl.program_id(0)
        x = p_ref[...]
        w = w_ref[...]
        y = jnp.zeros_like(x)
        for j, xs in enumerate(_conv_taps(x, S)):
            y = y + w[j:j + 1, :] * xs
        t = y * _sigmoid(y)
        is_norm, scale = _qkv_scale(c)
        r = lax.rsqrt(jnp.sum(t * t, axis=-1, keepdims=True) + 1e-6)
        o_ref[...] = jnp.where(is_norm, t * r * scale, t)

    return pl.pallas_call(
        body, grid=(GDN_QKV // LANES,),
        in_specs=[BS((S, LANES), lambda c: (0, c)), BS((8, LANES), lambda c: (0, c))],
        out_specs=BS((S, LANES), lambda c: (0, c)),
        out_shape=SDS((S, GDN_QKV), F32), name=name, compiler_params=_cp(("parallel",)))(proj, convw)


def gdn_pre_bwd(proj, convw, dqkv, name):
    S = proj.shape[0]

    def body(p_ref, w_ref, d_ref, dp_ref, dw_ref):
        c = pl.program_id(0)
        x = p_ref[...]
        w = w_ref[...]
        taps = _conv_taps(x, S)
        y = jnp.zeros_like(x)
        for j, xs in enumerate(taps):
            y = y + w[j:j + 1, :] * xs
        sg = _sigmoid(y)
        t = y * sg
        is_norm, scale = _qkv_scale(c)
        dout = d_ref[0, 0] + d_ref[1, 0]
        r = lax.rsqrt(jnp.sum(t * t, axis=-1, keepdims=True) + 1e-6)
        n = t * r
        dn = dout * scale
        dt_norm = r * (dn - n * jnp.sum(dn * n, axis=-1, keepdims=True))
        dt = jnp.where(is_norm, dt_norm, dout)
        dy = dt * (sg * (1.0 + y * (1.0 - sg)))
        row = lax.broadcasted_iota(jnp.int32, (8, LANES), 0)
        dw = jnp.zeros((8, LANES), F32)
        for j, xs in enumerate(taps):
            dw = dw + jnp.where(row == j, jnp.sum(dy * xs, axis=0, keepdims=True), 0.0)
        dw_ref[...] = dw
        tt = lax.broadcasted_iota(jnp.int32, x.shape, 0)
        dx = jnp.zeros_like(x)
        for j in range(GDN_CONV):
            sh = j - GDN_CONV // 2
            ds = dy if sh == 0 else pltpu.roll(dy, sh % S, 0)
            dx = dx + w[j:j + 1, :] * jnp.where((tt - sh >= 0) & (tt - sh < S), ds, 0.0)
        dp_ref[...] = dx.astype(BF16)

    return pl.pallas_call(
        body, grid=(GDN_QKV // LANES,),
        in_specs=[BS((S, LANES), lambda c: (0, c)), BS((8, LANES), lambda c: (0, c)),
                  BS((2, 1, S, LANES), lambda c: (0, c // GDN_H, 0, c % GDN_H))],
        out_specs=[BS((S, LANES), lambda c: (0, c)), BS((8, LANES), lambda c: (0, c))],
        out_shape=[SDS((S, GDN_QKV), BF16), SDS((8, GDN_QKV), F32)],
        name=name, compiler_params=_cp(("parallel",)))(proj, convw, dqkv)


def _chunk_sum_matrix(n, upper):
    i = lax.broadcasted_iota(jnp.int32, (n, n), 0)
    j = lax.broadcasted_iota(jnp.int32, (n, n), 1)
    same = (i // GDN_C) == (j // GDN_C)
    tri = (i <= j) if upper else (i >= j)
    return jnp.where(same & tri, 1.0, 0.0).astype(F32)


def _gate_lanes(shape):
    lane = lax.broadcasted_iota(jnp.int32, shape, 1)
    return lane < GDN_H, (lane >= GDN_H) & (lane < 2 * GDN_H), (lane >= 2 * GDN_H) & (lane < 4 * GDN_H)


def gdn_gate_fwd(proj, prm, name):
    S = proj.shape[0]
    tm = min(512, S)
    ct = GDN_INP // LANES - 1

    def body(p_ref, prm_ref, o_ref):
        ab = p_ref[...]
        a_log = prm_ref[0:1, :]
        dtb = prm_ref[1:2, :]
        z = ab + dtb
        sp = jnp.maximum(z, 0.0) + jnp.log(1.0 + jnp.exp(-jnp.abs(z)))
        g = -jnp.exp(a_log) * sp
        is_f, is_b, is_beta = _gate_lanes(ab.shape)
        gf = _dot(_chunk_sum_matrix(tm, False), jnp.where(is_f, g, 0.0), HI)
        gbk = _dot(_chunk_sum_matrix(tm, True), jnp.where(is_b, g, 0.0), HI)
        o_ref[...] = gf + gbk + jnp.where(is_beta, _sigmoid(ab), 0.0)

    return pl.pallas_call(
        body, grid=(S // tm,),
        in_specs=[BS((tm, LANES), lambda i: (i, ct)), BS((8, LANES), lambda i: (0, 0))],
        out_specs=BS((tm, LANES), lambda i: (i, 0)),
        out_shape=SDS((S, LANES), F32), name=name, compiler_params=_cp(("parallel",)))(proj, prm)


def gdn_gate_bwd(proj, prm, dgb, name):
    S = proj.shape[0]
    tm = min(512, S)
    ct = GDN_INP // LANES - 1

    def body(p_ref, prm_ref, d_ref, dab_ref, dprm_ref):
        i = pl.program_id(0)
        ab = p_ref[...]
        a_log = prm_ref[0:1, :]
        dtb = prm_ref[1:2, :]
        z = ab + dtb
        sp = jnp.maximum(z, 0.0) + jnp.log(1.0 + jnp.exp(-jnp.abs(z)))
        ea = jnp.exp(a_log)
        g = -ea * sp
        is_f, is_b, is_beta = _gate_lanes(ab.shape)
        d = d_ref[...]
        dg = (_dot_tn(_chunk_sum_matrix(tm, False), jnp.where(is_f, d, 0.0), HI)
              + _dot_tn(_chunk_sum_matrix(tm, True), jnp.where(is_b, d, 0.0), HI))
        da = dg * (-ea) * _sigmoid(z)
        beta = _sigmoid(ab)
        dab_ref[...] = jnp.where(is_beta, d * beta * (1.0 - beta), da).astype(BF16)
        row = lax.broadcasted_iota(jnp.int32, (8, LANES), 0)
        part = (jnp.where(row == 0, jnp.sum(dg * g, axis=0, keepdims=True), 0.0)
                + jnp.where(row == 1, jnp.sum(da, axis=0, keepdims=True), 0.0))

        @pl.when(i == 0)
        def _():
            dprm_ref[...] = part

        @pl.when(i > 0)
        def _():
            dprm_ref[...] += part

    return pl.pallas_call(
        body, grid=(S // tm,),
        in_specs=[BS((tm, LANES), lambda i: (i, ct)), BS((8, LANES), lambda i: (0, 0)), BS((tm, LANES), lambda i: (i, 0))],
        out_specs=[BS((tm, LANES), lambda i: (i, 0)), BS((8, LANES), lambda i: (0, 0))],
        out_shape=[SDS((S, LANES), BF16), SDS((8, LANES), F32)],
        name=name, compiler_params=_cp(("arbitrary",)))(proj, prm, dgb)


def _tri_masks(d):
    i = lax.broadcasted_iota(jnp.int32, (GDN_C, GDN_C), 0)
    j = lax.broadcasted_iota(jnp.int32, (GDN_C, GDN_C), 1)
    s = (i - j) * (1 - 2 * d)
    return s >= 0, s > 0


def _inv_unit_tri(a):
    i = lax.broadcasted_iota(jnp.int32, a.shape, 0)
    j = lax.broadcasted_iota(jnp.int32, a.shape, 1)
    m = -a
    p = jnp.where(i == j, 1.0, 0.0) + m
    for _ in range(int(math.log2(GDN_C)) - 1):
        m = _dot(m, m, HI)
        p = p + _dot(p, m, HI)
    return p


def _lane_col(x, lane_idx):
    lane = lax.broadcasted_iota(jnp.int32, x.shape, 1)
    return jnp.sum(jnp.where(lane == lane_idx, x, 0.0), axis=1, keepdims=True)


def _chunk_terms(q, k, v, gcol, grow, bcol, d):
    incl, strict = _tri_masks(d)
    decay = jnp.where(incl, jnp.exp(jnp.where(incl, gcol - grow, 0.0)), 0.0)
    kb = k * bcol
    kk = _dot_nt(_bf(kb), _bf(k))
    tinv = _inv_unit_tri(jnp.where(strict, kk * decay, 0.0))
    eg = jnp.exp(gcol)
    u = _dot(tinv, v * bcol, HI)
    w = _dot(tinv, kb * eg, HI)
    qk = _dot_nt(_bf(q), _bf(k))
    intra = jnp.where(incl, qk * decay, 0.0)
    glast = jnp.where(d == 0, gcol[GDN_C - 1:GDN_C, :], gcol[0:1, :])
    ek = jnp.exp(glast - gcol)
    return dict(incl=incl, strict=strict, decay=decay, kb=kb, kk=kk, tinv=tinv, eg=eg, u=u, w=w, qk=qk,
                intra=intra, glast=glast, ek=ek, q_dec=q * eg, k_dec=k * ek)


def _gdn_specs(S, nblk, order):
    R = GDN_GC * GDN_C

    def qkv_spec(off):
        return BS((R, GDN_DK), lambda d, h, n: (order(d, n), off + h))

    gb_spec = BS((R, LANES), lambda d, h, n: (order(d, n), 0))
    grow_spec = BS((1, GDN_GC, GDN_C), lambda d, h, n: (d * GDN_H + h, order(d, n), 0))
    st_spec = BS((1, 1, GDN_GC, GDN_DK, GDN_DK), lambda d, h, n: (d, h, order(d, n), 0, 0))
    return qkv_spec, gb_spec, grow_spec, st_spec


def gdn_scan_fwd(qkv, gb, grow, name):
    S = qkv.shape[0]
    R = GDN_GC * GDN_C
    nblk = S // R
    nc = S // GDN_C

    def order(d, n):
        return n + d * (nblk - 1 - 2 * n)

    qkv_spec, gb_spec, grow_spec, st_spec = _gdn_specs(S, nblk, order)

    def body(q_ref, k_ref, v_ref, gb_ref, grow_ref, o_ref, st_ref, s_scr):
        d = pl.program_id(0)
        h = pl.program_id(1)
        n = pl.program_id(2)

        @pl.when(n == 0)
        def _():
            s_scr[...] = jnp.zeros_like(s_scr)

        def chunk(cc, carry):
            ci = cc + d * (GDN_GC - 1 - 2 * cc)
            r0 = pl.multiple_of(ci * GDN_C, GDN_C)
            rows = pl.ds(r0, GDN_C)
            gbv = gb_ref[rows, :]
            gcol = _lane_col(gbv, d * GDN_H + h)
            bcol = _lane_col(gbv, 2 * GDN_H + d * GDN_H + h)
            grow_v = grow_ref[0, pl.ds(ci, 1), :]
            t = _chunk_terms(q_ref[rows, :], k_ref[rows, :], v_ref[rows, :], gcol, grow_v, bcol, d)
            st = s_scr[...]
            st_ref[0, 0, ci] = st
            sb = _bf(st)
            v_new = t["u"] - _dot(_bf(t["w"]), sb)
            o_ref[0, rows, :] = _dot(_bf(t["q_dec"]), sb) + _dot(_bf(t["intra"]), _bf(v_new))
            s_scr[...] = st * jnp.exp(t["glast"]) + _dot_tn(_bf(t["k_dec"]), _bf(v_new))
            return carry

        lax.fori_loop(0, GDN_GC, chunk, 0)

    return pl.pallas_call(
        body, grid=(2, GDN_H, nblk),
        in_specs=[qkv_spec(0), qkv_spec(GDN_H), qkv_spec(2 * GDN_H), gb_spec, grow_spec],
        out_specs=[BS((1, R, GDN_DK), lambda d, h, n: (d, order(d, n), h)), st_spec],
        out_shape=[SDS((2, S, GDN_H * GDN_DK), F32), SDS((2, GDN_H, nc, GDN_DK, GDN_DK), F32)],
        scratch_shapes=[pltpu.VMEM((GDN_DK, GDN_DK), F32)],
        name=name, compiler_params=_cp(("parallel", "parallel", "arbitrary")))(qkv, qkv, qkv, gb, grow)


def gdn_scan_bwd(qkv, gb, grow, states, do, name):
    S = qkv.shape[0]
    R = GDN_GC * GDN_C
    nblk = S // R

    def order(d, n):
        return (nblk - 1 - n) - d * (nblk - 1 - 2 * n)

    qkv_spec, gb_spec, grow_spec, st_spec = _gdn_specs(S, nblk, order)

    def body(q_ref, k_ref, v_ref, gb_ref, grow_ref, st_ref, do_ref, dqkv_ref, dgate_ref, ds_scr):
        d = pl.program_id(0)
        h = pl.program_id(1)
        n = pl.program_id(2)

        @pl.when(n == 0)
        def _():
            ds_scr[...] = jnp.zeros_like(ds_scr)

        def chunk(cc, carry):
            ci = (GDN_GC - 1 - cc) - d * (GDN_GC - 1 - 2 * cc)
            r0 = pl.multiple_of(ci * GDN_C, GDN_C)
            rows = pl.ds(r0, GDN_C)
            gbv = gb_ref[rows, :]
            gcol = _lane_col(gbv, d * GDN_H + h)
            bcol = _lane_col(gbv, 2 * GDN_H + d * GDN_H + h)
            grow_v = grow_ref[0, pl.ds(ci, 1), :]
            q, k, v = q_ref[rows, :], k_ref[rows, :], v_ref[rows, :]
            t = _chunk_terms(q, k, v, gcol, grow_v, bcol, d)
            incl, strict, decay, kb, eg, ek = t["incl"], t["strict"], t["decay"], t["kb"], t["eg"], t["ek"]
            u, w, tinv = t["u"], t["w"], t["tinv"]
            st = st_ref[0, 0, ci]
            dsn = ds_scr[...]
            sb, dsb = _bf(st), _bf(dsn)
            kbf, qbf = _bf(k), _bf(q)
            v_new = u - _dot(_bf(w), sb)
            dob = _bf(do_ref[rows, :])
            d_vnew = _dot_tn(_bf(t["intra"]), dob) + _dot(_bf(t["k_dec"]), dsb)
            d_intra = jnp.where(incl, _dot_nt(dob, _bf(v_new)), 0.0)
            d_qdec = _dot_nt(dob, sb)
            d_kdec = _dot_nt(_bf(v_new), dsb)
            egl = jnp.exp(t["glast"])
            dvb = _bf(d_vnew)
            ds_scr[...] = _dot_tn(_bf(t["q_dec"]), dob) + egl * dsn - _dot_tn(_bf(w), dvb)
            dgl = egl * jnp.sum(jnp.sum(st * dsn, axis=1, keepdims=True), axis=0, keepdims=True)
            dw = -_dot_nt(dvb, sb)
            d_ru = _dot_tn(tinv, d_vnew, HI)
            d_rw = _dot_tn(tinv, dw, HI)
            da = -jnp.where(strict, _dot_nt(_bf(d_ru), _bf(u)) + _dot_nt(_bf(d_rw), _bf(w)), 0.0)
            dv = d_ru * bcol
            dbeta = jnp.sum(d_ru * v, axis=1, keepdims=True)
            dkb = d_rw * eg
            dg = jnp.sum(d_rw * kb, axis=1, keepdims=True) * eg
            dkk = _bf(da * decay)
            dqk = _bf(d_intra * decay)
            dkb = dkb + _dot(dkk, kbf)
            dk = _dot_tn(dkk, _bf(kb)) + _dot_tn(dqk, qbf)
            dq = _dot(dqk, kbf) + d_qdec * eg
            dd = (da * t["kk"] + d_intra * t["qk"]) * decay
            dg = dg + jnp.sum(dd, axis=1, keepdims=True) - jnp.sum(dd.T, axis=1, keepdims=True)
            dg = dg + jnp.sum(d_qdec * t["q_dec"], axis=1, keepdims=True)
            dk = dk + d_kdec * ek
            ee = jnp.sum(d_kdec * t["k_dec"], axis=1, keepdims=True)
            dg = dg - ee
            dgl = dgl + jnp.sum(ee, axis=0, keepdims=True)
            dk = dk + dkb * bcol
            dbeta = dbeta + jnp.sum(dkb * k, axis=1, keepdims=True)
            ridx = lax.broadcasted_iota(jnp.int32, (GDN_C, 1), 0)
            dg = dg + jnp.where(ridx == (GDN_C - 1) * (1 - d), dgl, 0.0)
            dqkv_ref[0, 0, rows, :] = dq
            dqkv_ref[0, 1, rows, :] = dk
            dqkv_ref[0, 2, rows, :] = dv
            lane2 = lax.broadcasted_iota(jnp.int32, (GDN_C, 2), 1)
            dgate_ref[0, 0, rows, :] = jnp.where(lane2 == 0, dg, dbeta)
            return carry

        lax.fori_loop(0, GDN_GC, chunk, 0)

    return pl.pallas_call(
        body, grid=(2, GDN_H, nblk),
        in_specs=[qkv_spec(0), qkv_spec(GDN_H), qkv_spec(2 * GDN_H), gb_spec, grow_spec, st_spec,
                  BS((R, GDN_DK), lambda d, h, n: (order(d, n), h))],
        out_specs=[BS((1, 3, R, GDN_DK), lambda d, h, n: (d, 0, order(d, n), h)),
                   BS((1, 1, R, 2), lambda d, h, n: (d, h, order(d, n), 0))],
        out_shape=[SDS((2, 3, S, GDN_H * GDN_DK), F32), SDS((2, GDN_H, S, 2), F32)],
        scratch_shapes=[pltpu.VMEM((GDN_DK, GDN_DK), F32)],
        name=name, compiler_params=_cp(("parallel", "parallel", "arbitrary")))(qkv, qkv, qkv, gb, grow, states, do)


def gdn_post_fwd(o2, proj, nw, name):
    S = proj.shape[0]
    tm = min(512, S)
    zoff = GDN_QKV // LANES

    def body(o_ref, z_ref, nw_ref, y_ref):
        o = o_ref[0] + o_ref[1]
        z = z_ref[...]
        r = lax.rsqrt(jnp.mean(o * o, axis=-1, keepdims=True) + RMS_EPS)
        y_ref[...] = (o * r * nw_ref[...] * (z * _sigmoid(z))).astype(BF16)

    return pl.pallas_call(
        body, grid=(S // tm, GDN_H),
        in_specs=[BS((2, tm, LANES), lambda i, h: (0, i, h)), BS((tm, LANES), lambda i, h: (i, zoff + h)),
                  BS((1, LANES), lambda i, h: (0, 0))],
        out_specs=BS((tm, LANES), lambda i, h: (i, h)),
        out_shape=SDS((S, GDN_H * GDN_DK), BF16), name=name, compiler_params=_cp(("parallel", "parallel")))(o2, proj, nw)


def gdn_post_bwd(o2, proj, nw, dy, name):
    S = proj.shape[0]
    tm = min(512, S)
    zoff = GDN_QKV // LANES

    def body(o_ref, z_ref, nw_ref, dy_ref, do_ref, dz_ref, dnw_ref):
        first = (pl.program_id(0) == 0) & (pl.program_id(1) == 0)
        o = o_ref[0] + o_ref[1]
        z = z_ref[...]
        nwv = nw_ref[...]
        dyv = dy_ref[...]
        r = lax.rsqrt(jnp.mean(o * o, axis=-1, keepdims=True) + RMS_EPS)
        n = o * r
        sg = _sigmoid(z)
        sz = z * sg
        dz_ref[...] = (dyv * n * nwv * (sg * (1.0 + z * (1.0 - sg)))).astype(BF16)
        dn = dyv * nwv * sz
        do_ref[...] = r * (dn - n * jnp.mean(dn * n, axis=-1, keepdims=True))
        part = jnp.sum(dyv * n * sz, axis=0, keepdims=True)

        @pl.when(first)
        def _():
            dnw_ref[...] = part

        @pl.when(jnp.logical_not(first))
        def _():
            dnw_ref[...] += part

    blk = BS((tm, LANES), lambda i, h: (i, h))
    return pl.pallas_call(
        body, grid=(S // tm, GDN_H),
        in_specs=[BS((2, tm, LANES), lambda i, h: (0, i, h)), BS((tm, LANES), lambda i, h: (i, zoff + h)),
                  BS((1, LANES), lambda i, h: (0, 0)), blk],
        out_specs=[blk, blk, BS((1, LANES), lambda i, h: (0, 0))],
        out_shape=[SDS((S, GDN_H * GDN_DK), F32), SDS((S, GDN_H * GDN_DK), BF16), SDS((1, LANES), F32)],
        name=name, compiler_params=_cp(("arbitrary", "arbitrary")))(o2, proj, nw, dy)


def _gate_prm(a_log, dt_bias):
    z = jnp.zeros((8, LANES), F32)
    z = z.at[0, :2 * GDN_H].set(a_log.reshape(-1))
    return z.at[1, :2 * GDN_H].set(dt_bias.reshape(-1))


def gdn_fwd(x, g, w_all, convw, a_log, dt_bias, nw, w_out, tag):
    S = x.shape[0]
    h = rms_fwd(x, g, f"{tag}_rms")
    proj = mm(h, w_all, name=f"{tag}_proj", tn=1408)
    qkv = gdn_pre_fwd(proj, convw, f"{tag}_pre")
    prm = _gate_prm(a_log, dt_bias)
    gb = gdn_gate_fwd(proj, prm, f"{tag}_gate")
    grow = gb[:, :2 * GDN_H].T.reshape(2 * GDN_H, S // GDN_C, GDN_C)
    o2, states = gdn_scan_fwd(qkv, gb, grow, f"{tag}_scan")
    y = gdn_post_fwd(o2, proj, nw, f"{tag}_post")
    xn = mm(y, w_out, name=f"{tag}_out", epi=_add, extras=(x,))
    return xn, (h, proj, qkv, prm, gb, grow, o2, states, y)


def gdn_bwd(x, g, w_all, convw, nw, w_out, saved, dx, dxb, tag):
    S = x.shape[0]
    h, proj, qkv, prm, gb, grow, o2, states, y = saved
    dw_out = mm(y, dxb, name=f"{tag}_dwout", ta=True)
    dy = mm(dxb, w_out, name=f"{tag}_dy", tb=True)
    do, dz, dnw = gdn_post_bwd(o2, proj, nw, dy, f"{tag}_postb")
    dqkv, dgate = gdn_scan_bwd(qkv, gb, grow, states, do, f"{tag}_scanb")
    dgb = jnp.transpose(dgate, (2, 3, 0, 1)).reshape(S, 4 * GDN_H)
    dgb = jnp.pad(dgb, ((0, 0), (0, LANES - 4 * GDN_H)))
    dab, dprm = gdn_gate_bwd(proj, prm, dgb, f"{tag}_gateb")
    dpq, dconvw = gdn_pre_bwd(proj, convw, dqkv, f"{tag}_preb")
    dproj = jnp.concatenate([dpq, dz, dab], axis=1)
    dw_all = mm(h, dproj, name=f"{tag}_dwin", ta=True, tn=384)
    dh = mm(dproj, w_all, name=f"{tag}_dh", tb=True, tk=1408)
    dx, dxb, dg = rms_bwd(x, g, dh, dx, f"{tag}_rmsb")
    da_log = dprm[0, :2 * GDN_H].reshape(2, GDN_H)
    ddt = dprm[1, :2 * GDN_H].reshape(2, GDN_H)
    return dx, dxb, dg, dw_all, dconvw, da_log, ddt, dnw, dw_out


def _rel_bucket_np(rel):
    nb = REL_BUCKETS // 2
    max_exact = nb // 2
    ret = np.where(rel > 0, nb, 0)
    n = np.abs(rel)
    nf = np.maximum(n, 1).astype(np.float32)
    large = max_exact + (np.log(nf / max_exact) / np.float32(math.log(REL_MAX_DIST / max_exact))
                         * (nb - max_exact)).astype(np.int32)
    large = np.minimum(large, nb - 1)
    return ret + np.where(n < max_exact, n, large)


def _dswa_qb(L):
    return min(256, L)


def _bias_mats(rel_table, gi, L):
    _, dil = DSWA_CFG[gi]
    qb = _dswa_qb(L)
    offs = np.arange(-DSWA_HALF, DSWA_HALF + 1)
    f = jnp.take(rel_table, jnp.asarray(_rel_bucket_np(offs * dil)), axis=0)[:, gi * DSWA_HG:(gi + 1) * DSWA_HG].T
    i = np.arange(qb)[:, None]
    jj = np.arange(qb + 2 * DSWA_HALF)[None, :]
    idx_q = np.clip(jj - i, 0, 2 * DSWA_HALF)
    idx_k = np.clip(i.T - jj.T + 2 * DSWA_HALF, 0, 2 * DSWA_HALF)
    return f[:, idx_q].astype(F32), f[:, idx_k].astype(F32)


def _win_specs(qb, L, width, major):
    m = qb // DSWA_HALF
    last = L // DSWA_HALF - 1

    def prev(*ids):
        s, b = major(*ids)
        return (s, jnp.maximum(b * m - 1, 0), 0)

    def cur(*ids):
        s, b = major(*ids)
        return (s, b, 0)

    def nxt(*ids):
        s, b = major(*ids)
        return (s, jnp.minimum((b + 1) * m, last), 0)

    return [BS((1, DSWA_HALF, width), prev), BS((1, qb, width), cur), BS((1, DSWA_HALF, width), nxt)]


def _window(p_ref, c_ref, n_ref):
    return jnp.concatenate([p_ref[0], c_ref[0], n_ref[0]], axis=0)


def _band_valid(qb, b, L, transposed):
    shape = (qb + 2 * DSWA_HALF, qb) if transposed else (qb, qb + 2 * DSWA_HALF)
    blk = lax.broadcasted_iota(jnp.int32, shape, 1 if transposed else 0)
    win = lax.broadcasted_iota(jnp.int32, shape, 0 if transposed else 1)
    off = win - DSWA_HALF - blk
    pos = b * qb - DSWA_HALF + win
    return (jnp.abs(off) <= DSWA_HALF) & (pos >= 0) & (pos < L)


def attn_fwd(q, k, v, bias, dil, name):
    NS, L, E = q.shape
    qb = _dswa_qb(L)

    def major(s, b):
        return s, b

    win = _win_specs(qb, L, E, major)

    def body(q_ref, kp, kc, kn, vp, vc, vn, b_ref, o_ref, lse_ref):
        b = pl.program_id(1)
        kw = _window(kp, kc, kn)
        vw = _window(vp, vc, vn)
        s = _dot_nt(q_ref[0], kw) * (E ** -0.5) + b_ref[0]
        s = jnp.where(_band_valid(qb, b, L, False), s, NEG_INF)
        m = jnp.max(s, axis=-1, keepdims=True)
        p = jnp.exp(s - m)
        l = jnp.sum(p, axis=-1, keepdims=True)
        lse_ref[0] = m + jnp.log(l)
        o_ref[0] = _dot(_bf(p / l), vw)

    return pl.pallas_call(
        body, grid=(NS, L // qb),
        in_specs=[win[1]] + win + win + [BS((1, qb, qb + 2 * DSWA_HALF), lambda s, b: (s // dil, 0, 0))],
        out_specs=[BS((1, qb, E), lambda s, b: (s, b, 0)), BS((1, qb, 1), lambda s, b: (s, b, 0))],
        out_shape=[SDS((NS, L, E), F32), SDS((NS, L, 1), F32)],
        name=name, compiler_params=_cp(("parallel", "parallel")))(q, k, k, k, v, v, v, bias)


def attn_bwd_q(q, k, v, bias, lse, do, dd, dil, name):
    NS, L, E = q.shape
    qb = _dswa_qb(L)
    nb = L // qb

    def major(h, r, b):
        return h * dil + r, b

    win = _win_specs(qb, L, E, major)
    col = BS((1, qb, 1), lambda h, r, b: (h * dil + r, b, 0))

    def body(q_ref, kp, kc, kn, vp, vc, vn, b_ref, lse_ref, do_ref, dd_ref, dq_ref, db_ref):
        b = pl.program_id(2)
        first = (pl.program_id(1) == 0) & (b == 0)
        kw = _window(kp, kc, kn)
        vw = _window(vp, vc, vn)
        valid = _band_valid(qb, b, L, False)
        s = _dot_nt(q_ref[0], kw) * (E ** -0.5) + b_ref[0]
        p = jnp.exp(jnp.where(valid, s - lse_ref[0], NEG_INF))
        ds = p * (_dot_nt(do_ref[0], vw) - dd_ref[0])
        dq_ref[0] = (_dot(_bf(ds), kw) * (E ** -0.5)).astype(BF16)

        @pl.when(first)
        def _():
            db_ref[0] = ds

        @pl.when(jnp.logical_not(first))
        def _():
            db_ref[0] += ds

    bspec = BS((1, qb, qb + 2 * DSWA_HALF), lambda h, r, b: (h, 0, 0))
    return pl.pallas_call(
        body, grid=(DSWA_HG, dil, nb),
        in_specs=[win[1]] + win + win + [bspec, col, win[1], col],
        out_specs=[win[1], bspec],
        out_shape=[SDS((NS, L, E), BF16), SDS((DSWA_HG, qb, qb + 2 * DSWA_HALF), F32)],
        name=name, compiler_params=_cp(("parallel", "arbitrary", "arbitrary")))(q, k, k, k, v, v, v, bias, lse, do, dd)


def attn_bwd_kv(q, k, v, bias_t, lse, do, dd, dil, name):
    NS, L, E = q.shape
    qb = _dswa_qb(L)

    def major(s, b):
        return s, b

    win = _win_specs(qb, L, E, major)
    wcol = _win_specs(qb, L, 1, major)

    def body(kc, vc, qp, qc, qn, dop, doc, don, lp, lc, ln, ddp, ddc, ddn, b_ref, dk_ref, dv_ref):
        b = pl.program_id(1)
        qw = _window(qp, qc, qn)
        dow = _window(dop, doc, don)
        lw = _window(lp, lc, ln)
        ddw = _window(ddp, ddc, ddn)
        valid = _band_valid(qb, b, L, True)
        s = _dot_nt(qw, kc[0]) * (E ** -0.5) + b_ref[0]
        p = jnp.exp(jnp.where(valid, s - lw, NEG_INF))
        dv_ref[0] = _dot_tn(_bf(p), dow).astype(BF16)
        ds = jnp.where(valid, p * (_dot_nt(dow, vc[0]) - ddw), 0.0)
        dk_ref[0] = (_dot_tn(_bf(ds), qw) * (E ** -0.5)).astype(BF16)

    return pl.pallas_call(
        body, grid=(NS, L // qb),
        in_specs=[win[1], win[1]] + win + win + wcol + wcol + [BS((1, qb + 2 * DSWA_HALF, qb), lambda s, b: (s // dil, 0, 0))],
        out_specs=[win[1], win[1]],
        out_shape=[SDS((NS, L, E), BF16), SDS((NS, L, E), BF16)],
        name=name, compiler_params=_cp(("parallel", "parallel")))(k, v, q, q, q, do, do, do, lse, lse, lse, dd, dd, dd, bias_t)


def _head_expand():
    i = lax.broadcasted_iota(jnp.int32, (LANES, DSWA_HG * DSWA_E), 0)
    j = lax.broadcasted_iota(jnp.int32, (LANES, DSWA_HG * DSWA_E), 1)
    return jnp.where(i == j // DSWA_E, 1.0, 0.0).astype(F32)


def _group_alphas(lse3):
    m = jnp.maximum(jnp.maximum(lse3[0], lse3[1]), lse3[2])
    e = [jnp.exp(t - m) for t in lse3]
    tot = e[0] + e[1] + e[2]
    return [t / tot for t in e]


def combine_fwd(o_raw, lse3, name):
    S = o_raw.shape[0]
    tm = min(512, S)
    gw = DSWA_HG * DSWA_E

    def body(o_ref, l_ref, y_ref):
        alphas = _group_alphas([l_ref[0], l_ref[1], l_ref[2]])
        ex = _head_expand()
        for gi in range(3):
            cols = slice(gi * gw, (gi + 1) * gw)
            y_ref[:, cols] = (o_ref[:, cols] * _dot(alphas[gi], ex, HI)).astype(BF16)

    return pl.pallas_call(
        body, grid=(S // tm,),
        in_specs=[BS((tm, DSWA_W), lambda i: (i, 0)), BS((3, tm, LANES), lambda i: (0, i, 0))],
        out_specs=BS((tm, DSWA_W), lambda i: (i, 0)),
        out_shape=SDS((S, DSWA_W), BF16), name=name, compiler_params=_cp(("parallel",)))(o_raw, lse3)


def combine_bwd(o_raw, lse3, dy, name):
    S = o_raw.shape[0]
    tm = min(512, S)
    gw = DSWA_HG * DSWA_E

    def body(o_ref, l_ref, dy_ref, do_ref, dd_ref):
        alphas = _group_alphas([l_ref[0], l_ref[1], l_ref[2]])
        ex = _head_expand()
        dal = []
        for gi in range(3):
            cols = slice(gi * gw, (gi + 1) * gw)
            dyv = dy_ref[:, cols]
            do_ref[:, cols] = (dyv * _dot(alphas[gi], ex, HI)).astype(BF16)
            dal.append(_dot_nt(o_ref[:, cols] * dyv, ex, HI))
        c = alphas[0] * dal[0] + alphas[1] * dal[1] + alphas[2] * dal[2]
        for gi in range(3):
            dd_ref[gi] = alphas[gi] * c

    return pl.pallas_call(
        body, grid=(S // tm,),
        in_specs=[BS((tm, DSWA_W), lambda i: (i, 0)), BS((3, tm, LANES), lambda i: (0, i, 0)), BS((tm, DSWA_W), lambda i: (i, 0))],
        out_specs=[BS((tm, DSWA_W), lambda i: (i, 0)), BS((3, tm, LANES), lambda i: (0, i, 0))],
        out_shape=[SDS((S, DSWA_W), BF16), SDS((3, S, LANES), F32)],
        name=name, compiler_params=_cp(("parallel",)))(o_raw, lse3, dy)


def _to_sub(t, dil):
    S, hg, wd = t.shape
    return jnp.transpose(t.reshape(S // dil, dil, hg, wd), (2, 1, 0, 3)).reshape(hg * dil, S // dil, wd)


def _from_sub(t, dil):
    ns, L, wd = t.shape
    hg = ns // dil
    return jnp.transpose(t.reshape(hg, dil, L, wd), (2, 1, 0, 3)).reshape(L * dil, hg, wd)


def dswa_fwd(x, g, w_in, w_out, rel_table, tag):
    S = x.shape[0]
    h = rms_fwd(x, g, f"{tag}_rms")
    qkv = mm(h, w_in, name=f"{tag}_qkv", tn=1152, out_dtype=BF16).reshape(S, 3, DSWA_HEADS, DSWA_E)
    subs, outs, lses = [], [], []
    for gi, (_, dil) in enumerate(DSWA_CFG):
        hs = slice(gi * DSWA_HG, (gi + 1) * DSWA_HG)
        qs, ks, vs = (_to_sub(qkv[:, i, hs], dil) for i in range(3))
        bias, bias_t = _bias_mats(rel_table, gi, S // dil)
        o, lse = attn_fwd(qs, ks, vs, bias, dil, f"{tag}_att{gi}")
        subs.append((qs, ks, vs, lse))
        outs.append(_from_sub(o, dil).reshape(S, DSWA_HG * DSWA_E))
        lses.append(jnp.pad(_from_sub(lse, dil).reshape(S, DSWA_HG), ((0, 0), (0, LANES - DSWA_HG))))
    o_raw = jnp.concatenate(outs, axis=1)
    lse3 = jnp.stack(lses)
    y = combine_fwd(o_raw, lse3, f"{tag}_comb")
    xn = mm(y, w_out, name=f"{tag}_out", epi=_add, extras=(x,))
    return xn, (h, subs, o_raw, lse3, y)


def dswa_bwd(x, g, w_in, w_out, rel_table, saved, dx, dxb, tag):
    S = x.shape[0]
    h, subs, o_raw, lse3, y = saved
    dw_out = mm(y, dxb, name=f"{tag}_dwout", ta=True, tm=384)
    dy = mm(dxb, w_out, name=f"{tag}_dy", tb=True, tn=384)
    do_raw, dd3 = combine_bwd(o_raw, lse3, dy, f"{tag}_combb")
    do_raw = do_raw.reshape(S, DSWA_HEADS, DSWA_E)
    dqkv = []
    drel = jnp.zeros_like(rel_table)
    for gi, (_, dil) in enumerate(DSWA_CFG):
        hs = slice(gi * DSWA_HG, (gi + 1) * DSWA_HG)
        qs, ks, vs, lse = subs[gi]
        dos = _to_sub(do_raw[:, hs], dil)
        dds = _to_sub(dd3[gi, :, :DSWA_HG, None], dil)
        (bias, bias_t), bias_vjp = jax.vjp(lambda tbl: _bias_mats(tbl, gi, S // dil), rel_table)
        dq, dbias = attn_bwd_q(qs, ks, vs, bias, lse, dos, dds, dil, f"{tag}_attq{gi}")
        dk, dv = attn_bwd_kv(qs, ks, vs, bias_t, lse, dos, dds, dil, f"{tag}_attkv{gi}")
        drel = drel + bias_vjp((dbias, jnp.zeros_like(bias_t)))[0]
        dqkv.append([_from_sub(t, dil) for t in (dq, dk, dv)])
    dqkv = jnp.stack([jnp.concatenate([dqkv[gi][i] for gi in range(3)], axis=1) for i in range(3)], axis=1)
    dqkv = dqkv.reshape(S, 3 * DSWA_W)
    dw_in = mm(h, dqkv, name=f"{tag}_dwin", ta=True, tn=384)
    dh = mm(dqkv, w_in, name=f"{tag}_dh", tb=True, tk=1152)
    dx, dxb, dg = rms_bwd(x, g, dh, dx, f"{tag}_rmsb")
    return dx, dxb, dg, dw_in, dw_out, drel


def adamw(w, g, m, v, name):
    shape = w.shape
    last = shape[-1]
    w2, g2, m2, v2 = (t.reshape(-1, last) for t in (w, g, m, v))
    rows = w2.shape[0]
    tr = rows
    if rows > 512:
        tr = next(t for t in (512, 256, 192, 128, 64, 8) if rows % t == 0)
    c1 = 1.0 / (1.0 - ADAM_B1 ** ADAM_STEP)
    c2 = 1.0 / (1.0 - ADAM_B2 ** ADAM_STEP)

    def body(w_ref, g_ref, m_ref, v_ref, d_ref, nm_ref, nv_ref):
        gv = g_ref[...]
        nm = ADAM_B1 * m_ref[...] + (1.0 - ADAM_B1) * gv
        nv = ADAM_B2 * v_ref[...] + (1.0 - ADAM_B2) * (gv * gv)
        nm_ref[...] = nm
        nv_ref[...] = nv
        d_ref[...] = -ADAM_LR * ((nm * c1) / (jnp.sqrt(nv * c2) + ADAM_EPS) + ADAM_WD * w_ref[...])

    spec = BS((tr, last), lambda i: (i, 0))
    outs = pl.pallas_call(
        body, grid=(rows // tr,), in_specs=[spec] * 4, out_specs=[spec] * 3,
        out_shape=[SDS((rows, last), F32)] * 3, name=name, compiler_params=_cp(("parallel",)))(w2, g2, m2, v2)
    return tuple(o.reshape(shape) for o in outs)


ANY = BS(memory_space=pl.ANY)
GATHER_CHUNKS = 5


def _place():
    x, y, c = lax.axis_index("x"), lax.axis_index("y"), lax.axis_index("c")
    chips = [(1 - x, y), (x, 1 - y), (1 - x, 1 - y)]
    return x, y, c, chips


def _rcopy(src, dst, ssem, rsem, dev):
    return pltpu.make_async_remote_copy(src_ref=src, dst_ref=dst, send_sem=ssem, recv_sem=rsem, device_id=dev,
                                        device_id_type=MESH)


def gather_weights(pack, convp):
    RP = pack.shape[0]
    half = RP // 2
    nch = GATHER_CHUNKS
    ch = half // nch
    assert ch * nch == half and ch % 16 == 0

    def body(p_ref, c_ref, full_ref, cfull_ref, ssem, rsem, fssem, frsem, cssem, crsem, lsem):
        x, y, c, chips = _place()
        jme = 2 * x + y
        sib = (x, y, 1 - c)

        def rows(hc, q):
            return pl.ds(hc * half + q * ch, ch)

        own = pltpu.make_async_copy(p_ref, full_ref.at[jme], lsem.at[0])
        own_c = pltpu.make_async_copy(c_ref, cfull_ref.at[jme], lsem.at[1])
        own.start()
        own_c.start()
        sends = []
        for r, (cx, cy) in enumerate(chips):
            for q in range(nch):
                cp = _rcopy(p_ref.at[rows(c, q)], full_ref.at[jme, rows(c, q)], ssem.at[r, q], rsem.at[r, q], (cx, cy, c))
                cp.start()
                sends.append(cp)
            cp = _rcopy(c_ref, cfull_ref.at[jme], cssem.at[r], crsem.at[r], (cx, cy, c))
            cp.start()
            sends.append(cp)
        for r, (cx, cy) in enumerate(chips):
            jr = 2 * cx + cy
            for q in range(nch):
                piece = full_ref.at[jr, rows(c, q)]
                _rcopy(piece, piece, ssem.at[r, q], rsem.at[r, q], (cx, cy, c)).wait_recv()
                cp = _rcopy(piece, piece, fssem.at[r, q], frsem.at[r, q], sib)
                cp.start()
                sends.append(cp)
            _rcopy(c_ref, cfull_ref.at[jr], cssem.at[r], crsem.at[r], (cx, cy, c)).wait_recv()
        for r, (cx, cy) in enumerate(chips):
            jr = 2 * cx + cy
            for q in range(nch):
                piece = full_ref.at[jr, rows(1 - c, q)]
                _rcopy(piece, piece, fssem.at[r, q], frsem.at[r, q], sib).wait_recv()
        for cp in sends:
            cp.wait_send()
        own.wait()
        own_c.wait()

    dma = pltpu.SemaphoreType.DMA
    return pl.pallas_call(
        body, in_specs=[ANY, ANY], out_specs=[ANY, ANY],
        out_shape=[SDS((4,) + pack.shape, pack.dtype), SDS((4,) + convp.shape, convp.dtype)],
        scratch_shapes=[dma((3, nch)), dma((3, nch)), dma((3, nch)), dma((3, nch)), dma((3,)), dma((3,)), dma((2,))],
        name="gather_weights", compiler_params=pltpu.CompilerParams(has_side_effects=True))(pack, convp)


def rs_sibling_exchange(gpack):
    _, RP, W = gpack.shape
    half = RP // 2

    def body(g_ref, r_ref, ssem, rsem):
        x, y, c, _ = _place()
        cps = [_rcopy(g_ref.at[j, pl.ds((1 - c) * half, half)], r_ref.at[j], ssem.at[j], rsem.at[j], (x, y, 1 - c))
               for j in range(4)]
        for cp in cps:
            cp.start()
        for cp in cps:
            cp.wait()

    dma = pltpu.SemaphoreType.DMA
    return pl.pallas_call(
        body, in_specs=[ANY], out_specs=ANY, out_shape=SDS((4, half, W), gpack.dtype),
        scratch_shapes=[dma((4,)), dma((4,))], name="rs_sibling_exchange",
        compiler_params=pltpu.CompilerParams(has_side_effects=True))(gpack)


def _rs_tile(half):
    return next(t for t in (656, 512, 328, 256, 128, 64, 8) if half % t == 0)


def rs_add_sibling(gpack, recv, cidx):
    _, RP, W = gpack.shape
    half = RP // 2
    tr = _rs_tile(half)
    nb = half // tr

    def body(c_ref, g_ref, r_ref, o_ref):
        o_ref[...] = g_ref[...] + r_ref[...]

    gs = pltpu.PrefetchScalarGridSpec(
        num_scalar_prefetch=1, grid=(4, nb),
        in_specs=[BS((1, tr, W), lambda j, i, c: (j, c[0] * nb + i, 0)), BS((1, tr, W), lambda j, i, c: (j, i, 0))],
        out_specs=BS((1, tr, W), lambda j, i, c: (j, i, 0)))
    return pl.pallas_call(body, grid_spec=gs, out_shape=SDS((4, half, W), F32), name="rs_add_sibling",
                          compiler_params=_cp(("parallel", "parallel")))(cidx, gpack, recv)


def rs_chip_exchange(part):
    _, half, W = part.shape

    def body(p_ref, r_ref, ssem, rsem, lsem):
        x, y, c, chips = _place()
        own = pltpu.make_async_copy(p_ref.at[2 * x + y], r_ref.at[3], lsem)
        own.start()
        cps = [_rcopy(p_ref.at[2 * cx + cy], r_ref.at[r], ssem.at[r], rsem.at[r], (cx, cy, c))
               for r, (cx, cy) in enumerate(chips)]
        for cp in cps:
            cp.start()
        for cp in cps:
            cp.wait()
        own.wait()

    dma = pltpu.SemaphoreType.DMA
    return pl.pallas_call(
        body, in_specs=[ANY], out_specs=ANY, out_shape=SDS((4, half, W), part.dtype),
        scratch_shapes=[dma((3,)), dma((3,)), dma], name="rs_chip_exchange",
        compiler_params=pltpu.CompilerParams(has_side_effects=True))(part)


def rs_add_chips(recv):
    _, half, W = recv.shape
    tr = _rs_tile(half)

    def body(r_ref, o_ref):
        o_ref[...] = ((r_ref[0] + r_ref[1]) + r_ref[2]) + r_ref[3]

    return pl.pallas_call(
        body, grid=(half // tr,), in_specs=[BS((4, tr, W), lambda i: (0, i, 0))], out_specs=BS((tr, W), lambda i: (i, 0)),
        out_shape=SDS((half, W), F32), name="rs_add_chips", compiler_params=_cp(("parallel",)))(recv)


def rs_sibling_share(fin):
    half, W = fin.shape

    def body(f_ref, o_ref, ssem, rsem, lsem):
        x, y, c, _ = _place()
        mine = o_ref.at[pl.ds(c * half, half)]
        own = pltpu.make_async_copy(f_ref, mine, lsem)
        cp = _rcopy(f_ref, mine, ssem, rsem, (x, y, 1 - c))
        own.start()
        cp.start()
        cp.wait()
        own.wait()

    dma = pltpu.SemaphoreType.DMA
    return pl.pallas_call(
        body, in_specs=[ANY], out_specs=ANY, out_shape=SDS((2 * half, W), fin.dtype),
        scratch_shapes=[dma, dma, dma], name="rs_sibling_share",
        compiler_params=pltpu.CompilerParams(has_side_effects=True))(fin)


def allreduce_small(pack):
    R = pack.shape[0]

    def body(p_ref, o_ref, all_ref, ssem, rsem):
        x, y, c, _ = _place()
        me = 4 * x + 2 * y + c
        all_ref[me] = p_ref[...]
        cps = []
        for m in range(1, 8):
            peer = (1 - x if m & 4 else x, 1 - y if m & 2 else y, 1 - c if m & 1 else c)
            cp = _rcopy(p_ref, all_ref.at[me], ssem.at[m - 1], rsem.at[m - 1], peer)
            cp.start()
            cps.append(cp)
        for cp in cps:
            cp.wait()
        acc = all_ref[0]
        for i in range(1, 8):
            acc = acc + all_ref[i]
        o_ref[...] = acc

    dma = pltpu.SemaphoreType.DMA
    vm = BS(memory_space=pltpu.VMEM)
    return pl.pallas_call(
        body, in_specs=[vm], out_specs=vm, out_shape=SDS(pack.shape, F32),
        scratch_shapes=[pltpu.VMEM((8, R, LANES), F32), dma((7,)), dma((7,))], name="allreduce_small",
        compiler_params=pltpu.CompilerParams(has_side_effects=True))(pack)


PACK_W = 1024
PACK_ENTRIES = []
for _l in range(DEPTH):
    if _l % 2 == 0:
        PACK_ENTRIES += [("gdn_w_in", _l // 2, D_MODEL, GDN_IN // 4, True), ("gdn_w_out", _l // 2, D_MODEL // 4, D_MODEL, False)]
    else:
        PACK_ENTRIES += [("dswa_w_in", _l // 2, D_MODEL, 3 * DSWA_W // 4, True), ("dswa_w_out", _l // 2, DSWA_W // 4, D_MODEL, False)]
    PACK_ENTRIES += [("mlp_w1", _l, D_MODEL, D_FF // 4, True), ("mlp_w2", _l, D_FF // 4, D_MODEL, False)]
PACK_ROWS = [r * c // PACK_W for (_, _, r, c, _) in PACK_ENTRIES]
PACK_OFFS = [int(o) for o in np.cumsum([0] + PACK_ROWS)]
PACK_TOTAL = -(-PACK_OFFS[-1] // (32 * GATHER_CHUNKS)) * (32 * GATHER_CHUNKS)


def _pack_shards(shards, dtype):
    parts = [shards[name][li].astype(dtype).reshape(-1, PACK_W) for (name, li, _, _, _) in PACK_ENTRIES]
    parts.append(jnp.zeros((PACK_TOTAL - PACK_OFFS[-1], PACK_W), dtype))
    return jnp.concatenate(parts, axis=0)


def _unpack_full(full):
    mats = []
    for e, (name, li, r, c, by_col) in enumerate(PACK_ENTRIES):
        sh = full[:, PACK_OFFS[e]:PACK_OFFS[e + 1]].reshape(4, r, c)
        mats.append(jnp.concatenate([sh[j] for j in range(4)], axis=1 if by_col else 0))
    return mats


def _pack_grads(grads):
    per_chip = []
    for j in range(4):
        parts = []
        for e, (name, li, r, c, by_col) in enumerate(PACK_ENTRIES):
            g = grads[e]
            sh = g[:, c * j:c * (j + 1)] if by_col else g[r * j:r * (j + 1), :]
            parts.append(sh.reshape(-1, PACK_W))
        parts.append(jnp.zeros((PACK_TOTAL - PACK_OFFS[-1], PACK_W), F32))
        per_chip.append(jnp.concatenate(parts, axis=0))
    return jnp.stack(per_chip)


def _unpack_shard_grads(gsh):
    out = {}
    for e, (name, li, r, c, _) in enumerate(PACK_ENTRIES):
        out.setdefault(name, []).append(gsh[PACK_OFFS[e]:PACK_OFFS[e + 1]].reshape(r, c))
    return {k: jnp.stack(v) for k, v in out.items()}


def _flat_pad(t, mult=8 * LANES):
    f = t.reshape(-1)
    return jnp.pad(f, (0, (-f.shape[0]) % mult))


def kernel(x, norm_mix, norm_mlp, norm_final, rel_bias, gdn_w_in, gdn_conv_w, gdn_a_log, gdn_dt_bias, gdn_norm_w, gdn_w_out, dswa_w_in, dswa_w_out, mlp_w1, mlp_w2, loss_target, m_norm_mix, m_norm_mlp, m_norm_final, m_rel_bias, m_gdn_w_in, m_gdn_conv_w, m_gdn_a_log, m_gdn_dt_bias, m_gdn_norm_w, m_gdn_w_out, m_dswa_w_in, m_dswa_w_out, m_mlp_w1, m_mlp_w2, v_norm_mix, v_norm_mlp, v_norm_final, v_rel_bias, v_gdn_w_in, v_gdn_conv_w, v_gdn_a_log, v_gdn_dt_bias, v_gdn_norm_w, v_gdn_w_out, v_dswa_w_in, v_dswa_w_out, v_mlp_w1, v_mlp_w2):
    xi, yi, ci = lax.axis_index("x"), lax.axis_index("y"), lax.axis_index("c")
    jme = 2 * xi + yi
    big = dict(gdn_w_in=gdn_w_in, gdn_w_out=gdn_w_out, dswa_w_in=dswa_w_in, dswa_w_out=dswa_w_out, mlp_w1=mlp_w1, mlp_w2=mlp_w2)
    n_gdn = gdn_w_in.shape[0]
    conv_cols = gdn_conv_w.shape[-1]

    pack = _pack_shards(big, BF16)
    convp = jnp.pad(gdn_conv_w.reshape(n_gdn * GDN_CONV, conv_cols), ((0, 16 - n_gdn * GDN_CONV), (0, 0)))
    full, cfull = gather_weights(pack, convp)
    mats = _unpack_full(full)
    conv_all = jnp.transpose(cfull[:, :n_gdn * GDN_CONV], (1, 0, 2)).reshape(n_gdn, GDN_CONV, 4 * conv_cols)
    conv_all = jnp.pad(conv_all, ((0, 0), (0, 8 - GDN_CONV), (0, 0)))

    xs = x[0]
    saved = []
    for l in range(DEPTH):
        w_in, w_out, w1, w2 = mats[4 * l:4 * l + 4]
        gm, gp = norm_mix[l][None], norm_mlp[l][None]
        a = l // 2
        if l % 2 == 0:
            w_in = jnp.pad(w_in, ((0, 0), (0, GDN_INP - GDN_IN)))
            x_mid, sv = gdn_fwd(xs, gm, w_in, conv_all[a], gdn_a_log[a], gdn_dt_bias[a], gdn_norm_w[a][None], w_out, f"l{l}_gdn")
        else:
            x_mid, sv = dswa_fwd(xs, gm, w_in, w_out, rel_bias, f"l{l}_att")
        x_out, sv2 = mlp_fwd(x_mid, gp, w1, w2, f"l{l}_mlp")
        saved.append((xs, x_mid, w_in, sv, sv2))
        xs = x_out

    loss_part, dx, dxb, d_final = loss_head(xs, norm_final[None], loss_target[0], "loss_head")
    grads = [None] * len(PACK_ENTRIES)
    d_mix, d_mlp = [None] * DEPTH, [None] * DEPTH
    d_conv, d_alog, d_dt, d_nw = [None] * n_gdn, [None] * n_gdn, [None] * n_gdn, [None] * n_gdn
    d_rel = jnp.zeros_like(rel_bias)
    for l in reversed(range(DEPTH)):
        _, w_out, w1, w2 = mats[4 * l:4 * l + 4]
        x_in, x_mid, w_in, sv, sv2 = saved[l]
        gm, gp = norm_mix[l][None], norm_mlp[l][None]
        a = l // 2
        dx, dxb, d_mlp[l], grads[4 * l + 2], grads[4 * l + 3] = mlp_bwd(x_mid, gp, w1, w2, sv2, dx, dxb, f"l{l}_mlp")
        if l % 2 == 0:
            dx, dxb, d_mix[l], dw_all, d_conv[a], d_alog[a], d_dt[a], d_nw[a], grads[4 * l + 1] = gdn_bwd(
                x_in, gm, w_in, conv_all[a], gdn_norm_w[a][None], w_out, sv, dx, dxb, f"l{l}_gdn")
            grads[4 * l] = dw_all[:, :GDN_IN]
        else:
            dx, dxb, d_mix[l], grads[4 * l], grads[4 * l + 1], drel = dswa_bwd(x_in, gm, w_in, w_out, rel_bias, sv, dx, dxb, f"l{l}_att")
            d_rel = d_rel + drel

    cidx = ci.astype(jnp.int32).reshape(1)
    gpack = _pack_grads(grads)
    part = rs_add_sibling(gpack, rs_sibling_exchange(gpack), cidx)
    gsh = rs_sibling_share(rs_add_chips(rs_chip_exchange(part)))
    gbig = _unpack_shard_grads(gsh)

    small = [jnp.concatenate(d_mix, axis=0), jnp.concatenate(d_mlp, axis=0), d_final, d_rel,
             jnp.stack(d_conv), jnp.stack(d_alog), jnp.stack(d_dt), jnp.concatenate(d_nw, axis=0)]
    flat = [_flat_pad(t) for t in small]
    sizes = [f.shape[0] for f in flat]
    red = allreduce_small(jnp.concatenate(flat).reshape(-1, LANES)).reshape(-1)
    offs = np.cumsum([0] + sizes)
    red = [red[offs[i]:offs[i] + small[i].size].reshape(small[i].shape) for i in range(len(small))]
    g_conv_all = red[4][:, :GDN_CONV].reshape(n_gdn, GDN_CONV, 1, 4 * conv_cols)
    g_conv = lax.dynamic_slice_in_dim(g_conv_all, jme * conv_cols, conv_cols, axis=3)
    g = dict(norm_mix=red[0], norm_mlp=red[1], norm_final=red[2].reshape(norm_final.shape), rel_bias=red[3],
             gdn_conv_w=g_conv, gdn_a_log=red[5], gdn_dt_bias=red[6], gdn_norm_w=red[7][:, :GDN_DK], **gbig)

    w = dict(norm_mix=norm_mix, norm_mlp=norm_mlp, norm_final=norm_final, rel_bias=rel_bias, gdn_conv_w=gdn_conv_w,
             gdn_a_log=gdn_a_log, gdn_dt_bias=gdn_dt_bias, gdn_norm_w=gdn_norm_w, **big)
    m = dict(norm_mix=m_norm_mix, norm_mlp=m_norm_mlp, norm_final=m_norm_final, rel_bias=m_rel_bias, gdn_w_in=m_gdn_w_in,
             gdn_conv_w=m_gdn_conv_w, gdn_a_log=m_gdn_a_log, gdn_dt_bias=m_gdn_dt_bias, gdn_norm_w=m_gdn_norm_w,
             gdn_w_out=m_gdn_w_out, dswa_w_in=m_dswa_w_in, dswa_w_out=m_dswa_w_out, mlp_w1=m_mlp_w1, mlp_w2=m_mlp_w2)
    v = dict(norm_mix=v_norm_mix, norm_mlp=v_norm_mlp, norm_final=v_norm_final, rel_bias=v_rel_bias, gdn_w_in=v_gdn_w_in,
             gdn_conv_w=v_gdn_conv_w, gdn_a_log=v_gdn_a_log, gdn_dt_bias=v_gdn_dt_bias, gdn_norm_w=v_gdn_norm_w,
             gdn_w_out=v_gdn_w_out, dswa_w_in=v_dswa_w_in, dswa_w_out=v_dswa_w_out, mlp_w1=v_mlp_w1, mlp_w2=v_mlp_w2)
    names = ["norm_mix", "norm_mlp", "norm_final", "rel_bias", "gdn_w_in", "gdn_conv_w", "gdn_a_log", "gdn_dt_bias",
             "gdn_norm_w", "gdn_w_out", "dswa_w_in", "dswa_w_out", "mlp_w1", "mlp_w2"]
    upd = {n: adamw(w[n], g[n], m[n], v[n], f"adamw_{n}") for n in names}
    loss = lax.psum(loss_part[0, 0], ("x", "y", "c"))
    return (loss, dx[None], *[g[n] for n in names], *[upd[n][0] for n in names], *[upd[n][1] for n in names],
            *[upd[n][2] for n in names])
```

```python
import functools
import math

import numpy as np
import jax
import jax.numpy as jnp
from jax import lax
from jax.experimental import pallas as pl
from jax.experimental.pallas import tpu as pltpu

F32 = jnp.float32
BF16 = jnp.bfloat16
HI = lax.Precision.HIGHEST
BS = pl.BlockSpec
SDS = jax.ShapeDtypeStruct
MESH = pl.DeviceIdType.MESH

D_MODEL = 1024
D_FF = 4096
DEPTH = 4
RMS_EPS = 1e-6
NEG_INF = -1e30
LANES = 128
VMEM_LIMIT = 56 << 20

GDN_H = 8
GDN_DK = 128
GDN_CONV = 5
GDN_C = 64
GDN_GC = 8
GDN_QKV = 3 * GDN_H * GDN_DK
GDN_IN = GDN_QKV + GDN_H * GDN_DK + 4 * GDN_H
GDN_INP = 4224

DSWA_CFG = ((128, 1), (512, 4), (2048, 16))
DSWA_HG = 6
DSWA_E = 64
DSWA_HEADS = 18
DSWA_W = DSWA_HEADS * DSWA_E
DSWA_HALF = 64
REL_BUCKETS = 32
REL_MAX_DIST = 1024

ADAM_LR = 0.001
ADAM_B1 = 0.9
ADAM_B2 = 0.999
ADAM_EPS = 1e-08
ADAM_WD = 0.01
ADAM_STEP = 10


def _cp(sem=None):
    return pltpu.CompilerParams(dimension_semantics=sem, vmem_limit_bytes=VMEM_LIMIT)


def _dot(a, b, prec=None):
    return jnp.dot(a, b, precision=prec, preferred_element_type=F32)


def _dot_nt(a, b, prec=None):
    return lax.dot_general(a, b, (((1,), (1,)), ((), ())), precision=prec, preferred_element_type=F32)


def _dot_tn(a, b, prec=None):
    return lax.dot_general(a, b, (((0,), (0,)), ((), ())), precision=prec, preferred_element_type=F32)


def _bf(a):
    return a.astype(BF16)


def _sigmoid(x):
    return 1.0 / (1.0 + jnp.exp(-x))


def rms_fwd(x, g, name):
    S, Dm = x.shape
    tm = min(512, S)

    def body(x_ref, g_ref, o_ref):
        xv = x_ref[...]
        r = lax.rsqrt(jnp.mean(xv * xv, axis=-1, keepdims=True) + RMS_EPS)
        o_ref[...] = (xv * r * g_ref[...]).astype(o_ref.dtype)

    return pl.pallas_call(
        body, grid=(S // tm,),
        in_specs=[BS((tm, Dm), lambda i: (i, 0)), BS((1, Dm), lambda i: (0, 0))],
        out_specs=BS((tm, Dm), lambda i: (i, 0)),
        out_shape=SDS((S, Dm), BF16), name=name, compiler_params=_cp(("parallel",)))(x, g)


def rms_bwd(x, g, dh, dres, name):
    S, Dm = x.shape
    tm = min(512, S)

    def body(x_ref, g_ref, dh_ref, dres_ref, dx_ref, dxb_ref, dg_ref):
        i = pl.program_id(0)
        xv = x_ref[...]
        r = lax.rsqrt(jnp.mean(xv * xv, axis=-1, keepdims=True) + RMS_EPS)
        n = xv * r
        dhv = dh_ref[...]
        t = dhv * g_ref[...]
        dx = dres_ref[...] + r * (t - n * jnp.mean(n * t, axis=-1, keepdims=True))
        dx_ref[...] = dx
        dxb_ref[...] = dx.astype(BF16)
        part = jnp.sum(dhv * n, axis=0, keepdims=True)

        @pl.when(i == 0)
        def _():
            dg_ref[...] = part

        @pl.when(i > 0)
        def _():
            dg_ref[...] += part

    row = BS((tm, Dm), lambda i: (i, 0))
    vec = BS((1, Dm), lambda i: (0, 0))
    return pl.pallas_call(
        body, grid=(S // tm,), in_specs=[row, vec, row, row], out_specs=[row, row, vec],
        out_shape=[SDS((S, Dm), F32), SDS((S, Dm), BF16), SDS((1, Dm), F32)],
        name=name, compiler_params=_cp(("arbitrary",)))(x, g, dh, dres)


def loss_head(x, g, tgt, name):
    S, Dm = x.shape
    tm = min(512, S)

    def body(x_ref, g_ref, t_ref, loss_ref, dx_ref, dxb_ref, dg_ref):
        i = pl.program_id(0)
        xv = x_ref[...]
        gv = g_ref[...]
        r = lax.rsqrt(jnp.mean(xv * xv, axis=-1, keepdims=True) + RMS_EPS)
        n = xv * r
        err = n * gv - t_ref[...]
        lpart = 0.5 * jnp.sum(jnp.mean(err * err, axis=-1, keepdims=True), axis=0, keepdims=True)
        dout = err * (1.0 / Dm)
        t = dout * gv
        dx = r * (t - n * jnp.mean(n * t, axis=-1, keepdims=True))
        dx_ref[...] = dx
        dxb_ref[...] = dx.astype(BF16)
        part = jnp.sum(dout * n, axis=0, keepdims=True)

        @pl.when(i == 0)
        def _():
            dg_ref[...] = part
            loss_ref[...] = lpart

        @pl.when(i > 0)
        def _():
            dg_ref[...] += part
            loss_ref[...] += lpart

    row = BS((tm, Dm), lambda i: (i, 0))
    vec = BS((1, Dm), lambda i: (0, 0))
    one = BS((1, 1), lambda i: (0, 0))
    return pl.pallas_call(
        body, grid=(S // tm,), in_specs=[row, vec, row], out_specs=[one, row, row, vec],
        out_shape=[SDS((1, 1), F32), SDS((S, Dm), F32), SDS((S, Dm), BF16), SDS((1, Dm), F32)],
        name=name, compiler_params=_cp(("arbitrary",)))(x, g, tgt)


def mm(a, b, *, name, ta=False, tb=False, tm=512, tn=512, tk=None, out_dtype=F32, pre_a=None, epi=None,
       extras=()):
    M, K = (a.shape[1], a.shape[0]) if ta else a.shape
    N = b.shape[0] if tb else b.shape[1]
    tm, tn = min(tm, M), min(tn, N)
    tk = K if tk is None else min(tk, K)
    assert M % tm == 0 and N % tn == 0 and K % tk == 0, (name, M, N, K, tm, tn, tk)
    nk = K // tk
    ne = len(extras)
    a_spec = BS((tk, tm), lambda i, j, k: (k, i)) if ta else BS((tm, tk), lambda i, j, k: (i, k))
    b_spec = BS((tn, tk), lambda i, j, k: (j, k)) if tb else BS((tk, tn), lambda i, j, k: (k, j))
    o_spec = BS((tm, tn), lambda i, j, k: (i, j))
    dims = (((0 if ta else 1,), (1 if tb else 0,)), ((), ()))

    def body(a_ref, b_ref, *rest):
        e_refs, o_ref = rest[:ne], rest[ne]
        av = a_ref[...]
        if pre_a is not None:
            av = pre_a(av)
        p = lax.dot_general(_bf(av), _bf(b_ref[...]), dims, preferred_element_type=F32)

        def finish(acc):
            res = epi(acc, *[e[...] for e in e_refs]) if epi is not None else acc
            o_ref[...] = res.astype(o_ref.dtype)

        if nk == 1:
            finish(p)
        else:
            acc_ref = rest[ne + 1]
            k = pl.program_id(2)

            @pl.when(k == 0)
            def _():
                acc_ref[...] = p

            @pl.when(k > 0)
            def _():
                acc_ref[...] += p

            @pl.when(k == nk - 1)
            def _():
                finish(acc_ref[...])

    return pl.pallas_call(
        body, grid=(M // tm, N // tn, nk), in_specs=[a_spec, b_spec] + [o_spec] * ne, out_specs=o_spec,
        out_shape=SDS((M, N), out_dtype),
        scratch_shapes=[pltpu.VMEM((tm, tn), F32)] if nk > 1 else [],
        name=name, compiler_params=_cp(("parallel", "parallel", "arbitrary")))(a, b, *extras)


def _relu(acc):
    return jnp.maximum(acc, 0.0)


def _add(acc, res):
    return acc + res


def _sq(av):
    af = av.astype(F32)
    return af * af


def _times_2r(acc, r):
    return acc * (2.0 * r.astype(F32))


def mlp_fwd(x, g, w1, w2, tag):
    h = rms_fwd(x, g, f"{tag}_rms")
    r = mm(h, w1, name=f"{tag}_up", tn=1024, out_dtype=BF16, epi=_relu)
    xn = mm(r, w2, name=f"{tag}_down", pre_a=_sq, epi=_add, extras=(x,))
    return xn, (h, r)


def mlp_bwd(x, g, w1, w2, saved, dx, dxb, tag):
    h, r = saved
    da = mm(dxb, w2, name=f"{tag}_dact", tb=True, tn=1024, out_dtype=BF16, epi=_times_2r, extras=(r,))
    dw2 = mm(r, dxb, name=f"{tag}_dw2", ta=True, pre_a=_sq)
    dw1 = mm(h, da, name=f"{tag}_dw1", ta=True)
    dh = mm(da, w1, name=f"{tag}_dh", tb=True)
    dx, dxb, dg = rms_bwd(x, g, dh, dx, f"{tag}_rmsb")
    return dx, dxb, dg, dw1, dw2


def _conv_taps(x, S):
    t = lax.broadcasted_iota(jnp.int32, x.shape, 0)
    taps = []
    for j in range(GDN_CONV):
        sh = j - GDN_CONV // 2
        xs = x if sh == 0 else pltpu.roll(x, (-sh) % S, 0)
        taps.append(jnp.where((t + sh >= 0) & (t + sh < S), xs, 0.0))
    return taps


def _qkv_scale(c):
    is_norm = c < 2 * GDN_H
    scale = jnp.where(c < GDN_H, GDN_DK ** -0.5, 1.0)
    return is_norm, scale


def gdn_pre_fwd(proj, convw, name):
    S = proj.shape[0]

    def body(p_ref, w_ref, o_ref):
        c = pl.program_id(0)
        x = p_ref[...]
        w = w_ref[...]
        y = jnp.zeros_like(x)
        for j, xs in enumerate(_conv_taps(x, S)):
            y = y + w[j:j + 1, :] * xs
        t = y * _sigmoid(y)
        is_norm, scale = _qkv_scale(c)
        r = lax.rsqrt(jnp.sum(t * t, axis=-1, keepdims=True) + 1e-6)
        o_ref[...] = jnp.where(is_norm, t * r * scale, t)

    return pl.pallas_call(
        body, grid=(GDN_QKV // LANES,),
        in_specs=[BS((S, LANES), lambda c: (0, c)), BS((8, LANES), lambda c: (0, c))],
        out_specs=BS((S, LANES), lambda c: (0, c)),
        out_shape=SDS((S, GDN_QKV), F32), name=name, compiler_params=_cp(("parallel",)))(proj, convw)


def gdn_pre_bwd(proj, convw, dqkv, name):
    S = proj.shape[0]

    def body(p_ref, w_ref, d_ref, dp_ref, dw_ref):
        c = pl.program_id(0)
        x = p_ref[...]
        w = w_ref[...]
        taps = _conv_taps(x, S)
        y = jnp.zeros_like(x)
        for j, xs in enumerate(taps):
            y = y + w[j:j + 1, :] * xs
        sg = _sigmoid(y)
        t = y * sg
        is_norm, scale = _qkv_scale(c)
        dout = d_ref[0, 0] + d_ref[1, 0]
        r = lax.rsqrt(jnp.sum(t * t, axis=-1, keepdims=True) + 1e-6)
        n = t * r
        dn = dout * scale
        dt_norm = r * (dn - n * jnp.sum(dn * n, axis=-1, keepdims=True))
        dt = jnp.where(is_norm, dt_norm, dout)
        dy = dt * (sg * (1.0 + y * (1.0 - sg)))
        row = lax.broadcasted_iota(jnp.int32, (8, LANES), 0)
        dw = jnp.zeros((8, LANES), F32)
        for j, xs in enumerate(taps):
            dw = dw + jnp.where(row == j, jnp.sum(dy * xs, axis=0, keepdims=True), 0.0)
        dw_ref[...] = dw
        tt = lax.broadcasted_iota(jnp.int32, x.shape, 0)
        dx = jnp.zeros_like(x)
        for j in range(GDN_CONV):
            sh = j - GDN_CONV // 2
            ds = dy if sh == 0 else pltpu.roll(dy, sh % S, 0)
            dx = dx + w[j:j + 1, :] * jnp.where((tt - sh >= 0) & (tt - sh < S), ds, 0.0)
        dp_ref[...] = dx.astype(BF16)

    return pl.pallas_call(
        body, grid=(GDN_QKV // LANES,),
        in_specs=[BS((S, LANES), lambda c: (0, c)), BS((8, LANES), lambda c: (0, c)),
                  BS((2, 1, S, LANES), lambda c: (0, c // GDN_H, 0, c % GDN_H))],
        out_specs=[BS((S, LANES), lambda c: (0, c)), BS((8, LANES), lambda c: (0, c))],
        out_shape=[SDS((S, GDN_QKV), BF16), SDS((8, GDN_QKV), F32)],
        name=name, compiler_params=_cp(("parallel",)))(proj, convw, dqkv)


def _chunk_sum_matrix(n, upper):
    i = lax.broadcasted_iota(jnp.int32, (n, n), 0)
    j = lax.broadcasted_iota(jnp.int32, (n, n), 1)
    same = (i // GDN_C) == (j // GDN_C)
    tri = (i <= j) if upper else (i >= j)
    return jnp.where(same & tri, 1.0, 0.0).astype(F32)


def _gate_lanes(shape):
    lane = lax.broadcasted_iota(jnp.int32, shape, 1)
    return lane < GDN_H, (lane >= GDN_H) & (lane < 2 * GDN_H), (lane >= 2 * GDN_H) & (lane < 4 * GDN_H)


def gdn_gate_fwd(proj, prm, name):
    S = proj.shape[0]
    tm = min(512, S)
    ct = GDN_INP // LANES - 1

    def body(p_ref, prm_ref, o_ref):
        ab = p_ref[...]
        a_log = prm_ref[0:1, :]
        dtb = prm_ref[1:2, :]
        z = ab + dtb
        sp = jnp.maximum(z, 0.0) + jnp.log(1.0 + jnp.exp(-jnp.abs(z)))
        g = -jnp.exp(a_log) * sp
        is_f, is_b, is_beta = _gate_lanes(ab.shape)
        gf = _dot(_chunk_sum_matrix(tm, False), jnp.where(is_f, g, 0.0), HI)
        gbk = _dot(_chunk_sum_matrix(tm, True), jnp.where(is_b, g, 0.0), HI)
        o_ref[...] = gf + gbk + jnp.where(is_beta, _sigmoid(ab), 0.0)

    return pl.pallas_call(
        body, grid=(S // tm,),
        in_specs=[BS((tm, LANES), lambda i: (i, ct)), BS((8, LANES), lambda i: (0, 0))],
        out_specs=BS((tm, LANES), lambda i: (i, 0)),
        out_shape=SDS((S, LANES), F32), name=name, compiler_params=_cp(("parallel",)))(proj, prm)


def gdn_gate_bwd(proj, prm, dgb, name):
    S = proj.shape[0]
    tm = min(512, S)
    ct = GDN_INP // LANES - 1

    def body(p_ref, prm_ref, d_ref, dab_ref, dprm_ref):
        i = pl.program_id(0)
        ab = p_ref[...]
        a_log = prm_ref[0:1, :]
        dtb = prm_ref[1:2, :]
        z = ab + dtb
        sp = jnp.maximum(z, 0.0) + jnp.log(1.0 + jnp.exp(-jnp.abs(z)))
        ea = jnp.exp(a_log)
        g = -ea * sp
        is_f, is_b, is_beta = _gate_lanes(ab.shape)
        d = d_ref[...]
        dg = (_dot_tn(_chunk_sum_matrix(tm, False), jnp.where(is_f, d, 0.0), HI)
              + _dot_tn(_chunk_sum_matrix(tm, True), jnp.where(is_b, d, 0.0), HI))
        da = dg * (-ea) * _sigmoid(z)
        beta = _sigmoid(ab)
        dab_ref[...] = jnp.where(is_beta, d * beta * (1.0 - beta), da).astype(BF16)
        row = lax.broadcasted_iota(jnp.int32, (8, LANES), 0)
        part = (jnp.where(row == 0, jnp.sum(dg * g, axis=0, keepdims=True), 0.0)
                + jnp.where(row == 1, jnp.sum(da, axis=0, keepdims=True), 0.0))

        @pl.when(i == 0)
        def _():
            dprm_ref[...] = part

        @pl.when(i > 0)
        def _():
            dprm_ref[...] += part

    return pl.pallas_call(
        body, grid=(S // tm,),
        in_specs=[BS((tm, LANES), lambda i: (i, ct)), BS((8, LANES), lambda i: (0, 0)), BS((tm, LANES), lambda i: (i, 0))],
        out_specs=[BS((tm, LANES), lambda i: (i, 0)), BS((8, LANES), lambda i: (0, 0))],
        out_shape=[SDS((S, LANES), BF16), SDS((8, LANES), F32)],
        name=name, compiler_params=_cp(("arbitrary",)))(proj, prm, dgb)


def _tri_masks(d):
    i = lax.broadcasted_iota(jnp.int32, (GDN_C, GDN_C), 0)
    j = lax.broadcasted_iota(jnp.int32, (GDN_C, GDN_C), 1)
    s = (i - j) * (1 - 2 * d)
    return s >= 0, s > 0


def _split(a):
    hi = _bf(a)
    return hi, _bf(a - hi.astype(F32))


def _dot3(a, b):
    return _dot(a[0], b[0]) + (_dot(a[0], b[1]) + _dot(a[1], b[0]))


def _inv_unit_tri(a):
    i = lax.broadcasted_iota(jnp.int32, a.shape, 0)
    j = lax.broadcasted_iota(jnp.int32, a.shape, 1)
    m = -a
    p = jnp.where(i == j, 1.0, 0.0) + m
    for _ in range(int(math.log2(GDN_C)) - 1):
        ms = _split(m)
        m = _dot3(ms, ms)
        p = p + _dot3(_split(p), _split(m))
    return p


def _lane_col(x, lane_idx):
    lane = lax.broadcasted_iota(jnp.int32, x.shape, 1)
    return jnp.sum(jnp.where(lane == lane_idx, x, 0.0), axis=1, keepdims=True)


def _chunk_gates(gb_ref, grow_ref, ci, d, h):
    gbv = gb_ref[ci * GDN_C:(ci + 1) * GDN_C, :]
    gcol = _lane_col(gbv, d * GDN_H + h)
    bcol = _lane_col(gbv, 2 * GDN_H + d * GDN_H + h)
    glast = jnp.where(d == 0, gcol[GDN_C - 1:GDN_C, :], gcol[0:1, :])
    return gcol, bcol, grow_ref[0, ci:ci + 1, :], glast


def _chunk_base(q, k, gcol, grow, bcol, glast, d):
    incl, strict = _tri_masks(d)
    decay = jnp.where(incl, jnp.exp(jnp.where(incl, gcol - grow, 0.0)), 0.0)
    kb = k * bcol
    kk = _dot_nt(_bf(kb), _bf(k))
    qk = _dot_nt(_bf(q), _bf(k))
    eg = jnp.exp(gcol)
    ek = jnp.exp(glast - gcol)
    return dict(incl=incl, strict=strict, decay=decay, kb=kb, kk=kk, qk=qk, eg=eg, ek=ek, q_dec=q * eg, k_dec=k * ek)


def _chunk_solve(t, v, bcol):
    tinv = _inv_unit_tri(jnp.where(t["strict"], t["kk"] * t["decay"], 0.0))
    ts = _split(tinv)
    return tinv, _dot3(ts, _split(v * bcol)), _dot3(ts, _split(t["kb"] * t["eg"]))


def _gdn_specs(S, nblk, order):
    R = GDN_GC * GDN_C

    def qkv_spec(off):
        return BS((R, GDN_DK), lambda d, h, n: (order(d, n), off + h))

    gb_spec = BS((R, LANES), lambda d, h, n: (order(d, n), 0))
    grow_spec = BS((1, GDN_GC, GDN_C), lambda d, h, n: (d * GDN_H + h, order(d, n), 0))
    st_spec = BS((1, 1, GDN_GC, GDN_DK, GDN_DK), lambda d, h, n: (d, h, order(d, n), 0, 0))
    return qkv_spec, gb_spec, grow_spec, st_spec


def _lane_row(x):
    return jnp.broadcast_to(x, (1, LANES))


def gdn_scan_fwd(qkv, gb, grow, name):
    S = qkv.shape[0]
    R = GDN_GC * GDN_C
    nblk = S // R
    nc = S // GDN_C

    def order(d, n):
        return n + d * (nblk - 1 - 2 * n)

    qkv_spec, gb_spec, grow_spec, st_spec = _gdn_specs(S, nblk, order)

    def body(q_ref, k_ref, v_ref, gb_ref, grow_ref, o_ref, st_ref, s_scr, u_scr, w_scr, qd_scr, kd_scr, in_scr, egl_scr):
        d = pl.program_id(0)
        h = pl.program_id(1)
        n = pl.program_id(2)

        @pl.when(n == 0)
        def _():
            s_scr[...] = jnp.zeros_like(s_scr)

        for ci in range(GDN_GC):
            rows = slice(ci * GDN_C, (ci + 1) * GDN_C)
            gcol, bcol, grow_v, glast = _chunk_gates(gb_ref, grow_ref, ci, d, h)
            t = _chunk_base(q_ref[rows, :], k_ref[rows, :], gcol, grow_v, bcol, glast, d)
            _, u, w = _chunk_solve(t, v_ref[rows, :], bcol)
            u_scr[ci] = u
            w_scr[ci] = _bf(w)
            qd_scr[ci] = _bf(t["q_dec"])
            kd_scr[ci] = _bf(t["k_dec"])
            in_scr[ci] = _bf(jnp.where(t["incl"], t["qk"] * t["decay"], 0.0))
            egl_scr[ci] = _lane_row(jnp.exp(glast))

        def chunk(cc, carry):
            ci = cc + d * (GDN_GC - 1 - 2 * cc)
            rows = pl.ds(pl.multiple_of(ci * GDN_C, GDN_C), GDN_C)
            st = s_scr[...]
            st_ref[0, 0, ci] = st
            sb = _bf(st)
            vnb = _bf(u_scr[ci] - _dot(w_scr[ci], sb))
            o_ref[0, rows, :] = _dot(qd_scr[ci], sb) + _dot(in_scr[ci], vnb)
            s_scr[...] = st * egl_scr[ci] + _dot_tn(kd_scr[ci], vnb)
            return carry

        lax.fori_loop(0, GDN_GC, chunk, 0)

    blk = (GDN_GC, GDN_C, GDN_DK)
    return pl.pallas_call(
        body, grid=(2, GDN_H, nblk),
        in_specs=[qkv_spec(0), qkv_spec(GDN_H), qkv_spec(2 * GDN_H), gb_spec, grow_spec],
        out_specs=[BS((1, R, GDN_DK), lambda d, h, n: (d, order(d, n), h)), st_spec],
        out_shape=[SDS((2, S, GDN_H * GDN_DK), F32), SDS((2, GDN_H, nc, GDN_DK, GDN_DK), F32)],
        scratch_shapes=[pltpu.VMEM((GDN_DK, GDN_DK), F32), pltpu.VMEM(blk, F32), pltpu.VMEM(blk, BF16),
                        pltpu.VMEM(blk, BF16), pltpu.VMEM(blk, BF16), pltpu.VMEM((GDN_GC, GDN_C, GDN_C), BF16),
                        pltpu.VMEM((GDN_GC, 1, LANES), F32)],
        name=name, compiler_params=_cp(("parallel", "parallel", "arbitrary")))(qkv, qkv, qkv, gb, grow)


def gdn_scan_bwd(qkv, gb, grow, states, do, name):
    S = qkv.shape[0]
    R = GDN_GC * GDN_C
    nblk = S // R

    def order(d, n):
        return (nblk - 1 - n) - d * (nblk - 1 - 2 * n)

    qkv_spec, gb_spec, grow_spec, st_spec = _gdn_specs(S, nblk, order)

    def body(q_ref, k_ref, v_ref, gb_ref, grow_ref, st_ref, do_ref, dqkv_ref, dgate_ref,
             ds_scr, tt_scr, u_scr, w_scr, vn_scr, kd_scr, dv1_scr, qtdo_scr, egl_scr, dsin_scr, dvn_scr, sdot_scr):
        d = pl.program_id(0)
        h = pl.program_id(1)
        n = pl.program_id(2)

        @pl.when(n == 0)
        def _():
            ds_scr[...] = jnp.zeros_like(ds_scr)

        for ci in range(GDN_GC):
            rows = slice(ci * GDN_C, (ci + 1) * GDN_C)
            gcol, bcol, grow_v, glast = _chunk_gates(gb_ref, grow_ref, ci, d, h)
            t = _chunk_base(q_ref[rows, :], k_ref[rows, :], gcol, grow_v, bcol, glast, d)
            tinv, u, w = _chunk_solve(t, v_ref[rows, :], bcol)
            wb = _bf(w)
            dob = _bf(do_ref[rows, :])
            tt_scr[ci] = tinv.T
            u_scr[ci] = _bf(u)
            w_scr[ci] = wb
            vn_scr[ci] = _bf(u - _dot(wb, _bf(st_ref[0, 0, ci])))
            kd_scr[ci] = _bf(t["k_dec"])
            dv1_scr[ci] = _dot_tn(_bf(jnp.where(t["incl"], t["qk"] * t["decay"], 0.0)), dob)
            qtdo_scr[ci] = _dot_tn(_bf(t["q_dec"]), dob)
            egl_scr[ci] = _lane_row(jnp.exp(glast))

        def chunk(cc, carry):
            ci = (GDN_GC - 1 - cc) - d * (GDN_GC - 1 - 2 * cc)
            dsn = ds_scr[...]
            dsb = _bf(dsn)
            dsin_scr[ci] = dsb
            d_vnew = dv1_scr[ci] + _dot(kd_scr[ci], dsb)
            dvn_scr[ci] = d_vnew
            sd = jnp.sum(jnp.sum(st_ref[0, 0, ci] * dsn, axis=1, keepdims=True), axis=0, keepdims=True)
            sdot_scr[ci] = _lane_row(sd)
            ds_scr[...] = qtdo_scr[ci] + egl_scr[ci] * dsn - _dot_tn(w_scr[ci], _bf(d_vnew))
            return carry

        lax.fori_loop(0, GDN_GC, chunk, 0)

        for ci in range(GDN_GC):
            rows = slice(ci * GDN_C, (ci + 1) * GDN_C)
            gcol, bcol, grow_v, glast = _chunk_gates(gb_ref, grow_ref, ci, d, h)
            q, k, v = q_ref[rows, :], k_ref[rows, :], v_ref[rows, :]
            t = _chunk_base(q, k, gcol, grow_v, bcol, glast, d)
            incl, strict, decay, kb, eg, ek = t["incl"], t["strict"], t["decay"], t["kb"], t["eg"], t["ek"]
            sb = _bf(st_ref[0, 0, ci])
            dsb = dsin_scr[ci]
            vnb = vn_scr[ci]
            kbf, qbf = _bf(k), _bf(q)
            dob = _bf(do_ref[rows, :])
            d_vnew = dvn_scr[ci]
            dvb = _bf(d_vnew)
            d_intra = jnp.where(incl, _dot_nt(dob, vnb), 0.0)
            d_qdec = _dot_nt(dob, sb)
            d_kdec = _dot_nt(vnb, dsb)
            dgl = egl_scr[ci][:, 0:1] * sdot_scr[ci][:, 0:1]
            dw = -_dot_nt(dvb, sb)
            tts = _split(tt_scr[ci])
            d_ru = _dot3(tts, _split(d_vnew))
            d_rw = _dot3(tts, _split(dw))
            da = -jnp.where(strict, _dot_nt(_bf(d_ru), u_scr[ci]) + _dot_nt(_bf(d_rw), w_scr[ci]), 0.0)
            dv = d_ru * bcol
            dbeta = jnp.sum(d_ru * v, axis=1, keepdims=True)
            dkb = d_rw * eg
            dg = jnp.sum(d_rw * kb, axis=1, keepdims=True) * eg
            dkk = _bf(da * decay)
            dqk = _bf(d_intra * decay)
            dkb = dkb + _dot(dkk, kbf)
            dk = _dot_tn(dkk, _bf(kb)) + _dot_tn(dqk, qbf)
            dq = _dot(dqk, kbf) + d_qdec * eg
            dd = (da * t["kk"] + d_intra * t["qk"]) * decay
            dg = dg + jnp.sum(dd, axis=1, keepdims=True) - jnp.sum(dd.T, axis=1, keepdims=True)
            dg = dg + jnp.sum(d_qdec * t["q_dec"], axis=1, keepdims=True)
            dk = dk + d_kdec * ek
            ee = jnp.sum(d_kdec * t["k_dec"], axis=1, keepdims=True)
            dg = dg - ee
            dgl = dgl + jnp.sum(ee, axis=0, keepdims=True)
            dk = dk + dkb * bcol
            dbeta = dbeta + jnp.sum(dkb * k, axis=1, keepdims=True)
            ridx = lax.broadcasted_iota(jnp.int32, (GDN_C, 1), 0)
            dg = dg + jnp.where(ridx == (GDN_C - 1) * (1 - d), dgl, 0.0)
            dqkv_ref[0, 0, rows, :] = dq
            dqkv_ref[0, 1, rows, :] = dk
            dqkv_ref[0, 2, rows, :] = dv
            lane2 = lax.broadcasted_iota(jnp.int32, (GDN_C, 2), 1)
            dgate_ref[0, 0, rows, :] = jnp.where(lane2 == 0, dg, dbeta)

    blk = (GDN_GC, GDN_C, GDN_DK)
    sq = (GDN_GC, GDN_DK, GDN_DK)
    row = (GDN_GC, 1, LANES)
    return pl.pallas_call(
        body, grid=(2, GDN_H, nblk),
        in_specs=[qkv_spec(0), qkv_spec(GDN_H), qkv_spec(2 * GDN_H), gb_spec, grow_spec, st_spec,
                  BS((R, GDN_DK), lambda d, h, n: (order(d, n), h))],
        out_specs=[BS((1, 3, R, GDN_DK), lambda d, h, n: (d, 0, order(d, n), h)),
                   BS((1, 1, R, 2), lambda d, h, n: (d, h, order(d, n), 0))],
        out_shape=[SDS((2, 3, S, GDN_H * GDN_DK), F32), SDS((2, GDN_H, S, 2), F32)],
        scratch_shapes=[pltpu.VMEM((GDN_DK, GDN_DK), F32), pltpu.VMEM((GDN_GC, GDN_C, GDN_C), F32),
                        pltpu.VMEM(blk, BF16), pltpu.VMEM(blk, BF16), pltpu.VMEM(blk, BF16), pltpu.VMEM(blk, BF16),
                        pltpu.VMEM(blk, F32), pltpu.VMEM(sq, F32), pltpu.VMEM(row, F32), pltpu.VMEM(sq, BF16),
                        pltpu.VMEM(blk, F32), pltpu.VMEM(row, F32)],
        name=name, compiler_params=_cp(("parallel", "parallel", "arbitrary")))(qkv, qkv, qkv, gb, grow, states, do)


def gdn_post_fwd(o2, proj, nw, name):
    S = proj.shape[0]
    tm = min(512, S)
    zoff = GDN_QKV // LANES

    def body(o_ref, z_ref, nw_ref, y_ref):
        o = o_ref[0] + o_ref[1]
        z = z_ref[...]
        r = lax.rsqrt(jnp.mean(o * o, axis=-1, keepdims=True) + RMS_EPS)
        y_ref[...] = (o * r * nw_ref[...] * (z * _sigmoid(z))).astype(BF16)

    return pl.pallas_call(
        body, grid=(S // tm, GDN_H),
        in_specs=[BS((2, tm, LANES), lambda i, h: (0, i, h)), BS((tm, LANES), lambda i, h: (i, zoff + h)),
                  BS((1, LANES), lambda i, h: (0, 0))],
        out_specs=BS((tm, LANES), lambda i, h: (i, h)),
        out_shape=SDS((S, GDN_H * GDN_DK), BF16), name=name, compiler_params=_cp(("parallel", "parallel")))(o2, proj, nw)


def gdn_post_bwd(o2, proj, nw, dy, name):
    S = proj.shape[0]
    tm = min(512, S)
    zoff = GDN_QKV // LANES

    def body(o_ref, z_ref, nw_ref, dy_ref, do_ref, dz_ref, dnw_ref):
        first = (pl.program_id(0) == 0) & (pl.program_id(1) == 0)
        o = o_ref[0] + o_ref[1]
        z = z_ref[...]
        nwv = nw_ref[...]
        dyv = dy_ref[...]
        r = lax.rsqrt(jnp.mean(o * o, axis=-1, keepdims=True) + RMS_EPS)
        n = o * r
        sg = _sigmoid(z)
        sz = z * sg
        dz_ref[...] = (dyv * n * nwv * (sg * (1.0 + z * (1.0 - sg)))).astype(BF16)
        dn = dyv * nwv * sz
        do_ref[...] = r * (dn - n * jnp.mean(dn * n, axis=-1, keepdims=True))
        part = jnp.sum(dyv * n * sz, axis=0, keepdims=True)

        @pl.when(first)
        def _():
            dnw_ref[...] = part

        @pl.when(jnp.logical_not(first))
        def _():
            dnw_ref[...] += part

    blk = BS((tm, LANES), lambda i, h: (i, h))
    return pl.pallas_call(
        body, grid=(S // tm, GDN_H),
        in_specs=[BS((2, tm, LANES), lambda i, h: (0, i, h)), BS((tm, LANES), lambda i, h: (i, zoff + h)),
                  BS((1, LANES), lambda i, h: (0, 0)), blk],
        out_specs=[blk, blk, BS((1, LANES), lambda i, h: (0, 0))],
        out_shape=[SDS((S, GDN_H * GDN_DK), F32), SDS((S, GDN_H * GDN_DK), BF16), SDS((1, LANES), F32)],
        name=name, compiler_params=_cp(("arbitrary", "arbitrary")))(o2, proj, nw, dy)


def _gate_prm(a_log, dt_bias):
    z = jnp.zeros((8, LANES), F32)
    z = z.at[0, :2 * GDN_H].set(a_log.reshape(-1))
    return z.at[1, :2 * GDN_H].set(dt_bias.reshape(-1))


def gdn_fwd(x, g, w_all, convw, a_log, dt_bias, nw, w_out, tag):
    S = x.shape[0]
    h = rms_fwd(x, g, f"{tag}_rms")
    proj = mm(h, w_all, name=f"{tag}_proj", tn=1408)
    qkv = gdn_pre_fwd(proj, convw, f"{tag}_pre")
    prm = _gate_prm(a_log, dt_bias)
    gb = gdn_gate_fwd(proj, prm, f"{tag}_gate")
    grow = gb[:, :2 * GDN_H].T.reshape(2 * GDN_H, S // GDN_C, GDN_C)
    o2, states = gdn_scan_fwd(qkv, gb, grow, f"{tag}_scan")
    y = gdn_post_fwd(o2, proj, nw, f"{tag}_post")
    xn = mm(y, w_out, name=f"{tag}_out", epi=_add, extras=(x,))
    return xn, (h, proj, qkv, prm, gb, grow, o2, states, y)


def gdn_bwd(x, g, w_all, convw, nw, w_out, saved, dx, dxb, tag):
    S = x.shape[0]
    h, proj, qkv, prm, gb, grow, o2, states, y = saved
    dw_out = mm(y, dxb, name=f"{tag}_dwout", ta=True)
    dy = mm(dxb, w_out, name=f"{tag}_dy", tb=True)
    do, dz, dnw = gdn_post_bwd(o2, proj, nw, dy, f"{tag}_postb")
    dqkv, dgate = gdn_scan_bwd(qkv, gb, grow, states, do, f"{tag}_scanb")
    dgb = jnp.transpose(dgate, (2, 3, 0, 1)).reshape(S, 4 * GDN_H)
    dgb = jnp.pad(dgb, ((0, 0), (0, LANES - 4 * GDN_H)))
    dab, dprm = gdn_gate_bwd(proj, prm, dgb, f"{tag}_gateb")
    dpq, dconvw = gdn_pre_bwd(proj, convw, dqkv, f"{tag}_preb")
    dproj = jnp.concatenate([dpq, dz, dab], axis=1)
    dw_all = mm(h, dproj, name=f"{tag}_dwin", ta=True, tn=384)
    dh = mm(dproj, w_all, name=f"{tag}_dh", tb=True, tk=1408)
    dx, dxb, dg = rms_bwd(x, g, dh, dx, f"{tag}_rmsb")
    da_log = dprm[0, :2 * GDN_H].reshape(2, GDN_H)
    ddt = dprm[1, :2 * GDN_H].reshape(2, GDN_H)
    return dx, dxb, dg, dw_all, dconvw, da_log, ddt, dnw, dw_out


def _rel_bucket_np(rel):
    nb = REL_BUCKETS // 2
    max_exact = nb // 2
    ret = np.where(rel > 0, nb, 0)
    n = np.abs(rel)
    nf = np.maximum(n, 1).astype(np.float32)
    large = max_exact + (np.log(nf / max_exact) / np.float32(math.log(REL_MAX_DIST / max_exact))
                         * (nb - max_exact)).astype(np.int32)
    large = np.minimum(large, nb - 1)
    return ret + np.where(n < max_exact, n, large)


def _dswa_qb(L):
    return min(256, L)


def _toeplitz(f, rows, cols):
    period = rows + cols
    e = jnp.pad(f, ((0, 0), (0, period - f.shape[1])))
    y = jnp.tile(e, (1, rows))[:, :rows * (period - 1)]
    return y.reshape(f.shape[0], rows, period - 1)[:, :, :cols]


def _bias_mats(rel_table, gi, L):
    _, dil = DSWA_CFG[gi]
    qb = _dswa_qb(L)
    offs = np.arange(-DSWA_HALF, DSWA_HALF + 1)
    onehot = jnp.asarray(np.eye(REL_BUCKETS, dtype=np.float32)[_rel_bucket_np(offs * dil)])
    f = jnp.dot(onehot, rel_table, precision=HI)[:, gi * DSWA_HG:(gi + 1) * DSWA_HG].T
    bias = _toeplitz(f, qb, qb + 2 * DSWA_HALF)
    bias_t = jnp.transpose(_toeplitz(f[:, ::-1], qb, qb + 2 * DSWA_HALF), (0, 2, 1))
    return bias, bias_t


def _win_specs(qb, L, width, major):
    m = qb // DSWA_HALF
    last = L // DSWA_HALF - 1

    def prev(*ids):
        s, b = major(*ids)
        return (s, jnp.maximum(b * m - 1, 0), 0)

    def cur(*ids):
        s, b = major(*ids)
        return (s, b, 0)

    def nxt(*ids):
        s, b = major(*ids)
        return (s, jnp.minimum((b + 1) * m, last), 0)

    return [BS((1, DSWA_HALF, width), prev), BS((1, qb, width), cur), BS((1, DSWA_HALF, width), nxt)]


def _window(p_ref, c_ref, n_ref):
    return jnp.concatenate([p_ref[0], c_ref[0], n_ref[0]], axis=0)


def _band_valid(qb, b, L, transposed):
    shape = (qb + 2 * DSWA_HALF, qb) if transposed else (qb, qb + 2 * DSWA_HALF)
    blk = lax.broadcasted_iota(jnp.int32, shape, 1 if transposed else 0)
    win = lax.broadcasted_iota(jnp.int32, shape, 0 if transposed else 1)
    off = win - DSWA_HALF - blk
    pos = b * qb - DSWA_HALF + win
    return (jnp.abs(off) <= DSWA_HALF) & (pos >= 0) & (pos < L)


def attn_fwd(q, k, v, bias, dil, name):
    NS, L, E = q.shape
    qb = _dswa_qb(L)

    def major(s, b):
        return s, b

    win = _win_specs(qb, L, E, major)

    def body(q_ref, kp, kc, kn, vp, vc, vn, b_ref, o_ref, lse_ref):
        b = pl.program_id(1)
        kw = _window(kp, kc, kn)
        vw = _window(vp, vc, vn)
        s = _dot_nt(q_ref[0], kw) * (E ** -0.5) + b_ref[0]
        s = jnp.where(_band_valid(qb, b, L, False), s, NEG_INF)
        m = jnp.max(s, axis=-1, keepdims=True)
        p = jnp.exp(s - m)
        l = jnp.sum(p, axis=-1, keepdims=True)
        lse_ref[0] = m + jnp.log(l)
        o_ref[0] = _dot(_bf(p / l), vw)

    return pl.pallas_call(
        body, grid=(NS, L // qb),
        in_specs=[win[1]] + win + win + [BS((1, qb, qb + 2 * DSWA_HALF), lambda s, b: (s // dil, 0, 0))],
        out_specs=[BS((1, qb, E), lambda s, b: (s, b, 0)), BS((1, qb, 1), lambda s, b: (s, b, 0))],
        out_shape=[SDS((NS, L, E), F32), SDS((NS, L, 1), F32)],
        name=name, compiler_params=_cp(("parallel", "parallel")))(q, k, k, k, v, v, v, bias)


def attn_bwd_q(q, k, v, bias, lse, do, dd, dil, name):
    NS, L, E = q.shape
    qb = _dswa_qb(L)
    nb = L // qb

    def major(h, r, b):
        return h * dil + r, b

    win = _win_specs(qb, L, E, major)
    col = BS((1, qb, 1), lambda h, r, b: (h * dil + r, b, 0))

    def body(q_ref, kp, kc, kn, vp, vc, vn, b_ref, lse_ref, do_ref, dd_ref, dq_ref, db_ref):
        b = pl.program_id(2)
        first = (pl.program_id(1) == 0) & (b == 0)
        kw = _window(kp, kc, kn)
        vw = _window(vp, vc, vn)
        valid = _band_valid(qb, b, L, False)
        s = _dot_nt(q_ref[0], kw) * (E ** -0.5) + b_ref[0]
        p = jnp.exp(jnp.where(valid, s - lse_ref[0], NEG_INF))
        ds = p * (_dot_nt(do_ref[0], vw) - dd_ref[0])
        dq_ref[0] = (_dot(_bf(ds), kw) * (E ** -0.5)).astype(BF16)

        @pl.when(first)
        def _():
            db_ref[0] = ds

        @pl.when(jnp.logical_not(first))
        def _():
            db_ref[0] += ds

    bspec = BS((1, qb, qb + 2 * DSWA_HALF), lambda h, r, b: (h, 0, 0))
    return pl.pallas_call(
        body, grid=(DSWA_HG, dil, nb),
        in_specs=[win[1]] + win + win + [bspec, col, win[1], col],
        out_specs=[win[1], bspec],
        out_shape=[SDS((NS, L, E), BF16), SDS((DSWA_HG, qb, qb + 2 * DSWA_HALF), F32)],
        name=name, compiler_params=_cp(("parallel", "arbitrary", "arbitrary")))(q, k, k, k, v, v, v, bias, lse, do, dd)


def attn_bwd_kv(q, k, v, bias_t, lse, do, dd, dil, name):
    NS, L, E = q.shape
    qb = _dswa_qb(L)

    def major(s, b):
        return s, b

    win = _win_specs(qb, L, E, major)
    wcol = _win_specs(qb, L, 1, major)

    def body(kc, vc, qp, qc, qn, dop, doc, don, lp, lc, ln, ddp, ddc, ddn, b_ref, dk_ref, dv_ref):
        b = pl.program_id(1)
        qw = _window(qp, qc, qn)
        dow = _window(dop, doc, don)
        lw = _window(lp, lc, ln)
        ddw = _window(ddp, ddc, ddn)
        valid = _band_valid(qb, b, L, True)
        s = _dot_nt(qw, kc[0]) * (E ** -0.5) + b_ref[0]
        p = jnp.exp(jnp.where(valid, s - lw, NEG_INF))
        dv_ref[0] = _dot_tn(_bf(p), dow).astype(BF16)
        ds = jnp.where(valid, p * (_dot_nt(dow, vc[0]) - ddw), 0.0)
        dk_ref[0] = (_dot_tn(_bf(ds), qw) * (E ** -0.5)).astype(BF16)

    return pl.pallas_call(
        body, grid=(NS, L // qb),
        in_specs=[win[1], win[1]] + win + win + wcol + wcol + [BS((1, qb + 2 * DSWA_HALF, qb), lambda s, b: (s // dil, 0, 0))],
        out_specs=[win[1], win[1]],
        out_shape=[SDS((NS, L, E), BF16), SDS((NS, L, E), BF16)],
        name=name, compiler_params=_cp(("parallel", "parallel")))(k, v, q, q, q, do, do, do, lse, lse, lse, dd, dd, dd, bias_t)


def _head_expand():
    i = lax.broadcasted_iota(jnp.int32, (LANES, DSWA_HG * DSWA_E), 0)
    j = lax.broadcasted_iota(jnp.int32, (LANES, DSWA_HG * DSWA_E), 1)
    return jnp.where(i == j // DSWA_E, 1.0, 0.0).astype(F32)


def _group_alphas(lse3):
    m = jnp.maximum(jnp.maximum(lse3[0], lse3[1]), lse3[2])
    e = [jnp.exp(t - m) for t in lse3]
    tot = e[0] + e[1] + e[2]
    return [t / tot for t in e]


def combine_fwd(o_raw, lse3, name):
    S = o_raw.shape[0]
    tm = min(512, S)
    gw = DSWA_HG * DSWA_E

    def body(o_ref, l_ref, y_ref):
        alphas = _group_alphas([l_ref[0], l_ref[1], l_ref[2]])
        ex = _head_expand()
        for gi in range(3):
            cols = slice(gi * gw, (gi + 1) * gw)
            y_ref[:, cols] = (o_ref[:, cols] * _dot(alphas[gi], ex, HI)).astype(BF16)

    return pl.pallas_call(
        body, grid=(S // tm,),
        in_specs=[BS((tm, DSWA_W), lambda i: (i, 0)), BS((3, tm, LANES), lambda i: (0, i, 0))],
        out_specs=BS((tm, DSWA_W), lambda i: (i, 0)),
        out_shape=SDS((S, DSWA_W), BF16), name=name, compiler_params=_cp(("parallel",)))(o_raw, lse3)


def combine_bwd(o_raw, lse3, dy, name):
    S = o_raw.shape[0]
    tm = min(512, S)
    gw = DSWA_HG * DSWA_E

    def body(o_ref, l_ref, dy_ref, do_ref, dd_ref):
        alphas = _group_alphas([l_ref[0], l_ref[1], l_ref[2]])
        ex = _head_expand()
        dal = []
        for gi in range(3):
            cols = slice(gi * gw, (gi + 1) * gw)
            dyv = dy_ref[:, cols]
            do_ref[:, cols] = (dyv * _dot(alphas[gi], ex, HI)).astype(BF16)
            dal.append(_dot_nt(o_ref[:, cols] * dyv, ex, HI))
        c = alphas[0] * dal[0] + alphas[1] * dal[1] + alphas[2] * dal[2]
        for gi in range(3):
            dd_ref[gi] = alphas[gi] * c

    return pl.pallas_call(
        body, grid=(S // tm,),
        in_specs=[BS((tm, DSWA_W), lambda i: (i, 0)), BS((3, tm, LANES), lambda i: (0, i, 0)), BS((tm, DSWA_W), lambda i: (i, 0))],
        out_specs=[BS((tm, DSWA_W), lambda i: (i, 0)), BS((3, tm, LANES), lambda i: (0, i, 0))],
        out_shape=[SDS((S, DSWA_W), BF16), SDS((3, S, LANES), F32)],
        name=name, compiler_params=_cp(("parallel",)))(o_raw, lse3, dy)


def _to_sub(t, dil):
    S, hg, wd = t.shape
    return jnp.transpose(t.reshape(S // dil, dil, hg, wd), (2, 1, 0, 3)).reshape(hg * dil, S // dil, wd)


def _from_sub(t, dil):
    ns, L, wd = t.shape
    hg = ns // dil
    return jnp.transpose(t.reshape(hg, dil, L, wd), (2, 1, 0, 3)).reshape(L * dil, hg, wd)


def dswa_fwd(x, g, w_in, w_out, rel_table, tag):
    S = x.shape[0]
    h = rms_fwd(x, g, f"{tag}_rms")
    qkv = mm(h, w_in, name=f"{tag}_qkv", tn=1152, out_dtype=BF16).reshape(S, 3, DSWA_HEADS, DSWA_E)
    subs, outs, lses = [], [], []
    for gi, (_, dil) in enumerate(DSWA_CFG):
        hs = slice(gi * DSWA_HG, (gi + 1) * DSWA_HG)
        qs, ks, vs = (_to_sub(qkv[:, i, hs], dil) for i in range(3))
        bias, bias_t = _bias_mats(rel_table, gi, S // dil)
        o, lse = attn_fwd(qs, ks, vs, bias, dil, f"{tag}_att{gi}")
        subs.append((qs, ks, vs, lse))
        outs.append(_from_sub(o, dil).reshape(S, DSWA_HG * DSWA_E))
        lses.append(jnp.pad(_from_sub(lse, dil).reshape(S, DSWA_HG), ((0, 0), (0, LANES - DSWA_HG))))
    o_raw = jnp.concatenate(outs, axis=1)
    lse3 = jnp.stack(lses)
    y = combine_fwd(o_raw, lse3, f"{tag}_comb")
    xn = mm(y, w_out, name=f"{tag}_out", epi=_add, extras=(x,))
    return xn, (h, subs, o_raw, lse3, y)


def dswa_bwd(x, g, w_in, w_out, rel_table, saved, dx, dxb, tag):
    S = x.shape[0]
    h, subs, o_raw, lse3, y = saved
    dw_out = mm(y, dxb, name=f"{tag}_dwout", ta=True, tm=384)
    dy = mm(dxb, w_out, name=f"{tag}_dy", tb=True, tn=384)
    do_raw, dd3 = combine_bwd(o_raw, lse3, dy, f"{tag}_combb")
    do_raw = do_raw.reshape(S, DSWA_HEADS, DSWA_E)
    dqkv = []
    drel = jnp.zeros_like(rel_table)
    for gi, (_, dil) in enumerate(DSWA_CFG):
        hs = slice(gi * DSWA_HG, (gi + 1) * DSWA_HG)
        qs, ks, vs, lse = subs[gi]
        dos = _to_sub(do_raw[:, hs], dil)
        dds = _to_sub(dd3[gi, :, :DSWA_HG, None], dil)
        (bias, bias_t), bias_vjp = jax.vjp(lambda tbl: _bias_mats(tbl, gi, S // dil), rel_table)
        dq, dbias = attn_bwd_q(qs, ks, vs, bias, lse, dos, dds, dil, f"{tag}_attq{gi}")
        dk, dv = attn_bwd_kv(qs, ks, vs, bias_t, lse, dos, dds, dil, f"{tag}_attkv{gi}")
        drel = drel + bias_vjp((dbias, jnp.zeros_like(bias_t)))[0]
        dqkv.append([_from_sub(t, dil) for t in (dq, dk, dv)])
    dqkv = jnp.stack([jnp.concatenate([dqkv[gi][i] for gi in range(3)], axis=1) for i in range(3)], axis=1)
    dqkv = dqkv.reshape(S, 3 * DSWA_W)
    dw_in = mm(h, dqkv, name=f"{tag}_dwin", ta=True, tn=384)
    dh = mm(dqkv, w_in, name=f"{tag}_dh", tb=True, tk=1152)
    dx, dxb, dg = rms_bwd(x, g, dh, dx, f"{tag}_rmsb")
    return dx, dxb, dg, dw_in, dw_out, drel


def adamw(w, g, m, v, name):
    shape = w.shape
    last = shape[-1]
    w2, g2, m2, v2 = (t.reshape(-1, last) for t in (w, g, m, v))
    rows = w2.shape[0]
    tr = rows
    if rows > 512:
        tr = next(t for t in (512, 256, 192, 128, 64, 8) if rows % t == 0)
    c1 = 1.0 / (1.0 - ADAM_B1 ** ADAM_STEP)
    c2 = 1.0 / (1.0 - ADAM_B2 ** ADAM_STEP)

    def body(w_ref, g_ref, m_ref, v_ref, d_ref, nm_ref, nv_ref):
        gv = g_ref[...]
        nm = ADAM_B1 * m_ref[...] + (1.0 - ADAM_B1) * gv
        nv = ADAM_B2 * v_ref[...] + (1.0 - ADAM_B2) * (gv * gv)
        nm_ref[...] = nm
        nv_ref[...] = nv
        d_ref[...] = -ADAM_LR * ((nm * c1) / (jnp.sqrt(nv * c2) + ADAM_EPS) + ADAM_WD * w_ref[...])

    spec = BS((tr, last), lambda i: (i, 0))
    outs = pl.pallas_call(
        body, grid=(rows // tr,), in_specs=[spec] * 4, out_specs=[spec] * 3,
        out_shape=[SDS((rows, last), F32)] * 3, name=name, compiler_params=_cp(("parallel",)))(w2, g2, m2, v2)
    return tuple(o.reshape(shape) for o in outs)


ANY = BS(memory_space=pl.ANY)
GATHER_CHUNKS = 5


def _place():
    x, y, c = lax.axis_index("x"), lax.axis_index("y"), lax.axis_index("c")
    chips = [(1 - x, y), (x, 1 - y), (1 - x, 1 - y)]
    return x, y, c, chips


def _rcopy(src, dst, ssem, rsem, dev):
    return pltpu.make_async_remote_copy(src_ref=src, dst_ref=dst, send_sem=ssem, recv_sem=rsem, device_id=dev,
                                        device_id_type=MESH)


def gather_weights(pack, convp):
    RP = pack.shape[0]
    half = RP // 2
    nch = GATHER_CHUNKS
    ch = half // nch
    assert ch * nch == half and ch % 16 == 0

    def body(p_ref, c_ref, full_ref, cfull_ref, ssem, rsem, fssem, frsem, cssem, crsem, lsem):
        x, y, c, chips = _place()
        jme = 2 * x + y
        sib = (x, y, 1 - c)

        def rows(hc, q):
            return pl.ds(hc * half + q * ch, ch)

        own = pltpu.make_async_copy(p_ref, full_ref.at[jme], lsem.at[0])
        own_c = pltpu.make_async_copy(c_ref, cfull_ref.at[jme], lsem.at[1])
        own.start()
        own_c.start()
        sends = []
        for r, (cx, cy) in enumerate(chips):
            for q in range(nch):
                cp = _rcopy(p_ref.at[rows(c, q)], full_ref.at[jme, rows(c, q)], ssem.at[r, q], rsem.at[r, q], (cx, cy, c))
                cp.start()
                sends.append(cp)
            cp = _rcopy(c_ref, cfull_ref.at[jme], cssem.at[r], crsem.at[r], (cx, cy, c))
            cp.start()
            sends.append(cp)
        for r, (cx, cy) in enumerate(chips):
            jr = 2 * cx + cy
            for q in range(nch):
                piece = full_ref.at[jr, rows(c, q)]
                _rcopy(piece, piece, ssem.at[r, q], rsem.at[r, q], (cx, cy, c)).wait_recv()
                cp = _rcopy(piece, piece, fssem.at[r, q], frsem.at[r, q], sib)
                cp.start()
                sends.append(cp)
            _rcopy(c_ref, cfull_ref.at[jr], cssem.at[r], crsem.at[r], (cx, cy, c)).wait_recv()
        for r, (cx, cy) in enumerate(chips):
            jr = 2 * cx + cy
            for q in range(nch):
                piece = full_ref.at[jr, rows(1 - c, q)]
                _rcopy(piece, piece, fssem.at[r, q], frsem.at[r, q], sib).wait_recv()
        for cp in sends:
            cp.wait_send()
        own.wait()
        own_c.wait()

    dma = pltpu.SemaphoreType.DMA
    return pl.pallas_call(
        body, in_specs=[ANY, ANY], out_specs=[ANY, ANY],
        out_shape=[SDS((4,) + pack.shape, pack.dtype), SDS((4,) + convp.shape, convp.dtype)],
        scratch_shapes=[dma((3, nch)), dma((3, nch)), dma((3, nch)), dma((3, nch)), dma((3,)), dma((3,)), dma((2,))],
        name="gather_weights", compiler_params=pltpu.CompilerParams(has_side_effects=True))(pack, convp)


def rs_sibling_exchange(gpack):
    _, RP, W = gpack.shape
    half = RP // 2

    def body(g_ref, r_ref, ssem, rsem):
        x, y, c, _ = _place()
        cps = [_rcopy(g_ref.at[j, pl.ds((1 - c) * half, half)], r_ref.at[j], ssem.at[j], rsem.at[j], (x, y, 1 - c))
               for j in range(4)]
        for cp in cps:
            cp.start()
        for cp in cps:
            cp.wait()

    dma = pltpu.SemaphoreType.DMA
    return pl.pallas_call(
        body, in_specs=[ANY], out_specs=ANY, out_shape=SDS((4, half, W), gpack.dtype),
        scratch_shapes=[dma((4,)), dma((4,))], name="rs_sibling_exchange",
        compiler_params=pltpu.CompilerParams(has_side_effects=True))(gpack)


def _rs_tile(half):
    return next(t for t in (656, 512, 328, 256, 128, 64, 8) if half % t == 0)


def rs_add_sibling(gpack, recv, cidx):
    _, RP, W = gpack.shape
    half = RP // 2
    tr = _rs_tile(half)
    nb = half // tr

    def body(c_ref, g_ref, r_ref, o_ref):
        o_ref[...] = g_ref[...] + r_ref[...]

    gs = pltpu.PrefetchScalarGridSpec(
        num_scalar_prefetch=1, grid=(4, nb),
        in_specs=[BS((1, tr, W), lambda j, i, c: (j, c[0] * nb + i, 0)), BS((1, tr, W), lambda j, i, c: (j, i, 0))],
        out_specs=BS((1, tr, W), lambda j, i, c: (j, i, 0)))
    return pl.pallas_call(body, grid_spec=gs, out_shape=SDS((4, half, W), F32), name="rs_add_sibling",
                          compiler_params=_cp(("parallel", "parallel")))(cidx, gpack, recv)


def rs_chip_exchange(part):
    _, half, W = part.shape

    def body(p_ref, r_ref, ssem, rsem, lsem):
        x, y, c, chips = _place()
        own = pltpu.make_async_copy(p_ref.at[2 * x + y], r_ref.at[3], lsem)
        own.start()
        cps = [_rcopy(p_ref.at[2 * cx + cy], r_ref.at[r], ssem.at[r], rsem.at[r], (cx, cy, c))
               for r, (cx, cy) in enumerate(chips)]
        for cp in cps:
            cp.start()
        for cp in cps:
            cp.wait()
        own.wait()

    dma = pltpu.SemaphoreType.DMA
    return pl.pallas_call(
        body, in_specs=[ANY], out_specs=ANY, out_shape=SDS((4, half, W), part.dtype),
        scratch_shapes=[dma((3,)), dma((3,)), dma], name="rs_chip_exchange",
        compiler_params=pltpu.CompilerParams(has_side_effects=True))(part)


def rs_add_chips(recv):
    _, half, W = recv.shape
    tr = _rs_tile(half)

    def body(r_ref, o_ref):
        o_ref[...] = ((r_ref[0] + r_ref[1]) + r_ref[2]) + r_ref[3]

    return pl.pallas_call(
        body, grid=(half // tr,), in_specs=[BS((4, tr, W), lambda i: (0, i, 0))], out_specs=BS((tr, W), lambda i: (i, 0)),
        out_shape=SDS((half, W), F32), name="rs_add_chips", compiler_params=_cp(("parallel",)))(recv)


def rs_sibling_share(fin):
    half, W = fin.shape

    def body(f_ref, o_ref, ssem, rsem, lsem):
        x, y, c, _ = _place()
        mine = o_ref.at[pl.ds(c * half, half)]
        own = pltpu.make_async_copy(f_ref, mine, lsem)
        cp = _rcopy(f_ref, mine, ssem, rsem, (x, y, 1 - c))
        own.start()
        cp.start()
        cp.wait()
        own.wait()

    dma = pltpu.SemaphoreType.DMA
    return pl.pallas_call(
        body, in_specs=[ANY], out_specs=ANY, out_shape=SDS((2 * half, W), fin.dtype),
        scratch_shapes=[dma, dma, dma], name="rs_sibling_share",
        compiler_params=pltpu.CompilerParams(has_side_effects=True))(fin)


def allreduce_small(pack):
    R = pack.shape[0]

    def body(p_ref, o_ref, all_ref, ssem, rsem):
        x, y, c, _ = _place()
        me = 4 * x + 2 * y + c
        all_ref[me] = p_ref[...]
        cps = []
        for m in range(1, 8):
            peer = (1 - x if m & 4 else x, 1 - y if m & 2 else y, 1 - c if m & 1 else c)
            cp = _rcopy(p_ref, all_ref.at[me], ssem.at[m - 1], rsem.at[m - 1], peer)
            cp.start()
            cps.append(cp)
        for cp in cps:
            cp.wait()
        acc = all_ref[0]
        for i in range(1, 8):
            acc = acc + all_ref[i]
        o_ref[...] = acc

    dma = pltpu.SemaphoreType.DMA
    vm = BS(memory_space=pltpu.VMEM)
    return pl.pallas_call(
        body, in_specs=[vm], out_specs=vm, out_shape=SDS(pack.shape, F32),
        scratch_shapes=[pltpu.VMEM((8, R, LANES), F32), dma((7,)), dma((7,))], name="allreduce_small",
        compiler_params=pltpu.CompilerParams(has_side_effects=True))(pack)


PACK_W = 1024
PACK_ENTRIES = []
for _l in range(DEPTH):
    if _l % 2 == 0:
        PACK_ENTRIES += [("gdn_w_in", _l // 2, D_MODEL, GDN_IN // 4, True), ("gdn_w_out", _l // 2, D_MODEL // 4, D_MODEL, False)]
    else:
        PACK_ENTRIES += [("dswa_w_in", _l // 2, D_MODEL, 3 * DSWA_W // 4, True), ("dswa_w_out", _l // 2, DSWA_W // 4, D_MODEL, False)]
    PACK_ENTRIES += [("mlp_w1", _l, D_MODEL, D_FF // 4, True), ("mlp_w2", _l, D_FF // 4, D_MODEL, False)]
PACK_ROWS = [r * c // PACK_W for (_, _, r, c, _) in PACK_ENTRIES]
PACK_OFFS = [int(o) for o in np.cumsum([0] + PACK_ROWS)]
PACK_TOTAL = -(-PACK_OFFS[-1] // (32 * GATHER_CHUNKS)) * (32 * GATHER_CHUNKS)


def _pack_shards(shards, dtype):
    parts = [shards[name][li].astype(dtype).reshape(-1, PACK_W) for (name, li, _, _, _) in PACK_ENTRIES]
    parts.append(jnp.zeros((PACK_TOTAL - PACK_OFFS[-1], PACK_W), dtype))
    return jnp.concatenate(parts, axis=0)


def _unpack_full(full):
    mats = []
    for e, (name, li, r, c, by_col) in enumerate(PACK_ENTRIES):
        sh = full[:, PACK_OFFS[e]:PACK_OFFS[e + 1]].reshape(4, r, c)
        mats.append(jnp.concatenate([sh[j] for j in range(4)], axis=1 if by_col else 0))
    return mats


def _pack_grads(grads):
    per_chip = []
    for j in range(4):
        parts = []
        for e, (name, li, r, c, by_col) in enumerate(PACK_ENTRIES):
            g = grads[e]
            sh = g[:, c * j:c * (j + 1)] if by_col else g[r * j:r * (j + 1), :]
            parts.append(sh.reshape(-1, PACK_W))
        parts.append(jnp.zeros((PACK_TOTAL - PACK_OFFS[-1], PACK_W), F32))
        per_chip.append(jnp.concatenate(parts, axis=0))
    return jnp.stack(per_chip)


def _unpack_shard_grads(gsh):
    out = {}
    for e, (name, li, r, c, _) in enumerate(PACK_ENTRIES):
        out.setdefault(name, []).append(gsh[PACK_OFFS[e]:PACK_OFFS[e + 1]].reshape(r, c))
    return {k: jnp.stack(v) for k, v in out.items()}


def _flat_pad(t, mult=8 * LANES):
    f = t.reshape(-1)
    return jnp.pad(f, (0, (-f.shape[0]) % mult))


def kernel(x, norm_mix, norm_mlp, norm_final, rel_bias, gdn_w_in, gdn_conv_w, gdn_a_log, gdn_dt_bias, gdn_norm_w, gdn_w_out, dswa_w_in, dswa_w_out, mlp_w1, mlp_w2, loss_target, m_norm_mix, m_norm_mlp, m_norm_final, m_rel_bias, m_gdn_w_in, m_gdn_conv_w, m_gdn_a_log, m_gdn_dt_bias, m_gdn_norm_w, m_gdn_w_out, m_dswa_w_in, m_dswa_w_out, m_mlp_w1, m_mlp_w2, v_norm_mix, v_norm_mlp, v_norm_final, v_rel_bias, v_gdn_w_in, v_gdn_conv_w, v_gdn_a_log, v_gdn_dt_bias, v_gdn_norm_w, v_gdn_w_out, v_dswa_w_in, v_dswa_w_out, v_mlp_w1, v_mlp_w2):
    xi, yi, ci = lax.axis_index("x"), lax.axis_index("y"), lax.axis_index("c")
    jme = 2 * xi + yi
    big = dict(gdn_w_in=gdn_w_in, gdn_w_out=gdn_w_out, dswa_w_in=dswa_w_in, dswa_w_out=dswa_w_out, mlp_w1=mlp_w1, mlp_w2=mlp_w2)
    n_gdn = gdn_w_in.shape[0]
    conv_cols = gdn_conv_w.shape[-1]

    pack = _pack_shards(big, BF16)
    convp = jnp.pad(gdn_conv_w.reshape(n_gdn * GDN_CONV, conv_cols), ((0, 16 - n_gdn * GDN_CONV), (0, 0)))
    full, cfull = gather_weights(pack, convp)
    mats = _unpack_full(full)
    conv_all = jnp.transpose(cfull[:, :n_gdn * GDN_CONV], (1, 0, 2)).reshape(n_gdn, GDN_CONV, 4 * conv_cols)
    conv_all = jnp.pad(conv_all, ((0, 0), (0, 8 - GDN_CONV), (0, 0)))

    xs = x[0]
    saved = []
    for l in range(DEPTH):
        w_in, w_out, w1, w2 = mats[4 * l:4 * l + 4]
        gm, gp = norm_mix[l][None], norm_mlp[l][None]
        a = l // 2
        if l % 2 == 0:
            w_in = jnp.pad(w_in, ((0, 0), (0, GDN_INP - GDN_IN)))
            x_mid, sv = gdn_fwd(xs, gm, w_in, conv_all[a], gdn_a_log[a], gdn_dt_bias[a], gdn_norm_w[a][None], w_out, f"l{l}_gdn")
        else:
            x_mid, sv = dswa_fwd(xs, gm, w_in, w_out, rel_bias, f"l{l}_att")
        x_out, sv2 = mlp_fwd(x_mid, gp, w1, w2, f"l{l}_mlp")
        saved.append((xs, x_mid, w_in, sv, sv2))
        xs = x_out

    loss_part, dx, dxb, d_final = loss_head(xs, norm_final[None], loss_target[0], "loss_head")
    grads = [None] * len(PACK_ENTRIES)
    d_mix, d_mlp = [None] * DEPTH, [None] * DEPTH
    d_conv, d_alog, d_dt, d_nw = [None] * n_gdn, [None] * n_gdn, [None] * n_gdn, [None] * n_gdn
    d_rel = jnp.zeros_like(rel_bias)
    for l in reversed(range(DEPTH)):
        _, w_out, w1, w2 = mats[4 * l:4 * l + 4]
        x_in, x_mid, w_in, sv, sv2 = saved[l]
        gm, gp = norm_mix[l][None], norm_mlp[l][None]
        a = l // 2
        dx, dxb, d_mlp[l], grads[4 * l + 2], grads[4 * l + 3] = mlp_bwd(x_mid, gp, w1, w2, sv2, dx, dxb, f"l{l}_mlp")
        if l % 2 == 0:
            dx, dxb, d_mix[l], dw_all, d_conv[a], d_alog[a], d_dt[a], d_nw[a], grads[4 * l + 1] = gdn_bwd(
                x_in, gm, w_in, conv_all[a], gdn_norm_w[a][None], w_out, sv, dx, dxb, f"l{l}_gdn")
            grads[4 * l] = dw_all[:, :GDN_IN]
        else:
            dx, dxb, d_mix[l], grads[4 * l], grads[4 * l + 1], drel = dswa_bwd(x_in, gm, w_in, w_out, rel_bias, sv, dx, dxb, f"l{l}_att")
            d_rel = d_rel + drel

    cidx = ci.astype(jnp.int32).reshape(1)
    gpack = _pack_grads(grads)
    part = rs_add_sibling(gpack, rs_sibling_exchange(gpack), cidx)
    gsh = rs_sibling_share(rs_add_chips(rs_chip_exchange(part)))
    gbig = _unpack_shard_grads(gsh)

    small = [jnp.concatenate(d_mix, axis=0), jnp.concatenate(d_mlp, axis=0), d_final, d_rel,
             jnp.stack(d_conv), jnp.stack(d_alog), jnp.stack(d_dt), jnp.concatenate(d_nw, axis=0)]
    flat = [_flat_pad(t) for t in small]
    sizes = [f.shape[0] for f in flat]
    red = allreduce_small(jnp.concatenate(flat).reshape(-1, LANES)).reshape(-1)
    offs = np.cumsum([0] + sizes)
    red = [red[offs[i]:offs[i] + small[i].size].reshape(small[i].shape) for i in range(len(small))]
    g_conv_all = red[4][:, :GDN_CONV].reshape(n_gdn, GDN_CONV, 1, 4 * conv_cols)
    g_conv = lax.dynamic_slice_in_dim(g_conv_all, jme * conv_cols, conv_cols, axis=3)
    g = dict(norm_mix=red[0], norm_mlp=red[1], norm_final=red[2].reshape(norm_final.shape), rel_bias=red[3],
             gdn_conv_w=g_conv, gdn_a_log=red[5], gdn_dt_bias=red[6], gdn_norm_w=red[7][:, :GDN_DK], **gbig)

    w = dict(norm_mix=norm_mix, norm_mlp=norm_mlp, norm_final=norm_final, rel_bias=rel_bias, gdn_conv_w=gdn_conv_w,
             gdn_a_log=gdn_a_log, gdn_dt_bias=gdn_dt_bias, gdn_norm_w=gdn_norm_w, **big)
    m = dict(norm_mix=m_norm_mix, norm_mlp=m_norm_mlp, norm_final=m_norm_final, rel_bias=m_rel_bias, gdn_w_in=m_gdn_w_in,
             gdn_conv_w=m_gdn_conv_w, gdn_a_log=m_gdn_a_log, gdn_dt_bias=m_gdn_dt_bias, gdn_norm_w=m_gdn_norm_w,
             gdn_w_out=m_gdn_w_out, dswa_w_in=m_dswa_w_in, dswa_w_out=m_dswa_w_out, mlp_w1=m_mlp_w1, mlp_w2=m_mlp_w2)
    v = dict(norm_mix=v_norm_mix, norm_mlp=v_norm_mlp, norm_final=v_norm_final, rel_bias=v_rel_bias, gdn_w_in=v_gdn_w_in,
             gdn_conv_w=v_gdn_conv_w, gdn_a_log=v_gdn_a_log, gdn_dt_bias=v_gdn_dt_bias, gdn_norm_w=v_gdn_norm_w,
             gdn_w_out=v_gdn_w_out, dswa_w_in=v_dswa_w_in, dswa_w_out=v_dswa_w_out, mlp_w1=v_mlp_w1, mlp_w2=v_mlp_w2)
    names = ["norm_mix", "norm_mlp", "norm_final", "rel_bias", "gdn_w_in", "gdn_conv_w", "gdn_a_log", "gdn_dt_bias",
             "gdn_norm_w", "gdn_w_out", "dswa_w_in", "dswa_w_out", "mlp_w1", "mlp_w2"]
    upd = {n: adamw(w[n], g[n], m[n], v[n], f"adamw_{n}") for n in names}
    loss = lax.psum(loss_part[0, 0], ("x", "y", "c"))
    return (loss, dx[None], *[g[n] for n in names], *[upd[n][0] for n in names], *[upd[n][1] for n in names],
            *[upd[n][2] for n in names])
```

```python
import functools
import math

import numpy as np
import jax
import jax.numpy as jnp
from jax import lax
from jax.experimental import pallas as pl
from jax.experimental.pallas import tpu as pltpu

F32 = jnp.float32
BF16 = jnp.bfloat16
HI = lax.Precision.HIGHEST
BS = pl.BlockSpec
SDS = jax.ShapeDtypeStruct
MESH = pl.DeviceIdType.MESH

D_MODEL = 1024
D_FF = 4096
DEPTH = 4
RMS_EPS = 1e-6
NEG_INF = -1e30
LANES = 128
VMEM_LIMIT = 56 << 20

GDN_H = 8
GDN_DK = 128
GDN_CONV = 5
GDN_C = 64
GDN_GC = 8
GDN_HP = 4
GDN_QKV = 3 * GDN_H * GDN_DK
GDN_IN = GDN_QKV + GDN_H * GDN_DK + 4 * GDN_H
GDN_INP = 4224

DSWA_CFG = ((128, 1), (512, 4), (2048, 16))
DSWA_HG = 6
DSWA_E = 64
DSWA_HEADS = 18
DSWA_W = DSWA_HEADS * DSWA_E
DSWA_HALF = 64
REL_BUCKETS = 32
REL_MAX_DIST = 1024

ADAM_LR = 0.001
ADAM_B1 = 0.9
ADAM_B2 = 0.999
ADAM_EPS = 1e-08
ADAM_WD = 0.01
ADAM_STEP = 10


def _cp(sem=None):
    return pltpu.CompilerParams(dimension_semantics=sem, vmem_limit_bytes=VMEM_LIMIT)


def _dot(a, b, prec=None):
    return jnp.dot(a, b, precision=prec, preferred_element_type=F32)


def _dot_nt(a, b, prec=None):
    return lax.dot_general(a, b, (((1,), (1,)), ((), ())), precision=prec, preferred_element_type=F32)


def _dot_tn(a, b, prec=None):
    return lax.dot_general(a, b, (((0,), (0,)), ((), ())), precision=prec, preferred_element_type=F32)


def _bf(a):
    return a.astype(BF16)


def _sigmoid(x):
    return 1.0 / (1.0 + jnp.exp(-x))


def rms_fwd(x, g, name):
    S, Dm = x.shape
    tm = min(512, S)

    def body(x_ref, g_ref, o_ref):
        xv = x_ref[...]
        r = lax.rsqrt(jnp.mean(xv * xv, axis=-1, keepdims=True) + RMS_EPS)
        o_ref[...] = (xv * r * g_ref[...]).astype(o_ref.dtype)

    return pl.pallas_call(
        body, grid=(S // tm,),
        in_specs=[BS((tm, Dm), lambda i: (i, 0)), BS((1, Dm), lambda i: (0, 0))],
        out_specs=BS((tm, Dm), lambda i: (i, 0)),
        out_shape=SDS((S, Dm), BF16), name=name, compiler_params=_cp(("parallel",)))(x, g)


def rms_bwd(x, g, dh, dres, name):
    S, Dm = x.shape
    tm = min(512, S)

    def body(x_ref, g_ref, dh_ref, dres_ref, dx_ref, dxb_ref, dg_ref):
        i = pl.program_id(0)
        xv = x_ref[...]
        r = lax.rsqrt(jnp.mean(xv * xv, axis=-1, keepdims=True) + RMS_EPS)
        n = xv * r
        dhv = dh_ref[...]
        t = dhv * g_ref[...]
        dx = dres_ref[...] + r * (t - n * jnp.mean(n * t, axis=-1, keepdims=True))
        dx_ref[...] = dx
        dxb_ref[...] = dx.astype(BF16)
        part = jnp.sum(dhv * n, axis=0, keepdims=True)

        @pl.when(i == 0)
        def _():
            dg_ref[...] = part

        @pl.when(i > 0)
        def _():
            dg_ref[...] += part

    row = BS((tm, Dm), lambda i: (i, 0))
    vec = BS((1, Dm), lambda i: (0, 0))
    return pl.pallas_call(
        body, grid=(S // tm,), in_specs=[row, vec, row, row], out_specs=[row, row, vec],
        out_shape=[SDS((S, Dm), F32), SDS((S, Dm), BF16), SDS((1, Dm), F32)],
        name=name, compiler_params=_cp(("arbitrary",)))(x, g, dh, dres)


def loss_head(x, g, tgt, name):
    S, Dm = x.shape
    tm = min(512, S)

    def body(x_ref, g_ref, t_ref, loss_ref, dx_ref, dxb_ref, dg_ref):
        i = pl.program_id(0)
        xv = x_ref[...]
        gv = g_ref[...]
        r = lax.rsqrt(jnp.mean(xv * xv, axis=-1, keepdims=True) + RMS_EPS)
        n = xv * r
        err = n * gv - t_ref[...]
        lpart = 0.5 * jnp.sum(jnp.mean(err * err, axis=-1, keepdims=True), axis=0, keepdims=True)
        dout = err * (1.0 / Dm)
        t = dout * gv
        dx = r * (t - n * jnp.mean(n * t, axis=-1, keepdims=True))
        dx_ref[...] = dx
        dxb_ref[...] = dx.astype(BF16)
        part = jnp.sum(dout * n, axis=0, keepdims=True)

        @pl.when(i == 0)
        def _():
            dg_ref[...] = part
            loss_ref[...] = lpart

        @pl.when(i > 0)
        def _():
            dg_ref[...] += part
            loss_ref[...] += lpart

    row = BS((tm, Dm), lambda i: (i, 0))
    vec = BS((1, Dm), lambda i: (0, 0))
    one = BS((1, 1), lambda i: (0, 0))
    return pl.pallas_call(
        body, grid=(S // tm,), in_specs=[row, vec, row], out_specs=[one, row, row, vec],
        out_shape=[SDS((1, 1), F32), SDS((S, Dm), F32), SDS((S, Dm), BF16), SDS((1, Dm), F32)],
        name=name, compiler_params=_cp(("arbitrary",)))(x, g, tgt)


def mm(a, b, *, name, ta=False, tb=False, tm=512, tn=512, tk=None, out_dtype=F32, pre_a=None, epi=None,
       extras=()):
    M, K = (a.shape[1], a.shape[0]) if ta else a.shape
    N = b.shape[0] if tb else b.shape[1]
    tm, tn = min(tm, M), min(tn, N)
    tk = K if tk is None else min(tk, K)
    assert M % tm == 0 and N % tn == 0 and K % tk == 0, (name, M, N, K, tm, tn, tk)
    nk = K // tk
    ne = len(extras)
    a_spec = BS((tk, tm), lambda i, j, k: (k, i)) if ta else BS((tm, tk), lambda i, j, k: (i, k))
    b_spec = BS((tn, tk), lambda i, j, k: (j, k)) if tb else BS((tk, tn), lambda i, j, k: (k, j))
    o_spec = BS((tm, tn), lambda i, j, k: (i, j))
    dims = (((0 if ta else 1,), (1 if tb else 0,)), ((), ()))

    def body(a_ref, b_ref, *rest):
        e_refs, o_ref = rest[:ne], rest[ne]
        av = a_ref[...]
        if pre_a is not None:
            av = pre_a(av)
        p = lax.dot_general(_bf(av), _bf(b_ref[...]), dims, preferred_element_type=F32)

        def finish(acc):
            res = epi(acc, *[e[...] for e in e_refs]) if epi is not None else acc
            o_ref[...] = res.astype(o_ref.dtype)

        if nk == 1:
            finish(p)
        else:
            acc_ref = rest[ne + 1]
            k = pl.program_id(2)

            @pl.when(k == 0)
            def _():
                acc_ref[...] = p

            @pl.when(k > 0)
            def _():
                acc_ref[...] += p

            @pl.when(k == nk - 1)
            def _():
                finish(acc_ref[...])

    return pl.pallas_call(
        body, grid=(M // tm, N // tn, nk), in_specs=[a_spec, b_spec] + [o_spec] * ne, out_specs=o_spec,
        out_shape=SDS((M, N), out_dtype),
        scratch_shapes=[pltpu.VMEM((tm, tn), F32)] if nk > 1 else [],
        name=name, compiler_params=_cp(("parallel", "parallel", "arbitrary")))(a, b, *extras)


def _relu(acc):
    return jnp.maximum(acc, 0.0)


def _add(acc, res):
    return acc + res


def _sq(av):
    af = av.astype(F32)
    return af * af


def _times_2r(acc, r):
    return acc * (2.0 * r.astype(F32))


def mlp_fwd(x, g, w1, w2, tag):
    h = rms_fwd(x, g, f"{tag}_rms")
    r = mm(h, w1, name=f"{tag}_up", tn=1024, out_dtype=BF16, epi=_relu)
    xn = mm(r, w2, name=f"{tag}_down", pre_a=_sq, epi=_add, extras=(x,))
    return xn, (h, r)


def mlp_bwd(x, g, w1, w2, saved, dx, dxb, tag):
    h, r = saved
    da = mm(dxb, w2, name=f"{tag}_dact", tb=True, tn=1024, out_dtype=BF16, epi=_times_2r, extras=(r,))
    dw2 = mm(r, dxb, name=f"{tag}_dw2", ta=True, pre_a=_sq)
    dw1 = mm(h, da, name=f"{tag}_dw1", ta=True)
    dh = mm(da, w1, name=f"{tag}_dh", tb=True)
    dx, dxb, dg = rms_bwd(x, g, dh, dx, f"{tag}_rmsb")
    return dx, dxb, dg, dw1, dw2


def _conv_taps(x, S):
    t = lax.broadcasted_iota(jnp.int32, x.shape, 0)
    taps = []
    for j in range(GDN_CONV):
        sh = j - GDN_CONV // 2
        xs = x if sh == 0 else pltpu.roll(x, (-sh) % S, 0)
        taps.append(jnp.where((t + sh >= 0) & (t + sh < S), xs, 0.0))
    return taps


def _qkv_scale(c):
    is_norm = c < 2 * GDN_H
    scale = jnp.where(c < GDN_H, GDN_DK ** -0.5, 1.0)
    return is_norm, scale


def gdn_pre_fwd(proj, convw, name):
    S = proj.shape[0]

    def body(p_ref, w_ref, o_ref):
        c = pl.program_id(0)
        x = p_ref[...]
        w = w_ref[...]
        y = jnp.zeros_like(x)
        for j, xs in enumerate(_conv_taps(x, S)):
            y = y + w[j:j + 1, :] * xs
        t = y * _sigmoid(y)
        is_norm, scale = _qkv_scale(c)
        r = lax.rsqrt(jnp.sum(t * t, axis=-1, keepdims=True) + 1e-6)
        o_ref[...] = jnp.where(is_norm, t * r * scale, t)

    return pl.pallas_call(
        body, grid=(GDN_QKV // LANES,),
        in_specs=[BS((S, LANES), lambda c: (0, c)), BS((8, LANES), lambda c: (0, c))],
        out_specs=BS((S, LANES), lambda c: (0, c)),
        out_shape=SDS((S, GDN_QKV), F32), name=name, compiler_params=_cp(("parallel",)))(proj, convw)


def gdn_pre_bwd(proj, convw, dqkv, name):
    S = proj.shape[0]

    def body(p_ref, w_ref, d_ref, dp_ref, dw_ref):
        c = pl.program_id(0)
        x = p_ref[...]
        w = w_ref[...]
        taps = _conv_taps(x, S)
        y = jnp.zeros_like(x)
        for j, xs in enumerate(taps):
            y = y + w[j:j + 1, :] * xs
        sg = _sigmoid(y)
        t = y * sg
        is_norm, scale = _qkv_scale(c)
        dout = d_ref[0, 0] + d_ref[1, 0]
        r = lax.rsqrt(jnp.sum(t * t, axis=-1, keepdims=True) + 1e-6)
        n = t * r
        dn = dout * scale
        dt_norm = r * (dn - n * jnp.sum(dn * n, axis=-1, keepdims=True))
        dt = jnp.where(is_norm, dt_norm, dout)
        dy = dt * (sg * (1.0 + y * (1.0 - sg)))
        row = lax.broadcasted_iota(jnp.int32, (8, LANES), 0)
        dw = jnp.zeros((8, LANES), F32)
        for j, xs in enumerate(taps):
            dw = dw + jnp.where(row == j, jnp.sum(dy * xs, axis=0, keepdims=True), 0.0)
        dw_ref[...] = dw
        tt = lax.broadcasted_iota(jnp.int32, x.shape, 0)
        dx = jnp.zeros_like(x)
        for j in range(GDN_CONV):
            sh = j - GDN_CONV // 2
            ds = dy if sh == 0 else pltpu.roll(dy, sh % S, 0)
            dx = dx + w[j:j + 1, :] * jnp.where((tt - sh >= 0) & (tt - sh < S), ds, 0.0)
        dp_ref[...] = dx.astype(BF16)

    return pl.pallas_call(
        body, grid=(GDN_QKV // LANES,),
        in_specs=[BS((S, LANES), lambda c: (0, c)), BS((8, LANES), lambda c: (0, c)),
                  BS((2, 1, S, LANES), lambda c: (0, c // GDN_H, 0, c % GDN_H))],
        out_specs=[BS((S, LANES), lambda c: (0, c)), BS((8, LANES), lambda c: (0, c))],
        out_shape=[SDS((S, GDN_QKV), BF16), SDS((8, GDN_QKV), F32)],
        name=name, compiler_params=_cp(("parallel",)))(proj, convw, dqkv)


def _chunk_sum_matrix(n, upper):
    i = lax.broadcasted_iota(jnp.int32, (n, n), 0)
    j = lax.broadcasted_iota(jnp.int32, (n, n), 1)
    same = (i // GDN_C) == (j // GDN_C)
    tri = (i <= j) if upper else (i >= j)
    return jnp.where(same & tri, 1.0, 0.0).astype(F32)


def _gate_lanes(shape):
    lane = lax.broadcasted_iota(jnp.int32, shape, 1)
    return lane < GDN_H, (lane >= GDN_H) & (lane < 2 * GDN_H), (lane >= 2 * GDN_H) & (lane < 4 * GDN_H)


def gdn_gate_fwd(proj, prm, name):
    S = proj.shape[0]
    tm = min(512, S)
    ct = GDN_INP // LANES - 1

    def body(p_ref, prm_ref, o_ref):
        ab = p_ref[...]
        a_log = prm_ref[0:1, :]
        dtb = prm_ref[1:2, :]
        z = ab + dtb
        sp = jnp.maximum(z, 0.0) + jnp.log(1.0 + jnp.exp(-jnp.abs(z)))
        g = -jnp.exp(a_log) * sp
        is_f, is_b, is_beta = _gate_lanes(ab.shape)
        gf = _dot(_chunk_sum_matrix(tm, False), jnp.where(is_f, g, 0.0), HI)
        gbk = _dot(_chunk_sum_matrix(tm, True), jnp.where(is_b, g, 0.0), HI)
        o_ref[...] = gf + gbk + jnp.where(is_beta, _sigmoid(ab), 0.0)

    return pl.pallas_call(
        body, grid=(S // tm,),
        in_specs=[BS((tm, LANES), lambda i: (i, ct)), BS((8, LANES), lambda i: (0, 0))],
        out_specs=BS((tm, LANES), lambda i: (i, 0)),
        out_shape=SDS((S, LANES), F32), name=name, compiler_params=_cp(("parallel",)))(proj, prm)


def gdn_gate_bwd(proj, prm, dgb, name):
    S = proj.shape[0]
    tm = min(512, S)
    ct = GDN_INP // LANES - 1

    def body(p_ref, prm_ref, d_ref, dab_ref, dprm_ref):
        i = pl.program_id(0)
        ab = p_ref[...]
        a_log = prm_ref[0:1, :]
        dtb = prm_ref[1:2, :]
        z = ab + dtb
        sp = jnp.maximum(z, 0.0) + jnp.log(1.0 + jnp.exp(-jnp.abs(z)))
        ea = jnp.exp(a_log)
        g = -ea * sp
        is_f, is_b, is_beta = _gate_lanes(ab.shape)
        d = d_ref[...]
        dg = (_dot_tn(_chunk_sum_matrix(tm, False), jnp.where(is_f, d, 0.0), HI)
              + _dot_tn(_chunk_sum_matrix(tm, True), jnp.where(is_b, d, 0.0), HI))
        da = dg * (-ea) * _sigmoid(z)
        beta = _sigmoid(ab)
        dab_ref[...] = jnp.where(is_beta, d * beta * (1.0 - beta), da).astype(BF16)
        row = lax.broadcasted_iota(jnp.int32, (8, LANES), 0)
        part = (jnp.where(row == 0, jnp.sum(dg * g, axis=0, keepdims=True), 0.0)
                + jnp.where(row == 1, jnp.sum(da, axis=0, keepdims=True), 0.0))

        @pl.when(i == 0)
        def _():
            dprm_ref[...] = part

        @pl.when(i > 0)
        def _():
            dprm_ref[...] += part

    return pl.pallas_call(
        body, grid=(S // tm,),
        in_specs=[BS((tm, LANES), lambda i: (i, ct)), BS((8, LANES), lambda i: (0, 0)), BS((tm, LANES), lambda i: (i, 0))],
        out_specs=[BS((tm, LANES), lambda i: (i, 0)), BS((8, LANES), lambda i: (0, 0))],
        out_shape=[SDS((S, LANES), BF16), SDS((8, LANES), F32)],
        name=name, compiler_params=_cp(("arbitrary",)))(proj, prm, dgb)


def _tri_masks(d):
    i = lax.broadcasted_iota(jnp.int32, (GDN_C, GDN_C), 0)
    j = lax.broadcasted_iota(jnp.int32, (GDN_C, GDN_C), 1)
    s = (i - j) * (1 - 2 * d)
    return s >= 0, s > 0


def _split(a):
    hi = _bf(a)
    return hi, _bf(a - hi.astype(F32))


def _dot3(a, b):
    return _dot(a[0], b[0]) + (_dot(a[0], b[1]) + _dot(a[1], b[0]))


def _inv_unit_tri_many(mats):
    i = lax.broadcasted_iota(jnp.int32, mats[0].shape, 0)
    j = lax.broadcasted_iota(jnp.int32, mats[0].shape, 1)
    eye = jnp.where(i == j, 1.0, 0.0)
    ms = [-a for a in mats]
    ps = [eye + m for m in ms]
    for _ in range(int(math.log2(GDN_C)) - 1):
        sp = [_split(m) for m in ms]
        ms = [_dot3(s, s) for s in sp]
        sp = [_split(m) for m in ms]
        pp = [_split(p) for p in ps]
        ps = [p + _dot3(a, b) for p, a, b in zip(ps, pp, sp)]
    return ps


def _lane_col(x, lane_idx):
    lane = lax.broadcasted_iota(jnp.int32, x.shape, 1)
    return jnp.sum(jnp.where(lane == lane_idx, x, 0.0), axis=1, keepdims=True)


def _chunk_gates(gb_ref, grow_ref, hh, ci, d, head):
    gbv = gb_ref[ci * GDN_C:(ci + 1) * GDN_C, :]
    gcol = _lane_col(gbv, d * GDN_H + head)
    bcol = _lane_col(gbv, 2 * GDN_H + d * GDN_H + head)
    glast = jnp.where(d == 0, gcol[GDN_C - 1:GDN_C, :], gcol[0:1, :])
    return gcol, bcol, grow_ref[hh, ci:ci + 1, :], glast


def _chunk_base(q, k, gcol, grow, bcol, glast, d):
    incl, strict = _tri_masks(d)
    decay = jnp.where(incl, jnp.exp(jnp.where(incl, gcol - grow, 0.0)), 0.0)
    kb = k * bcol
    kk = _dot_nt(_bf(kb), _bf(k))
    qk = _dot_nt(_bf(q), _bf(k))
    eg = jnp.exp(gcol)
    ek = jnp.exp(glast - gcol)
    return dict(incl=incl, strict=strict, decay=decay, kb=kb, kk=kk, qk=qk, eg=eg, ek=ek, q_dec=q * eg, k_dec=k * ek,
                bcol=bcol, glast=glast)


def _block_terms(q_ref, k_ref, v_ref, gb_ref, grow_ref, d, h):
    keys = [(hh, ci) for hh in range(GDN_HP) for ci in range(GDN_GC)]
    ts = []
    for hh, ci in keys:
        rows = slice(ci * GDN_C, (ci + 1) * GDN_C)
        cols = slice(hh * GDN_DK, (hh + 1) * GDN_DK)
        gcol, bcol, grow_v, glast = _chunk_gates(gb_ref, grow_ref, hh, ci, d, h * GDN_HP + hh)
        t = _chunk_base(q_ref[rows, cols], k_ref[rows, cols], gcol, grow_v, bcol, glast, d)
        t["v"] = v_ref[rows, cols]
        ts.append(t)
    tinvs = _inv_unit_tri_many([jnp.where(t["strict"], t["kk"] * t["decay"], 0.0) for t in ts])
    sp = [_split(x) for x in tinvs]
    us = [_dot3(s, _split(t["v"] * t["bcol"])) for s, t in zip(sp, ts)]
    ws = [_dot3(s, _split(t["kb"] * t["eg"])) for s, t in zip(sp, ts)]
    for t, tinv, u, w in zip(ts, tinvs, us, ws):
        t.update(tinv=tinv, u=u, w=w)
    return keys, ts


def _gdn_specs(S, nblk, order):
    R = GDN_GC * GDN_C
    wd = GDN_HP * GDN_DK
    hb = GDN_H // GDN_HP

    def qkv_spec(part):
        return BS((R, wd), lambda d, h, n: (order(d, n), part * hb + h))

    gb_spec = BS((R, LANES), lambda d, h, n: (order(d, n), 0))
    grow_spec = BS((GDN_HP, GDN_GC, GDN_C), lambda d, h, n: (d * hb + h, order(d, n), 0))
    st_spec = BS((1, GDN_HP, GDN_GC, GDN_DK, GDN_DK), lambda d, h, n: (d, h, order(d, n), 0, 0))
    return qkv_spec, gb_spec, grow_spec, st_spec


def _lane_row(x):
    return jnp.broadcast_to(x, (1, LANES))


def gdn_scan_fwd(qkv, gb, grow, name):
    S = qkv.shape[0]
    R = GDN_GC * GDN_C
    nblk = S // R
    nc = S // GDN_C
    wd = GDN_HP * GDN_DK
    heads = range(GDN_HP)

    def order(d, n):
        return n + d * (nblk - 1 - 2 * n)

    qkv_spec, gb_spec, grow_spec, st_spec = _gdn_specs(S, nblk, order)

    def body(q_ref, k_ref, v_ref, gb_ref, grow_ref, o_ref, st_ref, s_scr, u_scr, w_scr, qd_scr, kd_scr, in_scr, egl_scr):
        d = pl.program_id(0)
        h = pl.program_id(1)
        n = pl.program_id(2)

        @pl.when(n == 0)
        def _():
            s_scr[...] = jnp.zeros_like(s_scr)

        keys, ts = _block_terms(q_ref, k_ref, v_ref, gb_ref, grow_ref, d, h)
        for (hh, ci), t in zip(keys, ts):
            u_scr[hh, ci] = t["u"]
            w_scr[hh, ci] = _bf(t["w"])
            qd_scr[hh, ci] = _bf(t["q_dec"])
            kd_scr[hh, ci] = _bf(t["k_dec"])
            in_scr[hh, ci] = _bf(jnp.where(t["incl"], t["qk"] * t["decay"], 0.0))
            egl_scr[hh, ci] = _lane_row(jnp.exp(t["glast"]))

        def chunk(cc, carry):
            ci = cc + d * (GDN_GC - 1 - 2 * cc)
            rows = pl.ds(pl.multiple_of(ci * GDN_C, GDN_C), GDN_C)
            sts = [s_scr[hh] for hh in heads]
            for hh in heads:
                st_ref[0, hh, ci] = sts[hh]
            sbs = [_bf(st) for st in sts]
            vns = [_bf(u_scr[hh, ci] - _dot(w_scr[hh, ci], sbs[hh])) for hh in heads]
            for hh in heads:
                s_scr[hh] = sts[hh] * egl_scr[hh, ci] + _dot_tn(kd_scr[hh, ci], vns[hh])
            for hh in heads:
                o_ref[0, rows, hh * GDN_DK:(hh + 1) * GDN_DK] = _dot(qd_scr[hh, ci], sbs[hh]) + _dot(in_scr[hh, ci], vns[hh])
            return carry

        lax.fori_loop(0, GDN_GC, chunk, 0)

    blk = (GDN_HP, GDN_GC, GDN_C, GDN_DK)
    return pl.pallas_call(
        body, grid=(2, GDN_H // GDN_HP, nblk),
        in_specs=[qkv_spec(0), qkv_spec(1), qkv_spec(2), gb_spec, grow_spec],
        out_specs=[BS((1, R, wd), lambda d, h, n: (d, order(d, n), h)), st_spec],
        out_shape=[SDS((2, S, GDN_H * GDN_DK), F32), SDS((2, GDN_H, nc, GDN_DK, GDN_DK), F32)],
        scratch_shapes=[pltpu.VMEM((GDN_HP, GDN_DK, GDN_DK), F32), pltpu.VMEM(blk, F32), pltpu.VMEM(blk, BF16),
                        pltpu.VMEM(blk, BF16), pltpu.VMEM(blk, BF16), pltpu.VMEM((GDN_HP, GDN_GC, GDN_C, GDN_C), BF16),
                        pltpu.VMEM((GDN_HP, GDN_GC, 1, LANES), F32)],
        name=name, compiler_params=_cp(("parallel", "parallel", "arbitrary")))(qkv, qkv, qkv, gb, grow)


def gdn_scan_bwd(qkv, gb, grow, states, do, name):
    S = qkv.shape[0]
    R = GDN_GC * GDN_C
    nblk = S // R
    wd = GDN_HP * GDN_DK
    heads = range(GDN_HP)

    def order(d, n):
        return (nblk - 1 - n) - d * (nblk - 1 - 2 * n)

    qkv_spec, gb_spec, grow_spec, st_spec = _gdn_specs(S, nblk, order)

    def body(q_ref, k_ref, v_ref, gb_ref, grow_ref, st_ref, do_ref, dqkv_ref, dgate_ref,
             ds_scr, w_scr, kd_scr, dv1_scr, qtdo_scr, egl_scr, dsin_scr, dvn_scr, sdot_scr):
        d = pl.program_id(0)
        h = pl.program_id(1)
        n = pl.program_id(2)

        @pl.when(n == 0)
        def _():
            ds_scr[...] = jnp.zeros_like(ds_scr)

        keys, ts = _block_terms(q_ref, k_ref, v_ref, gb_ref, grow_ref, d, h)
        for (hh, ci), t in zip(keys, ts):
            rows = slice(ci * GDN_C, (ci + 1) * GDN_C)
            t["wb"] = _bf(t["w"])
            t["dob"] = _bf(do_ref[rows, hh * GDN_DK:(hh + 1) * GDN_DK])
            t["sb"] = _bf(st_ref[0, hh, ci])
        for (hh, ci), t in zip(keys, ts):
            t["vnb"] = _bf(t["u"] - _dot(t["wb"], t["sb"]))
            w_scr[hh, ci] = t["wb"]
            kd_scr[hh, ci] = _bf(t["k_dec"])
            dv1_scr[hh, ci] = _dot_tn(_bf(jnp.where(t["incl"], t["qk"] * t["decay"], 0.0)), t["dob"])
            qtdo_scr[hh, ci] = _dot_tn(_bf(t["q_dec"]), t["dob"])
            egl_scr[hh, ci] = _lane_row(jnp.exp(t["glast"]))

        def chunk(cc, carry):
            ci = (GDN_GC - 1 - cc) - d * (GDN_GC - 1 - 2 * cc)
            dsns = [ds_scr[hh] for hh in heads]
            dsbs = [_bf(x) for x in dsns]
            dvns = [dv1_scr[hh, ci] + _dot(kd_scr[hh, ci], dsbs[hh]) for hh in heads]
            for hh in heads:
                ds_scr[hh] = qtdo_scr[hh, ci] + egl_scr[hh, ci] * dsns[hh] - _dot_tn(w_scr[hh, ci], _bf(dvns[hh]))
            for hh in heads:
                dsin_scr[hh, ci] = dsbs[hh]
                dvn_scr[hh, ci] = dvns[hh]
                sd = jnp.sum(jnp.sum(st_ref[0, hh, ci] * dsns[hh], axis=1, keepdims=True), axis=0, keepdims=True)
                sdot_scr[hh, ci] = _lane_row(sd)
            return carry

        lax.fori_loop(0, GDN_GC, chunk, 0)

        for (hh, ci), t in zip(keys, ts):
            t["d_vnew"] = dvn_scr[hh, ci]
            t["dvb"] = _bf(t["d_vnew"])
            t["dsb"] = dsin_scr[hh, ci]
        for t in ts:
            t["d_intra"] = jnp.where(t["incl"], _dot_nt(t["dob"], t["vnb"]), 0.0)
            t["d_qdec"] = _dot_nt(t["dob"], t["sb"])
            t["d_kdec"] = _dot_nt(t["vnb"], t["dsb"])
            t["dw"] = -_dot_nt(t["dvb"], t["sb"])
        for t in ts:
            tts = _split(t["tinv"].T)
            t["d_ru"] = _dot3(tts, _split(t["d_vnew"]))
            t["d_rw"] = _dot3(tts, _split(t["dw"]))
        for t in ts:
            t["da"] = -jnp.where(t["strict"], _dot_nt(_bf(t["d_ru"]), _bf(t["u"])) + _dot_nt(_bf(t["d_rw"]), t["wb"]), 0.0)
        for (hh, ci), t in zip(keys, ts):
            rows = slice(ci * GDN_C, (ci + 1) * GDN_C)
            cols = slice(hh * GDN_DK, (hh + 1) * GDN_DK)
            q, k, v = q_ref[rows, cols], k_ref[rows, cols], t["v"]
            decay, kb, eg, ek, bcol = t["decay"], t["kb"], t["eg"], t["ek"], t["bcol"]
            d_ru, d_rw, da, d_intra, d_qdec, d_kdec = t["d_ru"], t["d_rw"], t["da"], t["d_intra"], t["d_qdec"], t["d_kdec"]
            kbf, qbf = _bf(k), _bf(q)
            dgl = egl_scr[hh, ci][:, 0:1] * sdot_scr[hh, ci][:, 0:1]
            dv = d_ru * bcol
            dbeta = jnp.sum(d_ru * v, axis=1, keepdims=True)
            dkb = d_rw * eg
            dg = jnp.sum(d_rw * kb, axis=1, keepdims=True) * eg
            dkk = _bf(da * decay)
            dqk = _bf(d_intra * decay)
            dkb = dkb + _dot(dkk, kbf)
            dk = _dot_tn(dkk, _bf(kb)) + _dot_tn(dqk, qbf)
            dq = _dot(dqk, kbf) + d_qdec * eg
            dd = (da * t["kk"] + d_intra * t["qk"]) * decay
            dg = dg + jnp.sum(dd, axis=1, keepdims=True) - jnp.sum(dd.T, axis=1, keepdims=True)
            dg = dg + jnp.sum(d_qdec * t["q_dec"], axis=1, keepdims=True)
            dk = dk + d_kdec * ek
            ee = jnp.sum(d_kdec * t["k_dec"], axis=1, keepdims=True)
            dg = dg - ee
            dgl = dgl + jnp.sum(ee, axis=0, keepdims=True)
            dk = dk + dkb * bcol
            dbeta = dbeta + jnp.sum(dkb * k, axis=1, keepdims=True)
            ridx = lax.broadcasted_iota(jnp.int32, (GDN_C, 1), 0)
            dg = dg + jnp.where(ridx == (GDN_C - 1) * (1 - d), dgl, 0.0)
            dqkv_ref[0, 0, rows, cols] = dq
            dqkv_ref[0, 1, rows, cols] = dk
            dqkv_ref[0, 2, rows, cols] = dv
            lane2 = lax.broadcasted_iota(jnp.int32, (GDN_C, 2), 1)
            dgate_ref[0, hh, rows, :] = jnp.where(lane2 == 0, dg, dbeta)

    blk = (GDN_HP, GDN_GC, GDN_C, GDN_DK)
    sq = (GDN_HP, GDN_GC, GDN_DK, GDN_DK)
    row = (GDN_HP, GDN_GC, 1, LANES)
    return pl.pallas_call(
        body, grid=(2, GDN_H // GDN_HP, nblk),
        in_specs=[qkv_spec(0), qkv_spec(1), qkv_spec(2), gb_spec, grow_spec, st_spec,
                  BS((R, wd), lambda d, h, n: (order(d, n), h))],
        out_specs=[BS((1, 3, R, wd), lambda d, h, n: (d, 0, order(d, n), h)),
                   BS((1, GDN_HP, R, 2), lambda d, h, n: (d, h, order(d, n), 0))],
        out_shape=[SDS((2, 3, S, GDN_H * GDN_DK), F32), SDS((2, GDN_H, S, 2), F32)],
        scratch_shapes=[pltpu.VMEM((GDN_HP, GDN_DK, GDN_DK), F32), pltpu.VMEM(blk, BF16), pltpu.VMEM(blk, BF16),
                        pltpu.VMEM(blk, F32), pltpu.VMEM(sq, F32), pltpu.VMEM(row, F32), pltpu.VMEM(sq, BF16),
                        pltpu.VMEM(blk, F32), pltpu.VMEM(row, F32)],
        name=name, compiler_params=_cp(("parallel", "parallel", "arbitrary")))(qkv, qkv, qkv, gb, grow, states, do)


def gdn_post_fwd(o2, proj, nw, name):
    S = proj.shape[0]
    tm = min(512, S)
    zoff = GDN_QKV // LANES

    def body(o_ref, z_ref, nw_ref, y_ref):
        o = o_ref[0] + o_ref[1]
        z = z_ref[...]
        r = lax.rsqrt(jnp.mean(o * o, axis=-1, keepdims=True) + RMS_EPS)
        y_ref[...] = (o * r * nw_ref[...] * (z * _sigmoid(z))).astype(BF16)

    return pl.pallas_call(
        body, grid=(S // tm, GDN_H),
        in_specs=[BS((2, tm, LANES), lambda i, h: (0, i, h)), BS((tm, LANES), lambda i, h: (i, zoff + h)),
                  BS((1, LANES), lambda i, h: (0, 0))],
        out_specs=BS((tm, LANES), lambda i, h: (i, h)),
        out_shape=SDS((S, GDN_H * GDN_DK), BF16), name=name, compiler_params=_cp(("parallel", "parallel")))(o2, proj, nw)


def gdn_post_bwd(o2, proj, nw, dy, name):
    S = proj.shape[0]
    tm = min(512, S)
    zoff = GDN_QKV // LANES

    def body(o_ref, z_ref, nw_ref, dy_ref, do_ref, dz_ref, dnw_ref):
        first = (pl.program_id(0) == 0) & (pl.program_id(1) == 0)
        o = o_ref[0] + o_ref[1]
        z = z_ref[...]
        nwv = nw_ref[...]
        dyv = dy_ref[...]
        r = lax.rsqrt(jnp.mean(o * o, axis=-1, keepdims=True) + RMS_EPS)
        n = o * r
        sg = _sigmoid(z)
        sz = z * sg
        dz_ref[...] = (dyv * n * nwv * (sg * (1.0 + z * (1.0 - sg)))).astype(BF16)
        dn = dyv * nwv * sz
        do_ref[...] = r * (dn - n * jnp.mean(dn * n, axis=-1, keepdims=True))
        part = jnp.sum(dyv * n * sz, axis=0, keepdims=True)

        @pl.when(first)
        def _():
            dnw_ref[...] = part

        @pl.when(jnp.logical_not(first))
        def _():
            dnw_ref[...] += part

    blk = BS((tm, LANES), lambda i, h: (i, h))
    return pl.pallas_call(
        body, grid=(S // tm, GDN_H),
        in_specs=[BS((2, tm, LANES), lambda i, h: (0, i, h)), BS((tm, LANES), lambda i, h: (i, zoff + h)),
                  BS((1, LANES), lambda i, h: (0, 0)), blk],
        out_specs=[blk, blk, BS((1, LANES), lambda i, h: (0, 0))],
        out_shape=[SDS((S, GDN_H * GDN_DK), F32), SDS((S, GDN_H * GDN_DK), BF16), SDS((1, LANES), F32)],
        name=name, compiler_params=_cp(("arbitrary", "arbitrary")))(o2, proj, nw, dy)


def _gate_prm(a_log, dt_bias):
    z = jnp.zeros((8, LANES), F32)
    z = z.at[0, :2 * GDN_H].set(a_log.reshape(-1))
    return z.at[1, :2 * GDN_H].set(dt_bias.reshape(-1))


def gdn_fwd(x, g, w_all, convw, a_log, dt_bias, nw, w_out, tag):
    S = x.shape[0]
    h = rms_fwd(x, g, f"{tag}_rms")
    proj = mm(h, w_all, name=f"{tag}_proj", tn=1408)
    qkv = gdn_pre_fwd(proj, convw, f"{tag}_pre")
    prm = _gate_prm(a_log, dt_bias)
    gb = gdn_gate_fwd(proj, prm, f"{tag}_gate")
    grow = gb[:, :2 * GDN_H].T.reshape(2 * GDN_H, S // GDN_C, GDN_C)
    o2, states = gdn_scan_fwd(qkv, gb, grow, f"{tag}_scan")
    y = gdn_post_fwd(o2, proj, nw, f"{tag}_post")
    xn = mm(y, w_out, name=f"{tag}_out", epi=_add, extras=(x,))
    return xn, (h, proj, qkv, prm, gb, grow, o2, states, y)


def gdn_bwd(x, g, w_all, convw, nw, w_out, saved, dx, dxb, tag):
    S = x.shape[0]
    h, proj, qkv, prm, gb, grow, o2, states, y = saved
    dw_out = mm(y, dxb, name=f"{tag}_dwout", ta=True)
    dy = mm(dxb, w_out, name=f"{tag}_dy", tb=True)
    do, dz, dnw = gdn_post_bwd(o2, proj, nw, dy, f"{tag}_postb")
    dqkv, dgate = gdn_scan_bwd(qkv, gb, grow, states, do, f"{tag}_scanb")
    dgb = jnp.transpose(dgate, (2, 3, 0, 1)).reshape(S, 4 * GDN_H)
    dgb = jnp.pad(dgb, ((0, 0), (0, LANES - 4 * GDN_H)))
    dab, dprm = gdn_gate_bwd(proj, prm, dgb, f"{tag}_gateb")
    dpq, dconvw = gdn_pre_bwd(proj, convw, dqkv, f"{tag}_preb")
    dproj = jnp.concatenate([dpq, dz, dab], axis=1)
    dw_all = mm(h, dproj, name=f"{tag}_dwin", ta=True, tn=384)
    dh = mm(dproj, w_all, name=f"{tag}_dh", tb=True, tk=1408)
    dx, dxb, dg = rms_bwd(x, g, dh, dx, f"{tag}_rmsb")
    da_log = dprm[0, :2 * GDN_H].reshape(2, GDN_H)
    ddt = dprm[1, :2 * GDN_H].reshape(2, GDN_H)
    return dx, dxb, dg, dw_all, dconvw, da_log, ddt, dnw, dw_out


def _rel_bucket_np(rel):
    nb = REL_BUCKETS // 2
    max_exact = nb // 2
    ret = np.where(rel > 0, nb, 0)
    n = np.abs(rel)
    nf = np.maximum(n, 1).astype(np.float32)
    large = max_exact + (np.log(nf / max_exact) / np.float32(math.log(REL_MAX_DIST / max_exact))
                         * (nb - max_exact)).astype(np.int32)
    large = np.minimum(large, nb - 1)
    return ret + np.where(n < max_exact, n, large)


def _dswa_qb(L):
    return min(256, L)


def _toeplitz(f, rows, cols):
    period = rows + cols
    e = jnp.pad(f, ((0, 0), (0, period - f.shape[1])))
    y = jnp.tile(e, (1, rows))[:, :rows * (period - 1)]
    return y.reshape(f.shape[0], rows, period - 1)[:, :, :cols]


def _bias_mats(rel_table, gi, L):
    _, dil = DSWA_CFG[gi]
    qb = _dswa_qb(L)
    offs = np.arange(-DSWA_HALF, DSWA_HALF + 1)
    onehot = jnp.asarray(np.eye(REL_BUCKETS, dtype=np.float32)[_rel_bucket_np(offs * dil)])
    f = jnp.dot(onehot, rel_table, precision=HI)[:, gi * DSWA_HG:(gi + 1) * DSWA_HG].T
    bias = _toeplitz(f, qb, qb + 2 * DSWA_HALF)
    bias_t = jnp.transpose(_toeplitz(f[:, ::-1], qb, qb + 2 * DSWA_HALF), (0, 2, 1))
    return bias, bias_t


def _win_specs(qb, L, width, major):
    m = qb // DSWA_HALF
    last = L // DSWA_HALF - 1

    def prev(*ids):
        s, b = major(*ids)
        return (s, jnp.maximum(b * m - 1, 0), 0)

    def cur(*ids):
        s, b = major(*ids)
        return (s, b, 0)

    def nxt(*ids):
        s, b = major(*ids)
        return (s, jnp.minimum((b + 1) * m, last), 0)

    return [BS((1, DSWA_HALF, width), prev), BS((1, qb, width), cur), BS((1, DSWA_HALF, width), nxt)]


def _window(p_ref, c_ref, n_ref):
    return jnp.concatenate([p_ref[0], c_ref[0], n_ref[0]], axis=0)


def _band_valid(qb, b, L, transposed):
    shape = (qb + 2 * DSWA_HALF, qb) if transposed else (qb, qb + 2 * DSWA_HALF)
    blk = lax.broadcasted_iota(jnp.int32, shape, 1 if transposed else 0)
    win = lax.broadcasted_iota(jnp.int32, shape, 0 if transposed else 1)
    off = win - DSWA_HALF - blk
    pos = b * qb - DSWA_HALF + win
    return (jnp.abs(off) <= DSWA_HALF) & (pos >= 0) & (pos < L)


def attn_fwd(q, k, v, bias, dil, name):
    NS, L, E = q.shape
    qb = _dswa_qb(L)

    def major(s, b):
        return s, b

    win = _win_specs(qb, L, E, major)

    def body(q_ref, kp, kc, kn, vp, vc, vn, b_ref, o_ref, lse_ref):
        b = pl.program_id(1)
        kw = _window(kp, kc, kn)
        vw = _window(vp, vc, vn)
        s = _dot_nt(q_ref[0], kw) * (E ** -0.5) + b_ref[0]
        s = jnp.where(_band_valid(qb, b, L, False), s, NEG_INF)
        m = jnp.max(s, axis=-1, keepdims=True)
        p = jnp.exp(s - m)
        l = jnp.sum(p, axis=-1, keepdims=True)
        lse_ref[0] = m + jnp.log(l)
        o_ref[0] = _dot(_bf(p / l), vw)

    return pl.pallas_call(
        body, grid=(NS, L // qb),
        in_specs=[win[1]] + win + win + [BS((1, qb, qb + 2 * DSWA_HALF), lambda s, b: (s // dil, 0, 0))],
        out_specs=[BS((1, qb, E), lambda s, b: (s, b, 0)), BS((1, qb, 1), lambda s, b: (s, b, 0))],
        out_shape=[SDS((NS, L, E), F32), SDS((NS, L, 1), F32)],
        name=name, compiler_params=_cp(("parallel", "parallel")))(q, k, k, k, v, v, v, bias)


def attn_bwd_q(q, k, v, bias, lse, do, dd, dil, name):
    NS, L, E = q.shape
    qb = _dswa_qb(L)
    nb = L // qb

    def major(h, r, b):
        return h * dil + r, b

    win = _win_specs(qb, L, E, major)
    col = BS((1, qb, 1), lambda h, r, b: (h * dil + r, b, 0))

    def body(q_ref, kp, kc, kn, vp, vc, vn, b_ref, lse_ref, do_ref, dd_ref, dq_ref, db_ref):
        b = pl.program_id(2)
        first = (pl.program_id(1) == 0) & (b == 0)
        kw = _window(kp, kc, kn)
        vw = _window(vp, vc, vn)
        valid = _band_valid(qb, b, L, False)
        s = _dot_nt(q_ref[0], kw) * (E ** -0.5) + b_ref[0]
        p = jnp.exp(jnp.where(valid, s - lse_ref[0], NEG_INF))
        ds = p * (_dot_nt(do_ref[0], vw) - dd_ref[0])
        dq_ref[0] = (_dot(_bf(ds), kw) * (E ** -0.5)).astype(BF16)

        @pl.when(first)
        def _():
            db_ref[0] = ds

        @pl.when(jnp.logical_not(first))
        def _():
            db_ref[0] += ds

    bspec = BS((1, qb, qb + 2 * DSWA_HALF), lambda h, r, b: (h, 0, 0))
    return pl.pallas_call(
        body, grid=(DSWA_HG, dil, nb),
        in_specs=[win[1]] + win + win + [bspec, col, win[1], col],
        out_specs=[win[1], bspec],
        out_shape=[SDS((NS, L, E), BF16), SDS((DSWA_HG, qb, qb + 2 * DSWA_HALF), F32)],
        name=name, compiler_params=_cp(("parallel", "arbitrary", "arbitrary")))(q, k, k, k, v, v, v, bias, lse, do, dd)


def attn_bwd_kv(q, k, v, bias_t, lse, do, dd, dil, name):
    NS, L, E = q.shape
    qb = _dswa_qb(L)

    def major(s, b):
        return s, b

    win = _win_specs(qb, L, E, major)
    wcol = _win_specs(qb, L, 1, major)

    def body(kc, vc, qp, qc, qn, dop, doc, don, lp, lc, ln, ddp, ddc, ddn, b_ref, dk_ref, dv_ref):
        b = pl.program_id(1)
        qw = _window(qp, qc, qn)
        dow = _window(dop, doc, don)
        lw = _window(lp, lc, ln)
        ddw = _window(ddp, ddc, ddn)
        valid = _band_valid(qb, b, L, True)
        s = _dot_nt(qw, kc[0]) * (E ** -0.5) + b_ref[0]
        p = jnp.exp(jnp.where(valid, s - lw, NEG_INF))
        dv_ref[0] = _dot_tn(_bf(p), dow).astype(BF16)
        ds = jnp.where(valid, p * (_dot_nt(dow, vc[0]) - ddw), 0.0)
        dk_ref[0] = (_dot_tn(_bf(ds), qw) * (E ** -0.5)).astype(BF16)

    return pl.pallas_call(
        body, grid=(NS, L // qb),
        in_specs=[win[1], win[1]] + win + win + wcol + wcol + [BS((1, qb + 2 * DSWA_HALF, qb), lambda s, b: (s // dil, 0, 0))],
        out_specs=[win[1], win[1]],
        out_shape=[SDS((NS, L, E), BF16), SDS((NS, L, E), BF16)],
        name=name, compiler_params=_cp(("parallel", "parallel")))(k, v, q, q, q, do, do, do, lse, lse, lse, dd, dd, dd, bias_t)


def _head_expand():
    i = lax.broadcasted_iota(jnp.int32, (LANES, DSWA_HG * DSWA_E), 0)
    j = lax.broadcasted_iota(jnp.int32, (LANES, DSWA_HG * DSWA_E), 1)
    return jnp.where(i == j // DSWA_E, 1.0, 0.0).astype(F32)


def _group_alphas(lse3):
    m = jnp.maximum(jnp.maximum(lse3[0], lse3[1]), lse3[2])
    e = [jnp.exp(t - m) for t in lse3]
    tot = e[0] + e[1] + e[2]
    return [t / tot for t in e]


def combine_fwd(o_raw, lse3, name):
    S = o_raw.shape[0]
    tm = min(512, S)
    gw = DSWA_HG * DSWA_E

    def body(o_ref, l_ref, y_ref):
        alphas = _group_alphas([l_ref[0], l_ref[1], l_ref[2]])
        ex = _head_expand()
        for gi in range(3):
            cols = slice(gi * gw, (gi + 1) * gw)
            y_ref[:, cols] = (o_ref[:, cols] * _dot(alphas[gi], ex, HI)).astype(BF16)

    return pl.pallas_call(
        body, grid=(S // tm,),
        in_specs=[BS((tm, DSWA_W), lambda i: (i, 0)), BS((3, tm, LANES), lambda i: (0, i, 0))],
        out_specs=BS((tm, DSWA_W), lambda i: (i, 0)),
        out_shape=SDS((S, DSWA_W), BF16), name=name, compiler_params=_cp(("parallel",)))(o_raw, lse3)


def combine_bwd(o_raw, lse3, dy, name):
    S = o_raw.shape[0]
    tm = min(512, S)
    gw = DSWA_HG * DSWA_E

    def body(o_ref, l_ref, dy_ref, do_ref, dd_ref):
        alphas = _group_alphas([l_ref[0], l_ref[1], l_ref[2]])
        ex = _head_expand()
        dal = []
        for gi in range(3):
            cols = slice(gi * gw, (gi + 1) * gw)
            dyv = dy_ref[:, cols]
            do_ref[:, cols] = (dyv * _dot(alphas[gi], ex, HI)).astype(BF16)
            dal.append(_dot_nt(o_ref[:, cols] * dyv, ex, HI))
        c = alphas[0] * dal[0] + alphas[1] * dal[1] + alphas[2] * dal[2]
        for gi in range(3):
            dd_ref[gi] = alphas[gi] * c

    return pl.pallas_call(
        body, grid=(S // tm,),
        in_specs=[BS((tm, DSWA_W), lambda i: (i, 0)), BS((3, tm, LANES), lambda i: (0, i, 0)), BS((tm, DSWA_W), lambda i: (i, 0))],
        out_specs=[BS((tm, DSWA_W), lambda i: (i, 0)), BS((3, tm, LANES), lambda i: (0, i, 0))],
        out_shape=[SDS((S, DSWA_W), BF16), SDS((3, S, LANES), F32)],
        name=name, compiler_params=_cp(("parallel",)))(o_raw, lse3, dy)


def _to_sub(t, dil):
    S, hg, wd = t.shape
    return jnp.transpose(t.reshape(S // dil, dil, hg, wd), (2, 1, 0, 3)).reshape(hg * dil, S // dil, wd)


def _from_sub(t, dil):
    ns, L, wd = t.shape
    hg = ns // dil
    return jnp.transpose(t.reshape(hg, dil, L, wd), (2, 1, 0, 3)).reshape(L * dil, hg, wd)


def dswa_fwd(x, g, w_in, w_out, rel_table, tag):
    S = x.shape[0]
    h = rms_fwd(x, g, f"{tag}_rms")
    qkv = mm(h, w_in, name=f"{tag}_qkv", tn=1152, out_dtype=BF16).reshape(S, 3, DSWA_HEADS, DSWA_E)
    subs, outs, lses = [], [], []
    for gi, (_, dil) in enumerate(DSWA_CFG):
        hs = slice(gi * DSWA_HG, (gi + 1) * DSWA_HG)
        qs, ks, vs = (_to_sub(qkv[:, i, hs], dil) for i in range(3))
        bias, bias_t = _bias_mats(rel_table, gi, S // dil)
        o, lse = attn_fwd(qs, ks, vs, bias, dil, f"{tag}_att{gi}")
        subs.append((qs, ks, vs, lse))
        outs.append(_from_sub(o, dil).reshape(S, DSWA_HG * DSWA_E))
        lses.append(jnp.pad(_from_sub(lse, dil).reshape(S, DSWA_HG), ((0, 0), (0, LANES - DSWA_HG))))
    o_raw = jnp.concatenate(outs, axis=1)
    lse3 = jnp.stack(lses)
    y = combine_fwd(o_raw, lse3, f"{tag}_comb")
    xn = mm(y, w_out, name=f"{tag}_out", epi=_add, extras=(x,))
    return xn, (h, subs, o_raw, lse3, y)


def dswa_bwd(x, g, w_in, w_out, rel_table, saved, dx, dxb, tag):
    S = x.shape[0]
    h, subs, o_raw, lse3, y = saved
    dw_out = mm(y, dxb, name=f"{tag}_dwout", ta=True, tm=384)
    dy = mm(dxb, w_out, name=f"{tag}_dy", tb=True, tn=384)
    do_raw, dd3 = combine_bwd(o_raw, lse3, dy, f"{tag}_combb")
    do_raw = do_raw.reshape(S, DSWA_HEADS, DSWA_E)
    dqkv = []
    drel = jnp.zeros_like(rel_table)
    for gi, (_, dil) in enumerate(DSWA_CFG):
        hs = slice(gi * DSWA_HG, (gi + 1) * DSWA_HG)
        qs, ks, vs, lse = subs[gi]
        dos = _to_sub(do_raw[:, hs], dil)
        dds = _to_sub(dd3[gi, :, :DSWA_HG, None], dil)
        (bias, bias_t), bias_vjp = jax.vjp(lambda tbl: _bias_mats(tbl, gi, S // dil), rel_table)
        dq, dbias = attn_bwd_q(qs, ks, vs, bias, lse, dos, dds, dil, f"{tag}_attq{gi}")
        dk, dv = attn_bwd_kv(qs, ks, vs, bias_t, lse, dos, dds, dil, f"{tag}_attkv{gi}")
        drel = drel + bias_vjp((dbias, jnp.zeros_like(bias_t)))[0]
        dqkv.append([_from_sub(t, dil) for t in (dq, dk, dv)])
    dqkv = jnp.stack([jnp.concatenate([dqkv[gi][i] for gi in range(3)], axis=1) for i in range(3)], axis=1)
    dqkv = dqkv.reshape(S, 3 * DSWA_W)
    dw_in = mm(h, dqkv, name=f"{tag}_dwin", ta=True, tn=384)
    dh = mm(dqkv, w_in, name=f"{tag}_dh", tb=True, tk=1152)
    dx, dxb, dg = rms_bwd(x, g, dh, dx, f"{tag}_rmsb")
    return dx, dxb, dg, dw_in, dw_out, drel


def adamw(w, g, m, v, name):
    shape = w.shape
    last = shape[-1]
    w2, g2, m2, v2 = (t.reshape(-1, last) for t in (w, g, m, v))
    rows = w2.shape[0]
    tr = rows
    if rows > 512:
        tr = next(t for t in (512, 256, 192, 128, 64, 8) if rows % t == 0)
    c1 = 1.0 / (1.0 - ADAM_B1 ** ADAM_STEP)
    c2 = 1.0 / (1.0 - ADAM_B2 ** ADAM_STEP)

    def body(w_ref, g_ref, m_ref, v_ref, d_ref, nm_ref, nv_ref):
        gv = g_ref[...]
        nm = ADAM_B1 * m_ref[...] + (1.0 - ADAM_B1) * gv
        nv = ADAM_B2 * v_ref[...] + (1.0 - ADAM_B2) * (gv * gv)
        nm_ref[...] = nm
        nv_ref[...] = nv
        d_ref[...] = -ADAM_LR * ((nm * c1) / (jnp.sqrt(nv * c2) + ADAM_EPS) + ADAM_WD * w_ref[...])

    spec = BS((tr, last), lambda i: (i, 0))
    outs = pl.pallas_call(
        body, grid=(rows // tr,), in_specs=[spec] * 4, out_specs=[spec] * 3,
        out_shape=[SDS((rows, last), F32)] * 3, name=name, compiler_params=_cp(("parallel",)))(w2, g2, m2, v2)
    return tuple(o.reshape(shape) for o in outs)


ANY = BS(memory_space=pl.ANY)
GATHER_CHUNKS = 5


def _place():
    x, y, c = lax.axis_index("x"), lax.axis_index("y"), lax.axis_index("c")
    chips = [(1 - x, y), (x, 1 - y), (1 - x, 1 - y)]
    return x, y, c, chips


def _rcopy(src, dst, ssem, rsem, dev):
    return pltpu.make_async_remote_copy(src_ref=src, dst_ref=dst, send_sem=ssem, recv_sem=rsem, device_id=dev,
                                        device_id_type=MESH)


def gather_weights(pack, convp):
    RP = pack.shape[0]
    half = RP // 2
    nch = GATHER_CHUNKS
    ch = half // nch
    assert ch * nch == half and ch % 16 == 0

    def body(p_ref, c_ref, full_ref, cfull_ref, ssem, rsem, fssem, frsem, cssem, crsem, lsem):
        x, y, c, chips = _place()
        jme = 2 * x + y
        sib = (x, y, 1 - c)

        def rows(hc, q):
            return pl.ds(hc * half + q * ch, ch)

        own = pltpu.make_async_copy(p_ref, full_ref.at[jme], lsem.at[0])
        own_c = pltpu.make_async_copy(c_ref, cfull_ref.at[jme], lsem.at[1])
        own.start()
        own_c.start()
        sends = []
        for r, (cx, cy) in enumerate(chips):
            for q in range(nch):
                cp = _rcopy(p_ref.at[rows(c, q)], full_ref.at[jme, rows(c, q)], ssem.at[r, q], rsem.at[r, q], (cx, cy, c))
                cp.start()
                sends.append(cp)
            cp = _rcopy(c_ref, cfull_ref.at[jme], cssem.at[r], crsem.at[r], (cx, cy, c))
            cp.start()
            sends.append(cp)
        for r, (cx, cy) in enumerate(chips):
            jr = 2 * cx + cy
            for q in range(nch):
                piece = full_ref.at[jr, rows(c, q)]
                _rcopy(piece, piece, ssem.at[r, q], rsem.at[r, q], (cx, cy, c)).wait_recv()
                cp = _rcopy(piece, piece, fssem.at[r, q], frsem.at[r, q], sib)
                cp.start()
                sends.append(cp)
            _rcopy(c_ref, cfull_ref.at[jr], cssem.at[r], crsem.at[r], (cx, cy, c)).wait_recv()
        for r, (cx, cy) in enumerate(chips):
            jr = 2 * cx + cy
            for q in range(nch):
                piece = full_ref.at[jr, rows(1 - c, q)]
                _rcopy(piece, piece, fssem.at[r, q], frsem.at[r, q], sib).wait_recv()
        for cp in sends:
            cp.wait_send()
        own.wait()
        own_c.wait()

    dma = pltpu.SemaphoreType.DMA
    return pl.pallas_call(
        body, in_specs=[ANY, ANY], out_specs=[ANY, ANY],
        out_shape=[SDS((4,) + pack.shape, pack.dtype), SDS((4,) + convp.shape, convp.dtype)],
        scratch_shapes=[dma((3, nch)), dma((3, nch)), dma((3, nch)), dma((3, nch)), dma((3,)), dma((3,)), dma((2,))],
        name="gather_weights", compiler_params=pltpu.CompilerParams(has_side_effects=True))(pack, convp)


def rs_sibling_exchange(gpack):
    _, RP, W = gpack.shape
    half = RP // 2

    def body(g_ref, r_ref, ssem, rsem):
        x, y, c, _ = _place()
        cps = [_rcopy(g_ref.at[j, pl.ds((1 - c) * half, half)], r_ref.at[j], ssem.at[j], rsem.at[j], (x, y, 1 - c))
               for j in range(4)]
        for cp in cps:
            cp.start()
        for cp in cps:
            cp.wait()

    dma = pltpu.SemaphoreType.DMA
    return pl.pallas_call(
        body, in_specs=[ANY], out_specs=ANY, out_shape=SDS((4, half, W), gpack.dtype),
        scratch_shapes=[dma((4,)), dma((4,))], name="rs_sibling_exchange",
        compiler_params=pltpu.CompilerParams(has_side_effects=True))(gpack)


def _rs_tile(half):
    return next(t for t in (656, 512, 328, 256, 128, 64, 8) if half % t == 0)


def rs_add_sibling(gpack, recv, cidx):
    _, RP, W = gpack.shape
    half = RP // 2
    tr = _rs_tile(half)
    nb = half // tr

    def body(c_ref, g_ref, r_ref, o_ref):
        o_ref[...] = g_ref[...] + r_ref[...]

    gs = pltpu.PrefetchScalarGridSpec(
        num_scalar_prefetch=1, grid=(4, nb),
        in_specs=[BS((1, tr, W), lambda j, i, c: (j, c[0] * nb + i, 0)), BS((1, tr, W), lambda j, i, c: (j, i, 0))],
        out_specs=BS((1, tr, W), lambda j, i, c: (j, i, 0)))
    return pl.pallas_call(body, grid_spec=gs, out_shape=SDS((4, half, W), F32), name="rs_add_sibling",
                          compiler_params=_cp(("parallel", "parallel")))(cidx, gpack, recv)


def rs_chip_exchange(part):
    _, half, W = part.shape

    def body(p_ref, r_ref, ssem, rsem, lsem):
        x, y, c, chips = _place()
        own = pltpu.make_async_copy(p_ref.at[2 * x + y], r_ref.at[3], lsem)
        own.start()
        cps = [_rcopy(p_ref.at[2 * cx + cy], r_ref.at[r], ssem.at[r], rsem.at[r], (cx, cy, c))
               for r, (cx, cy) in enumerate(chips)]
        for cp in cps:
            cp.start()
        for cp in cps:
            cp.wait()
        own.wait()

    dma = pltpu.SemaphoreType.DMA
    return pl.pallas_call(
        body, in_specs=[ANY], out_specs=ANY, out_shape=SDS((4, half, W), part.dtype),
        scratch_shapes=[dma((3,)), dma((3,)), dma], name="rs_chip_exchange",
        compiler_params=pltpu.CompilerParams(has_side_effects=True))(part)


def rs_add_chips(recv):
    _, half, W = recv.shape
    tr = _rs_tile(half)

    def body(r_ref, o_ref):
        o_ref[...] = ((r_ref[0] + r_ref[1]) + r_ref[2]) + r_ref[3]

    return pl.pallas_call(
        body, grid=(half // tr,), in_specs=[BS((4, tr, W), lambda i: (0, i, 0))], out_specs=BS((tr, W), lambda i: (i, 0)),
        out_shape=SDS((half, W), F32), name="rs_add_chips", compiler_params=_cp(("parallel",)))(recv)


def rs_sibling_share(fin):
    half, W = fin.shape

    def body(f_ref, o_ref, ssem, rsem, lsem):
        x, y, c, _ = _place()
        mine = o_ref.at[pl.ds(c * half, half)]
        own = pltpu.make_async_copy(f_ref, mine, lsem)
        cp = _rcopy(f_ref, mine, ssem, rsem, (x, y, 1 - c))
        own.start()
        cp.start()
        cp.wait()
        own.wait()

    dma = pltpu.SemaphoreType.DMA
    return pl.pallas_call(
        body, in_specs=[ANY], out_specs=ANY, out_shape=SDS((2 * half, W), fin.dtype),
        scratch_shapes=[dma, dma, dma], name="rs_sibling_share",
        compiler_params=pltpu.CompilerParams(has_side_effects=True))(fin)


def allreduce_small(pack):
    R = pack.shape[0]

    def body(p_ref, o_ref, all_ref, ssem, rsem):
        x, y, c, _ = _place()
        me = 4 * x + 2 * y + c
        all_ref[me] = p_ref[...]
        cps = []
        for m in range(1, 8):
            peer = (1 - x if m & 4 else x, 1 - y if m & 2 else y, 1 - c if m & 1 else c)
            cp = _rcopy(p_ref, all_ref.at[me], ssem.at[m - 1], rsem.at[m - 1], peer)
            cp.start()
            cps.append(cp)
        for cp in cps:
            cp.wait()
        acc = all_ref[0]
        for i in range(1, 8):
            acc = acc + all_ref[i]
        o_ref[...] = acc

    dma = pltpu.SemaphoreType.DMA
    vm = BS(memory_space=pltpu.VMEM)
    return pl.pallas_call(
        body, in_specs=[vm], out_specs=vm, out_shape=SDS(pack.shape, F32),
        scratch_shapes=[pltpu.VMEM((8, R, LANES), F32), dma((7,)), dma((7,))], name="allreduce_small",
        compiler_params=pltpu.CompilerParams(has_side_effects=True))(pack)


PACK_W = 1024
PACK_ENTRIES = []
for _l in range(DEPTH):
    if _l % 2 == 0:
        PACK_ENTRIES += [("gdn_w_in", _l // 2, D_MODEL, GDN_IN // 4, True), ("gdn_w_out", _l // 2, D_MODEL // 4, D_MODEL, False)]
    else:
        PACK_ENTRIES += [("dswa_w_in", _l // 2, D_MODEL, 3 * DSWA_W // 4, True), ("dswa_w_out", _l // 2, DSWA_W // 4, D_MODEL, False)]
    PACK_ENTRIES += [("mlp_w1", _l, D_MODEL, D_FF // 4, True), ("mlp_w2", _l, D_FF // 4, D_MODEL, False)]
PACK_ROWS = [r * c // PACK_W for (_, _, r, c, _) in PACK_ENTRIES]
PACK_OFFS = [int(o) for o in np.cumsum([0] + PACK_ROWS)]
PACK_TOTAL = -(-PACK_OFFS[-1] // (32 * GATHER_CHUNKS)) * (32 * GATHER_CHUNKS)


def _pack_shards(shards, dtype):
    parts = [shards[name][li].astype(dtype).reshape(-1, PACK_W) for (name, li, _, _, _) in PACK_ENTRIES]
    parts.append(jnp.zeros((PACK_TOTAL - PACK_OFFS[-1], PACK_W), dtype))
    return jnp.concatenate(parts, axis=0)


def _unpack_full(full):
    mats = []
    for e, (name, li, r, c, by_col) in enumerate(PACK_ENTRIES):
        sh = full[:, PACK_OFFS[e]:PACK_OFFS[e + 1]].reshape(4, r, c)
        mats.append(jnp.concatenate([sh[j] for j in range(4)], axis=1 if by_col else 0))
    return mats


def _pack_grads(grads):
    per_chip = []
    for j in range(4):
        parts = []
        for e, (name, li, r, c, by_col) in enumerate(PACK_ENTRIES):
            g = grads[e]
            sh = g[:, c * j:c * (j + 1)] if by_col else g[r * j:r * (j + 1), :]
            parts.append(sh.reshape(-1, PACK_W))
        parts.append(jnp.zeros((PACK_TOTAL - PACK_OFFS[-1], PACK_W), F32))
        per_chip.append(jnp.concatenate(parts, axis=0))
    return jnp.stack(per_chip)


def _unpack_shard_grads(gsh):
    out = {}
    for e, (name, li, r, c, _) in enumerate(PACK_ENTRIES):
        out.setdefault(name, []).append(gsh[PACK_OFFS[e]:PACK_OFFS[e + 1]].reshape(r, c))
    return {k: jnp.stack(v) for k, v in out.items()}


def _flat_pad(t, mult=8 * LANES):
    f = t.reshape(-1)
    return jnp.pad(f, (0, (-f.shape[0]) % mult))


def kernel(x, norm_mix, norm_mlp, norm_final, rel_bias, gdn_w_in, gdn_conv_w, gdn_a_log, gdn_dt_bias, gdn_norm_w, gdn_w_out, dswa_w_in, dswa_w_out, mlp_w1, mlp_w2, loss_target, m_norm_mix, m_norm_mlp, m_norm_final, m_rel_bias, m_gdn_w_in, m_gdn_conv_w, m_gdn_a_log, m_gdn_dt_bias, m_gdn_norm_w, m_gdn_w_out, m_dswa_w_in, m_dswa_w_out, m_mlp_w1, m_mlp_w2, v_norm_mix, v_norm_mlp, v_norm_final, v_rel_bias, v_gdn_w_in, v_gdn_conv_w, v_gdn_a_log, v_gdn_dt_bias, v_gdn_norm_w, v_gdn_w_out, v_dswa_w_in, v_dswa_w_out, v_mlp_w1, v_mlp_w2):
    xi, yi, ci = lax.axis_index("x"), lax.axis_index("y"), lax.axis_index("c")
    jme = 2 * xi + yi
    big = dict(gdn_w_in=gdn_w_in, gdn_w_out=gdn_w_out, dswa_w_in=dswa_w_in, dswa_w_out=dswa_w_out, mlp_w1=mlp_w1, mlp_w2=mlp_w2)
    n_gdn = gdn_w_in.shape[0]
    conv_cols = gdn_conv_w.shape[-1]

    pack = _pack_shards(big, BF16)
    convp = jnp.pad(gdn_conv_w.reshape(n_gdn * GDN_CONV, conv_cols), ((0, 16 - n_gdn * GDN_CONV), (0, 0)))
    full, cfull = gather_weights(pack, convp)
    mats = _unpack_full(full)
    conv_all = jnp.transpose(cfull[:, :n_gdn * GDN_CONV], (1, 0, 2)).reshape(n_gdn, GDN_CONV, 4 * conv_cols)
    conv_all = jnp.pad(conv_all, ((0, 0), (0, 8 - GDN_CONV), (0, 0)))

    xs = x[0]
    saved = []
    for l in range(DEPTH):
        w_in, w_out, w1, w2 = mats[4 * l:4 * l + 4]
        gm, gp = norm_mix[l][None], norm_mlp[l][None]
        a = l // 2
        if l % 2 == 0:
            w_in = jnp.pad(w_in, ((0, 0), (0, GDN_INP - GDN_IN)))
            x_mid, sv = gdn_fwd(xs, gm, w_in, conv_all[a], gdn_a_log[a], gdn_dt_bias[a], gdn_norm_w[a][None], w_out, f"l{l}_gdn")
        else:
            x_mid, sv = dswa_fwd(xs, gm, w_in, w_out, rel_bias, f"l{l}_att")
        x_out, sv2 = mlp_fwd(x_mid, gp, w1, w2, f"l{l}_mlp")
        saved.append((xs, x_mid, w_in, sv, sv2))
        xs = x_out

    loss_part, dx, dxb, d_final = loss_head(xs, norm_final[None], loss_target[0], "loss_head")
    grads = [None] * len(PACK_ENTRIES)
    d_mix, d_mlp = [None] * DEPTH, [None] * DEPTH
    d_conv, d_alog, d_dt, d_nw = [None] * n_gdn, [None] * n_gdn, [None] * n_gdn, [None] * n_gdn
    d_rel = jnp.zeros_like(rel_bias)
    for l in reversed(range(DEPTH)):
        _, w_out, w1, w2 = mats[4 * l:4 * l + 4]
        x_in, x_mid, w_in, sv, sv2 = saved[l]
        gm, gp = norm_mix[l][None], norm_mlp[l][None]
        a = l // 2
        dx, dxb, d_mlp[l], grads[4 * l + 2], grads[4 * l + 3] = mlp_bwd(x_mid, gp, w1, w2, sv2, dx, dxb, f"l{l}_mlp")
        if l % 2 == 0:
            dx, dxb, d_mix[l], dw_all, d_conv[a], d_alog[a], d_dt[a], d_nw[a], grads[4 * l + 1] = gdn_bwd(
                x_in, gm, w_in, conv_all[a], gdn_norm_w[a][None], w_out, sv, dx, dxb, f"l{l}_gdn")
            grads[4 * l] = dw_all[:, :GDN_IN]
        else:
            dx, dxb, d_mix[l], grads[4 * l], grads[4 * l + 1], drel = dswa_bwd(x_in, gm, w_in, w_out, rel_bias, sv, dx, dxb, f"l{l}_att")
            d_rel = d_rel + drel

    cidx = ci.astype(jnp.int32).reshape(1)
    gpack = _pack_grads(grads)
    part = rs_add_sibling(gpack, rs_sibling_exchange(gpack), cidx)
    gsh = rs_sibling_share(rs_add_chips(rs_chip_exchange(part)))
    gbig = _unpack_shard_grads(gsh)

    small = [jnp.concatenate(d_mix, axis=0), jnp.concatenate(d_mlp, axis=0), d_final, d_rel,
             jnp.stack(d_conv), jnp.stack(d_alog), jnp.stack(d_dt), jnp.concatenate(d_nw, axis=0)]
    flat = [_flat_pad(t) for t in small]
    sizes = [f.shape[0] for f in flat]
    red = allreduce_small(jnp.concatenate(flat).reshape(-1, LANES)).reshape(-1)
    offs = np.cumsum([0] + sizes)
    red = [red[offs[i]:offs[i] + small[i].size].reshape(small[i].shape) for i in range(len(small))]
    g_conv_all = red[4][:, :GDN_CONV].reshape(n_gdn, GDN_CONV, 1, 4 * conv_cols)
    g_conv = lax.dynamic_slice_in_dim(g_conv_all, jme * conv_cols, conv_cols, axis=3)
    g = dict(norm_mix=red[0], norm_mlp=red[1], norm_final=red[2].reshape(norm_final.shape), rel_bias=red[3],
             gdn_conv_w=g_conv, gdn_a_log=red[5], gdn_dt_bias=red[6], gdn_norm_w=red[7][:, :GDN_DK], **gbig)

    w = dict(norm_mix=norm_mix, norm_mlp=norm_mlp, norm_final=norm_final, rel_bias=rel_bias, gdn_conv_w=gdn_conv_w,
             gdn_a_log=gdn_a_log, gdn_dt_bias=gdn_dt_bias, gdn_norm_w=gdn_norm_w, **big)
    m = dict(norm_mix=m_norm_mix, norm_mlp=m_norm_mlp, norm_final=m_norm_final, rel_bias=m_rel_bias, gdn_w_in=m_gdn_w_in,
             gdn_conv_w=m_gdn_conv_w, gdn_a_log=m_gdn_a_log, gdn_dt_bias=m_gdn_dt_bias, gdn_norm_w=m_gdn_norm_w,
             gdn_w_out=m_gdn_w_out, dswa_w_in=m_dswa_w_in, dswa_w_out=m_dswa_w_out, mlp_w1=m_mlp_w1, mlp_w2=m_mlp_w2)
    v = dict(norm_mix=v_norm_mix, norm_mlp=v_norm_mlp, norm_final=v_norm_final, rel_bias=v_rel_bias, gdn_w_in=v_gdn_w_in,
             gdn_conv_w=v_gdn_conv_w, gdn_a_log=v_gdn_a_log, gdn_dt_bias=v_gdn_dt_bias, gdn_norm_w=v_gdn_norm_w,
             gdn_w_out=v_gdn_w_out, dswa_w_in=v_dswa_w_in, dswa_w_out=v_dswa_w_out, mlp_w1=v_mlp_w1, mlp_w2=v_mlp_w2)
    names = ["norm_mix", "norm_mlp", "norm_final", "rel_bias", "gdn_w_in", "gdn_conv_w", "gdn_a_log", "gdn_dt_bias",
             "gdn_norm_w", "gdn_w_out", "dswa_w_in", "dswa_w_out", "mlp_w1", "mlp_w2"]
    upd = {n: adamw(w[n], g[n], m[n], v[n], f"adamw_{n}") for n in names}
    loss = lax.psum(loss_part[0, 0], ("x", "y", "c"))
    return (loss, dx[None], *[g[n] for n in names], *[upd[n][0] for n in names], *[upd[n][1] for n in names],
            *[upd[n][2] for n in names])
```

```python
import math
from typing import Callable, NamedTuple

import numpy as np
import jax
import jax.numpy as jnp
from jax import lax
from jax.experimental import pallas as pl
from jax.experimental.pallas import tpu as pltpu

F32 = jnp.float32
BF16 = jnp.bfloat16
HI = lax.Precision.HIGHEST
BS = pl.BlockSpec
SDS = jax.ShapeDtypeStruct
MESH = pl.DeviceIdType.MESH
ANY = BS(memory_space=pl.ANY)

D_MODEL = 1024
D_FF = 4096
DEPTH = 4
RMS_EPS = 1e-6
NEG_INF = -1e30
LANES = 128
VMEM_LIMIT = 56 << 20

GDN_H = 8
GDN_DK = 128
GDN_CONV = 5
GDN_C = 64
GDN_GC = 8
GDN_HP = 4
GDN_QKV = 3 * GDN_H * GDN_DK
GDN_IN = GDN_QKV + GDN_H * GDN_DK + 4 * GDN_H
GDN_INP = 4224

DSWA_CFG = ((128, 1), (512, 4), (2048, 16))
DSWA_HG = 6
DSWA_E = 64
DSWA_HEADS = 18
DSWA_W = DSWA_HEADS * DSWA_E
DSWA_HALF = 64
REL_BUCKETS = 32
REL_MAX_DIST = 1024

ADAM_LR = 0.001
ADAM_B1 = 0.9
ADAM_B2 = 0.999
ADAM_EPS = 1e-08
ADAM_WD = 0.01
ADAM_STEP = 10


def _cp(sem=None):
    return pltpu.CompilerParams(dimension_semantics=sem, vmem_limit_bytes=VMEM_LIMIT)


def _dot(a, b, prec=None):
    return jnp.dot(a, b, precision=prec, preferred_element_type=F32)


def _dot_nt(a, b, prec=None):
    return lax.dot_general(a, b, (((1,), (1,)), ((), ())), precision=prec, preferred_element_type=F32)


def _dot_tn(a, b, prec=None):
    return lax.dot_general(a, b, (((0,), (0,)), ((), ())), precision=prec, preferred_element_type=F32)


def _bf(a):
    return a.astype(BF16)


def _sigmoid(x):
    return 1.0 / (1.0 + jnp.exp(-x))


def rms_fwd(x, g, name):
    S, Dm = x.shape
    tm = min(512, S)

    def body(x_ref, g_ref, o_ref):
        xv = x_ref[...]
        r = lax.rsqrt(jnp.mean(xv * xv, axis=-1, keepdims=True) + RMS_EPS)
        o_ref[...] = (xv * r * g_ref[...]).astype(o_ref.dtype)

    return pl.pallas_call(
        body, grid=(S // tm,),
        in_specs=[BS((tm, Dm), lambda i: (i, 0)), BS((1, Dm), lambda i: (0, 0))],
        out_specs=BS((tm, Dm), lambda i: (i, 0)),
        out_shape=SDS((S, Dm), BF16), name=name, compiler_params=_cp(("parallel",)))(x, g)


def rms_bwd(x, g, dh, dres, name):
    S, Dm = x.shape
    tm = min(512, S)

    def body(x_ref, g_ref, dh_ref, dres_ref, dx_ref, dxb_ref, dg_ref):
        i = pl.program_id(0)
        xv = x_ref[...]
        r = lax.rsqrt(jnp.mean(xv * xv, axis=-1, keepdims=True) + RMS_EPS)
        n = xv * r
        dhv = dh_ref[...]
        t = dhv * g_ref[...]
        dx = dres_ref[...] + r * (t - n * jnp.mean(n * t, axis=-1, keepdims=True))
        dx_ref[...] = dx
        dxb_ref[...] = dx.astype(BF16)
        part = jnp.sum(dhv * n, axis=0, keepdims=True)

        @pl.when(i == 0)
        def _():
            dg_ref[...] = part

        @pl.when(i > 0)
        def _():
            dg_ref[...] += part

    row = BS((tm, Dm), lambda i: (i, 0))
    vec = BS((1, Dm), lambda i: (0, 0))
    return pl.pallas_call(
        body, grid=(S // tm,), in_specs=[row, vec, row, row], out_specs=[row, row, vec],
        out_shape=[SDS((S, Dm), F32), SDS((S, Dm), BF16), SDS((1, Dm), F32)],
        name=name, compiler_params=_cp(("arbitrary",)))(x, g, dh, dres)


def loss_head(x, g, tgt, name):
    S, Dm = x.shape
    tm = min(512, S)

    def body(x_ref, g_ref, t_ref, loss_ref, dx_ref, dxb_ref, dg_ref):
        i = pl.program_id(0)
        xv = x_ref[...]
        gv = g_ref[...]
        r = lax.rsqrt(jnp.mean(xv * xv, axis=-1, keepdims=True) + RMS_EPS)
        n = xv * r
        err = n * gv - t_ref[...]
        lpart = 0.5 * jnp.sum(jnp.mean(err * err, axis=-1, keepdims=True), axis=0, keepdims=True)
        dout = err * (1.0 / Dm)
        t = dout * gv
        dx = r * (t - n * jnp.mean(n * t, axis=-1, keepdims=True))
        dx_ref[...] = dx
        dxb_ref[...] = dx.astype(BF16)
        part = jnp.sum(dout * n, axis=0, keepdims=True)

        @pl.when(i == 0)
        def _():
            dg_ref[...] = part
            loss_ref[...] = lpart

        @pl.when(i > 0)
        def _():
            dg_ref[...] += part
            loss_ref[...] += lpart

    row = BS((tm, Dm), lambda i: (i, 0))
    vec = BS((1, Dm), lambda i: (0, 0))
    one = BS((1, 1), lambda i: (0, 0))
    return pl.pallas_call(
        body, grid=(S // tm,), in_specs=[row, vec, row], out_specs=[one, row, row, vec],
        out_shape=[SDS((1, 1), F32), SDS((S, Dm), F32), SDS((S, Dm), BF16), SDS((1, Dm), F32)],
        name=name, compiler_params=_cp(("arbitrary",)))(x, g, tgt)


def mm(a, b, *, name, ta=False, tb=False, tm=512, tn=512, tk=None, out_dtype=F32, pre_a=None, epi=None,
       extras=()):
    M, K = (a.shape[1], a.shape[0]) if ta else a.shape
    N = b.shape[0] if tb else b.shape[1]
    tm, tn = min(tm, M), min(tn, N)
    tk = K if tk is None else min(tk, K)
    assert M % tm == 0 and N % tn == 0 and K % tk == 0, (name, M, N, K, tm, tn, tk)
    nk = K // tk
    ne = len(extras)
    a_spec = BS((tk, tm), lambda i, j, k: (k, i)) if ta else BS((tm, tk), lambda i, j, k: (i, k))
    b_spec = BS((tn, tk), lambda i, j, k: (j, k)) if tb else BS((tk, tn), lambda i, j, k: (k, j))
    o_spec = BS((tm, tn), lambda i, j, k: (i, j))
    dims = (((0 if ta else 1,), (1 if tb else 0,)), ((), ()))

    def body(a_ref, b_ref, *rest):
        e_refs, o_ref = rest[:ne], rest[ne]
        av = a_ref[...]
        if pre_a is not None:
            av = pre_a(av)
        p = lax.dot_general(_bf(av), _bf(b_ref[...]), dims, preferred_element_type=F32)

        def finish(acc):
            res = epi(acc, *[e[...] for e in e_refs]) if epi is not None else acc
            o_ref[...] = res.astype(o_ref.dtype)

        if nk == 1:
            finish(p)
        else:
            acc_ref = rest[ne + 1]
            k = pl.program_id(2)

            @pl.when(k == 0)
            def _():
                acc_ref[...] = p

            @pl.when(k > 0)
            def _():
                acc_ref[...] += p

            @pl.when(k == nk - 1)
            def _():
                finish(acc_ref[...])

    return pl.pallas_call(
        body, grid=(M // tm, N // tn, nk), in_specs=[a_spec, b_spec] + [o_spec] * ne, out_specs=o_spec,
        out_shape=SDS((M, N), out_dtype),
        scratch_shapes=[pltpu.VMEM((tm, tn), F32)] if nk > 1 else [],
        name=name, compiler_params=_cp(("parallel", "parallel", "arbitrary")))(a, b, *extras)


def _relu(acc):
    return jnp.maximum(acc, 0.0)


def _add(acc, res):
    return acc + res


def _sq(av):
    af = av.astype(F32)
    return af * af


def _times_2r(acc, r):
    return acc * (2.0 * r.astype(F32))


def mlp_fwd(x, g, w1, w2, tag):
    h = rms_fwd(x, g, f"{tag}_rms")
    r = mm(h, w1, name=f"{tag}_up", tn=1024, out_dtype=BF16, epi=_relu)
    xn = mm(r, w2, name=f"{tag}_down", pre_a=_sq, epi=_add, extras=(x,))
    return xn, (h, r)


def mlp_bwd(x, g, w1, w2, saved, dx, dxb, tag):
    h, r = saved
    da = mm(dxb, w2, name=f"{tag}_dact", tb=True, tn=1024, out_dtype=BF16, epi=_times_2r, extras=(r,))
    dw2 = mm(r, dxb, name=f"{tag}_dw2", ta=True, pre_a=_sq)
    dw1 = mm(h, da, name=f"{tag}_dw1", ta=True)
    dh = mm(da, w1, name=f"{tag}_dh", tb=True)
    dx, dxb, dg = rms_bwd(x, g, dh, dx, f"{tag}_rmsb")
    return dx, dxb, dg, dw1, dw2


def _conv_taps(x, S):
    t = lax.broadcasted_iota(jnp.int32, x.shape, 0)
    taps = []
    for j in range(GDN_CONV):
        sh = j - GDN_CONV // 2
        xs = x if sh == 0 else pltpu.roll(x, (-sh) % S, 0)
        taps.append(jnp.where((t + sh >= 0) & (t + sh < S), xs, 0.0))
    return taps


def _qkv_scale(c):
    is_norm = c < 2 * GDN_H
    scale = jnp.where(c < GDN_H, GDN_DK ** -0.5, 1.0)
    return is_norm, scale


def gdn_pre_fwd(proj, convw, name):
    S = proj.shape[0]

    def body(p_ref, w_ref, o_ref):
        c = pl.program_id(0)
        x = p_ref[...]
        w = w_ref[...]
        y = jnp.zeros_like(x)
        for j, xs in enumerate(_conv_taps(x, S)):
            y = y + w[j:j + 1, :] * xs
        t = y * _sigmoid(y)
        is_norm, scale = _qkv_scale(c)
        r = lax.rsqrt(jnp.sum(t * t, axis=-1, keepdims=True) + 1e-6)
        o_ref[...] = jnp.where(is_norm, t * r * scale, t)

    return pl.pallas_call(
        body, grid=(GDN_QKV // LANES,),
        in_specs=[BS((S, LANES), lambda c: (0, c)), BS((8, LANES), lambda c: (0, c))],
        out_specs=BS((S, LANES), lambda c: (0, c)),
        out_shape=SDS((S, GDN_QKV), F32), name=name, compiler_params=_cp(("parallel",)))(proj, convw)


def gdn_pre_bwd(proj, convw, dqkv, name):
    S = proj.shape[0]

    def body(p_ref, w_ref, d_ref, dp_ref, dw_ref):
        c = pl.program_id(0)
        x = p_ref[...]
        w = w_ref[...]
        taps = _conv_taps(x, S)
        y = jnp.zeros_like(x)
        for j, xs in enumerate(taps):
            y = y + w[j:j + 1, :] * xs
        sg = _sigmoid(y)
        t = y * sg
        is_norm, scale = _qkv_scale(c)
        dout = d_ref[0, 0] + d_ref[1, 0]
        r = lax.rsqrt(jnp.sum(t * t, axis=-1, keepdims=True) + 1e-6)
        n = t * r
        dn = dout * scale
        dt_norm = r * (dn - n * jnp.sum(dn * n, axis=-1, keepdims=True))
        dt = jnp.where(is_norm, dt_norm, dout)
        dy = dt * (sg * (1.0 + y * (1.0 - sg)))
        row = lax.broadcasted_iota(jnp.int32, (8, LANES), 0)
        dw = jnp.zeros((8, LANES), F32)
        for j, xs in enumerate(taps):
            dw = dw + jnp.where(row == j, jnp.sum(dy * xs, axis=0, keepdims=True), 0.0)
        dw_ref[...] = dw
        tt = lax.broadcasted_iota(jnp.int32, x.shape, 0)
        dx = jnp.zeros_like(x)
        for j in range(GDN_CONV):
            sh = j - GDN_CONV // 2
            ds = dy if sh == 0 else pltpu.roll(dy, sh % S, 0)
            dx = dx + w[j:j + 1, :] * jnp.where((tt - sh >= 0) & (tt - sh < S), ds, 0.0)
        dp_ref[...] = dx.astype(BF16)

    return pl.pallas_call(
        body, grid=(GDN_QKV // LANES,),
        in_specs=[BS((S, LANES), lambda c: (0, c)), BS((8, LANES), lambda c: (0, c)),
                  BS((2, 1, S, LANES), lambda c: (0, c // GDN_H, 0, c % GDN_H))],
        out_specs=[BS((S, LANES), lambda c: (0, c)), BS((8, LANES), lambda c: (0, c))],
        out_shape=[SDS((S, GDN_QKV), BF16), SDS((8, GDN_QKV), F32)],
        name=name, compiler_params=_cp(("parallel",)))(proj, convw, dqkv)


def _chunk_sum_matrix(n, upper):
    i = lax.broadcasted_iota(jnp.int32, (n, n), 0)
    j = lax.broadcasted_iota(jnp.int32, (n, n), 1)
    same = (i // GDN_C) == (j // GDN_C)
    tri = (i <= j) if upper else (i >= j)
    return jnp.where(same & tri, 1.0, 0.0).astype(F32)


def _gate_lanes(shape):
    lane = lax.broadcasted_iota(jnp.int32, shape, 1)
    return lane < GDN_H, (lane >= GDN_H) & (lane < 2 * GDN_H), (lane >= 2 * GDN_H) & (lane < 4 * GDN_H)


def gdn_gate_fwd(proj, prm, name):
    S = proj.shape[0]
    tm = min(512, S)
    ct = GDN_INP // LANES - 1

    def body(p_ref, prm_ref, o_ref):
        ab = p_ref[...]
        a_log = prm_ref[0:1, :]
        dtb = prm_ref[1:2, :]
        z = ab + dtb
        sp = jnp.maximum(z, 0.0) + jnp.log(1.0 + jnp.exp(-jnp.abs(z)))
        g = -jnp.exp(a_log) * sp
        is_f, is_b, is_beta = _gate_lanes(ab.shape)
        gf = _dot(_chunk_sum_matrix(tm, False), jnp.where(is_f, g, 0.0), HI)
        gbk = _dot(_chunk_sum_matrix(tm, True), jnp.where(is_b, g, 0.0), HI)
        o_ref[...] = gf + gbk + jnp.where(is_beta, _sigmoid(ab), 0.0)

    return pl.pallas_call(
        body, grid=(S // tm,),
        in_specs=[BS((tm, LANES), lambda i: (i, ct)), BS((8, LANES), lambda i: (0, 0))],
        out_specs=BS((tm, LANES), lambda i: (i, 0)),
        out_shape=SDS((S, LANES), F32), name=name, compiler_params=_cp(("parallel",)))(proj, prm)


def gdn_gate_bwd(proj, prm, dgb, name):
    S = proj.shape[0]
    tm = min(512, S)
    ct = GDN_INP // LANES - 1

    def body(p_ref, prm_ref, d_ref, dab_ref, dprm_ref):
        i = pl.program_id(0)
        ab = p_ref[...]
        a_log = prm_ref[0:1, :]
        dtb = prm_ref[1:2, :]
        z = ab + dtb
        sp = jnp.maximum(z, 0.0) + jnp.log(1.0 + jnp.exp(-jnp.abs(z)))
        ea = jnp.exp(a_log)
        g = -ea * sp
        is_f, is_b, is_beta = _gate_lanes(ab.shape)
        d = d_ref[...]
        dg = (_dot_tn(_chunk_sum_matrix(tm, False), jnp.where(is_f, d, 0.0), HI)
              + _dot_tn(_chunk_sum_matrix(tm, True), jnp.where(is_b, d, 0.0), HI))
        da = dg * (-ea) * _sigmoid(z)
        beta = _sigmoid(ab)
        dab_ref[...] = jnp.where(is_beta, d * beta * (1.0 - beta), da).astype(BF16)
        row = lax.broadcasted_iota(jnp.int32, (8, LANES), 0)
        part = (jnp.where(row == 0, jnp.sum(dg * g, axis=0, keepdims=True), 0.0)
                + jnp.where(row == 1, jnp.sum(da, axis=0, keepdims=True), 0.0))

        @pl.when(i == 0)
        def _():
            dprm_ref[...] = part

        @pl.when(i > 0)
        def _():
            dprm_ref[...] += part

    return pl.pallas_call(
        body, grid=(S // tm,),
        in_specs=[BS((tm, LANES), lambda i: (i, ct)), BS((8, LANES), lambda i: (0, 0)), BS((tm, LANES), lambda i: (i, 0))],
        out_specs=[BS((tm, LANES), lambda i: (i, 0)), BS((8, LANES), lambda i: (0, 0))],
        out_shape=[SDS((S, LANES), BF16), SDS((8, LANES), F32)],
        name=name, compiler_params=_cp(("arbitrary",)))(proj, prm, dgb)


def _tri_masks(d):
    i = lax.broadcasted_iota(jnp.int32, (GDN_C, GDN_C), 0)
    j = lax.broadcasted_iota(jnp.int32, (GDN_C, GDN_C), 1)
    s = (i - j) * (1 - 2 * d)
    return s >= 0, s > 0


def _split(a):
    hi = _bf(a)
    return hi, _bf(a - hi.astype(F32))


def _dot3(a, b):
    return _dot(a[0], b[0]) + (_dot(a[0], b[1]) + _dot(a[1], b[0]))


def _inv_unit_tri_many(mats):
    i = lax.broadcasted_iota(jnp.int32, mats[0].shape, 0)
    j = lax.broadcasted_iota(jnp.int32, mats[0].shape, 1)
    eye = jnp.where(i == j, 1.0, 0.0)
    ms = [-a for a in mats]
    ps = [eye + m for m in ms]
    for _ in range(int(math.log2(GDN_C)) - 1):
        sp = [_split(m) for m in ms]
        ms = [_dot3(s, s) for s in sp]
        sp = [_split(m) for m in ms]
        pp = [_split(p) for p in ps]
        ps = [p + _dot3(a, b) for p, a, b in zip(ps, pp, sp)]
    return ps


def _lane_col(x, lane_idx):
    lane = lax.broadcasted_iota(jnp.int32, x.shape, 1)
    return jnp.sum(jnp.where(lane == lane_idx, x, 0.0), axis=1, keepdims=True)


def _chunk_gates(gb_ref, grow_ref, hh, ci, d, head):
    gbv = gb_ref[ci * GDN_C:(ci + 1) * GDN_C, :]
    gcol = _lane_col(gbv, d * GDN_H + head)
    bcol = _lane_col(gbv, 2 * GDN_H + d * GDN_H + head)
    glast = jnp.where(d == 0, gcol[GDN_C - 1:GDN_C, :], gcol[0:1, :])
    return gcol, bcol, grow_ref[hh, ci:ci + 1, :], glast


def _chunk_base(q, k, gcol, grow, bcol, glast, d):
    incl, strict = _tri_masks(d)
    decay = jnp.where(incl, jnp.exp(jnp.where(incl, gcol - grow, 0.0)), 0.0)
    kb = k * bcol
    kk = _dot_nt(_bf(kb), _bf(k))
    qk = _dot_nt(_bf(q), _bf(k))
    eg = jnp.exp(gcol)
    ek = jnp.exp(glast - gcol)
    return dict(incl=incl, strict=strict, decay=decay, kb=kb, kk=kk, qk=qk, eg=eg, ek=ek, q_dec=q * eg, k_dec=k * ek,
                bcol=bcol, glast=glast)


def _block_terms(q_ref, k_ref, v_ref, gb_ref, grow_ref, d, h):
    keys = [(hh, ci) for hh in range(GDN_HP) for ci in range(GDN_GC)]
    ts = []
    for hh, ci in keys:
        rows = slice(ci * GDN_C, (ci + 1) * GDN_C)
        cols = slice(hh * GDN_DK, (hh + 1) * GDN_DK)
        gcol, bcol, grow_v, glast = _chunk_gates(gb_ref, grow_ref, hh, ci, d, h * GDN_HP + hh)
        t = _chunk_base(q_ref[rows, cols], k_ref[rows, cols], gcol, grow_v, bcol, glast, d)
        t["v"] = v_ref[rows, cols]
        ts.append(t)
    tinvs = _inv_unit_tri_many([jnp.where(t["strict"], t["kk"] * t["decay"], 0.0) for t in ts])
    sp = [_split(x) for x in tinvs]
    us = [_dot3(s, _split(t["v"] * t["bcol"])) for s, t in zip(sp, ts)]
    ws = [_dot3(s, _split(t["kb"] * t["eg"])) for s, t in zip(sp, ts)]
    for t, tinv, u, w in zip(ts, tinvs, us, ws):
        t.update(tinv=tinv, u=u, w=w)
    return keys, ts


def _gdn_specs(S, nblk, order):
    R = GDN_GC * GDN_C
    wd = GDN_HP * GDN_DK
    hb = GDN_H // GDN_HP

    def qkv_spec(part):
        return BS((R, wd), lambda d, h, n: (order(d, n), part * hb + h))

    gb_spec = BS((R, LANES), lambda d, h, n: (order(d, n), 0))
    grow_spec = BS((GDN_HP, GDN_GC, GDN_C), lambda d, h, n: (d * hb + h, order(d, n), 0))
    st_spec = BS((1, GDN_HP, GDN_GC, GDN_DK, GDN_DK), lambda d, h, n: (d, h, order(d, n), 0, 0))
    return qkv_spec, gb_spec, grow_spec, st_spec


def _lane_row(x):
    return jnp.broadcast_to(x, (1, LANES))


def _side_parts(side):
    if side is None:
        return [], [], [], [], []
    return [ANY] * len(side.ins), [ANY] * len(side.outs), list(side.outs), list(side.sems), list(side.ins)


def _side_run(side, refs, n_in, n_out, n_scr, first, last):
    if side is None:
        return
    ns, no, nm = len(side.ins), len(side.outs), len(side.sems)
    s_in = refs[n_in:n_in + ns]
    s_out = refs[n_in + ns + n_out:n_in + ns + n_out + no]
    s_sem = refs[len(refs) - nm:]

    @pl.when(first)
    def _():
        side.start(s_in, s_out, s_sem)

    @pl.when(last)
    def _():
        side.wait(s_in, s_out, s_sem)


def gdn_scan_fwd(qkv, gb, grow, name, side=None):
    S = qkv.shape[0]
    R = GDN_GC * GDN_C
    nblk = S // R
    nc = S // GDN_C
    wd = GDN_HP * GDN_DK
    heads = range(GDN_HP)

    def order(d, n):
        return n + d * (nblk - 1 - 2 * n)

    qkv_spec, gb_spec, grow_spec, st_spec = _gdn_specs(S, nblk, order)

    s_in, s_out, s_shape, s_scr_shapes, s_ops = _side_parts(side)
    hb = GDN_H // GDN_HP

    def body(*refs):
        q_ref, k_ref, v_ref, gb_ref, grow_ref = refs[:5]
        o_ref, st_ref = refs[5 + len(s_in):7 + len(s_in)]
        s_scr, u_scr, w_scr, qd_scr, kd_scr, in_scr, egl_scr = refs[7 + len(s_in) + len(s_out):14 + len(s_in) + len(s_out)]
        d = pl.program_id(0)
        h = pl.program_id(1)
        n = pl.program_id(2)
        _side_run(side, refs, 5, 2, 7, (d == 0) & (h == 0) & (n == 0), (d == 1) & (h == hb - 1) & (n == nblk - 1))

        @pl.when(n == 0)
        def _():
            s_scr[...] = jnp.zeros_like(s_scr)

        keys, ts = _block_terms(q_ref, k_ref, v_ref, gb_ref, grow_ref, d, h)
        for (hh, ci), t in zip(keys, ts):
            u_scr[hh, ci] = t["u"]
            w_scr[hh, ci] = _bf(t["w"])
            qd_scr[hh, ci] = _bf(t["q_dec"])
            kd_scr[hh, ci] = _bf(t["k_dec"])
            in_scr[hh, ci] = _bf(jnp.where(t["incl"], t["qk"] * t["decay"], 0.0))
            egl_scr[hh, ci] = _lane_row(jnp.exp(t["glast"]))

        def chunk(cc, carry):
            ci = cc + d * (GDN_GC - 1 - 2 * cc)
            rows = pl.ds(pl.multiple_of(ci * GDN_C, GDN_C), GDN_C)
            sts = [s_scr[hh] for hh in heads]
            for hh in heads:
                st_ref[0, hh, ci] = sts[hh]
            sbs = [_bf(st) for st in sts]
            vns = [_bf(u_scr[hh, ci] - _dot(w_scr[hh, ci], sbs[hh])) for hh in heads]
            for hh in heads:
                s_scr[hh] = sts[hh] * egl_scr[hh, ci] + _dot_tn(kd_scr[hh, ci], vns[hh])
            for hh in heads:
                o_ref[0, rows, hh * GDN_DK:(hh + 1) * GDN_DK] = _dot(qd_scr[hh, ci], sbs[hh]) + _dot(in_scr[hh, ci], vns[hh])
            return carry

        lax.fori_loop(0, GDN_GC, chunk, 0)

    blk = (GDN_HP, GDN_GC, GDN_C, GDN_DK)
    return pl.pallas_call(
        body, grid=(2, GDN_H // GDN_HP, nblk),
        in_specs=[qkv_spec(0), qkv_spec(1), qkv_spec(2), gb_spec, grow_spec] + s_in,
        out_specs=[BS((1, R, wd), lambda d, h, n: (d, order(d, n), h)), st_spec] + s_out,
        out_shape=[SDS((2, S, GDN_H * GDN_DK), F32), SDS((2, GDN_H, nc, GDN_DK, GDN_DK), F32)] + s_shape,
        scratch_shapes=[pltpu.VMEM((GDN_HP, GDN_DK, GDN_DK), F32), pltpu.VMEM(blk, F32), pltpu.VMEM(blk, BF16),
                        pltpu.VMEM(blk, BF16), pltpu.VMEM(blk, BF16), pltpu.VMEM((GDN_HP, GDN_GC, GDN_C, GDN_C), BF16),
                        pltpu.VMEM((GDN_HP, GDN_GC, 1, LANES), F32)] + s_scr_shapes,
        name=name, compiler_params=_cp(("arbitrary", "arbitrary", "arbitrary")))(qkv, qkv, qkv, gb, grow, *s_ops)


def gdn_scan_bwd(qkv, gb, grow, states, do, name, side=None):
    S = qkv.shape[0]
    R = GDN_GC * GDN_C
    nblk = S // R
    wd = GDN_HP * GDN_DK
    heads = range(GDN_HP)

    def order(d, n):
        return (nblk - 1 - n) - d * (nblk - 1 - 2 * n)

    qkv_spec, gb_spec, grow_spec, st_spec = _gdn_specs(S, nblk, order)

    s_in, s_out, s_shape, s_scr_shapes, s_ops = _side_parts(side)
    hb = GDN_H // GDN_HP

    def body(*refs):
        q_ref, k_ref, v_ref, gb_ref, grow_ref, st_ref, do_ref = refs[:7]
        dqkv_ref, dgate_ref = refs[7 + len(s_in):9 + len(s_in)]
        (ds_scr, w_scr, kd_scr, dv1_scr, qtdo_scr, egl_scr, dsin_scr, dvn_scr,
         sdot_scr) = refs[9 + len(s_in) + len(s_out):18 + len(s_in) + len(s_out)]
        d = pl.program_id(0)
        h = pl.program_id(1)
        n = pl.program_id(2)
        _side_run(side, refs, 7, 2, 9, (d == 0) & (h == 0) & (n == 0), (d == 1) & (h == hb - 1) & (n == nblk - 1))

        @pl.when(n == 0)
        def _():
            ds_scr[...] = jnp.zeros_like(ds_scr)

        keys, ts = _block_terms(q_ref, k_ref, v_ref, gb_ref, grow_ref, d, h)
        for (hh, ci), t in zip(keys, ts):
            rows = slice(ci * GDN_C, (ci + 1) * GDN_C)
            t["wb"] = _bf(t["w"])
            t["dob"] = _bf(do_ref[rows, hh * GDN_DK:(hh + 1) * GDN_DK])
            t["sb"] = _bf(st_ref[0, hh, ci])
        for (hh, ci), t in zip(keys, ts):
            t["vnb"] = _bf(t["u"] - _dot(t["wb"], t["sb"]))
            w_scr[hh, ci] = t["wb"]
            kd_scr[hh, ci] = _bf(t["k_dec"])
            dv1_scr[hh, ci] = _dot_tn(_bf(jnp.where(t["incl"], t["qk"] * t["decay"], 0.0)), t["dob"])
            qtdo_scr[hh, ci] = _dot_tn(_bf(t["q_dec"]), t["dob"])
            egl_scr[hh, ci] = _lane_row(jnp.exp(t["glast"]))

        def chunk(cc, carry):
            ci = (GDN_GC - 1 - cc) - d * (GDN_GC - 1 - 2 * cc)
            dsns = [ds_scr[hh] for hh in heads]
            dsbs = [_bf(x) for x in dsns]
            dvns = [dv1_scr[hh, ci] + _dot(kd_scr[hh, ci], dsbs[hh]) for hh in heads]
            for hh in heads:
                ds_scr[hh] = qtdo_scr[hh, ci] + egl_scr[hh, ci] * dsns[hh] - _dot_tn(w_scr[hh, ci], _bf(dvns[hh]))
            for hh in heads:
                dsin_scr[hh, ci] = dsbs[hh]
                dvn_scr[hh, ci] = dvns[hh]
                sd = jnp.sum(jnp.sum(st_ref[0, hh, ci] * dsns[hh], axis=1, keepdims=True), axis=0, keepdims=True)
                sdot_scr[hh, ci] = _lane_row(sd)
            return carry

        lax.fori_loop(0, GDN_GC, chunk, 0)

        for (hh, ci), t in zip(keys, ts):
            t["d_vnew"] = dvn_scr[hh, ci]
            t["dvb"] = _bf(t["d_vnew"])
            t["dsb"] = dsin_scr[hh, ci]
        for t in ts:
            t["d_intra"] = jnp.where(t["incl"], _dot_nt(t["dob"], t["vnb"]), 0.0)
            t["d_qdec"] = _dot_nt(t["dob"], t["sb"])
            t["d_kdec"] = _dot_nt(t["vnb"], t["dsb"])
            t["dw"] = -_dot_nt(t["dvb"], t["sb"])
        for t in ts:
            tts = _split(t["tinv"].T)
            t["d_ru"] = _dot3(tts, _split(t["d_vnew"]))
            t["d_rw"] = _dot3(tts, _split(t["dw"]))
        for t in ts:
            t["da"] = -jnp.where(t["strict"], _dot_nt(_bf(t["d_ru"]), _bf(t["u"])) + _dot_nt(_bf(t["d_rw"]), t["wb"]), 0.0)
        for (hh, ci), t in zip(keys, ts):
            rows = slice(ci * GDN_C, (ci + 1) * GDN_C)
            cols = slice(hh * GDN_DK, (hh + 1) * GDN_DK)
            q, k, v = q_ref[rows, cols], k_ref[rows, cols], t["v"]
            decay, kb, eg, ek, bcol = t["decay"], t["kb"], t["eg"], t["ek"], t["bcol"]
            d_ru, d_rw, da, d_intra, d_qdec, d_kdec = t["d_ru"], t["d_rw"], t["da"], t["d_intra"], t["d_qdec"], t["d_kdec"]
            kbf, qbf = _bf(k), _bf(q)
            dgl = egl_scr[hh, ci][:, 0:1] * sdot_scr[hh, ci][:, 0:1]
            dv = d_ru * bcol
            dbeta = jnp.sum(d_ru * v, axis=1, keepdims=True)
            dkb = d_rw * eg
            dg = jnp.sum(d_rw * kb, axis=1, keepdims=True) * eg
            dkk = _bf(da * decay)
            dqk = _bf(d_intra * decay)
            dkb = dkb + _dot(dkk, kbf)
            dk = _dot_tn(dkk, _bf(kb)) + _dot_tn(dqk, qbf)
            dq = _dot(dqk, kbf) + d_qdec * eg
            dd = (da * t["kk"] + d_intra * t["qk"]) * decay
            dg = dg + jnp.sum(dd, axis=1, keepdims=True) - jnp.sum(dd.T, axis=1, keepdims=True)
            dg = dg + jnp.sum(d_qdec * t["q_dec"], axis=1, keepdims=True)
            dk = dk + d_kdec * ek
            ee = jnp.sum(d_kdec * t["k_dec"], axis=1, keepdims=True)
            dg = dg - ee
            dgl = dgl + jnp.sum(ee, axis=0, keepdims=True)
            dk = dk + dkb * bcol
            dbeta = dbeta + jnp.sum(dkb * k, axis=1, keepdims=True)
            ridx = lax.broadcasted_iota(jnp.int32, (GDN_C, 1), 0)
            dg = dg + jnp.where(ridx == (GDN_C - 1) * (1 - d), dgl, 0.0)
            dqkv_ref[0, 0, rows, cols] = dq
            dqkv_ref[0, 1, rows, cols] = dk
            dqkv_ref[0, 2, rows, cols] = dv
            lane2 = lax.broadcasted_iota(jnp.int32, (GDN_C, 2), 1)
            dgate_ref[0, hh, rows, :] = jnp.where(lane2 == 0, dg, dbeta)

    blk = (GDN_HP, GDN_GC, GDN_C, GDN_DK)
    sq = (GDN_HP, GDN_GC, GDN_DK, GDN_DK)
    row = (GDN_HP, GDN_GC, 1, LANES)
    return pl.pallas_call(
        body, grid=(2, GDN_H // GDN_HP, nblk),
        in_specs=[qkv_spec(0), qkv_spec(1), qkv_spec(2), gb_spec, grow_spec, st_spec,
                  BS((R, wd), lambda d, h, n: (order(d, n), h))] + s_in,
        out_specs=[BS((1, 3, R, wd), lambda d, h, n: (d, 0, order(d, n), h)),
                   BS((1, GDN_HP, R, 2), lambda d, h, n: (d, h, order(d, n), 0))] + s_out,
        out_shape=[SDS((2, 3, S, GDN_H * GDN_DK), F32), SDS((2, GDN_H, S, 2), F32)] + s_shape,
        scratch_shapes=[pltpu.VMEM((GDN_HP, GDN_DK, GDN_DK), F32), pltpu.VMEM(blk, BF16), pltpu.VMEM(blk, BF16),
                        pltpu.VMEM(blk, F32), pltpu.VMEM(sq, F32), pltpu.VMEM(row, F32), pltpu.VMEM(sq, BF16),
                        pltpu.VMEM(blk, F32), pltpu.VMEM(row, F32)] + s_scr_shapes,
        name=name, compiler_params=_cp(("arbitrary", "arbitrary", "arbitrary")))(qkv, qkv, qkv, gb, grow, states, do, *s_ops)


def gdn_post_fwd(o2, proj, nw, name):
    S = proj.shape[0]
    tm = min(512, S)
    zoff = GDN_QKV // LANES

    def body(o_ref, z_ref, nw_ref, y_ref):
        o = o_ref[0] + o_ref[1]
        z = z_ref[...]
        r = lax.rsqrt(jnp.mean(o * o, axis=-1, keepdims=True) + RMS_EPS)
        y_ref[...] = (o * r * nw_ref[...] * (z * _sigmoid(z))).astype(BF16)

    return pl.pallas_call(
        body, grid=(S // tm, GDN_H),
        in_specs=[BS((2, tm, LANES), lambda i, h: (0, i, h)), BS((tm, LANES), lambda i, h: (i, zoff + h)),
                  BS((1, LANES), lambda i, h: (0, 0))],
        out_specs=BS((tm, LANES), lambda i, h: (i, h)),
        out_shape=SDS((S, GDN_H * GDN_DK), BF16), name=name, compiler_params=_cp(("parallel", "parallel")))(o2, proj, nw)


def gdn_post_bwd(o2, proj, nw, dy, name):
    S = proj.shape[0]
    tm = min(512, S)
    zoff = GDN_QKV // LANES

    def body(o_ref, z_ref, nw_ref, dy_ref, do_ref, dz_ref, dnw_ref):
        first = (pl.program_id(0) == 0) & (pl.program_id(1) == 0)
        o = o_ref[0] + o_ref[1]
        z = z_ref[...]
        nwv = nw_ref[...]
        dyv = dy_ref[...]
        r = lax.rsqrt(jnp.mean(o * o, axis=-1, keepdims=True) + RMS_EPS)
        n = o * r
        sg = _sigmoid(z)
        sz = z * sg
        dz_ref[...] = (dyv * n * nwv * (sg * (1.0 + z * (1.0 - sg)))).astype(BF16)
        dn = dyv * nwv * sz
        do_ref[...] = r * (dn - n * jnp.mean(dn * n, axis=-1, keepdims=True))
        part = jnp.sum(dyv * n * sz, axis=0, keepdims=True)

        @pl.when(first)
        def _():
            dnw_ref[...] = part

        @pl.when(jnp.logical_not(first))
        def _():
            dnw_ref[...] += part

    blk = BS((tm, LANES), lambda i, h: (i, h))
    return pl.pallas_call(
        body, grid=(S // tm, GDN_H),
        in_specs=[BS((2, tm, LANES), lambda i, h: (0, i, h)), BS((tm, LANES), lambda i, h: (i, zoff + h)),
                  BS((1, LANES), lambda i, h: (0, 0)), blk],
        out_specs=[blk, blk, BS((1, LANES), lambda i, h: (0, 0))],
        out_shape=[SDS((S, GDN_H * GDN_DK), F32), SDS((S, GDN_H * GDN_DK), BF16), SDS((1, LANES), F32)],
        name=name, compiler_params=_cp(("arbitrary", "arbitrary")))(o2, proj, nw, dy)


def _gate_prm(a_log, dt_bias):
    z = jnp.zeros((8, LANES), F32)
    z = z.at[0, :2 * GDN_H].set(a_log.reshape(-1))
    return z.at[1, :2 * GDN_H].set(dt_bias.reshape(-1))


def gdn_fwd(x, g, w_all, convw, a_log, dt_bias, nw, w_out, tag, side=None):
    S = x.shape[0]
    h = rms_fwd(x, g, f"{tag}_rms")
    proj = mm(h, w_all, name=f"{tag}_proj", tn=1408)
    qkv = gdn_pre_fwd(proj, convw, f"{tag}_pre")
    prm = _gate_prm(a_log, dt_bias)
    gb = gdn_gate_fwd(proj, prm, f"{tag}_gate")
    grow = gb[:, :2 * GDN_H].T.reshape(2 * GDN_H, S // GDN_C, GDN_C)
    o2, states, *side_out = gdn_scan_fwd(qkv, gb, grow, f"{tag}_scan", side)
    y = gdn_post_fwd(o2, proj, nw, f"{tag}_post")
    xn = mm(y, w_out, name=f"{tag}_out", epi=_add, extras=(x,))
    return xn, (h, proj, qkv, prm, gb, grow, o2, states, y), side_out


def gdn_bwd(x, g, w_all, convw, nw, w_out, saved, dx, dxb, tag, side=None):
    S = x.shape[0]
    h, proj, qkv, prm, gb, grow, o2, states, y = saved
    dw_out = mm(y, dxb, name=f"{tag}_dwout", ta=True)
    dy = mm(dxb, w_out, name=f"{tag}_dy", tb=True)
    do, dz, dnw = gdn_post_bwd(o2, proj, nw, dy, f"{tag}_postb")
    dqkv, dgate, *side_out = gdn_scan_bwd(qkv, gb, grow, states, do, f"{tag}_scanb", side)
    dgb = jnp.transpose(dgate, (2, 3, 0, 1)).reshape(S, 4 * GDN_H)
    dgb = jnp.pad(dgb, ((0, 0), (0, LANES - 4 * GDN_H)))
    dab, dprm = gdn_gate_bwd(proj, prm, dgb, f"{tag}_gateb")
    dpq, dconvw = gdn_pre_bwd(proj, convw, dqkv, f"{tag}_preb")
    dproj = jnp.concatenate([dpq, dz, dab], axis=1)
    dw_all = mm(h, dproj, name=f"{tag}_dwin", ta=True, tn=384)
    dh = mm(dproj, w_all, name=f"{tag}_dh", tb=True, tk=1408)
    dx, dxb, dg = rms_bwd(x, g, dh, dx, f"{tag}_rmsb")
    da_log = dprm[0, :2 * GDN_H].reshape(2, GDN_H)
    ddt = dprm[1, :2 * GDN_H].reshape(2, GDN_H)
    return dx, dxb, dg, dw_all, dconvw, da_log, ddt, dnw, dw_out, side_out


def _rel_bucket_np(rel):
    nb = REL_BUCKETS // 2
    max_exact = nb // 2
    ret = np.where(rel > 0, nb, 0)
    n = np.abs(rel)
    nf = np.maximum(n, 1).astype(np.float32)
    large = max_exact + (np.log(nf / max_exact) / np.float32(math.log(REL_MAX_DIST / max_exact))
                         * (nb - max_exact)).astype(np.int32)
    large = np.minimum(large, nb - 1)
    return ret + np.where(n < max_exact, n, large)


def _dswa_qb(L):
    return min(256, L)


def _toeplitz(f, rows, cols):
    period = rows + cols
    e = jnp.pad(f, ((0, 0), (0, period - f.shape[1])))
    y = jnp.tile(e, (1, rows))[:, :rows * (period - 1)]
    return y.reshape(f.shape[0], rows, period - 1)[:, :, :cols]


def _bias_mats(rel_table, gi, L):
    _, dil = DSWA_CFG[gi]
    qb = _dswa_qb(L)
    offs = np.arange(-DSWA_HALF, DSWA_HALF + 1)
    onehot = jnp.asarray(np.eye(REL_BUCKETS, dtype=np.float32)[_rel_bucket_np(offs * dil)])
    f = jnp.dot(onehot, rel_table, precision=HI)[:, gi * DSWA_HG:(gi + 1) * DSWA_HG].T
    bias = _toeplitz(f, qb, qb + 2 * DSWA_HALF)
    bias_t = jnp.transpose(_toeplitz(f[:, ::-1], qb, qb + 2 * DSWA_HALF), (0, 2, 1))
    return bias, bias_t


def _win_specs(qb, L, width, major):
    m = qb // DSWA_HALF
    last = L // DSWA_HALF - 1

    def prev(*ids):
        s, b = major(*ids)
        return (s, jnp.maximum(b * m - 1, 0), 0)

    def cur(*ids):
        s, b = major(*ids)
        return (s, b, 0)

    def nxt(*ids):
        s, b = major(*ids)
        return (s, jnp.minimum((b + 1) * m, last), 0)

    return [BS((1, DSWA_HALF, width), prev), BS((1, qb, width), cur), BS((1, DSWA_HALF, width), nxt)]


def _window(p_ref, c_ref, n_ref):
    return jnp.concatenate([p_ref[0], c_ref[0], n_ref[0]], axis=0)


def _band_valid(qb, b, L, transposed):
    shape = (qb + 2 * DSWA_HALF, qb) if transposed else (qb, qb + 2 * DSWA_HALF)
    blk = lax.broadcasted_iota(jnp.int32, shape, 1 if transposed else 0)
    win = lax.broadcasted_iota(jnp.int32, shape, 0 if transposed else 1)
    off = win - DSWA_HALF - blk
    pos = b * qb - DSWA_HALF + win
    return (jnp.abs(off) <= DSWA_HALF) & (pos >= 0) & (pos < L)


def attn_fwd(q, k, v, bias, dil, name):
    NS, L, E = q.shape
    qb = _dswa_qb(L)

    def major(s, b):
        return s, b

    win = _win_specs(qb, L, E, major)

    def body(q_ref, kp, kc, kn, vp, vc, vn, b_ref, o_ref, lse_ref):
        b = pl.program_id(1)
        kw = _window(kp, kc, kn)
        vw = _window(vp, vc, vn)
        s = _dot_nt(q_ref[0], kw) * (E ** -0.5) + b_ref[0]
        s = jnp.where(_band_valid(qb, b, L, False), s, NEG_INF)
        m = jnp.max(s, axis=-1, keepdims=True)
        p = jnp.exp(s - m)
        l = jnp.sum(p, axis=-1, keepdims=True)
        lse_ref[0] = m + jnp.log(l)
        o_ref[0] = _dot(_bf(p / l), vw)

    return pl.pallas_call(
        body, grid=(NS, L // qb),
        in_specs=[win[1]] + win + win + [BS((1, qb, qb + 2 * DSWA_HALF), lambda s, b: (s // dil, 0, 0))],
        out_specs=[BS((1, qb, E), lambda s, b: (s, b, 0)), BS((1, qb, 1), lambda s, b: (s, b, 0))],
        out_shape=[SDS((NS, L, E), F32), SDS((NS, L, 1), F32)],
        name=name, compiler_params=_cp(("parallel", "parallel")))(q, k, k, k, v, v, v, bias)


def attn_bwd_q(q, k, v, bias, lse, do, dd, dil, name):
    NS, L, E = q.shape
    qb = _dswa_qb(L)
    nb = L // qb

    def major(h, r, b):
        return h * dil + r, b

    win = _win_specs(qb, L, E, major)
    col = BS((1, qb, 1), lambda h, r, b: (h * dil + r, b, 0))

    def body(q_ref, kp, kc, kn, vp, vc, vn, b_ref, lse_ref, do_ref, dd_ref, dq_ref, db_ref):
        b = pl.program_id(2)
        first = (pl.program_id(1) == 0) & (b == 0)
        kw = _window(kp, kc, kn)
        vw = _window(vp, vc, vn)
        valid = _band_valid(qb, b, L, False)
        s = _dot_nt(q_ref[0], kw) * (E ** -0.5) + b_ref[0]
        p = jnp.exp(jnp.where(valid, s - lse_ref[0], NEG_INF))
        ds = p * (_dot_nt(do_ref[0], vw) - dd_ref[0])
        dq_ref[0] = (_dot(_bf(ds), kw) * (E ** -0.5)).astype(BF16)

        @pl.when(first)
        def _():
            db_ref[0] = ds

        @pl.when(jnp.logical_not(first))
        def _():
            db_ref[0] += ds

    bspec = BS((1, qb, qb + 2 * DSWA_HALF), lambda h, r, b: (h, 0, 0))
    return pl.pallas_call(
        body, grid=(DSWA_HG, dil, nb),
        in_specs=[win[1]] + win + win + [bspec, col, win[1], col],
        out_specs=[win[1], bspec],
        out_shape=[SDS((NS, L, E), BF16), SDS((DSWA_HG, qb, qb + 2 * DSWA_HALF), F32)],
        name=name, compiler_params=_cp(("parallel", "arbitrary", "arbitrary")))(q, k, k, k, v, v, v, bias, lse, do, dd)


def attn_bwd_kv(q, k, v, bias_t, lse, do, dd, dil, name):
    NS, L, E = q.shape
    qb = _dswa_qb(L)

    def major(s, b):
        return s, b

    win = _win_specs(qb, L, E, major)
    wcol = _win_specs(qb, L, 1, major)

    def body(kc, vc, qp, qc, qn, dop, doc, don, lp, lc, ln, ddp, ddc, ddn, b_ref, dk_ref, dv_ref):
        b = pl.program_id(1)
        qw = _window(qp, qc, qn)
        dow = _window(dop, doc, don)
        lw = _window(lp, lc, ln)
        ddw = _window(ddp, ddc, ddn)
        valid = _band_valid(qb, b, L, True)
        s = _dot_nt(qw, kc[0]) * (E ** -0.5) + b_ref[0]
        p = jnp.exp(jnp.where(valid, s - lw, NEG_INF))
        dv_ref[0] = _dot_tn(_bf(p), dow).astype(BF16)
        ds = jnp.where(valid, p * (_dot_nt(dow, vc[0]) - ddw), 0.0)
        dk_ref[0] = (_dot_tn(_bf(ds), qw) * (E ** -0.5)).astype(BF16)

    return pl.pallas_call(
        body, grid=(NS, L // qb),
        in_specs=[win[1], win[1]] + win + win + wcol + wcol + [BS((1, qb + 2 * DSWA_HALF, qb), lambda s, b: (s // dil, 0, 0))],
        out_specs=[win[1], win[1]],
        out_shape=[SDS((NS, L, E), BF16), SDS((NS, L, E), BF16)],
        name=name, compiler_params=_cp(("parallel", "parallel")))(k, v, q, q, q, do, do, do, lse, lse, lse, dd, dd, dd, bias_t)


def _head_expand():
    i = lax.broadcasted_iota(jnp.int32, (LANES, DSWA_HG * DSWA_E), 0)
    j = lax.broadcasted_iota(jnp.int32, (LANES, DSWA_HG * DSWA_E), 1)
    return jnp.where(i == j // DSWA_E, 1.0, 0.0).astype(F32)


def _group_alphas(lse3):
    m = jnp.maximum(jnp.maximum(lse3[0], lse3[1]), lse3[2])
    e = [jnp.exp(t - m) for t in lse3]
    tot = e[0] + e[1] + e[2]
    return [t / tot for t in e]


def combine_fwd(o_raw, lse3, name):
    S = o_raw.shape[0]
    tm = min(512, S)
    gw = DSWA_HG * DSWA_E

    def body(o_ref, l_ref, y_ref):
        alphas = _group_alphas([l_ref[0], l_ref[1], l_ref[2]])
        ex = _head_expand()
        for gi in range(3):
            cols = slice(gi * gw, (gi + 1) * gw)
            y_ref[:, cols] = (o_ref[:, cols] * _dot(alphas[gi], ex, HI)).astype(BF16)

    return pl.pallas_call(
        body, grid=(S // tm,),
        in_specs=[BS((tm, DSWA_W), lambda i: (i, 0)), BS((3, tm, LANES), lambda i: (0, i, 0))],
        out_specs=BS((tm, DSWA_W), lambda i: (i, 0)),
        out_shape=SDS((S, DSWA_W), BF16), name=name, compiler_params=_cp(("parallel",)))(o_raw, lse3)


def combine_bwd(o_raw, lse3, dy, name):
    S = o_raw.shape[0]
    tm = min(512, S)
    gw = DSWA_HG * DSWA_E

    def body(o_ref, l_ref, dy_ref, do_ref, dd_ref):
        alphas = _group_alphas([l_ref[0], l_ref[1], l_ref[2]])
        ex = _head_expand()
        dal = []
        for gi in range(3):
            cols = slice(gi * gw, (gi + 1) * gw)
            dyv = dy_ref[:, cols]
            do_ref[:, cols] = (dyv * _dot(alphas[gi], ex, HI)).astype(BF16)
            dal.append(_dot_nt(o_ref[:, cols] * dyv, ex, HI))
        c = alphas[0] * dal[0] + alphas[1] * dal[1] + alphas[2] * dal[2]
        for gi in range(3):
            dd_ref[gi] = alphas[gi] * c

    return pl.pallas_call(
        body, grid=(S // tm,),
        in_specs=[BS((tm, DSWA_W), lambda i: (i, 0)), BS((3, tm, LANES), lambda i: (0, i, 0)), BS((tm, DSWA_W), lambda i: (i, 0))],
        out_specs=[BS((tm, DSWA_W), lambda i: (i, 0)), BS((3, tm, LANES), lambda i: (0, i, 0))],
        out_shape=[SDS((S, DSWA_W), BF16), SDS((3, S, LANES), F32)],
        name=name, compiler_params=_cp(("parallel",)))(o_raw, lse3, dy)


def _to_sub(t, dil):
    S, hg, wd = t.shape
    return jnp.transpose(t.reshape(S // dil, dil, hg, wd), (2, 1, 0, 3)).reshape(hg * dil, S // dil, wd)


def _from_sub(t, dil):
    ns, L, wd = t.shape
    hg = ns // dil
    return jnp.transpose(t.reshape(hg, dil, L, wd), (2, 1, 0, 3)).reshape(L * dil, hg, wd)


def dswa_fwd(x, g, w_in, w_out, rel_table, tag):
    S = x.shape[0]
    h = rms_fwd(x, g, f"{tag}_rms")
    qkv = mm(h, w_in, name=f"{tag}_qkv", tn=1152, out_dtype=BF16).reshape(S, 3, DSWA_HEADS, DSWA_E)
    subs, outs, lses = [], [], []
    for gi, (_, dil) in enumerate(DSWA_CFG):
        hs = slice(gi * DSWA_HG, (gi + 1) * DSWA_HG)
        qs, ks, vs = (_to_sub(qkv[:, i, hs], dil) for i in range(3))
        bias, bias_t = _bias_mats(rel_table, gi, S // dil)
        o, lse = attn_fwd(qs, ks, vs, bias, dil, f"{tag}_att{gi}")
        subs.append((qs, ks, vs, lse))
        outs.append(_from_sub(o, dil).reshape(S, DSWA_HG * DSWA_E))
        lses.append(jnp.pad(_from_sub(lse, dil).reshape(S, DSWA_HG), ((0, 0), (0, LANES - DSWA_HG))))
    o_raw = jnp.concatenate(outs, axis=1)
    lse3 = jnp.stack(lses)
    y = combine_fwd(o_raw, lse3, f"{tag}_comb")
    xn = mm(y, w_out, name=f"{tag}_out", epi=_add, extras=(x,))
    return xn, (h, subs, o_raw, lse3, y)


def dswa_bwd(x, g, w_in, w_out, rel_table, saved, dx, dxb, tag):
    S = x.shape[0]
    h, subs, o_raw, lse3, y = saved
    dw_out = mm(y, dxb, name=f"{tag}_dwout", ta=True, tm=384)
    dy = mm(dxb, w_out, name=f"{tag}_dy", tb=True, tn=384)
    do_raw, dd3 = combine_bwd(o_raw, lse3, dy, f"{tag}_combb")
    do_raw = do_raw.reshape(S, DSWA_HEADS, DSWA_E)
    dqkv = []
    drel = jnp.zeros_like(rel_table)
    for gi, (_, dil) in enumerate(DSWA_CFG):
        hs = slice(gi * DSWA_HG, (gi + 1) * DSWA_HG)
        qs, ks, vs, lse = subs[gi]
        dos = _to_sub(do_raw[:, hs], dil)
        dds = _to_sub(dd3[gi, :, :DSWA_HG, None], dil)
        (bias, bias_t), bias_vjp = jax.vjp(lambda tbl: _bias_mats(tbl, gi, S // dil), rel_table)
        dq, dbias = attn_bwd_q(qs, ks, vs, bias, lse, dos, dds, dil, f"{tag}_attq{gi}")
        dk, dv = attn_bwd_kv(qs, ks, vs, bias_t, lse, dos, dds, dil, f"{tag}_attkv{gi}")
        drel = drel + bias_vjp((dbias, jnp.zeros_like(bias_t)))[0]
        dqkv.append([_from_sub(t, dil) for t in (dq, dk, dv)])
    dqkv = jnp.stack([jnp.concatenate([dqkv[gi][i] for gi in range(3)], axis=1) for i in range(3)], axis=1)
    dqkv = dqkv.reshape(S, 3 * DSWA_W)
    dw_in = mm(h, dqkv, name=f"{tag}_dwin", ta=True, tn=384)
    dh = mm(dqkv, w_in, name=f"{tag}_dh", tb=True, tk=1152)
    dx, dxb, dg = rms_bwd(x, g, dh, dx, f"{tag}_rmsb")
    return dx, dxb, dg, dw_in, dw_out, drel


def adamw(w, g, m, v, name):
    shape = w.shape
    last = shape[-1]
    w2, g2, m2, v2 = (t.reshape(-1, last) for t in (w, g, m, v))
    rows = w2.shape[0]
    tr = rows
    if rows > 512:
        tr = next(t for t in (512, 256, 192, 128, 64, 8) if rows % t == 0)
    c1 = 1.0 / (1.0 - ADAM_B1 ** ADAM_STEP)
    c2 = 1.0 / (1.0 - ADAM_B2 ** ADAM_STEP)

    def body(w_ref, g_ref, m_ref, v_ref, d_ref, nm_ref, nv_ref):
        gv = g_ref[...]
        nm = ADAM_B1 * m_ref[...] + (1.0 - ADAM_B1) * gv
        nv = ADAM_B2 * v_ref[...] + (1.0 - ADAM_B2) * (gv * gv)
        nm_ref[...] = nm
        nv_ref[...] = nv
        d_ref[...] = -ADAM_LR * ((nm * c1) / (jnp.sqrt(nv * c2) + ADAM_EPS) + ADAM_WD * w_ref[...])

    spec = BS((tr, last), lambda i: (i, 0))
    outs = pl.pallas_call(
        body, grid=(rows // tr,), in_specs=[spec] * 4, out_specs=[spec] * 3,
        out_shape=[SDS((rows, last), F32)] * 3, name=name, compiler_params=_cp(("parallel",)))(w2, g2, m2, v2)
    return tuple(o.reshape(shape) for o in outs)


def _place():
    x, y, c = lax.axis_index("x"), lax.axis_index("y"), lax.axis_index("c")
    chips = [(1 - x, y), (x, 1 - y), (1 - x, 1 - y)]
    return x, y, c, chips


def _rcopy(src, dst, ssem, rsem, dev):
    return pltpu.make_async_remote_copy(src_ref=src, dst_ref=dst, send_sem=ssem, recv_sem=rsem, device_id=dev,
                                        device_id_type=MESH)


class SideJob(NamedTuple):
    ins: list
    outs: list
    sems: list
    start: Callable
    wait: Callable


def _job(ins, outs, sems, copies):
    def start(in_refs, out_refs, sem_refs):
        for cp in copies(in_refs, out_refs, sem_refs):
            cp.start()

    def wait(in_refs, out_refs, sem_refs):
        for cp in copies(in_refs, out_refs, sem_refs):
            cp.wait()

    return SideJob(list(ins), list(outs), list(sems), start, wait)


def gather_job(packs, halved):
    n = len(packs)
    dma = pltpu.SemaphoreType.DMA

    def copies(in_refs, out_refs, sems):
        ssem, rsem, lsem = sems
        x, y, c, chips = _place()
        jme = 2 * x + y
        cps = []
        for i, (p_ref, f_ref) in enumerate(zip(in_refs, out_refs)):
            rows = p_ref.shape[0]
            mine = pl.ds(c * (rows // 2), rows // 2) if halved[i] else pl.ds(0, rows)
            cps.append(pltpu.make_async_copy(p_ref, f_ref.at[jme], lsem.at[i]))
            for r, (cx, cy) in enumerate(chips):
                cps.append(_rcopy(p_ref.at[mine], f_ref.at[jme, mine], ssem.at[i, r], rsem.at[i, r], (cx, cy, c)))
        return cps

    return _job(packs, [SDS((4,) + p.shape, p.dtype) for p in packs], [dma((n, 3)), dma((n, 3)), dma((n,))], copies)


def chip_exchange_job(parts):
    n = len(parts)
    dma = pltpu.SemaphoreType.DMA

    def copies(in_refs, out_refs, sems):
        ssem, rsem, lsem = sems
        x, y, c, chips = _place()
        cps = []
        for i, (p_ref, r_ref) in enumerate(zip(in_refs, out_refs)):
            cps.append(pltpu.make_async_copy(p_ref.at[2 * x + y], r_ref.at[3], lsem.at[i]))
            for r, (cx, cy) in enumerate(chips):
                cps.append(_rcopy(p_ref.at[2 * cx + cy], r_ref.at[r], ssem.at[i, r], rsem.at[i, r], (cx, cy, c)))
        return cps

    return _job(parts, [SDS(p.shape, p.dtype) for p in parts], [dma((n, 3)), dma((n, 3)), dma((n,))], copies)


def run_job(job, name):
    ni, no = len(job.ins), len(job.outs)

    def body(*refs):
        job.start(refs[:ni], refs[ni:ni + no], refs[ni + no:])
        job.wait(refs[:ni], refs[ni:ni + no], refs[ni + no:])

    return pl.pallas_call(
        body, in_specs=[ANY] * ni, out_specs=[ANY] * no, out_shape=job.outs, scratch_shapes=job.sems, name=name,
        compiler_params=pltpu.CompilerParams(has_side_effects=True))(*job.ins)


def forward_to_sibling(fulls, name):
    n = len(fulls)

    def body(*refs):
        in_refs, out_refs, (ssem, rsem) = refs[:n], refs[n:2 * n], refs[2 * n:]
        x, y, c, chips = _place()
        cps = []
        for i in range(n):
            half = in_refs[i].shape[1] // 2
            for r, (cx, cy) in enumerate(chips):
                piece = (2 * cx + cy, pl.ds(c * half, half))
                cps.append(_rcopy(in_refs[i].at[piece], out_refs[i].at[piece], ssem.at[i, r], rsem.at[i, r], (x, y, 1 - c)))
        for cp in cps:
            cp.start()
        for cp in cps:
            cp.wait()

    dma = pltpu.SemaphoreType.DMA
    return pl.pallas_call(
        body, in_specs=[ANY] * n, out_specs=[ANY] * n, out_shape=[SDS(f.shape, f.dtype) for f in fulls],
        scratch_shapes=[dma((n, 3)), dma((n, 3))], input_output_aliases={i: i for i in range(n)}, name=name,
        compiler_params=pltpu.CompilerParams(has_side_effects=True))(*fulls)


def rs_sibling_exchange(gpack, name):
    _, rows, W = gpack.shape
    half = rows // 2

    def body(g_ref, r_ref, ssem, rsem):
        x, y, c, _ = _place()
        cps = [_rcopy(g_ref.at[j, pl.ds((1 - c) * half, half)], r_ref.at[j], ssem.at[j], rsem.at[j], (x, y, 1 - c))
               for j in range(4)]
        for cp in cps:
            cp.start()
        for cp in cps:
            cp.wait()

    dma = pltpu.SemaphoreType.DMA
    return pl.pallas_call(
        body, in_specs=[ANY], out_specs=ANY, out_shape=SDS((4, half, W), gpack.dtype),
        scratch_shapes=[dma((4,)), dma((4,))], name=name,
        compiler_params=pltpu.CompilerParams(has_side_effects=True))(gpack)


def _div_tile(n, limit):
    return next(t for t in range(limit - limit % 8, 0, -8) if n % t == 0)


def rs_add_sibling(gpack, recv, cidx, name):
    _, rows, W = gpack.shape
    half = rows // 2
    tr = _div_tile(half, 1024)
    nb = half // tr

    def body(c_ref, g_ref, r_ref, o_ref):
        o_ref[...] = g_ref[...] + r_ref[...]

    gs = pltpu.PrefetchScalarGridSpec(
        num_scalar_prefetch=1, grid=(4, nb),
        in_specs=[BS((1, tr, W), lambda j, i, c: (j, c[0] * nb + i, 0)), BS((1, tr, W), lambda j, i, c: (j, i, 0))],
        out_specs=BS((1, tr, W), lambda j, i, c: (j, i, 0)))
    return pl.pallas_call(body, grid_spec=gs, out_shape=SDS((4, half, W), F32), name=name,
                          compiler_params=_cp(("parallel", "parallel")))(cidx, gpack, recv)


def rs_add_chips(recv, name):
    _, half, W = recv.shape
    tr = _div_tile(half, 640)

    def body(r_ref, o_ref):
        o_ref[...] = ((r_ref[0] + r_ref[1]) + r_ref[2]) + r_ref[3]

    return pl.pallas_call(
        body, grid=(half // tr,), in_specs=[BS((4, tr, W), lambda i: (0, i, 0))], out_specs=BS((tr, W), lambda i: (i, 0)),
        out_shape=SDS((half, W), F32), name=name, compiler_params=_cp(("parallel",)))(recv)


def rs_sibling_share(fin, name):
    half, W = fin.shape

    def body(f_ref, o_ref, ssem, rsem, lsem):
        x, y, c, _ = _place()
        mine = o_ref.at[pl.ds(c * half, half)]
        own = pltpu.make_async_copy(f_ref, mine, lsem)
        cp = _rcopy(f_ref, mine, ssem, rsem, (x, y, 1 - c))
        own.start()
        cp.start()
        cp.wait()
        own.wait()

    dma = pltpu.SemaphoreType.DMA
    return pl.pallas_call(
        body, in_specs=[ANY], out_specs=ANY, out_shape=SDS((2 * half, W), fin.dtype),
        scratch_shapes=[dma, dma, dma], name=name,
        compiler_params=pltpu.CompilerParams(has_side_effects=True))(fin)


def allreduce_small(pack):
    R = pack.shape[0]

    def body(p_ref, o_ref, all_ref, ssem, rsem):
        x, y, c, _ = _place()
        me = 4 * x + 2 * y + c
        all_ref[me] = p_ref[...]
        cps = []
        for m in range(1, 8):
            peer = (1 - x if m & 4 else x, 1 - y if m & 2 else y, 1 - c if m & 1 else c)
            cp = _rcopy(p_ref, all_ref.at[me], ssem.at[m - 1], rsem.at[m - 1], peer)
            cp.start()
            cps.append(cp)
        for cp in cps:
            cp.wait()
        acc = all_ref[0]
        for i in range(1, 8):
            acc = acc + all_ref[i]
        o_ref[...] = acc

    dma = pltpu.SemaphoreType.DMA
    vm = BS(memory_space=pltpu.VMEM)
    return pl.pallas_call(
        body, in_specs=[vm], out_specs=vm, out_shape=SDS(pack.shape, F32),
        scratch_shapes=[pltpu.VMEM((8, R, LANES), F32), dma((7,)), dma((7,))], name="allreduce_small",
        compiler_params=pltpu.CompilerParams(has_side_effects=True))(pack)


PACK_W = 1024
PACK_ALIGN = 32


def _layer_entries(l):
    if l % 2 == 0:
        mixer = [("gdn_w_in", l // 2, D_MODEL, GDN_IN // 4, True), ("gdn_w_out", l // 2, D_MODEL // 4, D_MODEL, False)]
    else:
        mixer = [("dswa_w_in", l // 2, D_MODEL, 3 * DSWA_W // 4, True), ("dswa_w_out", l // 2, DSWA_W // 4, D_MODEL, False)]
    return mixer + [("mlp_w1", l, D_MODEL, D_FF // 4, True), ("mlp_w2", l, D_FF // 4, D_MODEL, False)]


def _layer_offsets(l):
    offs = [int(o) for o in np.cumsum([0] + [r * c // PACK_W for (_, _, r, c, _) in _layer_entries(l)])]
    return offs, -(-offs[-1] // PACK_ALIGN) * PACK_ALIGN


def _pack_layer(l, shards, dtype):
    offs, total = _layer_offsets(l)
    parts = [shards[name][li].astype(dtype).reshape(-1, PACK_W) for (name, li, _, _, _) in _layer_entries(l)]
    parts.append(jnp.zeros((total - offs[-1], PACK_W), dtype))
    return jnp.concatenate(parts, axis=0)


def _unpack_layer(l, full):
    offs, _ = _layer_offsets(l)
    mats = []
    for e, (_, _, r, c, by_col) in enumerate(_layer_entries(l)):
        sh = full[:, offs[e]:offs[e + 1]].reshape(4, r, c)
        mats.append(jnp.concatenate([sh[j] for j in range(4)], axis=1 if by_col else 0))
    return mats


def _pack_layer_grads(l, grads):
    offs, total = _layer_offsets(l)
    per_chip = []
    for j in range(4):
        parts = []
        for g, (_, _, r, c, by_col) in zip(grads, _layer_entries(l)):
            sh = g[:, c * j:c * (j + 1)] if by_col else g[r * j:r * (j + 1), :]
            parts.append(sh.reshape(-1, PACK_W))
        parts.append(jnp.zeros((total - offs[-1], PACK_W), F32))
        per_chip.append(jnp.concatenate(parts, axis=0))
    return jnp.stack(per_chip)


def _unpack_shard_grads(gshs):
    out = {}
    for l, gsh in enumerate(gshs):
        offs, _ = _layer_offsets(l)
        for e, (name, _, r, c, _) in enumerate(_layer_entries(l)):
            out.setdefault(name, []).append(gsh[offs[e]:offs[e + 1]].reshape(r, c))
    return {k: jnp.stack(v) for k, v in out.items()}


def _flat_pad(t, mult=8 * LANES):
    f = t.reshape(-1)
    return jnp.pad(f, (0, (-f.shape[0]) % mult))


def kernel(x, norm_mix, norm_mlp, norm_final, rel_bias, gdn_w_in, gdn_conv_w, gdn_a_log, gdn_dt_bias, gdn_norm_w, gdn_w_out, dswa_w_in, dswa_w_out, mlp_w1, mlp_w2, loss_target, m_norm_mix, m_norm_mlp, m_norm_final, m_rel_bias, m_gdn_w_in, m_gdn_conv_w, m_gdn_a_log, m_gdn_dt_bias, m_gdn_norm_w, m_gdn_w_out, m_dswa_w_in, m_dswa_w_out, m_mlp_w1, m_mlp_w2, v_norm_mix, v_norm_mlp, v_norm_final, v_rel_bias, v_gdn_w_in, v_gdn_conv_w, v_gdn_a_log, v_gdn_dt_bias, v_gdn_norm_w, v_gdn_w_out, v_dswa_w_in, v_dswa_w_out, v_mlp_w1, v_mlp_w2):
    xi, yi, ci = lax.axis_index("x"), lax.axis_index("y"), lax.axis_index("c")
    jme = 2 * xi + yi
    big = dict(gdn_w_in=gdn_w_in, gdn_w_out=gdn_w_out, dswa_w_in=dswa_w_in, dswa_w_out=dswa_w_out, mlp_w1=mlp_w1, mlp_w2=mlp_w2)
    n_gdn = gdn_w_in.shape[0]
    conv_cols = gdn_conv_w.shape[-1]

    packs = [_pack_layer(l, big, BF16) for l in range(DEPTH)]
    convp = jnp.pad(gdn_conv_w.reshape(n_gdn * GDN_CONV, conv_cols), ((0, 16 - n_gdn * GDN_CONV), (0, 0)))
    raw0, cfull = run_job(gather_job([packs[0], convp], [True, False]), "gather_l0")
    fulls = {0: forward_to_sibling([raw0], "forward_l0")[0]}
    conv_all = jnp.transpose(cfull[:, :n_gdn * GDN_CONV], (1, 0, 2)).reshape(n_gdn, GDN_CONV, 4 * conv_cols)
    conv_all = jnp.pad(conv_all, ((0, 0), (0, 8 - GDN_CONV), (0, 0)))
    fwd_jobs = {0: [1, 2], 2: [3]}

    xs = x[0]
    saved = []
    for l in range(DEPTH):
        w_in, w_out, w1, w2 = _unpack_layer(l, fulls[l])
        gm, gp = norm_mix[l][None], norm_mlp[l][None]
        a = l // 2
        if l % 2 == 0:
            w_in = jnp.pad(w_in, ((0, 0), (0, GDN_INP - GDN_IN)))
            job = gather_job([packs[t] for t in fwd_jobs[l]], [True] * len(fwd_jobs[l]))
            x_mid, sv, raws = gdn_fwd(xs, gm, w_in, conv_all[a], gdn_a_log[a], gdn_dt_bias[a], gdn_norm_w[a][None], w_out,
                                      f"l{l}_gdn", job)
            for t, f in zip(fwd_jobs[l], forward_to_sibling(raws, f"forward_from_l{l}")):
                fulls[t] = f
        else:
            x_mid, sv = dswa_fwd(xs, gm, w_in, w_out, rel_bias, f"l{l}_att")
        x_out, sv2 = mlp_fwd(x_mid, gp, w1, w2, f"l{l}_mlp")
        saved.append((xs, x_mid, (w_in, w_out, w1, w2), sv, sv2))
        xs = x_out

    cidx = ci.astype(jnp.int32).reshape(1)

    def chip_partial(l, grads4):
        gpack = _pack_layer_grads(l, grads4)
        return rs_add_sibling(gpack, rs_sibling_exchange(gpack, f"rs_sibling_l{l}"), cidx, f"rs_add_sibling_l{l}")

    def finish(l, recv):
        return rs_sibling_share(rs_add_chips(recv, f"rs_add_chips_l{l}"), f"rs_share_l{l}")

    loss_part, dx, dxb, d_final = loss_head(xs, norm_final[None], loss_target[0], "loss_head")
    d_mix, d_mlp = [None] * DEPTH, [None] * DEPTH
    d_conv, d_alog, d_dt, d_nw = [None] * n_gdn, [None] * n_gdn, [None] * n_gdn, [None] * n_gdn
    d_rel = jnp.zeros_like(rel_bias)
    parts, gshs = {}, [None] * DEPTH
    bwd_jobs = {2: [3], 0: [2, 1]}
    for l in reversed(range(DEPTH)):
        x_in, x_mid, (w_in, w_out, w1, w2), sv, sv2 = saved[l]
        gm, gp = norm_mix[l][None], norm_mlp[l][None]
        a = l // 2
        dx, dxb, d_mlp[l], dw1, dw2 = mlp_bwd(x_mid, gp, w1, w2, sv2, dx, dxb, f"l{l}_mlp")
        if l % 2 == 0:
            job = chip_exchange_job([parts[t] for t in bwd_jobs[l]])
            dx, dxb, d_mix[l], dw_all, d_conv[a], d_alog[a], d_dt[a], d_nw[a], dwo, recvs = gdn_bwd(
                x_in, gm, w_in, conv_all[a], gdn_norm_w[a][None], w_out, sv, dx, dxb, f"l{l}_gdn", job)
            for t, rv in zip(bwd_jobs[l], recvs):
                gshs[t] = finish(t, rv)
            dwi = dw_all[:, :GDN_IN]
        else:
            dx, dxb, d_mix[l], dwi, dwo, drel = dswa_bwd(x_in, gm, w_in, w_out, rel_bias, sv, dx, dxb, f"l{l}_att")
            d_rel = d_rel + drel
        parts[l] = chip_partial(l, [dwi, dwo, dw1, dw2])
    gshs[0] = finish(0, run_job(chip_exchange_job([parts[0]]), "rs_chip_exchange_l0")[0])
    gbig = _unpack_shard_grads(gshs)

    small = [jnp.concatenate(d_mix, axis=0), jnp.concatenate(d_mlp, axis=0), d_final, d_rel,
             jnp.stack(d_conv), jnp.stack(d_alog), jnp.stack(d_dt), jnp.concatenate(d_nw, axis=0)]
    flat = [_flat_pad(t) for t in small]
    sizes = [f.shape[0] for f in flat]
    red = allreduce_small(jnp.concatenate(flat).reshape(-1, LANES)).reshape(-1)
    offs = np.cumsum([0] + sizes)
    red = [red[offs[i]:offs[i] + small[i].size].reshape(small[i].shape) for i in range(len(small))]
    g_conv_all = red[4][:, :GDN_CONV].reshape(n_gdn, GDN_CONV, 1, 4 * conv_cols)
    g_conv = lax.dynamic_slice_in_dim(g_conv_all, jme * conv_cols, conv_cols, axis=3)
    g = dict(norm_mix=red[0], norm_mlp=red[1], norm_final=red[2].reshape(norm_final.shape), rel_bias=red[3],
             gdn_conv_w=g_conv, gdn_a_log=red[5], gdn_dt_bias=red[6], gdn_norm_w=red[7][:, :GDN_DK], **gbig)

    w = dict(norm_mix=norm_mix, norm_mlp=norm_mlp, norm_final=norm_final, rel_bias=rel_bias, gdn_conv_w=gdn_conv_w,
             gdn_a_log=gdn_a_log, gdn_dt_bias=gdn_dt_bias, gdn_norm_w=gdn_norm_w, **big)
    m = dict(norm_mix=m_norm_mix, norm_mlp=m_norm_mlp, norm_final=m_norm_final, rel_bias=m_rel_bias, gdn_w_in=m_gdn_w_in,
             gdn_conv_w=m_gdn_conv_w, gdn_a_log=m_gdn_a_log, gdn_dt_bias=m_gdn_dt_bias, gdn_norm_w=m_gdn_norm_w,
             gdn_w_out=m_gdn_w_out, dswa_w_in=m_dswa_w_in, dswa_w_out=m_dswa_w_out, mlp_w1=m_mlp_w1, mlp_w2=m_mlp_w2)
    v = dict(norm_mix=v_norm_mix, norm_mlp=v_norm_mlp, norm_final=v_norm_final, rel_bias=v_rel_bias, gdn_w_in=v_gdn_w_in,
             gdn_conv_w=v_gdn_conv_w, gdn_a_log=v_gdn_a_log, gdn_dt_bias=v_gdn_dt_bias, gdn_norm_w=v_gdn_norm_w,
             gdn_w_out=v_gdn_w_out, dswa_w_in=v_dswa_w_in, dswa_w_out=v_dswa_w_out, mlp_w1=v_mlp_w1, mlp_w2=v_mlp_w2)
    names = ["norm_mix", "norm_mlp", "norm_final", "rel_bias", "gdn_w_in", "gdn_conv_w", "gdn_a_log", "gdn_dt_bias",
             "gdn_norm_w", "gdn_w_out", "dswa_w_in", "dswa_w_out", "mlp_w1", "mlp_w2"]
    upd = {n: adamw(w[n], g[n], m[n], v[n], f"adamw_{n}") for n in names}
    loss = lax.psum(loss_part[0, 0], ("x", "y", "c"))
    return (loss, dx[None], *[g[n] for n in names], *[upd[n][0] for n in names], *[upd[n][1] for n in names],
            *[upd[n][2] for n in names])
```

```python
import math
from typing import Callable, NamedTuple

import numpy as np
import jax
import jax.numpy as jnp
from jax import lax
from jax.experimental import pallas as pl
from jax.experimental.pallas import tpu as pltpu

F32 = jnp.float32
BF16 = jnp.bfloat16
HI = lax.Precision.HIGHEST
BS = pl.BlockSpec
SDS = jax.ShapeDtypeStruct
MESH = pl.DeviceIdType.MESH
ANY = BS(memory_space=pl.ANY)

D_MODEL = 1024
D_FF = 4096
DEPTH = 4
RMS_EPS = 1e-6
NEG_INF = -1e30
LANES = 128
VMEM_LIMIT = 56 << 20

GDN_H = 8
GDN_DK = 128
GDN_CONV = 5
GDN_C = 64
GDN_GC = 8
GDN_HP = 4
GDN_QKV = 3 * GDN_H * GDN_DK
GDN_IN = GDN_QKV + GDN_H * GDN_DK + 4 * GDN_H
GDN_INP = 4224

DSWA_CFG = ((128, 1), (512, 4), (2048, 16))
DSWA_HG = 6
DSWA_E = 64
DSWA_HEADS = 18
DSWA_W = DSWA_HEADS * DSWA_E
DSWA_HALF = 64
REL_BUCKETS = 32
REL_MAX_DIST = 1024

ADAM_LR = 0.001
ADAM_B1 = 0.9
ADAM_B2 = 0.999
ADAM_EPS = 1e-08
ADAM_WD = 0.01
ADAM_STEP = 10


def _cp(sem=None):
    return pltpu.CompilerParams(dimension_semantics=sem, vmem_limit_bytes=VMEM_LIMIT)


def _dot(a, b, prec=None):
    return jnp.dot(a, b, precision=prec, preferred_element_type=F32)


def _dot_nt(a, b, prec=None):
    return lax.dot_general(a, b, (((1,), (1,)), ((), ())), precision=prec, preferred_element_type=F32)


def _dot_tn(a, b, prec=None):
    return lax.dot_general(a, b, (((0,), (0,)), ((), ())), precision=prec, preferred_element_type=F32)


def _bf(a):
    return a.astype(BF16)


def _sigmoid(x):
    return 1.0 / (1.0 + jnp.exp(-x))


def rms_fwd(x, g, name):
    S, Dm = x.shape
    tm = min(512, S)

    def body(x_ref, g_ref, o_ref):
        xv = x_ref[...]
        r = lax.rsqrt(jnp.mean(xv * xv, axis=-1, keepdims=True) + RMS_EPS)
        o_ref[...] = (xv * r * g_ref[...]).astype(o_ref.dtype)

    return pl.pallas_call(
        body, grid=(S // tm,),
        in_specs=[BS((tm, Dm), lambda i: (i, 0)), BS((1, Dm), lambda i: (0, 0))],
        out_specs=BS((tm, Dm), lambda i: (i, 0)),
        out_shape=SDS((S, Dm), BF16), name=name, compiler_params=_cp(("parallel",)))(x, g)


def rms_bwd(x, g, dh, dres, name):
    S, Dm = x.shape
    tm = min(512, S)

    def body(x_ref, g_ref, dh_ref, dres_ref, dx_ref, dxb_ref, dg_ref):
        i = pl.program_id(0)
        xv = x_ref[...]
        r = lax.rsqrt(jnp.mean(xv * xv, axis=-1, keepdims=True) + RMS_EPS)
        n = xv * r
        dhv = dh_ref[...]
        t = dhv * g_ref[...]
        dx = dres_ref[...] + r * (t - n * jnp.mean(n * t, axis=-1, keepdims=True))
        dx_ref[...] = dx
        dxb_ref[...] = dx.astype(BF16)
        part = jnp.sum(dhv * n, axis=0, keepdims=True)

        @pl.when(i == 0)
        def _():
            dg_ref[...] = part

        @pl.when(i > 0)
        def _():
            dg_ref[...] += part

    row = BS((tm, Dm), lambda i: (i, 0))
    vec = BS((1, Dm), lambda i: (0, 0))
    return pl.pallas_call(
        body, grid=(S // tm,), in_specs=[row, vec, row, row], out_specs=[row, row, vec],
        out_shape=[SDS((S, Dm), F32), SDS((S, Dm), BF16), SDS((1, Dm), F32)],
        name=name, compiler_params=_cp(("arbitrary",)))(x, g, dh, dres)


def loss_head(x, g, tgt, name):
    S, Dm = x.shape
    tm = min(512, S)

    def body(x_ref, g_ref, t_ref, loss_ref, dx_ref, dxb_ref, dg_ref):
        i = pl.program_id(0)
        xv = x_ref[...]
        gv = g_ref[...]
        r = lax.rsqrt(jnp.mean(xv * xv, axis=-1, keepdims=True) + RMS_EPS)
        n = xv * r
        err = n * gv - t_ref[...]
        lpart = 0.5 * jnp.sum(jnp.mean(err * err, axis=-1, keepdims=True), axis=0, keepdims=True)
        dout = err * (1.0 / Dm)
        t = dout * gv
        dx = r * (t - n * jnp.mean(n * t, axis=-1, keepdims=True))
        dx_ref[...] = dx
        dxb_ref[...] = dx.astype(BF16)
        part = jnp.sum(dout * n, axis=0, keepdims=True)

        @pl.when(i == 0)
        def _():
            dg_ref[...] = part
            loss_ref[...] = lpart

        @pl.when(i > 0)
        def _():
            dg_ref[...] += part
            loss_ref[...] += lpart

    row = BS((tm, Dm), lambda i: (i, 0))
    vec = BS((1, Dm), lambda i: (0, 0))
    one = BS((1, 1), lambda i: (0, 0))
    return pl.pallas_call(
        body, grid=(S // tm,), in_specs=[row, vec, row], out_specs=[one, row, row, vec],
        out_shape=[SDS((1, 1), F32), SDS((S, Dm), F32), SDS((S, Dm), BF16), SDS((1, Dm), F32)],
        name=name, compiler_params=_cp(("arbitrary",)))(x, g, tgt)


def mm(a, b, *, name, ta=False, tb=False, tm=512, tn=512, tk=None, out_dtype=F32, pre_a=None, epi=None,
       extras=()):
    M, K = (a.shape[1], a.shape[0]) if ta else a.shape
    N = b.shape[0] if tb else b.shape[1]
    tm, tn = min(tm, M), min(tn, N)
    tk = K if tk is None else min(tk, K)
    assert M % tm == 0 and N % tn == 0 and K % tk == 0, (name, M, N, K, tm, tn, tk)
    nk = K // tk
    ne = len(extras)
    a_spec = BS((tk, tm), lambda i, j, k: (k, i)) if ta else BS((tm, tk), lambda i, j, k: (i, k))
    b_spec = BS((tn, tk), lambda i, j, k: (j, k)) if tb else BS((tk, tn), lambda i, j, k: (k, j))
    o_spec = BS((tm, tn), lambda i, j, k: (i, j))
    dims = (((0 if ta else 1,), (1 if tb else 0,)), ((), ()))

    def body(a_ref, b_ref, *rest):
        e_refs, o_ref = rest[:ne], rest[ne]
        av = a_ref[...]
        if pre_a is not None:
            av = pre_a(av)
        p = lax.dot_general(_bf(av), _bf(b_ref[...]), dims, preferred_element_type=F32)

        def finish(acc):
            res = epi(acc, *[e[...] for e in e_refs]) if epi is not None else acc
            o_ref[...] = res.astype(o_ref.dtype)

        if nk == 1:
            finish(p)
        else:
            acc_ref = rest[ne + 1]
            k = pl.program_id(2)

            @pl.when(k == 0)
            def _():
                acc_ref[...] = p

            @pl.when(k > 0)
            def _():
                acc_ref[...] += p

            @pl.when(k == nk - 1)
            def _():
                finish(acc_ref[...])

    return pl.pallas_call(
        body, grid=(M // tm, N // tn, nk), in_specs=[a_spec, b_spec] + [o_spec] * ne, out_specs=o_spec,
        out_shape=SDS((M, N), out_dtype),
        scratch_shapes=[pltpu.VMEM((tm, tn), F32)] if nk > 1 else [],
        name=name, compiler_params=_cp(("parallel", "parallel", "arbitrary")))(a, b, *extras)


def _relu(acc):
    return jnp.maximum(acc, 0.0)


def _add(acc, res):
    return acc + res


def _sq(av):
    af = av.astype(F32)
    return af * af


def _times_2r(acc, r):
    return acc * (2.0 * r.astype(F32))


def mlp_fwd(x, g, w1, w2, tag):
    h = rms_fwd(x, g, f"{tag}_rms")
    r = mm(h, w1, name=f"{tag}_up", tn=1024, out_dtype=BF16, epi=_relu)
    xn = mm(r, w2, name=f"{tag}_down", pre_a=_sq, epi=_add, extras=(x,))
    return xn, (h, r)


def mlp_bwd(x, g, w1, w2, saved, dx, dxb, tag):
    h, r = saved
    da = mm(dxb, w2, name=f"{tag}_dact", tb=True, tn=1024, out_dtype=BF16, epi=_times_2r, extras=(r,))
    dw2 = mm(r, dxb, name=f"{tag}_dw2", ta=True, pre_a=_sq)
    dw1 = mm(h, da, name=f"{tag}_dw1", ta=True)
    dh = mm(da, w1, name=f"{tag}_dh", tb=True)
    dx, dxb, dg = rms_bwd(x, g, dh, dx, f"{tag}_rmsb")
    return dx, dxb, dg, dw1, dw2


def _conv_taps(x, S):
    t = lax.broadcasted_iota(jnp.int32, x.shape, 0)
    taps = []
    for j in range(GDN_CONV):
        sh = j - GDN_CONV // 2
        xs = x if sh == 0 else pltpu.roll(x, (-sh) % S, 0)
        taps.append(jnp.where((t + sh >= 0) & (t + sh < S), xs, 0.0))
    return taps


def _qkv_scale(c):
    is_norm = c < 2 * GDN_H
    scale = jnp.where(c < GDN_H, GDN_DK ** -0.5, 1.0)
    return is_norm, scale


def gdn_pre_fwd(proj, convw, name):
    S = proj.shape[0]

    def body(p_ref, w_ref, o_ref):
        c = pl.program_id(0)
        x = p_ref[...]
        w = w_ref[...]
        y = jnp.zeros_like(x)
        for j, xs in enumerate(_conv_taps(x, S)):
            y = y + w[j:j + 1, :] * xs
        t = y * _sigmoid(y)
        is_norm, scale = _qkv_scale(c)
        r = lax.rsqrt(jnp.sum(t * t, axis=-1, keepdims=True) + 1e-6)
        o_ref[...] = jnp.where(is_norm, t * r * scale, t)

    return pl.pallas_call(
        body, grid=(GDN_QKV // LANES,),
        in_specs=[BS((S, LANES), lambda c: (0, c)), BS((8, LANES), lambda c: (0, c))],
        out_specs=BS((S, LANES), lambda c: (0, c)),
        out_shape=SDS((S, GDN_QKV), F32), name=name, compiler_params=_cp(("parallel",)))(proj, convw)


def gdn_pre_bwd(proj, convw, dqkv, name):
    S = proj.shape[0]

    def body(p_ref, w_ref, d_ref, dp_ref, dw_ref):
        c = pl.program_id(0)
        x = p_ref[...]
        w = w_ref[...]
        taps = _conv_taps(x, S)
        y = jnp.zeros_like(x)
        for j, xs in enumerate(taps):
            y = y + w[j:j + 1, :] * xs
        sg = _sigmoid(y)
        t = y * sg
        is_norm, scale = _qkv_scale(c)
        dout = d_ref[0, 0] + d_ref[1, 0]
        r = lax.rsqrt(jnp.sum(t * t, axis=-1, keepdims=True) + 1e-6)
        n = t * r
        dn = dout * scale
        dt_norm = r * (dn - n * jnp.sum(dn * n, axis=-1, keepdims=True))
        dt = jnp.where(is_norm, dt_norm, dout)
        dy = dt * (sg * (1.0 + y * (1.0 - sg)))
        row = lax.broadcasted_iota(jnp.int32, (8, LANES), 0)
        dw = jnp.zeros((8, LANES), F32)
        for j, xs in enumerate(taps):
            dw = dw + jnp.where(row == j, jnp.sum(dy * xs, axis=0, keepdims=True), 0.0)
        dw_ref[...] = dw
        tt = lax.broadcasted_iota(jnp.int32, x.shape, 0)
        dx = jnp.zeros_like(x)
        for j in range(GDN_CONV):
            sh = j - GDN_CONV // 2
            ds = dy if sh == 0 else pltpu.roll(dy, sh % S, 0)
            dx = dx + w[j:j + 1, :] * jnp.where((tt - sh >= 0) & (tt - sh < S), ds, 0.0)
        dp_ref[...] = dx.astype(BF16)

    return pl.pallas_call(
        body, grid=(GDN_QKV // LANES,),
        in_specs=[BS((S, LANES), lambda c: (0, c)), BS((8, LANES), lambda c: (0, c)),
                  BS((2, 1, S, LANES), lambda c: (0, c // GDN_H, 0, c % GDN_H))],
        out_specs=[BS((S, LANES), lambda c: (0, c)), BS((8, LANES), lambda c: (0, c))],
        out_shape=[SDS((S, GDN_QKV), BF16), SDS((8, GDN_QKV), F32)],
        name=name, compiler_params=_cp(("parallel",)))(proj, convw, dqkv)


def _chunk_sum_matrix(n, upper):
    i = lax.broadcasted_iota(jnp.int32, (n, n), 0)
    j = lax.broadcasted_iota(jnp.int32, (n, n), 1)
    same = (i // GDN_C) == (j // GDN_C)
    tri = (i <= j) if upper else (i >= j)
    return jnp.where(same & tri, 1.0, 0.0).astype(F32)


def _gate_lanes(shape):
    lane = lax.broadcasted_iota(jnp.int32, shape, 1)
    return lane < GDN_H, (lane >= GDN_H) & (lane < 2 * GDN_H), (lane >= 2 * GDN_H) & (lane < 4 * GDN_H)


def gdn_gate_fwd(proj, prm, name):
    S = proj.shape[0]
    tm = min(512, S)
    ct = GDN_INP // LANES - 1

    def body(p_ref, prm_ref, o_ref):
        ab = p_ref[...]
        a_log = prm_ref[0:1, :]
        dtb = prm_ref[1:2, :]
        z = ab + dtb
        sp = jnp.maximum(z, 0.0) + jnp.log(1.0 + jnp.exp(-jnp.abs(z)))
        g = -jnp.exp(a_log) * sp
        is_f, is_b, is_beta = _gate_lanes(ab.shape)
        gf = _dot(_chunk_sum_matrix(tm, False), jnp.where(is_f, g, 0.0), HI)
        gbk = _dot(_chunk_sum_matrix(tm, True), jnp.where(is_b, g, 0.0), HI)
        o_ref[...] = gf + gbk + jnp.where(is_beta, _sigmoid(ab), 0.0)

    return pl.pallas_call(
        body, grid=(S // tm,),
        in_specs=[BS((tm, LANES), lambda i: (i, ct)), BS((8, LANES), lambda i: (0, 0))],
        out_specs=BS((tm, LANES), lambda i: (i, 0)),
        out_shape=SDS((S, LANES), F32), name=name, compiler_params=_cp(("parallel",)))(proj, prm)


def gdn_gate_bwd(proj, prm, dgb, name):
    S = proj.shape[0]
    tm = min(512, S)
    ct = GDN_INP // LANES - 1

    def body(p_ref, prm_ref, d_ref, dab_ref, dprm_ref):
        i = pl.program_id(0)
        ab = p_ref[...]
        a_log = prm_ref[0:1, :]
        dtb = prm_ref[1:2, :]
        z = ab + dtb
        sp = jnp.maximum(z, 0.0) + jnp.log(1.0 + jnp.exp(-jnp.abs(z)))
        ea = jnp.exp(a_log)
        g = -ea * sp
        is_f, is_b, is_beta = _gate_lanes(ab.shape)
        d = d_ref[...]
        dg = (_dot_tn(_chunk_sum_matrix(tm, False), jnp.where(is_f, d, 0.0), HI)
              + _dot_tn(_chunk_sum_matrix(tm, True), jnp.where(is_b, d, 0.0), HI))
        da = dg * (-ea) * _sigmoid(z)
        beta = _sigmoid(ab)
        dab_ref[...] = jnp.where(is_beta, d * beta * (1.0 - beta), da).astype(BF16)
        row = lax.broadcasted_iota(jnp.int32, (8, LANES), 0)
        part = (jnp.where(row == 0, jnp.sum(dg * g, axis=0, keepdims=True), 0.0)
                + jnp.where(row == 1, jnp.sum(da, axis=0, keepdims=True), 0.0))

        @pl.when(i == 0)
        def _():
            dprm_ref[...] = part

        @pl.when(i > 0)
        def _():
            dprm_ref[...] += part

    return pl.pallas_call(
        body, grid=(S // tm,),
        in_specs=[BS((tm, LANES), lambda i: (i, ct)), BS((8, LANES), lambda i: (0, 0)), BS((tm, LANES), lambda i: (i, 0))],
        out_specs=[BS((tm, LANES), lambda i: (i, 0)), BS((8, LANES), lambda i: (0, 0))],
        out_shape=[SDS((S, LANES), BF16), SDS((8, LANES), F32)],
        name=name, compiler_params=_cp(("arbitrary",)))(proj, prm, dgb)


def _tri_masks(d):
    i = lax.broadcasted_iota(jnp.int32, (GDN_C, GDN_C), 0)
    j = lax.broadcasted_iota(jnp.int32, (GDN_C, GDN_C), 1)
    s = (i - j) * (1 - 2 * d)
    return s >= 0, s > 0


def _split(a):
    hi = _bf(a)
    return hi, _bf(a - hi.astype(F32))


def _dot3(a, b):
    return _dot(a[0], b[0]) + (_dot(a[0], b[1]) + _dot(a[1], b[0]))


def _inv_unit_tri_many(mats):
    i = lax.broadcasted_iota(jnp.int32, mats[0].shape, 0)
    j = lax.broadcasted_iota(jnp.int32, mats[0].shape, 1)
    eye = jnp.where(i == j, 1.0, 0.0)
    ms = [-a for a in mats]
    ps = [eye + m for m in ms]
    for _ in range(int(math.log2(GDN_C)) - 1):
        sp = [_split(m) for m in ms]
        ms = [_dot3(s, s) for s in sp]
        sp = [_split(m) for m in ms]
        pp = [_split(p) for p in ps]
        ps = [p + _dot3(a, b) for p, a, b in zip(ps, pp, sp)]
    return ps


def _lane_col(x, lane_idx):
    lane = lax.broadcasted_iota(jnp.int32, x.shape, 1)
    return jnp.sum(jnp.where(lane == lane_idx, x, 0.0), axis=1, keepdims=True)


def _chunk_gates(gb_ref, grow_ref, hh, ci, d, head):
    gbv = gb_ref[ci * GDN_C:(ci + 1) * GDN_C, :]
    gcol = _lane_col(gbv, d * GDN_H + head)
    bcol = _lane_col(gbv, 2 * GDN_H + d * GDN_H + head)
    glast = jnp.where(d == 0, gcol[GDN_C - 1:GDN_C, :], gcol[0:1, :])
    return gcol, bcol, grow_ref[hh, ci:ci + 1, :], glast


def _chunk_base(q, k, gcol, grow, bcol, glast, d):
    incl, strict = _tri_masks(d)
    decay = jnp.where(incl, jnp.exp(jnp.where(incl, gcol - grow, 0.0)), 0.0)
    kb = k * bcol
    kk = _dot_nt(_bf(kb), _bf(k))
    qk = _dot_nt(_bf(q), _bf(k))
    eg = jnp.exp(gcol)
    ek = jnp.exp(glast - gcol)
    return dict(incl=incl, strict=strict, decay=decay, kb=kb, kk=kk, qk=qk, eg=eg, ek=ek, q_dec=q * eg, k_dec=k * ek,
                bcol=bcol, glast=glast)


def _block_terms(q_ref, k_ref, v_ref, gb_ref, grow_ref, d, h):
    keys = [(hh, ci) for hh in range(GDN_HP) for ci in range(GDN_GC)]
    ts = []
    for hh, ci in keys:
        rows = slice(ci * GDN_C, (ci + 1) * GDN_C)
        cols = slice(hh * GDN_DK, (hh + 1) * GDN_DK)
        gcol, bcol, grow_v, glast = _chunk_gates(gb_ref, grow_ref, hh, ci, d, h * GDN_HP + hh)
        t = _chunk_base(q_ref[rows, cols], k_ref[rows, cols], gcol, grow_v, bcol, glast, d)
        t["v"] = v_ref[rows, cols]
        ts.append(t)
    tinvs = _inv_unit_tri_many([jnp.where(t["strict"], t["kk"] * t["decay"], 0.0) for t in ts])
    sp = [_split(x) for x in tinvs]
    us = [_dot3(s, _split(t["v"] * t["bcol"])) for s, t in zip(sp, ts)]
    ws = [_dot3(s, _split(t["kb"] * t["eg"])) for s, t in zip(sp, ts)]
    for t, tinv, u, w in zip(ts, tinvs, us, ws):
        t.update(tinv=tinv, u=u, w=w)
    return keys, ts


def _gdn_specs(S, nblk, order):
    R = GDN_GC * GDN_C
    wd = GDN_HP * GDN_DK
    hb = GDN_H // GDN_HP

    def qkv_spec(part):
        return BS((R, wd), lambda d, h, n: (order(d, n), part * hb + h))

    gb_spec = BS((R, LANES), lambda d, h, n: (order(d, n), 0))
    grow_spec = BS((GDN_HP, GDN_GC, GDN_C), lambda d, h, n: (d * hb + h, order(d, n), 0))
    st_spec = BS((1, GDN_HP, GDN_GC, GDN_DK, GDN_DK), lambda d, h, n: (d, h, order(d, n), 0, 0))
    return qkv_spec, gb_spec, grow_spec, st_spec


def _lane_row(x):
    return jnp.broadcast_to(x, (1, LANES))


def _side_parts(side):
    if side is None:
        return [], [], [], [], []
    return [ANY] * len(side.ins), [ANY] * len(side.outs), list(side.outs), list(side.sems), list(side.ins)


def _side_run(side, refs, n_in, n_out, n_scr, first, last):
    if side is None:
        return
    ns, no, nm = len(side.ins), len(side.outs), len(side.sems)
    s_in = refs[n_in:n_in + ns]
    s_out = refs[n_in + ns + n_out:n_in + ns + n_out + no]
    s_sem = refs[len(refs) - nm:]

    @pl.when(first)
    def _():
        side.start(s_in, s_out, s_sem)

    @pl.when(last)
    def _():
        side.wait(s_in, s_out, s_sem)


def gdn_scan_fwd(qkv, gb, grow, name, side=None):
    S = qkv.shape[0]
    R = GDN_GC * GDN_C
    nblk = S // R
    nc = S // GDN_C
    wd = GDN_HP * GDN_DK
    heads = range(GDN_HP)

    def order(d, n):
        return n + d * (nblk - 1 - 2 * n)

    qkv_spec, gb_spec, grow_spec, st_spec = _gdn_specs(S, nblk, order)

    s_in, s_out, s_shape, s_scr_shapes, s_ops = _side_parts(side)
    hb = GDN_H // GDN_HP

    def body(*refs):
        q_ref, k_ref, v_ref, gb_ref, grow_ref = refs[:5]
        o_ref, st_ref = refs[5 + len(s_in):7 + len(s_in)]
        s_scr, u_scr, w_scr, qd_scr, kd_scr, in_scr, egl_scr = refs[7 + len(s_in) + len(s_out):14 + len(s_in) + len(s_out)]
        d = pl.program_id(0)
        h = pl.program_id(1)
        n = pl.program_id(2)
        _side_run(side, refs, 5, 2, 7, (d == 0) & (h == 0) & (n == 0), (d == 1) & (h == hb - 1) & (n == nblk - 1))

        @pl.when(n == 0)
        def _():
            s_scr[...] = jnp.zeros_like(s_scr)

        keys, ts = _block_terms(q_ref, k_ref, v_ref, gb_ref, grow_ref, d, h)
        for (hh, ci), t in zip(keys, ts):
            u_scr[hh, ci] = t["u"]
            w_scr[hh, ci] = _bf(t["w"])
            qd_scr[hh, ci] = _bf(t["q_dec"])
            kd_scr[hh, ci] = _bf(t["k_dec"])
            in_scr[hh, ci] = _bf(jnp.where(t["incl"], t["qk"] * t["decay"], 0.0))
            egl_scr[hh, ci] = _lane_row(jnp.exp(t["glast"]))

        def chunk(cc, carry):
            ci = cc + d * (GDN_GC - 1 - 2 * cc)
            rows = pl.ds(pl.multiple_of(ci * GDN_C, GDN_C), GDN_C)
            sts = [s_scr[hh] for hh in heads]
            for hh in heads:
                st_ref[0, hh, ci] = sts[hh]
            sbs = [_bf(st) for st in sts]
            vns = [_bf(u_scr[hh, ci] - _dot(w_scr[hh, ci], sbs[hh])) for hh in heads]
            for hh in heads:
                s_scr[hh] = sts[hh] * egl_scr[hh, ci] + _dot_tn(kd_scr[hh, ci], vns[hh])
            for hh in heads:
                o_ref[0, rows, hh * GDN_DK:(hh + 1) * GDN_DK] = _dot(qd_scr[hh, ci], sbs[hh]) + _dot(in_scr[hh, ci], vns[hh])
            return carry

        lax.fori_loop(0, GDN_GC, chunk, 0)

    blk = (GDN_HP, GDN_GC, GDN_C, GDN_DK)
    return pl.pallas_call(
        body, grid=(2, GDN_H // GDN_HP, nblk),
        in_specs=[qkv_spec(0), qkv_spec(1), qkv_spec(2), gb_spec, grow_spec] + s_in,
        out_specs=[BS((1, R, wd), lambda d, h, n: (d, order(d, n), h)), st_spec] + s_out,
        out_shape=[SDS((2, S, GDN_H * GDN_DK), F32), SDS((2, GDN_H, nc, GDN_DK, GDN_DK), F32)] + s_shape,
        scratch_shapes=[pltpu.VMEM((GDN_HP, GDN_DK, GDN_DK), F32), pltpu.VMEM(blk, F32), pltpu.VMEM(blk, BF16),
                        pltpu.VMEM(blk, BF16), pltpu.VMEM(blk, BF16), pltpu.VMEM((GDN_HP, GDN_GC, GDN_C, GDN_C), BF16),
                        pltpu.VMEM((GDN_HP, GDN_GC, 1, LANES), F32)] + s_scr_shapes,
        name=name, compiler_params=_cp(("arbitrary", "arbitrary", "arbitrary")))(qkv, qkv, qkv, gb, grow, *s_ops)


def gdn_scan_bwd(qkv, gb, grow, states, do, name, side=None):
    S = qkv.shape[0]
    R = GDN_GC * GDN_C
    nblk = S // R
    wd = GDN_HP * GDN_DK
    heads = range(GDN_HP)

    def order(d, n):
        return (nblk - 1 - n) - d * (nblk - 1 - 2 * n)

    qkv_spec, gb_spec, grow_spec, st_spec = _gdn_specs(S, nblk, order)

    s_in, s_out, s_shape, s_scr_shapes, s_ops = _side_parts(side)
    hb = GDN_H // GDN_HP

    def body(*refs):
        q_ref, k_ref, v_ref, gb_ref, grow_ref, st_ref, do_ref = refs[:7]
        dqkv_ref, dgate_ref = refs[7 + len(s_in):9 + len(s_in)]
        (ds_scr, w_scr, kd_scr, dv1_scr, qtdo_scr, egl_scr, dsin_scr, dvn_scr,
         sdot_scr) = refs[9 + len(s_in) + len(s_out):18 + len(s_in) + len(s_out)]
        d = pl.program_id(0)
        h = pl.program_id(1)
        n = pl.program_id(2)
        _side_run(side, refs, 7, 2, 9, (d == 0) & (h == 0) & (n == 0), (d == 1) & (h == hb - 1) & (n == nblk - 1))

        @pl.when(n == 0)
        def _():
            ds_scr[...] = jnp.zeros_like(ds_scr)

        keys, ts = _block_terms(q_ref, k_ref, v_ref, gb_ref, grow_ref, d, h)
        for (hh, ci), t in zip(keys, ts):
            rows = slice(ci * GDN_C, (ci + 1) * GDN_C)
            t["wb"] = _bf(t["w"])
            t["dob"] = _bf(do_ref[rows, hh * GDN_DK:(hh + 1) * GDN_DK])
            t["sb"] = _bf(st_ref[0, hh, ci])
        for (hh, ci), t in zip(keys, ts):
            t["vnb"] = _bf(t["u"] - _dot(t["wb"], t["sb"]))
            w_scr[hh, ci] = t["wb"]
            kd_scr[hh, ci] = _bf(t["k_dec"])
            dv1_scr[hh, ci] = _dot_tn(_bf(jnp.where(t["incl"], t["qk"] * t["decay"], 0.0)), t["dob"])
            qtdo_scr[hh, ci] = _dot_tn(_bf(t["q_dec"]), t["dob"])
            egl_scr[hh, ci] = _lane_row(jnp.exp(t["glast"]))

        def chunk(cc, carry):
            ci = (GDN_GC - 1 - cc) - d * (GDN_GC - 1 - 2 * cc)
            dsns = [ds_scr[hh] for hh in heads]
            dsbs = [_bf(x) for x in dsns]
            dvns = [dv1_scr[hh, ci] + _dot(kd_scr[hh, ci], dsbs[hh]) for hh in heads]
            for hh in heads:
                ds_scr[hh] = qtdo_scr[hh, ci] + egl_scr[hh, ci] * dsns[hh] - _dot_tn(w_scr[hh, ci], _bf(dvns[hh]))
            for hh in heads:
                dsin_scr[hh, ci] = dsbs[hh]
                dvn_scr[hh, ci] = dvns[hh]
                sd = jnp.sum(jnp.sum(st_ref[0, hh, ci] * dsns[hh], axis=1, keepdims=True), axis=0, keepdims=True)
                sdot_scr[hh, ci] = _lane_row(sd)
            return carry

        lax.fori_loop(0, GDN_GC, chunk, 0)

        for (hh, ci), t in zip(keys, ts):
            t["d_vnew"] = dvn_scr[hh, ci]
            t["dvb"] = _bf(t["d_vnew"])
            t["dsb"] = dsin_scr[hh, ci]
        for t in ts:
            t["d_intra"] = jnp.where(t["incl"], _dot_nt(t["dob"], t["vnb"]), 0.0)
            t["d_qdec"] = _dot_nt(t["dob"], t["sb"])
            t["d_kdec"] = _dot_nt(t["vnb"], t["dsb"])
            t["dw"] = -_dot_nt(t["dvb"], t["sb"])
        for t in ts:
            tts = _split(t["tinv"].T)
            t["d_ru"] = _dot3(tts, _split(t["d_vnew"]))
            t["d_rw"] = _dot3(tts, _split(t["dw"]))
        for t in ts:
            t["da"] = -jnp.where(t["strict"], _dot_nt(_bf(t["d_ru"]), _bf(t["u"])) + _dot_nt(_bf(t["d_rw"]), t["wb"]), 0.0)
        for (hh, ci), t in zip(keys, ts):
            rows = slice(ci * GDN_C, (ci + 1) * GDN_C)
            cols = slice(hh * GDN_DK, (hh + 1) * GDN_DK)
            q, k, v = q_ref[rows, cols], k_ref[rows, cols], t["v"]
            decay, kb, eg, ek, bcol = t["decay"], t["kb"], t["eg"], t["ek"], t["bcol"]
            d_ru, d_rw, da, d_intra, d_qdec, d_kdec = t["d_ru"], t["d_rw"], t["da"], t["d_intra"], t["d_qdec"], t["d_kdec"]
            kbf, qbf = _bf(k), _bf(q)
            dgl = egl_scr[hh, ci][:, 0:1] * sdot_scr[hh, ci][:, 0:1]
            dv = d_ru * bcol
            dbeta = jnp.sum(d_ru * v, axis=1, keepdims=True)
            dkb = d_rw * eg
            dg = jnp.sum(d_rw * kb, axis=1, keepdims=True) * eg
            dkk = _bf(da * decay)
            dqk = _bf(d_intra * decay)
            dkb = dkb + _dot(dkk, kbf)
            dk = _dot_tn(dkk, _bf(kb)) + _dot_tn(dqk, qbf)
            dq = _dot(dqk, kbf) + d_qdec * eg
            dd = (da * t["kk"] + d_intra * t["qk"]) * decay
            dg = dg + jnp.sum(dd, axis=1, keepdims=True) - jnp.sum(dd.T, axis=1, keepdims=True)
            dg = dg + jnp.sum(d_qdec * t["q_dec"], axis=1, keepdims=True)
            dk = dk + d_kdec * ek
            ee = jnp.sum(d_kdec * t["k_dec"], axis=1, keepdims=True)
            dg = dg - ee
            dgl = dgl + jnp.sum(ee, axis=0, keepdims=True)
            dk = dk + dkb * bcol
            dbeta = dbeta + jnp.sum(dkb * k, axis=1, keepdims=True)
            ridx = lax.broadcasted_iota(jnp.int32, (GDN_C, 1), 0)
            dg = dg + jnp.where(ridx == (GDN_C - 1) * (1 - d), dgl, 0.0)
            dqkv_ref[0, 0, rows, cols] = dq
            dqkv_ref[0, 1, rows, cols] = dk
            dqkv_ref[0, 2, rows, cols] = dv
            lane2 = lax.broadcasted_iota(jnp.int32, (GDN_C, 2), 1)
            dgate_ref[0, hh, rows, :] = jnp.where(lane2 == 0, dg, dbeta)

    blk = (GDN_HP, GDN_GC, GDN_C, GDN_DK)
    sq = (GDN_HP, GDN_GC, GDN_DK, GDN_DK)
    row = (GDN_HP, GDN_GC, 1, LANES)
    return pl.pallas_call(
        body, grid=(2, GDN_H // GDN_HP, nblk),
        in_specs=[qkv_spec(0), qkv_spec(1), qkv_spec(2), gb_spec, grow_spec, st_spec,
                  BS((R, wd), lambda d, h, n: (order(d, n), h))] + s_in,
        out_specs=[BS((1, 3, R, wd), lambda d, h, n: (d, 0, order(d, n), h)),
                   BS((1, GDN_HP, R, 2), lambda d, h, n: (d, h, order(d, n), 0))] + s_out,
        out_shape=[SDS((2, 3, S, GDN_H * GDN_DK), F32), SDS((2, GDN_H, S, 2), F32)] + s_shape,
        scratch_shapes=[pltpu.VMEM((GDN_HP, GDN_DK, GDN_DK), F32), pltpu.VMEM(blk, BF16), pltpu.VMEM(blk, BF16),
                        pltpu.VMEM(blk, F32), pltpu.VMEM(sq, F32), pltpu.VMEM(row, F32), pltpu.VMEM(sq, BF16),
                        pltpu.VMEM(blk, F32), pltpu.VMEM(row, F32)] + s_scr_shapes,
        name=name, compiler_params=_cp(("arbitrary", "arbitrary", "arbitrary")))(qkv, qkv, qkv, gb, grow, states, do, *s_ops)


def gdn_post_fwd(o2, proj, nw, name):
    S = proj.shape[0]
    tm = min(512, S)
    zoff = GDN_QKV // LANES

    def body(o_ref, z_ref, nw_ref, y_ref):
        o = o_ref[0] + o_ref[1]
        z = z_ref[...]
        r = lax.rsqrt(jnp.mean(o * o, axis=-1, keepdims=True) + RMS_EPS)
        y_ref[...] = (o * r * nw_ref[...] * (z * _sigmoid(z))).astype(BF16)

    return pl.pallas_call(
        body, grid=(S // tm, GDN_H),
        in_specs=[BS((2, tm, LANES), lambda i, h: (0, i, h)), BS((tm, LANES), lambda i, h: (i, zoff + h)),
                  BS((1, LANES), lambda i, h: (0, 0))],
        out_specs=BS((tm, LANES), lambda i, h: (i, h)),
        out_shape=SDS((S, GDN_H * GDN_DK), BF16), name=name, compiler_params=_cp(("parallel", "parallel")))(o2, proj, nw)


def gdn_post_bwd(o2, proj, nw, dy, name):
    S = proj.shape[0]
    tm = min(512, S)
    zoff = GDN_QKV // LANES

    def body(o_ref, z_ref, nw_ref, dy_ref, do_ref, dz_ref, dnw_ref):
        first = (pl.program_id(0) == 0) & (pl.program_id(1) == 0)
        o = o_ref[0] + o_ref[1]
        z = z_ref[...]
        nwv = nw_ref[...]
        dyv = dy_ref[...]
        r = lax.rsqrt(jnp.mean(o * o, axis=-1, keepdims=True) + RMS_EPS)
        n = o * r
        sg = _sigmoid(z)
        sz = z * sg
        dz_ref[...] = (dyv * n * nwv * (sg * (1.0 + z * (1.0 - sg)))).astype(BF16)
        dn = dyv * nwv * sz
        do_ref[...] = r * (dn - n * jnp.mean(dn * n, axis=-1, keepdims=True))
        part = jnp.sum(dyv * n * sz, axis=0, keepdims=True)

        @pl.when(first)
        def _():
            dnw_ref[...] = part

        @pl.when(jnp.logical_not(first))
        def _():
            dnw_ref[...] += part

    blk = BS((tm, LANES), lambda i, h: (i, h))
    return pl.pallas_call(
        body, grid=(S // tm, GDN_H),
        in_specs=[BS((2, tm, LANES), lambda i, h: (0, i, h)), BS((tm, LANES), lambda i, h: (i, zoff + h)),
                  BS((1, LANES), lambda i, h: (0, 0)), blk],
        out_specs=[blk, blk, BS((1, LANES), lambda i, h: (0, 0))],
        out_shape=[SDS((S, GDN_H * GDN_DK), F32), SDS((S, GDN_H * GDN_DK), BF16), SDS((1, LANES), F32)],
        name=name, compiler_params=_cp(("arbitrary", "arbitrary")))(o2, proj, nw, dy)


def _gate_prm(a_log, dt_bias):
    z = jnp.zeros((8, LANES), F32)
    z = z.at[0, :2 * GDN_H].set(a_log.reshape(-1))
    return z.at[1, :2 * GDN_H].set(dt_bias.reshape(-1))


def gdn_fwd(x, g, w_all, convw, a_log, dt_bias, nw, w_out, tag, side=None):
    S = x.shape[0]
    h = rms_fwd(x, g, f"{tag}_rms")
    proj = mm(h, w_all, name=f"{tag}_proj", tn=1408)
    qkv = gdn_pre_fwd(proj, convw, f"{tag}_pre")
    prm = _gate_prm(a_log, dt_bias)
    gb = gdn_gate_fwd(proj, prm, f"{tag}_gate")
    grow = gb[:, :2 * GDN_H].T.reshape(2 * GDN_H, S // GDN_C, GDN_C)
    o2, states, *side_out = gdn_scan_fwd(qkv, gb, grow, f"{tag}_scan", side)
    y = gdn_post_fwd(o2, proj, nw, f"{tag}_post")
    xn = mm(y, w_out, name=f"{tag}_out", epi=_add, extras=(x,))
    return xn, (h, proj, qkv, prm, gb, grow, o2, states, y), side_out


def gdn_bwd(x, g, w_all, convw, nw, w_out, saved, dx, dxb, tag, side=None):
    S = x.shape[0]
    h, proj, qkv, prm, gb, grow, o2, states, y = saved
    dw_out = mm(y, dxb, name=f"{tag}_dwout", ta=True)
    dy = mm(dxb, w_out, name=f"{tag}_dy", tb=True)
    do, dz, dnw = gdn_post_bwd(o2, proj, nw, dy, f"{tag}_postb")
    dqkv, dgate, *side_out = gdn_scan_bwd(qkv, gb, grow, states, do, f"{tag}_scanb", side)
    dgb = jnp.transpose(dgate, (2, 3, 0, 1)).reshape(S, 4 * GDN_H)
    dgb = jnp.pad(dgb, ((0, 0), (0, LANES - 4 * GDN_H)))
    dab, dprm = gdn_gate_bwd(proj, prm, dgb, f"{tag}_gateb")
    dpq, dconvw = gdn_pre_bwd(proj, convw, dqkv, f"{tag}_preb")
    dproj = jnp.concatenate([dpq, dz, dab], axis=1)
    dw_all = mm(h, dproj, name=f"{tag}_dwin", ta=True, tn=384)
    dh = mm(dproj, w_all, name=f"{tag}_dh", tb=True, tk=1408)
    dx, dxb, dg = rms_bwd(x, g, dh, dx, f"{tag}_rmsb")
    da_log = dprm[0, :2 * GDN_H].reshape(2, GDN_H)
    ddt = dprm[1, :2 * GDN_H].reshape(2, GDN_H)
    return dx, dxb, dg, dw_all, dconvw, da_log, ddt, dnw, dw_out, side_out


def _rel_bucket_np(rel):
    nb = REL_BUCKETS // 2
    max_exact = nb // 2
    ret = np.where(rel > 0, nb, 0)
    n = np.abs(rel)
    nf = np.maximum(n, 1).astype(np.float32)
    large = max_exact + (np.log(nf / max_exact) / np.float32(math.log(REL_MAX_DIST / max_exact))
                         * (nb - max_exact)).astype(np.int32)
    large = np.minimum(large, nb - 1)
    return ret + np.where(n < max_exact, n, large)


def _dswa_qb(L):
    return min(256, L)


def _toeplitz(f, rows, cols):
    period = rows + cols
    e = jnp.pad(f, ((0, 0), (0, period - f.shape[1])))
    y = jnp.tile(e, (1, rows))[:, :rows * (period - 1)]
    return y.reshape(f.shape[0], rows, period - 1)[:, :, :cols]


def _bias_mats(rel_table, gi, L):
    _, dil = DSWA_CFG[gi]
    qb = _dswa_qb(L)
    offs = np.arange(-DSWA_HALF, DSWA_HALF + 1)
    onehot = jnp.asarray(np.eye(REL_BUCKETS, dtype=np.float32)[_rel_bucket_np(offs * dil)])
    f = jnp.dot(onehot, rel_table, precision=HI)[:, gi * DSWA_HG:(gi + 1) * DSWA_HG].T
    bias = _toeplitz(f, qb, qb + 2 * DSWA_HALF)
    bias_t = jnp.transpose(_toeplitz(f[:, ::-1], qb, qb + 2 * DSWA_HALF), (0, 2, 1))
    return bias, bias_t


def _win_specs(qb, L, width, major):
    m = qb // DSWA_HALF
    last = L // DSWA_HALF - 1

    def prev(*ids):
        s, b = major(*ids)
        return (s, jnp.maximum(b * m - 1, 0), 0)

    def cur(*ids):
        s, b = major(*ids)
        return (s, b, 0)

    def nxt(*ids):
        s, b = major(*ids)
        return (s, jnp.minimum((b + 1) * m, last), 0)

    return [BS((1, DSWA_HALF, width), prev), BS((1, qb, width), cur), BS((1, DSWA_HALF, width), nxt)]


def _window(p_ref, c_ref, n_ref):
    return jnp.concatenate([p_ref[0], c_ref[0], n_ref[0]], axis=0)


def _band_valid(qb, b, L, transposed):
    shape = (qb + 2 * DSWA_HALF, qb) if transposed else (qb, qb + 2 * DSWA_HALF)
    blk = lax.broadcasted_iota(jnp.int32, shape, 1 if transposed else 0)
    win = lax.broadcasted_iota(jnp.int32, shape, 0 if transposed else 1)
    off = win - DSWA_HALF - blk
    pos = b * qb - DSWA_HALF + win
    return (jnp.abs(off) <= DSWA_HALF) & (pos >= 0) & (pos < L)


def attn_fwd(q, k, v, bias, dil, name):
    NS, L, E = q.shape
    qb = _dswa_qb(L)

    def major(s, b):
        return s, b

    win = _win_specs(qb, L, E, major)

    def body(q_ref, kp, kc, kn, vp, vc, vn, b_ref, o_ref, lse_ref):
        b = pl.program_id(1)
        kw = _window(kp, kc, kn)
        vw = _window(vp, vc, vn)
        s = _dot_nt(q_ref[0], kw) * (E ** -0.5) + b_ref[0]
        s = jnp.where(_band_valid(qb, b, L, False), s, NEG_INF)
        m = jnp.max(s, axis=-1, keepdims=True)
        p = jnp.exp(s - m)
        l = jnp.sum(p, axis=-1, keepdims=True)
        lse_ref[0] = m + jnp.log(l)
        o_ref[0] = _dot(_bf(p / l), vw)

    return pl.pallas_call(
        body, grid=(NS, L // qb),
        in_specs=[win[1]] + win + win + [BS((1, qb, qb + 2 * DSWA_HALF), lambda s, b: (s // dil, 0, 0))],
        out_specs=[BS((1, qb, E), lambda s, b: (s, b, 0)), BS((1, qb, 1), lambda s, b: (s, b, 0))],
        out_shape=[SDS((NS, L, E), F32), SDS((NS, L, 1), F32)],
        name=name, compiler_params=_cp(("parallel", "parallel")))(q, k, k, k, v, v, v, bias)


def attn_bwd_q(q, k, v, bias, lse, do, dd, dil, name):
    NS, L, E = q.shape
    qb = _dswa_qb(L)
    nb = L // qb

    def major(h, r, b):
        return h * dil + r, b

    win = _win_specs(qb, L, E, major)
    col = BS((1, qb, 1), lambda h, r, b: (h * dil + r, b, 0))

    def body(q_ref, kp, kc, kn, vp, vc, vn, b_ref, lse_ref, do_ref, dd_ref, dq_ref, db_ref):
        b = pl.program_id(2)
        first = (pl.program_id(1) == 0) & (b == 0)
        kw = _window(kp, kc, kn)
        vw = _window(vp, vc, vn)
        valid = _band_valid(qb, b, L, False)
        s = _dot_nt(q_ref[0], kw) * (E ** -0.5) + b_ref[0]
        p = jnp.exp(jnp.where(valid, s - lse_ref[0], NEG_INF))
        ds = p * (_dot_nt(do_ref[0], vw) - dd_ref[0])
        dq_ref[0] = (_dot(_bf(ds), kw) * (E ** -0.5)).astype(BF16)

        @pl.when(first)
        def _():
            db_ref[0] = ds

        @pl.when(jnp.logical_not(first))
        def _():
            db_ref[0] += ds

    bspec = BS((1, qb, qb + 2 * DSWA_HALF), lambda h, r, b: (h, 0, 0))
    return pl.pallas_call(
        body, grid=(DSWA_HG, dil, nb),
        in_specs=[win[1]] + win + win + [bspec, col, win[1], col],
        out_specs=[win[1], bspec],
        out_shape=[SDS((NS, L, E), BF16), SDS((DSWA_HG, qb, qb + 2 * DSWA_HALF), F32)],
        name=name, compiler_params=_cp(("parallel", "arbitrary", "arbitrary")))(q, k, k, k, v, v, v, bias, lse, do, dd)


def attn_bwd_kv(q, k, v, bias_t, lse, do, dd, dil, name):
    NS, L, E = q.shape
    qb = _dswa_qb(L)

    def major(s, b):
        return s, b

    win = _win_specs(qb, L, E, major)
    wcol = _win_specs(qb, L, 1, major)

    def body(kc, vc, qp, qc, qn, dop, doc, don, lp, lc, ln, ddp, ddc, ddn, b_ref, dk_ref, dv_ref):
        b = pl.program_id(1)
        qw = _window(qp, qc, qn)
        dow = _window(dop, doc, don)
        lw = _window(lp, lc, ln)
        ddw = _window(ddp, ddc, ddn)
        valid = _band_valid(qb, b, L, True)
        s = _dot_nt(qw, kc[0]) * (E ** -0.5) + b_ref[0]
        p = jnp.exp(jnp.where(valid, s - lw, NEG_INF))
        dv_ref[0] = _dot_tn(_bf(p), dow).astype(BF16)
        ds = jnp.where(valid, p * (_dot_nt(dow, vc[0]) - ddw), 0.0)
        dk_ref[0] = (_dot_tn(_bf(ds), qw) * (E ** -0.5)).astype(BF16)

    return pl.pallas_call(
        body, grid=(NS, L // qb),
        in_specs=[win[1], win[1]] + win + win + wcol + wcol + [BS((1, qb + 2 * DSWA_HALF, qb), lambda s, b: (s // dil, 0, 0))],
        out_specs=[win[1], win[1]],
        out_shape=[SDS((NS, L, E), BF16), SDS((NS, L, E), BF16)],
        name=name, compiler_params=_cp(("parallel", "parallel")))(k, v, q, q, q, do, do, do, lse, lse, lse, dd, dd, dd, bias_t)


def _head_expand():
    i = lax.broadcasted_iota(jnp.int32, (LANES, DSWA_HG * DSWA_E), 0)
    j = lax.broadcasted_iota(jnp.int32, (LANES, DSWA_HG * DSWA_E), 1)
    return jnp.where(i == j // DSWA_E, 1.0, 0.0).astype(F32)


def _group_alphas(lse3):
    m = jnp.maximum(jnp.maximum(lse3[0], lse3[1]), lse3[2])
    e = [jnp.exp(t - m) for t in lse3]
    tot = e[0] + e[1] + e[2]
    return [t / tot for t in e]


def combine_fwd(o_raw, lse3, name):
    S = o_raw.shape[0]
    tm = min(512, S)
    gw = DSWA_HG * DSWA_E

    def body(o_ref, l_ref, y_ref):
        alphas = _group_alphas([l_ref[0], l_ref[1], l_ref[2]])
        ex = _head_expand()
        for gi in range(3):
            cols = slice(gi * gw, (gi + 1) * gw)
            y_ref[:, cols] = (o_ref[:, cols] * _dot(alphas[gi], ex, HI)).astype(BF16)

    return pl.pallas_call(
        body, grid=(S // tm,),
        in_specs=[BS((tm, DSWA_W), lambda i: (i, 0)), BS((3, tm, LANES), lambda i: (0, i, 0))],
        out_specs=BS((tm, DSWA_W), lambda i: (i, 0)),
        out_shape=SDS((S, DSWA_W), BF16), name=name, compiler_params=_cp(("parallel",)))(o_raw, lse3)


def combine_bwd(o_raw, lse3, dy, name):
    S = o_raw.shape[0]
    tm = min(512, S)
    gw = DSWA_HG * DSWA_E

    def body(o_ref, l_ref, dy_ref, do_ref, dd_ref):
        alphas = _group_alphas([l_ref[0], l_ref[1], l_ref[2]])
        ex = _head_expand()
        dal = []
        for gi in range(3):
            cols = slice(gi * gw, (gi + 1) * gw)
            dyv = dy_ref[:, cols]
            do_ref[:, cols] = (dyv * _dot(alphas[gi], ex, HI)).astype(BF16)
            dal.append(_dot_nt(o_ref[:, cols] * dyv, ex, HI))
        c = alphas[0] * dal[0] + alphas[1] * dal[1] + alphas[2] * dal[2]
        for gi in range(3):
            dd_ref[gi] = alphas[gi] * c

    return pl.pallas_call(
        body, grid=(S // tm,),
        in_specs=[BS((tm, DSWA_W), lambda i: (i, 0)), BS((3, tm, LANES), lambda i: (0, i, 0)), BS((tm, DSWA_W), lambda i: (i, 0))],
        out_specs=[BS((tm, DSWA_W), lambda i: (i, 0)), BS((3, tm, LANES), lambda i: (0, i, 0))],
        out_shape=[SDS((S, DSWA_W), BF16), SDS((3, S, LANES), F32)],
        name=name, compiler_params=_cp(("parallel",)))(o_raw, lse3, dy)


def _to_sub(t, dil):
    S, hg, wd = t.shape
    return jnp.transpose(t.reshape(S // dil, dil, hg, wd), (2, 1, 0, 3)).reshape(hg * dil, S // dil, wd)


def _from_sub(t, dil):
    ns, L, wd = t.shape
    hg = ns // dil
    return jnp.transpose(t.reshape(hg, dil, L, wd), (2, 1, 0, 3)).reshape(L * dil, hg, wd)


def dswa_fwd(x, g, w_in, w_out, rel_table, tag):
    S = x.shape[0]
    h = rms_fwd(x, g, f"{tag}_rms")
    qkv = mm(h, w_in, name=f"{tag}_qkv", tn=1152, out_dtype=BF16).reshape(S, 3, DSWA_HEADS, DSWA_E)
    subs, outs, lses = [], [], []
    for gi, (_, dil) in enumerate(DSWA_CFG):
        hs = slice(gi * DSWA_HG, (gi + 1) * DSWA_HG)
        qs, ks, vs = (_to_sub(qkv[:, i, hs], dil) for i in range(3))
        bias, bias_t = _bias_mats(rel_table, gi, S // dil)
        o, lse = attn_fwd(qs, ks, vs, bias, dil, f"{tag}_att{gi}")
        subs.append((qs, ks, vs, lse))
        outs.append(_from_sub(o, dil).reshape(S, DSWA_HG * DSWA_E))
        lses.append(jnp.pad(_from_sub(lse, dil).reshape(S, DSWA_HG), ((0, 0), (0, LANES - DSWA_HG))))
    o_raw = jnp.concatenate(outs, axis=1)
    lse3 = jnp.stack(lses)
    y = combine_fwd(o_raw, lse3, f"{tag}_comb")
    xn = mm(y, w_out, name=f"{tag}_out", epi=_add, extras=(x,))
    return xn, (h, subs, o_raw, lse3, y)


def dswa_bwd(x, g, w_in, w_out, rel_table, saved, dx, dxb, tag):
    S = x.shape[0]
    h, subs, o_raw, lse3, y = saved
    dw_out = mm(y, dxb, name=f"{tag}_dwout", ta=True, tm=384)
    dy = mm(dxb, w_out, name=f"{tag}_dy", tb=True, tn=384)
    do_raw, dd3 = combine_bwd(o_raw, lse3, dy, f"{tag}_combb")
    do_raw = do_raw.reshape(S, DSWA_HEADS, DSWA_E)
    dqkv = []
    drel = jnp.zeros_like(rel_table)
    for gi, (_, dil) in enumerate(DSWA_CFG):
        hs = slice(gi * DSWA_HG, (gi + 1) * DSWA_HG)
        qs, ks, vs, lse = subs[gi]
        dos = _to_sub(do_raw[:, hs], dil)
        dds = _to_sub(dd3[gi, :, :DSWA_HG, None], dil)
        (bias, bias_t), bias_vjp = jax.vjp(lambda tbl: _bias_mats(tbl, gi, S // dil), rel_table)
        dq, dbias = attn_bwd_q(qs, ks, vs, bias, lse, dos, dds, dil, f"{tag}_attq{gi}")
        dk, dv = attn_bwd_kv(qs, ks, vs, bias_t, lse, dos, dds, dil, f"{tag}_attkv{gi}")
        drel = drel + bias_vjp((dbias, jnp.zeros_like(bias_t)))[0]
        dqkv.append([_from_sub(t, dil) for t in (dq, dk, dv)])
    dqkv = jnp.stack([jnp.concatenate([dqkv[gi][i] for gi in range(3)], axis=1) for i in range(3)], axis=1)
    dqkv = dqkv.reshape(S, 3 * DSWA_W)
    dw_in = mm(h, dqkv, name=f"{tag}_dwin", ta=True, tn=384)
    dh = mm(dqkv, w_in, name=f"{tag}_dh", tb=True, tk=1152)
    dx, dxb, dg = rms_bwd(x, g, dh, dx, f"{tag}_rmsb")
    return dx, dxb, dg, dw_in, dw_out, drel


def adamw(w, g, m, v, name):
    shape = w.shape
    last = shape[-1]
    w2, g2, m2, v2 = (t.reshape(-1, last) for t in (w, g, m, v))
    rows = w2.shape[0]
    tr = rows
    if rows > 512:
        tr = next(t for t in (512, 256, 192, 128, 64, 8) if rows % t == 0)
    c1 = 1.0 / (1.0 - ADAM_B1 ** ADAM_STEP)
    c2 = 1.0 / (1.0 - ADAM_B2 ** ADAM_STEP)

    def body(w_ref, g_ref, m_ref, v_ref, d_ref, nm_ref, nv_ref):
        gv = g_ref[...]
        nm = ADAM_B1 * m_ref[...] + (1.0 - ADAM_B1) * gv
        nv = ADAM_B2 * v_ref[...] + (1.0 - ADAM_B2) * (gv * gv)
        nm_ref[...] = nm
        nv_ref[...] = nv
        d_ref[...] = -ADAM_LR * ((nm * c1) / (jnp.sqrt(nv * c2) + ADAM_EPS) + ADAM_WD * w_ref[...])

    spec = BS((tr, last), lambda i: (i, 0))
    outs = pl.pallas_call(
        body, grid=(rows // tr,), in_specs=[spec] * 4, out_specs=[spec] * 3,
        out_shape=[SDS((rows, last), F32)] * 3, name=name, compiler_params=_cp(("parallel",)))(w2, g2, m2, v2)
    return tuple(o.reshape(shape) for o in outs)


def _place():
    x, y, c = lax.axis_index("x"), lax.axis_index("y"), lax.axis_index("c")
    chips = [(1 - x, y), (x, 1 - y), (1 - x, 1 - y)]
    return x, y, c, chips


def _rcopy(src, dst, ssem, rsem, dev):
    return pltpu.make_async_remote_copy(src_ref=src, dst_ref=dst, send_sem=ssem, recv_sem=rsem, device_id=dev,
                                        device_id_type=MESH)


class SideJob(NamedTuple):
    ins: list
    outs: list
    sems: list
    start: Callable
    wait: Callable


def _job(ins, outs, sems, copies):
    def start(in_refs, out_refs, sem_refs):
        for cp in copies(in_refs, out_refs, sem_refs):
            cp.start()

    def wait(in_refs, out_refs, sem_refs):
        for cp in copies(in_refs, out_refs, sem_refs):
            cp.wait()

    return SideJob(list(ins), list(outs), list(sems), start, wait)


def gather_job(packs, halved):
    n = len(packs)
    dma = pltpu.SemaphoreType.DMA

    def copies(in_refs, out_refs, sems):
        ssem, rsem = sems
        x, y, c, chips = _place()
        jme = 2 * x + y
        cps = []
        for i, (p_ref, f_ref) in enumerate(zip(in_refs, out_refs)):
            rows = p_ref.shape[0]
            mine = pl.ds(c * (rows // 2), rows // 2) if halved[i] else pl.ds(0, rows)
            for r, (cx, cy) in enumerate(chips):
                cps.append(_rcopy(p_ref.at[mine], f_ref.at[jme, mine], ssem.at[i, r], rsem.at[i, r], (cx, cy, c)))
        return cps

    return _job(packs, [SDS((4,) + p.shape, p.dtype) for p in packs], [dma((n, 3)), dma((n, 3))], copies)


def chip_exchange_job(parts):
    n = len(parts)
    dma = pltpu.SemaphoreType.DMA

    def copies(in_refs, out_refs, sems):
        ssem, rsem = sems
        x, y, c, chips = _place()
        cps = []
        for i, (p_ref, r_ref) in enumerate(zip(in_refs, out_refs)):
            for r, (cx, cy) in enumerate(chips):
                cps.append(_rcopy(p_ref.at[2 * cx + cy], r_ref.at[r], ssem.at[i, r], rsem.at[i, r], (cx, cy, c)))
        return cps

    return _job(parts, [SDS((3,) + p.shape[1:], p.dtype) for p in parts], [dma((n, 3)), dma((n, 3))], copies)


def run_job(job, name):
    ni, no = len(job.ins), len(job.outs)

    def body(*refs):
        job.start(refs[:ni], refs[ni:ni + no], refs[ni + no:])
        job.wait(refs[:ni], refs[ni:ni + no], refs[ni + no:])

    return pl.pallas_call(
        body, in_specs=[ANY] * ni, out_specs=[ANY] * no, out_shape=job.outs, scratch_shapes=job.sems, name=name,
        compiler_params=pltpu.CompilerParams(has_side_effects=True))(*job.ins)


def forward_to_sibling(fulls, name):
    n = len(fulls)

    def body(*refs):
        in_refs, out_refs, (ssem, rsem) = refs[:n], refs[n:2 * n], refs[2 * n:]
        x, y, c, chips = _place()
        cps = []
        for i in range(n):
            half = in_refs[i].shape[1] // 2
            for r, (cx, cy) in enumerate(chips):
                piece = (2 * cx + cy, pl.ds(c * half, half))
                cps.append(_rcopy(in_refs[i].at[piece], out_refs[i].at[piece], ssem.at[i, r], rsem.at[i, r], (x, y, 1 - c)))
        for cp in cps:
            cp.start()
        for cp in cps:
            cp.wait()

    dma = pltpu.SemaphoreType.DMA
    return pl.pallas_call(
        body, in_specs=[ANY] * n, out_specs=[ANY] * n, out_shape=[SDS(f.shape, f.dtype) for f in fulls],
        scratch_shapes=[dma((n, 3)), dma((n, 3))], input_output_aliases={i: i for i in range(n)}, name=name,
        compiler_params=pltpu.CompilerParams(has_side_effects=True))(*fulls)


def rs_sibling_exchange(gpack, name):
    _, rows, W = gpack.shape
    half = rows // 2

    def body(g_ref, r_ref, ssem, rsem):
        x, y, c, _ = _place()
        cps = [_rcopy(g_ref.at[j, pl.ds((1 - c) * half, half)], r_ref.at[j], ssem.at[j], rsem.at[j], (x, y, 1 - c))
               for j in range(4)]
        for cp in cps:
            cp.start()
        for cp in cps:
            cp.wait()

    dma = pltpu.SemaphoreType.DMA
    return pl.pallas_call(
        body, in_specs=[ANY], out_specs=ANY, out_shape=SDS((4, half, W), gpack.dtype),
        scratch_shapes=[dma((4,)), dma((4,))], name=name,
        compiler_params=pltpu.CompilerParams(has_side_effects=True))(gpack)


def _div_tile(n, limit):
    return next(t for t in range(limit - limit % 8, 0, -8) if n % t == 0)


def rs_add_sibling(gpack, recv, cidx, name):
    _, rows, W = gpack.shape
    half = rows // 2
    tr = _div_tile(half, 1024)
    nb = half // tr

    def body(c_ref, g_ref, r_ref, o_ref):
        o_ref[...] = g_ref[...] + r_ref[...]

    gs = pltpu.PrefetchScalarGridSpec(
        num_scalar_prefetch=1, grid=(4, nb),
        in_specs=[BS((1, tr, W), lambda j, i, c: (j, c[0] * nb + i, 0)), BS((1, tr, W), lambda j, i, c: (j, i, 0))],
        out_specs=BS((1, tr, W), lambda j, i, c: (j, i, 0)))
    return pl.pallas_call(body, grid_spec=gs, out_shape=SDS((4, half, W), F32), name=name,
                          compiler_params=_cp(("parallel", "parallel")))(cidx, gpack, recv)


def rs_add_chips(recv, part, place, name):
    _, half, W = recv.shape
    tr = _div_tile(half, 640)
    nb = half // tr

    def body(x_ref, y_ref, c_ref, r_ref, own_ref, o_ref):
        o_ref[...] = ((r_ref[0] + r_ref[1]) + r_ref[2]) + own_ref[0]

    gs = pltpu.PrefetchScalarGridSpec(
        num_scalar_prefetch=3, grid=(nb,),
        in_specs=[BS((3, tr, W), lambda i, x, y, c: (0, i, 0)), BS((1, tr, W), lambda i, x, y, c: (2 * x[0] + y[0], i, 0))],
        out_specs=BS((tr, W), lambda i, x, y, c: (c[0] * nb + i, 0)))
    return pl.pallas_call(body, grid_spec=gs, out_shape=SDS((2 * half, W), F32), name=name,
                          compiler_params=_cp(("parallel",)))(*place, recv, part)


def rs_sibling_share(gsh, name):
    rows, W = gsh.shape
    half = rows // 2

    def body(g_ref, o_ref, ssem, rsem):
        x, y, c, _ = _place()
        mine = pl.ds(c * half, half)
        cp = _rcopy(g_ref.at[mine], o_ref.at[mine], ssem, rsem, (x, y, 1 - c))
        cp.start()
        cp.wait()

    dma = pltpu.SemaphoreType.DMA
    return pl.pallas_call(
        body, in_specs=[ANY], out_specs=ANY, out_shape=SDS(gsh.shape, gsh.dtype),
        scratch_shapes=[dma, dma], input_output_aliases={0: 0}, name=name,
        compiler_params=pltpu.CompilerParams(has_side_effects=True))(gsh)


def allreduce_small(pack):
    R = pack.shape[0]

    def body(p_ref, o_ref, all_ref, ssem, rsem):
        x, y, c, _ = _place()
        me = 4 * x + 2 * y + c
        all_ref[me] = p_ref[...]
        cps = []
        for m in range(1, 8):
            peer = (1 - x if m & 4 else x, 1 - y if m & 2 else y, 1 - c if m & 1 else c)
            cp = _rcopy(p_ref, all_ref.at[me], ssem.at[m - 1], rsem.at[m - 1], peer)
            cp.start()
            cps.append(cp)
        for cp in cps:
            cp.wait()
        acc = all_ref[0]
        for i in range(1, 8):
            acc = acc + all_ref[i]
        o_ref[...] = acc

    dma = pltpu.SemaphoreType.DMA
    vm = BS(memory_space=pltpu.VMEM)
    return pl.pallas_call(
        body, in_specs=[vm], out_specs=vm, out_shape=SDS(pack.shape, F32),
        scratch_shapes=[pltpu.VMEM((8, R, LANES), F32), dma((7,)), dma((7,))], name="allreduce_small",
        compiler_params=pltpu.CompilerParams(has_side_effects=True))(pack)


PACK_W = 1024
PACK_ALIGN = 32


def _layer_entries(l):
    if l % 2 == 0:
        mixer = [("gdn_w_in", l // 2, D_MODEL, GDN_IN // 4, True), ("gdn_w_out", l // 2, D_MODEL // 4, D_MODEL, False)]
    else:
        mixer = [("dswa_w_in", l // 2, D_MODEL, 3 * DSWA_W // 4, True), ("dswa_w_out", l // 2, DSWA_W // 4, D_MODEL, False)]
    return mixer + [("mlp_w1", l, D_MODEL, D_FF // 4, True), ("mlp_w2", l, D_FF // 4, D_MODEL, False)]


def _layer_offsets(l):
    offs = [int(o) for o in np.cumsum([0] + [r * c // PACK_W for (_, _, r, c, _) in _layer_entries(l)])]
    return offs, -(-offs[-1] // PACK_ALIGN) * PACK_ALIGN


def _pack_layer(l, shards, dtype):
    offs, total = _layer_offsets(l)
    parts = [shards[name][li].astype(dtype).reshape(-1, PACK_W) for (name, li, _, _, _) in _layer_entries(l)]
    parts.append(jnp.zeros((total - offs[-1], PACK_W), dtype))
    return jnp.concatenate(parts, axis=0)


def _unpack_layer(l, full, own, jme):
    offs, _ = _layer_offsets(l)
    mats = []
    for e, (_, _, r, c, by_col) in enumerate(_layer_entries(l)):
        mine = own[offs[e]:offs[e + 1]]
        sh = [jnp.where(jme == j, mine, full[j, offs[e]:offs[e + 1]]).reshape(r, c) for j in range(4)]
        mats.append(jnp.concatenate(sh, axis=1 if by_col else 0))
    return mats


def _pack_layer_grads(l, grads):
    offs, total = _layer_offsets(l)
    per_chip = []
    for j in range(4):
        parts = []
        for g, (_, _, r, c, by_col) in zip(grads, _layer_entries(l)):
            sh = g[:, c * j:c * (j + 1)] if by_col else g[r * j:r * (j + 1), :]
            parts.append(sh.reshape(-1, PACK_W))
        parts.append(jnp.zeros((total - offs[-1], PACK_W), F32))
        per_chip.append(jnp.concatenate(parts, axis=0))
    return jnp.stack(per_chip)


def _unpack_shard_grads(gshs):
    out = {}
    for l, gsh in enumerate(gshs):
        offs, _ = _layer_offsets(l)
        for e, (name, _, r, c, _) in enumerate(_layer_entries(l)):
            out.setdefault(name, []).append(gsh[offs[e]:offs[e + 1]].reshape(r, c))
    return {k: jnp.stack(v) for k, v in out.items()}


def _flat_pad(t, mult=8 * LANES):
    f = t.reshape(-1)
    return jnp.pad(f, (0, (-f.shape[0]) % mult))


def kernel(x, norm_mix, norm_mlp, norm_final, rel_bias, gdn_w_in, gdn_conv_w, gdn_a_log, gdn_dt_bias, gdn_norm_w, gdn_w_out, dswa_w_in, dswa_w_out, mlp_w1, mlp_w2, loss_target, m_norm_mix, m_norm_mlp, m_norm_final, m_rel_bias, m_gdn_w_in, m_gdn_conv_w, m_gdn_a_log, m_gdn_dt_bias, m_gdn_norm_w, m_gdn_w_out, m_dswa_w_in, m_dswa_w_out, m_mlp_w1, m_mlp_w2, v_norm_mix, v_norm_mlp, v_norm_final, v_rel_bias, v_gdn_w_in, v_gdn_conv_w, v_gdn_a_log, v_gdn_dt_bias, v_gdn_norm_w, v_gdn_w_out, v_dswa_w_in, v_dswa_w_out, v_mlp_w1, v_mlp_w2):
    xi, yi, ci = lax.axis_index("x"), lax.axis_index("y"), lax.axis_index("c")
    jme = 2 * xi + yi
    big = dict(gdn_w_in=gdn_w_in, gdn_w_out=gdn_w_out, dswa_w_in=dswa_w_in, dswa_w_out=dswa_w_out, mlp_w1=mlp_w1, mlp_w2=mlp_w2)
    n_gdn = gdn_w_in.shape[0]
    conv_cols = gdn_conv_w.shape[-1]

    packs = [_pack_layer(l, big, BF16) for l in range(DEPTH)]
    convp = jnp.pad(gdn_conv_w.reshape(n_gdn * GDN_CONV, conv_cols), ((0, 16 - n_gdn * GDN_CONV), (0, 0)))
    raw0, cfull = run_job(gather_job([packs[0], convp], [True, False]), "gather_l0")
    fulls = {0: forward_to_sibling([raw0], "forward_l0")[0]}
    cfull = jnp.where((jnp.arange(4) == jme)[:, None, None], convp[None], cfull)
    conv_all = jnp.transpose(cfull[:, :n_gdn * GDN_CONV], (1, 0, 2)).reshape(n_gdn, GDN_CONV, 4 * conv_cols)
    conv_all = jnp.pad(conv_all, ((0, 0), (0, 8 - GDN_CONV), (0, 0)))
    fwd_jobs = {0: [1, 2], 2: [3]}

    xs = x[0]
    saved = []
    for l in range(DEPTH):
        w_in, w_out, w1, w2 = _unpack_layer(l, fulls[l], packs[l], jme)
        gm, gp = norm_mix[l][None], norm_mlp[l][None]
        a = l // 2
        if l % 2 == 0:
            w_in = jnp.pad(w_in, ((0, 0), (0, GDN_INP - GDN_IN)))
            job = gather_job([packs[t] for t in fwd_jobs[l]], [True] * len(fwd_jobs[l]))
            x_mid, sv, raws = gdn_fwd(xs, gm, w_in, conv_all[a], gdn_a_log[a], gdn_dt_bias[a], gdn_norm_w[a][None], w_out,
                                      f"l{l}_gdn", job)
            for t, f in zip(fwd_jobs[l], forward_to_sibling(raws, f"forward_from_l{l}")):
                fulls[t] = f
        else:
            x_mid, sv = dswa_fwd(xs, gm, w_in, w_out, rel_bias, f"l{l}_att")
        x_out, sv2 = mlp_fwd(x_mid, gp, w1, w2, f"l{l}_mlp")
        saved.append((xs, x_mid, (w_in, w_out, w1, w2), sv, sv2))
        xs = x_out

    cidx = ci.astype(jnp.int32).reshape(1)
    place = [t.astype(jnp.int32).reshape(1) for t in (xi, yi, ci)]

    def chip_partial(l, grads4):
        gpack = _pack_layer_grads(l, grads4)
        return rs_add_sibling(gpack, rs_sibling_exchange(gpack, f"rs_sibling_l{l}"), cidx, f"rs_add_sibling_l{l}")

    def finish(l, recv):
        return rs_sibling_share(rs_add_chips(recv, parts[l], place, f"rs_add_chips_l{l}"), f"rs_share_l{l}")

    loss_part, dx, dxb, d_final = loss_head(xs, norm_final[None], loss_target[0], "loss_head")
    d_mix, d_mlp = [None] * DEPTH, [None] * DEPTH
    d_conv, d_alog, d_dt, d_nw = [None] * n_gdn, [None] * n_gdn, [None] * n_gdn, [None] * n_gdn
    d_rel = jnp.zeros_like(rel_bias)
    parts, gshs = {}, [None] * DEPTH
    bwd_jobs = {2: [3], 0: [2, 1]}
    for l in reversed(range(DEPTH)):
        x_in, x_mid, (w_in, w_out, w1, w2), sv, sv2 = saved[l]
        gm, gp = norm_mix[l][None], norm_mlp[l][None]
        a = l // 2
        dx, dxb, d_mlp[l], dw1, dw2 = mlp_bwd(x_mid, gp, w1, w2, sv2, dx, dxb, f"l{l}_mlp")
        if l % 2 == 0:
            job = chip_exchange_job([parts[t] for t in bwd_jobs[l]])
            dx, dxb, d_mix[l], dw_all, d_conv[a], d_alog[a], d_dt[a], d_nw[a], dwo, recvs = gdn_bwd(
                x_in, gm, w_in, conv_all[a], gdn_norm_w[a][None], w_out, sv, dx, dxb, f"l{l}_gdn", job)
            for t, rv in zip(bwd_jobs[l], recvs):
                gshs[t] = finish(t, rv)
            dwi = dw_all[:, :GDN_IN]
        else:
            dx, dxb, d_mix[l], dwi, dwo, drel = dswa_bwd(x_in, gm, w_in, w_out, rel_bias, sv, dx, dxb, f"l{l}_att")
            d_rel = d_rel + drel
        parts[l] = chip_partial(l, [dwi, dwo, dw1, dw2])
    gshs[0] = finish(0, run_job(chip_exchange_job([parts[0]]), "rs_chip_exchange_l0")[0])
    gbig = _unpack_shard_grads(gshs)

    small = [jnp.concatenate(d_mix, axis=0), jnp.concatenate(d_mlp, axis=0), d_final, d_rel,
             jnp.stack(d_conv), jnp.stack(d_alog), jnp.stack(d_dt), jnp.concatenate(d_nw, axis=0)]
    flat = [_flat_pad(t) for t in small]
    sizes = [f.shape[0] for f in flat]
    red = allreduce_small(jnp.concatenate(flat).reshape(-1, LANES)).reshape(-1)
    offs = np.cumsum([0] + sizes)
    red = [red[offs[i]:offs[i] + small[i].size].reshape(small[i].shape) for i in range(len(small))]
    g_conv_all = red[4][:, :GDN_CONV].reshape(n_gdn, GDN_CONV, 1, 4 * conv_cols)
    g_conv = lax.dynamic_slice_in_dim(g_conv_all, jme * conv_cols, conv_cols, axis=3)
    g = dict(norm_mix=red[0], norm_mlp=red[1], norm_final=red[2].reshape(norm_final.shape), rel_bias=red[3],
             gdn_conv_w=g_conv, gdn_a_log=red[5], gdn_dt_bias=red[6], gdn_norm_w=red[7][:, :GDN_DK], **gbig)

    w = dict(norm_mix=norm_mix, norm_mlp=norm_mlp, norm_final=norm_final, rel_bias=rel_bias, gdn_conv_w=gdn_conv_w,
             gdn_a_log=gdn_a_log, gdn_dt_bias=gdn_dt_bias, gdn_norm_w=gdn_norm_w, **big)
    m = dict(norm_mix=m_norm_mix, norm_mlp=m_norm_mlp, norm_final=m_norm_final, rel_bias=m_rel_bias, gdn_w_in=m_gdn_w_in,
             gdn_conv_w=m_gdn_conv_w, gdn_a_log=m_gdn_a_log, gdn_dt_bias=m_gdn_dt_bias, gdn_norm_w=m_gdn_norm_w,
             gdn_w_out=m_gdn_w_out, dswa_w_in=m_dswa_w_in, dswa_w_out=m_dswa_w_out, mlp_w1=m_mlp_w1, mlp_w2=m_mlp_w2)
    v = dict(norm_mix=v_norm_mix, norm_mlp=v_norm_mlp, norm_final=v_norm_final, rel_bias=v_rel_bias, gdn_w_in=v_gdn_w_in,
             gdn_conv_w=v_gdn_conv_w, gdn_a_log=v_gdn_a_log, gdn_dt_bias=v_gdn_dt_bias, gdn_norm_w=v_gdn_norm_w,
             gdn_w_out=v_gdn_w_out, dswa_w_in=v_dswa_w_in, dswa_w_out=v_dswa_w_out, mlp_w1=v_mlp_w1, mlp_w2=v_mlp_w2)
    names = ["norm_mix", "norm_mlp", "norm_final", "rel_bias", "gdn_w_in", "gdn_conv_w", "gdn_a_log", "gdn_dt_bias",
             "gdn_norm_w", "gdn_w_out", "dswa_w_in", "dswa_w_out", "mlp_w1", "mlp_w2"]
    upd = {n: adamw(w[n], g[n], m[n], v[n], f"adamw_{n}") for n in names}
    loss = lax.psum(loss_part[0, 0], ("x", "y", "c"))
    return (loss, dx[None], *[g[n] for n in names], *[upd[n][0] for n in names], *[upd[n][1] for n in names],
            *[upd[n][2] for n in names])
```

```python
import math
from typing import Callable, NamedTuple

import numpy as np
import jax
import jax.numpy as jnp
from jax import lax
from jax.experimental import pallas as pl
from jax.experimental.pallas import tpu as pltpu

F32 = jnp.float32
BF16 = jnp.bfloat16
HI = lax.Precision.HIGHEST
BS = pl.BlockSpec
SDS = jax.ShapeDtypeStruct
MESH = pl.DeviceIdType.MESH
ANY = BS(memory_space=pl.ANY)

D_MODEL = 1024
D_FF = 4096
DEPTH = 4
RMS_EPS = 1e-6
NEG_INF = -1e30
LANES = 128
VMEM_LIMIT = 56 << 20

GDN_H = 8
GDN_DK = 128
GDN_CONV = 5
GDN_C = 64
GDN_GC = 8
GDN_HP = 4
GDN_QKV = 3 * GDN_H * GDN_DK
GDN_IN = GDN_QKV + GDN_H * GDN_DK + 4 * GDN_H
GDN_INP = 4224

DSWA_CFG = ((128, 1), (512, 4), (2048, 16))
DSWA_HG = 6
DSWA_E = 64
DSWA_HEADS = 18
DSWA_W = DSWA_HEADS * DSWA_E
DSWA_HALF = 64
REL_BUCKETS = 32
REL_MAX_DIST = 1024

ADAM_LR = 0.001
ADAM_B1 = 0.9
ADAM_B2 = 0.999
ADAM_EPS = 1e-08
ADAM_WD = 0.01
ADAM_STEP = 10


def _cp(sem=None):
    return pltpu.CompilerParams(dimension_semantics=sem, vmem_limit_bytes=VMEM_LIMIT)


def _dot(a, b, prec=None):
    return jnp.dot(a, b, precision=prec, preferred_element_type=F32)


def _dot_nt(a, b, prec=None):
    return lax.dot_general(a, b, (((1,), (1,)), ((), ())), precision=prec, preferred_element_type=F32)


def _dot_tn(a, b, prec=None):
    return lax.dot_general(a, b, (((0,), (0,)), ((), ())), precision=prec, preferred_element_type=F32)


def _bf(a):
    return a.astype(BF16)


def _sigmoid(x):
    return 1.0 / (1.0 + jnp.exp(-x))


def rms_fwd(x, g, name):
    S, Dm = x.shape
    tm = min(512, S)

    def body(x_ref, g_ref, o_ref):
        xv = x_ref[...]
        r = lax.rsqrt(jnp.mean(xv * xv, axis=-1, keepdims=True) + RMS_EPS)
        o_ref[...] = (xv * r * g_ref[...]).astype(o_ref.dtype)

    return pl.pallas_call(
        body, grid=(S // tm,),
        in_specs=[BS((tm, Dm), lambda i: (i, 0)), BS((1, Dm), lambda i: (0, 0))],
        out_specs=BS((tm, Dm), lambda i: (i, 0)),
        out_shape=SDS((S, Dm), BF16), name=name, compiler_params=_cp(("parallel",)))(x, g)


def rms_bwd(x, g, dh, dres, name):
    S, Dm = x.shape
    tm = min(512, S)

    def body(x_ref, g_ref, dh_ref, dres_ref, dx_ref, dxb_ref, dg_ref):
        i = pl.program_id(0)
        xv = x_ref[...]
        r = lax.rsqrt(jnp.mean(xv * xv, axis=-1, keepdims=True) + RMS_EPS)
        n = xv * r
        dhv = dh_ref[...]
        t = dhv * g_ref[...]
        dx = dres_ref[...] + r * (t - n * jnp.mean(n * t, axis=-1, keepdims=True))
        dx_ref[...] = dx
        dxb_ref[...] = dx.astype(BF16)
        part = jnp.sum(dhv * n, axis=0, keepdims=True)

        @pl.when(i == 0)
        def _():
            dg_ref[...] = part

        @pl.when(i > 0)
        def _():
            dg_ref[...] += part

    row = BS((tm, Dm), lambda i: (i, 0))
    vec = BS((1, Dm), lambda i: (0, 0))
    return pl.pallas_call(
        body, grid=(S // tm,), in_specs=[row, vec, row, row], out_specs=[row, row, vec],
        out_shape=[SDS((S, Dm), F32), SDS((S, Dm), BF16), SDS((1, Dm), F32)],
        name=name, compiler_params=_cp(("arbitrary",)))(x, g, dh, dres)


def loss_head(x, g, tgt, name):
    S, Dm = x.shape
    tm = min(512, S)

    def body(x_ref, g_ref, t_ref, loss_ref, dx_ref, dxb_ref, dg_ref):
        i = pl.program_id(0)
        xv = x_ref[...]
        gv = g_ref[...]
        r = lax.rsqrt(jnp.mean(xv * xv, axis=-1, keepdims=True) + RMS_EPS)
        n = xv * r
        err = n * gv - t_ref[...]
        lpart = 0.5 * jnp.sum(jnp.mean(err * err, axis=-1, keepdims=True), axis=0, keepdims=True)
        dout = err * (1.0 / Dm)
        t = dout * gv
        dx = r * (t - n * jnp.mean(n * t, axis=-1, keepdims=True))
        dx_ref[...] = dx
        dxb_ref[...] = dx.astype(BF16)
        part = jnp.sum(dout * n, axis=0, keepdims=True)

        @pl.when(i == 0)
        def _():
            dg_ref[...] = part
            loss_ref[...] = lpart

        @pl.when(i > 0)
        def _():
            dg_ref[...] += part
            loss_ref[...] += lpart

    row = BS((tm, Dm), lambda i: (i, 0))
    vec = BS((1, Dm), lambda i: (0, 0))
    one = BS((1, 1), lambda i: (0, 0))
    return pl.pallas_call(
        body, grid=(S // tm,), in_specs=[row, vec, row], out_specs=[one, row, row, vec],
        out_shape=[SDS((1, 1), F32), SDS((S, Dm), F32), SDS((S, Dm), BF16), SDS((1, Dm), F32)],
        name=name, compiler_params=_cp(("arbitrary",)))(x, g, tgt)


def mm(a, b, *, name, ta=False, tb=False, tm=512, tn=512, tk=None, out_dtype=F32, pre_a=None, epi=None,
       extras=()):
    M, K = (a.shape[1], a.shape[0]) if ta else a.shape
    N = b.shape[0] if tb else b.shape[1]
    tm, tn = min(tm, M), min(tn, N)
    tk = K if tk is None else min(tk, K)
    assert M % tm == 0 and N % tn == 0 and K % tk == 0, (name, M, N, K, tm, tn, tk)
    nk = K // tk
    ne = len(extras)
    a_spec = BS((tk, tm), lambda i, j, k: (k, i)) if ta else BS((tm, tk), lambda i, j, k: (i, k))
    b_spec = BS((tn, tk), lambda i, j, k: (j, k)) if tb else BS((tk, tn), lambda i, j, k: (k, j))
    o_spec = BS((tm, tn), lambda i, j, k: (i, j))
    dims = (((0 if ta else 1,), (1 if tb else 0,)), ((), ()))

    def body(a_ref, b_ref, *rest):
        e_refs, o_ref = rest[:ne], rest[ne]
        av = a_ref[...]
        if pre_a is not None:
            av = pre_a(av)
        p = lax.dot_general(_bf(av), _bf(b_ref[...]), dims, preferred_element_type=F32)

        def finish(acc):
            res = epi(acc, *[e[...] for e in e_refs]) if epi is not None else acc
            o_ref[...] = res.astype(o_ref.dtype)

        if nk == 1:
            finish(p)
        else:
            acc_ref = rest[ne + 1]
            k = pl.program_id(2)

            @pl.when(k == 0)
            def _():
                acc_ref[...] = p

            @pl.when(k > 0)
            def _():
                acc_ref[...] += p

            @pl.when(k == nk - 1)
            def _():
                finish(acc_ref[...])

    return pl.pallas_call(
        body, grid=(M // tm, N // tn, nk), in_specs=[a_spec, b_spec] + [o_spec] * ne, out_specs=o_spec,
        out_shape=SDS((M, N), out_dtype),
        scratch_shapes=[pltpu.VMEM((tm, tn), F32)] if nk > 1 else [],
        name=name, compiler_params=_cp(("parallel", "parallel", "arbitrary")))(a, b, *extras)


def _relu(acc):
    return jnp.maximum(acc, 0.0)


def _add(acc, res):
    return acc + res


def _sq(av):
    af = av.astype(F32)
    return af * af


def _times_2r(acc, r):
    return acc * (2.0 * r.astype(F32))


def mlp_fwd(x, g, w1, w2, tag):
    h = rms_fwd(x, g, f"{tag}_rms")
    r = mm(h, w1, name=f"{tag}_up", tn=1024, out_dtype=BF16, epi=_relu)
    xn = mm(r, w2, name=f"{tag}_down", pre_a=_sq, epi=_add, extras=(x,))
    return xn, (h, r)


def mlp_bwd(x, g, w1, w2, saved, dx, dxb, tag):
    h, r = saved
    da = mm(dxb, w2, name=f"{tag}_dact", tb=True, tn=1024, out_dtype=BF16, epi=_times_2r, extras=(r,))
    dw2 = mm(r, dxb, name=f"{tag}_dw2", ta=True, pre_a=_sq)
    dw1 = mm(h, da, name=f"{tag}_dw1", ta=True)
    dh = mm(da, w1, name=f"{tag}_dh", tb=True)
    dx, dxb, dg = rms_bwd(x, g, dh, dx, f"{tag}_rmsb")
    return dx, dxb, dg, dw1, dw2


def _conv_taps(x, S):
    t = lax.broadcasted_iota(jnp.int32, x.shape, 0)
    taps = []
    for j in range(GDN_CONV):
        sh = j - GDN_CONV // 2
        xs = x if sh == 0 else pltpu.roll(x, (-sh) % S, 0)
        taps.append(jnp.where((t + sh >= 0) & (t + sh < S), xs, 0.0))
    return taps


def _qkv_scale(c):
    is_norm = c < 2 * GDN_H
    scale = jnp.where(c < GDN_H, GDN_DK ** -0.5, 1.0)
    return is_norm, scale


def gdn_pre_fwd(proj, convw, name):
    S = proj.shape[0]

    def body(p_ref, w_ref, o_ref):
        c = pl.program_id(0)
        x = p_ref[...]
        w = w_ref[...]
        y = jnp.zeros_like(x)
        for j, xs in enumerate(_conv_taps(x, S)):
            y = y + w[j:j + 1, :] * xs
        t = y * _sigmoid(y)
        is_norm, scale = _qkv_scale(c)
        r = lax.rsqrt(jnp.sum(t * t, axis=-1, keepdims=True) + 1e-6)
        o_ref[...] = jnp.where(is_norm, t * r * scale, t)

    return pl.pallas_call(
        body, grid=(GDN_QKV // LANES,),
        in_specs=[BS((S, LANES), lambda c: (0, c)), BS((8, LANES), lambda c: (0, c))],
        out_specs=BS((S, LANES), lambda c: (0, c)),
        out_shape=SDS((S, GDN_QKV), F32), name=name, compiler_params=_cp(("parallel",)))(proj, convw)


def gdn_pre_bwd(proj, convw, dqkv, name):
    S = proj.shape[0]

    def body(p_ref, w_ref, d_ref, dp_ref, dw_ref):
        c = pl.program_id(0)
        x = p_ref[...]
        w = w_ref[...]
        taps = _conv_taps(x, S)
        y = jnp.zeros_like(x)
        for j, xs in enumerate(taps):
            y = y + w[j:j + 1, :] * xs
        sg = _sigmoid(y)
        t = y * sg
        is_norm, scale = _qkv_scale(c)
        dout = d_ref[0, 0] + d_ref[1, 0]
        r = lax.rsqrt(jnp.sum(t * t, axis=-1, keepdims=True) + 1e-6)
        n = t * r
        dn = dout * scale
        dt_norm = r * (dn - n * jnp.sum(dn * n, axis=-1, keepdims=True))
        dt = jnp.where(is_norm, dt_norm, dout)
        dy = dt * (sg * (1.0 + y * (1.0 - sg)))
        row = lax.broadcasted_iota(jnp.int32, (8, LANES), 0)
        dw = jnp.zeros((8, LANES), F32)
        for j, xs in enumerate(taps):
            dw = dw + jnp.where(row == j, jnp.sum(dy * xs, axis=0, keepdims=True), 0.0)
        dw_ref[...] = dw
        tt = lax.broadcasted_iota(jnp.int32, x.shape, 0)
        dx = jnp.zeros_like(x)
        for j in range(GDN_CONV):
            sh = j - GDN_CONV // 2
            ds = dy if sh == 0 else pltpu.roll(dy, sh % S, 0)
            dx = dx + w[j:j + 1, :] * jnp.where((tt - sh >= 0) & (tt - sh < S), ds, 0.0)
        dp_ref[...] = dx.astype(BF16)

    return pl.pallas_call(
        body, grid=(GDN_QKV // LANES,),
        in_specs=[BS((S, LANES), lambda c: (0, c)), BS((8, LANES), lambda c: (0, c)),
                  BS((2, 1, S, LANES), lambda c: (0, c // GDN_H, 0, c % GDN_H))],
        out_specs=[BS((S, LANES), lambda c: (0, c)), BS((8, LANES), lambda c: (0, c))],
        out_shape=[SDS((S, GDN_QKV), BF16), SDS((8, GDN_QKV), F32)],
        name=name, compiler_params=_cp(("parallel",)))(proj, convw, dqkv)


def _chunk_sum_matrix(n, upper):
    i = lax.broadcasted_iota(jnp.int32, (n, n), 0)
    j = lax.broadcasted_iota(jnp.int32, (n, n), 1)
    same = (i // GDN_C) == (j // GDN_C)
    tri = (i <= j) if upper else (i >= j)
    return jnp.where(same & tri, 1.0, 0.0).astype(F32)


def _gate_lanes(shape):
    lane = lax.broadcasted_iota(jnp.int32, shape, 1)
    return lane < GDN_H, (lane >= GDN_H) & (lane < 2 * GDN_H), (lane >= 2 * GDN_H) & (lane < 4 * GDN_H)


def gdn_gate_fwd(proj, prm, name):
    S = proj.shape[0]
    tm = min(512, S)
    ct = GDN_INP // LANES - 1

    def body(p_ref, prm_ref, o_ref):
        ab = p_ref[...]
        a_log = prm_ref[0:1, :]
        dtb = prm_ref[1:2, :]
        z = ab + dtb
        sp = jnp.maximum(z, 0.0) + jnp.log(1.0 + jnp.exp(-jnp.abs(z)))
        g = -jnp.exp(a_log) * sp
        is_f, is_b, is_beta = _gate_lanes(ab.shape)
        gf = _dot(_chunk_sum_matrix(tm, False), jnp.where(is_f, g, 0.0), HI)
        gbk = _dot(_chunk_sum_matrix(tm, True), jnp.where(is_b, g, 0.0), HI)
        o_ref[...] = gf + gbk + jnp.where(is_beta, _sigmoid(ab), 0.0)

    return pl.pallas_call(
        body, grid=(S // tm,),
        in_specs=[BS((tm, LANES), lambda i: (i, ct)), BS((8, LANES), lambda i: (0, 0))],
        out_specs=BS((tm, LANES), lambda i: (i, 0)),
        out_shape=SDS((S, LANES), F32), name=name, compiler_params=_cp(("parallel",)))(proj, prm)


def gdn_gate_bwd(proj, prm, dgb, name):
    S = proj.shape[0]
    tm = min(512, S)
    ct = GDN_INP // LANES - 1

    def body(p_ref, prm_ref, d_ref, dab_ref, dprm_ref):
        i = pl.program_id(0)
        ab = p_ref[...]
        a_log = prm_ref[0:1, :]
        dtb = prm_ref[1:2, :]
        z = ab + dtb
        sp = jnp.maximum(z, 0.0) + jnp.log(1.0 + jnp.exp(-jnp.abs(z)))
        ea = jnp.exp(a_log)
        g = -ea * sp
        is_f, is_b, is_beta = _gate_lanes(ab.shape)
        d = d_ref[...]
        dg = (_dot_tn(_chunk_sum_matrix(tm, False), jnp.where(is_f, d, 0.0), HI)
              + _dot_tn(_chunk_sum_matrix(tm, True), jnp.where(is_b, d, 0.0), HI))
        da = dg * (-ea) * _sigmoid(z)
        beta = _sigmoid(ab)
        dab_ref[...] = jnp.where(is_beta, d * beta * (1.0 - beta), da).astype(BF16)
        row = lax.broadcasted_iota(jnp.int32, (8, LANES), 0)
        part = (jnp.where(row == 0, jnp.sum(dg * g, axis=0, keepdims=True), 0.0)
                + jnp.where(row == 1, jnp.sum(da, axis=0, keepdims=True), 0.0))

        @pl.when(i == 0)
        def _():
            dprm_ref[...] = part

        @pl.when(i > 0)
        def _():
            dprm_ref[...] += part

    return pl.pallas_call(
        body, grid=(S // tm,),
        in_specs=[BS((tm, LANES), lambda i: (i, ct)), BS((8, LANES), lambda i: (0, 0)), BS((tm, LANES), lambda i: (i, 0))],
        out_specs=[BS((tm, LANES), lambda i: (i, 0)), BS((8, LANES), lambda i: (0, 0))],
        out_shape=[SDS((S, LANES), BF16), SDS((8, LANES), F32)],
        name=name, compiler_params=_cp(("arbitrary",)))(proj, prm, dgb)


def _tri_masks(d):
    i = lax.broadcasted_iota(jnp.int32, (GDN_C, GDN_C), 0)
    j = lax.broadcasted_iota(jnp.int32, (GDN_C, GDN_C), 1)
    s = (i - j) * (1 - 2 * d)
    return s >= 0, s > 0


def _split(a):
    hi = _bf(a)
    return hi, _bf(a - hi.astype(F32))


def _dot3(a, b):
    return _dot(a[0], b[0]) + (_dot(a[0], b[1]) + _dot(a[1], b[0]))


def _inv_unit_tri_many(mats):
    i = lax.broadcasted_iota(jnp.int32, mats[0].shape, 0)
    j = lax.broadcasted_iota(jnp.int32, mats[0].shape, 1)
    eye = jnp.where(i == j, 1.0, 0.0)
    ms = [-a for a in mats]
    ps = [eye + m for m in ms]
    for _ in range(int(math.log2(GDN_C)) - 1):
        sp = [_split(m) for m in ms]
        ms = [_dot3(s, s) for s in sp]
        sp = [_split(m) for m in ms]
        pp = [_split(p) for p in ps]
        ps = [p + _dot3(a, b) for p, a, b in zip(ps, pp, sp)]
    return ps


def _lane_col(x, lane_idx):
    lane = lax.broadcasted_iota(jnp.int32, x.shape, 1)
    return jnp.sum(jnp.where(lane == lane_idx, x, 0.0), axis=1, keepdims=True)


def _chunk_gates(gb_ref, grow_ref, hh, ci, d, head):
    gbv = gb_ref[ci * GDN_C:(ci + 1) * GDN_C, :]
    gcol = _lane_col(gbv, d * GDN_H + head)
    bcol = _lane_col(gbv, 2 * GDN_H + d * GDN_H + head)
    glast = jnp.where(d == 0, gcol[GDN_C - 1:GDN_C, :], gcol[0:1, :])
    return gcol, bcol, grow_ref[hh, ci:ci + 1, :], glast


def _chunk_base(q, k, gcol, grow, bcol, glast, d):
    incl, strict = _tri_masks(d)
    decay = jnp.where(incl, jnp.exp(jnp.where(incl, gcol - grow, 0.0)), 0.0)
    kb = k * bcol
    kk = _dot_nt(_bf(kb), _bf(k))
    qk = _dot_nt(_bf(q), _bf(k))
    eg = jnp.exp(gcol)
    ek = jnp.exp(glast - gcol)
    return dict(incl=incl, strict=strict, decay=decay, kb=kb, kk=kk, qk=qk, eg=eg, ek=ek, q_dec=q * eg, k_dec=k * ek,
                bcol=bcol, glast=glast)


def _block_terms(q_ref, k_ref, v_ref, gb_ref, grow_ref, d, h):
    keys = [(hh, ci) for hh in range(GDN_HP) for ci in range(GDN_GC)]
    ts = []
    for hh, ci in keys:
        rows = slice(ci * GDN_C, (ci + 1) * GDN_C)
        cols = slice(hh * GDN_DK, (hh + 1) * GDN_DK)
        gcol, bcol, grow_v, glast = _chunk_gates(gb_ref, grow_ref, hh, ci, d, h * GDN_HP + hh)
        t = _chunk_base(q_ref[rows, cols], k_ref[rows, cols], gcol, grow_v, bcol, glast, d)
        t["v"] = v_ref[rows, cols]
        ts.append(t)
    tinvs = _inv_unit_tri_many([jnp.where(t["strict"], t["kk"] * t["decay"], 0.0) for t in ts])
    sp = [_split(x) for x in tinvs]
    us = [_dot3(s, _split(t["v"] * t["bcol"])) for s, t in zip(sp, ts)]
    ws = [_dot3(s, _split(t["kb"] * t["eg"])) for s, t in zip(sp, ts)]
    for t, tinv, u, w in zip(ts, tinvs, us, ws):
        t.update(tinv=tinv, u=u, w=w)
    return keys, ts


def _gdn_specs(S, nblk, order):
    R = GDN_GC * GDN_C
    wd = GDN_HP * GDN_DK
    hb = GDN_H // GDN_HP

    def qkv_spec(part):
        return BS((R, wd), lambda d, h, n: (order(d, n), part * hb + h))

    gb_spec = BS((R, LANES), lambda d, h, n: (order(d, n), 0))
    grow_spec = BS((GDN_HP, GDN_GC, GDN_C), lambda d, h, n: (d * hb + h, order(d, n), 0))
    st_spec = BS((1, GDN_HP, GDN_GC, GDN_DK, GDN_DK), lambda d, h, n: (d, h, order(d, n), 0, 0))
    return qkv_spec, gb_spec, grow_spec, st_spec


def _lane_row(x):
    return jnp.broadcast_to(x, (1, LANES))


def _side_parts(side):
    if side is None:
        return [], [], [], [], []
    return [ANY] * len(side.ins), [ANY] * len(side.outs), list(side.outs), list(side.sems), list(side.ins)


def _side_run(side, refs, n_in, n_out, n_scr, first, last):
    if side is None:
        return
    ns, no, nm = len(side.ins), len(side.outs), len(side.sems)
    s_in = refs[n_in:n_in + ns]
    s_out = refs[n_in + ns + n_out:n_in + ns + n_out + no]
    s_sem = refs[len(refs) - nm:]

    @pl.when(first)
    def _():
        side.start(s_in, s_out, s_sem)

    @pl.when(last)
    def _():
        side.wait(s_in, s_out, s_sem)


def gdn_scan_fwd(qkv, gb, grow, name, side=None):
    S = qkv.shape[0]
    R = GDN_GC * GDN_C
    nblk = S // R
    nc = S // GDN_C
    wd = GDN_HP * GDN_DK
    heads = range(GDN_HP)

    def order(d, n):
        return n + d * (nblk - 1 - 2 * n)

    qkv_spec, gb_spec, grow_spec, st_spec = _gdn_specs(S, nblk, order)

    s_in, s_out, s_shape, s_scr_shapes, s_ops = _side_parts(side)
    hb = GDN_H // GDN_HP

    def body(*refs):
        q_ref, k_ref, v_ref, gb_ref, grow_ref = refs[:5]
        o_ref, st_ref = refs[5 + len(s_in):7 + len(s_in)]
        s_scr, u_scr, w_scr, qd_scr, kd_scr, in_scr, egl_scr = refs[7 + len(s_in) + len(s_out):14 + len(s_in) + len(s_out)]
        d = pl.program_id(0)
        h = pl.program_id(1)
        n = pl.program_id(2)
        _side_run(side, refs, 5, 2, 7, (d == 0) & (h == 0) & (n == 0), (d == 1) & (h == hb - 1) & (n == nblk - 1))

        @pl.when(n == 0)
        def _():
            s_scr[...] = jnp.zeros_like(s_scr)

        keys, ts = _block_terms(q_ref, k_ref, v_ref, gb_ref, grow_ref, d, h)
        for (hh, ci), t in zip(keys, ts):
            u_scr[hh, ci] = t["u"]
            w_scr[hh, ci] = _bf(t["w"])
            qd_scr[hh, ci] = _bf(t["q_dec"])
            kd_scr[hh, ci] = _bf(t["k_dec"])
            in_scr[hh, ci] = _bf(jnp.where(t["incl"], t["qk"] * t["decay"], 0.0))
            egl_scr[hh, ci] = _lane_row(jnp.exp(t["glast"]))

        def chunk(cc, carry):
            ci = cc + d * (GDN_GC - 1 - 2 * cc)
            rows = pl.ds(pl.multiple_of(ci * GDN_C, GDN_C), GDN_C)
            sts = [s_scr[hh] for hh in heads]
            for hh in heads:
                st_ref[0, hh, ci] = sts[hh]
            sbs = [_bf(st) for st in sts]
            vns = [_bf(u_scr[hh, ci] - _dot(w_scr[hh, ci], sbs[hh])) for hh in heads]
            for hh in heads:
                s_scr[hh] = sts[hh] * egl_scr[hh, ci] + _dot_tn(kd_scr[hh, ci], vns[hh])
            for hh in heads:
                o_ref[0, rows, hh * GDN_DK:(hh + 1) * GDN_DK] = _dot(qd_scr[hh, ci], sbs[hh]) + _dot(in_scr[hh, ci], vns[hh])
            return carry

        lax.fori_loop(0, GDN_GC, chunk, 0)

    blk = (GDN_HP, GDN_GC, GDN_C, GDN_DK)
    return pl.pallas_call(
        body, grid=(2, GDN_H // GDN_HP, nblk),
        in_specs=[qkv_spec(0), qkv_spec(1), qkv_spec(2), gb_spec, grow_spec] + s_in,
        out_specs=[BS((1, R, wd), lambda d, h, n: (d, order(d, n), h)), st_spec] + s_out,
        out_shape=[SDS((2, S, GDN_H * GDN_DK), F32), SDS((2, GDN_H, nc, GDN_DK, GDN_DK), F32)] + s_shape,
        scratch_shapes=[pltpu.VMEM((GDN_HP, GDN_DK, GDN_DK), F32), pltpu.VMEM(blk, F32), pltpu.VMEM(blk, BF16),
                        pltpu.VMEM(blk, BF16), pltpu.VMEM(blk, BF16), pltpu.VMEM((GDN_HP, GDN_GC, GDN_C, GDN_C), BF16),
                        pltpu.VMEM((GDN_HP, GDN_GC, 1, LANES), F32)] + s_scr_shapes,
        name=name, compiler_params=_cp(("arbitrary", "arbitrary", "arbitrary")))(qkv, qkv, qkv, gb, grow, *s_ops)


def gdn_scan_bwd(qkv, gb, grow, states, do, name, side=None):
    S = qkv.shape[0]
    R = GDN_GC * GDN_C
    nblk = S // R
    wd = GDN_HP * GDN_DK
    heads = range(GDN_HP)

    def order(d, n):
        return (nblk - 1 - n) - d * (nblk - 1 - 2 * n)

    qkv_spec, gb_spec, grow_spec, st_spec = _gdn_specs(S, nblk, order)

    s_in, s_out, s_shape, s_scr_shapes, s_ops = _side_parts(side)
    hb = GDN_H // GDN_HP

    def body(*refs):
        q_ref, k_ref, v_ref, gb_ref, grow_ref, st_ref, do_ref = refs[:7]
        dqkv_ref, dgate_ref = refs[7 + len(s_in):9 + len(s_in)]
        (ds_scr, w_scr, kd_scr, dv1_scr, qtdo_scr, egl_scr, dsin_scr, dvn_scr,
         sdot_scr) = refs[9 + len(s_in) + len(s_out):18 + len(s_in) + len(s_out)]
        d = pl.program_id(0)
        h = pl.program_id(1)
        n = pl.program_id(2)
        _side_run(side, refs, 7, 2, 9, (d == 0) & (h == 0) & (n == 0), (d == 1) & (h == hb - 1) & (n == nblk - 1))

        @pl.when(n == 0)
        def _():
            ds_scr[...] = jnp.zeros_like(ds_scr)

        keys, ts = _block_terms(q_ref, k_ref, v_ref, gb_ref, grow_ref, d, h)
        for (hh, ci), t in zip(keys, ts):
            rows = slice(ci * GDN_C, (ci + 1) * GDN_C)
            t["wb"] = _bf(t["w"])
            t["dob"] = _bf(do_ref[rows, hh * GDN_DK:(hh + 1) * GDN_DK])
            t["sb"] = _bf(st_ref[0, hh, ci])
        for (hh, ci), t in zip(keys, ts):
            t["vnb"] = _bf(t["u"] - _dot(t["wb"], t["sb"]))
            w_scr[hh, ci] = t["wb"]
            kd_scr[hh, ci] = _bf(t["k_dec"])
            dv1_scr[hh, ci] = _dot_tn(_bf(jnp.where(t["incl"], t["qk"] * t["decay"], 0.0)), t["dob"])
            qtdo_scr[hh, ci] = _dot_tn(_bf(t["q_dec"]), t["dob"])
            egl_scr[hh, ci] = _lane_row(jnp.exp(t["glast"]))

        def chunk(cc, carry):
            ci = (GDN_GC - 1 - cc) - d * (GDN_GC - 1 - 2 * cc)
            dsns = [ds_scr[hh] for hh in heads]
            dsbs = [_bf(x) for x in dsns]
            dvns = [dv1_scr[hh, ci] + _dot(kd_scr[hh, ci], dsbs[hh]) for hh in heads]
            for hh in heads:
                ds_scr[hh] = qtdo_scr[hh, ci] + egl_scr[hh, ci] * dsns[hh] - _dot_tn(w_scr[hh, ci], _bf(dvns[hh]))
            for hh in heads:
                dsin_scr[hh, ci] = dsbs[hh]
                dvn_scr[hh, ci] = dvns[hh]
                sd = jnp.sum(jnp.sum(st_ref[0, hh, ci] * dsns[hh], axis=1, keepdims=True), axis=0, keepdims=True)
                sdot_scr[hh, ci] = _lane_row(sd)
            return carry

        lax.fori_loop(0, GDN_GC, chunk, 0)

        for (hh, ci), t in zip(keys, ts):
            t["d_vnew"] = dvn_scr[hh, ci]
            t["dvb"] = _bf(t["d_vnew"])
            t["dsb"] = dsin_scr[hh, ci]
        for t in ts:
            t["d_intra"] = jnp.where(t["incl"], _dot_nt(t["dob"], t["vnb"]), 0.0)
            t["d_qdec"] = _dot_nt(t["dob"], t["sb"])
            t["d_kdec"] = _dot_nt(t["vnb"], t["dsb"])
            t["dw"] = -_dot_nt(t["dvb"], t["sb"])
        for t in ts:
            tts = _split(t["tinv"].T)
            t["d_ru"] = _dot3(tts, _split(t["d_vnew"]))
            t["d_rw"] = _dot3(tts, _split(t["dw"]))
        for t in ts:
            t["da"] = -jnp.where(t["strict"], _dot_nt(_bf(t["d_ru"]), _bf(t["u"])) + _dot_nt(_bf(t["d_rw"]), t["wb"]), 0.0)
        for (hh, ci), t in zip(keys, ts):
            rows = slice(ci * GDN_C, (ci + 1) * GDN_C)
            cols = slice(hh * GDN_DK, (hh + 1) * GDN_DK)
            q, k, v = q_ref[rows, cols], k_ref[rows, cols], t["v"]
            decay, kb, eg, ek, bcol = t["decay"], t["kb"], t["eg"], t["ek"], t["bcol"]
            d_ru, d_rw, da, d_intra, d_qdec, d_kdec = t["d_ru"], t["d_rw"], t["da"], t["d_intra"], t["d_qdec"], t["d_kdec"]
            kbf, qbf = _bf(k), _bf(q)
            dgl = egl_scr[hh, ci][:, 0:1] * sdot_scr[hh, ci][:, 0:1]
            dv = d_ru * bcol
            dbeta = jnp.sum(d_ru * v, axis=1, keepdims=True)
            dkb = d_rw * eg
            dg = jnp.sum(d_rw * kb, axis=1, keepdims=True) * eg
            dkk = _bf(da * decay)
            dqk = _bf(d_intra * decay)
            dkb = dkb + _dot(dkk, kbf)
            dk = _dot_tn(dkk, _bf(kb)) + _dot_tn(dqk, qbf)
            dq = _dot(dqk, kbf) + d_qdec * eg
            dd = (da * t["kk"] + d_intra * t["qk"]) * decay
            dg = dg + jnp.sum(dd, axis=1, keepdims=True) - jnp.sum(dd.T, axis=1, keepdims=True)
            dg = dg + jnp.sum(d_qdec * t["q_dec"], axis=1, keepdims=True)
            dk = dk + d_kdec * ek
            ee = jnp.sum(d_kdec * t["k_dec"], axis=1, keepdims=True)
            dg = dg - ee
            dgl = dgl + jnp.sum(ee, axis=0, keepdims=True)
            dk = dk + dkb * bcol
            dbeta = dbeta + jnp.sum(dkb * k, axis=1, keepdims=True)
            ridx = lax.broadcasted_iota(jnp.int32, (GDN_C, 1), 0)
            dg = dg + jnp.where(ridx == (GDN_C - 1) * (1 - d), dgl, 0.0)
            dqkv_ref[0, 0, rows, cols] = dq
            dqkv_ref[0, 1, rows, cols] = dk
            dqkv_ref[0, 2, rows, cols] = dv
            lane2 = lax.broadcasted_iota(jnp.int32, (GDN_C, 2), 1)
            dgate_ref[0, hh, rows, :] = jnp.where(lane2 == 0, dg, dbeta)

    blk = (GDN_HP, GDN_GC, GDN_C, GDN_DK)
    sq = (GDN_HP, GDN_GC, GDN_DK, GDN_DK)
    row = (GDN_HP, GDN_GC, 1, LANES)
    return pl.pallas_call(
        body, grid=(2, GDN_H // GDN_HP, nblk),
        in_specs=[qkv_spec(0), qkv_spec(1), qkv_spec(2), gb_spec, grow_spec, st_spec,
                  BS((R, wd), lambda d, h, n: (order(d, n), h))] + s_in,
        out_specs=[BS((1, 3, R, wd), lambda d, h, n: (d, 0, order(d, n), h)),
                   BS((1, GDN_HP, R, 2), lambda d, h, n: (d, h, order(d, n), 0))] + s_out,
        out_shape=[SDS((2, 3, S, GDN_H * GDN_DK), F32), SDS((2, GDN_H, S, 2), F32)] + s_shape,
        scratch_shapes=[pltpu.VMEM((GDN_HP, GDN_DK, GDN_DK), F32), pltpu.VMEM(blk, BF16), pltpu.VMEM(blk, BF16),
                        pltpu.VMEM(blk, F32), pltpu.VMEM(sq, F32), pltpu.VMEM(row, F32), pltpu.VMEM(sq, BF16),
                        pltpu.VMEM(blk, F32), pltpu.VMEM(row, F32)] + s_scr_shapes,
        name=name, compiler_params=_cp(("arbitrary", "arbitrary", "arbitrary")))(qkv, qkv, qkv, gb, grow, states, do, *s_ops)


def gdn_post_fwd(o2, proj, nw, name):
    S = proj.shape[0]
    tm = min(512, S)
    zoff = GDN_QKV // LANES

    def body(o_ref, z_ref, nw_ref, y_ref):
        o = o_ref[0] + o_ref[1]
        z = z_ref[...]
        r = lax.rsqrt(jnp.mean(o * o, axis=-1, keepdims=True) + RMS_EPS)
        y_ref[...] = (o * r * nw_ref[...] * (z * _sigmoid(z))).astype(BF16)

    return pl.pallas_call(
        body, grid=(S // tm, GDN_H),
        in_specs=[BS((2, tm, LANES), lambda i, h: (0, i, h)), BS((tm, LANES), lambda i, h: (i, zoff + h)),
                  BS((1, LANES), lambda i, h: (0, 0))],
        out_specs=BS((tm, LANES), lambda i, h: (i, h)),
        out_shape=SDS((S, GDN_H * GDN_DK), BF16), name=name, compiler_params=_cp(("parallel", "parallel")))(o2, proj, nw)


def gdn_post_bwd(o2, proj, nw, dy, name):
    S = proj.shape[0]
    tm = min(512, S)
    zoff = GDN_QKV // LANES

    def body(o_ref, z_ref, nw_ref, dy_ref, do_ref, dz_ref, dnw_ref):
        first = (pl.program_id(0) == 0) & (pl.program_id(1) == 0)
        o = o_ref[0] + o_ref[1]
        z = z_ref[...]
        nwv = nw_ref[...]
        dyv = dy_ref[...]
        r = lax.rsqrt(jnp.mean(o * o, axis=-1, keepdims=True) + RMS_EPS)
        n = o * r
        sg = _sigmoid(z)
        sz = z * sg
        dz_ref[...] = (dyv * n * nwv * (sg * (1.0 + z * (1.0 - sg)))).astype(BF16)
        dn = dyv * nwv * sz
        do_ref[...] = r * (dn - n * jnp.mean(dn * n, axis=-1, keepdims=True))
        part = jnp.sum(dyv * n * sz, axis=0, keepdims=True)

        @pl.when(first)
        def _():
            dnw_ref[...] = part

        @pl.when(jnp.logical_not(first))
        def _():
            dnw_ref[...] += part

    blk = BS((tm, LANES), lambda i, h: (i, h))
    return pl.pallas_call(
        body, grid=(S // tm, GDN_H),
        in_specs=[BS((2, tm, LANES), lambda i, h: (0, i, h)), BS((tm, LANES), lambda i, h: (i, zoff + h)),
                  BS((1, LANES), lambda i, h: (0, 0)), blk],
        out_specs=[blk, blk, BS((1, LANES), lambda i, h: (0, 0))],
        out_shape=[SDS((S, GDN_H * GDN_DK), F32), SDS((S, GDN_H * GDN_DK), BF16), SDS((1, LANES), F32)],
        name=name, compiler_params=_cp(("arbitrary", "arbitrary")))(o2, proj, nw, dy)


def _gate_prm(a_log, dt_bias):
    z = jnp.zeros((8, LANES), F32)
    z = z.at[0, :2 * GDN_H].set(a_log.reshape(-1))
    return z.at[1, :2 * GDN_H].set(dt_bias.reshape(-1))


def gdn_fwd(x, g, w_all, convw, a_log, dt_bias, nw, w_out, tag, side=None):
    S = x.shape[0]
    h = rms_fwd(x, g, f"{tag}_rms")
    proj = mm(h, w_all, name=f"{tag}_proj", tn=1408)
    qkv = gdn_pre_fwd(proj, convw, f"{tag}_pre")
    prm = _gate_prm(a_log, dt_bias)
    gb = gdn_gate_fwd(proj, prm, f"{tag}_gate")
    grow = gb[:, :2 * GDN_H].T.reshape(2 * GDN_H, S // GDN_C, GDN_C)
    o2, states, *side_out = gdn_scan_fwd(qkv, gb, grow, f"{tag}_scan", side)
    y = gdn_post_fwd(o2, proj, nw, f"{tag}_post")
    xn = mm(y, w_out, name=f"{tag}_out", epi=_add, extras=(x,))
    return xn, (h, proj, qkv, prm, gb, grow, o2, states, y), side_out


def gdn_bwd(x, g, w_all, convw, nw, w_out, saved, dx, dxb, tag, side=None):
    S = x.shape[0]
    h, proj, qkv, prm, gb, grow, o2, states, y = saved
    dw_out = mm(y, dxb, name=f"{tag}_dwout", ta=True)
    dy = mm(dxb, w_out, name=f"{tag}_dy", tb=True)
    do, dz, dnw = gdn_post_bwd(o2, proj, nw, dy, f"{tag}_postb")
    dqkv, dgate, *side_out = gdn_scan_bwd(qkv, gb, grow, states, do, f"{tag}_scanb", side)
    dgb = jnp.transpose(dgate, (2, 3, 0, 1)).reshape(S, 4 * GDN_H)
    dgb = jnp.pad(dgb, ((0, 0), (0, LANES - 4 * GDN_H)))
    dab, dprm = gdn_gate_bwd(proj, prm, dgb, f"{tag}_gateb")
    dpq, dconvw = gdn_pre_bwd(proj, convw, dqkv, f"{tag}_preb")
    dproj = jnp.concatenate([dpq, dz, dab], axis=1)
    dw_all = mm(h, dproj, name=f"{tag}_dwin", ta=True, tn=384)
    dh = mm(dproj, w_all, name=f"{tag}_dh", tb=True, tk=1408)
    dx, dxb, dg = rms_bwd(x, g, dh, dx, f"{tag}_rmsb")
    da_log = dprm[0, :2 * GDN_H].reshape(2, GDN_H)
    ddt = dprm[1, :2 * GDN_H].reshape(2, GDN_H)
    return dx, dxb, dg, dw_all, dconvw, da_log, ddt, dnw, dw_out, side_out


def _rel_bucket_np(rel):
    nb = REL_BUCKETS // 2
    max_exact = nb // 2
    ret = np.where(rel > 0, nb, 0)
    n = np.abs(rel)
    nf = np.maximum(n, 1).astype(np.float32)
    large = max_exact + (np.log(nf / max_exact) / np.float32(math.log(REL_MAX_DIST / max_exact))
                         * (nb - max_exact)).astype(np.int32)
    large = np.minimum(large, nb - 1)
    return ret + np.where(n < max_exact, n, large)


def _toeplitz(f, rows, cols):
    period = rows + cols
    e = jnp.pad(f, ((0, 0), (0, period - f.shape[1])))
    y = jnp.tile(e, (1, rows))[:, :rows * (period - 1)]
    return y.reshape(f.shape[0], rows, period - 1)[:, :, :cols]


ATT_Q = DSWA_HALF
ATT_W = 3 * DSWA_HALF
ATT_TB = 1024
ATT_PAIRS = DSWA_HG // 2


def _bias_mats(rel_table, gi):
    _, dil = DSWA_CFG[gi]
    offs = np.arange(-DSWA_HALF, DSWA_HALF + 1)
    onehot = jnp.asarray(np.eye(REL_BUCKETS, dtype=np.float32)[_rel_bucket_np(offs * dil)])
    f = jnp.dot(onehot, rel_table, precision=HI)[:, gi * DSWA_HG:(gi + 1) * DSWA_HG].T
    bias = _toeplitz(f, ATT_Q, ATT_W)
    bias_t = jnp.transpose(_toeplitz(f[:, ::-1], ATT_Q, ATT_W), (0, 2, 1))
    return bias.reshape(ATT_PAIRS, 2, ATT_Q, ATT_W), bias_t.reshape(ATT_PAIRS, 2, ATT_W, ATT_Q)


def _att_specs(S, d, col):
    halo = DSWA_HALF * d
    per = ATT_TB // halo
    last = S // halo - 1
    cur = BS((ATT_TB, LANES), lambda p, tb: (tb, col(p)))
    prev = BS((halo, LANES), lambda p, tb: (jnp.maximum(tb * per - 1, 0), col(p)))
    nxt = BS((halo, LANES), lambda p, tb: (jnp.minimum((tb + 1) * per, last), col(p)))
    return prev, cur, nxt


def _att_specs3(S, d, lead):
    halo = DSWA_HALF * d
    per = ATT_TB // halo
    last = S // halo - 1
    cur = BS((1, ATT_TB, LANES), lambda p, tb: (lead(p), tb, 0))
    prev = BS((1, halo, LANES), lambda p, tb: (lead(p), jnp.maximum(tb * per - 1, 0), 0))
    nxt = BS((1, halo, LANES), lambda p, tb: (lead(p), jnp.minimum((tb + 1) * per, last), 0))
    return prev, cur, nxt


class _Pieces:
    def __init__(self, prev, cur, nxt, d, lead=None, cast=None):
        self.refs, self.d, self.lead, self.cast, self.cache = (prev, cur, nxt), d, lead, cast, {}
        self.halo = DSWA_HALF * d
        self.nsb = ATT_TB // self.halo

    def __call__(self, r, sb):
        if (r, sb) not in self.cache:
            ref = self.refs[0] if sb < 0 else self.refs[2] if sb >= self.nsb else self.refs[1]
            start = r + (self.halo * sb if 0 <= sb < self.nsb else 0)
            rows = pl.ds(start, ATT_Q, stride=self.d) if self.d > 1 else pl.ds(start, ATT_Q)
            v = ref[rows, :] if self.lead is None else ref[0, rows, :]
            self.cache[(r, sb)] = v if self.cast is None else v.astype(self.cast)
        return self.cache[(r, sb)]

    def window(self, r, sb):
        return jnp.concatenate([self(r, sb - 1), self(r, sb), self(r, sb + 1)], axis=0)


ATT_GROUP = 8


def _tile_groups(d, nsb):
    tiles = [(r, sb) for r in range(d) for sb in range(nsb)]
    return [tiles[i:i + ATT_GROUP] for i in range(0, len(tiles), ATT_GROUP)]


def _tile_rows(r, sb, d):
    start = r + DSWA_HALF * d * sb
    return pl.ds(start, ATT_Q, stride=d) if d > 1 else pl.ds(start, ATT_Q)


def _tile_valid(tb, r, sb, d, S, transposed):
    shape = (ATT_W, ATT_Q) if transposed else (ATT_Q, ATT_W)
    blk = lax.broadcasted_iota(jnp.int32, shape, 1 if transposed else 0)
    win = lax.broadcasted_iota(jnp.int32, shape, 0 if transposed else 1)
    tok = tb * ATT_TB + r + d * (DSWA_HALF * (sb - 1) + win)
    return (jnp.abs(win - DSWA_HALF - blk) <= DSWA_HALF) & (tok >= 0) & (tok < S)


def _head_masks():
    lane = lax.broadcasted_iota(jnp.int32, (1, LANES), 1)
    return [lane < DSWA_E, lane >= DSWA_E], lane


def attn_fwd(qkv, bias, gi, name):
    S = qkv.shape[0]
    d = DSWA_CFG[gi][1]
    nsb = ATT_TB // (DSWA_HALF * d)
    npair = DSWA_HEADS // 2
    q_spec = _att_specs(S, d, lambda p: gi * ATT_PAIRS + p)[1]
    k_specs = _att_specs(S, d, lambda p: npair + gi * ATT_PAIRS + p)
    v_specs = _att_specs(S, d, lambda p: 2 * npair + gi * ATT_PAIRS + p)

    def body(q_ref, kp, kc, kn, vp, vc, vn, b_ref, o_ref, lse_ref):
        tb = pl.program_id(1)
        masks, lane = _head_masks()
        kpc = _Pieces(kp, kc, kn, d, cast=BF16)
        vpc = _Pieces(vp, vc, vn, d, cast=BF16)
        scale = DSWA_E ** -0.5
        for grp in _tile_groups(d, nsb):
            rows = [_tile_rows(r, sb, d) for r, sb in grp]
            qs = [q_ref[rw, :] for rw in rows]
            kws = [kpc.window(r, sb) for r, sb in grp]
            vws = [vpc.window(r, sb) for r, sb in grp]
            valids = [_tile_valid(tb, r, sb, d, S, False) for r, sb in grp]
            both = [(t, hh) for t in range(len(grp)) for hh in range(2)]
            ss = [_dot_nt(_bf(jnp.where(masks[hh], qs[t], 0.0)), kws[t]) * scale + b_ref[0, hh] for t, hh in both]
            ss = [jnp.where(valids[t], s, NEG_INF) for (t, hh), s in zip(both, ss)]
            ms = [jnp.max(s, axis=-1, keepdims=True) for s in ss]
            ps = [jnp.exp(s - m) for s, m in zip(ss, ms)]
            ls = [jnp.sum(p, axis=-1, keepdims=True) for p in ps]
            os = [_dot(_bf(p / l), vws[t]) for (t, hh), p, l in zip(both, ps, ls)]
            for t, rw in enumerate(rows):
                o_ref[rw, :] = jnp.where(masks[0], os[2 * t], os[2 * t + 1])
                lse_ref[0, rw, :] = (jnp.where(lane == 0, ms[2 * t] + jnp.log(ls[2 * t]), 0.0)
                                     + jnp.where(lane == 1, ms[2 * t + 1] + jnp.log(ls[2 * t + 1]), 0.0))

    return pl.pallas_call(
        body, grid=(ATT_PAIRS, S // ATT_TB),
        in_specs=[q_spec, *k_specs, *v_specs, BS((1, 2, ATT_Q, ATT_W), lambda p, tb: (p, 0, 0, 0))],
        out_specs=[BS((ATT_TB, LANES), lambda p, tb: (tb, p)), BS((1, ATT_TB, LANES), lambda p, tb: (p, tb, 0))],
        out_shape=[SDS((S, DSWA_HG * DSWA_E), F32), SDS((ATT_PAIRS, S, LANES), F32)],
        name=name, compiler_params=_cp(("parallel", "parallel")))(qkv, qkv, qkv, qkv, qkv, qkv, qkv, bias)


def attn_bwd_q(qkv, bias, lse, do, dd, gi, name):
    S = qkv.shape[0]
    d = DSWA_CFG[gi][1]
    nsb = ATT_TB // (DSWA_HALF * d)
    npair = DSWA_HEADS // 2
    q_spec = _att_specs(S, d, lambda p: gi * ATT_PAIRS + p)[1]
    k_specs = _att_specs(S, d, lambda p: npair + gi * ATT_PAIRS + p)
    v_specs = _att_specs(S, d, lambda p: 2 * npair + gi * ATT_PAIRS + p)
    bspec = BS((1, 2, ATT_Q, ATT_W), lambda p, tb: (p, 0, 0, 0))

    def body(q_ref, kp, kc, kn, vp, vc, vn, b_ref, lse_ref, do_ref, dd_ref, dq_ref, db_ref):
        tb = pl.program_id(1)
        masks, lane = _head_masks()
        kpc = _Pieces(kp, kc, kn, d, cast=BF16)
        vpc = _Pieces(vp, vc, vn, d, cast=BF16)
        db = [jnp.zeros((ATT_Q, ATT_W), F32), jnp.zeros((ATT_Q, ATT_W), F32)]
        scale = DSWA_E ** -0.5
        for grp in _tile_groups(d, nsb):
            rows = [_tile_rows(r, sb, d) for r, sb in grp]
            qs = [q_ref[rw, :] for rw in rows]
            dos = [do_ref[0, rw, :] for rw in rows]
            lses = [lse_ref[0, rw, :] for rw in rows]
            dds = [dd_ref[0, 0, rw, :] for rw in rows]
            kws = [kpc.window(r, sb) for r, sb in grp]
            vws = [vpc.window(r, sb) for r, sb in grp]
            valids = [_tile_valid(tb, r, sb, d, S, False) for r, sb in grp]
            both = [(t, hh) for t in range(len(grp)) for hh in range(2)]
            ss = [_dot_nt(_bf(jnp.where(masks[hh], qs[t], 0.0)), kws[t]) * scale + b_ref[0, hh] for t, hh in both]
            dps = [_dot_nt(_bf(jnp.where(masks[hh], dos[t], 0.0)), vws[t]) for t, hh in both]
            ps = [jnp.exp(jnp.where(valids[t], s - lses[t][:, hh:hh + 1], NEG_INF)) for (t, hh), s in zip(both, ss)]
            dss = [p * (dp - dds[t][:, hh:hh + 1]) for (t, hh), p, dp in zip(both, ps, dps)]
            dqs = [_dot(_bf(ds), kws[t]) * scale for (t, hh), ds in zip(both, dss)]
            for t, rw in enumerate(rows):
                dq_ref[rw, :] = jnp.where(masks[0], dqs[2 * t], dqs[2 * t + 1])
                db[0] = db[0] + dss[2 * t]
                db[1] = db[1] + dss[2 * t + 1]

        @pl.when(tb == 0)
        def _():
            db_ref[0, 0] = db[0]
            db_ref[0, 1] = db[1]

        @pl.when(tb > 0)
        def _():
            db_ref[0, 0] += db[0]
            db_ref[0, 1] += db[1]

    return pl.pallas_call(
        body, grid=(ATT_PAIRS, S // ATT_TB),
        in_specs=[q_spec, *k_specs, *v_specs, bspec, BS((1, ATT_TB, LANES), lambda p, tb: (p, tb, 0)),
                  BS((1, ATT_TB, LANES), lambda p, tb: (gi, tb, p)), BS((1, 1, ATT_TB, LANES), lambda p, tb: (gi, p, tb, 0))],
        out_specs=[BS((ATT_TB, LANES), lambda p, tb: (tb, p)), bspec],
        out_shape=[SDS((S, DSWA_HG * DSWA_E), F32), SDS((ATT_PAIRS, 2, ATT_Q, ATT_W), F32)],
        name=name, compiler_params=_cp(("parallel", "arbitrary")))(qkv, qkv, qkv, qkv, qkv, qkv, qkv, bias, lse, do, dd)


def attn_bwd_kv(qkv, bias_t, lse, do, dd, gi, name):
    S = qkv.shape[0]
    d = DSWA_CFG[gi][1]
    nsb = ATT_TB // (DSWA_HALF * d)
    npair = DSWA_HEADS // 2
    q_specs = _att_specs(S, d, lambda p: gi * ATT_PAIRS + p)
    k_spec = _att_specs(S, d, lambda p: npair + gi * ATT_PAIRS + p)[1]
    v_spec = _att_specs(S, d, lambda p: 2 * npair + gi * ATT_PAIRS + p)[1]
    halo = DSWA_HALF * d
    per = ATT_TB // halo
    last = S // halo - 1

    def do_spec(rows, blk):
        return BS((1, rows, LANES), lambda p, tb: (gi, blk(tb), p))

    def dd_spec(rows, blk):
        return BS((1, 1, rows, LANES), lambda p, tb: (gi, p, blk(tb), 0))

    blks = [(halo, lambda tb: jnp.maximum(tb * per - 1, 0)), (ATT_TB, lambda tb: tb),
            (halo, lambda tb: jnp.minimum((tb + 1) * per, last))]
    do_specs = [do_spec(*b) for b in blks]
    dd_specs = [dd_spec(*b) for b in blks]
    lse_specs = _att_specs3(S, d, lambda p: p)

    class _Lead4:
        def __init__(self, ref):
            self.ref = ref

        def __getitem__(self, idx):
            return self.ref[(0,) + idx]

    def body(k_ref, v_ref, qp, qc, qn, dop, doc, don, lp, lc, ln, ddp, ddc, ddn, b_ref, dk_ref, dv_ref):
        tb = pl.program_id(1)
        masks, lane = _head_masks()
        qpc = _Pieces(qp, qc, qn, d)
        dopc = _Pieces(dop, doc, don, d, lead=True)
        lpc = _Pieces(lp, lc, ln, d, lead=True)
        ddpc = _Pieces(_Lead4(ddp), _Lead4(ddc), _Lead4(ddn), d, lead=True)
        scale = DSWA_E ** -0.5
        for grp in _tile_groups(d, nsb):
            rows = [_tile_rows(r, sb, d) for r, sb in grp]
            kcs = [_bf(k_ref[rw, :]) for rw in rows]
            vcs = [_bf(v_ref[rw, :]) for rw in rows]
            qws = [qpc.window(r, sb) for r, sb in grp]
            dows = [dopc.window(r, sb) for r, sb in grp]
            lws = [lpc.window(r, sb) for r, sb in grp]
            ddws = [ddpc.window(r, sb) for r, sb in grp]
            qwbs = [_bf(x) for x in qws]
            dowbs = [_bf(x) for x in dows]
            valids = [_tile_valid(tb, r, sb, d, S, True) for r, sb in grp]
            both = [(t, hh) for t in range(len(grp)) for hh in range(2)]
            ss = [_dot_nt(_bf(jnp.where(masks[hh], qws[t], 0.0)), kcs[t]) * scale + b_ref[0, hh] for t, hh in both]
            dps = [_dot_nt(_bf(jnp.where(masks[hh], dows[t], 0.0)), vcs[t]) for t, hh in both]
            ps = [jnp.exp(jnp.where(valids[t], s - lws[t][:, hh:hh + 1], NEG_INF)) for (t, hh), s in zip(both, ss)]
            dvs = [_dot_tn(_bf(p), dowbs[t]) for (t, hh), p in zip(both, ps)]
            dss = [p * (dp - ddws[t][:, hh:hh + 1]) for (t, hh), p, dp in zip(both, ps, dps)]
            dks = [_dot_tn(_bf(ds), qwbs[t]) * scale for (t, hh), ds in zip(both, dss)]
            for t, rw in enumerate(rows):
                dk_ref[rw, :] = jnp.where(masks[0], dks[2 * t], dks[2 * t + 1])
                dv_ref[rw, :] = jnp.where(masks[0], dvs[2 * t], dvs[2 * t + 1])

    out = BS((ATT_TB, LANES), lambda p, tb: (tb, p))
    return pl.pallas_call(
        body, grid=(ATT_PAIRS, S // ATT_TB),
        in_specs=[k_spec, v_spec, *q_specs, *do_specs, *lse_specs, *dd_specs,
                  BS((1, 2, ATT_W, ATT_Q), lambda p, tb: (p, 0, 0, 0))],
        out_specs=[out, out],
        out_shape=[SDS((S, DSWA_HG * DSWA_E), F32), SDS((S, DSWA_HG * DSWA_E), F32)],
        name=name, compiler_params=_cp(("parallel", "parallel")))(
            qkv, qkv, qkv, qkv, qkv, do, do, do, lse, lse, lse, dd, dd, dd, bias_t)


def _pair_alphas(lses):
    m = jnp.maximum(jnp.maximum(lses[0], lses[1]), lses[2])
    e = [jnp.exp(t - m) for t in lses]
    tot = e[0] + e[1] + e[2]
    return [t / tot for t in e]


def _pair_expand(a, lane):
    return jnp.where(lane < DSWA_E, a[:, 0:1], a[:, 1:2])


def combine_fwd(o_raw, lse, name):
    S = o_raw.shape[0]
    tm = min(512, S)

    def body(o_ref, l_ref, y_ref):
        g = pl.program_id(2)
        lane = lax.broadcasted_iota(jnp.int32, (1, LANES), 1)
        alphas = _pair_alphas([l_ref[0, 0], l_ref[1, 0], l_ref[2, 0]])
        a = jnp.where(g == 0, alphas[0], jnp.where(g == 1, alphas[1], alphas[2]))
        y_ref[...] = (o_ref[...] * _pair_expand(a, lane)).astype(BF16)

    blk = BS((tm, LANES), lambda i, p, g: (i, g * ATT_PAIRS + p))
    return pl.pallas_call(
        body, grid=(S // tm, ATT_PAIRS, 3),
        in_specs=[blk, BS((3, 1, tm, LANES), lambda i, p, g: (0, p, i, 0))], out_specs=blk,
        out_shape=SDS((S, DSWA_W), BF16), name=name, compiler_params=_cp(("parallel", "parallel", "parallel")))(o_raw, lse)


def combine_bwd(o_raw, lse, dy, name):
    S = o_raw.shape[0]
    tm = min(512, S)

    def body(o0, o1, o2, l_ref, d0, d1, d2, do_ref, dd_ref):
        lane = lax.broadcasted_iota(jnp.int32, (1, LANES), 1)
        alphas = _pair_alphas([l_ref[0, 0], l_ref[1, 0], l_ref[2, 0]])
        c = jnp.zeros((tm, LANES), F32)
        for g, (o_ref, dy_ref) in enumerate(((o0, d0), (o1, d1), (o2, d2))):
            dyv = dy_ref[...]
            do_ref[g] = dyv * _pair_expand(alphas[g], lane)
            prod = o_ref[...] * dyv
            dal = (jnp.where(lane == 0, jnp.sum(jnp.where(lane < DSWA_E, prod, 0.0), axis=1, keepdims=True), 0.0)
                   + jnp.where(lane == 1, jnp.sum(jnp.where(lane >= DSWA_E, prod, 0.0), axis=1, keepdims=True), 0.0))
            c = c + alphas[g] * dal
        for g in range(3):
            dd_ref[g, 0] = alphas[g] * c

    def col(g):
        return BS((tm, LANES), lambda i, p: (i, g * ATT_PAIRS + p))

    return pl.pallas_call(
        body, grid=(S // tm, ATT_PAIRS),
        in_specs=[col(0), col(1), col(2), BS((3, 1, tm, LANES), lambda i, p: (0, p, i, 0)), col(0), col(1), col(2)],
        out_specs=[BS((3, tm, LANES), lambda i, p: (0, i, p)), BS((3, 1, tm, LANES), lambda i, p: (0, p, i, 0))],
        out_shape=[SDS((3, S, DSWA_HG * DSWA_E), F32), SDS((3, ATT_PAIRS, S, LANES), F32)],
        name=name, compiler_params=_cp(("parallel", "parallel")))(o_raw, o_raw, o_raw, lse, dy, dy, dy)


def dswa_fwd(x, g, w_in, w_out, rel_table, tag):
    h = rms_fwd(x, g, f"{tag}_rms")
    qkv = mm(h, w_in, name=f"{tag}_qkv", tn=1152)
    outs, lses = [], []
    for gi in range(3):
        bias, _ = _bias_mats(rel_table, gi)
        o, lse = attn_fwd(qkv, bias, gi, f"{tag}_att{gi}")
        outs.append(o)
        lses.append(lse)
    o_raw = jnp.concatenate(outs, axis=1)
    lse = jnp.stack(lses)
    y = combine_fwd(o_raw, lse, f"{tag}_comb")
    xn = mm(y, w_out, name=f"{tag}_out", epi=_add, extras=(x,))
    return xn, (h, qkv, o_raw, lse, y)


def dswa_bwd(x, g, w_in, w_out, rel_table, saved, dx, dxb, tag):
    h, qkv, o_raw, lse, y = saved
    dw_out = mm(y, dxb, name=f"{tag}_dwout", ta=True, tm=384)
    dy = mm(dxb, w_out, name=f"{tag}_dy", tb=True, tn=384)
    do_raw, dd = combine_bwd(o_raw, lse, dy, f"{tag}_combb")
    dqs, dks, dvs = [], [], []
    drel = jnp.zeros_like(rel_table)
    for gi in range(3):
        (bias, bias_t), bias_vjp = jax.vjp(lambda tbl: _bias_mats(tbl, gi), rel_table)
        dq, dbias = attn_bwd_q(qkv, bias, lse[gi], do_raw, dd, gi, f"{tag}_attq{gi}")
        dk, dv = attn_bwd_kv(qkv, bias_t, lse[gi], do_raw, dd, gi, f"{tag}_attkv{gi}")
        drel = drel + bias_vjp((dbias, jnp.zeros_like(bias_t)))[0]
        dqs.append(dq)
        dks.append(dk)
        dvs.append(dv)
    dqkv = jnp.concatenate(dqs + dks + dvs, axis=1).astype(BF16)
    dw_in = mm(h, dqkv, name=f"{tag}_dwin", ta=True, tn=384)
    dh = mm(dqkv, w_in, name=f"{tag}_dh", tb=True, tk=1152)
    dx, dxb, dg = rms_bwd(x, g, dh, dx, f"{tag}_rmsb")
    return dx, dxb, dg, dw_in, dw_out, drel


def adamw(w, g, m, v, name):
    shape = w.shape
    last = shape[-1]
    w2, g2, m2, v2 = (t.reshape(-1, last) for t in (w, g, m, v))
    rows = w2.shape[0]
    tr = rows
    if rows > 512:
        tr = next(t for t in (512, 256, 192, 128, 64, 8) if rows % t == 0)
    c1 = 1.0 / (1.0 - ADAM_B1 ** ADAM_STEP)
    c2 = 1.0 / (1.0 - ADAM_B2 ** ADAM_STEP)

    def body(w_ref, g_ref, m_ref, v_ref, d_ref, nm_ref, nv_ref):
        gv = g_ref[...]
        nm = ADAM_B1 * m_ref[...] + (1.0 - ADAM_B1) * gv
        nv = ADAM_B2 * v_ref[...] + (1.0 - ADAM_B2) * (gv * gv)
        nm_ref[...] = nm
        nv_ref[...] = nv
        d_ref[...] = -ADAM_LR * ((nm * c1) / (jnp.sqrt(nv * c2) + ADAM_EPS) + ADAM_WD * w_ref[...])

    spec = BS((tr, last), lambda i: (i, 0))
    outs = pl.pallas_call(
        body, grid=(rows // tr,), in_specs=[spec] * 4, out_specs=[spec] * 3,
        out_shape=[SDS((rows, last), F32)] * 3, name=name, compiler_params=_cp(("parallel",)))(w2, g2, m2, v2)
    return tuple(o.reshape(shape) for o in outs)


def _place():
    x, y, c = lax.axis_index("x"), lax.axis_index("y"), lax.axis_index("c")
    chips = [(1 - x, y), (x, 1 - y), (1 - x, 1 - y)]
    return x, y, c, chips


def _rcopy(src, dst, ssem, rsem, dev):
    return pltpu.make_async_remote_copy(src_ref=src, dst_ref=dst, send_sem=ssem, recv_sem=rsem, device_id=dev,
                                        device_id_type=MESH)


class SideJob(NamedTuple):
    ins: list
    outs: list
    sems: list
    start: Callable
    wait: Callable


def _job(ins, outs, sems, copies):
    def start(in_refs, out_refs, sem_refs):
        for cp in copies(in_refs, out_refs, sem_refs):
            cp.start()

    def wait(in_refs, out_refs, sem_refs):
        for cp in copies(in_refs, out_refs, sem_refs):
            cp.wait()

    return SideJob(list(ins), list(outs), list(sems), start, wait)


def gather_job(packs, halved):
    n = len(packs)
    dma = pltpu.SemaphoreType.DMA

    def copies(in_refs, out_refs, sems):
        ssem, rsem = sems
        x, y, c, chips = _place()
        jme = 2 * x + y
        cps = []
        for i, (p_ref, f_ref) in enumerate(zip(in_refs, out_refs)):
            rows = p_ref.shape[0]
            mine = pl.ds(c * (rows // 2), rows // 2) if halved[i] else pl.ds(0, rows)
            for r, (cx, cy) in enumerate(chips):
                cps.append(_rcopy(p_ref.at[mine], f_ref.at[jme, mine], ssem.at[i, r], rsem.at[i, r], (cx, cy, c)))
        return cps

    return _job(packs, [SDS((4,) + p.shape, p.dtype) for p in packs], [dma((n, 3)), dma((n, 3))], copies)


def chip_exchange_job(parts):
    n = len(parts)
    dma = pltpu.SemaphoreType.DMA

    def copies(in_refs, out_refs, sems):
        ssem, rsem = sems
        x, y, c, chips = _place()
        cps = []
        for i, (p_ref, r_ref) in enumerate(zip(in_refs, out_refs)):
            for r, (cx, cy) in enumerate(chips):
                cps.append(_rcopy(p_ref.at[2 * cx + cy], r_ref.at[r], ssem.at[i, r], rsem.at[i, r], (cx, cy, c)))
        return cps

    return _job(parts, [SDS((3,) + p.shape[1:], p.dtype) for p in parts], [dma((n, 3)), dma((n, 3))], copies)


def run_job(job, name):
    ni, no = len(job.ins), len(job.outs)

    def body(*refs):
        job.start(refs[:ni], refs[ni:ni + no], refs[ni + no:])
        job.wait(refs[:ni], refs[ni:ni + no], refs[ni + no:])

    return pl.pallas_call(
        body, in_specs=[ANY] * ni, out_specs=[ANY] * no, out_shape=job.outs, scratch_shapes=job.sems, name=name,
        compiler_params=pltpu.CompilerParams(has_side_effects=True))(*job.ins)


def forward_to_sibling(fulls, name):
    n = len(fulls)

    def body(*refs):
        in_refs, out_refs, (ssem, rsem) = refs[:n], refs[n:2 * n], refs[2 * n:]
        x, y, c, chips = _place()
        cps = []
        for i in range(n):
            half = in_refs[i].shape[1] // 2
            for r, (cx, cy) in enumerate(chips):
                piece = (2 * cx + cy, pl.ds(c * half, half))
                cps.append(_rcopy(in_refs[i].at[piece], out_refs[i].at[piece], ssem.at[i, r], rsem.at[i, r], (x, y, 1 - c)))
        for cp in cps:
            cp.start()
        for cp in cps:
            cp.wait()

    dma = pltpu.SemaphoreType.DMA
    return pl.pallas_call(
        body, in_specs=[ANY] * n, out_specs=[ANY] * n, out_shape=[SDS(f.shape, f.dtype) for f in fulls],
        scratch_shapes=[dma((n, 3)), dma((n, 3))], input_output_aliases={i: i for i in range(n)}, name=name,
        compiler_params=pltpu.CompilerParams(has_side_effects=True))(*fulls)


def rs_sibling_exchange(gpack, name):
    _, rows, W = gpack.shape
    half = rows // 2

    def body(g_ref, r_ref, ssem, rsem):
        x, y, c, _ = _place()
        cps = [_rcopy(g_ref.at[j, pl.ds((1 - c) * half, half)], r_ref.at[j], ssem.at[j], rsem.at[j], (x, y, 1 - c))
               for j in range(4)]
        for cp in cps:
            cp.start()
        for cp in cps:
            cp.wait()

    dma = pltpu.SemaphoreType.DMA
    return pl.pallas_call(
        body, in_specs=[ANY], out_specs=ANY, out_shape=SDS((4, half, W), gpack.dtype),
        scratch_shapes=[dma((4,)), dma((4,))], name=name,
        compiler_params=pltpu.CompilerParams(has_side_effects=True))(gpack)


def _div_tile(n, limit):
    return next(t for t in range(limit - limit % 8, 0, -8) if n % t == 0)


def rs_add_sibling(gpack, recv, cidx, name):
    _, rows, W = gpack.shape
    half = rows // 2
    tr = _div_tile(half, 1024)
    nb = half // tr

    def body(c_ref, g_ref, r_ref, o_ref):
        o_ref[...] = g_ref[...] + r_ref[...]

    gs = pltpu.PrefetchScalarGridSpec(
        num_scalar_prefetch=1, grid=(4, nb),
        in_specs=[BS((1, tr, W), lambda j, i, c: (j, c[0] * nb + i, 0)), BS((1, tr, W), lambda j, i, c: (j, i, 0))],
        out_specs=BS((1, tr, W), lambda j, i, c: (j, i, 0)))
    return pl.pallas_call(body, grid_spec=gs, out_shape=SDS((4, half, W), F32), name=name,
                          compiler_params=_cp(("parallel", "parallel")))(cidx, gpack, recv)


def rs_add_chips(recv, part, place, name):
    _, half, W = recv.shape
    tr = _div_tile(half, 640)
    nb = half // tr

    def body(x_ref, y_ref, c_ref, r_ref, own_ref, o_ref):
        o_ref[...] = ((r_ref[0] + r_ref[1]) + r_ref[2]) + own_ref[0]

    gs = pltpu.PrefetchScalarGridSpec(
        num_scalar_prefetch=3, grid=(nb,),
        in_specs=[BS((3, tr, W), lambda i, x, y, c: (0, i, 0)), BS((1, tr, W), lambda i, x, y, c: (2 * x[0] + y[0], i, 0))],
        out_specs=BS((tr, W), lambda i, x, y, c: (c[0] * nb + i, 0)))
    return pl.pallas_call(body, grid_spec=gs, out_shape=SDS((2 * half, W), F32), name=name,
                          compiler_params=_cp(("parallel",)))(*place, recv, part)


def rs_sibling_share(gsh, name):
    rows, W = gsh.shape
    half = rows // 2

    def body(g_ref, o_ref, ssem, rsem):
        x, y, c, _ = _place()
        mine = pl.ds(c * half, half)
        cp = _rcopy(g_ref.at[mine], o_ref.at[mine], ssem, rsem, (x, y, 1 - c))
        cp.start()
        cp.wait()

    dma = pltpu.SemaphoreType.DMA
    return pl.pallas_call(
        body, in_specs=[ANY], out_specs=ANY, out_shape=SDS(gsh.shape, gsh.dtype),
        scratch_shapes=[dma, dma], input_output_aliases={0: 0}, name=name,
        compiler_params=pltpu.CompilerParams(has_side_effects=True))(gsh)


def allreduce_small(pack):
    R = pack.shape[0]

    def body(p_ref, o_ref, all_ref, ssem, rsem):
        x, y, c, _ = _place()
        me = 4 * x + 2 * y + c
        all_ref[me] = p_ref[...]
        cps = []
        for m in range(1, 8):
            peer = (1 - x if m & 4 else x, 1 - y if m & 2 else y, 1 - c if m & 1 else c)
            cp = _rcopy(p_ref, all_ref.at[me], ssem.at[m - 1], rsem.at[m - 1], peer)
            cp.start()
            cps.append(cp)
        for cp in cps:
            cp.wait()
        acc = all_ref[0]
        for i in range(1, 8):
            acc = acc + all_ref[i]
        o_ref[...] = acc

    dma = pltpu.SemaphoreType.DMA
    vm = BS(memory_space=pltpu.VMEM)
    return pl.pallas_call(
        body, in_specs=[vm], out_specs=vm, out_shape=SDS(pack.shape, F32),
        scratch_shapes=[pltpu.VMEM((8, R, LANES), F32), dma((7,)), dma((7,))], name="allreduce_small",
        compiler_params=pltpu.CompilerParams(has_side_effects=True))(pack)


PACK_W = 1024
PACK_ALIGN = 32


def _layer_entries(l):
    if l % 2 == 0:
        mixer = [("gdn_w_in", l // 2, D_MODEL, GDN_IN // 4, True), ("gdn_w_out", l // 2, D_MODEL // 4, D_MODEL, False)]
    else:
        mixer = [("dswa_w_in", l // 2, D_MODEL, 3 * DSWA_W // 4, True), ("dswa_w_out", l // 2, DSWA_W // 4, D_MODEL, False)]
    return mixer + [("mlp_w1", l, D_MODEL, D_FF // 4, True), ("mlp_w2", l, D_FF // 4, D_MODEL, False)]


def _layer_offsets(l):
    offs = [int(o) for o in np.cumsum([0] + [r * c // PACK_W for (_, _, r, c, _) in _layer_entries(l)])]
    return offs, -(-offs[-1] // PACK_ALIGN) * PACK_ALIGN


def _pack_layer(l, shards, dtype):
    offs, total = _layer_offsets(l)
    parts = [shards[name][li].astype(dtype).reshape(-1, PACK_W) for (name, li, _, _, _) in _layer_entries(l)]
    parts.append(jnp.zeros((total - offs[-1], PACK_W), dtype))
    return jnp.concatenate(parts, axis=0)


def _unpack_layer(l, full, own, jme):
    offs, _ = _layer_offsets(l)
    mats = []
    for e, (_, _, r, c, by_col) in enumerate(_layer_entries(l)):
        mine = own[offs[e]:offs[e + 1]]
        sh = [jnp.where(jme == j, mine, full[j, offs[e]:offs[e + 1]]).reshape(r, c) for j in range(4)]
        mats.append(jnp.concatenate(sh, axis=1 if by_col else 0))
    return mats


def _pack_layer_grads(l, grads):
    offs, total = _layer_offsets(l)
    per_chip = []
    for j in range(4):
        parts = []
        for g, (_, _, r, c, by_col) in zip(grads, _layer_entries(l)):
            sh = g[:, c * j:c * (j + 1)] if by_col else g[r * j:r * (j + 1), :]
            parts.append(sh.reshape(-1, PACK_W))
        parts.append(jnp.zeros((total - offs[-1], PACK_W), F32))
        per_chip.append(jnp.concatenate(parts, axis=0))
    return jnp.stack(per_chip)


def _unpack_shard_grads(gshs):
    out = {}
    for l, gsh in enumerate(gshs):
        offs, _ = _layer_offsets(l)
        for e, (name, _, r, c, _) in enumerate(_layer_entries(l)):
            out.setdefault(name, []).append(gsh[offs[e]:offs[e + 1]].reshape(r, c))
    return {k: jnp.stack(v) for k, v in out.items()}


def _flat_pad(t, mult=8 * LANES):
    f = t.reshape(-1)
    return jnp.pad(f, (0, (-f.shape[0]) % mult))


def kernel(x, norm_mix, norm_mlp, norm_final, rel_bias, gdn_w_in, gdn_conv_w, gdn_a_log, gdn_dt_bias, gdn_norm_w, gdn_w_out, dswa_w_in, dswa_w_out, mlp_w1, mlp_w2, loss_target, m_norm_mix, m_norm_mlp, m_norm_final, m_rel_bias, m_gdn_w_in, m_gdn_conv_w, m_gdn_a_log, m_gdn_dt_bias, m_gdn_norm_w, m_gdn_w_out, m_dswa_w_in, m_dswa_w_out, m_mlp_w1, m_mlp_w2, v_norm_mix, v_norm_mlp, v_norm_final, v_rel_bias, v_gdn_w_in, v_gdn_conv_w, v_gdn_a_log, v_gdn_dt_bias, v_gdn_norm_w, v_gdn_w_out, v_dswa_w_in, v_dswa_w_out, v_mlp_w1, v_mlp_w2):
    xi, yi, ci = lax.axis_index("x"), lax.axis_index("y"), lax.axis_index("c")
    jme = 2 * xi + yi
    big = dict(gdn_w_in=gdn_w_in, gdn_w_out=gdn_w_out, dswa_w_in=dswa_w_in, dswa_w_out=dswa_w_out, mlp_w1=mlp_w1, mlp_w2=mlp_w2)
    n_gdn = gdn_w_in.shape[0]
    conv_cols = gdn_conv_w.shape[-1]

    packs = [_pack_layer(l, big, BF16) for l in range(DEPTH)]
    convp = jnp.pad(gdn_conv_w.reshape(n_gdn * GDN_CONV, conv_cols), ((0, 16 - n_gdn * GDN_CONV), (0, 0)))
    raw0, cfull = run_job(gather_job([packs[0], convp], [True, False]), "gather_l0")
    fulls = {0: forward_to_sibling([raw0], "forward_l0")[0]}
    cfull = jnp.where((jnp.arange(4) == jme)[:, None, None], convp[None], cfull)
    conv_all = jnp.transpose(cfull[:, :n_gdn * GDN_CONV], (1, 0, 2)).reshape(n_gdn, GDN_CONV, 4 * conv_cols)
    conv_all = jnp.pad(conv_all, ((0, 0), (0, 8 - GDN_CONV), (0, 0)))
    fwd_jobs = {0: [1, 2], 2: [3]}

    xs = x[0]
    saved = []
    for l in range(DEPTH):
        w_in, w_out, w1, w2 = _unpack_layer(l, fulls[l], packs[l], jme)
        gm, gp = norm_mix[l][None], norm_mlp[l][None]
        a = l // 2
        if l % 2 == 0:
            w_in = jnp.pad(w_in, ((0, 0), (0, GDN_INP - GDN_IN)))
            job = gather_job([packs[t] for t in fwd_jobs[l]], [True] * len(fwd_jobs[l]))
            x_mid, sv, raws = gdn_fwd(xs, gm, w_in, conv_all[a], gdn_a_log[a], gdn_dt_bias[a], gdn_norm_w[a][None], w_out,
                                      f"l{l}_gdn", job)
            for t, f in zip(fwd_jobs[l], forward_to_sibling(raws, f"forward_from_l{l}")):
                fulls[t] = f
        else:
            x_mid, sv = dswa_fwd(xs, gm, w_in, w_out, rel_bias, f"l{l}_att")
        x_out, sv2 = mlp_fwd(x_mid, gp, w1, w2, f"l{l}_mlp")
        saved.append((xs, x_mid, (w_in, w_out, w1, w2), sv, sv2))
        xs = x_out

    cidx = ci.astype(jnp.int32).reshape(1)
    place = [t.astype(jnp.int32).reshape(1) for t in (xi, yi, ci)]

    def chip_partial(l, grads4):
        gpack = _pack_layer_grads(l, grads4)
        return rs_add_sibling(gpack, rs_sibling_exchange(gpack, f"rs_sibling_l{l}"), cidx, f"rs_add_sibling_l{l}")

    def finish(l, recv):
        return rs_sibling_share(rs_add_chips(recv, parts[l], place, f"rs_add_chips_l{l}"), f"rs_share_l{l}")

    loss_part, dx, dxb, d_final = loss_head(xs, norm_final[None], loss_target[0], "loss_head")
    d_mix, d_mlp = [None] * DEPTH, [None] * DEPTH
    d_conv, d_alog, d_dt, d_nw = [None] * n_gdn, [None] * n_gdn, [None] * n_gdn, [None] * n_gdn
    d_rel = jnp.zeros_like(rel_bias)
    parts, gshs = {}, [None] * DEPTH
    bwd_jobs = {2: [3], 0: [2, 1]}
    for l in reversed(range(DEPTH)):
        x_in, x_mid, (w_in, w_out, w1, w2), sv, sv2 = saved[l]
        gm, gp = norm_mix[l][None], norm_mlp[l][None]
        a = l // 2
        dx, dxb, d_mlp[l], dw1, dw2 = mlp_bwd(x_mid, gp, w1, w2, sv2, dx, dxb, f"l{l}_mlp")
        if l % 2 == 0:
            job = chip_exchange_job([parts[t] for t in bwd_jobs[l]])
            dx, dxb, d_mix[l], dw_all, d_conv[a], d_alog[a], d_dt[a], d_nw[a], dwo, recvs = gdn_bwd(
                x_in, gm, w_in, conv_all[a], gdn_norm_w[a][None], w_out, sv, dx, dxb, f"l{l}_gdn", job)
            for t, rv in zip(bwd_jobs[l], recvs):
                gshs[t] = finish(t, rv)
            dwi = dw_all[:, :GDN_IN]
        else:
            dx, dxb, d_mix[l], dwi, dwo, drel = dswa_bwd(x_in, gm, w_in, w_out, rel_bias, sv, dx, dxb, f"l{l}_att")
            d_rel = d_rel + drel
        parts[l] = chip_partial(l, [dwi, dwo, dw1, dw2])
    gshs[0] = finish(0, run_job(chip_exchange_job([parts[0]]), "rs_chip_exchange_l0")[0])
    gbig = _unpack_shard_grads(gshs)

    small = [jnp.concatenate(d_mix, axis=0), jnp.concatenate(d_mlp, axis=0), d_final, d_rel,
             jnp.stack(d_conv), jnp.stack(d_alog), jnp.stack(d_dt), jnp.concatenate(d_nw, axis=0)]
    flat = [_flat_pad(t) for t in small]
    sizes = [f.shape[0] for f in flat]
    red = allreduce_small(jnp.concatenate(flat).reshape(-1, LANES)).reshape(-1)
    offs = np.cumsum([0] + sizes)
    red = [red[offs[i]:offs[i] + small[i].size].reshape(small[i].shape) for i in range(len(small))]
    g_conv_all = red[4][:, :GDN_CONV].reshape(n_gdn, GDN_CONV, 1, 4 * conv_cols)
    g_conv = lax.dynamic_slice_in_dim(g_conv_all, jme * conv_cols, conv_cols, axis=3)
    g = dict(norm_mix=red[0], norm_mlp=red[1], norm_final=red[2].reshape(norm_final.shape), rel_bias=red[3],
             gdn_conv_w=g_conv, gdn_a_log=red[5], gdn_dt_bias=red[6], gdn_norm_w=red[7][:, :GDN_DK], **gbig)

    w = dict(norm_mix=norm_mix, norm_mlp=norm_mlp, norm_final=norm_final, rel_bias=rel_bias, gdn_conv_w=gdn_conv_w,
             gdn_a_log=gdn_a_log, gdn_dt_bias=gdn_dt_bias, gdn_norm_w=gdn_norm_w, **big)
    m = dict(norm_mix=m_norm_mix, norm_mlp=m_norm_mlp, norm_final=m_norm_final, rel_bias=m_rel_bias, gdn_w_in=m_gdn_w_in,
             gdn_conv_w=m_gdn_conv_w, gdn_a_log=m_gdn_a_log, gdn_dt_bias=m_gdn_dt_bias, gdn_norm_w=m_gdn_norm_w,
             gdn_w_out=m_gdn_w_out, dswa_w_in=m_dswa_w_in, dswa_w_out=m_dswa_w_out, mlp_w1=m_mlp_w1, mlp_w2=m_mlp_w2)
    v = dict(norm_mix=v_norm_mix, norm_mlp=v_norm_mlp, norm_final=v_norm_final, rel_bias=v_rel_bias, gdn_w_in=v_gdn_w_in,
             gdn_conv_w=v_gdn_conv_w, gdn_a_log=v_gdn_a_log, gdn_dt_bias=v_gdn_dt_bias, gdn_norm_w=v_gdn_norm_w,
             gdn_w_out=v_gdn_w_out, dswa_w_in=v_dswa_w_in, dswa_w_out=v_dswa_w_out, mlp_w1=v_mlp_w1, mlp_w2=v_mlp_w2)
    names = ["norm_mix", "norm_mlp", "norm_final", "rel_bias", "gdn_w_in", "gdn_conv_w", "gdn_a_log", "gdn_dt_bias",
             "gdn_norm_w", "gdn_w_out", "dswa_w_in", "dswa_w_out", "mlp_w1", "mlp_w2"]
    upd = {n: adamw(w[n], g[n], m[n], v[n], f"adamw_{n}") for n in names}
    loss = lax.psum(loss_part[0, 0], ("x", "y", "c"))
    return (loss, dx[None], *[g[n] for n in names], *[upd[n][0] for n in names], *[upd[n][1] for n in names],
            *[upd[n][2] for n in names])
```

```python
import math
from typing import Callable, NamedTuple

import numpy as np
import jax
import jax.numpy as jnp
from jax import lax
from jax.experimental import pallas as pl
from jax.experimental.pallas import tpu as pltpu

F32 = jnp.float32
BF16 = jnp.bfloat16
HI = lax.Precision.HIGHEST
BS = pl.BlockSpec
SDS = jax.ShapeDtypeStruct
MESH = pl.DeviceIdType.MESH
ANY = BS(memory_space=pl.ANY)

D_MODEL = 1024
D_FF = 4096
DEPTH = 4
RMS_EPS = 1e-6
NEG_INF = -1e30
LANES = 128
VMEM_LIMIT = 56 << 20

GDN_H = 8
GDN_DK = 128
GDN_CONV = 5
GDN_C = 64
GDN_GC = 8
GDN_HP = 4
GDN_QKV = 3 * GDN_H * GDN_DK
GDN_IN = GDN_QKV + GDN_H * GDN_DK + 4 * GDN_H
GDN_INP = 4224

DSWA_CFG = ((128, 1), (512, 4), (2048, 16))
DSWA_HG = 6
DSWA_E = 64
DSWA_HEADS = 18
DSWA_W = DSWA_HEADS * DSWA_E
DSWA_HALF = 64
REL_BUCKETS = 32
REL_MAX_DIST = 1024

ADAM_LR = 0.001
ADAM_B1 = 0.9
ADAM_B2 = 0.999
ADAM_EPS = 1e-08
ADAM_WD = 0.01
ADAM_STEP = 10


def _cp(sem=None):
    return pltpu.CompilerParams(dimension_semantics=sem, vmem_limit_bytes=VMEM_LIMIT)


def _dot(a, b, prec=None):
    return jnp.dot(a, b, precision=prec, preferred_element_type=F32)


def _dot_nt(a, b, prec=None):
    return lax.dot_general(a, b, (((1,), (1,)), ((), ())), precision=prec, preferred_element_type=F32)


def _dot_tn(a, b, prec=None):
    return lax.dot_general(a, b, (((0,), (0,)), ((), ())), precision=prec, preferred_element_type=F32)


def _bf(a):
    return a.astype(BF16)


def _sigmoid(x):
    return 1.0 / (1.0 + jnp.exp(-x))


def rms_fwd(x, g, name):
    S, Dm = x.shape
    tm = min(512, S)

    def body(x_ref, g_ref, o_ref):
        xv = x_ref[...]
        r = lax.rsqrt(jnp.mean(xv * xv, axis=-1, keepdims=True) + RMS_EPS)
        o_ref[...] = (xv * r * g_ref[...]).astype(o_ref.dtype)

    return pl.pallas_call(
        body, grid=(S // tm,),
        in_specs=[BS((tm, Dm), lambda i: (i, 0)), BS((1, Dm), lambda i: (0, 0))],
        out_specs=BS((tm, Dm), lambda i: (i, 0)),
        out_shape=SDS((S, Dm), BF16), name=name, compiler_params=_cp(("parallel",)))(x, g)


def rms_bwd(x, g, dh, dres, name):
    S, Dm = x.shape
    tm = min(512, S)

    def body(x_ref, g_ref, dh_ref, dres_ref, dx_ref, dxb_ref, dg_ref):
        i = pl.program_id(0)
        xv = x_ref[...]
        r = lax.rsqrt(jnp.mean(xv * xv, axis=-1, keepdims=True) + RMS_EPS)
        n = xv * r
        dhv = dh_ref[...]
        t = dhv * g_ref[...]
        dx = dres_ref[...] + r * (t - n * jnp.mean(n * t, axis=-1, keepdims=True))
        dx_ref[...] = dx
        dxb_ref[...] = dx.astype(BF16)
        part = jnp.sum(dhv * n, axis=0, keepdims=True)

        @pl.when(i == 0)
        def _():
            dg_ref[...] = part

        @pl.when(i > 0)
        def _():
            dg_ref[...] += part

    row = BS((tm, Dm), lambda i: (i, 0))
    vec = BS((1, Dm), lambda i: (0, 0))
    return pl.pallas_call(
        body, grid=(S // tm,), in_specs=[row, vec, row, row], out_specs=[row, row, vec],
        out_shape=[SDS((S, Dm), F32), SDS((S, Dm), BF16), SDS((1, Dm), F32)],
        name=name, compiler_params=_cp(("arbitrary",)))(x, g, dh, dres)


def loss_head(x, g, tgt, name):
    S, Dm = x.shape
    tm = min(512, S)

    def body(x_ref, g_ref, t_ref, loss_ref, dx_ref, dxb_ref, dg_ref):
        i = pl.program_id(0)
        xv = x_ref[...]
        gv = g_ref[...]
        r = lax.rsqrt(jnp.mean(xv * xv, axis=-1, keepdims=True) + RMS_EPS)
        n = xv * r
        err = n * gv - t_ref[...]
        lpart = 0.5 * jnp.sum(jnp.mean(err * err, axis=-1, keepdims=True), axis=0, keepdims=True)
        dout = err * (1.0 / Dm)
        t = dout * gv
        dx = r * (t - n * jnp.mean(n * t, axis=-1, keepdims=True))
        dx_ref[...] = dx
        dxb_ref[...] = dx.astype(BF16)
        part = jnp.sum(dout * n, axis=0, keepdims=True)

        @pl.when(i == 0)
        def _():
            dg_ref[...] = part
            loss_ref[...] = lpart

        @pl.when(i > 0)
        def _():
            dg_ref[...] += part
            loss_ref[...] += lpart

    row = BS((tm, Dm), lambda i: (i, 0))
    vec = BS((1, Dm), lambda i: (0, 0))
    one = BS((1, 1), lambda i: (0, 0))
    return pl.pallas_call(
        body, grid=(S // tm,), in_specs=[row, vec, row], out_specs=[one, row, row, vec],
        out_shape=[SDS((1, 1), F32), SDS((S, Dm), F32), SDS((S, Dm), BF16), SDS((1, Dm), F32)],
        name=name, compiler_params=_cp(("arbitrary",)))(x, g, tgt)


MM_TK = 2048


def mm(a, b, *, name, ta=False, tb=False, tm=1024, tn=512, tk=None, out_dtype=F32, pre_a=None, epi=None,
       extras=()):
    M, K = (a.shape[1], a.shape[0]) if ta else a.shape
    N = b.shape[0] if tb else b.shape[1]
    tm, tn = min(tm, M), min(tn, N)
    if tk is None:
        tk = MM_TK if K % MM_TK == 0 else K
    tk = min(tk, K)
    assert M % tm == 0 and N % tn == 0 and K % tk == 0, (name, M, N, K, tm, tn, tk)
    nk = K // tk
    ne = len(extras)
    a_spec = BS((tk, tm), lambda i, j, k: (k, i)) if ta else BS((tm, tk), lambda i, j, k: (i, k))
    b_spec = BS((tn, tk), lambda i, j, k: (j, k)) if tb else BS((tk, tn), lambda i, j, k: (k, j))
    o_spec = BS((tm, tn), lambda i, j, k: (i, j))
    dims = (((0 if ta else 1,), (1 if tb else 0,)), ((), ()))

    def body(a_ref, b_ref, *rest):
        e_refs, o_ref = rest[:ne], rest[ne]
        av = a_ref[...]
        if pre_a is not None:
            av = pre_a(av)
        p = lax.dot_general(_bf(av), _bf(b_ref[...]), dims, preferred_element_type=F32)

        def finish(acc):
            res = epi(acc, *[e[...] for e in e_refs]) if epi is not None else acc
            o_ref[...] = res.astype(o_ref.dtype)

        if nk == 1:
            finish(p)
        else:
            acc_ref = rest[ne + 1]
            k = pl.program_id(2)

            @pl.when(k == 0)
            def _():
                acc_ref[...] = p

            @pl.when(k > 0)
            def _():
                acc_ref[...] += p

            @pl.when(k == nk - 1)
            def _():
                finish(acc_ref[...])

    return pl.pallas_call(
        body, grid=(M // tm, N // tn, nk), in_specs=[a_spec, b_spec] + [o_spec] * ne, out_specs=o_spec,
        out_shape=SDS((M, N), out_dtype),
        scratch_shapes=[pltpu.VMEM((tm, tn), F32)] if nk > 1 else [],
        name=name, compiler_params=_cp(("parallel", "parallel", "arbitrary")))(a, b, *extras)


def _relu(acc):
    return jnp.maximum(acc, 0.0)


def _add(acc, res):
    return acc + res


def _sq(av):
    return av * av


def _times_2r(acc, r):
    return acc * (2.0 * r.astype(F32))


def mlp_fwd(x, g, w1, w2, tag):
    h = rms_fwd(x, g, f"{tag}_rms")
    r = mm(h, w1, name=f"{tag}_up", tn=1024, out_dtype=BF16, epi=_relu)
    xn = mm(r, w2, name=f"{tag}_down", pre_a=_sq, epi=_add, extras=(x,))
    return xn, (h, r)


def mlp_bwd(x, g, w1, w2, saved, dx, dxb, tag):
    h, r = saved
    da = mm(dxb, w2, name=f"{tag}_dact", tb=True, tn=1024, out_dtype=BF16, epi=_times_2r, extras=(r,))
    dw2 = mm(r, dxb, name=f"{tag}_dw2", ta=True, pre_a=_sq)
    dw1 = mm(h, da, name=f"{tag}_dw1", ta=True)
    dh = mm(da, w1, name=f"{tag}_dh", tb=True)
    dx, dxb, dg = rms_bwd(x, g, dh, dx, f"{tag}_rmsb")
    return dx, dxb, dg, dw1, dw2


def _conv_taps(x, S):
    t = lax.broadcasted_iota(jnp.int32, x.shape, 0)
    taps = []
    for j in range(GDN_CONV):
        sh = j - GDN_CONV // 2
        xs = x if sh == 0 else pltpu.roll(x, (-sh) % S, 0)
        taps.append(jnp.where((t + sh >= 0) & (t + sh < S), xs, 0.0))
    return taps


def _qkv_scale(c):
    is_norm = c < 2 * GDN_H
    scale = jnp.where(c < GDN_H, GDN_DK ** -0.5, 1.0)
    return is_norm, scale


def gdn_pre_fwd(proj, convw, name):
    S = proj.shape[0]

    def body(p_ref, w_ref, o_ref):
        c = pl.program_id(0)
        x = p_ref[...]
        w = w_ref[...]
        y = jnp.zeros_like(x)
        for j, xs in enumerate(_conv_taps(x, S)):
            y = y + w[j:j + 1, :] * xs
        t = y * _sigmoid(y)
        is_norm, scale = _qkv_scale(c)
        r = lax.rsqrt(jnp.sum(t * t, axis=-1, keepdims=True) + 1e-6)
        o_ref[...] = jnp.where(is_norm, t * r * scale, t)

    return pl.pallas_call(
        body, grid=(GDN_QKV // LANES,),
        in_specs=[BS((S, LANES), lambda c: (0, c)), BS((8, LANES), lambda c: (0, c))],
        out_specs=BS((S, LANES), lambda c: (0, c)),
        out_shape=SDS((S, GDN_QKV), F32), name=name, compiler_params=_cp(("parallel",)))(proj, convw)


def gdn_pre_bwd(proj, convw, dqkv, name):
    S = proj.shape[0]

    def body(p_ref, w_ref, d_ref, dp_ref, dw_ref):
        c = pl.program_id(0)
        x = p_ref[...]
        w = w_ref[...]
        taps = _conv_taps(x, S)
        y = jnp.zeros_like(x)
        for j, xs in enumerate(taps):
            y = y + w[j:j + 1, :] * xs
        sg = _sigmoid(y)
        t = y * sg
        is_norm, scale = _qkv_scale(c)
        dout = d_ref[0, 0] + d_ref[1, 0]
        r = lax.rsqrt(jnp.sum(t * t, axis=-1, keepdims=True) + 1e-6)
        n = t * r
        dn = dout * scale
        dt_norm = r * (dn - n * jnp.sum(dn * n, axis=-1, keepdims=True))
        dt = jnp.where(is_norm, dt_norm, dout)
        dy = dt * (sg * (1.0 + y * (1.0 - sg)))
        row = lax.broadcasted_iota(jnp.int32, (8, LANES), 0)
        dw = jnp.zeros((8, LANES), F32)
        for j, xs in enumerate(taps):
            dw = dw + jnp.where(row == j, jnp.sum(dy * xs, axis=0, keepdims=True), 0.0)
        dw_ref[...] = dw
        tt = lax.broadcasted_iota(jnp.int32, x.shape, 0)
        dx = jnp.zeros_like(x)
        for j in range(GDN_CONV):
            sh = j - GDN_CONV // 2
            ds = dy if sh == 0 else pltpu.roll(dy, sh % S, 0)
            dx = dx + w[j:j + 1, :] * jnp.where((tt - sh >= 0) & (tt - sh < S), ds, 0.0)
        dp_ref[...] = dx.astype(BF16)

    return pl.pallas_call(
        body, grid=(GDN_QKV // LANES,),
        in_specs=[BS((S, LANES), lambda c: (0, c)), BS((8, LANES), lambda c: (0, c)),
                  BS((2, 1, S, LANES), lambda c: (0, c // GDN_H, 0, c % GDN_H))],
        out_specs=[BS((S, LANES), lambda c: (0, c)), BS((8, LANES), lambda c: (0, c))],
        out_shape=[SDS((S, GDN_QKV), BF16), SDS((8, GDN_QKV), F32)],
        name=name, compiler_params=_cp(("parallel",)))(proj, convw, dqkv)


def _chunk_sum_matrix(n, upper):
    i = lax.broadcasted_iota(jnp.int32, (n, n), 0)
    j = lax.broadcasted_iota(jnp.int32, (n, n), 1)
    same = (i // GDN_C) == (j // GDN_C)
    tri = (i <= j) if upper else (i >= j)
    return jnp.where(same & tri, 1.0, 0.0).astype(F32)


def _gate_lanes(shape):
    lane = lax.broadcasted_iota(jnp.int32, shape, 1)
    return lane < GDN_H, (lane >= GDN_H) & (lane < 2 * GDN_H), (lane >= 2 * GDN_H) & (lane < 4 * GDN_H)


def gdn_gate_fwd(proj, prm, name):
    S = proj.shape[0]
    tm = min(512, S)
    ct = GDN_INP // LANES - 1

    def body(p_ref, prm_ref, o_ref):
        ab = p_ref[...]
        a_log = prm_ref[0:1, :]
        dtb = prm_ref[1:2, :]
        z = ab + dtb
        sp = jnp.maximum(z, 0.0) + jnp.log(1.0 + jnp.exp(-jnp.abs(z)))
        g = -jnp.exp(a_log) * sp
        is_f, is_b, is_beta = _gate_lanes(ab.shape)
        gf = _dot(_chunk_sum_matrix(tm, False), jnp.where(is_f, g, 0.0), HI)
        gbk = _dot(_chunk_sum_matrix(tm, True), jnp.where(is_b, g, 0.0), HI)
        o_ref[...] = gf + gbk + jnp.where(is_beta, _sigmoid(ab), 0.0)

    return pl.pallas_call(
        body, grid=(S // tm,),
        in_specs=[BS((tm, LANES), lambda i: (i, ct)), BS((8, LANES), lambda i: (0, 0))],
        out_specs=BS((tm, LANES), lambda i: (i, 0)),
        out_shape=SDS((S, LANES), F32), name=name, compiler_params=_cp(("parallel",)))(proj, prm)


def gdn_gate_bwd(proj, prm, dgb, name):
    S = proj.shape[0]
    tm = min(512, S)
    ct = GDN_INP // LANES - 1

    def body(p_ref, prm_ref, d_ref, dab_ref, dprm_ref):
        i = pl.program_id(0)
        ab = p_ref[...]
        a_log = prm_ref[0:1, :]
        dtb = prm_ref[1:2, :]
        z = ab + dtb
        sp = jnp.maximum(z, 0.0) + jnp.log(1.0 + jnp.exp(-jnp.abs(z)))
        ea = jnp.exp(a_log)
        g = -ea * sp
        is_f, is_b, is_beta = _gate_lanes(ab.shape)
        d = d_ref[...]
        dg = (_dot_tn(_chunk_sum_matrix(tm, False), jnp.where(is_f, d, 0.0), HI)
              + _dot_tn(_chunk_sum_matrix(tm, True), jnp.where(is_b, d, 0.0), HI))
        da = dg * (-ea) * _sigmoid(z)
        beta = _sigmoid(ab)
        dab_ref[...] = jnp.where(is_beta, d * beta * (1.0 - beta), da).astype(BF16)
        row = lax.broadcasted_iota(jnp.int32, (8, LANES), 0)
        part = (jnp.where(row == 0, jnp.sum(dg * g, axis=0, keepdims=True), 0.0)
                + jnp.where(row == 1, jnp.sum(da, axis=0, keepdims=True), 0.0))

        @pl.when(i == 0)
        def _():
            dprm_ref[...] = part

        @pl.when(i > 0)
        def _():
            dprm_ref[...] += part

    return pl.pallas_call(
        body, grid=(S // tm,),
        in_specs=[BS((tm, LANES), lambda i: (i, ct)), BS((8, LANES), lambda i: (0, 0)), BS((tm, LANES), lambda i: (i, 0))],
        out_specs=[BS((tm, LANES), lambda i: (i, 0)), BS((8, LANES), lambda i: (0, 0))],
        out_shape=[SDS((S, LANES), BF16), SDS((8, LANES), F32)],
        name=name, compiler_params=_cp(("arbitrary",)))(proj, prm, dgb)


def _tri_masks(d):
    i = lax.broadcasted_iota(jnp.int32, (GDN_C, GDN_C), 0)
    j = lax.broadcasted_iota(jnp.int32, (GDN_C, GDN_C), 1)
    s = (i - j) * (1 - 2 * d)
    return s >= 0, s > 0


def _split(a):
    hi = _bf(a)
    return hi, _bf(a - hi.astype(F32))


def _dot3(a, b):
    return _dot(a[0], b[0]) + (_dot(a[0], b[1]) + _dot(a[1], b[0]))


def _inv_unit_tri_many(mats):
    i = lax.broadcasted_iota(jnp.int32, mats[0].shape, 0)
    j = lax.broadcasted_iota(jnp.int32, mats[0].shape, 1)
    eye = jnp.where(i == j, 1.0, 0.0)
    ms = [-a for a in mats]
    ps = [eye + m for m in ms]
    for _ in range(int(math.log2(GDN_C)) - 1):
        sp = [_split(m) for m in ms]
        ms = [_dot3(s, s) for s in sp]
        sp = [_split(m) for m in ms]
        pp = [_split(p) for p in ps]
        ps = [p + _dot3(a, b) for p, a, b in zip(ps, pp, sp)]
    return ps


def _lane_col(x, lane_idx):
    lane = lax.broadcasted_iota(jnp.int32, x.shape, 1)
    return jnp.sum(jnp.where(lane == lane_idx, x, 0.0), axis=1, keepdims=True)


def _chunk_gates(gb_ref, grow_ref, hh, ci, d, head):
    gbv = gb_ref[ci * GDN_C:(ci + 1) * GDN_C, :]
    gcol = _lane_col(gbv, d * GDN_H + head)
    bcol = _lane_col(gbv, 2 * GDN_H + d * GDN_H + head)
    glast = jnp.where(d == 0, gcol[GDN_C - 1:GDN_C, :], gcol[0:1, :])
    return gcol, bcol, grow_ref[hh, ci:ci + 1, :], glast


def _chunk_base(q, k, gcol, grow, bcol, glast, d):
    incl, strict = _tri_masks(d)
    decay = jnp.where(incl, jnp.exp(jnp.where(incl, gcol - grow, 0.0)), 0.0)
    kb = k * bcol
    kk = _dot_nt(_bf(kb), _bf(k))
    qk = _dot_nt(_bf(q), _bf(k))
    eg = jnp.exp(gcol)
    ek = jnp.exp(glast - gcol)
    return dict(incl=incl, strict=strict, decay=decay, kb=kb, kk=kk, qk=qk, eg=eg, ek=ek, q_dec=q * eg, k_dec=k * ek,
                bcol=bcol, glast=glast)


def _block_terms(q_ref, k_ref, v_ref, gb_ref, grow_ref, d, h):
    keys = [(hh, ci) for hh in range(GDN_HP) for ci in range(GDN_GC)]
    ts = []
    for hh, ci in keys:
        rows = slice(ci * GDN_C, (ci + 1) * GDN_C)
        cols = slice(hh * GDN_DK, (hh + 1) * GDN_DK)
        gcol, bcol, grow_v, glast = _chunk_gates(gb_ref, grow_ref, hh, ci, d, h * GDN_HP + hh)
        t = _chunk_base(q_ref[rows, cols], k_ref[rows, cols], gcol, grow_v, bcol, glast, d)
        t["v"] = v_ref[rows, cols]
        ts.append(t)
    tinvs = _inv_unit_tri_many([jnp.where(t["strict"], t["kk"] * t["decay"], 0.0) for t in ts])
    sp = [_split(x) for x in tinvs]
    us = [_dot3(s, _split(t["v"] * t["bcol"])) for s, t in zip(sp, ts)]
    ws = [_dot3(s, _split(t["kb"] * t["eg"])) for s, t in zip(sp, ts)]
    for t, tinv, u, w in zip(ts, tinvs, us, ws):
        t.update(tinv=tinv, u=u, w=w)
    return keys, ts


def _gdn_specs(S, nblk, order):
    R = GDN_GC * GDN_C
    wd = GDN_HP * GDN_DK
    hb = GDN_H // GDN_HP

    def qkv_spec(part):
        return BS((R, wd), lambda d, h, n: (order(d, n), part * hb + h))

    gb_spec = BS((R, LANES), lambda d, h, n: (order(d, n), 0))
    grow_spec = BS((GDN_HP, GDN_GC, GDN_C), lambda d, h, n: (d * hb + h, order(d, n), 0))
    st_spec = BS((1, GDN_HP, GDN_GC, GDN_DK, GDN_DK), lambda d, h, n: (d, h, order(d, n), 0, 0))
    return qkv_spec, gb_spec, grow_spec, st_spec


def _lane_row(x):
    return jnp.broadcast_to(x, (1, LANES))


def _side_parts(side):
    if side is None:
        return [], [], [], [], []
    return [ANY] * len(side.ins), [ANY] * len(side.outs), list(side.outs), list(side.sems), list(side.ins)


def _side_run(side, refs, n_in, n_out, n_scr, first, last):
    if side is None:
        return
    ns, no, nm = len(side.ins), len(side.outs), len(side.sems)
    s_in = refs[n_in:n_in + ns]
    s_out = refs[n_in + ns + n_out:n_in + ns + n_out + no]
    s_sem = refs[len(refs) - nm:]

    @pl.when(first)
    def _():
        side.start(s_in, s_out, s_sem)

    @pl.when(last)
    def _():
        side.wait(s_in, s_out, s_sem)


def gdn_scan_fwd(qkv, gb, grow, name, side=None):
    S = qkv.shape[0]
    R = GDN_GC * GDN_C
    nblk = S // R
    nc = S // GDN_C
    wd = GDN_HP * GDN_DK
    heads = range(GDN_HP)

    def order(d, n):
        return n + d * (nblk - 1 - 2 * n)

    qkv_spec, gb_spec, grow_spec, st_spec = _gdn_specs(S, nblk, order)

    s_in, s_out, s_shape, s_scr_shapes, s_ops = _side_parts(side)
    hb = GDN_H // GDN_HP

    def body(*refs):
        q_ref, k_ref, v_ref, gb_ref, grow_ref = refs[:5]
        o_ref, st_ref = refs[5 + len(s_in):7 + len(s_in)]
        s_scr, u_scr, w_scr, qd_scr, kd_scr, in_scr, egl_scr = refs[7 + len(s_in) + len(s_out):14 + len(s_in) + len(s_out)]
        d = pl.program_id(0)
        h = pl.program_id(1)
        n = pl.program_id(2)
        _side_run(side, refs, 5, 2, 7, (d == 0) & (h == 0) & (n == 0), (d == 1) & (h == hb - 1) & (n == nblk - 1))

        @pl.when(n == 0)
        def _():
            s_scr[...] = jnp.zeros_like(s_scr)

        keys, ts = _block_terms(q_ref, k_ref, v_ref, gb_ref, grow_ref, d, h)
        for (hh, ci), t in zip(keys, ts):
            u_scr[hh, ci] = t["u"]
            w_scr[hh, ci] = _bf(t["w"])
            qd_scr[hh, ci] = _bf(t["q_dec"])
            kd_scr[hh, ci] = _bf(t["k_dec"])
            in_scr[hh, ci] = _bf(jnp.where(t["incl"], t["qk"] * t["decay"], 0.0))
            egl_scr[hh, ci] = _lane_row(jnp.exp(t["glast"]))

        def chunk(cc, carry):
            ci = cc + d * (GDN_GC - 1 - 2 * cc)
            rows = pl.ds(pl.multiple_of(ci * GDN_C, GDN_C), GDN_C)
            sts = [s_scr[hh] for hh in heads]
            for hh in heads:
                st_ref[0, hh, ci] = sts[hh]
            sbs = [_bf(st) for st in sts]
            vns = [_bf(u_scr[hh, ci] - _dot(w_scr[hh, ci], sbs[hh])) for hh in heads]
            for hh in heads:
                s_scr[hh] = sts[hh] * egl_scr[hh, ci] + _dot_tn(kd_scr[hh, ci], vns[hh])
            for hh in heads:
                o_ref[0, rows, hh * GDN_DK:(hh + 1) * GDN_DK] = _dot(qd_scr[hh, ci], sbs[hh]) + _dot(in_scr[hh, ci], vns[hh])
            return carry

        lax.fori_loop(0, GDN_GC, chunk, 0)

    blk = (GDN_HP, GDN_GC, GDN_C, GDN_DK)
    return pl.pallas_call(
        body, grid=(2, GDN_H // GDN_HP, nblk),
        in_specs=[qkv_spec(0), qkv_spec(1), qkv_spec(2), gb_spec, grow_spec] + s_in,
        out_specs=[BS((1, R, wd), lambda d, h, n: (d, order(d, n), h)), st_spec] + s_out,
        out_shape=[SDS((2, S, GDN_H * GDN_DK), F32), SDS((2, GDN_H, nc, GDN_DK, GDN_DK), F32)] + s_shape,
        scratch_shapes=[pltpu.VMEM((GDN_HP, GDN_DK, GDN_DK), F32), pltpu.VMEM(blk, F32), pltpu.VMEM(blk, BF16),
                        pltpu.VMEM(blk, BF16), pltpu.VMEM(blk, BF16), pltpu.VMEM((GDN_HP, GDN_GC, GDN_C, GDN_C), BF16),
                        pltpu.VMEM((GDN_HP, GDN_GC, 1, LANES), F32)] + s_scr_shapes,
        name=name, compiler_params=_cp(("arbitrary", "arbitrary", "arbitrary")))(qkv, qkv, qkv, gb, grow, *s_ops)


def gdn_scan_bwd(qkv, gb, grow, states, do, name, side=None):
    S = qkv.shape[0]
    R = GDN_GC * GDN_C
    nblk = S // R
    wd = GDN_HP * GDN_DK
    heads = range(GDN_HP)

    def order(d, n):
        return (nblk - 1 - n) - d * (nblk - 1 - 2 * n)

    qkv_spec, gb_spec, grow_spec, st_spec = _gdn_specs(S, nblk, order)

    s_in, s_out, s_shape, s_scr_shapes, s_ops = _side_parts(side)
    hb = GDN_H // GDN_HP

    def body(*refs):
        q_ref, k_ref, v_ref, gb_ref, grow_ref, st_ref, do_ref = refs[:7]
        dqkv_ref, dgate_ref = refs[7 + len(s_in):9 + len(s_in)]
        (ds_scr, w_scr, kd_scr, dv1_scr, qtdo_scr, egl_scr, dsin_scr, dvn_scr,
         sdot_scr) = refs[9 + len(s_in) + len(s_out):18 + len(s_in) + len(s_out)]
        d = pl.program_id(0)
        h = pl.program_id(1)
        n = pl.program_id(2)
        _side_run(side, refs, 7, 2, 9, (d == 0) & (h == 0) & (n == 0), (d == 1) & (h == hb - 1) & (n == nblk - 1))

        @pl.when(n == 0)
        def _():
            ds_scr[...] = jnp.zeros_like(ds_scr)

        keys, ts = _block_terms(q_ref, k_ref, v_ref, gb_ref, grow_ref, d, h)
        for (hh, ci), t in zip(keys, ts):
            rows = slice(ci * GDN_C, (ci + 1) * GDN_C)
            t["wb"] = _bf(t["w"])
            t["dob"] = _bf(do_ref[rows, hh * GDN_DK:(hh + 1) * GDN_DK])
            t["sb"] = _bf(st_ref[0, hh, ci])
        for (hh, ci), t in zip(keys, ts):
            t["vnb"] = _bf(t["u"] - _dot(t["wb"], t["sb"]))
            w_scr[hh, ci] = t["wb"]
            kd_scr[hh, ci] = _bf(t["k_dec"])
            dv1_scr[hh, ci] = _dot_tn(_bf(jnp.where(t["incl"], t["qk"] * t["decay"], 0.0)), t["dob"])
            qtdo_scr[hh, ci] = _dot_tn(_bf(t["q_dec"]), t["dob"])
            egl_scr[hh, ci] = _lane_row(jnp.exp(t["glast"]))

        def chunk(cc, carry):
            ci = (GDN_GC - 1 - cc) - d * (GDN_GC - 1 - 2 * cc)
            dsns = [ds_scr[hh] for hh in heads]
            dsbs = [_bf(x) for x in dsns]
            dvns = [dv1_scr[hh, ci] + _dot(kd_scr[hh, ci], dsbs[hh]) for hh in heads]
            for hh in heads:
                ds_scr[hh] = qtdo_scr[hh, ci] + egl_scr[hh, ci] * dsns[hh] - _dot_tn(w_scr[hh, ci], _bf(dvns[hh]))
            for hh in heads:
                dsin_scr[hh, ci] = dsbs[hh]
                dvn_scr[hh, ci] = dvns[hh]
                sd = jnp.sum(jnp.sum(st_ref[0, hh, ci] * dsns[hh], axis=1, keepdims=True), axis=0, keepdims=True)
                sdot_scr[hh, ci] = _lane_row(sd)
            return carry

        lax.fori_loop(0, GDN_GC, chunk, 0)

        for (hh, ci), t in zip(keys, ts):
            t["d_vnew"] = dvn_scr[hh, ci]
            t["dvb"] = _bf(t["d_vnew"])
            t["dsb"] = dsin_scr[hh, ci]
        for t in ts:
            t["d_intra"] = jnp.where(t["incl"], _dot_nt(t["dob"], t["vnb"]), 0.0)
            t["d_qdec"] = _dot_nt(t["dob"], t["sb"])
            t["d_kdec"] = _dot_nt(t["vnb"], t["dsb"])
            t["dw"] = -_dot_nt(t["dvb"], t["sb"])
        for t in ts:
            tts = _split(t["tinv"].T)
            t["d_ru"] = _dot3(tts, _split(t["d_vnew"]))
            t["d_rw"] = _dot3(tts, _split(t["dw"]))
        for t in ts:
            t["da"] = -jnp.where(t["strict"], _dot_nt(_bf(t["d_ru"]), _bf(t["u"])) + _dot_nt(_bf(t["d_rw"]), t["wb"]), 0.0)
        for (hh, ci), t in zip(keys, ts):
            rows = slice(ci * GDN_C, (ci + 1) * GDN_C)
            cols = slice(hh * GDN_DK, (hh + 1) * GDN_DK)
            q, k, v = q_ref[rows, cols], k_ref[rows, cols], t["v"]
            decay, kb, eg, ek, bcol = t["decay"], t["kb"], t["eg"], t["ek"], t["bcol"]
            d_ru, d_rw, da, d_intra, d_qdec, d_kdec = t["d_ru"], t["d_rw"], t["da"], t["d_intra"], t["d_qdec"], t["d_kdec"]
            kbf, qbf = _bf(k), _bf(q)
            dgl = egl_scr[hh, ci][:, 0:1] * sdot_scr[hh, ci][:, 0:1]
            dv = d_ru * bcol
            dbeta = jnp.sum(d_ru * v, axis=1, keepdims=True)
            dkb = d_rw * eg
            dg = jnp.sum(d_rw * kb, axis=1, keepdims=True) * eg
            dkk = _bf(da * decay)
            dqk = _bf(d_intra * decay)
            dkb = dkb + _dot(dkk, kbf)
            dk = _dot_tn(dkk, _bf(kb)) + _dot_tn(dqk, qbf)
            dq = _dot(dqk, kbf) + d_qdec * eg
            dd = (da * t["kk"] + d_intra * t["qk"]) * decay
            dg = dg + jnp.sum(dd, axis=1, keepdims=True) - jnp.sum(dd.T, axis=1, keepdims=True)
            dg = dg + jnp.sum(d_qdec * t["q_dec"], axis=1, keepdims=True)
            dk = dk + d_kdec * ek
            ee = jnp.sum(d_kdec * t["k_dec"], axis=1, keepdims=True)
            dg = dg - ee
            dgl = dgl + jnp.sum(ee, axis=0, keepdims=True)
            dk = dk + dkb * bcol
            dbeta = dbeta + jnp.sum(dkb * k, axis=1, keepdims=True)
            ridx = lax.broadcasted_iota(jnp.int32, (GDN_C, 1), 0)
            dg = dg + jnp.where(ridx == (GDN_C - 1) * (1 - d), dgl, 0.0)
            dqkv_ref[0, 0, rows, cols] = dq
            dqkv_ref[0, 1, rows, cols] = dk
            dqkv_ref[0, 2, rows, cols] = dv
            lane2 = lax.broadcasted_iota(jnp.int32, (GDN_C, 2), 1)
            dgate_ref[0, hh, rows, :] = jnp.where(lane2 == 0, dg, dbeta)

    blk = (GDN_HP, GDN_GC, GDN_C, GDN_DK)
    sq = (GDN_HP, GDN_GC, GDN_DK, GDN_DK)
    row = (GDN_HP, GDN_GC, 1, LANES)
    return pl.pallas_call(
        body, grid=(2, GDN_H // GDN_HP, nblk),
        in_specs=[qkv_spec(0), qkv_spec(1), qkv_spec(2), gb_spec, grow_spec, st_spec,
                  BS((R, wd), lambda d, h, n: (order(d, n), h))] + s_in,
        out_specs=[BS((1, 3, R, wd), lambda d, h, n: (d, 0, order(d, n), h)),
                   BS((1, GDN_HP, R, 2), lambda d, h, n: (d, h, order(d, n), 0))] + s_out,
        out_shape=[SDS((2, 3, S, GDN_H * GDN_DK), F32), SDS((2, GDN_H, S, 2), F32)] + s_shape,
        scratch_shapes=[pltpu.VMEM((GDN_HP, GDN_DK, GDN_DK), F32), pltpu.VMEM(blk, BF16), pltpu.VMEM(blk, BF16),
                        pltpu.VMEM(blk, F32), pltpu.VMEM(sq, F32), pltpu.VMEM(row, F32), pltpu.VMEM(sq, BF16),
                        pltpu.VMEM(blk, F32), pltpu.VMEM(row, F32)] + s_scr_shapes,
        name=name, compiler_params=_cp(("arbitrary", "arbitrary", "arbitrary")))(qkv, qkv, qkv, gb, grow, states, do, *s_ops)


def gdn_post_fwd(o2, proj, nw, name):
    S = proj.shape[0]
    tm = min(512, S)
    zoff = GDN_QKV // LANES

    def body(o_ref, z_ref, nw_ref, y_ref):
        o = o_ref[0] + o_ref[1]
        z = z_ref[...]
        r = lax.rsqrt(jnp.mean(o * o, axis=-1, keepdims=True) + RMS_EPS)
        y_ref[...] = (o * r * nw_ref[...] * (z * _sigmoid(z))).astype(BF16)

    return pl.pallas_call(
        body, grid=(S // tm, GDN_H),
        in_specs=[BS((2, tm, LANES), lambda i, h: (0, i, h)), BS((tm, LANES), lambda i, h: (i, zoff + h)),
                  BS((1, LANES), lambda i, h: (0, 0))],
        out_specs=BS((tm, LANES), lambda i, h: (i, h)),
        out_shape=SDS((S, GDN_H * GDN_DK), BF16), name=name, compiler_params=_cp(("parallel", "parallel")))(o2, proj, nw)


def gdn_post_bwd(o2, proj, nw, dy, name):
    S = proj.shape[0]
    tm = min(512, S)
    zoff = GDN_QKV // LANES

    def body(o_ref, z_ref, nw_ref, dy_ref, do_ref, dz_ref, dnw_ref):
        first = (pl.program_id(0) == 0) & (pl.program_id(1) == 0)
        o = o_ref[0] + o_ref[1]
        z = z_ref[...]
        nwv = nw_ref[...]
        dyv = dy_ref[...]
        r = lax.rsqrt(jnp.mean(o * o, axis=-1, keepdims=True) + RMS_EPS)
        n = o * r
        sg = _sigmoid(z)
        sz = z * sg
        dz_ref[...] = (dyv * n * nwv * (sg * (1.0 + z * (1.0 - sg)))).astype(BF16)
        dn = dyv * nwv * sz
        do_ref[...] = r * (dn - n * jnp.mean(dn * n, axis=-1, keepdims=True))
        part = jnp.sum(dyv * n * sz, axis=0, keepdims=True)

        @pl.when(first)
        def _():
            dnw_ref[...] = part

        @pl.when(jnp.logical_not(first))
        def _():
            dnw_ref[...] += part

    blk = BS((tm, LANES), lambda i, h: (i, h))
    return pl.pallas_call(
        body, grid=(S // tm, GDN_H),
        in_specs=[BS((2, tm, LANES), lambda i, h: (0, i, h)), BS((tm, LANES), lambda i, h: (i, zoff + h)),
                  BS((1, LANES), lambda i, h: (0, 0)), blk],
        out_specs=[blk, blk, BS((1, LANES), lambda i, h: (0, 0))],
        out_shape=[SDS((S, GDN_H * GDN_DK), F32), SDS((S, GDN_H * GDN_DK), BF16), SDS((1, LANES), F32)],
        name=name, compiler_params=_cp(("arbitrary", "arbitrary")))(o2, proj, nw, dy)


def _gate_prm(a_log, dt_bias):
    z = jnp.zeros((8, LANES), F32)
    z = z.at[0, :2 * GDN_H].set(a_log.reshape(-1))
    return z.at[1, :2 * GDN_H].set(dt_bias.reshape(-1))


def gdn_fwd(x, g, w_all, convw, a_log, dt_bias, nw, w_out, tag, side=None):
    S = x.shape[0]
    h = rms_fwd(x, g, f"{tag}_rms")
    proj = mm(h, w_all, name=f"{tag}_proj", tn=1408)
    qkv = gdn_pre_fwd(proj, convw, f"{tag}_pre")
    prm = _gate_prm(a_log, dt_bias)
    gb = gdn_gate_fwd(proj, prm, f"{tag}_gate")
    grow = gb[:, :2 * GDN_H].T.reshape(2 * GDN_H, S // GDN_C, GDN_C)
    o2, states, *side_out = gdn_scan_fwd(qkv, gb, grow, f"{tag}_scan", side)
    y = gdn_post_fwd(o2, proj, nw, f"{tag}_post")
    xn = mm(y, w_out, name=f"{tag}_out", epi=_add, extras=(x,))
    return xn, (h, proj, qkv, prm, gb, grow, o2, states, y), side_out


def gdn_bwd(x, g, w_all, convw, nw, w_out, saved, dx, dxb, tag, side=None):
    S = x.shape[0]
    h, proj, qkv, prm, gb, grow, o2, states, y = saved
    dw_out = mm(y, dxb, name=f"{tag}_dwout", ta=True)
    dy = mm(dxb, w_out, name=f"{tag}_dy", tb=True)
    do, dz, dnw = gdn_post_bwd(o2, proj, nw, dy, f"{tag}_postb")
    dqkv, dgate, *side_out = gdn_scan_bwd(qkv, gb, grow, states, do, f"{tag}_scanb", side)
    dgb = jnp.transpose(dgate, (2, 3, 0, 1)).reshape(S, 4 * GDN_H)
    dgb = jnp.pad(dgb, ((0, 0), (0, LANES - 4 * GDN_H)))
    dab, dprm = gdn_gate_bwd(proj, prm, dgb, f"{tag}_gateb")
    dpq, dconvw = gdn_pre_bwd(proj, convw, dqkv, f"{tag}_preb")
    dproj = jnp.concatenate([dpq, dz, dab], axis=1)
    dw_all = mm(h, dproj, name=f"{tag}_dwin", ta=True, tn=384)
    dh = mm(dproj, w_all, name=f"{tag}_dh", tb=True, tk=1408)
    dx, dxb, dg = rms_bwd(x, g, dh, dx, f"{tag}_rmsb")
    da_log = dprm[0, :2 * GDN_H].reshape(2, GDN_H)
    ddt = dprm[1, :2 * GDN_H].reshape(2, GDN_H)
    return dx, dxb, dg, dw_all, dconvw, da_log, ddt, dnw, dw_out, side_out


def _rel_bucket_np(rel):
    nb = REL_BUCKETS // 2
    max_exact = nb // 2
    ret = np.where(rel > 0, nb, 0)
    n = np.abs(rel)
    nf = np.maximum(n, 1).astype(np.float32)
    large = max_exact + (np.log(nf / max_exact) / np.float32(math.log(REL_MAX_DIST / max_exact))
                         * (nb - max_exact)).astype(np.int32)
    large = np.minimum(large, nb - 1)
    return ret + np.where(n < max_exact, n, large)


def _toeplitz(f, rows, cols):
    period = rows + cols
    e = jnp.pad(f, ((0, 0), (0, period - f.shape[1])))
    y = jnp.tile(e, (1, rows))[:, :rows * (period - 1)]
    return y.reshape(f.shape[0], rows, period - 1)[:, :, :cols]


ATT_Q = DSWA_HALF
ATT_W = 3 * DSWA_HALF
ATT_TB = 1024
ATT_PAIRS = DSWA_HG // 2


def _bias_mats(rel_table, gi):
    _, dil = DSWA_CFG[gi]
    offs = np.arange(-DSWA_HALF, DSWA_HALF + 1)
    onehot = jnp.asarray(np.eye(REL_BUCKETS, dtype=np.float32)[_rel_bucket_np(offs * dil)])
    f = jnp.dot(onehot, rel_table, precision=HI)[:, gi * DSWA_HG:(gi + 1) * DSWA_HG].T
    bias = _toeplitz(f, ATT_Q, ATT_W)
    bias_t = jnp.transpose(_toeplitz(f[:, ::-1], ATT_Q, ATT_W), (0, 2, 1))
    return bias.reshape(ATT_PAIRS, 2, ATT_Q, ATT_W), bias_t.reshape(ATT_PAIRS, 2, ATT_W, ATT_Q)


def _att_specs(S, d, col):
    halo = DSWA_HALF * d
    per = ATT_TB // halo
    last = S // halo - 1
    cur = BS((ATT_TB, LANES), lambda p, tb: (tb, col(p)))
    prev = BS((halo, LANES), lambda p, tb: (jnp.maximum(tb * per - 1, 0), col(p)))
    nxt = BS((halo, LANES), lambda p, tb: (jnp.minimum((tb + 1) * per, last), col(p)))
    return prev, cur, nxt


def _att_specs3(S, d, lead):
    halo = DSWA_HALF * d
    per = ATT_TB // halo
    last = S // halo - 1
    cur = BS((1, ATT_TB, LANES), lambda p, tb: (lead(p), tb, 0))
    prev = BS((1, halo, LANES), lambda p, tb: (lead(p), jnp.maximum(tb * per - 1, 0), 0))
    nxt = BS((1, halo, LANES), lambda p, tb: (lead(p), jnp.minimum((tb + 1) * per, last), 0))
    return prev, cur, nxt


class _Pieces:
    def __init__(self, prev, cur, nxt, d, lead=None, cast=None):
        self.refs, self.d, self.lead, self.cast, self.cache = (prev, cur, nxt), d, lead, cast, {}
        self.halo = DSWA_HALF * d
        self.nsb = ATT_TB // self.halo

    def __call__(self, r, sb):
        if (r, sb) not in self.cache:
            ref = self.refs[0] if sb < 0 else self.refs[2] if sb >= self.nsb else self.refs[1]
            start = r + (self.halo * sb if 0 <= sb < self.nsb else 0)
            rows = pl.ds(start, ATT_Q, stride=self.d) if self.d > 1 else pl.ds(start, ATT_Q)
            v = ref[rows, :] if self.lead is None else ref[0, rows, :]
            self.cache[(r, sb)] = v if self.cast is None else v.astype(self.cast)
        return self.cache[(r, sb)]

    def window(self, r, sb):
        return jnp.concatenate([self(r, sb - 1), self(r, sb), self(r, sb + 1)], axis=0)


ATT_GROUP = 8


def _tile_groups(d, nsb):
    tiles = [(r, sb) for r in range(d) for sb in range(nsb)]
    return [tiles[i:i + ATT_GROUP] for i in range(0, len(tiles), ATT_GROUP)]


def _tile_rows(r, sb, d):
    start = r + DSWA_HALF * d * sb
    return pl.ds(start, ATT_Q, stride=d) if d > 1 else pl.ds(start, ATT_Q)


def _tile_valid(tb, r, sb, d, S, transposed):
    shape = (ATT_W, ATT_Q) if transposed else (ATT_Q, ATT_W)
    blk = lax.broadcasted_iota(jnp.int32, shape, 1 if transposed else 0)
    win = lax.broadcasted_iota(jnp.int32, shape, 0 if transposed else 1)
    tok = tb * ATT_TB + r + d * (DSWA_HALF * (sb - 1) + win)
    return (jnp.abs(win - DSWA_HALF - blk) <= DSWA_HALF) & (tok >= 0) & (tok < S)


def _head_masks():
    lane = lax.broadcasted_iota(jnp.int32, (1, LANES), 1)
    return [lane < DSWA_E, lane >= DSWA_E], lane


def attn_fwd(qkv, bias, gi, name):
    S = qkv.shape[0]
    d = DSWA_CFG[gi][1]
    nsb = ATT_TB // (DSWA_HALF * d)
    npair = DSWA_HEADS // 2
    q_spec = _att_specs(S, d, lambda p: gi * ATT_PAIRS + p)[1]
    k_specs = _att_specs(S, d, lambda p: npair + gi * ATT_PAIRS + p)
    v_specs = _att_specs(S, d, lambda p: 2 * npair + gi * ATT_PAIRS + p)

    def body(q_ref, kp, kc, kn, vp, vc, vn, b_ref, o_ref, lse_ref):
        tb = pl.program_id(1)
        masks, lane = _head_masks()
        kpc = _Pieces(kp, kc, kn, d, cast=BF16)
        vpc = _Pieces(vp, vc, vn, d, cast=BF16)
        scale = DSWA_E ** -0.5
        for grp in _tile_groups(d, nsb):
            rows = [_tile_rows(r, sb, d) for r, sb in grp]
            qs = [q_ref[rw, :] for rw in rows]
            kws = [kpc.window(r, sb) for r, sb in grp]
            vws = [vpc.window(r, sb) for r, sb in grp]
            valids = [_tile_valid(tb, r, sb, d, S, False) for r, sb in grp]
            both = [(t, hh) for t in range(len(grp)) for hh in range(2)]
            ss = [_dot_nt(_bf(jnp.where(masks[hh], qs[t], 0.0)), kws[t]) * scale + b_ref[0, hh] for t, hh in both]
            ss = [jnp.where(valids[t], s, NEG_INF) for (t, hh), s in zip(both, ss)]
            ms = [jnp.max(s, axis=-1, keepdims=True) for s in ss]
            ps = [jnp.exp(s - m) for s, m in zip(ss, ms)]
            ls = [jnp.sum(p, axis=-1, keepdims=True) for p in ps]
            os = [_dot(_bf(p / l), vws[t]) for (t, hh), p, l in zip(both, ps, ls)]
            for t, rw in enumerate(rows):
                o_ref[rw, :] = jnp.where(masks[0], os[2 * t], os[2 * t + 1])
                lse_ref[0, rw, :] = (jnp.where(lane == 0, ms[2 * t] + jnp.log(ls[2 * t]), 0.0)
                                     + jnp.where(lane == 1, ms[2 * t + 1] + jnp.log(ls[2 * t + 1]), 0.0))

    return pl.pallas_call(
        body, grid=(ATT_PAIRS, S // ATT_TB),
        in_specs=[q_spec, *k_specs, *v_specs, BS((1, 2, ATT_Q, ATT_W), lambda p, tb: (p, 0, 0, 0))],
        out_specs=[BS((ATT_TB, LANES), lambda p, tb: (tb, p)), BS((1, ATT_TB, LANES), lambda p, tb: (p, tb, 0))],
        out_shape=[SDS((S, DSWA_HG * DSWA_E), F32), SDS((ATT_PAIRS, S, LANES), F32)],
        name=name, compiler_params=_cp(("parallel", "parallel")))(qkv, qkv, qkv, qkv, qkv, qkv, qkv, bias)


def attn_bwd_q(qkv, bias, lse, do, dd, gi, name):
    S = qkv.shape[0]
    d = DSWA_CFG[gi][1]
    nsb = ATT_TB // (DSWA_HALF * d)
    npair = DSWA_HEADS // 2
    q_spec = _att_specs(S, d, lambda p: gi * ATT_PAIRS + p)[1]
    k_specs = _att_specs(S, d, lambda p: npair + gi * ATT_PAIRS + p)
    v_specs = _att_specs(S, d, lambda p: 2 * npair + gi * ATT_PAIRS + p)
    bspec = BS((1, 2, ATT_Q, ATT_W), lambda p, tb: (p, 0, 0, 0))

    def body(q_ref, kp, kc, kn, vp, vc, vn, b_ref, lse_ref, do_ref, dd_ref, dq_ref, db_ref):
        tb = pl.program_id(1)
        masks, lane = _head_masks()
        kpc = _Pieces(kp, kc, kn, d, cast=BF16)
        vpc = _Pieces(vp, vc, vn, d, cast=BF16)
        db = [jnp.zeros((ATT_Q, ATT_W), F32), jnp.zeros((ATT_Q, ATT_W), F32)]
        scale = DSWA_E ** -0.5
        for grp in _tile_groups(d, nsb):
            rows = [_tile_rows(r, sb, d) for r, sb in grp]
            qs = [q_ref[rw, :] for rw in rows]
            dos = [do_ref[0, rw, :] for rw in rows]
            lses = [lse_ref[0, rw, :] for rw in rows]
            dds = [dd_ref[0, 0, rw, :] for rw in rows]
            kws = [kpc.window(r, sb) for r, sb in grp]
            vws = [vpc.window(r, sb) for r, sb in grp]
            valids = [_tile_valid(tb, r, sb, d, S, False) for r, sb in grp]
            both = [(t, hh) for t in range(len(grp)) for hh in range(2)]
            ss = [_dot_nt(_bf(jnp.where(masks[hh], qs[t], 0.0)), kws[t]) * scale + b_ref[0, hh] for t, hh in both]
            dps = [_dot_nt(_bf(jnp.where(masks[hh], dos[t], 0.0)), vws[t]) for t, hh in both]
            ps = [jnp.exp(jnp.where(valids[t], s - lses[t][:, hh:hh + 1], NEG_INF)) for (t, hh), s in zip(both, ss)]
            dss = [p * (dp - dds[t][:, hh:hh + 1]) for (t, hh), p, dp in zip(both, ps, dps)]
            dqs = [_dot(_bf(ds), kws[t]) * scale for (t, hh), ds in zip(both, dss)]
            for t, rw in enumerate(rows):
                dq_ref[rw, :] = jnp.where(masks[0], dqs[2 * t], dqs[2 * t + 1])
                db[0] = db[0] + dss[2 * t]
                db[1] = db[1] + dss[2 * t + 1]

        @pl.when(tb == 0)
        def _():
            db_ref[0, 0] = db[0]
            db_ref[0, 1] = db[1]

        @pl.when(tb > 0)
        def _():
            db_ref[0, 0] += db[0]
            db_ref[0, 1] += db[1]

    return pl.pallas_call(
        body, grid=(ATT_PAIRS, S // ATT_TB),
        in_specs=[q_spec, *k_specs, *v_specs, bspec, BS((1, ATT_TB, LANES), lambda p, tb: (p, tb, 0)),
                  BS((1, ATT_TB, LANES), lambda p, tb: (gi, tb, p)), BS((1, 1, ATT_TB, LANES), lambda p, tb: (gi, p, tb, 0))],
        out_specs=[BS((ATT_TB, LANES), lambda p, tb: (tb, p)), bspec],
        out_shape=[SDS((S, DSWA_HG * DSWA_E), F32), SDS((ATT_PAIRS, 2, ATT_Q, ATT_W), F32)],
        name=name, compiler_params=_cp(("parallel", "arbitrary")))(qkv, qkv, qkv, qkv, qkv, qkv, qkv, bias, lse, do, dd)


def attn_bwd_kv(qkv, bias_t, lse, do, dd, gi, name):
    S = qkv.shape[0]
    d = DSWA_CFG[gi][1]
    nsb = ATT_TB // (DSWA_HALF * d)
    npair = DSWA_HEADS // 2
    q_specs = _att_specs(S, d, lambda p: gi * ATT_PAIRS + p)
    k_spec = _att_specs(S, d, lambda p: npair + gi * ATT_PAIRS + p)[1]
    v_spec = _att_specs(S, d, lambda p: 2 * npair + gi * ATT_PAIRS + p)[1]
    halo = DSWA_HALF * d
    per = ATT_TB // halo
    last = S // halo - 1

    def do_spec(rows, blk):
        return BS((1, rows, LANES), lambda p, tb: (gi, blk(tb), p))

    def dd_spec(rows, blk):
        return BS((1, 1, rows, LANES), lambda p, tb: (gi, p, blk(tb), 0))

    blks = [(halo, lambda tb: jnp.maximum(tb * per - 1, 0)), (ATT_TB, lambda tb: tb),
            (halo, lambda tb: jnp.minimum((tb + 1) * per, last))]
    do_specs = [do_spec(*b) for b in blks]
    dd_specs = [dd_spec(*b) for b in blks]
    lse_specs = _att_specs3(S, d, lambda p: p)

    class _Lead4:
        def __init__(self, ref):
            self.ref = ref

        def __getitem__(self, idx):
            return self.ref[(0,) + idx]

    def body(k_ref, v_ref, qp, qc, qn, dop, doc, don, lp, lc, ln, ddp, ddc, ddn, b_ref, dk_ref, dv_ref):
        tb = pl.program_id(1)
        masks, lane = _head_masks()
        qpc = _Pieces(qp, qc, qn, d)
        dopc = _Pieces(dop, doc, don, d, lead=True)
        lpc = _Pieces(lp, lc, ln, d, lead=True)
        ddpc = _Pieces(_Lead4(ddp), _Lead4(ddc), _Lead4(ddn), d, lead=True)
        scale = DSWA_E ** -0.5
        for grp in _tile_groups(d, nsb):
            rows = [_tile_rows(r, sb, d) for r, sb in grp]
            kcs = [_bf(k_ref[rw, :]) for rw in rows]
            vcs = [_bf(v_ref[rw, :]) for rw in rows]
            qws = [qpc.window(r, sb) for r, sb in grp]
            dows = [dopc.window(r, sb) for r, sb in grp]
            lws = [lpc.window(r, sb) for r, sb in grp]
            ddws = [ddpc.window(r, sb) for r, sb in grp]
            qwbs = [_bf(x) for x in qws]
            dowbs = [_bf(x) for x in dows]
            valids = [_tile_valid(tb, r, sb, d, S, True) for r, sb in grp]
            both = [(t, hh) for t in range(len(grp)) for hh in range(2)]
            ss = [_dot_nt(_bf(jnp.where(masks[hh], qws[t], 0.0)), kcs[t]) * scale + b_ref[0, hh] for t, hh in both]
            dps = [_dot_nt(_bf(jnp.where(masks[hh], dows[t], 0.0)), vcs[t]) for t, hh in both]
            ps = [jnp.exp(jnp.where(valids[t], s - lws[t][:, hh:hh + 1], NEG_INF)) for (t, hh), s in zip(both, ss)]
            dvs = [_dot_tn(_bf(p), dowbs[t]) for (t, hh), p in zip(both, ps)]
            dss = [p * (dp - ddws[t][:, hh:hh + 1]) for (t, hh), p, dp in zip(both, ps, dps)]
            dks = [_dot_tn(_bf(ds), qwbs[t]) * scale for (t, hh), ds in zip(both, dss)]
            for t, rw in enumerate(rows):
                dk_ref[rw, :] = jnp.where(masks[0], dks[2 * t], dks[2 * t + 1])
                dv_ref[rw, :] = jnp.where(masks[0], dvs[2 * t], dvs[2 * t + 1])

    out = BS((ATT_TB, LANES), lambda p, tb: (tb, p))
    return pl.pallas_call(
        body, grid=(ATT_PAIRS, S // ATT_TB),
        in_specs=[k_spec, v_spec, *q_specs, *do_specs, *lse_specs, *dd_specs,
                  BS((1, 2, ATT_W, ATT_Q), lambda p, tb: (p, 0, 0, 0))],
        out_specs=[out, out],
        out_shape=[SDS((S, DSWA_HG * DSWA_E), F32), SDS((S, DSWA_HG * DSWA_E), F32)],
        name=name, compiler_params=_cp(("parallel", "parallel")))(
            qkv, qkv, qkv, qkv, qkv, do, do, do, lse, lse, lse, dd, dd, dd, bias_t)


def _pair_alphas(lses):
    m = jnp.maximum(jnp.maximum(lses[0], lses[1]), lses[2])
    e = [jnp.exp(t - m) for t in lses]
    tot = e[0] + e[1] + e[2]
    return [t / tot for t in e]


def _pair_expand(a, lane):
    return jnp.where(lane < DSWA_E, a[:, 0:1], a[:, 1:2])


def combine_fwd(o_raw, lse, name):
    S = o_raw.shape[0]
    tm = min(512, S)

    def body(o_ref, l_ref, y_ref):
        g = pl.program_id(2)
        lane = lax.broadcasted_iota(jnp.int32, (1, LANES), 1)
        alphas = _pair_alphas([l_ref[0, 0], l_ref[1, 0], l_ref[2, 0]])
        a = jnp.where(g == 0, alphas[0], jnp.where(g == 1, alphas[1], alphas[2]))
        y_ref[...] = (o_ref[...] * _pair_expand(a, lane)).astype(BF16)

    blk = BS((tm, LANES), lambda i, p, g: (i, g * ATT_PAIRS + p))
    return pl.pallas_call(
        body, grid=(S // tm, ATT_PAIRS, 3),
        in_specs=[blk, BS((3, 1, tm, LANES), lambda i, p, g: (0, p, i, 0))], out_specs=blk,
        out_shape=SDS((S, DSWA_W), BF16), name=name, compiler_params=_cp(("parallel", "parallel", "parallel")))(o_raw, lse)


def combine_bwd(o_raw, lse, dy, name):
    S = o_raw.shape[0]
    tm = min(512, S)

    def body(o0, o1, o2, l_ref, d0, d1, d2, do_ref, dd_ref):
        lane = lax.broadcasted_iota(jnp.int32, (1, LANES), 1)
        alphas = _pair_alphas([l_ref[0, 0], l_ref[1, 0], l_ref[2, 0]])
        c = jnp.zeros((tm, LANES), F32)
        for g, (o_ref, dy_ref) in enumerate(((o0, d0), (o1, d1), (o2, d2))):
            dyv = dy_ref[...]
            do_ref[g] = dyv * _pair_expand(alphas[g], lane)
            prod = o_ref[...] * dyv
            dal = (jnp.where(lane == 0, jnp.sum(jnp.where(lane < DSWA_E, prod, 0.0), axis=1, keepdims=True), 0.0)
                   + jnp.where(lane == 1, jnp.sum(jnp.where(lane >= DSWA_E, prod, 0.0), axis=1, keepdims=True), 0.0))
            c = c + alphas[g] * dal
        for g in range(3):
            dd_ref[g, 0] = alphas[g] * c

    def col(g):
        return BS((tm, LANES), lambda i, p: (i, g * ATT_PAIRS + p))

    return pl.pallas_call(
        body, grid=(S // tm, ATT_PAIRS),
        in_specs=[col(0), col(1), col(2), BS((3, 1, tm, LANES), lambda i, p: (0, p, i, 0)), col(0), col(1), col(2)],
        out_specs=[BS((3, tm, LANES), lambda i, p: (0, i, p)), BS((3, 1, tm, LANES), lambda i, p: (0, p, i, 0))],
        out_shape=[SDS((3, S, DSWA_HG * DSWA_E), F32), SDS((3, ATT_PAIRS, S, LANES), F32)],
        name=name, compiler_params=_cp(("parallel", "parallel")))(o_raw, o_raw, o_raw, lse, dy, dy, dy)


def dswa_fwd(x, g, w_in, w_out, rel_table, tag):
    h = rms_fwd(x, g, f"{tag}_rms")
    qkv = mm(h, w_in, name=f"{tag}_qkv", tn=1152)
    outs, lses = [], []
    for gi in range(3):
        bias, _ = _bias_mats(rel_table, gi)
        o, lse = attn_fwd(qkv, bias, gi, f"{tag}_att{gi}")
        outs.append(o)
        lses.append(lse)
    o_raw = jnp.concatenate(outs, axis=1)
    lse = jnp.stack(lses)
    y = combine_fwd(o_raw, lse, f"{tag}_comb")
    xn = mm(y, w_out, name=f"{tag}_out", epi=_add, extras=(x,))
    return xn, (h, qkv, o_raw, lse, y)


def dswa_bwd(x, g, w_in, w_out, rel_table, saved, dx, dxb, tag):
    h, qkv, o_raw, lse, y = saved
    dw_out = mm(y, dxb, name=f"{tag}_dwout", ta=True, tm=384)
    dy = mm(dxb, w_out, name=f"{tag}_dy", tb=True, tn=384)
    do_raw, dd = combine_bwd(o_raw, lse, dy, f"{tag}_combb")
    dqs, dks, dvs = [], [], []
    drel = jnp.zeros_like(rel_table)
    for gi in range(3):
        (bias, bias_t), bias_vjp = jax.vjp(lambda tbl: _bias_mats(tbl, gi), rel_table)
        dq, dbias = attn_bwd_q(qkv, bias, lse[gi], do_raw, dd, gi, f"{tag}_attq{gi}")
        dk, dv = attn_bwd_kv(qkv, bias_t, lse[gi], do_raw, dd, gi, f"{tag}_attkv{gi}")
        drel = drel + bias_vjp((dbias, jnp.zeros_like(bias_t)))[0]
        dqs.append(dq)
        dks.append(dk)
        dvs.append(dv)
    dqkv = jnp.concatenate(dqs + dks + dvs, axis=1).astype(BF16)
    dw_in = mm(h, dqkv, name=f"{tag}_dwin", ta=True, tn=384)
    dh = mm(dqkv, w_in, name=f"{tag}_dh", tb=True, tk=1152)
    dx, dxb, dg = rms_bwd(x, g, dh, dx, f"{tag}_rmsb")
    return dx, dxb, dg, dw_in, dw_out, drel


def adamw(w, g, m, v, name):
    shape = w.shape
    last = shape[-1]
    w2, g2, m2, v2 = (t.reshape(-1, last) for t in (w, g, m, v))
    rows = w2.shape[0]
    tr = rows
    if rows > 512:
        tr = next(t for t in (512, 256, 192, 128, 64, 8) if rows % t == 0)
    c1 = 1.0 / (1.0 - ADAM_B1 ** ADAM_STEP)
    c2 = 1.0 / (1.0 - ADAM_B2 ** ADAM_STEP)

    def body(w_ref, g_ref, m_ref, v_ref, d_ref, nm_ref, nv_ref):
        gv = g_ref[...]
        nm = ADAM_B1 * m_ref[...] + (1.0 - ADAM_B1) * gv
        nv = ADAM_B2 * v_ref[...] + (1.0 - ADAM_B2) * (gv * gv)
        nm_ref[...] = nm
        nv_ref[...] = nv
        d_ref[...] = -ADAM_LR * ((nm * c1) / (jnp.sqrt(nv * c2) + ADAM_EPS) + ADAM_WD * w_ref[...])

    spec = BS((tr, last), lambda i: (i, 0))
    outs = pl.pallas_call(
        body, grid=(rows // tr,), in_specs=[spec] * 4, out_specs=[spec] * 3,
        out_shape=[SDS((rows, last), F32)] * 3, name=name, compiler_params=_cp(("parallel",)))(w2, g2, m2, v2)
    return tuple(o.reshape(shape) for o in outs)


def _place():
    x, y, c = lax.axis_index("x"), lax.axis_index("y"), lax.axis_index("c")
    chips = [(1 - x, y), (x, 1 - y), (1 - x, 1 - y)]
    return x, y, c, chips


def _rcopy(src, dst, ssem, rsem, dev):
    return pltpu.make_async_remote_copy(src_ref=src, dst_ref=dst, send_sem=ssem, recv_sem=rsem, device_id=dev,
                                        device_id_type=MESH)


class SideJob(NamedTuple):
    ins: list
    outs: list
    sems: list
    start: Callable
    wait: Callable


def _job(ins, outs, sems, copies):
    def start(in_refs, out_refs, sem_refs):
        for cp in copies(in_refs, out_refs, sem_refs):
            cp.start()

    def wait(in_refs, out_refs, sem_refs):
        for cp in copies(in_refs, out_refs, sem_refs):
            cp.wait()

    return SideJob(list(ins), list(outs), list(sems), start, wait)


def gather_job(packs, halved):
    n = len(packs)
    dma = pltpu.SemaphoreType.DMA

    def copies(in_refs, out_refs, sems):
        ssem, rsem = sems
        x, y, c, chips = _place()
        jme = 2 * x + y
        cps = []
        for i, (p_ref, f_ref) in enumerate(zip(in_refs, out_refs)):
            rows = p_ref.shape[0]
            mine = pl.ds(c * (rows // 2), rows // 2) if halved[i] else pl.ds(0, rows)
            for r, (cx, cy) in enumerate(chips):
                cps.append(_rcopy(p_ref.at[mine], f_ref.at[jme, mine], ssem.at[i, r], rsem.at[i, r], (cx, cy, c)))
        return cps

    return _job(packs, [SDS((4,) + p.shape, p.dtype) for p in packs], [dma((n, 3)), dma((n, 3))], copies)


def chip_exchange_job(parts):
    n = len(parts)
    dma = pltpu.SemaphoreType.DMA

    def copies(in_refs, out_refs, sems):
        ssem, rsem = sems
        x, y, c, chips = _place()
        cps = []
        for i, (p_ref, r_ref) in enumerate(zip(in_refs, out_refs)):
            for r, (cx, cy) in enumerate(chips):
                cps.append(_rcopy(p_ref.at[2 * cx + cy], r_ref.at[r], ssem.at[i, r], rsem.at[i, r], (cx, cy, c)))
        return cps

    return _job(parts, [SDS((3,) + p.shape[1:], p.dtype) for p in parts], [dma((n, 3)), dma((n, 3))], copies)


def run_job(job, name):
    ni, no = len(job.ins), len(job.outs)

    def body(*refs):
        job.start(refs[:ni], refs[ni:ni + no], refs[ni + no:])
        job.wait(refs[:ni], refs[ni:ni + no], refs[ni + no:])

    return pl.pallas_call(
        body, in_specs=[ANY] * ni, out_specs=[ANY] * no, out_shape=job.outs, scratch_shapes=job.sems, name=name,
        compiler_params=pltpu.CompilerParams(has_side_effects=True))(*job.ins)


def forward_to_sibling(fulls, name):
    n = len(fulls)

    def body(*refs):
        in_refs, out_refs, (ssem, rsem) = refs[:n], refs[n:2 * n], refs[2 * n:]
        x, y, c, chips = _place()
        cps = []
        for i in range(n):
            half = in_refs[i].shape[1] // 2
            for r, (cx, cy) in enumerate(chips):
                piece = (2 * cx + cy, pl.ds(c * half, half))
                cps.append(_rcopy(in_refs[i].at[piece], out_refs[i].at[piece], ssem.at[i, r], rsem.at[i, r], (x, y, 1 - c)))
        for cp in cps:
            cp.start()
        for cp in cps:
            cp.wait()

    dma = pltpu.SemaphoreType.DMA
    return pl.pallas_call(
        body, in_specs=[ANY] * n, out_specs=[ANY] * n, out_shape=[SDS(f.shape, f.dtype) for f in fulls],
        scratch_shapes=[dma((n, 3)), dma((n, 3))], input_output_aliases={i: i for i in range(n)}, name=name,
        compiler_params=pltpu.CompilerParams(has_side_effects=True))(*fulls)


def rs_sibling_exchange(gpack, name):
    _, rows, W = gpack.shape
    half = rows // 2

    def body(g_ref, r_ref, ssem, rsem):
        x, y, c, _ = _place()
        cps = [_rcopy(g_ref.at[j, pl.ds((1 - c) * half, half)], r_ref.at[j], ssem.at[j], rsem.at[j], (x, y, 1 - c))
               for j in range(4)]
        for cp in cps:
            cp.start()
        for cp in cps:
            cp.wait()

    dma = pltpu.SemaphoreType.DMA
    return pl.pallas_call(
        body, in_specs=[ANY], out_specs=ANY, out_shape=SDS((4, half, W), gpack.dtype),
        scratch_shapes=[dma((4,)), dma((4,))], name=name,
        compiler_params=pltpu.CompilerParams(has_side_effects=True))(gpack)


def _div_tile(n, limit):
    return next(t for t in range(limit - limit % 16, 0, -16) if n % t == 0)


def rs_add_sibling(gpack, recv, cidx, name, out_dtype=F32):
    _, rows, W = gpack.shape
    half = rows // 2
    tr = _div_tile(half, 1024)
    nb = half // tr

    def body(c_ref, g_ref, r_ref, o_ref):
        o_ref[...] = (g_ref[...] + r_ref[...]).astype(o_ref.dtype)

    gs = pltpu.PrefetchScalarGridSpec(
        num_scalar_prefetch=1, grid=(4, nb),
        in_specs=[BS((1, tr, W), lambda j, i, c: (j, c[0] * nb + i, 0)), BS((1, tr, W), lambda j, i, c: (j, i, 0))],
        out_specs=BS((1, tr, W), lambda j, i, c: (j, i, 0)))
    return pl.pallas_call(body, grid_spec=gs, out_shape=SDS((4, half, W), out_dtype), name=name,
                          compiler_params=_cp(("parallel", "parallel")))(cidx, gpack, recv)


def rs_add_chips(recv, part, place, name):
    _, half, W = recv.shape
    tr = _div_tile(half, 640)
    nb = half // tr

    def body(x_ref, y_ref, c_ref, r_ref, own_ref, o_ref):
        r0, r1, r2, own = (t.astype(F32) for t in (r_ref[0], r_ref[1], r_ref[2], own_ref[0]))
        o_ref[...] = ((r0 + r1) + r2) + own

    gs = pltpu.PrefetchScalarGridSpec(
        num_scalar_prefetch=3, grid=(nb,),
        in_specs=[BS((3, tr, W), lambda i, x, y, c: (0, i, 0)), BS((1, tr, W), lambda i, x, y, c: (2 * x[0] + y[0], i, 0))],
        out_specs=BS((tr, W), lambda i, x, y, c: (c[0] * nb + i, 0)))
    return pl.pallas_call(body, grid_spec=gs, out_shape=SDS((2 * half, W), F32), name=name,
                          compiler_params=_cp(("parallel",)))(*place, recv, part)


def rs_sibling_share(gsh, name):
    rows, W = gsh.shape
    half = rows // 2

    def body(g_ref, o_ref, ssem, rsem):
        x, y, c, _ = _place()
        mine = pl.ds(c * half, half)
        cp = _rcopy(g_ref.at[mine], o_ref.at[mine], ssem, rsem, (x, y, 1 - c))
        cp.start()
        cp.wait()

    dma = pltpu.SemaphoreType.DMA
    return pl.pallas_call(
        body, in_specs=[ANY], out_specs=ANY, out_shape=SDS(gsh.shape, gsh.dtype),
        scratch_shapes=[dma, dma], input_output_aliases={0: 0}, name=name,
        compiler_params=pltpu.CompilerParams(has_side_effects=True))(gsh)


def allreduce_small(pack):
    R = pack.shape[0]

    def body(p_ref, o_ref, all_ref, ssem, rsem):
        x, y, c, _ = _place()
        me = 4 * x + 2 * y + c
        all_ref[me] = p_ref[...]
        cps = []
        for m in range(1, 8):
            peer = (1 - x if m & 4 else x, 1 - y if m & 2 else y, 1 - c if m & 1 else c)
            cp = _rcopy(p_ref, all_ref.at[me], ssem.at[m - 1], rsem.at[m - 1], peer)
            cp.start()
            cps.append(cp)
        for cp in cps:
            cp.wait()
        acc = all_ref[0]
        for i in range(1, 8):
            acc = acc + all_ref[i]
        o_ref[...] = acc

    dma = pltpu.SemaphoreType.DMA
    vm = BS(memory_space=pltpu.VMEM)
    return pl.pallas_call(
        body, in_specs=[vm], out_specs=vm, out_shape=SDS(pack.shape, F32),
        scratch_shapes=[pltpu.VMEM((8, R, LANES), F32), dma((7,)), dma((7,))], name="allreduce_small",
        compiler_params=pltpu.CompilerParams(has_side_effects=True))(pack)


PACK_W = 1024
PACK_ALIGN = 32


def _layer_entries(l):
    if l % 2 == 0:
        mixer = [("gdn_w_in", l // 2, D_MODEL, GDN_IN // 4, True), ("gdn_w_out", l // 2, D_MODEL // 4, D_MODEL, False)]
    else:
        mixer = [("dswa_w_in", l // 2, D_MODEL, 3 * DSWA_W // 4, True), ("dswa_w_out", l // 2, DSWA_W // 4, D_MODEL, False)]
    return mixer + [("mlp_w1", l, D_MODEL, D_FF // 4, True), ("mlp_w2", l, D_FF // 4, D_MODEL, False)]


def _layer_offsets(l):
    offs = [int(o) for o in np.cumsum([0] + [r * c // PACK_W for (_, _, r, c, _) in _layer_entries(l)])]
    return offs, -(-offs[-1] // PACK_ALIGN) * PACK_ALIGN


def _pack_layer(l, shards, dtype):
    offs, total = _layer_offsets(l)
    parts = [shards[name][li].astype(dtype).reshape(-1, PACK_W) for (name, li, _, _, _) in _layer_entries(l)]
    parts.append(jnp.zeros((total - offs[-1], PACK_W), dtype))
    return jnp.concatenate(parts, axis=0)


def _unpack_layer(l, full, own, jme):
    offs, _ = _layer_offsets(l)
    mats = []
    for e, (_, _, r, c, by_col) in enumerate(_layer_entries(l)):
        mine = own[offs[e]:offs[e + 1]]
        sh = [jnp.where(jme == j, mine, full[j, offs[e]:offs[e + 1]]).reshape(r, c) for j in range(4)]
        mats.append(jnp.concatenate(sh, axis=1 if by_col else 0))
    return mats


def _pack_layer_grads(l, grads):
    offs, total = _layer_offsets(l)
    per_chip = []
    for j in range(4):
        parts = []
        for g, (_, _, r, c, by_col) in zip(grads, _layer_entries(l)):
            sh = g[:, c * j:c * (j + 1)] if by_col else g[r * j:r * (j + 1), :]
            parts.append(sh.reshape(-1, PACK_W))
        parts.append(jnp.zeros((total - offs[-1], PACK_W), F32))
        per_chip.append(jnp.concatenate(parts, axis=0))
    return jnp.stack(per_chip)


def _unpack_shard_grads(gshs):
    out = {}
    for l, gsh in enumerate(gshs):
        offs, _ = _layer_offsets(l)
        for e, (name, _, r, c, _) in enumerate(_layer_entries(l)):
            out.setdefault(name, []).append(gsh[offs[e]:offs[e + 1]].reshape(r, c))
    return {k: jnp.stack(v) for k, v in out.items()}


def _flat_pad(t, mult=8 * LANES):
    f = t.reshape(-1)
    return jnp.pad(f, (0, (-f.shape[0]) % mult))


def kernel(x, norm_mix, norm_mlp, norm_final, rel_bias, gdn_w_in, gdn_conv_w, gdn_a_log, gdn_dt_bias, gdn_norm_w, gdn_w_out, dswa_w_in, dswa_w_out, mlp_w1, mlp_w2, loss_target, m_norm_mix, m_norm_mlp, m_norm_final, m_rel_bias, m_gdn_w_in, m_gdn_conv_w, m_gdn_a_log, m_gdn_dt_bias, m_gdn_norm_w, m_gdn_w_out, m_dswa_w_in, m_dswa_w_out, m_mlp_w1, m_mlp_w2, v_norm_mix, v_norm_mlp, v_norm_final, v_rel_bias, v_gdn_w_in, v_gdn_conv_w, v_gdn_a_log, v_gdn_dt_bias, v_gdn_norm_w, v_gdn_w_out, v_dswa_w_in, v_dswa_w_out, v_mlp_w1, v_mlp_w2):
    xi, yi, ci = lax.axis_index("x"), lax.axis_index("y"), lax.axis_index("c")
    jme = 2 * xi + yi
    big = dict(gdn_w_in=gdn_w_in, gdn_w_out=gdn_w_out, dswa_w_in=dswa_w_in, dswa_w_out=dswa_w_out, mlp_w1=mlp_w1, mlp_w2=mlp_w2)
    n_gdn = gdn_w_in.shape[0]
    conv_cols = gdn_conv_w.shape[-1]

    packs = [_pack_layer(l, big, BF16) for l in range(DEPTH)]
    convp = jnp.pad(gdn_conv_w.reshape(n_gdn * GDN_CONV, conv_cols), ((0, 16 - n_gdn * GDN_CONV), (0, 0)))
    raw0, cfull = run_job(gather_job([packs[0], convp], [True, False]), "gather_l0")
    fulls = {0: forward_to_sibling([raw0], "forward_l0")[0]}
    cfull = jnp.where((jnp.arange(4) == jme)[:, None, None], convp[None], cfull)
    conv_all = jnp.transpose(cfull[:, :n_gdn * GDN_CONV], (1, 0, 2)).reshape(n_gdn, GDN_CONV, 4 * conv_cols)
    conv_all = jnp.pad(conv_all, ((0, 0), (0, 8 - GDN_CONV), (0, 0)))
    fwd_jobs = {0: [1, 2], 2: [3]}

    xs = x[0]
    saved = []
    for l in range(DEPTH):
        w_in, w_out, w1, w2 = _unpack_layer(l, fulls[l], packs[l], jme)
        gm, gp = norm_mix[l][None], norm_mlp[l][None]
        a = l // 2
        if l % 2 == 0:
            w_in = jnp.pad(w_in, ((0, 0), (0, GDN_INP - GDN_IN)))
            job = gather_job([packs[t] for t in fwd_jobs[l]], [True] * len(fwd_jobs[l]))
            x_mid, sv, raws = gdn_fwd(xs, gm, w_in, conv_all[a], gdn_a_log[a], gdn_dt_bias[a], gdn_norm_w[a][None], w_out,
                                      f"l{l}_gdn", job)
            for t, f in zip(fwd_jobs[l], forward_to_sibling(raws, f"forward_from_l{l}")):
                fulls[t] = f
        else:
            x_mid, sv = dswa_fwd(xs, gm, w_in, w_out, rel_bias, f"l{l}_att")
        x_out, sv2 = mlp_fwd(x_mid, gp, w1, w2, f"l{l}_mlp")
        saved.append((xs, x_mid, (w_in, w_out, w1, w2), sv, sv2))
        xs = x_out

    cidx = ci.astype(jnp.int32).reshape(1)
    place = [t.astype(jnp.int32).reshape(1) for t in (xi, yi, ci)]

    def chip_partial(l, grads4):
        gpack = _pack_layer_grads(l, grads4)
        return rs_add_sibling(gpack, rs_sibling_exchange(gpack, f"rs_sibling_l{l}"), cidx, f"rs_add_sibling_l{l}",
                              BF16 if l == 0 else F32)

    def finish(l, recv):
        return rs_sibling_share(rs_add_chips(recv, parts[l], place, f"rs_add_chips_l{l}"), f"rs_share_l{l}")

    loss_part, dx, dxb, d_final = loss_head(xs, norm_final[None], loss_target[0], "loss_head")
    d_mix, d_mlp = [None] * DEPTH, [None] * DEPTH
    d_conv, d_alog, d_dt, d_nw = [None] * n_gdn, [None] * n_gdn, [None] * n_gdn, [None] * n_gdn
    d_rel = jnp.zeros_like(rel_bias)
    parts, gshs = {}, [None] * DEPTH
    bwd_jobs = {2: [3], 0: [2, 1]}
    for l in reversed(range(DEPTH)):
        x_in, x_mid, (w_in, w_out, w1, w2), sv, sv2 = saved[l]
        gm, gp = norm_mix[l][None], norm_mlp[l][None]
        a = l // 2
        dx, dxb, d_mlp[l], dw1, dw2 = mlp_bwd(x_mid, gp, w1, w2, sv2, dx, dxb, f"l{l}_mlp")
        if l % 2 == 0:
            job = chip_exchange_job([parts[t] for t in bwd_jobs[l]])
            dx, dxb, d_mix[l], dw_all, d_conv[a], d_alog[a], d_dt[a], d_nw[a], dwo, recvs = gdn_bwd(
                x_in, gm, w_in, conv_all[a], gdn_norm_w[a][None], w_out, sv, dx, dxb, f"l{l}_gdn", job)
            for t, rv in zip(bwd_jobs[l], recvs):
                gshs[t] = finish(t, rv)
            dwi = dw_all[:, :GDN_IN]
        else:
            dx, dxb, d_mix[l], dwi, dwo, drel = dswa_bwd(x_in, gm, w_in, w_out, rel_bias, sv, dx, dxb, f"l{l}_att")
            d_rel = d_rel + drel
        parts[l] = chip_partial(l, [dwi, dwo, dw1, dw2])
    gshs[0] = finish(0, run_job(chip_exchange_job([parts[0]]), "rs_chip_exchange_l0")[0])
    gbig = _unpack_shard_grads(gshs)

    small = [jnp.concatenate(d_mix, axis=0), jnp.concatenate(d_mlp, axis=0), d_final, d_rel,
             jnp.stack(d_conv), jnp.stack(d_alog), jnp.stack(d_dt), jnp.concatenate(d_nw, axis=0)]
    flat = [_flat_pad(t) for t in small]
    sizes = [f.shape[0] for f in flat]
    red = allreduce_small(jnp.concatenate(flat).reshape(-1, LANES)).reshape(-1)
    offs = np.cumsum([0] + sizes)
    red = [red[offs[i]:offs[i] + small[i].size].reshape(small[i].shape) for i in range(len(small))]
    g_conv_all = red[4][:, :GDN_CONV].reshape(n_gdn, GDN_CONV, 1, 4 * conv_cols)
    g_conv = lax.dynamic_slice_in_dim(g_conv_all, jme * conv_cols, conv_cols, axis=3)
    g = dict(norm_mix=red[0], norm_mlp=red[1], norm_final=red[2].reshape(norm_final.shape), rel_bias=red[3],
             gdn_conv_w=g_conv, gdn_a_log=red[5], gdn_dt_bias=red[6], gdn_norm_w=red[7][:, :GDN_DK], **gbig)

    w = dict(norm_mix=norm_mix, norm_mlp=norm_mlp, norm_final=norm_final, rel_bias=rel_bias, gdn_conv_w=gdn_conv_w,
             gdn_a_log=gdn_a_log, gdn_dt_bias=gdn_dt_bias, gdn_norm_w=gdn_norm_w, **big)
    m = dict(norm_mix=m_norm_mix, norm_mlp=m_norm_mlp, norm_final=m_norm_final, rel_bias=m_rel_bias, gdn_w_in=m_gdn_w_in,
             gdn_conv_w=m_gdn_conv_w, gdn_a_log=m_gdn_a_log, gdn_dt_bias=m_gdn_dt_bias, gdn_norm_w=m_gdn_norm_w,
             gdn_w_out=m_gdn_w_out, dswa_w_in=m_dswa_w_in, dswa_w_out=m_dswa_w_out, mlp_w1=m_mlp_w1, mlp_w2=m_mlp_w2)
    v = dict(norm_mix=v_norm_mix, norm_mlp=v_norm_mlp, norm_final=v_norm_final, rel_bias=v_rel_bias, gdn_w_in=v_gdn_w_in,
             gdn_conv_w=v_gdn_conv_w, gdn_a_log=v_gdn_a_log, gdn_dt_bias=v_gdn_dt_bias, gdn_norm_w=v_gdn_norm_w,
             gdn_w_out=v_gdn_w_out, dswa_w_in=v_dswa_w_in, dswa_w_out=v_dswa_w_out, mlp_w1=v_mlp_w1, mlp_w2=v_mlp_w2)
    names = ["norm_mix", "norm_mlp", "norm_final", "rel_bias", "gdn_w_in", "gdn_conv_w", "gdn_a_log", "gdn_dt_bias",
             "gdn_norm_w", "gdn_w_out", "dswa_w_in", "dswa_w_out", "mlp_w1", "mlp_w2"]
    upd = {n: adamw(w[n], g[n], m[n], v[n], f"adamw_{n}") for n in names}
    loss = lax.psum(loss_part[0, 0], ("x", "y", "c"))
    return (loss, dx[None], *[g[n] for n in names], *[upd[n][0] for n in names], *[upd[n][1] for n in names],
            *[upd[n][2] for n in names])
```

```python
import math
from typing import Callable, NamedTuple

import numpy as np
import jax
import jax.numpy as jnp
from jax import lax
from jax.experimental import pallas as pl
from jax.experimental.pallas import tpu as pltpu

F32 = jnp.float32
BF16 = jnp.bfloat16
HI = lax.Precision.HIGHEST
BS = pl.BlockSpec
SDS = jax.ShapeDtypeStruct
MESH = pl.DeviceIdType.MESH
ANY = BS(memory_space=pl.ANY)

D_MODEL = 1024
D_FF = 4096
DEPTH = 4
RMS_EPS = 1e-6
NEG_INF = -1e30
LANES = 128
VMEM_LIMIT = 56 << 20

GDN_H = 8
GDN_DK = 128
GDN_CONV = 5
GDN_C = 64
GDN_GC = 8
GDN_HP_FWD = 8
GDN_HP_BWD = 4
GDN_QKV = 3 * GDN_H * GDN_DK
GDN_IN = GDN_QKV + GDN_H * GDN_DK + 4 * GDN_H
GDN_INP = 4224

DSWA_CFG = ((128, 1), (512, 4), (2048, 16))
DSWA_HG = 6
DSWA_E = 64
DSWA_HEADS = 18
DSWA_W = DSWA_HEADS * DSWA_E
DSWA_HALF = 64
REL_BUCKETS = 32
REL_MAX_DIST = 1024

ADAM_LR = 0.001
ADAM_B1 = 0.9
ADAM_B2 = 0.999
ADAM_EPS = 1e-08
ADAM_WD = 0.01
ADAM_STEP = 10


def _cp(sem=None):
    return pltpu.CompilerParams(dimension_semantics=sem, vmem_limit_bytes=VMEM_LIMIT)


def _dot(a, b, prec=None):
    return jnp.dot(a, b, precision=prec, preferred_element_type=F32)


def _dot_nt(a, b, prec=None):
    return lax.dot_general(a, b, (((1,), (1,)), ((), ())), precision=prec, preferred_element_type=F32)


def _dot_tn(a, b, prec=None):
    return lax.dot_general(a, b, (((0,), (0,)), ((), ())), precision=prec, preferred_element_type=F32)


def _bf(a):
    return a.astype(BF16)


def _sigmoid(x):
    return 1.0 / (1.0 + jnp.exp(-x))


def rms_fwd(x, g, name):
    S, Dm = x.shape
    tm = min(512, S)

    def body(x_ref, g_ref, o_ref):
        xv = x_ref[...]
        r = lax.rsqrt(jnp.mean(xv * xv, axis=-1, keepdims=True) + RMS_EPS)
        o_ref[...] = (xv * r * g_ref[...]).astype(o_ref.dtype)

    return pl.pallas_call(
        body, grid=(S // tm,),
        in_specs=[BS((tm, Dm), lambda i: (i, 0)), BS((1, Dm), lambda i: (0, 0))],
        out_specs=BS((tm, Dm), lambda i: (i, 0)),
        out_shape=SDS((S, Dm), BF16), name=name, compiler_params=_cp(("parallel",)))(x, g)


def rms_bwd(x, g, dh, dres, name):
    S, Dm = x.shape
    tm = min(512, S)

    def body(x_ref, g_ref, dh_ref, dres_ref, dx_ref, dxb_ref, dg_ref):
        i = pl.program_id(0)
        xv = x_ref[...]
        r = lax.rsqrt(jnp.mean(xv * xv, axis=-1, keepdims=True) + RMS_EPS)
        n = xv * r
        dhv = dh_ref[...]
        t = dhv * g_ref[...]
        dx = dres_ref[...] + r * (t - n * jnp.mean(n * t, axis=-1, keepdims=True))
        dx_ref[...] = dx
        dxb_ref[...] = dx.astype(BF16)
        part = jnp.sum(dhv * n, axis=0, keepdims=True)

        @pl.when(i == 0)
        def _():
            dg_ref[...] = part

        @pl.when(i > 0)
        def _():
            dg_ref[...] += part

    row = BS((tm, Dm), lambda i: (i, 0))
    vec = BS((1, Dm), lambda i: (0, 0))
    return pl.pallas_call(
        body, grid=(S // tm,), in_specs=[row, vec, row, row], out_specs=[row, row, vec],
        out_shape=[SDS((S, Dm), F32), SDS((S, Dm), BF16), SDS((1, Dm), F32)],
        name=name, compiler_params=_cp(("arbitrary",)))(x, g, dh, dres)


def loss_head(x, g, tgt, name):
    S, Dm = x.shape
    tm = min(512, S)

    def body(x_ref, g_ref, t_ref, loss_ref, dx_ref, dxb_ref, dg_ref):
        i = pl.program_id(0)
        xv = x_ref[...]
        gv = g_ref[...]
        r = lax.rsqrt(jnp.mean(xv * xv, axis=-1, keepdims=True) + RMS_EPS)
        n = xv * r
        err = n * gv - t_ref[...]
        lpart = 0.5 * jnp.sum(jnp.mean(err * err, axis=-1, keepdims=True), axis=0, keepdims=True)
        dout = err * (1.0 / Dm)
        t = dout * gv
        dx = r * (t - n * jnp.mean(n * t, axis=-1, keepdims=True))
        dx_ref[...] = dx
        dxb_ref[...] = dx.astype(BF16)
        part = jnp.sum(dout * n, axis=0, keepdims=True)

        @pl.when(i == 0)
        def _():
            dg_ref[...] = part
            loss_ref[...] = lpart

        @pl.when(i > 0)
        def _():
            dg_ref[...] += part
            loss_ref[...] += lpart

    row = BS((tm, Dm), lambda i: (i, 0))
    vec = BS((1, Dm), lambda i: (0, 0))
    one = BS((1, 1), lambda i: (0, 0))
    return pl.pallas_call(
        body, grid=(S // tm,), in_specs=[row, vec, row], out_specs=[one, row, row, vec],
        out_shape=[SDS((1, 1), F32), SDS((S, Dm), F32), SDS((S, Dm), BF16), SDS((1, Dm), F32)],
        name=name, compiler_params=_cp(("arbitrary",)))(x, g, tgt)


MM_TK = 2048


def mm(a, b, *, name, ta=False, tb=False, tm=1024, tn=512, tk=None, out_dtype=F32, pre_a=None, epi=None,
       extras=()):
    M, K = (a.shape[1], a.shape[0]) if ta else a.shape
    N = b.shape[0] if tb else b.shape[1]
    tm, tn = min(tm, M), min(tn, N)
    if tk is None:
        tk = MM_TK if K % MM_TK == 0 else K
    tk = min(tk, K)
    assert M % tm == 0 and N % tn == 0 and K % tk == 0, (name, M, N, K, tm, tn, tk)
    nk = K // tk
    ne = len(extras)
    a_spec = BS((tk, tm), lambda i, j, k: (k, i)) if ta else BS((tm, tk), lambda i, j, k: (i, k))
    b_spec = BS((tn, tk), lambda i, j, k: (j, k)) if tb else BS((tk, tn), lambda i, j, k: (k, j))
    o_spec = BS((tm, tn), lambda i, j, k: (i, j))
    dims = (((0 if ta else 1,), (1 if tb else 0,)), ((), ()))

    def body(a_ref, b_ref, *rest):
        e_refs, o_ref = rest[:ne], rest[ne]
        av = a_ref[...]
        if pre_a is not None:
            av = pre_a(av)
        p = lax.dot_general(_bf(av), _bf(b_ref[...]), dims, preferred_element_type=F32)

        def finish(acc):
            res = epi(acc, *[e[...] for e in e_refs]) if epi is not None else acc
            o_ref[...] = res.astype(o_ref.dtype)

        if nk == 1:
            finish(p)
        else:
            acc_ref = rest[ne + 1]
            k = pl.program_id(2)

            @pl.when(k == 0)
            def _():
                acc_ref[...] = p

            @pl.when(k > 0)
            def _():
                acc_ref[...] += p

            @pl.when(k == nk - 1)
            def _():
                finish(acc_ref[...])

    return pl.pallas_call(
        body, grid=(M // tm, N // tn, nk), in_specs=[a_spec, b_spec] + [o_spec] * ne, out_specs=o_spec,
        out_shape=SDS((M, N), out_dtype),
        scratch_shapes=[pltpu.VMEM((tm, tn), F32)] if nk > 1 else [],
        name=name, compiler_params=_cp(("parallel", "parallel", "arbitrary")))(a, b, *extras)


def _relu(acc):
    return jnp.maximum(acc, 0.0)


def _add(acc, res):
    return acc + res


def _sq(av):
    return av * av


def _times_2r(acc, r):
    return acc * (2.0 * r.astype(F32))


def mlp_fwd(x, g, w1, w2, tag):
    h = rms_fwd(x, g, f"{tag}_rms")
    r = mm(h, w1, name=f"{tag}_up", tn=1024, out_dtype=BF16, epi=_relu)
    xn = mm(r, w2, name=f"{tag}_down", pre_a=_sq, epi=_add, extras=(x,))
    return xn, (h, r)


def mlp_bwd(x, g, w1, w2, saved, dx, dxb, tag):
    h, r = saved
    da = mm(dxb, w2, name=f"{tag}_dact", tb=True, tn=1024, out_dtype=BF16, epi=_times_2r, extras=(r,))
    dw2 = mm(r, dxb, name=f"{tag}_dw2", ta=True, pre_a=_sq)
    dw1 = mm(h, da, name=f"{tag}_dw1", ta=True)
    dh = mm(da, w1, name=f"{tag}_dh", tb=True)
    dx, dxb, dg = rms_bwd(x, g, dh, dx, f"{tag}_rmsb")
    return dx, dxb, dg, dw1, dw2


def _conv_taps(x, S):
    t = lax.broadcasted_iota(jnp.int32, x.shape, 0)
    taps = []
    for j in range(GDN_CONV):
        sh = j - GDN_CONV // 2
        xs = x if sh == 0 else pltpu.roll(x, (-sh) % S, 0)
        taps.append(jnp.where((t + sh >= 0) & (t + sh < S), xs, 0.0))
    return taps


def _qkv_scale(c):
    is_norm = c < 2 * GDN_H
    scale = jnp.where(c < GDN_H, GDN_DK ** -0.5, 1.0)
    return is_norm, scale


def gdn_pre_fwd(proj, convw, name):
    S = proj.shape[0]

    def body(p_ref, w_ref, o_ref):
        c = pl.program_id(0)
        x = p_ref[...]
        w = w_ref[...]
        y = jnp.zeros_like(x)
        for j, xs in enumerate(_conv_taps(x, S)):
            y = y + w[j:j + 1, :] * xs
        t = y * _sigmoid(y)
        is_norm, scale = _qkv_scale(c)
        r = lax.rsqrt(jnp.sum(t * t, axis=-1, keepdims=True) + 1e-6)
        o_ref[...] = jnp.where(is_norm, t * r * scale, t)

    return pl.pallas_call(
        body, grid=(GDN_QKV // LANES,),
        in_specs=[BS((S, LANES), lambda c: (0, c)), BS((8, LANES), lambda c: (0, c))],
        out_specs=BS((S, LANES), lambda c: (0, c)),
        out_shape=SDS((S, GDN_QKV), F32), name=name, compiler_params=_cp(("parallel",)))(proj, convw)


def gdn_pre_bwd(proj, convw, dqkv, name):
    S = proj.shape[0]

    def body(p_ref, w_ref, d_ref, dp_ref, dw_ref):
        c = pl.program_id(0)
        x = p_ref[...]
        w = w_ref[...]
        taps = _conv_taps(x, S)
        y = jnp.zeros_like(x)
        for j, xs in enumerate(taps):
            y = y + w[j:j + 1, :] * xs
        sg = _sigmoid(y)
        t = y * sg
        is_norm, scale = _qkv_scale(c)
        dout = d_ref[0, 0] + d_ref[1, 0]
        r = lax.rsqrt(jnp.sum(t * t, axis=-1, keepdims=True) + 1e-6)
        n = t * r
        dn = dout * scale
        dt_norm = r * (dn - n * jnp.sum(dn * n, axis=-1, keepdims=True))
        dt = jnp.where(is_norm, dt_norm, dout)
        dy = dt * (sg * (1.0 + y * (1.0 - sg)))
        row = lax.broadcasted_iota(jnp.int32, (8, LANES), 0)
        dw = jnp.zeros((8, LANES), F32)
        for j, xs in enumerate(taps):
            dw = dw + jnp.where(row == j, jnp.sum(dy * xs, axis=0, keepdims=True), 0.0)
        dw_ref[...] = dw
        tt = lax.broadcasted_iota(jnp.int32, x.shape, 0)
        dx = jnp.zeros_like(x)
        for j in range(GDN_CONV):
            sh = j - GDN_CONV // 2
            ds = dy if sh == 0 else pltpu.roll(dy, sh % S, 0)
            dx = dx + w[j:j + 1, :] * jnp.where((tt - sh >= 0) & (tt - sh < S), ds, 0.0)
        dp_ref[...] = dx.astype(BF16)

    return pl.pallas_call(
        body, grid=(GDN_QKV // LANES,),
        in_specs=[BS((S, LANES), lambda c: (0, c)), BS((8, LANES), lambda c: (0, c)),
                  BS((2, 1, S, LANES), lambda c: (0, c // GDN_H, 0, c % GDN_H))],
        out_specs=[BS((S, LANES), lambda c: (0, c)), BS((8, LANES), lambda c: (0, c))],
        out_shape=[SDS((S, GDN_QKV), BF16), SDS((8, GDN_QKV), F32)],
        name=name, compiler_params=_cp(("parallel",)))(proj, convw, dqkv)


def _chunk_sum_matrix(n, upper):
    i = lax.broadcasted_iota(jnp.int32, (n, n), 0)
    j = lax.broadcasted_iota(jnp.int32, (n, n), 1)
    same = (i // GDN_C) == (j // GDN_C)
    tri = (i <= j) if upper else (i >= j)
    return jnp.where(same & tri, 1.0, 0.0).astype(F32)


def _gate_lanes(shape):
    lane = lax.broadcasted_iota(jnp.int32, shape, 1)
    return lane < GDN_H, (lane >= GDN_H) & (lane < 2 * GDN_H), (lane >= 2 * GDN_H) & (lane < 4 * GDN_H)


def gdn_gate_fwd(proj, prm, name):
    S = proj.shape[0]
    tm = min(512, S)
    ct = GDN_INP // LANES - 1

    def body(p_ref, prm_ref, o_ref):
        ab = p_ref[...]
        a_log = prm_ref[0:1, :]
        dtb = prm_ref[1:2, :]
        z = ab + dtb
        sp = jnp.maximum(z, 0.0) + jnp.log(1.0 + jnp.exp(-jnp.abs(z)))
        g = -jnp.exp(a_log) * sp
        is_f, is_b, is_beta = _gate_lanes(ab.shape)
        gf = _dot(_chunk_sum_matrix(tm, False), jnp.where(is_f, g, 0.0), HI)
        gbk = _dot(_chunk_sum_matrix(tm, True), jnp.where(is_b, g, 0.0), HI)
        o_ref[...] = gf + gbk + jnp.where(is_beta, _sigmoid(ab), 0.0)

    return pl.pallas_call(
        body, grid=(S // tm,),
        in_specs=[BS((tm, LANES), lambda i: (i, ct)), BS((8, LANES), lambda i: (0, 0))],
        out_specs=BS((tm, LANES), lambda i: (i, 0)),
        out_shape=SDS((S, LANES), F32), name=name, compiler_params=_cp(("parallel",)))(proj, prm)


def gdn_gate_bwd(proj, prm, dgb, name):
    S = proj.shape[0]
    tm = min(512, S)
    ct = GDN_INP // LANES - 1

    def body(p_ref, prm_ref, d_ref, dab_ref, dprm_ref):
        i = pl.program_id(0)
        ab = p_ref[...]
        a_log = prm_ref[0:1, :]
        dtb = prm_ref[1:2, :]
        z = ab + dtb
        sp = jnp.maximum(z, 0.0) + jnp.log(1.0 + jnp.exp(-jnp.abs(z)))
        ea = jnp.exp(a_log)
        g = -ea * sp
        is_f, is_b, is_beta = _gate_lanes(ab.shape)
        d = d_ref[...]
        dg = (_dot_tn(_chunk_sum_matrix(tm, False), jnp.where(is_f, d, 0.0), HI)
              + _dot_tn(_chunk_sum_matrix(tm, True), jnp.where(is_b, d, 0.0), HI))
        da = dg * (-ea) * _sigmoid(z)
        beta = _sigmoid(ab)
        dab_ref[...] = jnp.where(is_beta, d * beta * (1.0 - beta), da).astype(BF16)
        row = lax.broadcasted_iota(jnp.int32, (8, LANES), 0)
        part = (jnp.where(row == 0, jnp.sum(dg * g, axis=0, keepdims=True), 0.0)
                + jnp.where(row == 1, jnp.sum(da, axis=0, keepdims=True), 0.0))

        @pl.when(i == 0)
        def _():
            dprm_ref[...] = part

        @pl.when(i > 0)
        def _():
            dprm_ref[...] += part

    return pl.pallas_call(
        body, grid=(S // tm,),
        in_specs=[BS((tm, LANES), lambda i: (i, ct)), BS((8, LANES), lambda i: (0, 0)), BS((tm, LANES), lambda i: (i, 0))],
        out_specs=[BS((tm, LANES), lambda i: (i, 0)), BS((8, LANES), lambda i: (0, 0))],
        out_shape=[SDS((S, LANES), BF16), SDS((8, LANES), F32)],
        name=name, compiler_params=_cp(("arbitrary",)))(proj, prm, dgb)


def _tri_masks(d):
    i = lax.broadcasted_iota(jnp.int32, (GDN_C, GDN_C), 0)
    j = lax.broadcasted_iota(jnp.int32, (GDN_C, GDN_C), 1)
    s = (i - j) * (1 - 2 * d)
    return s >= 0, s > 0


def _split(a):
    hi = _bf(a)
    return hi, _bf(a - hi.astype(F32))


def _dot3(a, b):
    return _dot(a[0], b[0]) + (_dot(a[0], b[1]) + _dot(a[1], b[0]))


def _inv_unit_tri_many(mats):
    i = lax.broadcasted_iota(jnp.int32, mats[0].shape, 0)
    j = lax.broadcasted_iota(jnp.int32, mats[0].shape, 1)
    eye = jnp.where(i == j, 1.0, 0.0)
    ms = [-a for a in mats]
    ps = [eye + m for m in ms]
    for _ in range(int(math.log2(GDN_C)) - 1):
        sp = [_split(m) for m in ms]
        ms = [_dot3(s, s) for s in sp]
        sp = [_split(m) for m in ms]
        pp = [_split(p) for p in ps]
        ps = [p + _dot3(a, b) for p, a, b in zip(ps, pp, sp)]
    return ps


def _lane_col(x, lane_idx):
    lane = lax.broadcasted_iota(jnp.int32, x.shape, 1)
    return jnp.sum(jnp.where(lane == lane_idx, x, 0.0), axis=1, keepdims=True)


def _chunk_gates(gb_ref, grow_ref, hh, ci, d, head):
    gbv = gb_ref[ci * GDN_C:(ci + 1) * GDN_C, :]
    gcol = _lane_col(gbv, d * GDN_H + head)
    bcol = _lane_col(gbv, 2 * GDN_H + d * GDN_H + head)
    glast = jnp.where(d == 0, gcol[GDN_C - 1:GDN_C, :], gcol[0:1, :])
    return gcol, bcol, grow_ref[hh, ci:ci + 1, :], glast


def _chunk_base(q, k, gcol, grow, bcol, glast, d):
    incl, strict = _tri_masks(d)
    decay = jnp.where(incl, jnp.exp(jnp.where(incl, gcol - grow, 0.0)), 0.0)
    kb = k * bcol
    kk = _dot_nt(_bf(kb), _bf(k))
    qk = _dot_nt(_bf(q), _bf(k))
    eg = jnp.exp(gcol)
    ek = jnp.exp(glast - gcol)
    return dict(incl=incl, strict=strict, decay=decay, kb=kb, kk=kk, qk=qk, eg=eg, ek=ek, q_dec=q * eg, k_dec=k * ek,
                bcol=bcol, glast=glast)


def _block_terms(q_ref, k_ref, v_ref, gb_ref, grow_ref, d, h, hp):
    keys = [(hh, ci) for hh in range(hp) for ci in range(GDN_GC)]
    ts = []
    for hh, ci in keys:
        rows = slice(ci * GDN_C, (ci + 1) * GDN_C)
        cols = slice(hh * GDN_DK, (hh + 1) * GDN_DK)
        gcol, bcol, grow_v, glast = _chunk_gates(gb_ref, grow_ref, hh, ci, d, h * hp + hh)
        t = _chunk_base(q_ref[rows, cols], k_ref[rows, cols], gcol, grow_v, bcol, glast, d)
        t["v"] = v_ref[rows, cols]
        ts.append(t)
    tinvs = _inv_unit_tri_many([jnp.where(t["strict"], t["kk"] * t["decay"], 0.0) for t in ts])
    sp = [_split(x) for x in tinvs]
    us = [_dot3(s, _split(t["v"] * t["bcol"])) for s, t in zip(sp, ts)]
    ws = [_dot3(s, _split(t["kb"] * t["eg"])) for s, t in zip(sp, ts)]
    for t, tinv, u, w in zip(ts, tinvs, us, ws):
        t.update(tinv=tinv, u=u, w=w)
    return keys, ts


def _gdn_specs(S, nblk, order, hp):
    R = GDN_GC * GDN_C
    wd = hp * GDN_DK
    hb = GDN_H // hp

    def qkv_spec(part):
        return BS((R, wd), lambda d, h, n: (order(d, n), part * hb + h))

    gb_spec = BS((R, LANES), lambda d, h, n: (order(d, n), 0))
    grow_spec = BS((hp, GDN_GC, GDN_C), lambda d, h, n: (d * hb + h, order(d, n), 0))
    st_spec = BS((1, hp, GDN_GC, GDN_DK, GDN_DK), lambda d, h, n: (d, h, order(d, n), 0, 0))
    return qkv_spec, gb_spec, grow_spec, st_spec


def _lane_row(x):
    return jnp.broadcast_to(x, (1, LANES))


def _side_parts(side):
    if side is None:
        return [], [], [], [], []
    return [ANY] * len(side.ins), [ANY] * len(side.outs), list(side.outs), list(side.sems), list(side.ins)


def _side_run(side, refs, n_in, n_out, n_scr, first, last):
    if side is None:
        return
    ns, no, nm = len(side.ins), len(side.outs), len(side.sems)
    s_in = refs[n_in:n_in + ns]
    s_out = refs[n_in + ns + n_out:n_in + ns + n_out + no]
    s_sem = refs[len(refs) - nm:]

    @pl.when(first)
    def _():
        side.start(s_in, s_out, s_sem)

    @pl.when(last)
    def _():
        side.wait(s_in, s_out, s_sem)


def gdn_scan_fwd(qkv, gb, grow, name, side=None):
    S = qkv.shape[0]
    R = GDN_GC * GDN_C
    nblk = S // R
    nc = S // GDN_C
    hp = GDN_HP_FWD
    wd = hp * GDN_DK
    heads = range(hp)

    def order(d, n):
        return n + d * (nblk - 1 - 2 * n)

    qkv_spec, gb_spec, grow_spec, st_spec = _gdn_specs(S, nblk, order, hp)

    s_in, s_out, s_shape, s_scr_shapes, s_ops = _side_parts(side)
    hb = GDN_H // hp

    def body(*refs):
        q_ref, k_ref, v_ref, gb_ref, grow_ref = refs[:5]
        o_ref, st_ref = refs[5 + len(s_in):7 + len(s_in)]
        s_scr, u_scr, w_scr, qd_scr, kd_scr, in_scr, egl_scr = refs[7 + len(s_in) + len(s_out):14 + len(s_in) + len(s_out)]
        d = pl.program_id(0)
        h = pl.program_id(1)
        n = pl.program_id(2)
        _side_run(side, refs, 5, 2, 7, (d == 0) & (h == 0) & (n == 0), (d == 1) & (h == hb - 1) & (n == nblk - 1))

        @pl.when(n == 0)
        def _():
            s_scr[...] = jnp.zeros_like(s_scr)

        keys, ts = _block_terms(q_ref, k_ref, v_ref, gb_ref, grow_ref, d, h, hp)
        for (hh, ci), t in zip(keys, ts):
            u_scr[hh, ci] = t["u"]
            w_scr[hh, ci] = _bf(t["w"])
            qd_scr[hh, ci] = _bf(t["q_dec"])
            kd_scr[hh, ci] = _bf(t["k_dec"])
            in_scr[hh, ci] = _bf(jnp.where(t["incl"], t["qk"] * t["decay"], 0.0))
            egl_scr[hh, ci] = _lane_row(jnp.exp(t["glast"]))

        def chunk(cc, carry):
            ci = cc + d * (GDN_GC - 1 - 2 * cc)
            rows = pl.ds(pl.multiple_of(ci * GDN_C, GDN_C), GDN_C)
            sts = [s_scr[hh] for hh in heads]
            for hh in heads:
                st_ref[0, hh, ci] = sts[hh]
            sbs = [_bf(st) for st in sts]
            vns = [_bf(u_scr[hh, ci] - _dot(w_scr[hh, ci], sbs[hh])) for hh in heads]
            for hh in heads:
                s_scr[hh] = sts[hh] * egl_scr[hh, ci] + _dot_tn(kd_scr[hh, ci], vns[hh])
            for hh in heads:
                o_ref[0, rows, hh * GDN_DK:(hh + 1) * GDN_DK] = _dot(qd_scr[hh, ci], sbs[hh]) + _dot(in_scr[hh, ci], vns[hh])
            return carry

        lax.fori_loop(0, GDN_GC, chunk, 0)

    blk = (hp, GDN_GC, GDN_C, GDN_DK)
    return pl.pallas_call(
        body, grid=(2, GDN_H // hp, nblk),
        in_specs=[qkv_spec(0), qkv_spec(1), qkv_spec(2), gb_spec, grow_spec] + s_in,
        out_specs=[BS((1, R, wd), lambda d, h, n: (d, order(d, n), h)), st_spec] + s_out,
        out_shape=[SDS((2, S, GDN_H * GDN_DK), F32), SDS((2, GDN_H, nc, GDN_DK, GDN_DK), F32)] + s_shape,
        scratch_shapes=[pltpu.VMEM((hp, GDN_DK, GDN_DK), F32), pltpu.VMEM(blk, F32), pltpu.VMEM(blk, BF16),
                        pltpu.VMEM(blk, BF16), pltpu.VMEM(blk, BF16), pltpu.VMEM((hp, GDN_GC, GDN_C, GDN_C), BF16),
                        pltpu.VMEM((hp, GDN_GC, 1, LANES), F32)] + s_scr_shapes,
        name=name, compiler_params=_cp(("arbitrary", "arbitrary", "arbitrary")))(qkv, qkv, qkv, gb, grow, *s_ops)


def gdn_scan_bwd(qkv, gb, grow, states, do, name, side=None):
    S = qkv.shape[0]
    R = GDN_GC * GDN_C
    nblk = S // R
    hp = GDN_HP_BWD
    wd = hp * GDN_DK
    heads = range(hp)

    def order(d, n):
        return (nblk - 1 - n) - d * (nblk - 1 - 2 * n)

    qkv_spec, gb_spec, grow_spec, st_spec = _gdn_specs(S, nblk, order, hp)

    s_in, s_out, s_shape, s_scr_shapes, s_ops = _side_parts(side)
    hb = GDN_H // hp

    def body(*refs):
        q_ref, k_ref, v_ref, gb_ref, grow_ref, st_ref, do_ref = refs[:7]
        dqkv_ref, dgate_ref = refs[7 + len(s_in):9 + len(s_in)]
        (ds_scr, w_scr, kd_scr, dv1_scr, qtdo_scr, egl_scr, dsin_scr, dvn_scr,
         sdot_scr) = refs[9 + len(s_in) + len(s_out):18 + len(s_in) + len(s_out)]
        d = pl.program_id(0)
        h = pl.program_id(1)
        n = pl.program_id(2)
        _side_run(side, refs, 7, 2, 9, (d == 0) & (h == 0) & (n == 0), (d == 1) & (h == hb - 1) & (n == nblk - 1))

        @pl.when(n == 0)
        def _():
            ds_scr[...] = jnp.zeros_like(ds_scr)

        keys, ts = _block_terms(q_ref, k_ref, v_ref, gb_ref, grow_ref, d, h, hp)
        for (hh, ci), t in zip(keys, ts):
            rows = slice(ci * GDN_C, (ci + 1) * GDN_C)
            t["wb"] = _bf(t["w"])
            t["dob"] = _bf(do_ref[rows, hh * GDN_DK:(hh + 1) * GDN_DK])
            t["sb"] = _bf(st_ref[0, hh, ci])
        for (hh, ci), t in zip(keys, ts):
            t["vnb"] = _bf(t["u"] - _dot(t["wb"], t["sb"]))
            w_scr[hh, ci] = t["wb"]
            kd_scr[hh, ci] = _bf(t["k_dec"])
            dv1_scr[hh, ci] = _dot_tn(_bf(jnp.where(t["incl"], t["qk"] * t["decay"], 0.0)), t["dob"])
            qtdo_scr[hh, ci] = _dot_tn(_bf(t["q_dec"]), t["dob"])
            egl_scr[hh, ci] = _lane_row(jnp.exp(t["glast"]))

        def chunk(cc, carry):
            ci = (GDN_GC - 1 - cc) - d * (GDN_GC - 1 - 2 * cc)
            dsns = [ds_scr[hh] for hh in heads]
            dsbs = [_bf(x) for x in dsns]
            dvns = [dv1_scr[hh, ci] + _dot(kd_scr[hh, ci], dsbs[hh]) for hh in heads]
            for hh in heads:
                ds_scr[hh] = qtdo_scr[hh, ci] + egl_scr[hh, ci] * dsns[hh] - _dot_tn(w_scr[hh, ci], _bf(dvns[hh]))
            for hh in heads:
                dsin_scr[hh, ci] = dsbs[hh]
                dvn_scr[hh, ci] = dvns[hh]
                sd = jnp.sum(jnp.sum(st_ref[0, hh, ci] * dsns[hh], axis=1, keepdims=True), axis=0, keepdims=True)
                sdot_scr[hh, ci] = _lane_row(sd)
            return carry

        lax.fori_loop(0, GDN_GC, chunk, 0)

        for (hh, ci), t in zip(keys, ts):
            t["d_vnew"] = dvn_scr[hh, ci]
            t["dvb"] = _bf(t["d_vnew"])
            t["dsb"] = dsin_scr[hh, ci]
        for t in ts:
            t["d_intra"] = jnp.where(t["incl"], _dot_nt(t["dob"], t["vnb"]), 0.0)
            t["d_qdec"] = _dot_nt(t["dob"], t["sb"])
            t["d_kdec"] = _dot_nt(t["vnb"], t["dsb"])
            t["dw"] = -_dot_nt(t["dvb"], t["sb"])
        for t in ts:
            tts = _split(t["tinv"].T)
            t["d_ru"] = _dot3(tts, _split(t["d_vnew"]))
            t["d_rw"] = _dot3(tts, _split(t["dw"]))
        for t in ts:
            t["da"] = -jnp.where(t["strict"], _dot_nt(_bf(t["d_ru"]), _bf(t["u"])) + _dot_nt(_bf(t["d_rw"]), t["wb"]), 0.0)
        for (hh, ci), t in zip(keys, ts):
            rows = slice(ci * GDN_C, (ci + 1) * GDN_C)
            cols = slice(hh * GDN_DK, (hh + 1) * GDN_DK)
            q, k, v = q_ref[rows, cols], k_ref[rows, cols], t["v"]
            decay, kb, eg, ek, bcol = t["decay"], t["kb"], t["eg"], t["ek"], t["bcol"]
            d_ru, d_rw, da, d_intra, d_qdec, d_kdec = t["d_ru"], t["d_rw"], t["da"], t["d_intra"], t["d_qdec"], t["d_kdec"]
            kbf, qbf = _bf(k), _bf(q)
            dgl = egl_scr[hh, ci][:, 0:1] * sdot_scr[hh, ci][:, 0:1]
            dv = d_ru * bcol
            dbeta = jnp.sum(d_ru * v, axis=1, keepdims=True)
            dkb = d_rw * eg
            dg = jnp.sum(d_rw * kb, axis=1, keepdims=True) * eg
            dkk = _bf(da * decay)
            dqk = _bf(d_intra * decay)
            dkb = dkb + _dot(dkk, kbf)
            dk = _dot_tn(dkk, _bf(kb)) + _dot_tn(dqk, qbf)
            dq = _dot(dqk, kbf) + d_qdec * eg
            dd = (da * t["kk"] + d_intra * t["qk"]) * decay
            dg = dg + jnp.sum(dd, axis=1, keepdims=True) - jnp.sum(dd.T, axis=1, keepdims=True)
            dg = dg + jnp.sum(d_qdec * t["q_dec"], axis=1, keepdims=True)
            dk = dk + d_kdec * ek
            ee = jnp.sum(d_kdec * t["k_dec"], axis=1, keepdims=True)
            dg = dg - ee
            dgl = dgl + jnp.sum(ee, axis=0, keepdims=True)
            dk = dk + dkb * bcol
            dbeta = dbeta + jnp.sum(dkb * k, axis=1, keepdims=True)
            ridx = lax.broadcasted_iota(jnp.int32, (GDN_C, 1), 0)
            dg = dg + jnp.where(ridx == (GDN_C - 1) * (1 - d), dgl, 0.0)
            dqkv_ref[0, 0, rows, cols] = dq
            dqkv_ref[0, 1, rows, cols] = dk
            dqkv_ref[0, 2, rows, cols] = dv
            lane2 = lax.broadcasted_iota(jnp.int32, (GDN_C, 2), 1)
            dgate_ref[0, hh, rows, :] = jnp.where(lane2 == 0, dg, dbeta)

    blk = (hp, GDN_GC, GDN_C, GDN_DK)
    sq = (hp, GDN_GC, GDN_DK, GDN_DK)
    row = (hp, GDN_GC, 1, LANES)
    return pl.pallas_call(
        body, grid=(2, GDN_H // hp, nblk),
        in_specs=[qkv_spec(0), qkv_spec(1), qkv_spec(2), gb_spec, grow_spec, st_spec,
                  BS((R, wd), lambda d, h, n: (order(d, n), h))] + s_in,
        out_specs=[BS((1, 3, R, wd), lambda d, h, n: (d, 0, order(d, n), h)),
                   BS((1, hp, R, 2), lambda d, h, n: (d, h, order(d, n), 0))] + s_out,
        out_shape=[SDS((2, 3, S, GDN_H * GDN_DK), F32), SDS((2, GDN_H, S, 2), F32)] + s_shape,
        scratch_shapes=[pltpu.VMEM((hp, GDN_DK, GDN_DK), F32), pltpu.VMEM(blk, BF16), pltpu.VMEM(blk, BF16),
                        pltpu.VMEM(blk, F32), pltpu.VMEM(sq, F32), pltpu.VMEM(row, F32), pltpu.VMEM(sq, BF16),
                        pltpu.VMEM(blk, F32), pltpu.VMEM(row, F32)] + s_scr_shapes,
        name=name, compiler_params=_cp(("arbitrary", "arbitrary", "arbitrary")))(qkv, qkv, qkv, gb, grow, states, do, *s_ops)


def gdn_post_fwd(o2, proj, nw, name):
    S = proj.shape[0]
    tm = min(512, S)
    zoff = GDN_QKV // LANES

    def body(o_ref, z_ref, nw_ref, y_ref):
        o = o_ref[0] + o_ref[1]
        z = z_ref[...]
        r = lax.rsqrt(jnp.mean(o * o, axis=-1, keepdims=True) + RMS_EPS)
        y_ref[...] = (o * r * nw_ref[...] * (z * _sigmoid(z))).astype(BF16)

    return pl.pallas_call(
        body, grid=(S // tm, GDN_H),
        in_specs=[BS((2, tm, LANES), lambda i, h: (0, i, h)), BS((tm, LANES), lambda i, h: (i, zoff + h)),
                  BS((1, LANES), lambda i, h: (0, 0))],
        out_specs=BS((tm, LANES), lambda i, h: (i, h)),
        out_shape=SDS((S, GDN_H * GDN_DK), BF16), name=name, compiler_params=_cp(("parallel", "parallel")))(o2, proj, nw)


def gdn_post_bwd(o2, proj, nw, dy, name):
    S = proj.shape[0]
    tm = min(512, S)
    zoff = GDN_QKV // LANES

    def body(o_ref, z_ref, nw_ref, dy_ref, do_ref, dz_ref, dnw_ref):
        first = (pl.program_id(0) == 0) & (pl.program_id(1) == 0)
        o = o_ref[0] + o_ref[1]
        z = z_ref[...]
        nwv = nw_ref[...]
        dyv = dy_ref[...]
        r = lax.rsqrt(jnp.mean(o * o, axis=-1, keepdims=True) + RMS_EPS)
        n = o * r
        sg = _sigmoid(z)
        sz = z * sg
        dz_ref[...] = (dyv * n * nwv * (sg * (1.0 + z * (1.0 - sg)))).astype(BF16)
        dn = dyv * nwv * sz
        do_ref[...] = r * (dn - n * jnp.mean(dn * n, axis=-1, keepdims=True))
        part = jnp.sum(dyv * n * sz, axis=0, keepdims=True)

        @pl.when(first)
        def _():
            dnw_ref[...] = part

        @pl.when(jnp.logical_not(first))
        def _():
            dnw_ref[...] += part

    blk = BS((tm, LANES), lambda i, h: (i, h))
    return pl.pallas_call(
        body, grid=(S // tm, GDN_H),
        in_specs=[BS((2, tm, LANES), lambda i, h: (0, i, h)), BS((tm, LANES), lambda i, h: (i, zoff + h)),
                  BS((1, LANES), lambda i, h: (0, 0)), blk],
        out_specs=[blk, blk, BS((1, LANES), lambda i, h: (0, 0))],
        out_shape=[SDS((S, GDN_H * GDN_DK), F32), SDS((S, GDN_H * GDN_DK), BF16), SDS((1, LANES), F32)],
        name=name, compiler_params=_cp(("arbitrary", "arbitrary")))(o2, proj, nw, dy)


def _gate_prm(a_log, dt_bias):
    z = jnp.zeros((8, LANES), F32)
    z = z.at[0, :2 * GDN_H].set(a_log.reshape(-1))
    return z.at[1, :2 * GDN_H].set(dt_bias.reshape(-1))


def gdn_fwd(x, g, w_all, convw, a_log, dt_bias, nw, w_out, tag, side=None):
    S = x.shape[0]
    h = rms_fwd(x, g, f"{tag}_rms")
    proj = mm(h, w_all, name=f"{tag}_proj", tn=1408)
    qkv = gdn_pre_fwd(proj, convw, f"{tag}_pre")
    prm = _gate_prm(a_log, dt_bias)
    gb = gdn_gate_fwd(proj, prm, f"{tag}_gate")
    grow = gb[:, :2 * GDN_H].T.reshape(2 * GDN_H, S // GDN_C, GDN_C)
    o2, states, *side_out = gdn_scan_fwd(qkv, gb, grow, f"{tag}_scan", side)
    y = gdn_post_fwd(o2, proj, nw, f"{tag}_post")
    xn = mm(y, w_out, name=f"{tag}_out", epi=_add, extras=(x,))
    return xn, (h, proj, qkv, prm, gb, grow, o2, states, y), side_out


def gdn_bwd(x, g, w_all, convw, nw, w_out, saved, dx, dxb, tag, side=None):
    S = x.shape[0]
    h, proj, qkv, prm, gb, grow, o2, states, y = saved
    dw_out = mm(y, dxb, name=f"{tag}_dwout", ta=True)
    dy = mm(dxb, w_out, name=f"{tag}_dy", tb=True)
    do, dz, dnw = gdn_post_bwd(o2, proj, nw, dy, f"{tag}_postb")
    dqkv, dgate, *side_out = gdn_scan_bwd(qkv, gb, grow, states, do, f"{tag}_scanb", side)
    dgb = jnp.transpose(dgate, (2, 3, 0, 1)).reshape(S, 4 * GDN_H)
    dgb = jnp.pad(dgb, ((0, 0), (0, LANES - 4 * GDN_H)))
    dab, dprm = gdn_gate_bwd(proj, prm, dgb, f"{tag}_gateb")
    dpq, dconvw = gdn_pre_bwd(proj, convw, dqkv, f"{tag}_preb")
    dproj = jnp.concatenate([dpq, dz, dab], axis=1)
    dw_all = mm(h, dproj, name=f"{tag}_dwin", ta=True, tn=384)
    dh = mm(dproj, w_all, name=f"{tag}_dh", tb=True, tk=1408)
    dx, dxb, dg = rms_bwd(x, g, dh, dx, f"{tag}_rmsb")
    da_log = dprm[0, :2 * GDN_H].reshape(2, GDN_H)
    ddt = dprm[1, :2 * GDN_H].reshape(2, GDN_H)
    return dx, dxb, dg, dw_all, dconvw, da_log, ddt, dnw, dw_out, side_out


def _rel_bucket_np(rel):
    nb = REL_BUCKETS // 2
    max_exact = nb // 2
    ret = np.where(rel > 0, nb, 0)
    n = np.abs(rel)
    nf = np.maximum(n, 1).astype(np.float32)
    large = max_exact + (np.log(nf / max_exact) / np.float32(math.log(REL_MAX_DIST / max_exact))
                         * (nb - max_exact)).astype(np.int32)
    large = np.minimum(large, nb - 1)
    return ret + np.where(n < max_exact, n, large)


def _toeplitz(f, rows, cols):
    period = rows + cols
    e = jnp.pad(f, ((0, 0), (0, period - f.shape[1])))
    y = jnp.tile(e, (1, rows))[:, :rows * (period - 1)]
    return y.reshape(f.shape[0], rows, period - 1)[:, :, :cols]


ATT_Q = DSWA_HALF
ATT_W = 3 * DSWA_HALF
ATT_TB = 1024
ATT_PAIRS = DSWA_HG // 2


def _bias_mats(rel_table, gi):
    _, dil = DSWA_CFG[gi]
    offs = np.arange(-DSWA_HALF, DSWA_HALF + 1)
    onehot = jnp.asarray(np.eye(REL_BUCKETS, dtype=np.float32)[_rel_bucket_np(offs * dil)])
    f = jnp.dot(onehot, rel_table, precision=HI)[:, gi * DSWA_HG:(gi + 1) * DSWA_HG].T
    bias = _toeplitz(f, ATT_Q, ATT_W)
    bias_t = jnp.transpose(_toeplitz(f[:, ::-1], ATT_Q, ATT_W), (0, 2, 1))
    return bias.reshape(ATT_PAIRS, 2, ATT_Q, ATT_W), bias_t.reshape(ATT_PAIRS, 2, ATT_W, ATT_Q)


def _att_specs(S, d, col):
    halo = DSWA_HALF * d
    per = ATT_TB // halo
    last = S // halo - 1
    cur = BS((ATT_TB, LANES), lambda p, tb: (tb, col(p)))
    prev = BS((halo, LANES), lambda p, tb: (jnp.maximum(tb * per - 1, 0), col(p)))
    nxt = BS((halo, LANES), lambda p, tb: (jnp.minimum((tb + 1) * per, last), col(p)))
    return prev, cur, nxt


def _att_specs3(S, d, lead):
    halo = DSWA_HALF * d
    per = ATT_TB // halo
    last = S // halo - 1
    cur = BS((1, ATT_TB, LANES), lambda p, tb: (lead(p), tb, 0))
    prev = BS((1, halo, LANES), lambda p, tb: (lead(p), jnp.maximum(tb * per - 1, 0), 0))
    nxt = BS((1, halo, LANES), lambda p, tb: (lead(p), jnp.minimum((tb + 1) * per, last), 0))
    return prev, cur, nxt


class _Pieces:
    def __init__(self, prev, cur, nxt, d, lead=None, cast=None):
        self.refs, self.d, self.lead, self.cast, self.cache = (prev, cur, nxt), d, lead, cast, {}
        self.halo = DSWA_HALF * d
        self.nsb = ATT_TB // self.halo

    def __call__(self, r, sb):
        if (r, sb) not in self.cache:
            ref = self.refs[0] if sb < 0 else self.refs[2] if sb >= self.nsb else self.refs[1]
            start = r + (self.halo * sb if 0 <= sb < self.nsb else 0)
            rows = pl.ds(start, ATT_Q, stride=self.d) if self.d > 1 else pl.ds(start, ATT_Q)
            v = ref[rows, :] if self.lead is None else ref[0, rows, :]
            self.cache[(r, sb)] = v if self.cast is None else v.astype(self.cast)
        return self.cache[(r, sb)]

    def window(self, r, sb):
        return jnp.concatenate([self(r, sb - 1), self(r, sb), self(r, sb + 1)], axis=0)


ATT_GROUP = 8


def _tile_groups(d, nsb):
    tiles = [(r, sb) for r in range(d) for sb in range(nsb)]
    return [tiles[i:i + ATT_GROUP] for i in range(0, len(tiles), ATT_GROUP)]


def _tile_rows(r, sb, d):
    start = r + DSWA_HALF * d * sb
    return pl.ds(start, ATT_Q, stride=d) if d > 1 else pl.ds(start, ATT_Q)


def _tile_valid(tb, r, sb, d, S, transposed):
    shape = (ATT_W, ATT_Q) if transposed else (ATT_Q, ATT_W)
    blk = lax.broadcasted_iota(jnp.int32, shape, 1 if transposed else 0)
    win = lax.broadcasted_iota(jnp.int32, shape, 0 if transposed else 1)
    tok = tb * ATT_TB + r + d * (DSWA_HALF * (sb - 1) + win)
    return (jnp.abs(win - DSWA_HALF - blk) <= DSWA_HALF) & (tok >= 0) & (tok < S)


def _head_masks():
    lane = lax.broadcasted_iota(jnp.int32, (1, LANES), 1)
    return [lane < DSWA_E, lane >= DSWA_E], lane


def attn_fwd(qkv, bias, gi, name):
    S = qkv.shape[0]
    d = DSWA_CFG[gi][1]
    nsb = ATT_TB // (DSWA_HALF * d)
    npair = DSWA_HEADS // 2
    q_spec = _att_specs(S, d, lambda p: gi * ATT_PAIRS + p)[1]
    k_specs = _att_specs(S, d, lambda p: npair + gi * ATT_PAIRS + p)
    v_specs = _att_specs(S, d, lambda p: 2 * npair + gi * ATT_PAIRS + p)

    def body(q_ref, kp, kc, kn, vp, vc, vn, b_ref, o_ref, lse_ref):
        tb = pl.program_id(1)
        masks, lane = _head_masks()
        kpc = _Pieces(kp, kc, kn, d, cast=BF16)
        vpc = _Pieces(vp, vc, vn, d, cast=BF16)
        scale = DSWA_E ** -0.5
        for grp in _tile_groups(d, nsb):
            rows = [_tile_rows(r, sb, d) for r, sb in grp]
            qs = [q_ref[rw, :] for rw in rows]
            kws = [kpc.window(r, sb) for r, sb in grp]
            vws = [vpc.window(r, sb) for r, sb in grp]
            valids = [_tile_valid(tb, r, sb, d, S, False) for r, sb in grp]
            both = [(t, hh) for t in range(len(grp)) for hh in range(2)]
            ss = [_dot_nt(_bf(jnp.where(masks[hh], qs[t], 0.0)), kws[t]) * scale + b_ref[0, hh] for t, hh in both]
            ss = [jnp.where(valids[t], s, NEG_INF) for (t, hh), s in zip(both, ss)]
            ms = [jnp.max(s, axis=-1, keepdims=True) for s in ss]
            ps = [jnp.exp(s - m) for s, m in zip(ss, ms)]
            ls = [jnp.sum(p, axis=-1, keepdims=True) for p in ps]
            os = [_dot(_bf(p / l), vws[t]) for (t, hh), p, l in zip(both, ps, ls)]
            for t, rw in enumerate(rows):
                o_ref[rw, :] = jnp.where(masks[0], os[2 * t], os[2 * t + 1])
                lse_ref[0, rw, :] = (jnp.where(lane == 0, ms[2 * t] + jnp.log(ls[2 * t]), 0.0)
                                     + jnp.where(lane == 1, ms[2 * t + 1] + jnp.log(ls[2 * t + 1]), 0.0))

    return pl.pallas_call(
        body, grid=(ATT_PAIRS, S // ATT_TB),
        in_specs=[q_spec, *k_specs, *v_specs, BS((1, 2, ATT_Q, ATT_W), lambda p, tb: (p, 0, 0, 0))],
        out_specs=[BS((ATT_TB, LANES), lambda p, tb: (tb, p)), BS((1, ATT_TB, LANES), lambda p, tb: (p, tb, 0))],
        out_shape=[SDS((S, DSWA_HG * DSWA_E), F32), SDS((ATT_PAIRS, S, LANES), F32)],
        name=name, compiler_params=_cp(("parallel", "parallel")))(qkv, qkv, qkv, qkv, qkv, qkv, qkv, bias)


def attn_bwd_q(qkv, bias, lse, do, dd, gi, name):
    S = qkv.shape[0]
    d = DSWA_CFG[gi][1]
    nsb = ATT_TB // (DSWA_HALF * d)
    npair = DSWA_HEADS // 2
    q_spec = _att_specs(S, d, lambda p: gi * ATT_PAIRS + p)[1]
    k_specs = _att_specs(S, d, lambda p: npair + gi * ATT_PAIRS + p)
    v_specs = _att_specs(S, d, lambda p: 2 * npair + gi * ATT_PAIRS + p)
    bspec = BS((1, 2, ATT_Q, ATT_W), lambda p, tb: (p, 0, 0, 0))

    def body(q_ref, kp, kc, kn, vp, vc, vn, b_ref, lse_ref, do_ref, dd_ref, dq_ref, db_ref):
        tb = pl.program_id(1)
        masks, lane = _head_masks()
        kpc = _Pieces(kp, kc, kn, d, cast=BF16)
        vpc = _Pieces(vp, vc, vn, d, cast=BF16)
        db = [jnp.zeros((ATT_Q, ATT_W), F32), jnp.zeros((ATT_Q, ATT_W), F32)]
        scale = DSWA_E ** -0.5
        for grp in _tile_groups(d, nsb):
            rows = [_tile_rows(r, sb, d) for r, sb in grp]
            qs = [q_ref[rw, :] for rw in rows]
            dos = [do_ref[0, rw, :] for rw in rows]
            lses = [lse_ref[0, rw, :] for rw in rows]
            dds = [dd_ref[0, 0, rw, :] for rw in rows]
            kws = [kpc.window(r, sb) for r, sb in grp]
            vws = [vpc.window(r, sb) for r, sb in grp]
            valids = [_tile_valid(tb, r, sb, d, S, False) for r, sb in grp]
            both = [(t, hh) for t in range(len(grp)) for hh in range(2)]
            ss = [_dot_nt(_bf(jnp.where(masks[hh], qs[t], 0.0)), kws[t]) * scale + b_ref[0, hh] for t, hh in both]
            dps = [_dot_nt(_bf(jnp.where(masks[hh], dos[t], 0.0)), vws[t]) for t, hh in both]
            ps = [jnp.exp(jnp.where(valids[t], s - lses[t][:, hh:hh + 1], NEG_INF)) for (t, hh), s in zip(both, ss)]
            dss = [p * (dp - dds[t][:, hh:hh + 1]) for (t, hh), p, dp in zip(both, ps, dps)]
            dqs = [_dot(_bf(ds), kws[t]) * scale for (t, hh), ds in zip(both, dss)]
            for t, rw in enumerate(rows):
                dq_ref[rw, :] = jnp.where(masks[0], dqs[2 * t], dqs[2 * t + 1])
                db[0] = db[0] + dss[2 * t]
                db[1] = db[1] + dss[2 * t + 1]

        @pl.when(tb == 0)
        def _():
            db_ref[0, 0] = db[0]
            db_ref[0, 1] = db[1]

        @pl.when(tb > 0)
        def _():
            db_ref[0, 0] += db[0]
            db_ref[0, 1] += db[1]

    return pl.pallas_call(
        body, grid=(ATT_PAIRS, S // ATT_TB),
        in_specs=[q_spec, *k_specs, *v_specs, bspec, BS((1, ATT_TB, LANES), lambda p, tb: (p, tb, 0)),
                  BS((1, ATT_TB, LANES), lambda p, tb: (gi, tb, p)), BS((1, 1, ATT_TB, LANES), lambda p, tb: (gi, p, tb, 0))],
        out_specs=[BS((ATT_TB, LANES), lambda p, tb: (tb, p)), bspec],
        out_shape=[SDS((S, DSWA_HG * DSWA_E), F32), SDS((ATT_PAIRS, 2, ATT_Q, ATT_W), F32)],
        name=name, compiler_params=_cp(("parallel", "arbitrary")))(qkv, qkv, qkv, qkv, qkv, qkv, qkv, bias, lse, do, dd)


def attn_bwd_kv(qkv, bias_t, lse, do, dd, gi, name):
    S = qkv.shape[0]
    d = DSWA_CFG[gi][1]
    nsb = ATT_TB // (DSWA_HALF * d)
    npair = DSWA_HEADS // 2
    q_specs = _att_specs(S, d, lambda p: gi * ATT_PAIRS + p)
    k_spec = _att_specs(S, d, lambda p: npair + gi * ATT_PAIRS + p)[1]
    v_spec = _att_specs(S, d, lambda p: 2 * npair + gi * ATT_PAIRS + p)[1]
    halo = DSWA_HALF * d
    per = ATT_TB // halo
    last = S // halo - 1

    def do_spec(rows, blk):
        return BS((1, rows, LANES), lambda p, tb: (gi, blk(tb), p))

    def dd_spec(rows, blk):
        return BS((1, 1, rows, LANES), lambda p, tb: (gi, p, blk(tb), 0))

    blks = [(halo, lambda tb: jnp.maximum(tb * per - 1, 0)), (ATT_TB, lambda tb: tb),
            (halo, lambda tb: jnp.minimum((tb + 1) * per, last))]
    do_specs = [do_spec(*b) for b in blks]
    dd_specs = [dd_spec(*b) for b in blks]
    lse_specs = _att_specs3(S, d, lambda p: p)

    class _Lead4:
        def __init__(self, ref):
            self.ref = ref

        def __getitem__(self, idx):
            return self.ref[(0,) + idx]

    def body(k_ref, v_ref, qp, qc, qn, dop, doc, don, lp, lc, ln, ddp, ddc, ddn, b_ref, dk_ref, dv_ref):
        tb = pl.program_id(1)
        masks, lane = _head_masks()
        qpc = _Pieces(qp, qc, qn, d)
        dopc = _Pieces(dop, doc, don, d, lead=True)
        lpc = _Pieces(lp, lc, ln, d, lead=True)
        ddpc = _Pieces(_Lead4(ddp), _Lead4(ddc), _Lead4(ddn), d, lead=True)
        scale = DSWA_E ** -0.5
        for grp in _tile_groups(d, nsb):
            rows = [_tile_rows(r, sb, d) for r, sb in grp]
            kcs = [_bf(k_ref[rw, :]) for rw in rows]
            vcs = [_bf(v_ref[rw, :]) for rw in rows]
            qws = [qpc.window(r, sb) for r, sb in grp]
            dows = [dopc.window(r, sb) for r, sb in grp]
            lws = [lpc.window(r, sb) for r, sb in grp]
            ddws = [ddpc.window(r, sb) for r, sb in grp]
            qwbs = [_bf(x) for x in qws]
            dowbs = [_bf(x) for x in dows]
            valids = [_tile_valid(tb, r, sb, d, S, True) for r, sb in grp]
            both = [(t, hh) for t in range(len(grp)) for hh in range(2)]
            ss = [_dot_nt(_bf(jnp.where(masks[hh], qws[t], 0.0)), kcs[t]) * scale + b_ref[0, hh] for t, hh in both]
            dps = [_dot_nt(_bf(jnp.where(masks[hh], dows[t], 0.0)), vcs[t]) for t, hh in both]
            ps = [jnp.exp(jnp.where(valids[t], s - lws[t][:, hh:hh + 1], NEG_INF)) for (t, hh), s in zip(both, ss)]
            dvs = [_dot_tn(_bf(p), dowbs[t]) for (t, hh), p in zip(both, ps)]
            dss = [p * (dp - ddws[t][:, hh:hh + 1]) for (t, hh), p, dp in zip(both, ps, dps)]
            dks = [_dot_tn(_bf(ds), qwbs[t]) * scale for (t, hh), ds in zip(both, dss)]
            for t, rw in enumerate(rows):
                dk_ref[rw, :] = jnp.where(masks[0], dks[2 * t], dks[2 * t + 1])
                dv_ref[rw, :] = jnp.where(masks[0], dvs[2 * t], dvs[2 * t + 1])

    out = BS((ATT_TB, LANES), lambda p, tb: (tb, p))
    return pl.pallas_call(
        body, grid=(ATT_PAIRS, S // ATT_TB),
        in_specs=[k_spec, v_spec, *q_specs, *do_specs, *lse_specs, *dd_specs,
                  BS((1, 2, ATT_W, ATT_Q), lambda p, tb: (p, 0, 0, 0))],
        out_specs=[out, out],
        out_shape=[SDS((S, DSWA_HG * DSWA_E), F32), SDS((S, DSWA_HG * DSWA_E), F32)],
        name=name, compiler_params=_cp(("parallel", "parallel")))(
            qkv, qkv, qkv, qkv, qkv, do, do, do, lse, lse, lse, dd, dd, dd, bias_t)


def _pair_alphas(lses):
    m = jnp.maximum(jnp.maximum(lses[0], lses[1]), lses[2])
    e = [jnp.exp(t - m) for t in lses]
    tot = e[0] + e[1] + e[2]
    return [t / tot for t in e]


def _pair_expand(a, lane):
    return jnp.where(lane < DSWA_E, a[:, 0:1], a[:, 1:2])


def combine_fwd(o_raw, lse, name):
    S = o_raw.shape[0]
    tm = min(512, S)

    def body(o_ref, l_ref, y_ref):
        g = pl.program_id(2)
        lane = lax.broadcasted_iota(jnp.int32, (1, LANES), 1)
        alphas = _pair_alphas([l_ref[0, 0], l_ref[1, 0], l_ref[2, 0]])
        a = jnp.where(g == 0, alphas[0], jnp.where(g == 1, alphas[1], alphas[2]))
        y_ref[...] = (o_ref[...] * _pair_expand(a, lane)).astype(BF16)

    blk = BS((tm, LANES), lambda i, p, g: (i, g * ATT_PAIRS + p))
    return pl.pallas_call(
        body, grid=(S // tm, ATT_PAIRS, 3),
        in_specs=[blk, BS((3, 1, tm, LANES), lambda i, p, g: (0, p, i, 0))], out_specs=blk,
        out_shape=SDS((S, DSWA_W), BF16), name=name, compiler_params=_cp(("parallel", "parallel", "parallel")))(o_raw, lse)


def combine_bwd(o_raw, lse, dy, name):
    S = o_raw.shape[0]
    tm = min(512, S)

    def body(o0, o1, o2, l_ref, d0, d1, d2, do_ref, dd_ref):
        lane = lax.broadcasted_iota(jnp.int32, (1, LANES), 1)
        alphas = _pair_alphas([l_ref[0, 0], l_ref[1, 0], l_ref[2, 0]])
        c = jnp.zeros((tm, LANES), F32)
        for g, (o_ref, dy_ref) in enumerate(((o0, d0), (o1, d1), (o2, d2))):
            dyv = dy_ref[...]
            do_ref[g] = dyv * _pair_expand(alphas[g], lane)
            prod = o_ref[...] * dyv
            dal = (jnp.where(lane == 0, jnp.sum(jnp.where(lane < DSWA_E, prod, 0.0), axis=1, keepdims=True), 0.0)
                   + jnp.where(lane == 1, jnp.sum(jnp.where(lane >= DSWA_E, prod, 0.0), axis=1, keepdims=True), 0.0))
            c = c + alphas[g] * dal
        for g in range(3):
            dd_ref[g, 0] = alphas[g] * c

    def col(g):
        return BS((tm, LANES), lambda i, p: (i, g * ATT_PAIRS + p))

    return pl.pallas_call(
        body, grid=(S // tm, ATT_PAIRS),
        in_specs=[col(0), col(1), col(2), BS((3, 1, tm, LANES), lambda i, p: (0, p, i, 0)), col(0), col(1), col(2)],
        out_specs=[BS((3, tm, LANES), lambda i, p: (0, i, p)), BS((3, 1, tm, LANES), lambda i, p: (0, p, i, 0))],
        out_shape=[SDS((3, S, DSWA_HG * DSWA_E), F32), SDS((3, ATT_PAIRS, S, LANES), F32)],
        name=name, compiler_params=_cp(("parallel", "parallel")))(o_raw, o_raw, o_raw, lse, dy, dy, dy)


def dswa_fwd(x, g, w_in, w_out, rel_table, tag):
    h = rms_fwd(x, g, f"{tag}_rms")
    qkv = mm(h, w_in, name=f"{tag}_qkv", tn=1152)
    outs, lses = [], []
    for gi in range(3):
        bias, _ = _bias_mats(rel_table, gi)
        o, lse = attn_fwd(qkv, bias, gi, f"{tag}_att{gi}")
        outs.append(o)
        lses.append(lse)
    o_raw = jnp.concatenate(outs, axis=1)
    lse = jnp.stack(lses)
    y = combine_fwd(o_raw, lse, f"{tag}_comb")
    xn = mm(y, w_out, name=f"{tag}_out", epi=_add, extras=(x,))
    return xn, (h, qkv, o_raw, lse, y)


def dswa_bwd(x, g, w_in, w_out, rel_table, saved, dx, dxb, tag):
    h, qkv, o_raw, lse, y = saved
    dw_out = mm(y, dxb, name=f"{tag}_dwout", ta=True, tm=384)
    dy = mm(dxb, w_out, name=f"{tag}_dy", tb=True, tn=384)
    do_raw, dd = combine_bwd(o_raw, lse, dy, f"{tag}_combb")
    dqs, dks, dvs = [], [], []
    drel = jnp.zeros_like(rel_table)
    for gi in range(3):
        (bias, bias_t), bias_vjp = jax.vjp(lambda tbl: _bias_mats(tbl, gi), rel_table)
        dq, dbias = attn_bwd_q(qkv, bias, lse[gi], do_raw, dd, gi, f"{tag}_attq{gi}")
        dk, dv = attn_bwd_kv(qkv, bias_t, lse[gi], do_raw, dd, gi, f"{tag}_attkv{gi}")
        drel = drel + bias_vjp((dbias, jnp.zeros_like(bias_t)))[0]
        dqs.append(dq)
        dks.append(dk)
        dvs.append(dv)
    dqkv = jnp.concatenate(dqs + dks + dvs, axis=1).astype(BF16)
    dw_in = mm(h, dqkv, name=f"{tag}_dwin", ta=True, tn=384)
    dh = mm(dqkv, w_in, name=f"{tag}_dh", tb=True, tk=1152)
    dx, dxb, dg = rms_bwd(x, g, dh, dx, f"{tag}_rmsb")
    return dx, dxb, dg, dw_in, dw_out, drel


def adamw(w, g, m, v, name):
    shape = w.shape
    last = shape[-1]
    w2, g2, m2, v2 = (t.reshape(-1, last) for t in (w, g, m, v))
    rows = w2.shape[0]
    tr = rows
    if rows > 512:
        tr = next(t for t in (512, 256, 192, 128, 64, 8) if rows % t == 0)
    c1 = 1.0 / (1.0 - ADAM_B1 ** ADAM_STEP)
    c2 = 1.0 / (1.0 - ADAM_B2 ** ADAM_STEP)

    def body(w_ref, g_ref, m_ref, v_ref, d_ref, nm_ref, nv_ref):
        gv = g_ref[...]
        nm = ADAM_B1 * m_ref[...] + (1.0 - ADAM_B1) * gv
        nv = ADAM_B2 * v_ref[...] + (1.0 - ADAM_B2) * (gv * gv)
        nm_ref[...] = nm
        nv_ref[...] = nv
        d_ref[...] = -ADAM_LR * ((nm * c1) / (jnp.sqrt(nv * c2) + ADAM_EPS) + ADAM_WD * w_ref[...])

    spec = BS((tr, last), lambda i: (i, 0))
    outs = pl.pallas_call(
        body, grid=(rows // tr,), in_specs=[spec] * 4, out_specs=[spec] * 3,
        out_shape=[SDS((rows, last), F32)] * 3, name=name, compiler_params=_cp(("parallel",)))(w2, g2, m2, v2)
    return tuple(o.reshape(shape) for o in outs)


def _place():
    x, y, c = lax.axis_index("x"), lax.axis_index("y"), lax.axis_index("c")
    chips = [(1 - x, y), (x, 1 - y), (1 - x, 1 - y)]
    return x, y, c, chips


def _rcopy(src, dst, ssem, rsem, dev):
    return pltpu.make_async_remote_copy(src_ref=src, dst_ref=dst, send_sem=ssem, recv_sem=rsem, device_id=dev,
                                        device_id_type=MESH)


class SideJob(NamedTuple):
    ins: list
    outs: list
    sems: list
    start: Callable
    wait: Callable


def _job(ins, outs, sems, copies):
    def start(in_refs, out_refs, sem_refs):
        for cp in copies(in_refs, out_refs, sem_refs):
            cp.start()

    def wait(in_refs, out_refs, sem_refs):
        for cp in copies(in_refs, out_refs, sem_refs):
            cp.wait()

    return SideJob(list(ins), list(outs), list(sems), start, wait)


def gather_job(packs, halved):
    n = len(packs)
    dma = pltpu.SemaphoreType.DMA

    def copies(in_refs, out_refs, sems):
        ssem, rsem = sems
        x, y, c, chips = _place()
        jme = 2 * x + y
        cps = []
        for i, (p_ref, f_ref) in enumerate(zip(in_refs, out_refs)):
            rows = p_ref.shape[0]
            mine = pl.ds(c * (rows // 2), rows // 2) if halved[i] else pl.ds(0, rows)
            for r, (cx, cy) in enumerate(chips):
                cps.append(_rcopy(p_ref.at[mine], f_ref.at[jme, mine], ssem.at[i, r], rsem.at[i, r], (cx, cy, c)))
        return cps

    return _job(packs, [SDS((4,) + p.shape, p.dtype) for p in packs], [dma((n, 3)), dma((n, 3))], copies)


def chip_exchange_job(parts):
    n = len(parts)
    dma = pltpu.SemaphoreType.DMA

    def copies(in_refs, out_refs, sems):
        ssem, rsem = sems
        x, y, c, chips = _place()
        cps = []
        for i, (p_ref, r_ref) in enumerate(zip(in_refs, out_refs)):
            for r, (cx, cy) in enumerate(chips):
                cps.append(_rcopy(p_ref.at[2 * cx + cy], r_ref.at[r], ssem.at[i, r], rsem.at[i, r], (cx, cy, c)))
        return cps

    return _job(parts, [SDS((3,) + p.shape[1:], p.dtype) for p in parts], [dma((n, 3)), dma((n, 3))], copies)


def run_job(job, name):
    ni, no = len(job.ins), len(job.outs)

    def body(*refs):
        job.start(refs[:ni], refs[ni:ni + no], refs[ni + no:])
        job.wait(refs[:ni], refs[ni:ni + no], refs[ni + no:])

    return pl.pallas_call(
        body, in_specs=[ANY] * ni, out_specs=[ANY] * no, out_shape=job.outs, scratch_shapes=job.sems, name=name,
        compiler_params=pltpu.CompilerParams(has_side_effects=True))(*job.ins)


def forward_to_sibling(fulls, name):
    n = len(fulls)

    def body(*refs):
        in_refs, out_refs, (ssem, rsem) = refs[:n], refs[n:2 * n], refs[2 * n:]
        x, y, c, chips = _place()
        cps = []
        for i in range(n):
            half = in_refs[i].shape[1] // 2
            for r, (cx, cy) in enumerate(chips):
                piece = (2 * cx + cy, pl.ds(c * half, half))
                cps.append(_rcopy(in_refs[i].at[piece], out_refs[i].at[piece], ssem.at[i, r], rsem.at[i, r], (x, y, 1 - c)))
        for cp in cps:
            cp.start()
        for cp in cps:
            cp.wait()

    dma = pltpu.SemaphoreType.DMA
    return pl.pallas_call(
        body, in_specs=[ANY] * n, out_specs=[ANY] * n, out_shape=[SDS(f.shape, f.dtype) for f in fulls],
        scratch_shapes=[dma((n, 3)), dma((n, 3))], input_output_aliases={i: i for i in range(n)}, name=name,
        compiler_params=pltpu.CompilerParams(has_side_effects=True))(*fulls)


def rs_sibling_exchange(gpack, name):
    _, rows, W = gpack.shape
    half = rows // 2

    def body(g_ref, r_ref, ssem, rsem):
        x, y, c, _ = _place()
        cps = [_rcopy(g_ref.at[j, pl.ds((1 - c) * half, half)], r_ref.at[j], ssem.at[j], rsem.at[j], (x, y, 1 - c))
               for j in range(4)]
        for cp in cps:
            cp.start()
        for cp in cps:
            cp.wait()

    dma = pltpu.SemaphoreType.DMA
    return pl.pallas_call(
        body, in_specs=[ANY], out_specs=ANY, out_shape=SDS((4, half, W), gpack.dtype),
        scratch_shapes=[dma((4,)), dma((4,))], name=name,
        compiler_params=pltpu.CompilerParams(has_side_effects=True))(gpack)


def _div_tile(n, limit):
    return next(t for t in range(limit - limit % 16, 0, -16) if n % t == 0)


def rs_add_sibling(gpack, recv, cidx, name, out_dtype=F32):
    _, rows, W = gpack.shape
    half = rows // 2
    tr = _div_tile(half, 1024)
    nb = half // tr

    def body(c_ref, g_ref, r_ref, o_ref):
        o_ref[...] = (g_ref[...] + r_ref[...]).astype(o_ref.dtype)

    gs = pltpu.PrefetchScalarGridSpec(
        num_scalar_prefetch=1, grid=(4, nb),
        in_specs=[BS((1, tr, W), lambda j, i, c: (j, c[0] * nb + i, 0)), BS((1, tr, W), lambda j, i, c: (j, i, 0))],
        out_specs=BS((1, tr, W), lambda j, i, c: (j, i, 0)))
    return pl.pallas_call(body, grid_spec=gs, out_shape=SDS((4, half, W), out_dtype), name=name,
                          compiler_params=_cp(("parallel", "parallel")))(cidx, gpack, recv)


def rs_add_chips(recv, part, place, name):
    _, half, W = recv.shape
    tr = _div_tile(half, 640)
    nb = half // tr

    def body(x_ref, y_ref, c_ref, r_ref, own_ref, o_ref):
        r0, r1, r2, own = (t.astype(F32) for t in (r_ref[0], r_ref[1], r_ref[2], own_ref[0]))
        o_ref[...] = ((r0 + r1) + r2) + own

    gs = pltpu.PrefetchScalarGridSpec(
        num_scalar_prefetch=3, grid=(nb,),
        in_specs=[BS((3, tr, W), lambda i, x, y, c: (0, i, 0)), BS((1, tr, W), lambda i, x, y, c: (2 * x[0] + y[0], i, 0))],
        out_specs=BS((tr, W), lambda i, x, y, c: (c[0] * nb + i, 0)))
    return pl.pallas_call(body, grid_spec=gs, out_shape=SDS((2 * half, W), F32), name=name,
                          compiler_params=_cp(("parallel",)))(*place, recv, part)


def rs_sibling_share(gsh, name):
    rows, W = gsh.shape
    half = rows // 2

    def body(g_ref, o_ref, ssem, rsem):
        x, y, c, _ = _place()
        mine = pl.ds(c * half, half)
        cp = _rcopy(g_ref.at[mine], o_ref.at[mine], ssem, rsem, (x, y, 1 - c))
        cp.start()
        cp.wait()

    dma = pltpu.SemaphoreType.DMA
    return pl.pallas_call(
        body, in_specs=[ANY], out_specs=ANY, out_shape=SDS(gsh.shape, gsh.dtype),
        scratch_shapes=[dma, dma], input_output_aliases={0: 0}, name=name,
        compiler_params=pltpu.CompilerParams(has_side_effects=True))(gsh)


def allreduce_small(pack):
    R = pack.shape[0]

    def body(p_ref, o_ref, all_ref, ssem, rsem):
        x, y, c, _ = _place()
        me = 4 * x + 2 * y + c
        all_ref[me] = p_ref[...]
        cps = []
        for m in range(1, 8):
            peer = (1 - x if m & 4 else x, 1 - y if m & 2 else y, 1 - c if m & 1 else c)
            cp = _rcopy(p_ref, all_ref.at[me], ssem.at[m - 1], rsem.at[m - 1], peer)
            cp.start()
            cps.append(cp)
        for cp in cps:
            cp.wait()
        acc = all_ref[0]
        for i in range(1, 8):
            acc = acc + all_ref[i]
        o_ref[...] = acc

    dma = pltpu.SemaphoreType.DMA
    vm = BS(memory_space=pltpu.VMEM)
    return pl.pallas_call(
        body, in_specs=[vm], out_specs=vm, out_shape=SDS(pack.shape, F32),
        scratch_shapes=[pltpu.VMEM((8, R, LANES), F32), dma((7,)), dma((7,))], name="allreduce_small",
        compiler_params=pltpu.CompilerParams(has_side_effects=True))(pack)


PACK_W = 1024
PACK_ALIGN = 32


def _layer_entries(l):
    if l % 2 == 0:
        mixer = [("gdn_w_in", l // 2, D_MODEL, GDN_IN // 4, True), ("gdn_w_out", l // 2, D_MODEL // 4, D_MODEL, False)]
    else:
        mixer = [("dswa_w_in", l // 2, D_MODEL, 3 * DSWA_W // 4, True), ("dswa_w_out", l // 2, DSWA_W // 4, D_MODEL, False)]
    return mixer + [("mlp_w1", l, D_MODEL, D_FF // 4, True), ("mlp_w2", l, D_FF // 4, D_MODEL, False)]


def _layer_offsets(l):
    offs = [int(o) for o in np.cumsum([0] + [r * c // PACK_W for (_, _, r, c, _) in _layer_entries(l)])]
    return offs, -(-offs[-1] // PACK_ALIGN) * PACK_ALIGN


def _pack_layer(l, shards, dtype):
    offs, total = _layer_offsets(l)
    parts = [shards[name][li].astype(dtype).reshape(-1, PACK_W) for (name, li, _, _, _) in _layer_entries(l)]
    parts.append(jnp.zeros((total - offs[-1], PACK_W), dtype))
    return jnp.concatenate(parts, axis=0)


def _unpack_layer(l, full, own, jme):
    offs, _ = _layer_offsets(l)
    mats = []
    for e, (_, _, r, c, by_col) in enumerate(_layer_entries(l)):
        mine = own[offs[e]:offs[e + 1]]
        sh = [jnp.where(jme == j, mine, full[j, offs[e]:offs[e + 1]]).reshape(r, c) for j in range(4)]
        mats.append(jnp.concatenate(sh, axis=1 if by_col else 0))
    return mats


def _pack_layer_grads(l, grads):
    offs, total = _layer_offsets(l)
    per_chip = []
    for j in range(4):
        parts = []
        for g, (_, _, r, c, by_col) in zip(grads, _layer_entries(l)):
            sh = g[:, c * j:c * (j + 1)] if by_col else g[r * j:r * (j + 1), :]
            parts.append(sh.reshape(-1, PACK_W))
        parts.append(jnp.zeros((total - offs[-1], PACK_W), F32))
        per_chip.append(jnp.concatenate(parts, axis=0))
    return jnp.stack(per_chip)


def _unpack_shard_grads(gshs):
    out = {}
    for l, gsh in enumerate(gshs):
        offs, _ = _layer_offsets(l)
        for e, (name, _, r, c, _) in enumerate(_layer_entries(l)):
            out.setdefault(name, []).append(gsh[offs[e]:offs[e + 1]].reshape(r, c))
    return {k: jnp.stack(v) for k, v in out.items()}


def _flat_pad(t, mult=8 * LANES):
    f = t.reshape(-1)
    return jnp.pad(f, (0, (-f.shape[0]) % mult))


def kernel(x, norm_mix, norm_mlp, norm_final, rel_bias, gdn_w_in, gdn_conv_w, gdn_a_log, gdn_dt_bias, gdn_norm_w, gdn_w_out, dswa_w_in, dswa_w_out, mlp_w1, mlp_w2, loss_target, m_norm_mix, m_norm_mlp, m_norm_final, m_rel_bias, m_gdn_w_in, m_gdn_conv_w, m_gdn_a_log, m_gdn_dt_bias, m_gdn_norm_w, m_gdn_w_out, m_dswa_w_in, m_dswa_w_out, m_mlp_w1, m_mlp_w2, v_norm_mix, v_norm_mlp, v_norm_final, v_rel_bias, v_gdn_w_in, v_gdn_conv_w, v_gdn_a_log, v_gdn_dt_bias, v_gdn_norm_w, v_gdn_w_out, v_dswa_w_in, v_dswa_w_out, v_mlp_w1, v_mlp_w2):
    xi, yi, ci = lax.axis_index("x"), lax.axis_index("y"), lax.axis_index("c")
    jme = 2 * xi + yi
    big = dict(gdn_w_in=gdn_w_in, gdn_w_out=gdn_w_out, dswa_w_in=dswa_w_in, dswa_w_out=dswa_w_out, mlp_w1=mlp_w1, mlp_w2=mlp_w2)
    n_gdn = gdn_w_in.shape[0]
    conv_cols = gdn_conv_w.shape[-1]

    packs = [_pack_layer(l, big, BF16) for l in range(DEPTH)]
    convp = jnp.pad(gdn_conv_w.reshape(n_gdn * GDN_CONV, conv_cols), ((0, 16 - n_gdn * GDN_CONV), (0, 0)))
    raw0, cfull = run_job(gather_job([packs[0], convp], [True, False]), "gather_l0")
    fulls = {0: forward_to_sibling([raw0], "forward_l0")[0]}
    cfull = jnp.where((jnp.arange(4) == jme)[:, None, None], convp[None], cfull)
    conv_all = jnp.transpose(cfull[:, :n_gdn * GDN_CONV], (1, 0, 2)).reshape(n_gdn, GDN_CONV, 4 * conv_cols)
    conv_all = jnp.pad(conv_all, ((0, 0), (0, 8 - GDN_CONV), (0, 0)))
    fwd_jobs = {0: [1, 2], 2: [3]}

    xs = x[0]
    saved = []
    for l in range(DEPTH):
        w_in, w_out, w1, w2 = _unpack_layer(l, fulls[l], packs[l], jme)
        gm, gp = norm_mix[l][None], norm_mlp[l][None]
        a = l // 2
        if l % 2 == 0:
            w_in = jnp.pad(w_in, ((0, 0), (0, GDN_INP - GDN_IN)))
            job = gather_job([packs[t] for t in fwd_jobs[l]], [True] * len(fwd_jobs[l]))
            x_mid, sv, raws = gdn_fwd(xs, gm, w_in, conv_all[a], gdn_a_log[a], gdn_dt_bias[a], gdn_norm_w[a][None], w_out,
                                      f"l{l}_gdn", job)
            for t, f in zip(fwd_jobs[l], forward_to_sibling(raws, f"forward_from_l{l}")):
                fulls[t] = f
        else:
            x_mid, sv = dswa_fwd(xs, gm, w_in, w_out, rel_bias, f"l{l}_att")
        x_out, sv2 = mlp_fwd(x_mid, gp, w1, w2, f"l{l}_mlp")
        saved.append((xs, x_mid, (w_in, w_out, w1, w2), sv, sv2))
        xs = x_out

    cidx = ci.astype(jnp.int32).reshape(1)
    place = [t.astype(jnp.int32).reshape(1) for t in (xi, yi, ci)]

    def chip_partial(l, grads4):
        gpack = _pack_layer_grads(l, grads4)
        return rs_add_sibling(gpack, rs_sibling_exchange(gpack, f"rs_sibling_l{l}"), cidx, f"rs_add_sibling_l{l}",
                              BF16)

    def finish(l, recv):
        return rs_sibling_share(rs_add_chips(recv, parts[l], place, f"rs_add_chips_l{l}"), f"rs_share_l{l}")

    loss_part, dx, dxb, d_final = loss_head(xs, norm_final[None], loss_target[0], "loss_head")
    d_mix, d_mlp = [None] * DEPTH, [None] * DEPTH
    d_conv, d_alog, d_dt, d_nw = [None] * n_gdn, [None] * n_gdn, [None] * n_gdn, [None] * n_gdn
    d_rel = jnp.zeros_like(rel_bias)
    parts, gshs = {}, [None] * DEPTH
    bwd_jobs = {2: [3], 0: [2, 1]}
    for l in reversed(range(DEPTH)):
        x_in, x_mid, (w_in, w_out, w1, w2), sv, sv2 = saved[l]
        gm, gp = norm_mix[l][None], norm_mlp[l][None]
        a = l // 2
        dx, dxb, d_mlp[l], dw1, dw2 = mlp_bwd(x_mid, gp, w1, w2, sv2, dx, dxb, f"l{l}_mlp")
        if l % 2 == 0:
            job = chip_exchange_job([parts[t] for t in bwd_jobs[l]])
            dx, dxb, d_mix[l], dw_all, d_conv[a], d_alog[a], d_dt[a], d_nw[a], dwo, recvs = gdn_bwd(
                x_in, gm, w_in, conv_all[a], gdn_norm_w[a][None], w_out, sv, dx, dxb, f"l{l}_gdn", job)
            for t, rv in zip(bwd_jobs[l], recvs):
                gshs[t] = finish(t, rv)
            dwi = dw_all[:, :GDN_IN]
        else:
            dx, dxb, d_mix[l], dwi, dwo, drel = dswa_bwd(x_in, gm, w_in, w_out, rel_bias, sv, dx, dxb, f"l{l}_att")
            d_rel = d_rel + drel
        parts[l] = chip_partial(l, [dwi, dwo, dw1, dw2])
    gshs[0] = finish(0, run_job(chip_exchange_job([parts[0]]), "rs_chip_exchange_l0")[0])
    gbig = _unpack_shard_grads(gshs)

    small = [jnp.concatenate(d_mix, axis=0), jnp.concatenate(d_mlp, axis=0), d_final, d_rel,
             jnp.stack(d_conv), jnp.stack(d_alog), jnp.stack(d_dt), jnp.concatenate(d_nw, axis=0)]
    flat = [_flat_pad(t) for t in small]
    sizes = [f.shape[0] for f in flat]
    red = allreduce_small(jnp.concatenate(flat).reshape(-1, LANES)).reshape(-1)
    offs = np.cumsum([0] + sizes)
    red = [red[offs[i]:offs[i] + small[i].size].reshape(small[i].shape) for i in range(len(small))]
    g_conv_all = red[4][:, :GDN_CONV].reshape(n_gdn, GDN_CONV, 1, 4 * conv_cols)
    g_conv = lax.dynamic_slice_in_dim(g_conv_all, jme * conv_cols, conv_cols, axis=3)
    g = dict(norm_mix=red[0], norm_mlp=red[1], norm_final=red[2].reshape(norm_final.shape), rel_bias=red[3],
             gdn_conv_w=g_conv, gdn_a_log=red[5], gdn_dt_bias=red[6], gdn_norm_w=red[7][:, :GDN_DK], **gbig)

    w = dict(norm_mix=norm_mix, norm_mlp=norm_mlp, norm_final=norm_final, rel_bias=rel_bias, gdn_conv_w=gdn_conv_w,
             gdn_a_log=gdn_a_log, gdn_dt_bias=gdn_dt_bias, gdn_norm_w=gdn_norm_w, **big)
    m = dict(norm_mix=m_norm_mix, norm_mlp=m_norm_mlp, norm_final=m_norm_final, rel_bias=m_rel_bias, gdn_w_in=m_gdn_w_in,
             gdn_conv_w=m_gdn_conv_w, gdn_a_log=m_gdn_a_log, gdn_dt_bias=m_gdn_dt_bias, gdn_norm_w=m_gdn_norm_w,
             gdn_w_out=m_gdn_w_out, dswa_w_in=m_dswa_w_in, dswa_w_out=m_dswa_w_out, mlp_w1=m_mlp_w1, mlp_w2=m_mlp_w2)
    v = dict(norm_mix=v_norm_mix, norm_mlp=v_norm_mlp, norm_final=v_norm_final, rel_bias=v_rel_bias, gdn_w_in=v_gdn_w_in,
             gdn_conv_w=v_gdn_conv_w, gdn_a_log=v_gdn_a_log, gdn_dt_bias=v_gdn_dt_bias, gdn_norm_w=v_gdn_norm_w,
             gdn_w_out=v_gdn_w_out, dswa_w_in=v_dswa_w_in, dswa_w_out=v_dswa_w_out, mlp_w1=v_mlp_w1, mlp_w2=v_mlp_w2)
    names = ["norm_mix", "norm_mlp", "norm_final", "rel_bias", "gdn_w_in", "gdn_conv_w", "gdn_a_log", "gdn_dt_bias",
             "gdn_norm_w", "gdn_w_out", "dswa_w_in", "dswa_w_out", "mlp_w1", "mlp_w2"]
    upd = {n: adamw(w[n], g[n], m[n], v[n], f"adamw_{n}") for n in names}
    loss = lax.psum(loss_part[0, 0], ("x", "y", "c"))
    return (loss, dx[None], *[g[n] for n in names], *[upd[n][0] for n in names], *[upd[n][1] for n in names],
            *[upd[n][2] for n in names])
```

```python
import math
from typing import Callable, NamedTuple

import numpy as np
import jax
import jax.numpy as jnp
from jax import lax
from jax.experimental import pallas as pl
from jax.experimental.pallas import tpu as pltpu

F32 = jnp.float32
BF16 = jnp.bfloat16
HI = lax.Precision.HIGHEST
BS = pl.BlockSpec
SDS = jax.ShapeDtypeStruct
MESH = pl.DeviceIdType.MESH
ANY = BS(memory_space=pl.ANY)

D_MODEL = 1024
D_FF = 4096
DEPTH = 4
RMS_EPS = 1e-6
NEG_INF = -1e30
LANES = 128
VMEM_LIMIT = 56 << 20

GDN_H = 8
GDN_DK = 128
GDN_CONV = 5
GDN_C = 64
GDN_GC = 8
GDN_HP_FWD = 8
GDN_HP_BWD = 4
GDN_QKV = 3 * GDN_H * GDN_DK
GDN_IN = GDN_QKV + GDN_H * GDN_DK + 4 * GDN_H
GDN_INP = 4224

DSWA_CFG = ((128, 1), (512, 4), (2048, 16))
DSWA_HG = 6
DSWA_E = 64
DSWA_HEADS = 18
DSWA_W = DSWA_HEADS * DSWA_E
DSWA_HALF = 64
REL_BUCKETS = 32
REL_MAX_DIST = 1024

ADAM_LR = 0.001
ADAM_B1 = 0.9
ADAM_B2 = 0.999
ADAM_EPS = 1e-08
ADAM_WD = 0.01
ADAM_STEP = 10


def _cp(sem=None):
    return pltpu.CompilerParams(dimension_semantics=sem, vmem_limit_bytes=VMEM_LIMIT)


def _dot(a, b, prec=None):
    return jnp.dot(a, b, precision=prec, preferred_element_type=F32)


def _dot_nt(a, b, prec=None):
    return lax.dot_general(a, b, (((1,), (1,)), ((), ())), precision=prec, preferred_element_type=F32)


def _dot_tn(a, b, prec=None):
    return lax.dot_general(a, b, (((0,), (0,)), ((), ())), precision=prec, preferred_element_type=F32)


def _bf(a):
    return a.astype(BF16)


def _sigmoid(x):
    return 1.0 / (1.0 + jnp.exp(-x))


def rms_fwd(x, g, name):
    S, Dm = x.shape
    tm = min(512, S)

    def body(x_ref, g_ref, o_ref):
        xv = x_ref[...]
        r = lax.rsqrt(jnp.mean(xv * xv, axis=-1, keepdims=True) + RMS_EPS)
        o_ref[...] = (xv * r * g_ref[...]).astype(o_ref.dtype)

    return pl.pallas_call(
        body, grid=(S // tm,),
        in_specs=[BS((tm, Dm), lambda i: (i, 0)), BS((1, Dm), lambda i: (0, 0))],
        out_specs=BS((tm, Dm), lambda i: (i, 0)),
        out_shape=SDS((S, Dm), BF16), name=name, compiler_params=_cp(("parallel",)))(x, g)


def rms_bwd(x, g, dh, dres, name):
    S, Dm = x.shape
    tm = min(512, S)

    def body(x_ref, g_ref, dh_ref, dres_ref, dx_ref, dxb_ref, dg_ref):
        i = pl.program_id(0)
        xv = x_ref[...]
        r = lax.rsqrt(jnp.mean(xv * xv, axis=-1, keepdims=True) + RMS_EPS)
        n = xv * r
        dhv = dh_ref[...]
        t = dhv * g_ref[...]
        dx = dres_ref[...] + r * (t - n * jnp.mean(n * t, axis=-1, keepdims=True))
        dx_ref[...] = dx
        dxb_ref[...] = dx.astype(BF16)
        part = jnp.sum(dhv * n, axis=0, keepdims=True)

        @pl.when(i == 0)
        def _():
            dg_ref[...] = part

        @pl.when(i > 0)
        def _():
            dg_ref[...] += part

    row = BS((tm, Dm), lambda i: (i, 0))
    vec = BS((1, Dm), lambda i: (0, 0))
    return pl.pallas_call(
        body, grid=(S // tm,), in_specs=[row, vec, row, row], out_specs=[row, row, vec],
        out_shape=[SDS((S, Dm), F32), SDS((S, Dm), BF16), SDS((1, Dm), F32)],
        name=name, compiler_params=_cp(("arbitrary",)))(x, g, dh, dres)


def loss_head(x, g, tgt, name):
    S, Dm = x.shape
    tm = min(512, S)

    def body(x_ref, g_ref, t_ref, loss_ref, dx_ref, dxb_ref, dg_ref):
        i = pl.program_id(0)
        xv = x_ref[...]
        gv = g_ref[...]
        r = lax.rsqrt(jnp.mean(xv * xv, axis=-1, keepdims=True) + RMS_EPS)
        n = xv * r
        err = n * gv - t_ref[...]
        lpart = 0.5 * jnp.sum(jnp.mean(err * err, axis=-1, keepdims=True), axis=0, keepdims=True)
        dout = err * (1.0 / Dm)
        t = dout * gv
        dx = r * (t - n * jnp.mean(n * t, axis=-1, keepdims=True))
        dx_ref[...] = dx
        dxb_ref[...] = dx.astype(BF16)
        part = jnp.sum(dout * n, axis=0, keepdims=True)

        @pl.when(i == 0)
        def _():
            dg_ref[...] = part
            loss_ref[...] = lpart

        @pl.when(i > 0)
        def _():
            dg_ref[...] += part
            loss_ref[...] += lpart

    row = BS((tm, Dm), lambda i: (i, 0))
    vec = BS((1, Dm), lambda i: (0, 0))
    one = BS((1, 1), lambda i: (0, 0))
    return pl.pallas_call(
        body, grid=(S // tm,), in_specs=[row, vec, row], out_specs=[one, row, row, vec],
        out_shape=[SDS((1, 1), F32), SDS((S, Dm), F32), SDS((S, Dm), BF16), SDS((1, Dm), F32)],
        name=name, compiler_params=_cp(("arbitrary",)))(x, g, tgt)


MM_TK = 2048


def mm(a, b, *, name, ta=False, tb=False, tm=1024, tn=512, tk=None, out_dtype=F32, pre_a=None, epi=None,
       extras=()):
    M, K = (a.shape[1], a.shape[0]) if ta else a.shape
    N = b.shape[0] if tb else b.shape[1]
    if ta:
        tm = min(tm, 512)
        tk = K if tk is None else tk
    tm, tn = min(tm, M), min(tn, N)
    if tk is None:
        tk = MM_TK if K % MM_TK == 0 else K
    tk = min(tk, K)
    assert M % tm == 0 and N % tn == 0 and K % tk == 0, (name, M, N, K, tm, tn, tk)
    nk = K // tk
    ne = len(extras)
    a_spec = BS((tk, tm), lambda i, j, k: (k, i)) if ta else BS((tm, tk), lambda i, j, k: (i, k))
    b_spec = BS((tn, tk), lambda i, j, k: (j, k)) if tb else BS((tk, tn), lambda i, j, k: (k, j))
    o_spec = BS((tm, tn), lambda i, j, k: (i, j))
    dims = (((0 if ta else 1,), (1 if tb else 0,)), ((), ()))

    def body(a_ref, b_ref, *rest):
        e_refs, o_ref = rest[:ne], rest[ne]
        av = a_ref[...]
        if pre_a is not None:
            av = pre_a(av)
        p = lax.dot_general(_bf(av), _bf(b_ref[...]), dims, preferred_element_type=F32)

        def finish(acc):
            res = epi(acc, *[e[...] for e in e_refs]) if epi is not None else acc
            o_ref[...] = res.astype(o_ref.dtype)

        if nk == 1:
            finish(p)
        else:
            acc_ref = rest[ne + 1]
            k = pl.program_id(2)

            @pl.when(k == 0)
            def _():
                acc_ref[...] = p

            @pl.when(k > 0)
            def _():
                acc_ref[...] += p

            @pl.when(k == nk - 1)
            def _():
                finish(acc_ref[...])

    return pl.pallas_call(
        body, grid=(M // tm, N // tn, nk), in_specs=[a_spec, b_spec] + [o_spec] * ne, out_specs=o_spec,
        out_shape=SDS((M, N), out_dtype),
        scratch_shapes=[pltpu.VMEM((tm, tn), F32)] if nk > 1 else [],
        name=name, compiler_params=_cp(("parallel", "parallel", "arbitrary")))(a, b, *extras)


def _relu(acc):
    return jnp.maximum(acc, 0.0)


def _add(acc, res):
    return acc + res


def _sq(av):
    return av * av


def _times_2r(acc, r):
    return acc * (2.0 * r.astype(F32))


def mlp_fwd(x, g, w1, w2, tag):
    h = rms_fwd(x, g, f"{tag}_rms")
    r = mm(h, w1, name=f"{tag}_up", tn=1024, out_dtype=BF16, epi=_relu)
    xn = mm(r, w2, name=f"{tag}_down", pre_a=_sq, epi=_add, extras=(x,))
    return xn, (h, r)


def mlp_bwd(x, g, w1, w2, saved, dx, dxb, tag):
    h, r = saved
    da = mm(dxb, w2, name=f"{tag}_dact", tb=True, tn=1024, out_dtype=BF16, epi=_times_2r, extras=(r,))
    dw2 = mm(r, dxb, name=f"{tag}_dw2", ta=True, pre_a=_sq)
    dw1 = mm(h, da, name=f"{tag}_dw1", ta=True)
    dh = mm(da, w1, name=f"{tag}_dh", tb=True)
    dx, dxb, dg = rms_bwd(x, g, dh, dx, f"{tag}_rmsb")
    return dx, dxb, dg, dw1, dw2


def _conv_taps(x, S):
    t = lax.broadcasted_iota(jnp.int32, x.shape, 0)
    taps = []
    for j in range(GDN_CONV):
        sh = j - GDN_CONV // 2
        xs = x if sh == 0 else pltpu.roll(x, (-sh) % S, 0)
        taps.append(jnp.where((t + sh >= 0) & (t + sh < S), xs, 0.0))
    return taps


def _qkv_scale(c):
    is_norm = c < 2 * GDN_H
    scale = jnp.where(c < GDN_H, GDN_DK ** -0.5, 1.0)
    return is_norm, scale


def gdn_pre_fwd(proj, convw, name):
    S = proj.shape[0]

    def body(p_ref, w_ref, o_ref):
        c = pl.program_id(0)
        x = p_ref[...]
        w = w_ref[...]
        y = jnp.zeros_like(x)
        for j, xs in enumerate(_conv_taps(x, S)):
            y = y + w[j:j + 1, :] * xs
        t = y * _sigmoid(y)
        is_norm, scale = _qkv_scale(c)
        r = lax.rsqrt(jnp.sum(t * t, axis=-1, keepdims=True) + 1e-6)
        o_ref[...] = jnp.where(is_norm, t * r * scale, t)

    return pl.pallas_call(
        body, grid=(GDN_QKV // LANES,),
        in_specs=[BS((S, LANES), lambda c: (0, c)), BS((8, LANES), lambda c: (0, c))],
        out_specs=BS((S, LANES), lambda c: (0, c)),
        out_shape=SDS((S, GDN_QKV), F32), name=name, compiler_params=_cp(("parallel",)))(proj, convw)


def gdn_pre_bwd(proj, convw, dqkv, name):
    S = proj.shape[0]

    def body(p_ref, w_ref, d_ref, dp_ref, dw_ref):
        c = pl.program_id(0)
        x = p_ref[...]
        w = w_ref[...]
        taps = _conv_taps(x, S)
        y = jnp.zeros_like(x)
        for j, xs in enumerate(taps):
            y = y + w[j:j + 1, :] * xs
        sg = _sigmoid(y)
        t = y * sg
        is_norm, scale = _qkv_scale(c)
        dout = d_ref[0, 0] + d_ref[1, 0]
        r = lax.rsqrt(jnp.sum(t * t, axis=-1, keepdims=True) + 1e-6)
        n = t * r
        dn = dout * scale
        dt_norm = r * (dn - n * jnp.sum(dn * n, axis=-1, keepdims=True))
        dt = jnp.where(is_norm, dt_norm, dout)
        dy = dt * (sg * (1.0 + y * (1.0 - sg)))
        row = lax.broadcasted_iota(jnp.int32, (8, LANES), 0)
        dw = jnp.zeros((8, LANES), F32)
        for j, xs in enumerate(taps):
            dw = dw + jnp.where(row == j, jnp.sum(dy * xs, axis=0, keepdims=True), 0.0)
        dw_ref[...] = dw
        tt = lax.broadcasted_iota(jnp.int32, x.shape, 0)
        dx = jnp.zeros_like(x)
        for j in range(GDN_CONV):
            sh = j - GDN_CONV // 2
            ds = dy if sh == 0 else pltpu.roll(dy, sh % S, 0)
            dx = dx + w[j:j + 1, :] * jnp.where((tt - sh >= 0) & (tt - sh < S), ds, 0.0)
        dp_ref[...] = dx.astype(BF16)

    return pl.pallas_call(
        body, grid=(GDN_QKV // LANES,),
        in_specs=[BS((S, LANES), lambda c: (0, c)), BS((8, LANES), lambda c: (0, c)),
                  BS((2, 1, S, LANES), lambda c: (0, c // GDN_H, 0, c % GDN_H))],
        out_specs=[BS((S, LANES), lambda c: (0, c)), BS((8, LANES), lambda c: (0, c))],
        out_shape=[SDS((S, GDN_QKV), BF16), SDS((8, GDN_QKV), F32)],
        name=name, compiler_params=_cp(("parallel",)))(proj, convw, dqkv)


def _chunk_sum_matrix(n, upper):
    i = lax.broadcasted_iota(jnp.int32, (n, n), 0)
    j = lax.broadcasted_iota(jnp.int32, (n, n), 1)
    same = (i // GDN_C) == (j // GDN_C)
    tri = (i <= j) if upper else (i >= j)
    return jnp.where(same & tri, 1.0, 0.0).astype(F32)


def _gate_lanes(shape):
    lane = lax.broadcasted_iota(jnp.int32, shape, 1)
    return lane < GDN_H, (lane >= GDN_H) & (lane < 2 * GDN_H), (lane >= 2 * GDN_H) & (lane < 4 * GDN_H)


def gdn_gate_fwd(proj, prm, name):
    S = proj.shape[0]
    tm = min(512, S)
    ct = GDN_INP // LANES - 1

    def body(p_ref, prm_ref, o_ref):
        ab = p_ref[...]
        a_log = prm_ref[0:1, :]
        dtb = prm_ref[1:2, :]
        z = ab + dtb
        sp = jnp.maximum(z, 0.0) + jnp.log(1.0 + jnp.exp(-jnp.abs(z)))
        g = -jnp.exp(a_log) * sp
        is_f, is_b, is_beta = _gate_lanes(ab.shape)
        gf = _dot(_chunk_sum_matrix(tm, False), jnp.where(is_f, g, 0.0), HI)
        gbk = _dot(_chunk_sum_matrix(tm, True), jnp.where(is_b, g, 0.0), HI)
        o_ref[...] = gf + gbk + jnp.where(is_beta, _sigmoid(ab), 0.0)

    return pl.pallas_call(
        body, grid=(S // tm,),
        in_specs=[BS((tm, LANES), lambda i: (i, ct)), BS((8, LANES), lambda i: (0, 0))],
        out_specs=BS((tm, LANES), lambda i: (i, 0)),
        out_shape=SDS((S, LANES), F32), name=name, compiler_params=_cp(("parallel",)))(proj, prm)


def gdn_gate_bwd(proj, prm, dgb, name):
    S = proj.shape[0]
    tm = min(512, S)
    ct = GDN_INP // LANES - 1

    def body(p_ref, prm_ref, d_ref, dab_ref, dprm_ref):
        i = pl.program_id(0)
        ab = p_ref[...]
        a_log = prm_ref[0:1, :]
        dtb = prm_ref[1:2, :]
        z = ab + dtb
        sp = jnp.maximum(z, 0.0) + jnp.log(1.0 + jnp.exp(-jnp.abs(z)))
        ea = jnp.exp(a_log)
        g = -ea * sp
        is_f, is_b, is_beta = _gate_lanes(ab.shape)
        d = d_ref[...]
        dg = (_dot_tn(_chunk_sum_matrix(tm, False), jnp.where(is_f, d, 0.0), HI)
              + _dot_tn(_chunk_sum_matrix(tm, True), jnp.where(is_b, d, 0.0), HI))
        da = dg * (-ea) * _sigmoid(z)
        beta = _sigmoid(ab)
        dab_ref[...] = jnp.where(is_beta, d * beta * (1.0 - beta), da).astype(BF16)
        row = lax.broadcasted_iota(jnp.int32, (8, LANES), 0)
        part = (jnp.where(row == 0, jnp.sum(dg * g, axis=0, keepdims=True), 0.0)
                + jnp.where(row == 1, jnp.sum(da, axis=0, keepdims=True), 0.0))

        @pl.when(i == 0)
        def _():
            dprm_ref[...] = part

        @pl.when(i > 0)
        def _():
            dprm_ref[...] += part

    return pl.pallas_call(
        body, grid=(S // tm,),
        in_specs=[BS((tm, LANES), lambda i: (i, ct)), BS((8, LANES), lambda i: (0, 0)), BS((tm, LANES), lambda i: (i, 0))],
        out_specs=[BS((tm, LANES), lambda i: (i, 0)), BS((8, LANES), lambda i: (0, 0))],
        out_shape=[SDS((S, LANES), BF16), SDS((8, LANES), F32)],
        name=name, compiler_params=_cp(("arbitrary",)))(proj, prm, dgb)


def _tri_masks(d):
    i = lax.broadcasted_iota(jnp.int32, (GDN_C, GDN_C), 0)
    j = lax.broadcasted_iota(jnp.int32, (GDN_C, GDN_C), 1)
    s = (i - j) * (1 - 2 * d)
    return s >= 0, s > 0


def _split(a):
    hi = _bf(a)
    return hi, _bf(a - hi.astype(F32))


def _dot3(a, b):
    return _dot(a[0], b[0]) + (_dot(a[0], b[1]) + _dot(a[1], b[0]))


def _inv_unit_tri_many(mats):
    i = lax.broadcasted_iota(jnp.int32, mats[0].shape, 0)
    j = lax.broadcasted_iota(jnp.int32, mats[0].shape, 1)
    eye = jnp.where(i == j, 1.0, 0.0)
    ms = [-a for a in mats]
    ps = [eye + m for m in ms]
    for _ in range(int(math.log2(GDN_C)) - 1):
        sp = [_split(m) for m in ms]
        ms = [_dot3(s, s) for s in sp]
        sp = [_split(m) for m in ms]
        pp = [_split(p) for p in ps]
        ps = [p + _dot3(a, b) for p, a, b in zip(ps, pp, sp)]
    return ps


def _lane_col(x, lane_idx):
    lane = lax.broadcasted_iota(jnp.int32, x.shape, 1)
    return jnp.sum(jnp.where(lane == lane_idx, x, 0.0), axis=1, keepdims=True)


def _chunk_gates(gb_ref, grow_ref, hh, ci, d, head):
    gbv = gb_ref[ci * GDN_C:(ci + 1) * GDN_C, :]
    gcol = _lane_col(gbv, d * GDN_H + head)
    bcol = _lane_col(gbv, 2 * GDN_H + d * GDN_H + head)
    glast = jnp.where(d == 0, gcol[GDN_C - 1:GDN_C, :], gcol[0:1, :])
    return gcol, bcol, grow_ref[hh, ci:ci + 1, :], glast


def _chunk_base(q, k, gcol, grow, bcol, glast, d):
    incl, strict = _tri_masks(d)
    decay = jnp.where(incl, jnp.exp(jnp.where(incl, gcol - grow, 0.0)), 0.0)
    kb = k * bcol
    kk = _dot_nt(_bf(kb), _bf(k))
    qk = _dot_nt(_bf(q), _bf(k))
    eg = jnp.exp(gcol)
    ek = jnp.exp(glast - gcol)
    return dict(incl=incl, strict=strict, decay=decay, kb=kb, kk=kk, qk=qk, eg=eg, ek=ek, q_dec=q * eg, k_dec=k * ek,
                bcol=bcol, glast=glast)


def _block_terms(q_ref, k_ref, v_ref, gb_ref, grow_ref, d, h, hp):
    keys = [(hh, ci) for hh in range(hp) for ci in range(GDN_GC)]
    ts = []
    for hh, ci in keys:
        rows = slice(ci * GDN_C, (ci + 1) * GDN_C)
        cols = slice(hh * GDN_DK, (hh + 1) * GDN_DK)
        gcol, bcol, grow_v, glast = _chunk_gates(gb_ref, grow_ref, hh, ci, d, h * hp + hh)
        t = _chunk_base(q_ref[rows, cols], k_ref[rows, cols], gcol, grow_v, bcol, glast, d)
        t["v"] = v_ref[rows, cols]
        ts.append(t)
    tinvs = _inv_unit_tri_many([jnp.where(t["strict"], t["kk"] * t["decay"], 0.0) for t in ts])
    sp = [_split(x) for x in tinvs]
    us = [_dot3(s, _split(t["v"] * t["bcol"])) for s, t in zip(sp, ts)]
    ws = [_dot3(s, _split(t["kb"] * t["eg"])) for s, t in zip(sp, ts)]
    for t, tinv, u, w in zip(ts, tinvs, us, ws):
        t.update(tinv=tinv, u=u, w=w)
    return keys, ts


def _gdn_specs(S, nblk, order, hp):
    R = GDN_GC * GDN_C
    wd = hp * GDN_DK
    hb = GDN_H // hp

    def qkv_spec(part):
        return BS((R, wd), lambda d, h, n: (order(d, n), part * hb + h))

    gb_spec = BS((R, LANES), lambda d, h, n: (order(d, n), 0))
    grow_spec = BS((hp, GDN_GC, GDN_C), lambda d, h, n: (d * hb + h, order(d, n), 0))
    st_spec = BS((1, hp, GDN_GC, GDN_DK, GDN_DK), lambda d, h, n: (d, h, order(d, n), 0, 0))
    return qkv_spec, gb_spec, grow_spec, st_spec


def _lane_row(x):
    return jnp.broadcast_to(x, (1, LANES))


def _side_parts(side):
    if side is None:
        return [], [], [], [], []
    return [ANY] * len(side.ins), [ANY] * len(side.outs), list(side.outs), list(side.sems), list(side.ins)


def _side_run(side, refs, n_in, n_out, n_scr, first, last):
    if side is None:
        return
    ns, no, nm = len(side.ins), len(side.outs), len(side.sems)
    s_in = refs[n_in:n_in + ns]
    s_out = refs[n_in + ns + n_out:n_in + ns + n_out + no]
    s_sem = refs[len(refs) - nm:]

    @pl.when(first)
    def _():
        side.start(s_in, s_out, s_sem)

    @pl.when(last)
    def _():
        side.wait(s_in, s_out, s_sem)


def gdn_scan_fwd(qkv, gb, grow, name, side=None):
    S = qkv.shape[0]
    R = GDN_GC * GDN_C
    nblk = S // R
    nc = S // GDN_C
    hp = GDN_HP_FWD
    wd = hp * GDN_DK
    heads = range(hp)

    def order(d, n):
        return n + d * (nblk - 1 - 2 * n)

    qkv_spec, gb_spec, grow_spec, st_spec = _gdn_specs(S, nblk, order, hp)

    s_in, s_out, s_shape, s_scr_shapes, s_ops = _side_parts(side)
    hb = GDN_H // hp

    def body(*refs):
        q_ref, k_ref, v_ref, gb_ref, grow_ref = refs[:5]
        o_ref, st_ref = refs[5 + len(s_in):7 + len(s_in)]
        s_scr, u_scr, w_scr, qd_scr, kd_scr, in_scr, egl_scr = refs[7 + len(s_in) + len(s_out):14 + len(s_in) + len(s_out)]
        d = pl.program_id(0)
        h = pl.program_id(1)
        n = pl.program_id(2)
        _side_run(side, refs, 5, 2, 7, (d == 0) & (h == 0) & (n == 0), (d == 1) & (h == hb - 1) & (n == nblk - 1))

        @pl.when(n == 0)
        def _():
            s_scr[...] = jnp.zeros_like(s_scr)

        keys, ts = _block_terms(q_ref, k_ref, v_ref, gb_ref, grow_ref, d, h, hp)
        for (hh, ci), t in zip(keys, ts):
            u_scr[hh, ci] = t["u"]
            w_scr[hh, ci] = _bf(t["w"])
            qd_scr[hh, ci] = _bf(t["q_dec"])
            kd_scr[hh, ci] = _bf(t["k_dec"])
            in_scr[hh, ci] = _bf(jnp.where(t["incl"], t["qk"] * t["decay"], 0.0))
            egl_scr[hh, ci] = _lane_row(jnp.exp(t["glast"]))

        def chunk(cc, carry):
            ci = cc + d * (GDN_GC - 1 - 2 * cc)
            rows = pl.ds(pl.multiple_of(ci * GDN_C, GDN_C), GDN_C)
            sts = [s_scr[hh] for hh in heads]
            for hh in heads:
                st_ref[0, hh, ci] = sts[hh]
            sbs = [_bf(st) for st in sts]
            vns = [_bf(u_scr[hh, ci] - _dot(w_scr[hh, ci], sbs[hh])) for hh in heads]
            for hh in heads:
                s_scr[hh] = sts[hh] * egl_scr[hh, ci] + _dot_tn(kd_scr[hh, ci], vns[hh])
            for hh in heads:
                o_ref[0, rows, hh * GDN_DK:(hh + 1) * GDN_DK] = _dot(qd_scr[hh, ci], sbs[hh]) + _dot(in_scr[hh, ci], vns[hh])
            return carry

        lax.fori_loop(0, GDN_GC, chunk, 0)

    blk = (hp, GDN_GC, GDN_C, GDN_DK)
    return pl.pallas_call(
        body, grid=(2, GDN_H // hp, nblk),
        in_specs=[qkv_spec(0), qkv_spec(1), qkv_spec(2), gb_spec, grow_spec] + s_in,
        out_specs=[BS((1, R, wd), lambda d, h, n: (d, order(d, n), h)), st_spec] + s_out,
        out_shape=[SDS((2, S, GDN_H * GDN_DK), F32), SDS((2, GDN_H, nc, GDN_DK, GDN_DK), F32)] + s_shape,
        scratch_shapes=[pltpu.VMEM((hp, GDN_DK, GDN_DK), F32), pltpu.VMEM(blk, F32), pltpu.VMEM(blk, BF16),
                        pltpu.VMEM(blk, BF16), pltpu.VMEM(blk, BF16), pltpu.VMEM((hp, GDN_GC, GDN_C, GDN_C), BF16),
                        pltpu.VMEM((hp, GDN_GC, 1, LANES), F32)] + s_scr_shapes,
        name=name, compiler_params=_cp(("arbitrary", "arbitrary", "arbitrary")))(qkv, qkv, qkv, gb, grow, *s_ops)


def gdn_scan_bwd(qkv, gb, grow, states, do, name, side=None):
    S = qkv.shape[0]
    R = GDN_GC * GDN_C
    nblk = S // R
    hp = GDN_HP_BWD
    wd = hp * GDN_DK
    heads = range(hp)

    def order(d, n):
        return (nblk - 1 - n) - d * (nblk - 1 - 2 * n)

    qkv_spec, gb_spec, grow_spec, st_spec = _gdn_specs(S, nblk, order, hp)

    s_in, s_out, s_shape, s_scr_shapes, s_ops = _side_parts(side)
    hb = GDN_H // hp

    def body(*refs):
        q_ref, k_ref, v_ref, gb_ref, grow_ref, st_ref, do_ref = refs[:7]
        dqkv_ref, dgate_ref = refs[7 + len(s_in):9 + len(s_in)]
        (ds_scr, w_scr, kd_scr, dv1_scr, qtdo_scr, egl_scr, dsin_scr, dvn_scr,
         sdot_scr) = refs[9 + len(s_in) + len(s_out):18 + len(s_in) + len(s_out)]
        d = pl.program_id(0)
        h = pl.program_id(1)
        n = pl.program_id(2)
        _side_run(side, refs, 7, 2, 9, (d == 0) & (h == 0) & (n == 0), (d == 1) & (h == hb - 1) & (n == nblk - 1))

        @pl.when(n == 0)
        def _():
            ds_scr[...] = jnp.zeros_like(ds_scr)

        keys, ts = _block_terms(q_ref, k_ref, v_ref, gb_ref, grow_ref, d, h, hp)
        for (hh, ci), t in zip(keys, ts):
            rows = slice(ci * GDN_C, (ci + 1) * GDN_C)
            t["wb"] = _bf(t["w"])
            t["dob"] = _bf(do_ref[rows, hh * GDN_DK:(hh + 1) * GDN_DK])
            t["sb"] = _bf(st_ref[0, hh, ci])
        for (hh, ci), t in zip(keys, ts):
            t["vnb"] = _bf(t["u"] - _dot(t["wb"], t["sb"]))
            w_scr[hh, ci] = t["wb"]
            kd_scr[hh, ci] = _bf(t["k_dec"])
            dv1_scr[hh, ci] = _dot_tn(_bf(jnp.where(t["incl"], t["qk"] * t["decay"], 0.0)), t["dob"])
            qtdo_scr[hh, ci] = _dot_tn(_bf(t["q_dec"]), t["dob"])
            egl_scr[hh, ci] = _lane_row(jnp.exp(t["glast"]))

        def chunk(cc, carry):
            ci = (GDN_GC - 1 - cc) - d * (GDN_GC - 1 - 2 * cc)
            dsns = [ds_scr[hh] for hh in heads]
            dsbs = [_bf(x) for x in dsns]
            dvns = [dv1_scr[hh, ci] + _dot(kd_scr[hh, ci], dsbs[hh]) for hh in heads]
            for hh in heads:
                ds_scr[hh] = qtdo_scr[hh, ci] + egl_scr[hh, ci] * dsns[hh] - _dot_tn(w_scr[hh, ci], _bf(dvns[hh]))
            for hh in heads:
                dsin_scr[hh, ci] = dsbs[hh]
                dvn_scr[hh, ci] = dvns[hh]
                sd = jnp.sum(jnp.sum(st_ref[0, hh, ci] * dsns[hh], axis=1, keepdims=True), axis=0, keepdims=True)
                sdot_scr[hh, ci] = _lane_row(sd)
            return carry

        lax.fori_loop(0, GDN_GC, chunk, 0)

        for (hh, ci), t in zip(keys, ts):
            t["d_vnew"] = dvn_scr[hh, ci]
            t["dvb"] = _bf(t["d_vnew"])
            t["dsb"] = dsin_scr[hh, ci]
        for t in ts:
            t["d_intra"] = jnp.where(t["incl"], _dot_nt(t["dob"], t["vnb"]), 0.0)
            t["d_qdec"] = _dot_nt(t["dob"], t["sb"])
            t["d_kdec"] = _dot_nt(t["vnb"], t["dsb"])
            t["dw"] = -_dot_nt(t["dvb"], t["sb"])
        for t in ts:
            tts = _split(t["tinv"].T)
            t["d_ru"] = _dot3(tts, _split(t["d_vnew"]))
            t["d_rw"] = _dot3(tts, _split(t["dw"]))
        for t in ts:
            t["da"] = -jnp.where(t["strict"], _dot_nt(_bf(t["d_ru"]), _bf(t["u"])) + _dot_nt(_bf(t["d_rw"]), t["wb"]), 0.0)
        for (hh, ci), t in zip(keys, ts):
            rows = slice(ci * GDN_C, (ci + 1) * GDN_C)
            cols = slice(hh * GDN_DK, (hh + 1) * GDN_DK)
            q, k, v = q_ref[rows, cols], k_ref[rows, cols], t["v"]
            decay, kb, eg, ek, bcol = t["decay"], t["kb"], t["eg"], t["ek"], t["bcol"]
            d_ru, d_rw, da, d_intra, d_qdec, d_kdec = t["d_ru"], t["d_rw"], t["da"], t["d_intra"], t["d_qdec"], t["d_kdec"]
            kbf, qbf = _bf(k), _bf(q)
            dgl = egl_scr[hh, ci][:, 0:1] * sdot_scr[hh, ci][:, 0:1]
            dv = d_ru * bcol
            dbeta = jnp.sum(d_ru * v, axis=1, keepdims=True)
            dkb = d_rw * eg
            dg = jnp.sum(d_rw * kb, axis=1, keepdims=True) * eg
            dkk = _bf(da * decay)
            dqk = _bf(d_intra * decay)
            dkb = dkb + _dot(dkk, kbf)
            dk = _dot_tn(dkk, _bf(kb)) + _dot_tn(dqk, qbf)
            dq = _dot(dqk, kbf) + d_qdec * eg
            dd = (da * t["kk"] + d_intra * t["qk"]) * decay
            dg = dg + jnp.sum(dd, axis=1, keepdims=True) - jnp.sum(dd.T, axis=1, keepdims=True)
            dg = dg + jnp.sum(d_qdec * t["q_dec"], axis=1, keepdims=True)
            dk = dk + d_kdec * ek
            ee = jnp.sum(d_kdec * t["k_dec"], axis=1, keepdims=True)
            dg = dg - ee
            dgl = dgl + jnp.sum(ee, axis=0, keepdims=True)
            dk = dk + dkb * bcol
            dbeta = dbeta + jnp.sum(dkb * k, axis=1, keepdims=True)
            ridx = lax.broadcasted_iota(jnp.int32, (GDN_C, 1), 0)
            dg = dg + jnp.where(ridx == (GDN_C - 1) * (1 - d), dgl, 0.0)
            dqkv_ref[0, 0, rows, cols] = dq
            dqkv_ref[0, 1, rows, cols] = dk
            dqkv_ref[0, 2, rows, cols] = dv
            lane2 = lax.broadcasted_iota(jnp.int32, (GDN_C, 2), 1)
            dgate_ref[0, hh, rows, :] = jnp.where(lane2 == 0, dg, dbeta)

    blk = (hp, GDN_GC, GDN_C, GDN_DK)
    sq = (hp, GDN_GC, GDN_DK, GDN_DK)
    row = (hp, GDN_GC, 1, LANES)
    return pl.pallas_call(
        body, grid=(2, GDN_H // hp, nblk),
        in_specs=[qkv_spec(0), qkv_spec(1), qkv_spec(2), gb_spec, grow_spec, st_spec,
                  BS((R, wd), lambda d, h, n: (order(d, n), h))] + s_in,
        out_specs=[BS((1, 3, R, wd), lambda d, h, n: (d, 0, order(d, n), h)),
                   BS((1, hp, R, 2), lambda d, h, n: (d, h, order(d, n), 0))] + s_out,
        out_shape=[SDS((2, 3, S, GDN_H * GDN_DK), F32), SDS((2, GDN_H, S, 2), F32)] + s_shape,
        scratch_shapes=[pltpu.VMEM((hp, GDN_DK, GDN_DK), F32), pltpu.VMEM(blk, BF16), pltpu.VMEM(blk, BF16),
                        pltpu.VMEM(blk, F32), pltpu.VMEM(sq, F32), pltpu.VMEM(row, F32), pltpu.VMEM(sq, BF16),
                        pltpu.VMEM(blk, F32), pltpu.VMEM(row, F32)] + s_scr_shapes,
        name=name, compiler_params=_cp(("arbitrary", "arbitrary", "arbitrary")))(qkv, qkv, qkv, gb, grow, states, do, *s_ops)


def gdn_post_fwd(o2, proj, nw, name):
    S = proj.shape[0]
    tm = min(512, S)
    zoff = GDN_QKV // LANES

    def body(o_ref, z_ref, nw_ref, y_ref):
        o = o_ref[0] + o_ref[1]
        z = z_ref[...]
        r = lax.rsqrt(jnp.mean(o * o, axis=-1, keepdims=True) + RMS_EPS)
        y_ref[...] = (o * r * nw_ref[...] * (z * _sigmoid(z))).astype(BF16)

    return pl.pallas_call(
        body, grid=(S // tm, GDN_H),
        in_specs=[BS((2, tm, LANES), lambda i, h: (0, i, h)), BS((tm, LANES), lambda i, h: (i, zoff + h)),
                  BS((1, LANES), lambda i, h: (0, 0))],
        out_specs=BS((tm, LANES), lambda i, h: (i, h)),
        out_shape=SDS((S, GDN_H * GDN_DK), BF16), name=name, compiler_params=_cp(("parallel", "parallel")))(o2, proj, nw)


def gdn_post_bwd(o2, proj, nw, dy, name):
    S = proj.shape[0]
    tm = min(512, S)
    zoff = GDN_QKV // LANES

    def body(o_ref, z_ref, nw_ref, dy_ref, do_ref, dz_ref, dnw_ref):
        first = (pl.program_id(0) == 0) & (pl.program_id(1) == 0)
        o = o_ref[0] + o_ref[1]
        z = z_ref[...]
        nwv = nw_ref[...]
        dyv = dy_ref[...]
        r = lax.rsqrt(jnp.mean(o * o, axis=-1, keepdims=True) + RMS_EPS)
        n = o * r
        sg = _sigmoid(z)
        sz = z * sg
        dz_ref[...] = (dyv * n * nwv * (sg * (1.0 + z * (1.0 - sg)))).astype(BF16)
        dn = dyv * nwv * sz
        do_ref[...] = r * (dn - n * jnp.mean(dn * n, axis=-1, keepdims=True))
        part = jnp.sum(dyv * n * sz, axis=0, keepdims=True)

        @pl.when(first)
        def _():
            dnw_ref[...] = part

        @pl.when(jnp.logical_not(first))
        def _():
            dnw_ref[...] += part

    blk = BS((tm, LANES), lambda i, h: (i, h))
    return pl.pallas_call(
        body, grid=(S // tm, GDN_H),
        in_specs=[BS((2, tm, LANES), lambda i, h: (0, i, h)), BS((tm, LANES), lambda i, h: (i, zoff + h)),
                  BS((1, LANES), lambda i, h: (0, 0)), blk],
        out_specs=[blk, blk, BS((1, LANES), lambda i, h: (0, 0))],
        out_shape=[SDS((S, GDN_H * GDN_DK), F32), SDS((S, GDN_H * GDN_DK), BF16), SDS((1, LANES), F32)],
        name=name, compiler_params=_cp(("arbitrary", "arbitrary")))(o2, proj, nw, dy)


def _gate_prm(a_log, dt_bias):
    z = jnp.zeros((8, LANES), F32)
    z = z.at[0, :2 * GDN_H].set(a_log.reshape(-1))
    return z.at[1, :2 * GDN_H].set(dt_bias.reshape(-1))


def gdn_fwd(x, g, w_all, convw, a_log, dt_bias, nw, w_out, tag, side=None):
    S = x.shape[0]
    h = rms_fwd(x, g, f"{tag}_rms")
    proj = mm(h, w_all, name=f"{tag}_proj", tn=1408)
    qkv = gdn_pre_fwd(proj, convw, f"{tag}_pre")
    prm = _gate_prm(a_log, dt_bias)
    gb = gdn_gate_fwd(proj, prm, f"{tag}_gate")
    grow = gb[:, :2 * GDN_H].T.reshape(2 * GDN_H, S // GDN_C, GDN_C)
    o2, states, *side_out = gdn_scan_fwd(qkv, gb, grow, f"{tag}_scan", side)
    y = gdn_post_fwd(o2, proj, nw, f"{tag}_post")
    xn = mm(y, w_out, name=f"{tag}_out", epi=_add, extras=(x,))
    return xn, (h, proj, qkv, prm, gb, grow, o2, states, y), side_out


def gdn_bwd(x, g, w_all, convw, nw, w_out, saved, dx, dxb, tag, side=None):
    S = x.shape[0]
    h, proj, qkv, prm, gb, grow, o2, states, y = saved
    dw_out = mm(y, dxb, name=f"{tag}_dwout", ta=True)
    dy = mm(dxb, w_out, name=f"{tag}_dy", tb=True)
    do, dz, dnw = gdn_post_bwd(o2, proj, nw, dy, f"{tag}_postb")
    dqkv, dgate, *side_out = gdn_scan_bwd(qkv, gb, grow, states, do, f"{tag}_scanb", side)
    dgb = jnp.transpose(dgate, (2, 3, 0, 1)).reshape(S, 4 * GDN_H)
    dgb = jnp.pad(dgb, ((0, 0), (0, LANES - 4 * GDN_H)))
    dab, dprm = gdn_gate_bwd(proj, prm, dgb, f"{tag}_gateb")
    dpq, dconvw = gdn_pre_bwd(proj, convw, dqkv, f"{tag}_preb")
    dproj = jnp.concatenate([dpq, dz, dab], axis=1)
    dw_all = mm(h, dproj, name=f"{tag}_dwin", ta=True, tn=384)
    dh = mm(dproj, w_all, name=f"{tag}_dh", tb=True, tk=1408)
    dx, dxb, dg = rms_bwd(x, g, dh, dx, f"{tag}_rmsb")
    da_log = dprm[0, :2 * GDN_H].reshape(2, GDN_H)
    ddt = dprm[1, :2 * GDN_H].reshape(2, GDN_H)
    return dx, dxb, dg, dw_all, dconvw, da_log, ddt, dnw, dw_out, side_out


def _rel_bucket_np(rel):
    nb = REL_BUCKETS // 2
    max_exact = nb // 2
    ret = np.where(rel > 0, nb, 0)
    n = np.abs(rel)
    nf = np.maximum(n, 1).astype(np.float32)
    large = max_exact + (np.log(nf / max_exact) / np.float32(math.log(REL_MAX_DIST / max_exact))
                         * (nb - max_exact)).astype(np.int32)
    large = np.minimum(large, nb - 1)
    return ret + np.where(n < max_exact, n, large)


def _toeplitz(f, rows, cols):
    period = rows + cols
    e = jnp.pad(f, ((0, 0), (0, period - f.shape[1])))
    y = jnp.tile(e, (1, rows))[:, :rows * (period - 1)]
    return y.reshape(f.shape[0], rows, period - 1)[:, :, :cols]


ATT_Q = DSWA_HALF
ATT_W = 3 * DSWA_HALF
ATT_TB = 1024
ATT_PAIRS = DSWA_HG // 2


def _bias_mats(rel_table, gi):
    _, dil = DSWA_CFG[gi]
    offs = np.arange(-DSWA_HALF, DSWA_HALF + 1)
    onehot = jnp.asarray(np.eye(REL_BUCKETS, dtype=np.float32)[_rel_bucket_np(offs * dil)])
    f = jnp.dot(onehot, rel_table, precision=HI)[:, gi * DSWA_HG:(gi + 1) * DSWA_HG].T
    bias = _toeplitz(f, ATT_Q, ATT_W)
    bias_t = jnp.transpose(_toeplitz(f[:, ::-1], ATT_Q, ATT_W), (0, 2, 1))
    return bias.reshape(ATT_PAIRS, 2, ATT_Q, ATT_W), bias_t.reshape(ATT_PAIRS, 2, ATT_W, ATT_Q)


def _att_specs(S, d, col):
    halo = DSWA_HALF * d
    per = ATT_TB // halo
    last = S // halo - 1
    cur = BS((ATT_TB, LANES), lambda p, tb: (tb, col(p)))
    prev = BS((halo, LANES), lambda p, tb: (jnp.maximum(tb * per - 1, 0), col(p)))
    nxt = BS((halo, LANES), lambda p, tb: (jnp.minimum((tb + 1) * per, last), col(p)))
    return prev, cur, nxt


def _att_specs3(S, d, lead):
    halo = DSWA_HALF * d
    per = ATT_TB // halo
    last = S // halo - 1
    cur = BS((1, ATT_TB, LANES), lambda p, tb: (lead(p), tb, 0))
    prev = BS((1, halo, LANES), lambda p, tb: (lead(p), jnp.maximum(tb * per - 1, 0), 0))
    nxt = BS((1, halo, LANES), lambda p, tb: (lead(p), jnp.minimum((tb + 1) * per, last), 0))
    return prev, cur, nxt


class _Pieces:
    def __init__(self, prev, cur, nxt, d, lead=None, cast=None):
        self.refs, self.d, self.lead, self.cast, self.cache = (prev, cur, nxt), d, lead, cast, {}
        self.halo = DSWA_HALF * d
        self.nsb = ATT_TB // self.halo

    def __call__(self, r, sb):
        if (r, sb) not in self.cache:
            ref = self.refs[0] if sb < 0 else self.refs[2] if sb >= self.nsb else self.refs[1]
            start = r + (self.halo * sb if 0 <= sb < self.nsb else 0)
            rows = pl.ds(start, ATT_Q, stride=self.d) if self.d > 1 else pl.ds(start, ATT_Q)
            v = ref[rows, :] if self.lead is None else ref[0, rows, :]
            self.cache[(r, sb)] = v if self.cast is None else v.astype(self.cast)
        return self.cache[(r, sb)]

    def window(self, r, sb):
        return jnp.concatenate([self(r, sb - 1), self(r, sb), self(r, sb + 1)], axis=0)


ATT_GROUP = 8


def _tile_groups(d, nsb):
    tiles = [(r, sb) for r in range(d) for sb in range(nsb)]
    return [tiles[i:i + ATT_GROUP] for i in range(0, len(tiles), ATT_GROUP)]


def _tile_rows(r, sb, d):
    start = r + DSWA_HALF * d * sb
    return pl.ds(start, ATT_Q, stride=d) if d > 1 else pl.ds(start, ATT_Q)


def _tile_valid(tb, r, sb, d, S, transposed):
    shape = (ATT_W, ATT_Q) if transposed else (ATT_Q, ATT_W)
    blk = lax.broadcasted_iota(jnp.int32, shape, 1 if transposed else 0)
    win = lax.broadcasted_iota(jnp.int32, shape, 0 if transposed else 1)
    tok = tb * ATT_TB + r + d * (DSWA_HALF * (sb - 1) + win)
    return (jnp.abs(win - DSWA_HALF - blk) <= DSWA_HALF) & (tok >= 0) & (tok < S)


def _head_masks():
    lane = lax.broadcasted_iota(jnp.int32, (1, LANES), 1)
    return [lane < DSWA_E, lane >= DSWA_E], lane


def attn_fwd(qkv, bias, gi, name):
    S = qkv.shape[0]
    d = DSWA_CFG[gi][1]
    nsb = ATT_TB // (DSWA_HALF * d)
    npair = DSWA_HEADS // 2
    q_spec = _att_specs(S, d, lambda p: gi * ATT_PAIRS + p)[1]
    k_specs = _att_specs(S, d, lambda p: npair + gi * ATT_PAIRS + p)
    v_specs = _att_specs(S, d, lambda p: 2 * npair + gi * ATT_PAIRS + p)

    def body(q_ref, kp, kc, kn, vp, vc, vn, b_ref, o_ref, lse_ref):
        tb = pl.program_id(1)
        masks, lane = _head_masks()
        kpc = _Pieces(kp, kc, kn, d, cast=BF16)
        vpc = _Pieces(vp, vc, vn, d, cast=BF16)
        scale = DSWA_E ** -0.5
        for grp in _tile_groups(d, nsb):
            rows = [_tile_rows(r, sb, d) for r, sb in grp]
            qs = [q_ref[rw, :] for rw in rows]
            kws = [kpc.window(r, sb) for r, sb in grp]
            vws = [vpc.window(r, sb) for r, sb in grp]
            valids = [_tile_valid(tb, r, sb, d, S, False) for r, sb in grp]
            both = [(t, hh) for t in range(len(grp)) for hh in range(2)]
            ss = [_dot_nt(_bf(jnp.where(masks[hh], qs[t], 0.0)), kws[t]) * scale + b_ref[0, hh] for t, hh in both]
            ss = [jnp.where(valids[t], s, NEG_INF) for (t, hh), s in zip(both, ss)]
            ms = [jnp.max(s, axis=-1, keepdims=True) for s in ss]
            ps = [jnp.exp(s - m) for s, m in zip(ss, ms)]
            ls = [jnp.sum(p, axis=-1, keepdims=True) for p in ps]
            os = [_dot(_bf(p / l), vws[t]) for (t, hh), p, l in zip(both, ps, ls)]
            for t, rw in enumerate(rows):
                o_ref[rw, :] = jnp.where(masks[0], os[2 * t], os[2 * t + 1])
                lse_ref[0, rw, :] = (jnp.where(lane == 0, ms[2 * t] + jnp.log(ls[2 * t]), 0.0)
                                     + jnp.where(lane == 1, ms[2 * t + 1] + jnp.log(ls[2 * t + 1]), 0.0))

    return pl.pallas_call(
        body, grid=(ATT_PAIRS, S // ATT_TB),
        in_specs=[q_spec, *k_specs, *v_specs, BS((1, 2, ATT_Q, ATT_W), lambda p, tb: (p, 0, 0, 0))],
        out_specs=[BS((ATT_TB, LANES), lambda p, tb: (tb, p)), BS((1, ATT_TB, LANES), lambda p, tb: (p, tb, 0))],
        out_shape=[SDS((S, DSWA_HG * DSWA_E), F32), SDS((ATT_PAIRS, S, LANES), F32)],
        name=name, compiler_params=_cp(("parallel", "parallel")))(qkv, qkv, qkv, qkv, qkv, qkv, qkv, bias)


def attn_bwd_q(qkv, bias, lse, do, dd, gi, name):
    S = qkv.shape[0]
    d = DSWA_CFG[gi][1]
    nsb = ATT_TB // (DSWA_HALF * d)
    npair = DSWA_HEADS // 2
    q_spec = _att_specs(S, d, lambda p: gi * ATT_PAIRS + p)[1]
    k_specs = _att_specs(S, d, lambda p: npair + gi * ATT_PAIRS + p)
    v_specs = _att_specs(S, d, lambda p: 2 * npair + gi * ATT_PAIRS + p)
    bspec = BS((1, 2, ATT_Q, ATT_W), lambda p, tb: (p, 0, 0, 0))

    def body(q_ref, kp, kc, kn, vp, vc, vn, b_ref, lse_ref, do_ref, dd_ref, dq_ref, db_ref):
        tb = pl.program_id(1)
        masks, lane = _head_masks()
        kpc = _Pieces(kp, kc, kn, d, cast=BF16)
        vpc = _Pieces(vp, vc, vn, d, cast=BF16)
        db = [jnp.zeros((ATT_Q, ATT_W), F32), jnp.zeros((ATT_Q, ATT_W), F32)]
        scale = DSWA_E ** -0.5
        for grp in _tile_groups(d, nsb):
            rows = [_tile_rows(r, sb, d) for r, sb in grp]
            qs = [q_ref[rw, :] for rw in rows]
            dos = [do_ref[0, rw, :] for rw in rows]
            lses = [lse_ref[0, rw, :] for rw in rows]
            dds = [dd_ref[0, 0, rw, :] for rw in rows]
            kws = [kpc.window(r, sb) for r, sb in grp]
            vws = [vpc.window(r, sb) for r, sb in grp]
            valids = [_tile_valid(tb, r, sb, d, S, False) for r, sb in grp]
            both = [(t, hh) for t in range(len(grp)) for hh in range(2)]
            ss = [_dot_nt(_bf(jnp.where(masks[hh], qs[t], 0.0)), kws[t]) * scale + b_ref[0, hh] for t, hh in both]
            dps = [_dot_nt(_bf(jnp.where(masks[hh], dos[t], 0.0)), vws[t]) for t, hh in both]
            ps = [jnp.exp(jnp.where(valids[t], s - lses[t][:, hh:hh + 1], NEG_INF)) for (t, hh), s in zip(both, ss)]
            dss = [p * (dp - dds[t][:, hh:hh + 1]) for (t, hh), p, dp in zip(both, ps, dps)]
            dqs = [_dot(_bf(ds), kws[t]) * scale for (t, hh), ds in zip(both, dss)]
            for t, rw in enumerate(rows):
                dq_ref[rw, :] = jnp.where(masks[0], dqs[2 * t], dqs[2 * t + 1])
                db[0] = db[0] + dss[2 * t]
                db[1] = db[1] + dss[2 * t + 1]

        @pl.when(tb == 0)
        def _():
            db_ref[0, 0] = db[0]
            db_ref[0, 1] = db[1]

        @pl.when(tb > 0)
        def _():
            db_ref[0, 0] += db[0]
            db_ref[0, 1] += db[1]

    return pl.pallas_call(
        body, grid=(ATT_PAIRS, S // ATT_TB),
        in_specs=[q_spec, *k_specs, *v_specs, bspec, BS((1, ATT_TB, LANES), lambda p, tb: (p, tb, 0)),
                  BS((1, ATT_TB, LANES), lambda p, tb: (gi, tb, p)), BS((1, 1, ATT_TB, LANES), lambda p, tb: (gi, p, tb, 0))],
        out_specs=[BS((ATT_TB, LANES), lambda p, tb: (tb, p)), bspec],
        out_shape=[SDS((S, DSWA_HG * DSWA_E), F32), SDS((ATT_PAIRS, 2, ATT_Q, ATT_W), F32)],
        name=name, compiler_params=_cp(("parallel", "arbitrary")))(qkv, qkv, qkv, qkv, qkv, qkv, qkv, bias, lse, do, dd)


def attn_bwd_kv(qkv, bias_t, lse, do, dd, gi, name):
    S = qkv.shape[0]
    d = DSWA_CFG[gi][1]
    nsb = ATT_TB // (DSWA_HALF * d)
    npair = DSWA_HEADS // 2
    q_specs = _att_specs(S, d, lambda p: gi * ATT_PAIRS + p)
    k_spec = _att_specs(S, d, lambda p: npair + gi * ATT_PAIRS + p)[1]
    v_spec = _att_specs(S, d, lambda p: 2 * npair + gi * ATT_PAIRS + p)[1]
    halo = DSWA_HALF * d
    per = ATT_TB // halo
    last = S // halo - 1

    def do_spec(rows, blk):
        return BS((1, rows, LANES), lambda p, tb: (gi, blk(tb), p))

    def dd_spec(rows, blk):
        return BS((1, 1, rows, LANES), lambda p, tb: (gi, p, blk(tb), 0))

    blks = [(halo, lambda tb: jnp.maximum(tb * per - 1, 0)), (ATT_TB, lambda tb: tb),
            (halo, lambda tb: jnp.minimum((tb + 1) * per, last))]
    do_specs = [do_spec(*b) for b in blks]
    dd_specs = [dd_spec(*b) for b in blks]
    lse_specs = _att_specs3(S, d, lambda p: p)

    class _Lead4:
        def __init__(self, ref):
            self.ref = ref

        def __getitem__(self, idx):
            return self.ref[(0,) + idx]

    def body(k_ref, v_ref, qp, qc, qn, dop, doc, don, lp, lc, ln, ddp, ddc, ddn, b_ref, dk_ref, dv_ref):
        tb = pl.program_id(1)
        masks, lane = _head_masks()
        qpc = _Pieces(qp, qc, qn, d)
        dopc = _Pieces(dop, doc, don, d, lead=True)
        lpc = _Pieces(lp, lc, ln, d, lead=True)
        ddpc = _Pieces(_Lead4(ddp), _Lead4(ddc), _Lead4(ddn), d, lead=True)
        scale = DSWA_E ** -0.5
        for grp in _tile_groups(d, nsb):
            rows = [_tile_rows(r, sb, d) for r, sb in grp]
            kcs = [_bf(k_ref[rw, :]) for rw in rows]
            vcs = [_bf(v_ref[rw, :]) for rw in rows]
            qws = [qpc.window(r, sb) for r, sb in grp]
            dows = [dopc.window(r, sb) for r, sb in grp]
            lws = [lpc.window(r, sb) for r, sb in grp]
            ddws = [ddpc.window(r, sb) for r, sb in grp]
            qwbs = [_bf(x) for x in qws]
            dowbs = [_bf(x) for x in dows]
            valids = [_tile_valid(tb, r, sb, d, S, True) for r, sb in grp]
            both = [(t, hh) for t in range(len(grp)) for hh in range(2)]
            ss = [_dot_nt(_bf(jnp.where(masks[hh], qws[t], 0.0)), kcs[t]) * scale + b_ref[0, hh] for t, hh in both]
            dps = [_dot_nt(_bf(jnp.where(masks[hh], dows[t], 0.0)), vcs[t]) for t, hh in both]
            ps = [jnp.exp(jnp.where(valids[t], s - lws[t][:, hh:hh + 1], NEG_INF)) for (t, hh), s in zip(both, ss)]
            dvs = [_dot_tn(_bf(p), dowbs[t]) for (t, hh), p in zip(both, ps)]
            dss = [p * (dp - ddws[t][:, hh:hh + 1]) for (t, hh), p, dp in zip(both, ps, dps)]
            dks = [_dot_tn(_bf(ds), qwbs[t]) * scale for (t, hh), ds in zip(both, dss)]
            for t, rw in enumerate(rows):
                dk_ref[rw, :] = jnp.where(masks[0], dks[2 * t], dks[2 * t + 1])
                dv_ref[rw, :] = jnp.where(masks[0], dvs[2 * t], dvs[2 * t + 1])

    out = BS((ATT_TB, LANES), lambda p, tb: (tb, p))
    return pl.pallas_call(
        body, grid=(ATT_PAIRS, S // ATT_TB),
        in_specs=[k_spec, v_spec, *q_specs, *do_specs, *lse_specs, *dd_specs,
                  BS((1, 2, ATT_W, ATT_Q), lambda p, tb: (p, 0, 0, 0))],
        out_specs=[out, out],
        out_shape=[SDS((S, DSWA_HG * DSWA_E), F32), SDS((S, DSWA_HG * DSWA_E), F32)],
        name=name, compiler_params=_cp(("parallel", "parallel")))(
            qkv, qkv, qkv, qkv, qkv, do, do, do, lse, lse, lse, dd, dd, dd, bias_t)


def _pair_alphas(lses):
    m = jnp.maximum(jnp.maximum(lses[0], lses[1]), lses[2])
    e = [jnp.exp(t - m) for t in lses]
    tot = e[0] + e[1] + e[2]
    return [t / tot for t in e]


def _pair_expand(a, lane):
    return jnp.where(lane < DSWA_E, a[:, 0:1], a[:, 1:2])


def combine_fwd(o_raw, lse, name):
    S = o_raw.shape[0]
    tm = min(512, S)

    def body(o_ref, l_ref, y_ref):
        g = pl.program_id(2)
        lane = lax.broadcasted_iota(jnp.int32, (1, LANES), 1)
        alphas = _pair_alphas([l_ref[0, 0], l_ref[1, 0], l_ref[2, 0]])
        a = jnp.where(g == 0, alphas[0], jnp.where(g == 1, alphas[1], alphas[2]))
        y_ref[...] = (o_ref[...] * _pair_expand(a, lane)).astype(BF16)

    blk = BS((tm, LANES), lambda i, p, g: (i, g * ATT_PAIRS + p))
    return pl.pallas_call(
        body, grid=(S // tm, ATT_PAIRS, 3),
        in_specs=[blk, BS((3, 1, tm, LANES), lambda i, p, g: (0, p, i, 0))], out_specs=blk,
        out_shape=SDS((S, DSWA_W), BF16), name=name, compiler_params=_cp(("parallel", "parallel", "parallel")))(o_raw, lse)


def combine_bwd(o_raw, lse, dy, name):
    S = o_raw.shape[0]
    tm = min(512, S)

    def body(o0, o1, o2, l_ref, d0, d1, d2, do_ref, dd_ref):
        lane = lax.broadcasted_iota(jnp.int32, (1, LANES), 1)
        alphas = _pair_alphas([l_ref[0, 0], l_ref[1, 0], l_ref[2, 0]])
        c = jnp.zeros((tm, LANES), F32)
        for g, (o_ref, dy_ref) in enumerate(((o0, d0), (o1, d1), (o2, d2))):
            dyv = dy_ref[...]
            do_ref[g] = dyv * _pair_expand(alphas[g], lane)
            prod = o_ref[...] * dyv
            dal = (jnp.where(lane == 0, jnp.sum(jnp.where(lane < DSWA_E, prod, 0.0), axis=1, keepdims=True), 0.0)
                   + jnp.where(lane == 1, jnp.sum(jnp.where(lane >= DSWA_E, prod, 0.0), axis=1, keepdims=True), 0.0))
            c = c + alphas[g] * dal
        for g in range(3):
            dd_ref[g, 0] = alphas[g] * c

    def col(g):
        return BS((tm, LANES), lambda i, p: (i, g * ATT_PAIRS + p))

    return pl.pallas_call(
        body, grid=(S // tm, ATT_PAIRS),
        in_specs=[col(0), col(1), col(2), BS((3, 1, tm, LANES), lambda i, p: (0, p, i, 0)), col(0), col(1), col(2)],
        out_specs=[BS((3, tm, LANES), lambda i, p: (0, i, p)), BS((3, 1, tm, LANES), lambda i, p: (0, p, i, 0))],
        out_shape=[SDS((3, S, DSWA_HG * DSWA_E), F32), SDS((3, ATT_PAIRS, S, LANES), F32)],
        name=name, compiler_params=_cp(("parallel", "parallel")))(o_raw, o_raw, o_raw, lse, dy, dy, dy)


def dswa_fwd(x, g, w_in, w_out, rel_table, tag):
    h = rms_fwd(x, g, f"{tag}_rms")
    qkv = mm(h, w_in, name=f"{tag}_qkv", tn=1152)
    outs, lses = [], []
    for gi in range(3):
        bias, _ = _bias_mats(rel_table, gi)
        o, lse = attn_fwd(qkv, bias, gi, f"{tag}_att{gi}")
        outs.append(o)
        lses.append(lse)
    o_raw = jnp.concatenate(outs, axis=1)
    lse = jnp.stack(lses)
    y = combine_fwd(o_raw, lse, f"{tag}_comb")
    xn = mm(y, w_out, name=f"{tag}_out", epi=_add, extras=(x,))
    return xn, (h, qkv, o_raw, lse, y)


def dswa_bwd(x, g, w_in, w_out, rel_table, saved, dx, dxb, tag):
    h, qkv, o_raw, lse, y = saved
    dw_out = mm(y, dxb, name=f"{tag}_dwout", ta=True, tm=384)
    dy = mm(dxb, w_out, name=f"{tag}_dy", tb=True, tn=384)
    do_raw, dd = combine_bwd(o_raw, lse, dy, f"{tag}_combb")
    dqs, dks, dvs = [], [], []
    drel = jnp.zeros_like(rel_table)
    for gi in range(3):
        (bias, bias_t), bias_vjp = jax.vjp(lambda tbl: _bias_mats(tbl, gi), rel_table)
        dq, dbias = attn_bwd_q(qkv, bias, lse[gi], do_raw, dd, gi, f"{tag}_attq{gi}")
        dk, dv = attn_bwd_kv(qkv, bias_t, lse[gi], do_raw, dd, gi, f"{tag}_attkv{gi}")
        drel = drel + bias_vjp((dbias, jnp.zeros_like(bias_t)))[0]
        dqs.append(dq)
        dks.append(dk)
        dvs.append(dv)
    dqkv = jnp.concatenate(dqs + dks + dvs, axis=1).astype(BF16)
    dw_in = mm(h, dqkv, name=f"{tag}_dwin", ta=True, tn=384)
    dh = mm(dqkv, w_in, name=f"{tag}_dh", tb=True, tk=1152)
    dx, dxb, dg = rms_bwd(x, g, dh, dx, f"{tag}_rmsb")
    return dx, dxb, dg, dw_in, dw_out, drel


def adamw(w, g, m, v, name):
    shape = w.shape
    last = shape[-1]
    w2, g2, m2, v2 = (t.reshape(-1, last) for t in (w, g, m, v))
    rows = w2.shape[0]
    tr = rows
    if rows > 512:
        tr = next(t for t in (512, 256, 192, 128, 64, 8) if rows % t == 0)
    c1 = 1.0 / (1.0 - ADAM_B1 ** ADAM_STEP)
    c2 = 1.0 / (1.0 - ADAM_B2 ** ADAM_STEP)

    def body(w_ref, g_ref, m_ref, v_ref, d_ref, nm_ref, nv_ref):
        gv = g_ref[...]
        nm = ADAM_B1 * m_ref[...] + (1.0 - ADAM_B1) * gv
        nv = ADAM_B2 * v_ref[...] + (1.0 - ADAM_B2) * (gv * gv)
        nm_ref[...] = nm
        nv_ref[...] = nv
        d_ref[...] = -ADAM_LR * ((nm * c1) / (jnp.sqrt(nv * c2) + ADAM_EPS) + ADAM_WD * w_ref[...])

    spec = BS((tr, last), lambda i: (i, 0))
    outs = pl.pallas_call(
        body, grid=(rows // tr,), in_specs=[spec] * 4, out_specs=[spec] * 3,
        out_shape=[SDS((rows, last), F32)] * 3, name=name, compiler_params=_cp(("parallel",)))(w2, g2, m2, v2)
    return tuple(o.reshape(shape) for o in outs)


def _place():
    x, y, c = lax.axis_index("x"), lax.axis_index("y"), lax.axis_index("c")
    chips = [(1 - x, y), (x, 1 - y), (1 - x, 1 - y)]
    return x, y, c, chips


def _rcopy(src, dst, ssem, rsem, dev):
    return pltpu.make_async_remote_copy(src_ref=src, dst_ref=dst, send_sem=ssem, recv_sem=rsem, device_id=dev,
                                        device_id_type=MESH)


class SideJob(NamedTuple):
    ins: list
    outs: list
    sems: list
    start: Callable
    wait: Callable


def _job(ins, outs, sems, copies):
    def start(in_refs, out_refs, sem_refs):
        for cp in copies(in_refs, out_refs, sem_refs):
            cp.start()

    def wait(in_refs, out_refs, sem_refs):
        for cp in copies(in_refs, out_refs, sem_refs):
            cp.wait()

    return SideJob(list(ins), list(outs), list(sems), start, wait)


def gather_job(packs, halved):
    n = len(packs)
    dma = pltpu.SemaphoreType.DMA

    def copies(in_refs, out_refs, sems):
        ssem, rsem = sems
        x, y, c, chips = _place()
        jme = 2 * x + y
        cps = []
        for i, (p_ref, f_ref) in enumerate(zip(in_refs, out_refs)):
            rows = p_ref.shape[0]
            mine = pl.ds(c * (rows // 2), rows // 2) if halved[i] else pl.ds(0, rows)
            for r, (cx, cy) in enumerate(chips):
                cps.append(_rcopy(p_ref.at[mine], f_ref.at[jme, mine], ssem.at[i, r], rsem.at[i, r], (cx, cy, c)))
        return cps

    return _job(packs, [SDS((4,) + p.shape, p.dtype) for p in packs], [dma((n, 3)), dma((n, 3))], copies)


def chip_exchange_job(parts):
    n = len(parts)
    dma = pltpu.SemaphoreType.DMA

    def copies(in_refs, out_refs, sems):
        ssem, rsem = sems
        x, y, c, chips = _place()
        cps = []
        for i, (p_ref, r_ref) in enumerate(zip(in_refs, out_refs)):
            for r, (cx, cy) in enumerate(chips):
                cps.append(_rcopy(p_ref.at[2 * cx + cy], r_ref.at[r], ssem.at[i, r], rsem.at[i, r], (cx, cy, c)))
        return cps

    return _job(parts, [SDS((3,) + p.shape[1:], p.dtype) for p in parts], [dma((n, 3)), dma((n, 3))], copies)


def run_job(job, name):
    ni, no = len(job.ins), len(job.outs)

    def body(*refs):
        job.start(refs[:ni], refs[ni:ni + no], refs[ni + no:])
        job.wait(refs[:ni], refs[ni:ni + no], refs[ni + no:])

    return pl.pallas_call(
        body, in_specs=[ANY] * ni, out_specs=[ANY] * no, out_shape=job.outs, scratch_shapes=job.sems, name=name,
        compiler_params=pltpu.CompilerParams(has_side_effects=True))(*job.ins)


def forward_to_sibling(fulls, name):
    n = len(fulls)

    def body(*refs):
        in_refs, out_refs, (ssem, rsem) = refs[:n], refs[n:2 * n], refs[2 * n:]
        x, y, c, chips = _place()
        cps = []
        for i in range(n):
            half = in_refs[i].shape[1] // 2
            for r, (cx, cy) in enumerate(chips):
                piece = (2 * cx + cy, pl.ds(c * half, half))
                cps.append(_rcopy(in_refs[i].at[piece], out_refs[i].at[piece], ssem.at[i, r], rsem.at[i, r], (x, y, 1 - c)))
        for cp in cps:
            cp.start()
        for cp in cps:
            cp.wait()

    dma = pltpu.SemaphoreType.DMA
    return pl.pallas_call(
        body, in_specs=[ANY] * n, out_specs=[ANY] * n, out_shape=[SDS(f.shape, f.dtype) for f in fulls],
        scratch_shapes=[dma((n, 3)), dma((n, 3))], input_output_aliases={i: i for i in range(n)}, name=name,
        compiler_params=pltpu.CompilerParams(has_side_effects=True))(*fulls)


def rs_sibling_exchange(gpack, name):
    _, rows, W = gpack.shape
    half = rows // 2

    def body(g_ref, r_ref, ssem, rsem):
        x, y, c, _ = _place()
        cps = [_rcopy(g_ref.at[j, pl.ds((1 - c) * half, half)], r_ref.at[j], ssem.at[j], rsem.at[j], (x, y, 1 - c))
               for j in range(4)]
        for cp in cps:
            cp.start()
        for cp in cps:
            cp.wait()

    dma = pltpu.SemaphoreType.DMA
    return pl.pallas_call(
        body, in_specs=[ANY], out_specs=ANY, out_shape=SDS((4, half, W), gpack.dtype),
        scratch_shapes=[dma((4,)), dma((4,))], name=name,
        compiler_params=pltpu.CompilerParams(has_side_effects=True))(gpack)


def _div_tile(n, limit):
    return next(t for t in range(limit - limit % 16, 0, -16) if n % t == 0)


def rs_add_sibling(gpack, recv, cidx, name, out_dtype=F32):
    _, rows, W = gpack.shape
    half = rows // 2
    tr = _div_tile(half, 1024)
    nb = half // tr

    def body(c_ref, g_ref, r_ref, o_ref):
        o_ref[...] = (g_ref[...].astype(F32) + r_ref[...].astype(F32)).astype(o_ref.dtype)

    gs = pltpu.PrefetchScalarGridSpec(
        num_scalar_prefetch=1, grid=(4, nb),
        in_specs=[BS((1, tr, W), lambda j, i, c: (j, c[0] * nb + i, 0)), BS((1, tr, W), lambda j, i, c: (j, i, 0))],
        out_specs=BS((1, tr, W), lambda j, i, c: (j, i, 0)))
    return pl.pallas_call(body, grid_spec=gs, out_shape=SDS((4, half, W), out_dtype), name=name,
                          compiler_params=_cp(("parallel", "parallel")))(cidx, gpack, recv)


def rs_add_chips(recv, part, place, name):
    _, half, W = recv.shape
    tr = _div_tile(half, 640)
    nb = half // tr

    def body(x_ref, y_ref, c_ref, r_ref, own_ref, o_ref):
        r0, r1, r2, own = (t.astype(F32) for t in (r_ref[0], r_ref[1], r_ref[2], own_ref[0]))
        o_ref[...] = ((r0 + r1) + r2) + own

    gs = pltpu.PrefetchScalarGridSpec(
        num_scalar_prefetch=3, grid=(nb,),
        in_specs=[BS((3, tr, W), lambda i, x, y, c: (0, i, 0)), BS((1, tr, W), lambda i, x, y, c: (2 * x[0] + y[0], i, 0))],
        out_specs=BS((tr, W), lambda i, x, y, c: (c[0] * nb + i, 0)))
    return pl.pallas_call(body, grid_spec=gs, out_shape=SDS((2 * half, W), F32), name=name,
                          compiler_params=_cp(("parallel",)))(*place, recv, part)


def rs_sibling_share(gsh, name):
    rows, W = gsh.shape
    half = rows // 2

    def body(g_ref, o_ref, ssem, rsem):
        x, y, c, _ = _place()
        mine = pl.ds(c * half, half)
        cp = _rcopy(g_ref.at[mine], o_ref.at[mine], ssem, rsem, (x, y, 1 - c))
        cp.start()
        cp.wait()

    dma = pltpu.SemaphoreType.DMA
    return pl.pallas_call(
        body, in_specs=[ANY], out_specs=ANY, out_shape=SDS(gsh.shape, gsh.dtype),
        scratch_shapes=[dma, dma], input_output_aliases={0: 0}, name=name,
        compiler_params=pltpu.CompilerParams(has_side_effects=True))(gsh)


def allreduce_small(pack):
    R = pack.shape[0]

    def body(p_ref, o_ref, all_ref, ssem, rsem):
        x, y, c, _ = _place()
        me = 4 * x + 2 * y + c
        all_ref[me] = p_ref[...]
        cps = []
        for m in range(1, 8):
            peer = (1 - x if m & 4 else x, 1 - y if m & 2 else y, 1 - c if m & 1 else c)
            cp = _rcopy(p_ref, all_ref.at[me], ssem.at[m - 1], rsem.at[m - 1], peer)
            cp.start()
            cps.append(cp)
        for cp in cps:
            cp.wait()
        acc = all_ref[0]
        for i in range(1, 8):
            acc = acc + all_ref[i]
        o_ref[...] = acc

    dma = pltpu.SemaphoreType.DMA
    vm = BS(memory_space=pltpu.VMEM)
    return pl.pallas_call(
        body, in_specs=[vm], out_specs=vm, out_shape=SDS(pack.shape, F32),
        scratch_shapes=[pltpu.VMEM((8, R, LANES), F32), dma((7,)), dma((7,))], name="allreduce_small",
        compiler_params=pltpu.CompilerParams(has_side_effects=True))(pack)


PACK_W = 1024
PACK_ALIGN = 32


def _layer_entries(l):
    if l % 2 == 0:
        mixer = [("gdn_w_in", l // 2, D_MODEL, GDN_IN // 4, True), ("gdn_w_out", l // 2, D_MODEL // 4, D_MODEL, False)]
    else:
        mixer = [("dswa_w_in", l // 2, D_MODEL, 3 * DSWA_W // 4, True), ("dswa_w_out", l // 2, DSWA_W // 4, D_MODEL, False)]
    return mixer + [("mlp_w1", l, D_MODEL, D_FF // 4, True), ("mlp_w2", l, D_FF // 4, D_MODEL, False)]


def _layer_offsets(l):
    offs = [int(o) for o in np.cumsum([0] + [r * c // PACK_W for (_, _, r, c, _) in _layer_entries(l)])]
    return offs, -(-offs[-1] // PACK_ALIGN) * PACK_ALIGN


def _pack_layer(l, shards, dtype):
    offs, total = _layer_offsets(l)
    parts = [shards[name][li].astype(dtype).reshape(-1, PACK_W) for (name, li, _, _, _) in _layer_entries(l)]
    parts.append(jnp.zeros((total - offs[-1], PACK_W), dtype))
    return jnp.concatenate(parts, axis=0)


def _unpack_layer(l, full, own, jme):
    offs, _ = _layer_offsets(l)
    mats = []
    for e, (_, _, r, c, by_col) in enumerate(_layer_entries(l)):
        mine = own[offs[e]:offs[e + 1]]
        sh = [jnp.where(jme == j, mine, full[j, offs[e]:offs[e + 1]]).reshape(r, c) for j in range(4)]
        mats.append(jnp.concatenate(sh, axis=1 if by_col else 0))
    return mats


def _pack_layer_grads(l, grads):
    offs, total = _layer_offsets(l)
    per_chip = []
    for j in range(4):
        parts = []
        for g, (_, _, r, c, by_col) in zip(grads, _layer_entries(l)):
            sh = g[:, c * j:c * (j + 1)] if by_col else g[r * j:r * (j + 1), :]
            parts.append(sh.astype(BF16).reshape(-1, PACK_W))
        parts.append(jnp.zeros((total - offs[-1], PACK_W), BF16))
        per_chip.append(jnp.concatenate(parts, axis=0))
    return jnp.stack(per_chip)


def _unpack_shard_grads(gshs):
    out = {}
    for l, gsh in enumerate(gshs):
        offs, _ = _layer_offsets(l)
        for e, (name, _, r, c, _) in enumerate(_layer_entries(l)):
            out.setdefault(name, []).append(gsh[offs[e]:offs[e + 1]].reshape(r, c))
    return {k: jnp.stack(v) for k, v in out.items()}


def _flat_pad(t, mult=8 * LANES):
    f = t.reshape(-1)
    return jnp.pad(f, (0, (-f.shape[0]) % mult))


def kernel(x, norm_mix, norm_mlp, norm_final, rel_bias, gdn_w_in, gdn_conv_w, gdn_a_log, gdn_dt_bias, gdn_norm_w, gdn_w_out, dswa_w_in, dswa_w_out, mlp_w1, mlp_w2, loss_target, m_norm_mix, m_norm_mlp, m_norm_final, m_rel_bias, m_gdn_w_in, m_gdn_conv_w, m_gdn_a_log, m_gdn_dt_bias, m_gdn_norm_w, m_gdn_w_out, m_dswa_w_in, m_dswa_w_out, m_mlp_w1, m_mlp_w2, v_norm_mix, v_norm_mlp, v_norm_final, v_rel_bias, v_gdn_w_in, v_gdn_conv_w, v_gdn_a_log, v_gdn_dt_bias, v_gdn_norm_w, v_gdn_w_out, v_dswa_w_in, v_dswa_w_out, v_mlp_w1, v_mlp_w2):
    xi, yi, ci = lax.axis_index("x"), lax.axis_index("y"), lax.axis_index("c")
    jme = 2 * xi + yi
    big = dict(gdn_w_in=gdn_w_in, gdn_w_out=gdn_w_out, dswa_w_in=dswa_w_in, dswa_w_out=dswa_w_out, mlp_w1=mlp_w1, mlp_w2=mlp_w2)
    n_gdn = gdn_w_in.shape[0]
    conv_cols = gdn_conv_w.shape[-1]

    packs = [_pack_layer(l, big, BF16) for l in range(DEPTH)]
    convp = jnp.pad(gdn_conv_w.reshape(n_gdn * GDN_CONV, conv_cols), ((0, 16 - n_gdn * GDN_CONV), (0, 0)))
    raw0, cfull = run_job(gather_job([packs[0], convp], [True, False]), "gather_l0")
    fulls = {0: forward_to_sibling([raw0], "forward_l0")[0]}
    cfull = jnp.where((jnp.arange(4) == jme)[:, None, None], convp[None], cfull)
    conv_all = jnp.transpose(cfull[:, :n_gdn * GDN_CONV], (1, 0, 2)).reshape(n_gdn, GDN_CONV, 4 * conv_cols)
    conv_all = jnp.pad(conv_all, ((0, 0), (0, 8 - GDN_CONV), (0, 0)))
    fwd_jobs = {0: [1, 2], 2: [3]}

    xs = x[0]
    saved = []
    for l in range(DEPTH):
        w_in, w_out, w1, w2 = _unpack_layer(l, fulls[l], packs[l], jme)
        gm, gp = norm_mix[l][None], norm_mlp[l][None]
        a = l // 2
        if l % 2 == 0:
            w_in = jnp.pad(w_in, ((0, 0), (0, GDN_INP - GDN_IN)))
            job = gather_job([packs[t] for t in fwd_jobs[l]], [True] * len(fwd_jobs[l]))
            x_mid, sv, raws = gdn_fwd(xs, gm, w_in, conv_all[a], gdn_a_log[a], gdn_dt_bias[a], gdn_norm_w[a][None], w_out,
                                      f"l{l}_gdn", job)
            for t, f in zip(fwd_jobs[l], forward_to_sibling(raws, f"forward_from_l{l}")):
                fulls[t] = f
        else:
            x_mid, sv = dswa_fwd(xs, gm, w_in, w_out, rel_bias, f"l{l}_att")
        x_out, sv2 = mlp_fwd(x_mid, gp, w1, w2, f"l{l}_mlp")
        saved.append((xs, x_mid, (w_in, w_out, w1, w2), sv, sv2))
        xs = x_out

    cidx = ci.astype(jnp.int32).reshape(1)
    place = [t.astype(jnp.int32).reshape(1) for t in (xi, yi, ci)]

    def chip_partial(l, grads4):
        gpack = _pack_layer_grads(l, grads4)
        return rs_add_sibling(gpack, rs_sibling_exchange(gpack, f"rs_sibling_l{l}"), cidx, f"rs_add_sibling_l{l}",
                              BF16)

    def finish(l, recv):
        return rs_sibling_share(rs_add_chips(recv, parts[l], place, f"rs_add_chips_l{l}"), f"rs_share_l{l}")

    loss_part, dx, dxb, d_final = loss_head(xs, norm_final[None], loss_target[0], "loss_head")
    d_mix, d_mlp = [None] * DEPTH, [None] * DEPTH
    d_conv, d_alog, d_dt, d_nw = [None] * n_gdn, [None] * n_gdn, [None] * n_gdn, [None] * n_gdn
    d_rel = jnp.zeros_like(rel_bias)
    parts, gshs = {}, [None] * DEPTH
    bwd_jobs = {2: [3], 0: [2, 1]}
    for l in reversed(range(DEPTH)):
        x_in, x_mid, (w_in, w_out, w1, w2), sv, sv2 = saved[l]
        gm, gp = norm_mix[l][None], norm_mlp[l][None]
        a = l // 2
        dx, dxb, d_mlp[l], dw1, dw2 = mlp_bwd(x_mid, gp, w1, w2, sv2, dx, dxb, f"l{l}_mlp")
        if l % 2 == 0:
            job = chip_exchange_job([parts[t] for t in bwd_jobs[l]])
            dx, dxb, d_mix[l], dw_all, d_conv[a], d_alog[a], d_dt[a], d_nw[a], dwo, recvs = gdn_bwd(
                x_in, gm, w_in, conv_all[a], gdn_norm_w[a][None], w_out, sv, dx, dxb, f"l{l}_gdn", job)
            for t, rv in zip(bwd_jobs[l], recvs):
                gshs[t] = finish(t, rv)
            dwi = dw_all[:, :GDN_IN]
        else:
            dx, dxb, d_mix[l], dwi, dwo, drel = dswa_bwd(x_in, gm, w_in, w_out, rel_bias, sv, dx, dxb, f"l{l}_att")
            d_rel = d_rel + drel
        parts[l] = chip_partial(l, [dwi, dwo, dw1, dw2])
    gshs[0] = finish(0, run_job(chip_exchange_job([parts[0]]), "rs_chip_exchange_l0")[0])
    gbig = _unpack_shard_grads(gshs)

    small = [jnp.concatenate(d_mix, axis=0), jnp.concatenate(d_mlp, axis=0), d_final, d_rel,
             jnp.stack(d_conv), jnp.stack(d_alog), jnp.stack(d_dt), jnp.concatenate(d_nw, axis=0)]
    flat = [_flat_pad(t) for t in small]
    sizes = [f.shape[0] for f in flat]
    red = allreduce_small(jnp.concatenate(flat).reshape(-1, LANES)).reshape(-1)
    offs = np.cumsum([0] + sizes)
    red = [red[offs[i]:offs[i] + small[i].size].reshape(small[i].shape) for i in range(len(small))]
    g_conv_all = red[4][:, :GDN_CONV].reshape(n_gdn, GDN_CONV, 1, 4 * conv_cols)
    g_conv = lax.dynamic_slice_in_dim(g_conv_all, jme * conv_cols, conv_cols, axis=3)
    g = dict(norm_mix=red[0], norm_mlp=red[1], norm_final=red[2].reshape(norm_final.shape), rel_bias=red[3],
             gdn_conv_w=g_conv, gdn_a_log=red[5], gdn_dt_bias=red[6], gdn_norm_w=red[7][:, :GDN_DK], **gbig)

    w = dict(norm_mix=norm_mix, norm_mlp=norm_mlp, norm_final=norm_final, rel_bias=rel_bias, gdn_conv_w=gdn_conv_w,
             gdn_a_log=gdn_a_log, gdn_dt_bias=gdn_dt_bias, gdn_norm_w=gdn_norm_w, **big)
    m = dict(norm_mix=m_norm_mix, norm_mlp=m_norm_mlp, norm_final=m_norm_final, rel_bias=m_rel_bias, gdn_w_in=m_gdn_w_in,
             gdn_conv_w=m_gdn_conv_w, gdn_a_log=m_gdn_a_log, gdn_dt_bias=m_gdn_dt_bias, gdn_norm_w=m_gdn_norm_w,
             gdn_w_out=m_gdn_w_out, dswa_w_in=m_dswa_w_in, dswa_w_out=m_dswa_w_out, mlp_w1=m_mlp_w1, mlp_w2=m_mlp_w2)
    v = dict(norm_mix=v_norm_mix, norm_mlp=v_norm_mlp, norm_final=v_norm_final, rel_bias=v_rel_bias, gdn_w_in=v_gdn_w_in,
             gdn_conv_w=v_gdn_conv_w, gdn_a_log=v_gdn_a_log, gdn_dt_bias=v_gdn_dt_bias, gdn_norm_w=v_gdn_norm_w,
             gdn_w_out=v_gdn_w_out, dswa_w_in=v_dswa_w_in, dswa_w_out=v_dswa_w_out, mlp_w1=v_mlp_w1, mlp_w2=v_mlp_w2)
    names = ["norm_mix", "norm_mlp", "norm_final", "rel_bias", "gdn_w_in", "gdn_conv_w", "gdn_a_log", "gdn_dt_bias",
             "gdn_norm_w", "gdn_w_out", "dswa_w_in", "dswa_w_out", "mlp_w1", "mlp_w2"]
    upd = {n: adamw(w[n], g[n], m[n], v[n], f"adamw_{n}") for n in names}
    loss = lax.psum(loss_part[0, 0], ("x", "y", "c"))
    return (loss, dx[None], *[g[n] for n in names], *[upd[n][0] for n in names], *[upd[n][1] for n in names],
            *[upd[n][2] for n in names])
```

```python
import math
from typing import Callable, NamedTuple

import numpy as np
import jax
import jax.numpy as jnp
from jax import lax
from jax.experimental import pallas as pl
from jax.experimental.pallas import tpu as pltpu

F32 = jnp.float32
BF16 = jnp.bfloat16
HI = lax.Precision.HIGHEST
BS = pl.BlockSpec
SDS = jax.ShapeDtypeStruct
MESH = pl.DeviceIdType.MESH
ANY = BS(memory_space=pl.ANY)

D_MODEL = 1024
D_FF = 4096
DEPTH = 4
RMS_EPS = 1e-6
NEG_INF = -1e30
LANES = 128
VMEM_LIMIT = 56 << 20

GDN_H = 8
GDN_DK = 128
GDN_CONV = 5
GDN_C = 64
GDN_GC = 8
GDN_HP_FWD = 8
GDN_HP_BWD = 4
GDN_QKV = 3 * GDN_H * GDN_DK
GDN_IN = GDN_QKV + GDN_H * GDN_DK + 4 * GDN_H
GDN_INP = 4224

DSWA_CFG = ((128, 1), (512, 4), (2048, 16))
DSWA_HG = 6
DSWA_E = 64
DSWA_HEADS = 18
DSWA_W = DSWA_HEADS * DSWA_E
DSWA_HALF = 64
REL_BUCKETS = 32
REL_MAX_DIST = 1024

ADAM_LR = 0.001
ADAM_B1 = 0.9
ADAM_B2 = 0.999
ADAM_EPS = 1e-08
ADAM_WD = 0.01
ADAM_STEP = 10


def _cp(sem=None):
    return pltpu.CompilerParams(dimension_semantics=sem, vmem_limit_bytes=VMEM_LIMIT)


def _dot(a, b, prec=None):
    return jnp.dot(a, b, precision=prec, preferred_element_type=F32)


def _dot_nt(a, b, prec=None):
    return lax.dot_general(a, b, (((1,), (1,)), ((), ())), precision=prec, preferred_element_type=F32)


def _dot_tn(a, b, prec=None):
    return lax.dot_general(a, b, (((0,), (0,)), ((), ())), precision=prec, preferred_element_type=F32)


def _bf(a):
    return a.astype(BF16)


def _sigmoid(x):
    return 1.0 / (1.0 + jnp.exp(-x))


def rms_fwd(x, g, name):
    S, Dm = x.shape
    tm = min(512, S)

    def body(x_ref, g_ref, o_ref):
        xv = x_ref[...]
        r = lax.rsqrt(jnp.mean(xv * xv, axis=-1, keepdims=True) + RMS_EPS)
        o_ref[...] = (xv * r * g_ref[...]).astype(o_ref.dtype)

    return pl.pallas_call(
        body, grid=(S // tm,),
        in_specs=[BS((tm, Dm), lambda i: (i, 0)), BS((1, Dm), lambda i: (0, 0))],
        out_specs=BS((tm, Dm), lambda i: (i, 0)),
        out_shape=SDS((S, Dm), BF16), name=name, compiler_params=_cp(("parallel",)))(x, g)


def rms_bwd(x, g, dh, dres, name):
    S, Dm = x.shape
    tm = min(512, S)

    def body(x_ref, g_ref, dh_ref, dres_ref, dx_ref, dxb_ref, dg_ref):
        i = pl.program_id(0)
        xv = x_ref[...]
        r = lax.rsqrt(jnp.mean(xv * xv, axis=-1, keepdims=True) + RMS_EPS)
        n = xv * r
        dhv = dh_ref[...]
        t = dhv * g_ref[...]
        dx = dres_ref[...] + r * (t - n * jnp.mean(n * t, axis=-1, keepdims=True))
        dx_ref[...] = dx
        dxb_ref[...] = dx.astype(BF16)
        part = jnp.sum(dhv * n, axis=0, keepdims=True)

        @pl.when(i == 0)
        def _():
            dg_ref[...] = part

        @pl.when(i > 0)
        def _():
            dg_ref[...] += part

    row = BS((tm, Dm), lambda i: (i, 0))
    vec = BS((1, Dm), lambda i: (0, 0))
    return pl.pallas_call(
        body, grid=(S // tm,), in_specs=[row, vec, row, row], out_specs=[row, row, vec],
        out_shape=[SDS((S, Dm), F32), SDS((S, Dm), BF16), SDS((1, Dm), F32)],
        name=name, compiler_params=_cp(("arbitrary",)))(x, g, dh, dres)


def loss_head(x, g, tgt, name):
    S, Dm = x.shape
    tm = min(512, S)

    def body(x_ref, g_ref, t_ref, loss_ref, dx_ref, dxb_ref, dg_ref):
        i = pl.program_id(0)
        xv = x_ref[...]
        gv = g_ref[...]
        r = lax.rsqrt(jnp.mean(xv * xv, axis=-1, keepdims=True) + RMS_EPS)
        n = xv * r
        err = n * gv - t_ref[...]
        lpart = 0.5 * jnp.sum(jnp.mean(err * err, axis=-1, keepdims=True), axis=0, keepdims=True)
        dout = err * (1.0 / Dm)
        t = dout * gv
        dx = r * (t - n * jnp.mean(n * t, axis=-1, keepdims=True))
        dx_ref[...] = dx
        dxb_ref[...] = dx.astype(BF16)
        part = jnp.sum(dout * n, axis=0, keepdims=True)

        @pl.when(i == 0)
        def _():
            dg_ref[...] = part
            loss_ref[...] = lpart

        @pl.when(i > 0)
        def _():
            dg_ref[...] += part
            loss_ref[...] += lpart

    row = BS((tm, Dm), lambda i: (i, 0))
    vec = BS((1, Dm), lambda i: (0, 0))
    one = BS((1, 1), lambda i: (0, 0))
    return pl.pallas_call(
        body, grid=(S // tm,), in_specs=[row, vec, row], out_specs=[one, row, row, vec],
        out_shape=[SDS((1, 1), F32), SDS((S, Dm), F32), SDS((S, Dm), BF16), SDS((1, Dm), F32)],
        name=name, compiler_params=_cp(("arbitrary",)))(x, g, tgt)


MM_TK = 2048


def mm(a, b, *, name, ta=False, tb=False, tm=1024, tn=512, tk=None, out_dtype=F32, pre_a=None, epi=None,
       extras=()):
    M, K = (a.shape[1], a.shape[0]) if ta else a.shape
    N = b.shape[0] if tb else b.shape[1]
    if ta:
        tm = min(tm, 512)
        tk = K if tk is None else tk
    elif tk is None and K > MM_TK:
        tk, tn = K, min(tn, 256)
    tm, tn = min(tm, M), min(tn, N)
    tk = K if tk is None else min(tk, K)
    assert M % tm == 0 and N % tn == 0 and K % tk == 0, (name, M, N, K, tm, tn, tk)
    nk = K // tk
    ne = len(extras)
    a_spec = BS((tk, tm), lambda i, j, k: (k, i)) if ta else BS((tm, tk), lambda i, j, k: (i, k))
    b_spec = BS((tn, tk), lambda i, j, k: (j, k)) if tb else BS((tk, tn), lambda i, j, k: (k, j))
    o_spec = BS((tm, tn), lambda i, j, k: (i, j))
    dims = (((0 if ta else 1,), (1 if tb else 0,)), ((), ()))

    def body(a_ref, b_ref, *rest):
        e_refs, o_ref = rest[:ne], rest[ne]
        av = a_ref[...]
        if pre_a is not None:
            av = pre_a(av)
        p = lax.dot_general(_bf(av), _bf(b_ref[...]), dims, preferred_element_type=F32)

        def finish(acc):
            res = epi(acc, *[e[...] for e in e_refs]) if epi is not None else acc
            o_ref[...] = res.astype(o_ref.dtype)

        if nk == 1:
            finish(p)
        else:
            acc_ref = rest[ne + 1]
            k = pl.program_id(2)

            @pl.when(k == 0)
            def _():
                acc_ref[...] = p

            @pl.when(k > 0)
            def _():
                acc_ref[...] += p

            @pl.when(k == nk - 1)
            def _():
                finish(acc_ref[...])

    return pl.pallas_call(
        body, grid=(M // tm, N // tn, nk), in_specs=[a_spec, b_spec] + [o_spec] * ne, out_specs=o_spec,
        out_shape=SDS((M, N), out_dtype),
        scratch_shapes=[pltpu.VMEM((tm, tn), F32)] if nk > 1 else [],
        name=name, compiler_params=_cp(("parallel", "parallel", "arbitrary")))(a, b, *extras)


def _relu(acc):
    return jnp.maximum(acc, 0.0)


def _add(acc, res):
    return acc + res


def _sq(av):
    return av * av


def _times_2r(acc, r):
    return acc * (2.0 * r.astype(F32))


def mlp_fwd(x, g, w1, w2, tag):
    h = rms_fwd(x, g, f"{tag}_rms")
    r = mm(h, w1, name=f"{tag}_up", tn=1024, out_dtype=BF16, epi=_relu)
    xn = mm(r, w2, name=f"{tag}_down", pre_a=_sq, epi=_add, extras=(x,))
    return xn, (h, r)


def mlp_bwd(x, g, w1, w2, saved, dx, dxb, tag):
    h, r = saved
    da = mm(dxb, w2, name=f"{tag}_dact", tb=True, tn=1024, out_dtype=BF16, epi=_times_2r, extras=(r,))
    dw2 = mm(r, dxb, name=f"{tag}_dw2", ta=True, pre_a=_sq)
    dw1 = mm(h, da, name=f"{tag}_dw1", ta=True)
    dh = mm(da, w1, name=f"{tag}_dh", tb=True)
    dx, dxb, dg = rms_bwd(x, g, dh, dx, f"{tag}_rmsb")
    return dx, dxb, dg, dw1, dw2


def _conv_taps(x, S):
    t = lax.broadcasted_iota(jnp.int32, x.shape, 0)
    taps = []
    for j in range(GDN_CONV):
        sh = j - GDN_CONV // 2
        xs = x if sh == 0 else pltpu.roll(x, (-sh) % S, 0)
        taps.append(jnp.where((t + sh >= 0) & (t + sh < S), xs, 0.0))
    return taps


def _qkv_scale(c):
    is_norm = c < 2 * GDN_H
    scale = jnp.where(c < GDN_H, GDN_DK ** -0.5, 1.0)
    return is_norm, scale


def gdn_pre_fwd(proj, convw, name):
    S = proj.shape[0]

    def body(p_ref, w_ref, o_ref):
        c = pl.program_id(0)
        x = p_ref[...]
        w = w_ref[...]
        y = jnp.zeros_like(x)
        for j, xs in enumerate(_conv_taps(x, S)):
            y = y + w[j:j + 1, :] * xs
        t = y * _sigmoid(y)
        is_norm, scale = _qkv_scale(c)
        r = lax.rsqrt(jnp.sum(t * t, axis=-1, keepdims=True) + 1e-6)
        o_ref[...] = jnp.where(is_norm, t * r * scale, t)

    return pl.pallas_call(
        body, grid=(GDN_QKV // LANES,),
        in_specs=[BS((S, LANES), lambda c: (0, c)), BS((8, LANES), lambda c: (0, c))],
        out_specs=BS((S, LANES), lambda c: (0, c)),
        out_shape=SDS((S, GDN_QKV), F32), name=name, compiler_params=_cp(("parallel",)))(proj, convw)


def gdn_pre_bwd(proj, convw, dqkv, name):
    S = proj.shape[0]

    def body(p_ref, w_ref, d_ref, dp_ref, dw_ref):
        c = pl.program_id(0)
        x = p_ref[...]
        w = w_ref[...]
        taps = _conv_taps(x, S)
        y = jnp.zeros_like(x)
        for j, xs in enumerate(taps):
            y = y + w[j:j + 1, :] * xs
        sg = _sigmoid(y)
        t = y * sg
        is_norm, scale = _qkv_scale(c)
        dout = d_ref[0, 0] + d_ref[1, 0]
        r = lax.rsqrt(jnp.sum(t * t, axis=-1, keepdims=True) + 1e-6)
        n = t * r
        dn = dout * scale
        dt_norm = r * (dn - n * jnp.sum(dn * n, axis=-1, keepdims=True))
        dt = jnp.where(is_norm, dt_norm, dout)
        dy = dt * (sg * (1.0 + y * (1.0 - sg)))
        row = lax.broadcasted_iota(jnp.int32, (8, LANES), 0)
        dw = jnp.zeros((8, LANES), F32)
        for j, xs in enumerate(taps):
            dw = dw + jnp.where(row == j, jnp.sum(dy * xs, axis=0, keepdims=True), 0.0)
        dw_ref[...] = dw
        tt = lax.broadcasted_iota(jnp.int32, x.shape, 0)
        dx = jnp.zeros_like(x)
        for j in range(GDN_CONV):
            sh = j - GDN_CONV // 2
            ds = dy if sh == 0 else pltpu.roll(dy, sh % S, 0)
            dx = dx + w[j:j + 1, :] * jnp.where((tt - sh >= 0) & (tt - sh < S), ds, 0.0)
        dp_ref[...] = dx.astype(BF16)

    return pl.pallas_call(
        body, grid=(GDN_QKV // LANES,),
        in_specs=[BS((S, LANES), lambda c: (0, c)), BS((8, LANES), lambda c: (0, c)),
                  BS((2, 1, S, LANES), lambda c: (0, c // GDN_H, 0, c % GDN_H))],
        out_specs=[BS((S, LANES), lambda c: (0, c)), BS((8, LANES), lambda c: (0, c))],
        out_shape=[SDS((S, GDN_QKV), BF16), SDS((8, GDN_QKV), F32)],
        name=name, compiler_params=_cp(("parallel",)))(proj, convw, dqkv)


def _chunk_sum_matrix(n, upper):
    i = lax.broadcasted_iota(jnp.int32, (n, n), 0)
    j = lax.broadcasted_iota(jnp.int32, (n, n), 1)
    same = (i // GDN_C) == (j // GDN_C)
    tri = (i <= j) if upper else (i >= j)
    return jnp.where(same & tri, 1.0, 0.0).astype(F32)


def _gate_lanes(shape):
    lane = lax.broadcasted_iota(jnp.int32, shape, 1)
    return lane < GDN_H, (lane >= GDN_H) & (lane < 2 * GDN_H), (lane >= 2 * GDN_H) & (lane < 4 * GDN_H)


def gdn_gate_fwd(proj, prm, name):
    S = proj.shape[0]
    tm = min(512, S)
    ct = GDN_INP // LANES - 1

    def body(p_ref, prm_ref, o_ref):
        ab = p_ref[...]
        a_log = prm_ref[0:1, :]
        dtb = prm_ref[1:2, :]
        z = ab + dtb
        sp = jnp.maximum(z, 0.0) + jnp.log(1.0 + jnp.exp(-jnp.abs(z)))
        g = -jnp.exp(a_log) * sp
        is_f, is_b, is_beta = _gate_lanes(ab.shape)
        gf = _dot(_chunk_sum_matrix(tm, False), jnp.where(is_f, g, 0.0), HI)
        gbk = _dot(_chunk_sum_matrix(tm, True), jnp.where(is_b, g, 0.0), HI)
        o_ref[...] = gf + gbk + jnp.where(is_beta, _sigmoid(ab), 0.0)

    return pl.pallas_call(
        body, grid=(S // tm,),
        in_specs=[BS((tm, LANES), lambda i: (i, ct)), BS((8, LANES), lambda i: (0, 0))],
        out_specs=BS((tm, LANES), lambda i: (i, 0)),
        out_shape=SDS((S, LANES), F32), name=name, compiler_params=_cp(("parallel",)))(proj, prm)


def gdn_gate_bwd(proj, prm, dgb, name):
    S = proj.shape[0]
    tm = min(512, S)
    ct = GDN_INP // LANES - 1

    def body(p_ref, prm_ref, d_ref, dab_ref, dprm_ref):
        i = pl.program_id(0)
        ab = p_ref[...]
        a_log = prm_ref[0:1, :]
        dtb = prm_ref[1:2, :]
        z = ab + dtb
        sp = jnp.maximum(z, 0.0) + jnp.log(1.0 + jnp.exp(-jnp.abs(z)))
        ea = jnp.exp(a_log)
        g = -ea * sp
        is_f, is_b, is_beta = _gate_lanes(ab.shape)
        d = d_ref[...]
        dg = (_dot_tn(_chunk_sum_matrix(tm, False), jnp.where(is_f, d, 0.0), HI)
              + _dot_tn(_chunk_sum_matrix(tm, True), jnp.where(is_b, d, 0.0), HI))
        da = dg * (-ea) * _sigmoid(z)
        beta = _sigmoid(ab)
        dab_ref[...] = jnp.where(is_beta, d * beta * (1.0 - beta), da).astype(BF16)
        row = lax.broadcasted_iota(jnp.int32, (8, LANES), 0)
        part = (jnp.where(row == 0, jnp.sum(dg * g, axis=0, keepdims=True), 0.0)
                + jnp.where(row == 1, jnp.sum(da, axis=0, keepdims=True), 0.0))

        @pl.when(i == 0)
        def _():
            dprm_ref[...] = part

        @pl.when(i > 0)
        def _():
            dprm_ref[...] += part

    return pl.pallas_call(
        body, grid=(S // tm,),
        in_specs=[BS((tm, LANES), lambda i: (i, ct)), BS((8, LANES), lambda i: (0, 0)), BS((tm, LANES), lambda i: (i, 0))],
        out_specs=[BS((tm, LANES), lambda i: (i, 0)), BS((8, LANES), lambda i: (0, 0))],
        out_shape=[SDS((S, LANES), BF16), SDS((8, LANES), F32)],
        name=name, compiler_params=_cp(("arbitrary",)))(proj, prm, dgb)


def _tri_masks(d):
    i = lax.broadcasted_iota(jnp.int32, (GDN_C, GDN_C), 0)
    j = lax.broadcasted_iota(jnp.int32, (GDN_C, GDN_C), 1)
    s = (i - j) * (1 - 2 * d)
    return s >= 0, s > 0


def _split(a):
    hi = _bf(a)
    return hi, _bf(a - hi.astype(F32))


def _dot3(a, b):
    return _dot(a[0], b[0]) + (_dot(a[0], b[1]) + _dot(a[1], b[0]))


def _inv_unit_tri_many(mats):
    i = lax.broadcasted_iota(jnp.int32, mats[0].shape, 0)
    j = lax.broadcasted_iota(jnp.int32, mats[0].shape, 1)
    eye = jnp.where(i == j, 1.0, 0.0)
    ms = [-a for a in mats]
    ps = [eye + m for m in ms]
    for _ in range(int(math.log2(GDN_C)) - 1):
        sp = [_split(m) for m in ms]
        ms = [_dot3(s, s) for s in sp]
        sp = [_split(m) for m in ms]
        pp = [_split(p) for p in ps]
        ps = [p + _dot3(a, b) for p, a, b in zip(ps, pp, sp)]
    return ps


def _lane_col(x, lane_idx):
    lane = lax.broadcasted_iota(jnp.int32, x.shape, 1)
    return jnp.sum(jnp.where(lane == lane_idx, x, 0.0), axis=1, keepdims=True)


def _chunk_gates(gb_ref, grow_ref, hh, ci, d, head):
    gbv = gb_ref[ci * GDN_C:(ci + 1) * GDN_C, :]
    gcol = _lane_col(gbv, d * GDN_H + head)
    bcol = _lane_col(gbv, 2 * GDN_H + d * GDN_H + head)
    glast = jnp.where(d == 0, gcol[GDN_C - 1:GDN_C, :], gcol[0:1, :])
    return gcol, bcol, grow_ref[hh, ci:ci + 1, :], glast


def _chunk_base(q, k, gcol, grow, bcol, glast, d):
    incl, strict = _tri_masks(d)
    decay = jnp.where(incl, jnp.exp(jnp.where(incl, gcol - grow, 0.0)), 0.0)
    kb = k * bcol
    kk = _dot_nt(_bf(kb), _bf(k))
    qk = _dot_nt(_bf(q), _bf(k))
    eg = jnp.exp(gcol)
    ek = jnp.exp(glast - gcol)
    return dict(incl=incl, strict=strict, decay=decay, kb=kb, kk=kk, qk=qk, eg=eg, ek=ek, q_dec=q * eg, k_dec=k * ek,
                bcol=bcol, glast=glast)


def _block_terms(q_ref, k_ref, v_ref, gb_ref, grow_ref, d, h, hp):
    keys = [(hh, ci) for hh in range(hp) for ci in range(GDN_GC)]
    ts = []
    for hh, ci in keys:
        rows = slice(ci * GDN_C, (ci + 1) * GDN_C)
        cols = slice(hh * GDN_DK, (hh + 1) * GDN_DK)
        gcol, bcol, grow_v, glast = _chunk_gates(gb_ref, grow_ref, hh, ci, d, h * hp + hh)
        t = _chunk_base(q_ref[rows, cols], k_ref[rows, cols], gcol, grow_v, bcol, glast, d)
        t["v"] = v_ref[rows, cols]
        ts.append(t)
    tinvs = _inv_unit_tri_many([jnp.where(t["strict"], t["kk"] * t["decay"], 0.0) for t in ts])
    sp = [_split(x) for x in tinvs]
    us = [_dot3(s, _split(t["v"] * t["bcol"])) for s, t in zip(sp, ts)]
    ws = [_dot3(s, _split(t["kb"] * t["eg"])) for s, t in zip(sp, ts)]
    for t, tinv, u, w in zip(ts, tinvs, us, ws):
        t.update(tinv=tinv, u=u, w=w)
    return keys, ts


def _gdn_specs(S, nblk, order, hp):
    R = GDN_GC * GDN_C
    wd = hp * GDN_DK
    hb = GDN_H // hp

    def qkv_spec(part):
        return BS((R, wd), lambda d, h, n: (order(d, n), part * hb + h))

    gb_spec = BS((R, LANES), lambda d, h, n: (order(d, n), 0))
    grow_spec = BS((hp, GDN_GC, GDN_C), lambda d, h, n: (d * hb + h, order(d, n), 0))
    st_spec = BS((1, hp, GDN_GC, GDN_DK, GDN_DK), lambda d, h, n: (d, h, order(d, n), 0, 0))
    return qkv_spec, gb_spec, grow_spec, st_spec


def _lane_row(x):
    return jnp.broadcast_to(x, (1, LANES))


def _side_parts(side):
    if side is None:
        return [], [], [], [], []
    return [ANY] * len(side.ins), [ANY] * len(side.outs), list(side.outs), list(side.sems), list(side.ins)


def _side_run(side, refs, n_in, n_out, n_scr, first, last):
    if side is None:
        return
    ns, no, nm = len(side.ins), len(side.outs), len(side.sems)
    s_in = refs[n_in:n_in + ns]
    s_out = refs[n_in + ns + n_out:n_in + ns + n_out + no]
    s_sem = refs[len(refs) - nm:]

    @pl.when(first)
    def _():
        side.start(s_in, s_out, s_sem)

    @pl.when(last)
    def _():
        side.wait(s_in, s_out, s_sem)


def gdn_scan_fwd(qkv, gb, grow, name, side=None):
    S = qkv.shape[0]
    R = GDN_GC * GDN_C
    nblk = S // R
    nc = S // GDN_C
    hp = GDN_HP_FWD
    wd = hp * GDN_DK
    heads = range(hp)

    def order(d, n):
        return n + d * (nblk - 1 - 2 * n)

    qkv_spec, gb_spec, grow_spec, st_spec = _gdn_specs(S, nblk, order, hp)

    s_in, s_out, s_shape, s_scr_shapes, s_ops = _side_parts(side)
    hb = GDN_H // hp

    def body(*refs):
        q_ref, k_ref, v_ref, gb_ref, grow_ref = refs[:5]
        o_ref, st_ref = refs[5 + len(s_in):7 + len(s_in)]
        s_scr, u_scr, w_scr, qd_scr, kd_scr, in_scr, egl_scr = refs[7 + len(s_in) + len(s_out):14 + len(s_in) + len(s_out)]
        d = pl.program_id(0)
        h = pl.program_id(1)
        n = pl.program_id(2)
        _side_run(side, refs, 5, 2, 7, (d == 0) & (h == 0) & (n == 0), (d == 1) & (h == hb - 1) & (n == nblk - 1))

        @pl.when(n == 0)
        def _():
            s_scr[...] = jnp.zeros_like(s_scr)

        keys, ts = _block_terms(q_ref, k_ref, v_ref, gb_ref, grow_ref, d, h, hp)
        for (hh, ci), t in zip(keys, ts):
            u_scr[hh, ci] = t["u"]
            w_scr[hh, ci] = _bf(t["w"])
            qd_scr[hh, ci] = _bf(t["q_dec"])
            kd_scr[hh, ci] = _bf(t["k_dec"])
            in_scr[hh, ci] = _bf(jnp.where(t["incl"], t["qk"] * t["decay"], 0.0))
            egl_scr[hh, ci] = _lane_row(jnp.exp(t["glast"]))

        def chunk(cc, carry):
            ci = cc + d * (GDN_GC - 1 - 2 * cc)
            rows = pl.ds(pl.multiple_of(ci * GDN_C, GDN_C), GDN_C)
            sts = [s_scr[hh] for hh in heads]
            for hh in heads:
                st_ref[0, hh, ci] = sts[hh]
            sbs = [_bf(st) for st in sts]
            vns = [_bf(u_scr[hh, ci] - _dot(w_scr[hh, ci], sbs[hh])) for hh in heads]
            for hh in heads:
                s_scr[hh] = sts[hh] * egl_scr[hh, ci] + _dot_tn(kd_scr[hh, ci], vns[hh])
            for hh in heads:
                o_ref[0, rows, hh * GDN_DK:(hh + 1) * GDN_DK] = _dot(qd_scr[hh, ci], sbs[hh]) + _dot(in_scr[hh, ci], vns[hh])
            return carry

        lax.fori_loop(0, GDN_GC, chunk, 0)

    blk = (hp, GDN_GC, GDN_C, GDN_DK)
    return pl.pallas_call(
        body, grid=(2, GDN_H // hp, nblk),
        in_specs=[qkv_spec(0), qkv_spec(1), qkv_spec(2), gb_spec, grow_spec] + s_in,
        out_specs=[BS((1, R, wd), lambda d, h, n: (d, order(d, n), h)), st_spec] + s_out,
        out_shape=[SDS((2, S, GDN_H * GDN_DK), F32), SDS((2, GDN_H, nc, GDN_DK, GDN_DK), F32)] + s_shape,
        scratch_shapes=[pltpu.VMEM((hp, GDN_DK, GDN_DK), F32), pltpu.VMEM(blk, F32), pltpu.VMEM(blk, BF16),
                        pltpu.VMEM(blk, BF16), pltpu.VMEM(blk, BF16), pltpu.VMEM((hp, GDN_GC, GDN_C, GDN_C), BF16),
                        pltpu.VMEM((hp, GDN_GC, 1, LANES), F32)] + s_scr_shapes,
        name=name, compiler_params=_cp(("arbitrary", "arbitrary", "arbitrary")))(qkv, qkv, qkv, gb, grow, *s_ops)


def gdn_scan_bwd(qkv, gb, grow, states, do, name, side=None):
    S = qkv.shape[0]
    R = GDN_GC * GDN_C
    nblk = S // R
    hp = GDN_HP_BWD
    wd = hp * GDN_DK
    heads = range(hp)

    def order(d, n):
        return (nblk - 1 - n) - d * (nblk - 1 - 2 * n)

    qkv_spec, gb_spec, grow_spec, st_spec = _gdn_specs(S, nblk, order, hp)

    s_in, s_out, s_shape, s_scr_shapes, s_ops = _side_parts(side)
    hb = GDN_H // hp

    def body(*refs):
        q_ref, k_ref, v_ref, gb_ref, grow_ref, st_ref, do_ref = refs[:7]
        dqkv_ref, dgate_ref = refs[7 + len(s_in):9 + len(s_in)]
        (ds_scr, w_scr, kd_scr, dv1_scr, qtdo_scr, egl_scr, dsin_scr, dvn_scr,
         sdot_scr) = refs[9 + len(s_in) + len(s_out):18 + len(s_in) + len(s_out)]
        d = pl.program_id(0)
        h = pl.program_id(1)
        n = pl.program_id(2)
        _side_run(side, refs, 7, 2, 9, (d == 0) & (h == 0) & (n == 0), (d == 1) & (h == hb - 1) & (n == nblk - 1))

        @pl.when(n == 0)
        def _():
            ds_scr[...] = jnp.zeros_like(ds_scr)

        keys, ts = _block_terms(q_ref, k_ref, v_ref, gb_ref, grow_ref, d, h, hp)
        for (hh, ci), t in zip(keys, ts):
            rows = slice(ci * GDN_C, (ci + 1) * GDN_C)
            t["wb"] = _bf(t["w"])
            t["dob"] = _bf(do_ref[rows, hh * GDN_DK:(hh + 1) * GDN_DK])
            t["sb"] = _bf(st_ref[0, hh, ci])
        for (hh, ci), t in zip(keys, ts):
            t["vnb"] = _bf(t["u"] - _dot(t["wb"], t["sb"]))
            w_scr[hh, ci] = t["wb"]
            kd_scr[hh, ci] = _bf(t["k_dec"])
            dv1_scr[hh, ci] = _dot_tn(_bf(jnp.where(t["incl"], t["qk"] * t["decay"], 0.0)), t["dob"])
            qtdo_scr[hh, ci] = _dot_tn(_bf(t["q_dec"]), t["dob"])
            egl_scr[hh, ci] = _lane_row(jnp.exp(t["glast"]))

        def chunk(cc, carry):
            ci = (GDN_GC - 1 - cc) - d * (GDN_GC - 1 - 2 * cc)
            dsns = [ds_scr[hh] for hh in heads]
            dsbs = [_bf(x) for x in dsns]
            dvns = [dv1_scr[hh, ci] + _dot(kd_scr[hh, ci], dsbs[hh]) for hh in heads]
            for hh in heads:
                ds_scr[hh] = qtdo_scr[hh, ci] + egl_scr[hh, ci] * dsns[hh] - _dot_tn(w_scr[hh, ci], _bf(dvns[hh]))
            for hh in heads:
                dsin_scr[hh, ci] = dsbs[hh]
                dvn_scr[hh, ci] = dvns[hh]
                sd = jnp.sum(jnp.sum(st_ref[0, hh, ci] * dsns[hh], axis=1, keepdims=True), axis=0, keepdims=True)
                sdot_scr[hh, ci] = _lane_row(sd)
            return carry

        lax.fori_loop(0, GDN_GC, chunk, 0)

        for (hh, ci), t in zip(keys, ts):
            t["d_vnew"] = dvn_scr[hh, ci]
            t["dvb"] = _bf(t["d_vnew"])
            t["dsb"] = dsin_scr[hh, ci]
        for t in ts:
            t["d_intra"] = jnp.where(t["incl"], _dot_nt(t["dob"], t["vnb"]), 0.0)
            t["d_qdec"] = _dot_nt(t["dob"], t["sb"])
            t["d_kdec"] = _dot_nt(t["vnb"], t["dsb"])
            t["dw"] = -_dot_nt(t["dvb"], t["sb"])
        for t in ts:
            tts = _split(t["tinv"].T)
            t["d_ru"] = _dot3(tts, _split(t["d_vnew"]))
            t["d_rw"] = _dot3(tts, _split(t["dw"]))
        for t in ts:
            t["da"] = -jnp.where(t["strict"], _dot_nt(_bf(t["d_ru"]), _bf(t["u"])) + _dot_nt(_bf(t["d_rw"]), t["wb"]), 0.0)
        for (hh, ci), t in zip(keys, ts):
            rows = slice(ci * GDN_C, (ci + 1) * GDN_C)
            cols = slice(hh * GDN_DK, (hh + 1) * GDN_DK)
            q, k, v = q_ref[rows, cols], k_ref[rows, cols], t["v"]
            decay, kb, eg, ek, bcol = t["decay"], t["kb"], t["eg"], t["ek"], t["bcol"]
            d_ru, d_rw, da, d_intra, d_qdec, d_kdec = t["d_ru"], t["d_rw"], t["da"], t["d_intra"], t["d_qdec"], t["d_kdec"]
            kbf, qbf = _bf(k), _bf(q)
            dgl = egl_scr[hh, ci][:, 0:1] * sdot_scr[hh, ci][:, 0:1]
            dv = d_ru * bcol
            dbeta = jnp.sum(d_ru * v, axis=1, keepdims=True)
            dkb = d_rw * eg
            dg = jnp.sum(d_rw * kb, axis=1, keepdims=True) * eg
            dkk = _bf(da * decay)
            dqk = _bf(d_intra * decay)
            dkb = dkb + _dot(dkk, kbf)
            dk = _dot_tn(dkk, _bf(kb)) + _dot_tn(dqk, qbf)
            dq = _dot(dqk, kbf) + d_qdec * eg
            dd = (da * t["kk"] + d_intra * t["qk"]) * decay
            dg = dg + jnp.sum(dd, axis=1, keepdims=True) - jnp.sum(dd.T, axis=1, keepdims=True)
            dg = dg + jnp.sum(d_qdec * t["q_dec"], axis=1, keepdims=True)
            dk = dk + d_kdec * ek
            ee = jnp.sum(d_kdec * t["k_dec"], axis=1, keepdims=True)
            dg = dg - ee
            dgl = dgl + jnp.sum(ee, axis=0, keepdims=True)
            dk = dk + dkb * bcol
            dbeta = dbeta + jnp.sum(dkb * k, axis=1, keepdims=True)
            ridx = lax.broadcasted_iota(jnp.int32, (GDN_C, 1), 0)
            dg = dg + jnp.where(ridx == (GDN_C - 1) * (1 - d), dgl, 0.0)
            dqkv_ref[0, 0, rows, cols] = dq
            dqkv_ref[0, 1, rows, cols] = dk
            dqkv_ref[0, 2, rows, cols] = dv
            lane2 = lax.broadcasted_iota(jnp.int32, (GDN_C, 2), 1)
            dgate_ref[0, hh, rows, :] = jnp.where(lane2 == 0, dg, dbeta)

    blk = (hp, GDN_GC, GDN_C, GDN_DK)
    sq = (hp, GDN_GC, GDN_DK, GDN_DK)
    row = (hp, GDN_GC, 1, LANES)
    return pl.pallas_call(
        body, grid=(2, GDN_H // hp, nblk),
        in_specs=[qkv_spec(0), qkv_spec(1), qkv_spec(2), gb_spec, grow_spec, st_spec,
                  BS((R, wd), lambda d, h, n: (order(d, n), h))] + s_in,
        out_specs=[BS((1, 3, R, wd), lambda d, h, n: (d, 0, order(d, n), h)),
                   BS((1, hp, R, 2), lambda d, h, n: (d, h, order(d, n), 0))] + s_out,
        out_shape=[SDS((2, 3, S, GDN_H * GDN_DK), F32), SDS((2, GDN_H, S, 2), F32)] + s_shape,
        scratch_shapes=[pltpu.VMEM((hp, GDN_DK, GDN_DK), F32), pltpu.VMEM(blk, BF16), pltpu.VMEM(blk, BF16),
                        pltpu.VMEM(blk, F32), pltpu.VMEM(sq, F32), pltpu.VMEM(row, F32), pltpu.VMEM(sq, BF16),
                        pltpu.VMEM(blk, F32), pltpu.VMEM(row, F32)] + s_scr_shapes,
        name=name, compiler_params=_cp(("arbitrary", "arbitrary", "arbitrary")))(qkv, qkv, qkv, gb, grow, states, do, *s_ops)


def gdn_post_fwd(o2, proj, nw, name):
    S = proj.shape[0]
    tm = min(512, S)
    zoff = GDN_QKV // LANES

    def body(o_ref, z_ref, nw_ref, y_ref):
        o = o_ref[0] + o_ref[1]
        z = z_ref[...]
        r = lax.rsqrt(jnp.mean(o * o, axis=-1, keepdims=True) + RMS_EPS)
        y_ref[...] = (o * r * nw_ref[...] * (z * _sigmoid(z))).astype(BF16)

    return pl.pallas_call(
        body, grid=(S // tm, GDN_H),
        in_specs=[BS((2, tm, LANES), lambda i, h: (0, i, h)), BS((tm, LANES), lambda i, h: (i, zoff + h)),
                  BS((1, LANES), lambda i, h: (0, 0))],
        out_specs=BS((tm, LANES), lambda i, h: (i, h)),
        out_shape=SDS((S, GDN_H * GDN_DK), BF16), name=name, compiler_params=_cp(("parallel", "parallel")))(o2, proj, nw)


def gdn_post_bwd(o2, proj, nw, dy, name):
    S = proj.shape[0]
    tm = min(512, S)
    zoff = GDN_QKV // LANES

    def body(o_ref, z_ref, nw_ref, dy_ref, do_ref, dz_ref, dnw_ref):
        first = (pl.program_id(0) == 0) & (pl.program_id(1) == 0)
        o = o_ref[0] + o_ref[1]
        z = z_ref[...]
        nwv = nw_ref[...]
        dyv = dy_ref[...]
        r = lax.rsqrt(jnp.mean(o * o, axis=-1, keepdims=True) + RMS_EPS)
        n = o * r
        sg = _sigmoid(z)
        sz = z * sg
        dz_ref[...] = (dyv * n * nwv * (sg * (1.0 + z * (1.0 - sg)))).astype(BF16)
        dn = dyv * nwv * sz
        do_ref[...] = r * (dn - n * jnp.mean(dn * n, axis=-1, keepdims=True))
        part = jnp.sum(dyv * n * sz, axis=0, keepdims=True)

        @pl.when(first)
        def _():
            dnw_ref[...] = part

        @pl.when(jnp.logical_not(first))
        def _():
            dnw_ref[...] += part

    blk = BS((tm, LANES), lambda i, h: (i, h))
    return pl.pallas_call(
        body, grid=(S // tm, GDN_H),
        in_specs=[BS((2, tm, LANES), lambda i, h: (0, i, h)), BS((tm, LANES), lambda i, h: (i, zoff + h)),
                  BS((1, LANES), lambda i, h: (0, 0)), blk],
        out_specs=[blk, blk, BS((1, LANES), lambda i, h: (0, 0))],
        out_shape=[SDS((S, GDN_H * GDN_DK), F32), SDS((S, GDN_H * GDN_DK), BF16), SDS((1, LANES), F32)],
        name=name, compiler_params=_cp(("arbitrary", "arbitrary")))(o2, proj, nw, dy)


def _gate_prm(a_log, dt_bias):
    z = jnp.zeros((8, LANES), F32)
    z = z.at[0, :2 * GDN_H].set(a_log.reshape(-1))
    return z.at[1, :2 * GDN_H].set(dt_bias.reshape(-1))


def gdn_fwd(x, g, w_all, convw, a_log, dt_bias, nw, w_out, tag, side=None):
    S = x.shape[0]
    h = rms_fwd(x, g, f"{tag}_rms")
    proj = mm(h, w_all, name=f"{tag}_proj", tn=1408)
    qkv = gdn_pre_fwd(proj, convw, f"{tag}_pre")
    prm = _gate_prm(a_log, dt_bias)
    gb = gdn_gate_fwd(proj, prm, f"{tag}_gate")
    grow = gb[:, :2 * GDN_H].T.reshape(2 * GDN_H, S // GDN_C, GDN_C)
    o2, states, *side_out = gdn_scan_fwd(qkv, gb, grow, f"{tag}_scan", side)
    y = gdn_post_fwd(o2, proj, nw, f"{tag}_post")
    xn = mm(y, w_out, name=f"{tag}_out", epi=_add, extras=(x,))
    return xn, (h, proj, qkv, prm, gb, grow, o2, states, y), side_out


def gdn_bwd(x, g, w_all, convw, nw, w_out, saved, dx, dxb, tag, side=None):
    S = x.shape[0]
    h, proj, qkv, prm, gb, grow, o2, states, y = saved
    dw_out = mm(y, dxb, name=f"{tag}_dwout", ta=True)
    dy = mm(dxb, w_out, name=f"{tag}_dy", tb=True)
    do, dz, dnw = gdn_post_bwd(o2, proj, nw, dy, f"{tag}_postb")
    dqkv, dgate, *side_out = gdn_scan_bwd(qkv, gb, grow, states, do, f"{tag}_scanb", side)
    dgb = jnp.transpose(dgate, (2, 3, 0, 1)).reshape(S, 4 * GDN_H)
    dgb = jnp.pad(dgb, ((0, 0), (0, LANES - 4 * GDN_H)))
    dab, dprm = gdn_gate_bwd(proj, prm, dgb, f"{tag}_gateb")
    dpq, dconvw = gdn_pre_bwd(proj, convw, dqkv, f"{tag}_preb")
    dproj = jnp.concatenate([dpq, dz, dab], axis=1)
    dw_all = mm(h, dproj, name=f"{tag}_dwin", ta=True, tn=384)
    dh = mm(dproj, w_all, name=f"{tag}_dh", tb=True)
    dx, dxb, dg = rms_bwd(x, g, dh, dx, f"{tag}_rmsb")
    da_log = dprm[0, :2 * GDN_H].reshape(2, GDN_H)
    ddt = dprm[1, :2 * GDN_H].reshape(2, GDN_H)
    return dx, dxb, dg, dw_all, dconvw, da_log, ddt, dnw, dw_out, side_out


def _rel_bucket_np(rel):
    nb = REL_BUCKETS // 2
    max_exact = nb // 2
    ret = np.where(rel > 0, nb, 0)
    n = np.abs(rel)
    nf = np.maximum(n, 1).astype(np.float32)
    large = max_exact + (np.log(nf / max_exact) / np.float32(math.log(REL_MAX_DIST / max_exact))
                         * (nb - max_exact)).astype(np.int32)
    large = np.minimum(large, nb - 1)
    return ret + np.where(n < max_exact, n, large)


def _toeplitz(f, rows, cols):
    period = rows + cols
    e = jnp.pad(f, ((0, 0), (0, period - f.shape[1])))
    y = jnp.tile(e, (1, rows))[:, :rows * (period - 1)]
    return y.reshape(f.shape[0], rows, period - 1)[:, :, :cols]


ATT_Q = DSWA_HALF
ATT_W = 3 * DSWA_HALF
ATT_TB = 1024
ATT_PAIRS = DSWA_HG // 2


def _bias_mats(rel_table, gi):
    _, dil = DSWA_CFG[gi]
    offs = np.arange(-DSWA_HALF, DSWA_HALF + 1)
    onehot = jnp.asarray(np.eye(REL_BUCKETS, dtype=np.float32)[_rel_bucket_np(offs * dil)])
    f = jnp.dot(onehot, rel_table, precision=HI)[:, gi * DSWA_HG:(gi + 1) * DSWA_HG].T
    bias = _toeplitz(f, ATT_Q, ATT_W)
    bias_t = jnp.transpose(_toeplitz(f[:, ::-1], ATT_Q, ATT_W), (0, 2, 1))
    return bias.reshape(ATT_PAIRS, 2, ATT_Q, ATT_W), bias_t.reshape(ATT_PAIRS, 2, ATT_W, ATT_Q)


def _att_specs(S, d, col):
    halo = DSWA_HALF * d
    per = ATT_TB // halo
    last = S // halo - 1
    cur = BS((ATT_TB, LANES), lambda p, tb: (tb, col(p)))
    prev = BS((halo, LANES), lambda p, tb: (jnp.maximum(tb * per - 1, 0), col(p)))
    nxt = BS((halo, LANES), lambda p, tb: (jnp.minimum((tb + 1) * per, last), col(p)))
    return prev, cur, nxt


def _att_specs3(S, d, lead):
    halo = DSWA_HALF * d
    per = ATT_TB // halo
    last = S // halo - 1
    cur = BS((1, ATT_TB, LANES), lambda p, tb: (lead(p), tb, 0))
    prev = BS((1, halo, LANES), lambda p, tb: (lead(p), jnp.maximum(tb * per - 1, 0), 0))
    nxt = BS((1, halo, LANES), lambda p, tb: (lead(p), jnp.minimum((tb + 1) * per, last), 0))
    return prev, cur, nxt


class _Pieces:
    def __init__(self, prev, cur, nxt, d, lead=None, cast=None):
        self.refs, self.d, self.lead, self.cast, self.cache = (prev, cur, nxt), d, lead, cast, {}
        self.halo = DSWA_HALF * d
        self.nsb = ATT_TB // self.halo

    def __call__(self, r, sb):
        if (r, sb) not in self.cache:
            ref = self.refs[0] if sb < 0 else self.refs[2] if sb >= self.nsb else self.refs[1]
            start = r + (self.halo * sb if 0 <= sb < self.nsb else 0)
            rows = pl.ds(start, ATT_Q, stride=self.d) if self.d > 1 else pl.ds(start, ATT_Q)
            v = ref[rows, :] if self.lead is None else ref[0, rows, :]
            self.cache[(r, sb)] = v if self.cast is None else v.astype(self.cast)
        return self.cache[(r, sb)]

    def window(self, r, sb):
        return jnp.concatenate([self(r, sb - 1), self(r, sb), self(r, sb + 1)], axis=0)


ATT_GROUP = 8


def _tile_groups(d, nsb):
    tiles = [(r, sb) for r in range(d) for sb in range(nsb)]
    return [tiles[i:i + ATT_GROUP] for i in range(0, len(tiles), ATT_GROUP)]


def _tile_rows(r, sb, d):
    start = r + DSWA_HALF * d * sb
    return pl.ds(start, ATT_Q, stride=d) if d > 1 else pl.ds(start, ATT_Q)


def _tile_valid(tb, r, sb, d, S, transposed):
    shape = (ATT_W, ATT_Q) if transposed else (ATT_Q, ATT_W)
    blk = lax.broadcasted_iota(jnp.int32, shape, 1 if transposed else 0)
    win = lax.broadcasted_iota(jnp.int32, shape, 0 if transposed else 1)
    tok = tb * ATT_TB + r + d * (DSWA_HALF * (sb - 1) + win)
    return (jnp.abs(win - DSWA_HALF - blk) <= DSWA_HALF) & (tok >= 0) & (tok < S)


def _head_masks():
    lane = lax.broadcasted_iota(jnp.int32, (1, LANES), 1)
    return [lane < DSWA_E, lane >= DSWA_E], lane


def attn_fwd(qkv, bias, gi, name):
    S = qkv.shape[0]
    d = DSWA_CFG[gi][1]
    nsb = ATT_TB // (DSWA_HALF * d)
    npair = DSWA_HEADS // 2
    q_spec = _att_specs(S, d, lambda p: gi * ATT_PAIRS + p)[1]
    k_specs = _att_specs(S, d, lambda p: npair + gi * ATT_PAIRS + p)
    v_specs = _att_specs(S, d, lambda p: 2 * npair + gi * ATT_PAIRS + p)

    def body(q_ref, kp, kc, kn, vp, vc, vn, b_ref, o_ref, lse_ref):
        tb = pl.program_id(1)
        masks, lane = _head_masks()
        kpc = _Pieces(kp, kc, kn, d, cast=BF16)
        vpc = _Pieces(vp, vc, vn, d, cast=BF16)
        scale = DSWA_E ** -0.5
        for grp in _tile_groups(d, nsb):
            rows = [_tile_rows(r, sb, d) for r, sb in grp]
            qs = [q_ref[rw, :] for rw in rows]
            kws = [kpc.window(r, sb) for r, sb in grp]
            vws = [vpc.window(r, sb) for r, sb in grp]
            valids = [_tile_valid(tb, r, sb, d, S, False) for r, sb in grp]
            both = [(t, hh) for t in range(len(grp)) for hh in range(2)]
            ss = [_dot_nt(_bf(jnp.where(masks[hh], qs[t], 0.0)), kws[t]) * scale + b_ref[0, hh] for t, hh in both]
            ss = [jnp.where(valids[t], s, NEG_INF) for (t, hh), s in zip(both, ss)]
            ms = [jnp.max(s, axis=-1, keepdims=True) for s in ss]
            ps = [jnp.exp(s - m) for s, m in zip(ss, ms)]
            ls = [jnp.sum(p, axis=-1, keepdims=True) for p in ps]
            os = [_dot(_bf(p / l), vws[t]) for (t, hh), p, l in zip(both, ps, ls)]
            for t, rw in enumerate(rows):
                o_ref[rw, :] = jnp.where(masks[0], os[2 * t], os[2 * t + 1])
                lse_ref[0, rw, :] = (jnp.where(lane == 0, ms[2 * t] + jnp.log(ls[2 * t]), 0.0)
                                     + jnp.where(lane == 1, ms[2 * t + 1] + jnp.log(ls[2 * t + 1]), 0.0))

    return pl.pallas_call(
        body, grid=(ATT_PAIRS, S // ATT_TB),
        in_specs=[q_spec, *k_specs, *v_specs, BS((1, 2, ATT_Q, ATT_W), lambda p, tb: (p, 0, 0, 0))],
        out_specs=[BS((ATT_TB, LANES), lambda p, tb: (tb, p)), BS((1, ATT_TB, LANES), lambda p, tb: (p, tb, 0))],
        out_shape=[SDS((S, DSWA_HG * DSWA_E), F32), SDS((ATT_PAIRS, S, LANES), F32)],
        name=name, compiler_params=_cp(("parallel", "parallel")))(qkv, qkv, qkv, qkv, qkv, qkv, qkv, bias)


def attn_bwd_q(qkv, bias, lse, do, dd, gi, name):
    S = qkv.shape[0]
    d = DSWA_CFG[gi][1]
    nsb = ATT_TB // (DSWA_HALF * d)
    npair = DSWA_HEADS // 2
    q_spec = _att_specs(S, d, lambda p: gi * ATT_PAIRS + p)[1]
    k_specs = _att_specs(S, d, lambda p: npair + gi * ATT_PAIRS + p)
    v_specs = _att_specs(S, d, lambda p: 2 * npair + gi * ATT_PAIRS + p)
    bspec = BS((1, 2, ATT_Q, ATT_W), lambda p, tb: (p, 0, 0, 0))

    def body(q_ref, kp, kc, kn, vp, vc, vn, b_ref, lse_ref, do_ref, dd_ref, dq_ref, db_ref):
        tb = pl.program_id(1)
        masks, lane = _head_masks()
        kpc = _Pieces(kp, kc, kn, d, cast=BF16)
        vpc = _Pieces(vp, vc, vn, d, cast=BF16)
        db = [jnp.zeros((ATT_Q, ATT_W), F32), jnp.zeros((ATT_Q, ATT_W), F32)]
        scale = DSWA_E ** -0.5
        for grp in _tile_groups(d, nsb):
            rows = [_tile_rows(r, sb, d) for r, sb in grp]
            qs = [q_ref[rw, :] for rw in rows]
            dos = [do_ref[0, rw, :] for rw in rows]
            lses = [lse_ref[0, rw, :] for rw in rows]
            dds = [dd_ref[0, 0, rw, :] for rw in rows]
            kws = [kpc.window(r, sb) for r, sb in grp]
            vws = [vpc.window(r, sb) for r, sb in grp]
            valids = [_tile_valid(tb, r, sb, d, S, False) for r, sb in grp]
            both = [(t, hh) for t in range(len(grp)) for hh in range(2)]
            ss = [_dot_nt(_bf(jnp.where(masks[hh], qs[t], 0.0)), kws[t]) * scale + b_ref[0, hh] for t, hh in both]
            dps = [_dot_nt(_bf(jnp.where(masks[hh], dos[t], 0.0)), vws[t]) for t, hh in both]
            ps = [jnp.exp(jnp.where(valids[t], s - lses[t][:, hh:hh + 1], NEG_INF)) for (t, hh), s in zip(both, ss)]
            dss = [p * (dp - dds[t][:, hh:hh + 1]) for (t, hh), p, dp in zip(both, ps, dps)]
            dqs = [_dot(_bf(ds), kws[t]) * scale for (t, hh), ds in zip(both, dss)]
            for t, rw in enumerate(rows):
                dq_ref[rw, :] = jnp.where(masks[0], dqs[2 * t], dqs[2 * t + 1])
                db[0] = db[0] + dss[2 * t]
                db[1] = db[1] + dss[2 * t + 1]

        @pl.when(tb == 0)
        def _():
            db_ref[0, 0] = db[0]
            db_ref[0, 1] = db[1]

        @pl.when(tb > 0)
        def _():
            db_ref[0, 0] += db[0]
            db_ref[0, 1] += db[1]

    return pl.pallas_call(
        body, grid=(ATT_PAIRS, S // ATT_TB),
        in_specs=[q_spec, *k_specs, *v_specs, bspec, BS((1, ATT_TB, LANES), lambda p, tb: (p, tb, 0)),
                  BS((1, ATT_TB, LANES), lambda p, tb: (gi, tb, p)), BS((1, 1, ATT_TB, LANES), lambda p, tb: (gi, p, tb, 0))],
        out_specs=[BS((ATT_TB, LANES), lambda p, tb: (tb, p)), bspec],
        out_shape=[SDS((S, DSWA_HG * DSWA_E), F32), SDS((ATT_PAIRS, 2, ATT_Q, ATT_W), F32)],
        name=name, compiler_params=_cp(("parallel", "arbitrary")))(qkv, qkv, qkv, qkv, qkv, qkv, qkv, bias, lse, do, dd)


def attn_bwd_kv(qkv, bias_t, lse, do, dd, gi, name):
    S = qkv.shape[0]
    d = DSWA_CFG[gi][1]
    nsb = ATT_TB // (DSWA_HALF * d)
    npair = DSWA_HEADS // 2
    q_specs = _att_specs(S, d, lambda p: gi * ATT_PAIRS + p)
    k_spec = _att_specs(S, d, lambda p: npair + gi * ATT_PAIRS + p)[1]
    v_spec = _att_specs(S, d, lambda p: 2 * npair + gi * ATT_PAIRS + p)[1]
    halo = DSWA_HALF * d
    per = ATT_TB // halo
    last = S // halo - 1

    def do_spec(rows, blk):
        return BS((1, rows, LANES), lambda p, tb: (gi, blk(tb), p))

    def dd_spec(rows, blk):
        return BS((1, 1, rows, LANES), lambda p, tb: (gi, p, blk(tb), 0))

    blks = [(halo, lambda tb: jnp.maximum(tb * per - 1, 0)), (ATT_TB, lambda tb: tb),
            (halo, lambda tb: jnp.minimum((tb + 1) * per, last))]
    do_specs = [do_spec(*b) for b in blks]
    dd_specs = [dd_spec(*b) for b in blks]
    lse_specs = _att_specs3(S, d, lambda p: p)

    class _Lead4:
        def __init__(self, ref):
            self.ref = ref

        def __getitem__(self, idx):
            return self.ref[(0,) + idx]

    def body(k_ref, v_ref, qp, qc, qn, dop, doc, don, lp, lc, ln, ddp, ddc, ddn, b_ref, dk_ref, dv_ref):
        tb = pl.program_id(1)
        masks, lane = _head_masks()
        qpc = _Pieces(qp, qc, qn, d)
        dopc = _Pieces(dop, doc, don, d, lead=True)
        lpc = _Pieces(lp, lc, ln, d, lead=True)
        ddpc = _Pieces(_Lead4(ddp), _Lead4(ddc), _Lead4(ddn), d, lead=True)
        scale = DSWA_E ** -0.5
        for grp in _tile_groups(d, nsb):
            rows = [_tile_rows(r, sb, d) for r, sb in grp]
            kcs = [_bf(k_ref[rw, :]) for rw in rows]
            vcs = [_bf(v_ref[rw, :]) for rw in rows]
            qws = [qpc.window(r, sb) for r, sb in grp]
            dows = [dopc.window(r, sb) for r, sb in grp]
            lws = [lpc.window(r, sb) for r, sb in grp]
            ddws = [ddpc.window(r, sb) for r, sb in grp]
            qwbs = [_bf(x) for x in qws]
            dowbs = [_bf(x) for x in dows]
            valids = [_tile_valid(tb, r, sb, d, S, True) for r, sb in grp]
            both = [(t, hh) for t in range(len(grp)) for hh in range(2)]
            ss = [_dot_nt(_bf(jnp.where(masks[hh], qws[t], 0.0)), kcs[t]) * scale + b_ref[0, hh] for t, hh in both]
            dps = [_dot_nt(_bf(jnp.where(masks[hh], dows[t], 0.0)), vcs[t]) for t, hh in both]
            ps = [jnp.exp(jnp.where(valids[t], s - lws[t][:, hh:hh + 1], NEG_INF)) for (t, hh), s in zip(both, ss)]
            dvs = [_dot_tn(_bf(p), dowbs[t]) for (t, hh), p in zip(both, ps)]
            dss = [p * (dp - ddws[t][:, hh:hh + 1]) for (t, hh), p, dp in zip(both, ps, dps)]
            dks = [_dot_tn(_bf(ds), qwbs[t]) * scale for (t, hh), ds in zip(both, dss)]
            for t, rw in enumerate(rows):
                dk_ref[rw, :] = jnp.where(masks[0], dks[2 * t], dks[2 * t + 1])
                dv_ref[rw, :] = jnp.where(masks[0], dvs[2 * t], dvs[2 * t + 1])

    out = BS((ATT_TB, LANES), lambda p, tb: (tb, p))
    return pl.pallas_call(
        body, grid=(ATT_PAIRS, S // ATT_TB),
        in_specs=[k_spec, v_spec, *q_specs, *do_specs, *lse_specs, *dd_specs,
                  BS((1, 2, ATT_W, ATT_Q), lambda p, tb: (p, 0, 0, 0))],
        out_specs=[out, out],
        out_shape=[SDS((S, DSWA_HG * DSWA_E), F32), SDS((S, DSWA_HG * DSWA_E), F32)],
        name=name, compiler_params=_cp(("parallel", "parallel")))(
            qkv, qkv, qkv, qkv, qkv, do, do, do, lse, lse, lse, dd, dd, dd, bias_t)


def _pair_alphas(lses):
    m = jnp.maximum(jnp.maximum(lses[0], lses[1]), lses[2])
    e = [jnp.exp(t - m) for t in lses]
    tot = e[0] + e[1] + e[2]
    return [t / tot for t in e]


def _pair_expand(a, lane):
    return jnp.where(lane < DSWA_E, a[:, 0:1], a[:, 1:2])


def combine_fwd(o_raw, lse, name):
    S = o_raw.shape[0]
    tm = min(512, S)

    def body(o_ref, l_ref, y_ref):
        g = pl.program_id(2)
        lane = lax.broadcasted_iota(jnp.int32, (1, LANES), 1)
        alphas = _pair_alphas([l_ref[0, 0], l_ref[1, 0], l_ref[2, 0]])
        a = jnp.where(g == 0, alphas[0], jnp.where(g == 1, alphas[1], alphas[2]))
        y_ref[...] = (o_ref[...] * _pair_expand(a, lane)).astype(BF16)

    blk = BS((tm, LANES), lambda i, p, g: (i, g * ATT_PAIRS + p))
    return pl.pallas_call(
        body, grid=(S // tm, ATT_PAIRS, 3),
        in_specs=[blk, BS((3, 1, tm, LANES), lambda i, p, g: (0, p, i, 0))], out_specs=blk,
        out_shape=SDS((S, DSWA_W), BF16), name=name, compiler_params=_cp(("parallel", "parallel", "parallel")))(o_raw, lse)


def combine_bwd(o_raw, lse, dy, name):
    S = o_raw.shape[0]
    tm = min(512, S)

    def body(o0, o1, o2, l_ref, d0, d1, d2, do_ref, dd_ref):
        lane = lax.broadcasted_iota(jnp.int32, (1, LANES), 1)
        alphas = _pair_alphas([l_ref[0, 0], l_ref[1, 0], l_ref[2, 0]])
        c = jnp.zeros((tm, LANES), F32)
        for g, (o_ref, dy_ref) in enumerate(((o0, d0), (o1, d1), (o2, d2))):
            dyv = dy_ref[...]
            do_ref[g] = dyv * _pair_expand(alphas[g], lane)
            prod = o_ref[...] * dyv
            dal = (jnp.where(lane == 0, jnp.sum(jnp.where(lane < DSWA_E, prod, 0.0), axis=1, keepdims=True), 0.0)
                   + jnp.where(lane == 1, jnp.sum(jnp.where(lane >= DSWA_E, prod, 0.0), axis=1, keepdims=True), 0.0))
            c = c + alphas[g] * dal
        for g in range(3):
            dd_ref[g, 0] = alphas[g] * c

    def col(g):
        return BS((tm, LANES), lambda i, p: (i, g * ATT_PAIRS + p))

    return pl.pallas_call(
        body, grid=(S // tm, ATT_PAIRS),
        in_specs=[col(0), col(1), col(2), BS((3, 1, tm, LANES), lambda i, p: (0, p, i, 0)), col(0), col(1), col(2)],
        out_specs=[BS((3, tm, LANES), lambda i, p: (0, i, p)), BS((3, 1, tm, LANES), lambda i, p: (0, p, i, 0))],
        out_shape=[SDS((3, S, DSWA_HG * DSWA_E), F32), SDS((3, ATT_PAIRS, S, LANES), F32)],
        name=name, compiler_params=_cp(("parallel", "parallel")))(o_raw, o_raw, o_raw, lse, dy, dy, dy)


def dswa_fwd(x, g, w_in, w_out, rel_table, tag):
    h = rms_fwd(x, g, f"{tag}_rms")
    qkv = mm(h, w_in, name=f"{tag}_qkv", tn=1152)
    outs, lses = [], []
    for gi in range(3):
        bias, _ = _bias_mats(rel_table, gi)
        o, lse = attn_fwd(qkv, bias, gi, f"{tag}_att{gi}")
        outs.append(o)
        lses.append(lse)
    o_raw = jnp.concatenate(outs, axis=1)
    lse = jnp.stack(lses)
    y = combine_fwd(o_raw, lse, f"{tag}_comb")
    xn = mm(y, w_out, name=f"{tag}_out", epi=_add, extras=(x,))
    return xn, (h, qkv, o_raw, lse, y)


def dswa_bwd(x, g, w_in, w_out, rel_table, saved, dx, dxb, tag):
    h, qkv, o_raw, lse, y = saved
    dw_out = mm(y, dxb, name=f"{tag}_dwout", ta=True, tm=384)
    dy = mm(dxb, w_out, name=f"{tag}_dy", tb=True, tn=384)
    do_raw, dd = combine_bwd(o_raw, lse, dy, f"{tag}_combb")
    dqs, dks, dvs = [], [], []
    drel = jnp.zeros_like(rel_table)
    for gi in range(3):
        (bias, bias_t), bias_vjp = jax.vjp(lambda tbl: _bias_mats(tbl, gi), rel_table)
        dq, dbias = attn_bwd_q(qkv, bias, lse[gi], do_raw, dd, gi, f"{tag}_attq{gi}")
        dk, dv = attn_bwd_kv(qkv, bias_t, lse[gi], do_raw, dd, gi, f"{tag}_attkv{gi}")
        drel = drel + bias_vjp((dbias, jnp.zeros_like(bias_t)))[0]
        dqs.append(dq)
        dks.append(dk)
        dvs.append(dv)
    dqkv = jnp.concatenate(dqs + dks + dvs, axis=1).astype(BF16)
    dw_in = mm(h, dqkv, name=f"{tag}_dwin", ta=True, tn=384)
    dh = mm(dqkv, w_in, name=f"{tag}_dh", tb=True)
    dx, dxb, dg = rms_bwd(x, g, dh, dx, f"{tag}_rmsb")
    return dx, dxb, dg, dw_in, dw_out, drel


def adamw(w, g, m, v, name):
    shape = w.shape
    last = shape[-1]
    w2, g2, m2, v2 = (t.reshape(-1, last) for t in (w, g, m, v))
    rows = w2.shape[0]
    tr = rows
    if rows > 512:
        tr = next(t for t in (512, 256, 192, 128, 64, 8) if rows % t == 0)
    c1 = 1.0 / (1.0 - ADAM_B1 ** ADAM_STEP)
    c2 = 1.0 / (1.0 - ADAM_B2 ** ADAM_STEP)

    def body(w_ref, g_ref, m_ref, v_ref, d_ref, nm_ref, nv_ref):
        gv = g_ref[...]
        nm = ADAM_B1 * m_ref[...] + (1.0 - ADAM_B1) * gv
        nv = ADAM_B2 * v_ref[...] + (1.0 - ADAM_B2) * (gv * gv)
        nm_ref[...] = nm
        nv_ref[...] = nv
        d_ref[...] = -ADAM_LR * ((nm * c1) / (jnp.sqrt(nv * c2) + ADAM_EPS) + ADAM_WD * w_ref[...])

    spec = BS((tr, last), lambda i: (i, 0))
    outs = pl.pallas_call(
        body, grid=(rows // tr,), in_specs=[spec] * 4, out_specs=[spec] * 3,
        out_shape=[SDS((rows, last), F32)] * 3, name=name, compiler_params=_cp(("parallel",)))(w2, g2, m2, v2)
    return tuple(o.reshape(shape) for o in outs)


def _place():
    x, y, c = lax.axis_index("x"), lax.axis_index("y"), lax.axis_index("c")
    chips = [(1 - x, y), (x, 1 - y), (1 - x, 1 - y)]
    return x, y, c, chips


def _rcopy(src, dst, ssem, rsem, dev):
    return pltpu.make_async_remote_copy(src_ref=src, dst_ref=dst, send_sem=ssem, recv_sem=rsem, device_id=dev,
                                        device_id_type=MESH)


class SideJob(NamedTuple):
    ins: list
    outs: list
    sems: list
    start: Callable
    wait: Callable


def _job(ins, outs, sems, copies):
    def start(in_refs, out_refs, sem_refs):
        for cp in copies(in_refs, out_refs, sem_refs):
            cp.start()

    def wait(in_refs, out_refs, sem_refs):
        for cp in copies(in_refs, out_refs, sem_refs):
            cp.wait()

    return SideJob(list(ins), list(outs), list(sems), start, wait)


def gather_job(packs, halved):
    n = len(packs)
    dma = pltpu.SemaphoreType.DMA

    def copies(in_refs, out_refs, sems):
        ssem, rsem = sems
        x, y, c, chips = _place()
        jme = 2 * x + y
        cps = []
        for i, (p_ref, f_ref) in enumerate(zip(in_refs, out_refs)):
            rows = p_ref.shape[0]
            mine = pl.ds(c * (rows // 2), rows // 2) if halved[i] else pl.ds(0, rows)
            for r, (cx, cy) in enumerate(chips):
                cps.append(_rcopy(p_ref.at[mine], f_ref.at[jme, mine], ssem.at[i, r], rsem.at[i, r], (cx, cy, c)))
        return cps

    return _job(packs, [SDS((4,) + p.shape, p.dtype) for p in packs], [dma((n, 3)), dma((n, 3))], copies)


def chip_exchange_job(parts):
    n = len(parts)
    dma = pltpu.SemaphoreType.DMA

    def copies(in_refs, out_refs, sems):
        ssem, rsem = sems
        x, y, c, chips = _place()
        cps = []
        for i, (p_ref, r_ref) in enumerate(zip(in_refs, out_refs)):
            for r, (cx, cy) in enumerate(chips):
                cps.append(_rcopy(p_ref.at[2 * cx + cy], r_ref.at[r], ssem.at[i, r], rsem.at[i, r], (cx, cy, c)))
        return cps

    return _job(parts, [SDS((3,) + p.shape[1:], p.dtype) for p in parts], [dma((n, 3)), dma((n, 3))], copies)


def run_job(job, name):
    ni, no = len(job.ins), len(job.outs)

    def body(*refs):
        job.start(refs[:ni], refs[ni:ni + no], refs[ni + no:])
        job.wait(refs[:ni], refs[ni:ni + no], refs[ni + no:])

    return pl.pallas_call(
        body, in_specs=[ANY] * ni, out_specs=[ANY] * no, out_shape=job.outs, scratch_shapes=job.sems, name=name,
        compiler_params=pltpu.CompilerParams(has_side_effects=True))(*job.ins)


def forward_to_sibling(fulls, name):
    n = len(fulls)

    def body(*refs):
        in_refs, out_refs, (ssem, rsem) = refs[:n], refs[n:2 * n], refs[2 * n:]
        x, y, c, chips = _place()
        cps = []
        for i in range(n):
            half = in_refs[i].shape[1] // 2
            for r, (cx, cy) in enumerate(chips):
                piece = (2 * cx + cy, pl.ds(c * half, half))
                cps.append(_rcopy(in_refs[i].at[piece], out_refs[i].at[piece], ssem.at[i, r], rsem.at[i, r], (x, y, 1 - c)))
        for cp in cps:
            cp.start()
        for cp in cps:
            cp.wait()

    dma = pltpu.SemaphoreType.DMA
    return pl.pallas_call(
        body, in_specs=[ANY] * n, out_specs=[ANY] * n, out_shape=[SDS(f.shape, f.dtype) for f in fulls],
        scratch_shapes=[dma((n, 3)), dma((n, 3))], input_output_aliases={i: i for i in range(n)}, name=name,
        compiler_params=pltpu.CompilerParams(has_side_effects=True))(*fulls)


def rs_sibling_exchange(gpack, name):
    _, rows, W = gpack.shape
    half = rows // 2

    def body(g_ref, r_ref, ssem, rsem):
        x, y, c, _ = _place()
        cps = [_rcopy(g_ref.at[j, pl.ds((1 - c) * half, half)], r_ref.at[j], ssem.at[j], rsem.at[j], (x, y, 1 - c))
               for j in range(4)]
        for cp in cps:
            cp.start()
        for cp in cps:
            cp.wait()

    dma = pltpu.SemaphoreType.DMA
    return pl.pallas_call(
        body, in_specs=[ANY], out_specs=ANY, out_shape=SDS((4, half, W), gpack.dtype),
        scratch_shapes=[dma((4,)), dma((4,))], name=name,
        compiler_params=pltpu.CompilerParams(has_side_effects=True))(gpack)


def _div_tile(n, limit):
    return next(t for t in range(limit - limit % 16, 0, -16) if n % t == 0)


def rs_add_sibling(gpack, recv, cidx, name, out_dtype=F32):
    _, rows, W = gpack.shape
    half = rows // 2
    tr = _div_tile(half, 1024)
    nb = half // tr

    def body(c_ref, g_ref, r_ref, o_ref):
        o_ref[...] = (g_ref[...].astype(F32) + r_ref[...].astype(F32)).astype(o_ref.dtype)

    gs = pltpu.PrefetchScalarGridSpec(
        num_scalar_prefetch=1, grid=(4, nb),
        in_specs=[BS((1, tr, W), lambda j, i, c: (j, c[0] * nb + i, 0)), BS((1, tr, W), lambda j, i, c: (j, i, 0))],
        out_specs=BS((1, tr, W), lambda j, i, c: (j, i, 0)))
    return pl.pallas_call(body, grid_spec=gs, out_shape=SDS((4, half, W), out_dtype), name=name,
                          compiler_params=_cp(("parallel", "parallel")))(cidx, gpack, recv)


def rs_add_chips(recv, part, place, name):
    _, half, W = recv.shape
    tr = _div_tile(half, 640)
    nb = half // tr

    def body(x_ref, y_ref, c_ref, r_ref, own_ref, o_ref):
        r0, r1, r2, own = (t.astype(F32) for t in (r_ref[0], r_ref[1], r_ref[2], own_ref[0]))
        o_ref[...] = ((r0 + r1) + r2) + own

    gs = pltpu.PrefetchScalarGridSpec(
        num_scalar_prefetch=3, grid=(nb,),
        in_specs=[BS((3, tr, W), lambda i, x, y, c: (0, i, 0)), BS((1, tr, W), lambda i, x, y, c: (2 * x[0] + y[0], i, 0))],
        out_specs=BS((tr, W), lambda i, x, y, c: (c[0] * nb + i, 0)))
    return pl.pallas_call(body, grid_spec=gs, out_shape=SDS((2 * half, W), F32), name=name,
                          compiler_params=_cp(("parallel",)))(*place, recv, part)


def rs_sibling_share(gsh, name):
    rows, W = gsh.shape
    half = rows // 2

    def body(g_ref, o_ref, ssem, rsem):
        x, y, c, _ = _place()
        mine = pl.ds(c * half, half)
        cp = _rcopy(g_ref.at[mine], o_ref.at[mine], ssem, rsem, (x, y, 1 - c))
        cp.start()
        cp.wait()

    dma = pltpu.SemaphoreType.DMA
    return pl.pallas_call(
        body, in_specs=[ANY], out_specs=ANY, out_shape=SDS(gsh.shape, gsh.dtype),
        scratch_shapes=[dma, dma], input_output_aliases={0: 0}, name=name,
        compiler_params=pltpu.CompilerParams(has_side_effects=True))(gsh)


def allreduce_small(pack):
    R = pack.shape[0]

    def body(p_ref, o_ref, all_ref, ssem, rsem):
        x, y, c, _ = _place()
        me = 4 * x + 2 * y + c
        all_ref[me] = p_ref[...]
        cps = []
        for m in range(1, 8):
            peer = (1 - x if m & 4 else x, 1 - y if m & 2 else y, 1 - c if m & 1 else c)
            cp = _rcopy(p_ref, all_ref.at[me], ssem.at[m - 1], rsem.at[m - 1], peer)
            cp.start()
            cps.append(cp)
        for cp in cps:
            cp.wait()
        acc = all_ref[0]
        for i in range(1, 8):
            acc = acc + all_ref[i]
        o_ref[...] = acc

    dma = pltpu.SemaphoreType.DMA
    vm = BS(memory_space=pltpu.VMEM)
    return pl.pallas_call(
        body, in_specs=[vm], out_specs=vm, out_shape=SDS(pack.shape, F32),
        scratch_shapes=[pltpu.VMEM((8, R, LANES), F32), dma((7,)), dma((7,))], name="allreduce_small",
        compiler_params=pltpu.CompilerParams(has_side_effects=True))(pack)


PACK_W = 1024
PACK_ALIGN = 32


def _layer_entries(l):
    if l % 2 == 0:
        mixer = [("gdn_w_in", l // 2, D_MODEL, GDN_IN // 4, True), ("gdn_w_out", l // 2, D_MODEL // 4, D_MODEL, False)]
    else:
        mixer = [("dswa_w_in", l // 2, D_MODEL, 3 * DSWA_W // 4, True), ("dswa_w_out", l // 2, DSWA_W // 4, D_MODEL, False)]
    return mixer + [("mlp_w1", l, D_MODEL, D_FF // 4, True), ("mlp_w2", l, D_FF // 4, D_MODEL, False)]


def _layer_offsets(l):
    offs = [int(o) for o in np.cumsum([0] + [r * c // PACK_W for (_, _, r, c, _) in _layer_entries(l)])]
    return offs, -(-offs[-1] // PACK_ALIGN) * PACK_ALIGN


def _pack_layer(l, shards, dtype):
    offs, total = _layer_offsets(l)
    parts = [shards[name][li].astype(dtype).reshape(-1, PACK_W) for (name, li, _, _, _) in _layer_entries(l)]
    parts.append(jnp.zeros((total - offs[-1], PACK_W), dtype))
    return jnp.concatenate(parts, axis=0)


def _unpack_layer(l, full, own, jme):
    offs, _ = _layer_offsets(l)
    mats = []
    for e, (_, _, r, c, by_col) in enumerate(_layer_entries(l)):
        mine = own[offs[e]:offs[e + 1]]
        sh = [jnp.where(jme == j, mine, full[j, offs[e]:offs[e + 1]]).reshape(r, c) for j in range(4)]
        mats.append(jnp.concatenate(sh, axis=1 if by_col else 0))
    return mats


def _pack_layer_grads(l, grads):
    offs, total = _layer_offsets(l)
    per_chip = []
    for j in range(4):
        parts = []
        for g, (_, _, r, c, by_col) in zip(grads, _layer_entries(l)):
            sh = g[:, c * j:c * (j + 1)] if by_col else g[r * j:r * (j + 1), :]
            parts.append(sh.astype(BF16).reshape(-1, PACK_W))
        parts.append(jnp.zeros((total - offs[-1], PACK_W), BF16))
        per_chip.append(jnp.concatenate(parts, axis=0))
    return jnp.stack(per_chip)


def _unpack_shard_grads(gshs):
    out = {}
    for l, gsh in enumerate(gshs):
        offs, _ = _layer_offsets(l)
        for e, (name, _, r, c, _) in enumerate(_layer_entries(l)):
            out.setdefault(name, []).append(gsh[offs[e]:offs[e + 1]].reshape(r, c))
    return {k: jnp.stack(v) for k, v in out.items()}


def _flat_pad(t, mult=8 * LANES):
    f = t.reshape(-1)
    return jnp.pad(f, (0, (-f.shape[0]) % mult))


def kernel(x, norm_mix, norm_mlp, norm_final, rel_bias, gdn_w_in, gdn_conv_w, gdn_a_log, gdn_dt_bias, gdn_norm_w, gdn_w_out, dswa_w_in, dswa_w_out, mlp_w1, mlp_w2, loss_target, m_norm_mix, m_norm_mlp, m_norm_final, m_rel_bias, m_gdn_w_in, m_gdn_conv_w, m_gdn_a_log, m_gdn_dt_bias, m_gdn_norm_w, m_gdn_w_out, m_dswa_w_in, m_dswa_w_out, m_mlp_w1, m_mlp_w2, v_norm_mix, v_norm_mlp, v_norm_final, v_rel_bias, v_gdn_w_in, v_gdn_conv_w, v_gdn_a_log, v_gdn_dt_bias, v_gdn_norm_w, v_gdn_w_out, v_dswa_w_in, v_dswa_w_out, v_mlp_w1, v_mlp_w2):
    xi, yi, ci = lax.axis_index("x"), lax.axis_index("y"), lax.axis_index("c")
    jme = 2 * xi + yi
    big = dict(gdn_w_in=gdn_w_in, gdn_w_out=gdn_w_out, dswa_w_in=dswa_w_in, dswa_w_out=dswa_w_out, mlp_w1=mlp_w1, mlp_w2=mlp_w2)
    n_gdn = gdn_w_in.shape[0]
    conv_cols = gdn_conv_w.shape[-1]

    packs = [_pack_layer(l, big, BF16) for l in range(DEPTH)]
    convp = jnp.pad(gdn_conv_w.reshape(n_gdn * GDN_CONV, conv_cols), ((0, 16 - n_gdn * GDN_CONV), (0, 0)))
    raw0, cfull = run_job(gather_job([packs[0], convp], [True, False]), "gather_l0")
    fulls = {0: forward_to_sibling([raw0], "forward_l0")[0]}
    cfull = jnp.where((jnp.arange(4) == jme)[:, None, None], convp[None], cfull)
    conv_all = jnp.transpose(cfull[:, :n_gdn * GDN_CONV], (1, 0, 2)).reshape(n_gdn, GDN_CONV, 4 * conv_cols)
    conv_all = jnp.pad(conv_all, ((0, 0), (0, 8 - GDN_CONV), (0, 0)))
    fwd_jobs = {0: [1, 2], 2: [3]}

    xs = x[0]
    saved = []
    for l in range(DEPTH):
        w_in, w_out, w1, w2 = _unpack_layer(l, fulls[l], packs[l], jme)
        gm, gp = norm_mix[l][None], norm_mlp[l][None]
        a = l // 2
        if l % 2 == 0:
            w_in = jnp.pad(w_in, ((0, 0), (0, GDN_INP - GDN_IN)))
            job = gather_job([packs[t] for t in fwd_jobs[l]], [True] * len(fwd_jobs[l]))
            x_mid, sv, raws = gdn_fwd(xs, gm, w_in, conv_all[a], gdn_a_log[a], gdn_dt_bias[a], gdn_norm_w[a][None], w_out,
                                      f"l{l}_gdn", job)
            for t, f in zip(fwd_jobs[l], forward_to_sibling(raws, f"forward_from_l{l}")):
                fulls[t] = f
        else:
            x_mid, sv = dswa_fwd(xs, gm, w_in, w_out, rel_bias, f"l{l}_att")
        x_out, sv2 = mlp_fwd(x_mid, gp, w1, w2, f"l{l}_mlp")
        saved.append((xs, x_mid, (w_in, w_out, w1, w2), sv, sv2))
        xs = x_out

    cidx = ci.astype(jnp.int32).reshape(1)
    place = [t.astype(jnp.int32).reshape(1) for t in (xi, yi, ci)]

    def chip_partial(l, grads4):
        gpack = _pack_layer_grads(l, grads4)
        return rs_add_sibling(gpack, rs_sibling_exchange(gpack, f"rs_sibling_l{l}"), cidx, f"rs_add_sibling_l{l}",
                              BF16)

    def finish(l, recv):
        return rs_sibling_share(rs_add_chips(recv, parts[l], place, f"rs_add_chips_l{l}"), f"rs_share_l{l}")

    loss_part, dx, dxb, d_final = loss_head(xs, norm_final[None], loss_target[0], "loss_head")
    d_mix, d_mlp = [None] * DEPTH, [None] * DEPTH
    d_conv, d_alog, d_dt, d_nw = [None] * n_gdn, [None] * n_gdn, [None] * n_gdn, [None] * n_gdn
    d_rel = jnp.zeros_like(rel_bias)
    parts, gshs = {}, [None] * DEPTH
    bwd_jobs = {2: [3], 0: [2, 1]}
    for l in reversed(range(DEPTH)):
        x_in, x_mid, (w_in, w_out, w1, w2), sv, sv2 = saved[l]
        gm, gp = norm_mix[l][None], norm_mlp[l][None]
        a = l // 2
        dx, dxb, d_mlp[l], dw1, dw2 = mlp_bwd(x_mid, gp, w1, w2, sv2, dx, dxb, f"l{l}_mlp")
        if l % 2 == 0:
            job = chip_exchange_job([parts[t] for t in bwd_jobs[l]])
            dx, dxb, d_mix[l], dw_all, d_conv[a], d_alog[a], d_dt[a], d_nw[a], dwo, recvs = gdn_bwd(
                x_in, gm, w_in, conv_all[a], gdn_norm_w[a][None], w_out, sv, dx, dxb, f"l{l}_gdn", job)
            for t, rv in zip(bwd_jobs[l], recvs):
                gshs[t] = finish(t, rv)
            dwi = dw_all[:, :GDN_IN]
        else:
            dx, dxb, d_mix[l], dwi, dwo, drel = dswa_bwd(x_in, gm, w_in, w_out, rel_bias, sv, dx, dxb, f"l{l}_att")
            d_rel = d_rel + drel
        parts[l] = chip_partial(l, [dwi, dwo, dw1, dw2])
    gshs[0] = finish(0, run_job(chip_exchange_job([parts[0]]), "rs_chip_exchange_l0")[0])
    gbig = _unpack_shard_grads(gshs)

    small = [jnp.concatenate(d_mix, axis=0), jnp.concatenate(d_mlp, axis=0), d_final, d_rel,
             jnp.stack(d_conv), jnp.stack(d_alog), jnp.stack(d_dt), jnp.concatenate(d_nw, axis=0)]
    flat = [_flat_pad(t) for t in small]
    sizes = [f.shape[0] for f in flat]
    red = allreduce_small(jnp.concatenate(flat).reshape(-1, LANES)).reshape(-1)
    offs = np.cumsum([0] + sizes)
    red = [red[offs[i]:offs[i] + small[i].size].reshape(small[i].shape) for i in range(len(small))]
    g_conv_all = red[4][:, :GDN_CONV].reshape(n_gdn, GDN_CONV, 1, 4 * conv_cols)
    g_conv = lax.dynamic_slice_in_dim(g_conv_all, jme * conv_cols, conv_cols, axis=3)
    g = dict(norm_mix=red[0], norm_mlp=red[1], norm_final=red[2].reshape(norm_final.shape), rel_bias=red[3],
             gdn_conv_w=g_conv, gdn_a_log=red[5], gdn_dt_bias=red[6], gdn_norm_w=red[7][:, :GDN_DK], **gbig)

    w = dict(norm_mix=norm_mix, norm_mlp=norm_mlp, norm_final=norm_final, rel_bias=rel_bias, gdn_conv_w=gdn_conv_w,
             gdn_a_log=gdn_a_log, gdn_dt_bias=gdn_dt_bias, gdn_norm_w=gdn_norm_w, **big)
    m = dict(norm_mix=m_norm_mix, norm_mlp=m_norm_mlp, norm_final=m_norm_final, rel_bias=m_rel_bias, gdn_w_in=m_gdn_w_in,
             gdn_conv_w=m_gdn_conv_w, gdn_a_log=m_gdn_a_log, gdn_dt_bias=m_gdn_dt_bias, gdn_norm_w=m_gdn_norm_w,
             gdn_w_out=m_gdn_w_out, dswa_w_in=m_dswa_w_in, dswa_w_out=m_dswa_w_out, mlp_w1=m_mlp_w1, mlp_w2=m_mlp_w2)
    v = dict(norm_mix=v_norm_mix, norm_mlp=v_norm_mlp, norm_final=v_norm_final, rel_bias=v_rel_bias, gdn_w_in=v_gdn_w_in,
             gdn_conv_w=v_gdn_conv_w, gdn_a_log=v_gdn_a_log, gdn_dt_bias=v_gdn_dt_bias, gdn_norm_w=v_gdn_norm_w,
             gdn_w_out=v_gdn_w_out, dswa_w_in=v_dswa_w_in, dswa_w_out=v_dswa_w_out, mlp_w1=v_mlp_w1, mlp_w2=v_mlp_w2)
    names = ["norm_mix", "norm_mlp", "norm_final", "rel_bias", "gdn_w_in", "gdn_conv_w", "gdn_a_log", "gdn_dt_bias",
             "gdn_norm_w", "gdn_w_out", "dswa_w_in", "dswa_w_out", "mlp_w1", "mlp_w2"]
    upd = {n: adamw(w[n], g[n], m[n], v[n], f"adamw_{n}") for n in names}
    loss = lax.psum(loss_part[0, 0], ("x", "y", "c"))
    return (loss, dx[None], *[g[n] for n in names], *[upd[n][0] for n in names], *[upd[n][1] for n in names],
            *[upd[n][2] for n in names])
```

```python
import math
from typing import Callable, NamedTuple

import numpy as np
import jax
import jax.numpy as jnp
from jax import lax
from jax.experimental import pallas as pl
from jax.experimental.pallas import tpu as pltpu

F32 = jnp.float32
BF16 = jnp.bfloat16
HI = lax.Precision.HIGHEST
BS = pl.BlockSpec
SDS = jax.ShapeDtypeStruct
MESH = pl.DeviceIdType.MESH
ANY = BS(memory_space=pl.ANY)

D_MODEL = 1024
D_FF = 4096
DEPTH = 4
RMS_EPS = 1e-6
NEG_INF = -1e30
LANES = 128
VMEM_LIMIT = 56 << 20

GDN_H = 8
GDN_DK = 128
GDN_CONV = 5
GDN_C = 64
GDN_GC = 8
GDN_HP_FWD = 8
GDN_HP_BWD = 4
GDN_QKV = 3 * GDN_H * GDN_DK
GDN_IN = GDN_QKV + GDN_H * GDN_DK + 4 * GDN_H
GDN_INP = 4224

DSWA_CFG = ((128, 1), (512, 4), (2048, 16))
DSWA_HG = 6
DSWA_E = 64
DSWA_HEADS = 18
DSWA_W = DSWA_HEADS * DSWA_E
DSWA_HALF = 64
REL_BUCKETS = 32
REL_MAX_DIST = 1024

ADAM_LR = 0.001
ADAM_B1 = 0.9
ADAM_B2 = 0.999
ADAM_EPS = 1e-08
ADAM_WD = 0.01
ADAM_STEP = 10


def _cp(sem=None):
    return pltpu.CompilerParams(dimension_semantics=sem, vmem_limit_bytes=VMEM_LIMIT)


def _dot(a, b, prec=None):
    return jnp.dot(a, b, precision=prec, preferred_element_type=F32)


def _dot_nt(a, b, prec=None):
    return lax.dot_general(a, b, (((1,), (1,)), ((), ())), precision=prec, preferred_element_type=F32)


def _dot_tn(a, b, prec=None):
    return lax.dot_general(a, b, (((0,), (0,)), ((), ())), precision=prec, preferred_element_type=F32)


def _bf(a):
    return a.astype(BF16)


def _sigmoid(x):
    return 1.0 / (1.0 + jnp.exp(-x))


def rms_fwd(x, g, name):
    S, Dm = x.shape
    tm = min(512, S)

    def body(x_ref, g_ref, o_ref, ot_ref):
        xv = x_ref[...]
        r = lax.rsqrt(jnp.mean(xv * xv, axis=-1, keepdims=True) + RMS_EPS)
        hb = (xv * r * g_ref[...]).astype(o_ref.dtype)
        o_ref[...] = hb
        ot_ref[...] = hb.T

    return pl.pallas_call(
        body, grid=(S // tm,),
        in_specs=[BS((tm, Dm), lambda i: (i, 0)), BS((1, Dm), lambda i: (0, 0))],
        out_specs=[BS((tm, Dm), lambda i: (i, 0)), BS((Dm, tm), lambda i: (0, i))],
        out_shape=[SDS((S, Dm), BF16), SDS((Dm, S), BF16)], name=name, compiler_params=_cp(("parallel",)))(x, g)


def rms_bwd(x, g, dh, dres, name):
    S, Dm = x.shape
    tm = min(512, S)

    def body(x_ref, g_ref, dh_ref, dres_ref, dx_ref, dxb_ref, dg_ref):
        i = pl.program_id(0)
        xv = x_ref[...]
        r = lax.rsqrt(jnp.mean(xv * xv, axis=-1, keepdims=True) + RMS_EPS)
        n = xv * r
        dhv = dh_ref[...]
        t = dhv * g_ref[...]
        dx = dres_ref[...] + r * (t - n * jnp.mean(n * t, axis=-1, keepdims=True))
        dx_ref[...] = dx
        dxb_ref[...] = dx.astype(BF16)
        part = jnp.sum(dhv * n, axis=0, keepdims=True)

        @pl.when(i == 0)
        def _():
            dg_ref[...] = part

        @pl.when(i > 0)
        def _():
            dg_ref[...] += part

    row = BS((tm, Dm), lambda i: (i, 0))
    vec = BS((1, Dm), lambda i: (0, 0))
    return pl.pallas_call(
        body, grid=(S // tm,), in_specs=[row, vec, row, row], out_specs=[row, row, vec],
        out_shape=[SDS((S, Dm), F32), SDS((S, Dm), BF16), SDS((1, Dm), F32)],
        name=name, compiler_params=_cp(("arbitrary",)))(x, g, dh, dres)


def loss_head(x, g, tgt, name):
    S, Dm = x.shape
    tm = min(512, S)

    def body(x_ref, g_ref, t_ref, loss_ref, dx_ref, dxb_ref, dg_ref):
        i = pl.program_id(0)
        xv = x_ref[...]
        gv = g_ref[...]
        r = lax.rsqrt(jnp.mean(xv * xv, axis=-1, keepdims=True) + RMS_EPS)
        n = xv * r
        err = n * gv - t_ref[...]
        lpart = 0.5 * jnp.sum(jnp.mean(err * err, axis=-1, keepdims=True), axis=0, keepdims=True)
        dout = err * (1.0 / Dm)
        t = dout * gv
        dx = r * (t - n * jnp.mean(n * t, axis=-1, keepdims=True))
        dx_ref[...] = dx
        dxb_ref[...] = dx.astype(BF16)
        part = jnp.sum(dout * n, axis=0, keepdims=True)

        @pl.when(i == 0)
        def _():
            dg_ref[...] = part
            loss_ref[...] = lpart

        @pl.when(i > 0)
        def _():
            dg_ref[...] += part
            loss_ref[...] += lpart

    row = BS((tm, Dm), lambda i: (i, 0))
    vec = BS((1, Dm), lambda i: (0, 0))
    one = BS((1, 1), lambda i: (0, 0))
    return pl.pallas_call(
        body, grid=(S // tm,), in_specs=[row, vec, row], out_specs=[one, row, row, vec],
        out_shape=[SDS((1, 1), F32), SDS((S, Dm), F32), SDS((S, Dm), BF16), SDS((1, Dm), F32)],
        name=name, compiler_params=_cp(("arbitrary",)))(x, g, tgt)


MM_TK = 2048


def mm(a, b, *, name, ta=False, tb=False, tm=1024, tn=512, tk=None, out_dtype=F32, pre_a=None, epi=None,
       extras=()):
    M, K = (a.shape[1], a.shape[0]) if ta else a.shape
    N = b.shape[0] if tb else b.shape[1]
    if ta:
        tm = min(tm, 512)
        tk = K if tk is None else tk
    elif tk is None and K > MM_TK:
        tk, tn = K, (min(tn, 256) if N % 256 == 0 else tn)
    tm, tn = min(tm, M), min(tn, N)
    tk = K if tk is None else min(tk, K)
    assert M % tm == 0 and N % tn == 0 and K % tk == 0, (name, M, N, K, tm, tn, tk)
    nk = K // tk
    ne = len(extras)
    a_spec = BS((tk, tm), lambda i, j, k: (k, i)) if ta else BS((tm, tk), lambda i, j, k: (i, k))
    b_spec = BS((tn, tk), lambda i, j, k: (j, k)) if tb else BS((tk, tn), lambda i, j, k: (k, j))
    o_spec = BS((tm, tn), lambda i, j, k: (i, j))
    dims = (((0 if ta else 1,), (1 if tb else 0,)), ((), ()))

    def body(a_ref, b_ref, *rest):
        e_refs, o_ref = rest[:ne], rest[ne]
        av = a_ref[...]
        if pre_a is not None:
            av = pre_a(av)
        p = lax.dot_general(_bf(av), _bf(b_ref[...]), dims, preferred_element_type=F32)

        def finish(acc):
            res = epi(acc, *[e[...] for e in e_refs]) if epi is not None else acc
            o_ref[...] = res.astype(o_ref.dtype)

        if nk == 1:
            finish(p)
        else:
            acc_ref = rest[ne + 1]
            k = pl.program_id(2)

            @pl.when(k == 0)
            def _():
                acc_ref[...] = p

            @pl.when(k > 0)
            def _():
                acc_ref[...] += p

            @pl.when(k == nk - 1)
            def _():
                finish(acc_ref[...])

    return pl.pallas_call(
        body, grid=(M // tm, N // tn, nk), in_specs=[a_spec, b_spec] + [o_spec] * ne, out_specs=o_spec,
        out_shape=SDS((M, N), out_dtype),
        scratch_shapes=[pltpu.VMEM((tm, tn), F32)] if nk > 1 else [],
        name=name, compiler_params=_cp(("parallel", "parallel", "arbitrary")))(a, b, *extras)


def _relu(acc):
    return jnp.maximum(acc, 0.0)


def _add(acc, res):
    return acc + res


def _sq(av):
    return av * av


def _times_2r(acc, r):
    return acc * (2.0 * r.astype(F32))


def mlp_fwd(x, g, w1, w2, tag):
    h, ht = rms_fwd(x, g, f"{tag}_rms")
    r = mm(h, w1, name=f"{tag}_up", tn=1024, out_dtype=BF16, epi=_relu)
    xn = mm(r, w2, name=f"{tag}_down", pre_a=_sq, epi=_add, extras=(x,))
    return xn, (ht, r)


def mlp_bwd(x, g, w1, w2, saved, dx, dxb, tag):
    ht, r = saved
    da = mm(dxb, w2, name=f"{tag}_dact", tb=True, tn=1024, out_dtype=BF16, epi=_times_2r, extras=(r,))
    dw2 = mm(r, dxb, name=f"{tag}_dw2", ta=True, pre_a=_sq)
    dw1 = mm(ht, da, name=f"{tag}_dw1")
    dh = mm(da, w1, name=f"{tag}_dh", tb=True)
    dx, dxb, dg = rms_bwd(x, g, dh, dx, f"{tag}_rmsb")
    return dx, dxb, dg, dw1, dw2


def _conv_taps(x, S):
    t = lax.broadcasted_iota(jnp.int32, x.shape, 0)
    taps = []
    for j in range(GDN_CONV):
        sh = j - GDN_CONV // 2
        xs = x if sh == 0 else pltpu.roll(x, (-sh) % S, 0)
        taps.append(jnp.where((t + sh >= 0) & (t + sh < S), xs, 0.0))
    return taps


def _qkv_scale(c):
    is_norm = c < 2 * GDN_H
    scale = jnp.where(c < GDN_H, GDN_DK ** -0.5, 1.0)
    return is_norm, scale


def gdn_pre_fwd(proj, convw, name):
    S = proj.shape[0]

    def body(p_ref, w_ref, o_ref):
        c = pl.program_id(0)
        x = p_ref[...]
        w = w_ref[...]
        y = jnp.zeros_like(x)
        for j, xs in enumerate(_conv_taps(x, S)):
            y = y + w[j:j + 1, :] * xs
        t = y * _sigmoid(y)
        is_norm, scale = _qkv_scale(c)
        r = lax.rsqrt(jnp.sum(t * t, axis=-1, keepdims=True) + 1e-6)
        o_ref[...] = jnp.where(is_norm, t * r * scale, t)

    return pl.pallas_call(
        body, grid=(GDN_QKV // LANES,),
        in_specs=[BS((S, LANES), lambda c: (0, c)), BS((8, LANES), lambda c: (0, c))],
        out_specs=BS((S, LANES), lambda c: (0, c)),
        out_shape=SDS((S, GDN_QKV), F32), name=name, compiler_params=_cp(("parallel",)))(proj, convw)


def gdn_pre_bwd(proj, convw, dqkv, name):
    S = proj.shape[0]

    def body(p_ref, w_ref, d_ref, dp_ref, dw_ref):
        c = pl.program_id(0)
        x = p_ref[...]
        w = w_ref[...]
        taps = _conv_taps(x, S)
        y = jnp.zeros_like(x)
        for j, xs in enumerate(taps):
            y = y + w[j:j + 1, :] * xs
        sg = _sigmoid(y)
        t = y * sg
        is_norm, scale = _qkv_scale(c)
        dout = d_ref[0, 0] + d_ref[1, 0]
        r = lax.rsqrt(jnp.sum(t * t, axis=-1, keepdims=True) + 1e-6)
        n = t * r
        dn = dout * scale
        dt_norm = r * (dn - n * jnp.sum(dn * n, axis=-1, keepdims=True))
        dt = jnp.where(is_norm, dt_norm, dout)
        dy = dt * (sg * (1.0 + y * (1.0 - sg)))
        row = lax.broadcasted_iota(jnp.int32, (8, LANES), 0)
        dw = jnp.zeros((8, LANES), F32)
        for j, xs in enumerate(taps):
            dw = dw + jnp.where(row == j, jnp.sum(dy * xs, axis=0, keepdims=True), 0.0)
        dw_ref[...] = dw
        tt = lax.broadcasted_iota(jnp.int32, x.shape, 0)
        dx = jnp.zeros_like(x)
        for j in range(GDN_CONV):
            sh = j - GDN_CONV // 2
            ds = dy if sh == 0 else pltpu.roll(dy, sh % S, 0)
            dx = dx + w[j:j + 1, :] * jnp.where((tt - sh >= 0) & (tt - sh < S), ds, 0.0)
        dp_ref[...] = dx.astype(BF16)

    return pl.pallas_call(
        body, grid=(GDN_QKV // LANES,),
        in_specs=[BS((S, LANES), lambda c: (0, c)), BS((8, LANES), lambda c: (0, c)),
                  BS((2, 1, S, LANES), lambda c: (0, c // GDN_H, 0, c % GDN_H))],
        out_specs=[BS((S, LANES), lambda c: (0, c)), BS((8, LANES), lambda c: (0, c))],
        out_shape=[SDS((S, GDN_QKV), BF16), SDS((8, GDN_QKV), F32)],
        name=name, compiler_params=_cp(("parallel",)))(proj, convw, dqkv)


def _chunk_sum_matrix(n, upper):
    i = lax.broadcasted_iota(jnp.int32, (n, n), 0)
    j = lax.broadcasted_iota(jnp.int32, (n, n), 1)
    same = (i // GDN_C) == (j // GDN_C)
    tri = (i <= j) if upper else (i >= j)
    return jnp.where(same & tri, 1.0, 0.0).astype(F32)


def _gate_lanes(shape):
    lane = lax.broadcasted_iota(jnp.int32, shape, 1)
    return lane < GDN_H, (lane >= GDN_H) & (lane < 2 * GDN_H), (lane >= 2 * GDN_H) & (lane < 4 * GDN_H)


def gdn_gate_fwd(proj, prm, name):
    S = proj.shape[0]
    tm = min(512, S)
    ct = GDN_INP // LANES - 1

    def body(p_ref, prm_ref, o_ref):
        ab = p_ref[...]
        a_log = prm_ref[0:1, :]
        dtb = prm_ref[1:2, :]
        z = ab + dtb
        sp = jnp.maximum(z, 0.0) + jnp.log(1.0 + jnp.exp(-jnp.abs(z)))
        g = -jnp.exp(a_log) * sp
        is_f, is_b, is_beta = _gate_lanes(ab.shape)
        gf = _dot(_chunk_sum_matrix(tm, False), jnp.where(is_f, g, 0.0), HI)
        gbk = _dot(_chunk_sum_matrix(tm, True), jnp.where(is_b, g, 0.0), HI)
        o_ref[...] = gf + gbk + jnp.where(is_beta, _sigmoid(ab), 0.0)

    return pl.pallas_call(
        body, grid=(S // tm,),
        in_specs=[BS((tm, LANES), lambda i: (i, ct)), BS((8, LANES), lambda i: (0, 0))],
        out_specs=BS((tm, LANES), lambda i: (i, 0)),
        out_shape=SDS((S, LANES), F32), name=name, compiler_params=_cp(("parallel",)))(proj, prm)


def gdn_gate_bwd(proj, prm, dgb, name):
    S = proj.shape[0]
    tm = min(512, S)
    ct = GDN_INP // LANES - 1

    def body(p_ref, prm_ref, d_ref, dab_ref, dprm_ref):
        i = pl.program_id(0)
        ab = p_ref[...]
        a_log = prm_ref[0:1, :]
        dtb = prm_ref[1:2, :]
        z = ab + dtb
        sp = jnp.maximum(z, 0.0) + jnp.log(1.0 + jnp.exp(-jnp.abs(z)))
        ea = jnp.exp(a_log)
        g = -ea * sp
        is_f, is_b, is_beta = _gate_lanes(ab.shape)
        d = d_ref[...]
        dg = (_dot_tn(_chunk_sum_matrix(tm, False), jnp.where(is_f, d, 0.0), HI)
              + _dot_tn(_chunk_sum_matrix(tm, True), jnp.where(is_b, d, 0.0), HI))
        da = dg * (-ea) * _sigmoid(z)
        beta = _sigmoid(ab)
        dab_ref[...] = jnp.where(is_beta, d * beta * (1.0 - beta), da).astype(BF16)
        row = lax.broadcasted_iota(jnp.int32, (8, LANES), 0)
        part = (jnp.where(row == 0, jnp.sum(dg * g, axis=0, keepdims=True), 0.0)
                + jnp.where(row == 1, jnp.sum(da, axis=0, keepdims=True), 0.0))

        @pl.when(i == 0)
        def _():
            dprm_ref[...] = part

        @pl.when(i > 0)
        def _():
            dprm_ref[...] += part

    return pl.pallas_call(
        body, grid=(S // tm,),
        in_specs=[BS((tm, LANES), lambda i: (i, ct)), BS((8, LANES), lambda i: (0, 0)), BS((tm, LANES), lambda i: (i, 0))],
        out_specs=[BS((tm, LANES), lambda i: (i, 0)), BS((8, LANES), lambda i: (0, 0))],
        out_shape=[SDS((S, LANES), BF16), SDS((8, LANES), F32)],
        name=name, compiler_params=_cp(("arbitrary",)))(proj, prm, dgb)


def _tri_masks(d):
    i = lax.broadcasted_iota(jnp.int32, (GDN_C, GDN_C), 0)
    j = lax.broadcasted_iota(jnp.int32, (GDN_C, GDN_C), 1)
    s = (i - j) * (1 - 2 * d)
    return s >= 0, s > 0


def _split(a):
    hi = _bf(a)
    return hi, _bf(a - hi.astype(F32))


def _dot3(a, b):
    return _dot(a[0], b[0]) + (_dot(a[0], b[1]) + _dot(a[1], b[0]))


def _inv_unit_tri_many(mats):
    i = lax.broadcasted_iota(jnp.int32, mats[0].shape, 0)
    j = lax.broadcasted_iota(jnp.int32, mats[0].shape, 1)
    eye = jnp.where(i == j, 1.0, 0.0)
    ms = [-a for a in mats]
    ps = [eye + m for m in ms]
    for _ in range(int(math.log2(GDN_C)) - 1):
        sp = [_split(m) for m in ms]
        ms = [_dot3(s, s) for s in sp]
        sp = [_split(m) for m in ms]
        pp = [_split(p) for p in ps]
        ps = [p + _dot3(a, b) for p, a, b in zip(ps, pp, sp)]
    return ps


def _lane_col(x, lane_idx):
    lane = lax.broadcasted_iota(jnp.int32, x.shape, 1)
    return jnp.sum(jnp.where(lane == lane_idx, x, 0.0), axis=1, keepdims=True)


def _chunk_gates(gb_ref, grow_ref, hh, ci, d, head):
    gbv = gb_ref[ci * GDN_C:(ci + 1) * GDN_C, :]
    gcol = _lane_col(gbv, d * GDN_H + head)
    bcol = _lane_col(gbv, 2 * GDN_H + d * GDN_H + head)
    glast = jnp.where(d == 0, gcol[GDN_C - 1:GDN_C, :], gcol[0:1, :])
    return gcol, bcol, grow_ref[hh, ci:ci + 1, :], glast


def _chunk_base(q, k, gcol, grow, bcol, glast, d):
    incl, strict = _tri_masks(d)
    decay = jnp.where(incl, jnp.exp(jnp.where(incl, gcol - grow, 0.0)), 0.0)
    kb = k * bcol
    kk = _dot_nt(_bf(kb), _bf(k))
    qk = _dot_nt(_bf(q), _bf(k))
    eg = jnp.exp(gcol)
    ek = jnp.exp(glast - gcol)
    return dict(incl=incl, strict=strict, decay=decay, kb=kb, kk=kk, qk=qk, eg=eg, ek=ek, q_dec=q * eg, k_dec=k * ek,
                bcol=bcol, glast=glast)


def _block_terms(q_ref, k_ref, v_ref, gb_ref, grow_ref, d, h, hp):
    keys = [(hh, ci) for hh in range(hp) for ci in range(GDN_GC)]
    ts = []
    for hh, ci in keys:
        rows = slice(ci * GDN_C, (ci + 1) * GDN_C)
        cols = slice(hh * GDN_DK, (hh + 1) * GDN_DK)
        gcol, bcol, grow_v, glast = _chunk_gates(gb_ref, grow_ref, hh, ci, d, h * hp + hh)
        t = _chunk_base(q_ref[rows, cols], k_ref[rows, cols], gcol, grow_v, bcol, glast, d)
        t["v"] = v_ref[rows, cols]
        ts.append(t)
    tinvs = _inv_unit_tri_many([jnp.where(t["strict"], t["kk"] * t["decay"], 0.0) for t in ts])
    sp = [_split(x) for x in tinvs]
    us = [_dot3(s, _split(t["v"] * t["bcol"])) for s, t in zip(sp, ts)]
    ws = [_dot3(s, _split(t["kb"] * t["eg"])) for s, t in zip(sp, ts)]
    for t, tinv, u, w in zip(ts, tinvs, us, ws):
        t.update(tinv=tinv, u=u, w=w)
    return keys, ts


def _gdn_specs(S, nblk, order, hp):
    R = GDN_GC * GDN_C
    wd = hp * GDN_DK
    hb = GDN_H // hp

    def qkv_spec(part):
        return BS((R, wd), lambda d, h, n: (order(d, n), part * hb + h))

    gb_spec = BS((R, LANES), lambda d, h, n: (order(d, n), 0))
    grow_spec = BS((hp, GDN_GC, GDN_C), lambda d, h, n: (d * hb + h, order(d, n), 0))
    st_spec = BS((1, hp, GDN_GC, GDN_DK, GDN_DK), lambda d, h, n: (d, h, order(d, n), 0, 0))
    return qkv_spec, gb_spec, grow_spec, st_spec


def _lane_row(x):
    return jnp.broadcast_to(x, (1, LANES))


def _side_parts(side):
    if side is None:
        return [], [], [], [], []
    return [ANY] * len(side.ins), [ANY] * len(side.outs), list(side.outs), list(side.sems), list(side.ins)


def _side_run(side, refs, n_in, n_out, n_scr, first, last):
    if side is None:
        return
    ns, no, nm = len(side.ins), len(side.outs), len(side.sems)
    s_in = refs[n_in:n_in + ns]
    s_out = refs[n_in + ns + n_out:n_in + ns + n_out + no]
    s_sem = refs[len(refs) - nm:]

    @pl.when(first)
    def _():
        side.start(s_in, s_out, s_sem)

    @pl.when(last)
    def _():
        side.wait(s_in, s_out, s_sem)


def gdn_scan_fwd(qkv, gb, grow, name, side=None):
    S = qkv.shape[0]
    R = GDN_GC * GDN_C
    nblk = S // R
    nc = S // GDN_C
    hp = GDN_HP_FWD
    wd = hp * GDN_DK
    heads = range(hp)

    def order(d, n):
        return n + d * (nblk - 1 - 2 * n)

    qkv_spec, gb_spec, grow_spec, st_spec = _gdn_specs(S, nblk, order, hp)

    s_in, s_out, s_shape, s_scr_shapes, s_ops = _side_parts(side)
    hb = GDN_H // hp

    def body(*refs):
        q_ref, k_ref, v_ref, gb_ref, grow_ref = refs[:5]
        o_ref, st_ref = refs[5 + len(s_in):7 + len(s_in)]
        s_scr, u_scr, w_scr, qd_scr, kd_scr, in_scr, egl_scr = refs[7 + len(s_in) + len(s_out):14 + len(s_in) + len(s_out)]
        d = pl.program_id(0)
        h = pl.program_id(1)
        n = pl.program_id(2)
        _side_run(side, refs, 5, 2, 7, (d == 0) & (h == 0) & (n == 0), (d == 1) & (h == hb - 1) & (n == nblk - 1))

        @pl.when(n == 0)
        def _():
            s_scr[...] = jnp.zeros_like(s_scr)

        keys, ts = _block_terms(q_ref, k_ref, v_ref, gb_ref, grow_ref, d, h, hp)
        for (hh, ci), t in zip(keys, ts):
            u_scr[hh, ci] = t["u"]
            w_scr[hh, ci] = _bf(t["w"])
            qd_scr[hh, ci] = _bf(t["q_dec"])
            kd_scr[hh, ci] = _bf(t["k_dec"])
            in_scr[hh, ci] = _bf(jnp.where(t["incl"], t["qk"] * t["decay"], 0.0))
            egl_scr[hh, ci] = _lane_row(jnp.exp(t["glast"]))

        def chunk(cc, carry):
            ci = cc + d * (GDN_GC - 1 - 2 * cc)
            rows = pl.ds(pl.multiple_of(ci * GDN_C, GDN_C), GDN_C)
            sts = [s_scr[hh] for hh in heads]
            for hh in heads:
                st_ref[0, hh, ci] = sts[hh]
            sbs = [_bf(st) for st in sts]
            vns = [_bf(u_scr[hh, ci] - _dot(w_scr[hh, ci], sbs[hh])) for hh in heads]
            for hh in heads:
                s_scr[hh] = sts[hh] * egl_scr[hh, ci] + _dot_tn(kd_scr[hh, ci], vns[hh])
            for hh in heads:
                o_ref[0, rows, hh * GDN_DK:(hh + 1) * GDN_DK] = _dot(qd_scr[hh, ci], sbs[hh]) + _dot(in_scr[hh, ci], vns[hh])
            return carry

        lax.fori_loop(0, GDN_GC, chunk, 0)

    blk = (hp, GDN_GC, GDN_C, GDN_DK)
    return pl.pallas_call(
        body, grid=(2, GDN_H // hp, nblk),
        in_specs=[qkv_spec(0), qkv_spec(1), qkv_spec(2), gb_spec, grow_spec] + s_in,
        out_specs=[BS((1, R, wd), lambda d, h, n: (d, order(d, n), h)), st_spec] + s_out,
        out_shape=[SDS((2, S, GDN_H * GDN_DK), F32), SDS((2, GDN_H, nc, GDN_DK, GDN_DK), F32)] + s_shape,
        scratch_shapes=[pltpu.VMEM((hp, GDN_DK, GDN_DK), F32), pltpu.VMEM(blk, F32), pltpu.VMEM(blk, BF16),
                        pltpu.VMEM(blk, BF16), pltpu.VMEM(blk, BF16), pltpu.VMEM((hp, GDN_GC, GDN_C, GDN_C), BF16),
                        pltpu.VMEM((hp, GDN_GC, 1, LANES), F32)] + s_scr_shapes,
        name=name, compiler_params=_cp(("arbitrary", "arbitrary", "arbitrary")))(qkv, qkv, qkv, gb, grow, *s_ops)


def gdn_scan_bwd(qkv, gb, grow, states, do, name, side=None):
    S = qkv.shape[0]
    R = GDN_GC * GDN_C
    nblk = S // R
    hp = GDN_HP_BWD
    wd = hp * GDN_DK
    heads = range(hp)

    def order(d, n):
        return (nblk - 1 - n) - d * (nblk - 1 - 2 * n)

    qkv_spec, gb_spec, grow_spec, st_spec = _gdn_specs(S, nblk, order, hp)

    s_in, s_out, s_shape, s_scr_shapes, s_ops = _side_parts(side)
    hb = GDN_H // hp

    def body(*refs):
        q_ref, k_ref, v_ref, gb_ref, grow_ref, st_ref, do_ref = refs[:7]
        dqkv_ref, dgate_ref = refs[7 + len(s_in):9 + len(s_in)]
        (ds_scr, w_scr, kd_scr, dv1_scr, qtdo_scr, egl_scr, dsin_scr, dvn_scr,
         sdot_scr) = refs[9 + len(s_in) + len(s_out):18 + len(s_in) + len(s_out)]
        d = pl.program_id(0)
        h = pl.program_id(1)
        n = pl.program_id(2)
        _side_run(side, refs, 7, 2, 9, (d == 0) & (h == 0) & (n == 0), (d == 1) & (h == hb - 1) & (n == nblk - 1))

        @pl.when(n == 0)
        def _():
            ds_scr[...] = jnp.zeros_like(ds_scr)

        keys, ts = _block_terms(q_ref, k_ref, v_ref, gb_ref, grow_ref, d, h, hp)
        for (hh, ci), t in zip(keys, ts):
            rows = slice(ci * GDN_C, (ci + 1) * GDN_C)
            t["wb"] = _bf(t["w"])
            t["dob"] = _bf(do_ref[rows, hh * GDN_DK:(hh + 1) * GDN_DK])
            t["sb"] = _bf(st_ref[0, hh, ci])
        for (hh, ci), t in zip(keys, ts):
            t["vnb"] = _bf(t["u"] - _dot(t["wb"], t["sb"]))
            w_scr[hh, ci] = t["wb"]
            kd_scr[hh, ci] = _bf(t["k_dec"])
            dv1_scr[hh, ci] = _dot_tn(_bf(jnp.where(t["incl"], t["qk"] * t["decay"], 0.0)), t["dob"])
            qtdo_scr[hh, ci] = _dot_tn(_bf(t["q_dec"]), t["dob"])
            egl_scr[hh, ci] = _lane_row(jnp.exp(t["glast"]))

        def chunk(cc, carry):
            ci = (GDN_GC - 1 - cc) - d * (GDN_GC - 1 - 2 * cc)
            dsns = [ds_scr[hh] for hh in heads]
            dsbs = [_bf(x) for x in dsns]
            dvns = [dv1_scr[hh, ci] + _dot(kd_scr[hh, ci], dsbs[hh]) for hh in heads]
            for hh in heads:
                ds_scr[hh] = qtdo_scr[hh, ci] + egl_scr[hh, ci] * dsns[hh] - _dot_tn(w_scr[hh, ci], _bf(dvns[hh]))
            for hh in heads:
                dsin_scr[hh, ci] = dsbs[hh]
                dvn_scr[hh, ci] = dvns[hh]
                sd = jnp.sum(jnp.sum(st_ref[0, hh, ci] * dsns[hh], axis=1, keepdims=True), axis=0, keepdims=True)
                sdot_scr[hh, ci] = _lane_row(sd)
            return carry

        lax.fori_loop(0, GDN_GC, chunk, 0)

        for (hh, ci), t in zip(keys, ts):
            t["d_vnew"] = dvn_scr[hh, ci]
            t["dvb"] = _bf(t["d_vnew"])
            t["dsb"] = dsin_scr[hh, ci]
        for t in ts:
            t["d_intra"] = jnp.where(t["incl"], _dot_nt(t["dob"], t["vnb"]), 0.0)
            t["d_qdec"] = _dot_nt(t["dob"], t["sb"])
            t["d_kdec"] = _dot_nt(t["vnb"], t["dsb"])
            t["dw"] = -_dot_nt(t["dvb"], t["sb"])
        for t in ts:
            tts = _split(t["tinv"].T)
            t["d_ru"] = _dot3(tts, _split(t["d_vnew"]))
            t["d_rw"] = _dot3(tts, _split(t["dw"]))
        for t in ts:
            t["da"] = -jnp.where(t["strict"], _dot_nt(_bf(t["d_ru"]), _bf(t["u"])) + _dot_nt(_bf(t["d_rw"]), t["wb"]), 0.0)
        for (hh, ci), t in zip(keys, ts):
            rows = slice(ci * GDN_C, (ci + 1) * GDN_C)
            cols = slice(hh * GDN_DK, (hh + 1) * GDN_DK)
            q, k, v = q_ref[rows, cols], k_ref[rows, cols], t["v"]
            decay, kb, eg, ek, bcol = t["decay"], t["kb"], t["eg"], t["ek"], t["bcol"]
            d_ru, d_rw, da, d_intra, d_qdec, d_kdec = t["d_ru"], t["d_rw"], t["da"], t["d_intra"], t["d_qdec"], t["d_kdec"]
            kbf, qbf = _bf(k), _bf(q)
            dgl = egl_scr[hh, ci][:, 0:1] * sdot_scr[hh, ci][:, 0:1]
            dv = d_ru * bcol
            dbeta = jnp.sum(d_ru * v, axis=1, keepdims=True)
            dkb = d_rw * eg
            dg = jnp.sum(d_rw * kb, axis=1, keepdims=True) * eg
            dkk = _bf(da * decay)
            dqk = _bf(d_intra * decay)
            dkb = dkb + _dot(dkk, kbf)
            dk = _dot_tn(dkk, _bf(kb)) + _dot_tn(dqk, qbf)
            dq = _dot(dqk, kbf) + d_qdec * eg
            dd = (da * t["kk"] + d_intra * t["qk"]) * decay
            dg = dg + jnp.sum(dd, axis=1, keepdims=True) - jnp.sum(dd.T, axis=1, keepdims=True)
            dg = dg + jnp.sum(d_qdec * t["q_dec"], axis=1, keepdims=True)
            dk = dk + d_kdec * ek
            ee = jnp.sum(d_kdec * t["k_dec"], axis=1, keepdims=True)
            dg = dg - ee
            dgl = dgl + jnp.sum(ee, axis=0, keepdims=True)
            dk = dk + dkb * bcol
            dbeta = dbeta + jnp.sum(dkb * k, axis=1, keepdims=True)
            ridx = lax.broadcasted_iota(jnp.int32, (GDN_C, 1), 0)
            dg = dg + jnp.where(ridx == (GDN_C - 1) * (1 - d), dgl, 0.0)
            dqkv_ref[0, 0, rows, cols] = dq
            dqkv_ref[0, 1, rows, cols] = dk
            dqkv_ref[0, 2, rows, cols] = dv
            lane2 = lax.broadcasted_iota(jnp.int32, (GDN_C, 2), 1)
            dgate_ref[0, hh, rows, :] = jnp.where(lane2 == 0, dg, dbeta)

    blk = (hp, GDN_GC, GDN_C, GDN_DK)
    sq = (hp, GDN_GC, GDN_DK, GDN_DK)
    row = (hp, GDN_GC, 1, LANES)
    return pl.pallas_call(
        body, grid=(2, GDN_H // hp, nblk),
        in_specs=[qkv_spec(0), qkv_spec(1), qkv_spec(2), gb_spec, grow_spec, st_spec,
                  BS((R, wd), lambda d, h, n: (order(d, n), h))] + s_in,
        out_specs=[BS((1, 3, R, wd), lambda d, h, n: (d, 0, order(d, n), h)),
                   BS((1, hp, R, 2), lambda d, h, n: (d, h, order(d, n), 0))] + s_out,
        out_shape=[SDS((2, 3, S, GDN_H * GDN_DK), F32), SDS((2, GDN_H, S, 2), F32)] + s_shape,
        scratch_shapes=[pltpu.VMEM((hp, GDN_DK, GDN_DK), F32), pltpu.VMEM(blk, BF16), pltpu.VMEM(blk, BF16),
                        pltpu.VMEM(blk, F32), pltpu.VMEM(sq, F32), pltpu.VMEM(row, F32), pltpu.VMEM(sq, BF16),
                        pltpu.VMEM(blk, F32), pltpu.VMEM(row, F32)] + s_scr_shapes,
        name=name, compiler_params=_cp(("arbitrary", "arbitrary", "arbitrary")))(qkv, qkv, qkv, gb, grow, states, do, *s_ops)


def gdn_post_fwd(o2, proj, nw, name):
    S = proj.shape[0]
    tm = min(512, S)
    zoff = GDN_QKV // LANES

    def body(o_ref, z_ref, nw_ref, y_ref):
        o = o_ref[0] + o_ref[1]
        z = z_ref[...]
        r = lax.rsqrt(jnp.mean(o * o, axis=-1, keepdims=True) + RMS_EPS)
        y_ref[...] = (o * r * nw_ref[...] * (z * _sigmoid(z))).astype(BF16)

    return pl.pallas_call(
        body, grid=(S // tm, GDN_H),
        in_specs=[BS((2, tm, LANES), lambda i, h: (0, i, h)), BS((tm, LANES), lambda i, h: (i, zoff + h)),
                  BS((1, LANES), lambda i, h: (0, 0))],
        out_specs=BS((tm, LANES), lambda i, h: (i, h)),
        out_shape=SDS((S, GDN_H * GDN_DK), BF16), name=name, compiler_params=_cp(("parallel", "parallel")))(o2, proj, nw)


def gdn_post_bwd(o2, proj, nw, dy, name):
    S = proj.shape[0]
    tm = min(512, S)
    zoff = GDN_QKV // LANES

    def body(o_ref, z_ref, nw_ref, dy_ref, do_ref, dz_ref, dnw_ref):
        first = (pl.program_id(0) == 0) & (pl.program_id(1) == 0)
        o = o_ref[0] + o_ref[1]
        z = z_ref[...]
        nwv = nw_ref[...]
        dyv = dy_ref[...]
        r = lax.rsqrt(jnp.mean(o * o, axis=-1, keepdims=True) + RMS_EPS)
        n = o * r
        sg = _sigmoid(z)
        sz = z * sg
        dz_ref[...] = (dyv * n * nwv * (sg * (1.0 + z * (1.0 - sg)))).astype(BF16)
        dn = dyv * nwv * sz
        do_ref[...] = r * (dn - n * jnp.mean(dn * n, axis=-1, keepdims=True))
        part = jnp.sum(dyv * n * sz, axis=0, keepdims=True)

        @pl.when(first)
        def _():
            dnw_ref[...] = part

        @pl.when(jnp.logical_not(first))
        def _():
            dnw_ref[...] += part

    blk = BS((tm, LANES), lambda i, h: (i, h))
    return pl.pallas_call(
        body, grid=(S // tm, GDN_H),
        in_specs=[BS((2, tm, LANES), lambda i, h: (0, i, h)), BS((tm, LANES), lambda i, h: (i, zoff + h)),
                  BS((1, LANES), lambda i, h: (0, 0)), blk],
        out_specs=[blk, blk, BS((1, LANES), lambda i, h: (0, 0))],
        out_shape=[SDS((S, GDN_H * GDN_DK), F32), SDS((S, GDN_H * GDN_DK), BF16), SDS((1, LANES), F32)],
        name=name, compiler_params=_cp(("arbitrary", "arbitrary")))(o2, proj, nw, dy)


def _gate_prm(a_log, dt_bias):
    z = jnp.zeros((8, LANES), F32)
    z = z.at[0, :2 * GDN_H].set(a_log.reshape(-1))
    return z.at[1, :2 * GDN_H].set(dt_bias.reshape(-1))


def gdn_fwd(x, g, w_all, convw, a_log, dt_bias, nw, w_out, tag, side=None):
    S = x.shape[0]
    h, ht = rms_fwd(x, g, f"{tag}_rms")
    proj = mm(h, w_all, name=f"{tag}_proj", tn=1408)
    qkv = gdn_pre_fwd(proj, convw, f"{tag}_pre")
    prm = _gate_prm(a_log, dt_bias)
    gb = gdn_gate_fwd(proj, prm, f"{tag}_gate")
    grow = gb[:, :2 * GDN_H].T.reshape(2 * GDN_H, S // GDN_C, GDN_C)
    o2, states, *side_out = gdn_scan_fwd(qkv, gb, grow, f"{tag}_scan", side)
    y = gdn_post_fwd(o2, proj, nw, f"{tag}_post")
    xn = mm(y, w_out, name=f"{tag}_out", epi=_add, extras=(x,))
    return xn, (ht, proj, qkv, prm, gb, grow, o2, states, y), side_out


def gdn_bwd(x, g, w_all, convw, nw, w_out, saved, dx, dxb, tag, side=None):
    S = x.shape[0]
    ht, proj, qkv, prm, gb, grow, o2, states, y = saved
    dw_out = mm(y, dxb, name=f"{tag}_dwout", ta=True)
    dy = mm(dxb, w_out, name=f"{tag}_dy", tb=True)
    do, dz, dnw = gdn_post_bwd(o2, proj, nw, dy, f"{tag}_postb")
    dqkv, dgate, *side_out = gdn_scan_bwd(qkv, gb, grow, states, do, f"{tag}_scanb", side)
    dgb = jnp.transpose(dgate, (2, 3, 0, 1)).reshape(S, 4 * GDN_H)
    dgb = jnp.pad(dgb, ((0, 0), (0, LANES - 4 * GDN_H)))
    dab, dprm = gdn_gate_bwd(proj, prm, dgb, f"{tag}_gateb")
    dpq, dconvw = gdn_pre_bwd(proj, convw, dqkv, f"{tag}_preb")
    dproj = jnp.concatenate([dpq, dz, dab], axis=1)
    dw_all = mm(ht, dproj, name=f"{tag}_dwin", tn=384)
    dh = mm(dproj, w_all, name=f"{tag}_dh", tb=True)
    dx, dxb, dg = rms_bwd(x, g, dh, dx, f"{tag}_rmsb")
    da_log = dprm[0, :2 * GDN_H].reshape(2, GDN_H)
    ddt = dprm[1, :2 * GDN_H].reshape(2, GDN_H)
    return dx, dxb, dg, dw_all, dconvw, da_log, ddt, dnw, dw_out, side_out


def _rel_bucket_np(rel):
    nb = REL_BUCKETS // 2
    max_exact = nb // 2
    ret = np.where(rel > 0, nb, 0)
    n = np.abs(rel)
    nf = np.maximum(n, 1).astype(np.float32)
    large = max_exact + (np.log(nf / max_exact) / np.float32(math.log(REL_MAX_DIST / max_exact))
                         * (nb - max_exact)).astype(np.int32)
    large = np.minimum(large, nb - 1)
    return ret + np.where(n < max_exact, n, large)


def _toeplitz(f, rows, cols):
    period = rows + cols
    e = jnp.pad(f, ((0, 0), (0, period - f.shape[1])))
    y = jnp.tile(e, (1, rows))[:, :rows * (period - 1)]
    return y.reshape(f.shape[0], rows, period - 1)[:, :, :cols]


ATT_Q = DSWA_HALF
ATT_W = 3 * DSWA_HALF
ATT_TB = 1024
ATT_PAIRS = DSWA_HG // 2


def _bias_mats(rel_table, gi):
    _, dil = DSWA_CFG[gi]
    offs = np.arange(-DSWA_HALF, DSWA_HALF + 1)
    onehot = jnp.asarray(np.eye(REL_BUCKETS, dtype=np.float32)[_rel_bucket_np(offs * dil)])
    f = jnp.dot(onehot, rel_table, precision=HI)[:, gi * DSWA_HG:(gi + 1) * DSWA_HG].T
    bias = _toeplitz(f, ATT_Q, ATT_W)
    bias_t = jnp.transpose(_toeplitz(f[:, ::-1], ATT_Q, ATT_W), (0, 2, 1))
    return bias.reshape(ATT_PAIRS, 2, ATT_Q, ATT_W), bias_t.reshape(ATT_PAIRS, 2, ATT_W, ATT_Q)


def _att_specs(S, d, col):
    halo = DSWA_HALF * d
    per = ATT_TB // halo
    last = S // halo - 1
    cur = BS((ATT_TB, LANES), lambda p, tb: (tb, col(p)))
    prev = BS((halo, LANES), lambda p, tb: (jnp.maximum(tb * per - 1, 0), col(p)))
    nxt = BS((halo, LANES), lambda p, tb: (jnp.minimum((tb + 1) * per, last), col(p)))
    return prev, cur, nxt


def _att_specs3(S, d, lead):
    halo = DSWA_HALF * d
    per = ATT_TB // halo
    last = S // halo - 1
    cur = BS((1, ATT_TB, LANES), lambda p, tb: (lead(p), tb, 0))
    prev = BS((1, halo, LANES), lambda p, tb: (lead(p), jnp.maximum(tb * per - 1, 0), 0))
    nxt = BS((1, halo, LANES), lambda p, tb: (lead(p), jnp.minimum((tb + 1) * per, last), 0))
    return prev, cur, nxt


class _Pieces:
    def __init__(self, prev, cur, nxt, d, lead=None, cast=None):
        self.refs, self.d, self.lead, self.cast, self.cache = (prev, cur, nxt), d, lead, cast, {}
        self.halo = DSWA_HALF * d
        self.nsb = ATT_TB // self.halo

    def __call__(self, r, sb):
        if (r, sb) not in self.cache:
            ref = self.refs[0] if sb < 0 else self.refs[2] if sb >= self.nsb else self.refs[1]
            start = r + (self.halo * sb if 0 <= sb < self.nsb else 0)
            rows = pl.ds(start, ATT_Q, stride=self.d) if self.d > 1 else pl.ds(start, ATT_Q)
            v = ref[rows, :] if self.lead is None else ref[0, rows, :]
            self.cache[(r, sb)] = v if self.cast is None else v.astype(self.cast)
        return self.cache[(r, sb)]

    def window(self, r, sb):
        return jnp.concatenate([self(r, sb - 1), self(r, sb), self(r, sb + 1)], axis=0)


ATT_GROUP = 8


def _tile_groups(d, nsb):
    tiles = [(r, sb) for r in range(d) for sb in range(nsb)]
    return [tiles[i:i + ATT_GROUP] for i in range(0, len(tiles), ATT_GROUP)]


def _tile_rows(r, sb, d):
    start = r + DSWA_HALF * d * sb
    return pl.ds(start, ATT_Q, stride=d) if d > 1 else pl.ds(start, ATT_Q)


def _tile_valid(tb, r, sb, d, S, transposed):
    shape = (ATT_W, ATT_Q) if transposed else (ATT_Q, ATT_W)
    blk = lax.broadcasted_iota(jnp.int32, shape, 1 if transposed else 0)
    win = lax.broadcasted_iota(jnp.int32, shape, 0 if transposed else 1)
    tok = tb * ATT_TB + r + d * (DSWA_HALF * (sb - 1) + win)
    return (jnp.abs(win - DSWA_HALF - blk) <= DSWA_HALF) & (tok >= 0) & (tok < S)


def _head_masks():
    lane = lax.broadcasted_iota(jnp.int32, (1, LANES), 1)
    return [lane < DSWA_E, lane >= DSWA_E], lane


def attn_fwd(qkv, bias, gi, name):
    S = qkv.shape[0]
    d = DSWA_CFG[gi][1]
    nsb = ATT_TB // (DSWA_HALF * d)
    npair = DSWA_HEADS // 2
    q_spec = _att_specs(S, d, lambda p: gi * ATT_PAIRS + p)[1]
    k_specs = _att_specs(S, d, lambda p: npair + gi * ATT_PAIRS + p)
    v_specs = _att_specs(S, d, lambda p: 2 * npair + gi * ATT_PAIRS + p)

    def body(q_ref, kp, kc, kn, vp, vc, vn, b_ref, o_ref, lse_ref):
        tb = pl.program_id(1)
        masks, lane = _head_masks()
        kpc = _Pieces(kp, kc, kn, d, cast=BF16)
        vpc = _Pieces(vp, vc, vn, d, cast=BF16)
        scale = DSWA_E ** -0.5
        for grp in _tile_groups(d, nsb):
            rows = [_tile_rows(r, sb, d) for r, sb in grp]
            qs = [q_ref[rw, :] for rw in rows]
            kws = [kpc.window(r, sb) for r, sb in grp]
            vws = [vpc.window(r, sb) for r, sb in grp]
            valids = [_tile_valid(tb, r, sb, d, S, False) for r, sb in grp]
            both = [(t, hh) for t in range(len(grp)) for hh in range(2)]
            ss = [_dot_nt(_bf(jnp.where(masks[hh], qs[t], 0.0)), kws[t]) * scale + b_ref[0, hh] for t, hh in both]
            ss = [jnp.where(valids[t], s, NEG_INF) for (t, hh), s in zip(both, ss)]
            ms = [jnp.max(s, axis=-1, keepdims=True) for s in ss]
            ps = [jnp.exp(s - m) for s, m in zip(ss, ms)]
            ls = [jnp.sum(p, axis=-1, keepdims=True) for p in ps]
            os = [_dot(_bf(p / l), vws[t]) for (t, hh), p, l in zip(both, ps, ls)]
            for t, rw in enumerate(rows):
                o_ref[rw, :] = jnp.where(masks[0], os[2 * t], os[2 * t + 1])
                lse_ref[0, rw, :] = (jnp.where(lane == 0, ms[2 * t] + jnp.log(ls[2 * t]), 0.0)
                                     + jnp.where(lane == 1, ms[2 * t + 1] + jnp.log(ls[2 * t + 1]), 0.0))

    return pl.pallas_call(
        body, grid=(ATT_PAIRS, S // ATT_TB),
        in_specs=[q_spec, *k_specs, *v_specs, BS((1, 2, ATT_Q, ATT_W), lambda p, tb: (p, 0, 0, 0))],
        out_specs=[BS((ATT_TB, LANES), lambda p, tb: (tb, p)), BS((1, ATT_TB, LANES), lambda p, tb: (p, tb, 0))],
        out_shape=[SDS((S, DSWA_HG * DSWA_E), F32), SDS((ATT_PAIRS, S, LANES), F32)],
        name=name, compiler_params=_cp(("parallel", "parallel")))(qkv, qkv, qkv, qkv, qkv, qkv, qkv, bias)


def attn_bwd_q(qkv, bias, lse, do, dd, gi, name):
    S = qkv.shape[0]
    d = DSWA_CFG[gi][1]
    nsb = ATT_TB // (DSWA_HALF * d)
    npair = DSWA_HEADS // 2
    q_spec = _att_specs(S, d, lambda p: gi * ATT_PAIRS + p)[1]
    k_specs = _att_specs(S, d, lambda p: npair + gi * ATT_PAIRS + p)
    v_specs = _att_specs(S, d, lambda p: 2 * npair + gi * ATT_PAIRS + p)
    bspec = BS((1, 2, ATT_Q, ATT_W), lambda p, tb: (p, 0, 0, 0))

    def body(q_ref, kp, kc, kn, vp, vc, vn, b_ref, lse_ref, do_ref, dd_ref, dq_ref, db_ref):
        tb = pl.program_id(1)
        masks, lane = _head_masks()
        kpc = _Pieces(kp, kc, kn, d, cast=BF16)
        vpc = _Pieces(vp, vc, vn, d, cast=BF16)
        db = [jnp.zeros((ATT_Q, ATT_W), F32), jnp.zeros((ATT_Q, ATT_W), F32)]
        scale = DSWA_E ** -0.5
        for grp in _tile_groups(d, nsb):
            rows = [_tile_rows(r, sb, d) for r, sb in grp]
            qs = [q_ref[rw, :] for rw in rows]
            dos = [do_ref[0, rw, :] for rw in rows]
            lses = [lse_ref[0, rw, :] for rw in rows]
            dds = [dd_ref[0, 0, rw, :] for rw in rows]
            kws = [kpc.window(r, sb) for r, sb in grp]
            vws = [vpc.window(r, sb) for r, sb in grp]
            valids = [_tile_valid(tb, r, sb, d, S, False) for r, sb in grp]
            both = [(t, hh) for t in range(len(grp)) for hh in range(2)]
            ss = [_dot_nt(_bf(jnp.where(masks[hh], qs[t], 0.0)), kws[t]) * scale + b_ref[0, hh] for t, hh in both]
            dps = [_dot_nt(_bf(jnp.where(masks[hh], dos[t], 0.0)), vws[t]) for t, hh in both]
            ps = [jnp.exp(jnp.where(valids[t], s - lses[t][:, hh:hh + 1], NEG_INF)) for (t, hh), s in zip(both, ss)]
            dss = [p * (dp - dds[t][:, hh:hh + 1]) for (t, hh), p, dp in zip(both, ps, dps)]
            dqs = [_dot(_bf(ds), kws[t]) * scale for (t, hh), ds in zip(both, dss)]
            for t, rw in enumerate(rows):
                dq_ref[rw, :] = jnp.where(masks[0], dqs[2 * t], dqs[2 * t + 1])
                db[0] = db[0] + dss[2 * t]
                db[1] = db[1] + dss[2 * t + 1]

        @pl.when(tb == 0)
        def _():
            db_ref[0, 0] = db[0]
            db_ref[0, 1] = db[1]

        @pl.when(tb > 0)
        def _():
            db_ref[0, 0] += db[0]
            db_ref[0, 1] += db[1]

    return pl.pallas_call(
        body, grid=(ATT_PAIRS, S // ATT_TB),
        in_specs=[q_spec, *k_specs, *v_specs, bspec, BS((1, ATT_TB, LANES), lambda p, tb: (p, tb, 0)),
                  BS((1, ATT_TB, LANES), lambda p, tb: (gi, tb, p)), BS((1, 1, ATT_TB, LANES), lambda p, tb: (gi, p, tb, 0))],
        out_specs=[BS((ATT_TB, LANES), lambda p, tb: (tb, p)), bspec],
        out_shape=[SDS((S, DSWA_HG * DSWA_E), F32), SDS((ATT_PAIRS, 2, ATT_Q, ATT_W), F32)],
        name=name, compiler_params=_cp(("parallel", "arbitrary")))(qkv, qkv, qkv, qkv, qkv, qkv, qkv, bias, lse, do, dd)


def attn_bwd_kv(qkv, bias_t, lse, do, dd, gi, name):
    S = qkv.shape[0]
    d = DSWA_CFG[gi][1]
    nsb = ATT_TB // (DSWA_HALF * d)
    npair = DSWA_HEADS // 2
    q_specs = _att_specs(S, d, lambda p: gi * ATT_PAIRS + p)
    k_spec = _att_specs(S, d, lambda p: npair + gi * ATT_PAIRS + p)[1]
    v_spec = _att_specs(S, d, lambda p: 2 * npair + gi * ATT_PAIRS + p)[1]
    halo = DSWA_HALF * d
    per = ATT_TB // halo
    last = S // halo - 1

    def do_spec(rows, blk):
        return BS((1, rows, LANES), lambda p, tb: (gi, blk(tb), p))

    def dd_spec(rows, blk):
        return BS((1, 1, rows, LANES), lambda p, tb: (gi, p, blk(tb), 0))

    blks = [(halo, lambda tb: jnp.maximum(tb * per - 1, 0)), (ATT_TB, lambda tb: tb),
            (halo, lambda tb: jnp.minimum((tb + 1) * per, last))]
    do_specs = [do_spec(*b) for b in blks]
    dd_specs = [dd_spec(*b) for b in blks]
    lse_specs = _att_specs3(S, d, lambda p: p)

    class _Lead4:
        def __init__(self, ref):
            self.ref = ref

        def __getitem__(self, idx):
            return self.ref[(0,) + idx]

    def body(k_ref, v_ref, qp, qc, qn, dop, doc, don, lp, lc, ln, ddp, ddc, ddn, b_ref, dk_ref, dv_ref):
        tb = pl.program_id(1)
        masks, lane = _head_masks()
        qpc = _Pieces(qp, qc, qn, d)
        dopc = _Pieces(dop, doc, don, d, lead=True)
        lpc = _Pieces(lp, lc, ln, d, lead=True)
        ddpc = _Pieces(_Lead4(ddp), _Lead4(ddc), _Lead4(ddn), d, lead=True)
        scale = DSWA_E ** -0.5
        for grp in _tile_groups(d, nsb):
            rows = [_tile_rows(r, sb, d) for r, sb in grp]
            kcs = [_bf(k_ref[rw, :]) for rw in rows]
            vcs = [_bf(v_ref[rw, :]) for rw in rows]
            qws = [qpc.window(r, sb) for r, sb in grp]
            dows = [dopc.window(r, sb) for r, sb in grp]
            lws = [lpc.window(r, sb) for r, sb in grp]
            ddws = [ddpc.window(r, sb) for r, sb in grp]
            qwbs = [_bf(x) for x in qws]
            dowbs = [_bf(x) for x in dows]
            valids = [_tile_valid(tb, r, sb, d, S, True) for r, sb in grp]
            both = [(t, hh) for t in range(len(grp)) for hh in range(2)]
            ss = [_dot_nt(_bf(jnp.where(masks[hh], qws[t], 0.0)), kcs[t]) * scale + b_ref[0, hh] for t, hh in both]
            dps = [_dot_nt(_bf(jnp.where(masks[hh], dows[t], 0.0)), vcs[t]) for t, hh in both]
            ps = [jnp.exp(jnp.where(valids[t], s - lws[t][:, hh:hh + 1], NEG_INF)) for (t, hh), s in zip(both, ss)]
            dvs = [_dot_tn(_bf(p), dowbs[t]) for (t, hh), p in zip(both, ps)]
            dss = [p * (dp - ddws[t][:, hh:hh + 1]) for (t, hh), p, dp in zip(both, ps, dps)]
            dks = [_dot_tn(_bf(ds), qwbs[t]) * scale for (t, hh), ds in zip(both, dss)]
            for t, rw in enumerate(rows):
                dk_ref[rw, :] = jnp.where(masks[0], dks[2 * t], dks[2 * t + 1])
                dv_ref[rw, :] = jnp.where(masks[0], dvs[2 * t], dvs[2 * t + 1])

    out = BS((ATT_TB, LANES), lambda p, tb: (tb, p))
    return pl.pallas_call(
        body, grid=(ATT_PAIRS, S // ATT_TB),
        in_specs=[k_spec, v_spec, *q_specs, *do_specs, *lse_specs, *dd_specs,
                  BS((1, 2, ATT_W, ATT_Q), lambda p, tb: (p, 0, 0, 0))],
        out_specs=[out, out],
        out_shape=[SDS((S, DSWA_HG * DSWA_E), F32), SDS((S, DSWA_HG * DSWA_E), F32)],
        name=name, compiler_params=_cp(("parallel", "parallel")))(
            qkv, qkv, qkv, qkv, qkv, do, do, do, lse, lse, lse, dd, dd, dd, bias_t)


def _pair_alphas(lses):
    m = jnp.maximum(jnp.maximum(lses[0], lses[1]), lses[2])
    e = [jnp.exp(t - m) for t in lses]
    tot = e[0] + e[1] + e[2]
    return [t / tot for t in e]


def _pair_expand(a, lane):
    return jnp.where(lane < DSWA_E, a[:, 0:1], a[:, 1:2])


def combine_fwd(o_raw, lse, name):
    S = o_raw.shape[0]
    tm = min(512, S)

    def body(o_ref, l_ref, y_ref):
        g = pl.program_id(2)
        lane = lax.broadcasted_iota(jnp.int32, (1, LANES), 1)
        alphas = _pair_alphas([l_ref[0, 0], l_ref[1, 0], l_ref[2, 0]])
        a = jnp.where(g == 0, alphas[0], jnp.where(g == 1, alphas[1], alphas[2]))
        y_ref[...] = (o_ref[...] * _pair_expand(a, lane)).astype(BF16)

    blk = BS((tm, LANES), lambda i, p, g: (i, g * ATT_PAIRS + p))
    return pl.pallas_call(
        body, grid=(S // tm, ATT_PAIRS, 3),
        in_specs=[blk, BS((3, 1, tm, LANES), lambda i, p, g: (0, p, i, 0))], out_specs=blk,
        out_shape=SDS((S, DSWA_W), BF16), name=name, compiler_params=_cp(("parallel", "parallel", "parallel")))(o_raw, lse)


def combine_bwd(o_raw, lse, dy, name):
    S = o_raw.shape[0]
    tm = min(512, S)

    def body(o0, o1, o2, l_ref, d0, d1, d2, do_ref, dd_ref):
        lane = lax.broadcasted_iota(jnp.int32, (1, LANES), 1)
        alphas = _pair_alphas([l_ref[0, 0], l_ref[1, 0], l_ref[2, 0]])
        c = jnp.zeros((tm, LANES), F32)
        for g, (o_ref, dy_ref) in enumerate(((o0, d0), (o1, d1), (o2, d2))):
            dyv = dy_ref[...]
            do_ref[g] = dyv * _pair_expand(alphas[g], lane)
            prod = o_ref[...] * dyv
            dal = (jnp.where(lane == 0, jnp.sum(jnp.where(lane < DSWA_E, prod, 0.0), axis=1, keepdims=True), 0.0)
                   + jnp.where(lane == 1, jnp.sum(jnp.where(lane >= DSWA_E, prod, 0.0), axis=1, keepdims=True), 0.0))
            c = c + alphas[g] * dal
        for g in range(3):
            dd_ref[g, 0] = alphas[g] * c

    def col(g):
        return BS((tm, LANES), lambda i, p: (i, g * ATT_PAIRS + p))

    return pl.pallas_call(
        body, grid=(S // tm, ATT_PAIRS),
        in_specs=[col(0), col(1), col(2), BS((3, 1, tm, LANES), lambda i, p: (0, p, i, 0)), col(0), col(1), col(2)],
        out_specs=[BS((3, tm, LANES), lambda i, p: (0, i, p)), BS((3, 1, tm, LANES), lambda i, p: (0, p, i, 0))],
        out_shape=[SDS((3, S, DSWA_HG * DSWA_E), F32), SDS((3, ATT_PAIRS, S, LANES), F32)],
        name=name, compiler_params=_cp(("parallel", "parallel")))(o_raw, o_raw, o_raw, lse, dy, dy, dy)


def dswa_fwd(x, g, w_in, w_out, rel_table, tag):
    h, ht = rms_fwd(x, g, f"{tag}_rms")
    qkv = mm(h, w_in, name=f"{tag}_qkv", tn=1152)
    outs, lses = [], []
    for gi in range(3):
        bias, _ = _bias_mats(rel_table, gi)
        o, lse = attn_fwd(qkv, bias, gi, f"{tag}_att{gi}")
        outs.append(o)
        lses.append(lse)
    o_raw = jnp.concatenate(outs, axis=1)
    lse = jnp.stack(lses)
    y = combine_fwd(o_raw, lse, f"{tag}_comb")
    xn = mm(y, w_out, name=f"{tag}_out", epi=_add, extras=(x,))
    return xn, (ht, qkv, o_raw, lse, y)


def dswa_bwd(x, g, w_in, w_out, rel_table, saved, dx, dxb, tag):
    ht, qkv, o_raw, lse, y = saved
    dw_out = mm(y, dxb, name=f"{tag}_dwout", ta=True, tm=384)
    dy = mm(dxb, w_out, name=f"{tag}_dy", tb=True, tn=384)
    do_raw, dd = combine_bwd(o_raw, lse, dy, f"{tag}_combb")
    dqs, dks, dvs = [], [], []
    drel = jnp.zeros_like(rel_table)
    for gi in range(3):
        (bias, bias_t), bias_vjp = jax.vjp(lambda tbl: _bias_mats(tbl, gi), rel_table)
        dq, dbias = attn_bwd_q(qkv, bias, lse[gi], do_raw, dd, gi, f"{tag}_attq{gi}")
        dk, dv = attn_bwd_kv(qkv, bias_t, lse[gi], do_raw, dd, gi, f"{tag}_attkv{gi}")
        drel = drel + bias_vjp((dbias, jnp.zeros_like(bias_t)))[0]
        dqs.append(dq)
        dks.append(dk)
        dvs.append(dv)
    dqkv = jnp.concatenate(dqs + dks + dvs, axis=1).astype(BF16)
    dw_in = mm(ht, dqkv, name=f"{tag}_dwin", tn=384)
    dh = mm(dqkv, w_in, name=f"{tag}_dh", tb=True)
    dx, dxb, dg = rms_bwd(x, g, dh, dx, f"{tag}_rmsb")
    return dx, dxb, dg, dw_in, dw_out, drel


def adamw(w, g, m, v, name):
    shape = w.shape
    last = shape[-1]
    w2, g2, m2, v2 = (t.reshape(-1, last) for t in (w, g, m, v))
    rows = w2.shape[0]
    tr = rows
    if rows > 512:
        tr = next(t for t in (512, 256, 192, 128, 64, 8) if rows % t == 0)
    c1 = 1.0 / (1.0 - ADAM_B1 ** ADAM_STEP)
    c2 = 1.0 / (1.0 - ADAM_B2 ** ADAM_STEP)

    def body(w_ref, g_ref, m_ref, v_ref, d_ref, nm_ref, nv_ref):
        gv = g_ref[...]
        nm = ADAM_B1 * m_ref[...] + (1.0 - ADAM_B1) * gv
        nv = ADAM_B2 * v_ref[...] + (1.0 - ADAM_B2) * (gv * gv)
        nm_ref[...] = nm
        nv_ref[...] = nv
        d_ref[...] = -ADAM_LR * ((nm * c1) / (jnp.sqrt(nv * c2) + ADAM_EPS) + ADAM_WD * w_ref[...])

    spec = BS((tr, last), lambda i: (i, 0))
    outs = pl.pallas_call(
        body, grid=(rows // tr,), in_specs=[spec] * 4, out_specs=[spec] * 3,
        out_shape=[SDS((rows, last), F32)] * 3, name=name, compiler_params=_cp(("parallel",)))(w2, g2, m2, v2)
    return tuple(o.reshape(shape) for o in outs)


def _place():
    x, y, c = lax.axis_index("x"), lax.axis_index("y"), lax.axis_index("c")
    chips = [(1 - x, y), (x, 1 - y), (1 - x, 1 - y)]
    return x, y, c, chips


def _rcopy(src, dst, ssem, rsem, dev):
    return pltpu.make_async_remote_copy(src_ref=src, dst_ref=dst, send_sem=ssem, recv_sem=rsem, device_id=dev,
                                        device_id_type=MESH)


class SideJob(NamedTuple):
    ins: list
    outs: list
    sems: list
    start: Callable
    wait: Callable


def _job(ins, outs, sems, copies):
    def start(in_refs, out_refs, sem_refs):
        for cp in copies(in_refs, out_refs, sem_refs):
            cp.start()

    def wait(in_refs, out_refs, sem_refs):
        for cp in copies(in_refs, out_refs, sem_refs):
            cp.wait()

    return SideJob(list(ins), list(outs), list(sems), start, wait)


def gather_job(packs, halved):
    n = len(packs)
    dma = pltpu.SemaphoreType.DMA

    def copies(in_refs, out_refs, sems):
        ssem, rsem = sems
        x, y, c, chips = _place()
        jme = 2 * x + y
        cps = []
        for i, (p_ref, f_ref) in enumerate(zip(in_refs, out_refs)):
            rows = p_ref.shape[0]
            mine = pl.ds(c * (rows // 2), rows // 2) if halved[i] else pl.ds(0, rows)
            for r, (cx, cy) in enumerate(chips):
                cps.append(_rcopy(p_ref.at[mine], f_ref.at[jme, mine], ssem.at[i, r], rsem.at[i, r], (cx, cy, c)))
        return cps

    return _job(packs, [SDS((4,) + p.shape, p.dtype) for p in packs], [dma((n, 3)), dma((n, 3))], copies)


def chip_exchange_job(parts):
    n = len(parts)
    dma = pltpu.SemaphoreType.DMA

    def copies(in_refs, out_refs, sems):
        ssem, rsem = sems
        x, y, c, chips = _place()
        cps = []
        for i, (p_ref, r_ref) in enumerate(zip(in_refs, out_refs)):
            for r, (cx, cy) in enumerate(chips):
                cps.append(_rcopy(p_ref.at[2 * cx + cy], r_ref.at[r], ssem.at[i, r], rsem.at[i, r], (cx, cy, c)))
        return cps

    return _job(parts, [SDS((3,) + p.shape[1:], p.dtype) for p in parts], [dma((n, 3)), dma((n, 3))], copies)


def run_job(job, name):
    ni, no = len(job.ins), len(job.outs)

    def body(*refs):
        job.start(refs[:ni], refs[ni:ni + no], refs[ni + no:])
        job.wait(refs[:ni], refs[ni:ni + no], refs[ni + no:])

    return pl.pallas_call(
        body, in_specs=[ANY] * ni, out_specs=[ANY] * no, out_shape=job.outs, scratch_shapes=job.sems, name=name,
        compiler_params=pltpu.CompilerParams(has_side_effects=True))(*job.ins)


def forward_to_sibling(fulls, name):
    n = len(fulls)

    def body(*refs):
        in_refs, out_refs, (ssem, rsem) = refs[:n], refs[n:2 * n], refs[2 * n:]
        x, y, c, chips = _place()
        cps = []
        for i in range(n):
            half = in_refs[i].shape[1] // 2
            for r, (cx, cy) in enumerate(chips):
                piece = (2 * cx + cy, pl.ds(c * half, half))
                cps.append(_rcopy(in_refs[i].at[piece], out_refs[i].at[piece], ssem.at[i, r], rsem.at[i, r], (x, y, 1 - c)))
        for cp in cps:
            cp.start()
        for cp in cps:
            cp.wait()

    dma = pltpu.SemaphoreType.DMA
    return pl.pallas_call(
        body, in_specs=[ANY] * n, out_specs=[ANY] * n, out_shape=[SDS(f.shape, f.dtype) for f in fulls],
        scratch_shapes=[dma((n, 3)), dma((n, 3))], input_output_aliases={i: i for i in range(n)}, name=name,
        compiler_params=pltpu.CompilerParams(has_side_effects=True))(*fulls)


def rs_sibling_exchange(gpack, name):
    _, rows, W = gpack.shape
    half = rows // 2

    def body(g_ref, r_ref, ssem, rsem):
        x, y, c, _ = _place()
        cps = [_rcopy(g_ref.at[j, pl.ds((1 - c) * half, half)], r_ref.at[j], ssem.at[j], rsem.at[j], (x, y, 1 - c))
               for j in range(4)]
        for cp in cps:
            cp.start()
        for cp in cps:
            cp.wait()

    dma = pltpu.SemaphoreType.DMA
    return pl.pallas_call(
        body, in_specs=[ANY], out_specs=ANY, out_shape=SDS((4, half, W), gpack.dtype),
        scratch_shapes=[dma((4,)), dma((4,))], name=name,
        compiler_params=pltpu.CompilerParams(has_side_effects=True))(gpack)


def _div_tile(n, limit):
    return next(t for t in range(limit - limit % 16, 0, -16) if n % t == 0)


def rs_add_sibling(gpack, recv, cidx, name, out_dtype=F32):
    _, rows, W = gpack.shape
    half = rows // 2
    tr = _div_tile(half, 1024)
    nb = half // tr

    def body(c_ref, g_ref, r_ref, o_ref):
        o_ref[...] = (g_ref[...].astype(F32) + r_ref[...].astype(F32)).astype(o_ref.dtype)

    gs = pltpu.PrefetchScalarGridSpec(
        num_scalar_prefetch=1, grid=(4, nb),
        in_specs=[BS((1, tr, W), lambda j, i, c: (j, c[0] * nb + i, 0)), BS((1, tr, W), lambda j, i, c: (j, i, 0))],
        out_specs=BS((1, tr, W), lambda j, i, c: (j, i, 0)))
    return pl.pallas_call(body, grid_spec=gs, out_shape=SDS((4, half, W), out_dtype), name=name,
                          compiler_params=_cp(("parallel", "parallel")))(cidx, gpack, recv)


def rs_add_chips(recv, part, place, name):
    _, half, W = recv.shape
    tr = _div_tile(half, 640)
    nb = half // tr

    def body(x_ref, y_ref, c_ref, r_ref, own_ref, o_ref):
        r0, r1, r2, own = (t.astype(F32) for t in (r_ref[0], r_ref[1], r_ref[2], own_ref[0]))
        o_ref[...] = ((r0 + r1) + r2) + own

    gs = pltpu.PrefetchScalarGridSpec(
        num_scalar_prefetch=3, grid=(nb,),
        in_specs=[BS((3, tr, W), lambda i, x, y, c: (0, i, 0)), BS((1, tr, W), lambda i, x, y, c: (2 * x[0] + y[0], i, 0))],
        out_specs=BS((tr, W), lambda i, x, y, c: (c[0] * nb + i, 0)))
    return pl.pallas_call(body, grid_spec=gs, out_shape=SDS((2 * half, W), F32), name=name,
                          compiler_params=_cp(("parallel",)))(*place, recv, part)


def rs_sibling_share(gsh, name):
    rows, W = gsh.shape
    half = rows // 2

    def body(g_ref, o_ref, ssem, rsem):
        x, y, c, _ = _place()
        mine = pl.ds(c * half, half)
        cp = _rcopy(g_ref.at[mine], o_ref.at[mine], ssem, rsem, (x, y, 1 - c))
        cp.start()
        cp.wait()

    dma = pltpu.SemaphoreType.DMA
    return pl.pallas_call(
        body, in_specs=[ANY], out_specs=ANY, out_shape=SDS(gsh.shape, gsh.dtype),
        scratch_shapes=[dma, dma], input_output_aliases={0: 0}, name=name,
        compiler_params=pltpu.CompilerParams(has_side_effects=True))(gsh)


def allreduce_small(pack):
    R = pack.shape[0]

    def body(p_ref, o_ref, all_ref, ssem, rsem):
        x, y, c, _ = _place()
        me = 4 * x + 2 * y + c
        all_ref[me] = p_ref[...]
        cps = []
        for m in range(1, 8):
            peer = (1 - x if m & 4 else x, 1 - y if m & 2 else y, 1 - c if m & 1 else c)
            cp = _rcopy(p_ref, all_ref.at[me], ssem.at[m - 1], rsem.at[m - 1], peer)
            cp.start()
            cps.append(cp)
        for cp in cps:
            cp.wait()
        acc = all_ref[0]
        for i in range(1, 8):
            acc = acc + all_ref[i]
        o_ref[...] = acc

    dma = pltpu.SemaphoreType.DMA
    vm = BS(memory_space=pltpu.VMEM)
    return pl.pallas_call(
        body, in_specs=[vm], out_specs=vm, out_shape=SDS(pack.shape, F32),
        scratch_shapes=[pltpu.VMEM((8, R, LANES), F32), dma((7,)), dma((7,))], name="allreduce_small",
        compiler_params=pltpu.CompilerParams(has_side_effects=True))(pack)


PACK_W = 1024
PACK_ALIGN = 32


def _layer_entries(l):
    if l % 2 == 0:
        mixer = [("gdn_w_in", l // 2, D_MODEL, GDN_IN // 4, True), ("gdn_w_out", l // 2, D_MODEL // 4, D_MODEL, False)]
    else:
        mixer = [("dswa_w_in", l // 2, D_MODEL, 3 * DSWA_W // 4, True), ("dswa_w_out", l // 2, DSWA_W // 4, D_MODEL, False)]
    return mixer + [("mlp_w1", l, D_MODEL, D_FF // 4, True), ("mlp_w2", l, D_FF // 4, D_MODEL, False)]


def _entries_offsets(entries):
    offs = [int(o) for o in np.cumsum([0] + [r * c // PACK_W for (_, _, r, c, _) in entries])]
    return offs, -(-offs[-1] // PACK_ALIGN) * PACK_ALIGN


def _layer_offsets(l):
    return _entries_offsets(_layer_entries(l))


def _pack_layer(l, shards, dtype):
    offs, total = _layer_offsets(l)
    parts = [shards[name][li].astype(dtype).reshape(-1, PACK_W) for (name, li, _, _, _) in _layer_entries(l)]
    parts.append(jnp.zeros((total - offs[-1], PACK_W), dtype))
    return jnp.concatenate(parts, axis=0)


def _unpack_layer(l, full, own, jme):
    offs, _ = _layer_offsets(l)
    mats = []
    for e, (_, _, r, c, by_col) in enumerate(_layer_entries(l)):
        mine = own[offs[e]:offs[e + 1]]
        sh = [jnp.where(jme == j, mine, full[j, offs[e]:offs[e + 1]]).reshape(r, c) for j in range(4)]
        mats.append(jnp.concatenate(sh, axis=1 if by_col else 0))
    return mats


def _pack_grads(entries, grads):
    offs, total = _entries_offsets(entries)
    per_chip = []
    for j in range(4):
        parts = []
        for g, (_, _, r, c, by_col) in zip(grads, entries):
            sh = g[:, c * j:c * (j + 1)] if by_col else g[r * j:r * (j + 1), :]
            parts.append(sh.astype(BF16).reshape(-1, PACK_W))
        parts.append(jnp.zeros((total - offs[-1], PACK_W), BF16))
        per_chip.append(jnp.concatenate(parts, axis=0))
    return jnp.stack(per_chip)


def _unpack_shard_grads(units):
    out = {}
    for entries, gsh in units:
        offs, _ = _entries_offsets(entries)
        for e, (name, _, r, c, _) in enumerate(entries):
            out.setdefault(name, []).append(gsh[offs[e]:offs[e + 1]].reshape(r, c))
    return {k: jnp.stack(v) for k, v in out.items()}


def _flat_pad(t, mult=8 * LANES):
    f = t.reshape(-1)
    return jnp.pad(f, (0, (-f.shape[0]) % mult))


def kernel(x, norm_mix, norm_mlp, norm_final, rel_bias, gdn_w_in, gdn_conv_w, gdn_a_log, gdn_dt_bias, gdn_norm_w, gdn_w_out, dswa_w_in, dswa_w_out, mlp_w1, mlp_w2, loss_target, m_norm_mix, m_norm_mlp, m_norm_final, m_rel_bias, m_gdn_w_in, m_gdn_conv_w, m_gdn_a_log, m_gdn_dt_bias, m_gdn_norm_w, m_gdn_w_out, m_dswa_w_in, m_dswa_w_out, m_mlp_w1, m_mlp_w2, v_norm_mix, v_norm_mlp, v_norm_final, v_rel_bias, v_gdn_w_in, v_gdn_conv_w, v_gdn_a_log, v_gdn_dt_bias, v_gdn_norm_w, v_gdn_w_out, v_dswa_w_in, v_dswa_w_out, v_mlp_w1, v_mlp_w2):
    xi, yi, ci = lax.axis_index("x"), lax.axis_index("y"), lax.axis_index("c")
    jme = 2 * xi + yi
    big = dict(gdn_w_in=gdn_w_in, gdn_w_out=gdn_w_out, dswa_w_in=dswa_w_in, dswa_w_out=dswa_w_out, mlp_w1=mlp_w1, mlp_w2=mlp_w2)
    n_gdn = gdn_w_in.shape[0]
    conv_cols = gdn_conv_w.shape[-1]

    packs = [_pack_layer(l, big, BF16) for l in range(DEPTH)]
    convp = jnp.pad(gdn_conv_w.reshape(n_gdn * GDN_CONV, conv_cols), ((0, 16 - n_gdn * GDN_CONV), (0, 0)))
    raw0, cfull = run_job(gather_job([packs[0], convp], [True, False]), "gather_l0")
    fulls = {0: forward_to_sibling([raw0], "forward_l0")[0]}
    cfull = jnp.where((jnp.arange(4) == jme)[:, None, None], convp[None], cfull)
    conv_all = jnp.transpose(cfull[:, :n_gdn * GDN_CONV], (1, 0, 2)).reshape(n_gdn, GDN_CONV, 4 * conv_cols)
    conv_all = jnp.pad(conv_all, ((0, 0), (0, 8 - GDN_CONV), (0, 0)))
    fwd_jobs = {0: [1, 2], 2: [3]}

    xs = x[0]
    saved = []
    for l in range(DEPTH):
        w_in, w_out, w1, w2 = _unpack_layer(l, fulls[l], packs[l], jme)
        gm, gp = norm_mix[l][None], norm_mlp[l][None]
        a = l // 2
        if l % 2 == 0:
            w_in = jnp.pad(w_in, ((0, 0), (0, GDN_INP - GDN_IN)))
            job = gather_job([packs[t] for t in fwd_jobs[l]], [True] * len(fwd_jobs[l]))
            x_mid, sv, raws = gdn_fwd(xs, gm, w_in, conv_all[a], gdn_a_log[a], gdn_dt_bias[a], gdn_norm_w[a][None], w_out,
                                      f"l{l}_gdn", job)
            for t, f in zip(fwd_jobs[l], forward_to_sibling(raws, f"forward_from_l{l}")):
                fulls[t] = f
        else:
            x_mid, sv = dswa_fwd(xs, gm, w_in, w_out, rel_bias, f"l{l}_att")
        x_out, sv2 = mlp_fwd(x_mid, gp, w1, w2, f"l{l}_mlp")
        saved.append((xs, x_mid, (w_in, w_out, w1, w2), sv, sv2))
        xs = x_out

    cidx = ci.astype(jnp.int32).reshape(1)
    place = [t.astype(jnp.int32).reshape(1) for t in (xi, yi, ci)]
    units = {"0a": _layer_entries(0)[:2], "0b": _layer_entries(0)[2:], **{str(l): _layer_entries(l) for l in (1, 2, 3)}}

    def chip_partial(u, grads):
        gpack = _pack_grads(units[u], grads)
        return rs_add_sibling(gpack, rs_sibling_exchange(gpack, f"rs_sibling_{u}"), cidx, f"rs_add_sibling_{u}", BF16)

    def finish(u, recv):
        return rs_sibling_share(rs_add_chips(recv, parts[u], place, f"rs_add_chips_{u}"), f"rs_share_{u}")

    loss_part, dx, dxb, d_final = loss_head(xs, norm_final[None], loss_target[0], "loss_head")
    d_mix, d_mlp = [None] * DEPTH, [None] * DEPTH
    d_conv, d_alog, d_dt, d_nw = [None] * n_gdn, [None] * n_gdn, [None] * n_gdn, [None] * n_gdn
    d_rel = jnp.zeros_like(rel_bias)
    parts, gshs = {}, {}
    bwd_jobs = {2: ["3"], 0: ["2", "1", "0b"]}
    for l in reversed(range(DEPTH)):
        x_in, x_mid, (w_in, w_out, w1, w2), sv, sv2 = saved[l]
        gm, gp = norm_mix[l][None], norm_mlp[l][None]
        a = l // 2
        dx, dxb, d_mlp[l], dw1, dw2 = mlp_bwd(x_mid, gp, w1, w2, sv2, dx, dxb, f"l{l}_mlp")
        if l == 0:
            parts["0b"] = chip_partial("0b", [dw1, dw2])
        if l % 2 == 0:
            job = chip_exchange_job([parts[u] for u in bwd_jobs[l]])
            dx, dxb, d_mix[l], dw_all, d_conv[a], d_alog[a], d_dt[a], d_nw[a], dwo, recvs = gdn_bwd(
                x_in, gm, w_in, conv_all[a], gdn_norm_w[a][None], w_out, sv, dx, dxb, f"l{l}_gdn", job)
            for u, rv in zip(bwd_jobs[l], recvs):
                gshs[u] = finish(u, rv)
            dwi = dw_all[:, :GDN_IN]
        else:
            dx, dxb, d_mix[l], dwi, dwo, drel = dswa_bwd(x_in, gm, w_in, w_out, rel_bias, sv, dx, dxb, f"l{l}_att")
            d_rel = d_rel + drel
        if l == 0:
            parts["0a"] = chip_partial("0a", [dwi, dwo])
        else:
            parts[str(l)] = chip_partial(str(l), [dwi, dwo, dw1, dw2])
    gshs["0a"] = finish("0a", run_job(chip_exchange_job([parts["0a"]]), "rs_chip_exchange_0a")[0])
    gbig = _unpack_shard_grads([(units[u], gshs[u]) for u in ("0a", "0b", "1", "2", "3")])

    small = [jnp.concatenate(d_mix, axis=0), jnp.concatenate(d_mlp, axis=0), d_final, d_rel,
             jnp.stack(d_conv), jnp.stack(d_alog), jnp.stack(d_dt), jnp.concatenate(d_nw, axis=0)]
    flat = [_flat_pad(t) for t in small]
    sizes = [f.shape[0] for f in flat]
    red = allreduce_small(jnp.concatenate(flat).reshape(-1, LANES)).reshape(-1)
    offs = np.cumsum([0] + sizes)
    red = [red[offs[i]:offs[i] + small[i].size].reshape(small[i].shape) for i in range(len(small))]
    g_conv_all = red[4][:, :GDN_CONV].reshape(n_gdn, GDN_CONV, 1, 4 * conv_cols)
    g_conv = lax.dynamic_slice_in_dim(g_conv_all, jme * conv_cols, conv_cols, axis=3)
    g = dict(norm_mix=red[0], norm_mlp=red[1], norm_final=red[2].reshape(norm_final.shape), rel_bias=red[3],
             gdn_conv_w=g_conv, gdn_a_log=red[5], gdn_dt_bias=red[6], gdn_norm_w=red[7][:, :GDN_DK], **gbig)

    w = dict(norm_mix=norm_mix, norm_mlp=norm_mlp, norm_final=norm_final, rel_bias=rel_bias, gdn_conv_w=gdn_conv_w,
             gdn_a_log=gdn_a_log, gdn_dt_bias=gdn_dt_bias, gdn_norm_w=gdn_norm_w, **big)
    m = dict(norm_mix=m_norm_mix, norm_mlp=m_norm_mlp, norm_final=m_norm_final, rel_bias=m_rel_bias, gdn_w_in=m_gdn_w_in,
             gdn_conv_w=m_gdn_conv_w, gdn_a_log=m_gdn_a_log, gdn_dt_bias=m_gdn_dt_bias, gdn_norm_w=m_gdn_norm_w,
             gdn_w_out=m_gdn_w_out, dswa_w_in=m_dswa_w_in, dswa_w_out=m_dswa_w_out, mlp_w1=m_mlp_w1, mlp_w2=m_mlp_w2)
    v = dict(norm_mix=v_norm_mix, norm_mlp=v_norm_mlp, norm_final=v_norm_final, rel_bias=v_rel_bias, gdn_w_in=v_gdn_w_in,
             gdn_conv_w=v_gdn_conv_w, gdn_a_log=v_gdn_a_log, gdn_dt_bias=v_gdn_dt_bias, gdn_norm_w=v_gdn_norm_w,
             gdn_w_out=v_gdn_w_out, dswa_w_in=v_dswa_w_in, dswa_w_out=v_dswa_w_out, mlp_w1=v_mlp_w1, mlp_w2=v_mlp_w2)
    names = ["norm_mix", "norm_mlp", "norm_final", "rel_bias", "gdn_w_in", "gdn_conv_w", "gdn_a_log", "gdn_dt_bias",
             "gdn_norm_w", "gdn_w_out", "dswa_w_in", "dswa_w_out", "mlp_w1", "mlp_w2"]
    upd = {n: adamw(w[n], g[n], m[n], v[n], f"adamw_{n}") for n in names}
    loss = lax.psum(loss_part[0, 0], ("x", "y", "c"))
    return (loss, dx[None], *[g[n] for n in names], *[upd[n][0] for n in names], *[upd[n][1] for n in names],
            *[upd[n][2] for n in names])
```

```python
import math
from typing import Callable, NamedTuple

import numpy as np
import jax
import jax.numpy as jnp
from jax import lax
from jax.experimental import pallas as pl
from jax.experimental.pallas import tpu as pltpu

F32 = jnp.float32
BF16 = jnp.bfloat16
HI = lax.Precision.HIGHEST
BS = pl.BlockSpec
SDS = jax.ShapeDtypeStruct
MESH = pl.DeviceIdType.MESH
ANY = BS(memory_space=pl.ANY)

D_MODEL = 1024
D_FF = 4096
DEPTH = 4
RMS_EPS = 1e-6
NEG_INF = -1e30
LANES = 128
VMEM_LIMIT = 56 << 20

GDN_H = 8
GDN_DK = 128
GDN_CONV = 5
GDN_C = 64
GDN_GC = 8
GDN_HP_FWD = 8
GDN_HP_BWD = 4
GDN_QKV = 3 * GDN_H * GDN_DK
GDN_IN = GDN_QKV + GDN_H * GDN_DK + 4 * GDN_H
GDN_INP = 4224

DSWA_CFG = ((128, 1), (512, 4), (2048, 16))
DSWA_HG = 6
DSWA_E = 64
DSWA_HEADS = 18
DSWA_W = DSWA_HEADS * DSWA_E
DSWA_HALF = 64
REL_BUCKETS = 32
REL_MAX_DIST = 1024

ADAM_LR = 0.001
ADAM_B1 = 0.9
ADAM_B2 = 0.999
ADAM_EPS = 1e-08
ADAM_WD = 0.01
ADAM_STEP = 10


def _cp(sem=None):
    return pltpu.CompilerParams(dimension_semantics=sem, vmem_limit_bytes=VMEM_LIMIT)


def _dot(a, b, prec=None):
    return jnp.dot(a, b, precision=prec, preferred_element_type=F32)


def _dot_nt(a, b, prec=None):
    return lax.dot_general(a, b, (((1,), (1,)), ((), ())), precision=prec, preferred_element_type=F32)


def _dot_tn(a, b, prec=None):
    return lax.dot_general(a, b, (((0,), (0,)), ((), ())), precision=prec, preferred_element_type=F32)


def _bf(a):
    return a.astype(BF16)


def _sigmoid(x):
    return 1.0 / (1.0 + jnp.exp(-x))


def rms_fwd(x, g, name):
    S, Dm = x.shape
    tm = min(512, S)

    def body(x_ref, g_ref, o_ref, ot_ref):
        xv = x_ref[...]
        r = lax.rsqrt(jnp.mean(xv * xv, axis=-1, keepdims=True) + RMS_EPS)
        hb = (xv * r * g_ref[...]).astype(o_ref.dtype)
        o_ref[...] = hb
        ot_ref[...] = hb.T

    return pl.pallas_call(
        body, grid=(S // tm,),
        in_specs=[BS((tm, Dm), lambda i: (i, 0)), BS((1, Dm), lambda i: (0, 0))],
        out_specs=[BS((tm, Dm), lambda i: (i, 0)), BS((Dm, tm), lambda i: (0, i))],
        out_shape=[SDS((S, Dm), BF16), SDS((Dm, S), BF16)], name=name, compiler_params=_cp(("parallel",)))(x, g)


def rms_bwd(x, g, dh, dres, name):
    S, Dm = x.shape
    tm = min(512, S)

    def body(x_ref, g_ref, dh_ref, dres_ref, dx_ref, dxb_ref, dg_ref):
        i = pl.program_id(0)
        xv = x_ref[...]
        r = lax.rsqrt(jnp.mean(xv * xv, axis=-1, keepdims=True) + RMS_EPS)
        n = xv * r
        dhv = dh_ref[...]
        t = dhv * g_ref[...]
        dx = dres_ref[...] + r * (t - n * jnp.mean(n * t, axis=-1, keepdims=True))
        dx_ref[...] = dx
        dxb_ref[...] = dx.astype(BF16)
        part = jnp.sum(dhv * n, axis=0, keepdims=True)

        @pl.when(i == 0)
        def _():
            dg_ref[...] = part

        @pl.when(i > 0)
        def _():
            dg_ref[...] += part

    row = BS((tm, Dm), lambda i: (i, 0))
    vec = BS((1, Dm), lambda i: (0, 0))
    return pl.pallas_call(
        body, grid=(S // tm,), in_specs=[row, vec, row, row], out_specs=[row, row, vec],
        out_shape=[SDS((S, Dm), F32), SDS((S, Dm), BF16), SDS((1, Dm), F32)],
        name=name, compiler_params=_cp(("arbitrary",)))(x, g, dh, dres)


def loss_head(x, g, tgt, name):
    S, Dm = x.shape
    tm = min(512, S)

    def body(x_ref, g_ref, t_ref, loss_ref, dx_ref, dxb_ref, dg_ref):
        i = pl.program_id(0)
        xv = x_ref[...]
        gv = g_ref[...]
        r = lax.rsqrt(jnp.mean(xv * xv, axis=-1, keepdims=True) + RMS_EPS)
        n = xv * r
        err = n * gv - t_ref[...]
        lpart = 0.5 * jnp.sum(jnp.mean(err * err, axis=-1, keepdims=True), axis=0, keepdims=True)
        dout = err * (1.0 / Dm)
        t = dout * gv
        dx = r * (t - n * jnp.mean(n * t, axis=-1, keepdims=True))
        dx_ref[...] = dx
        dxb_ref[...] = dx.astype(BF16)
        part = jnp.sum(dout * n, axis=0, keepdims=True)

        @pl.when(i == 0)
        def _():
            dg_ref[...] = part
            loss_ref[...] = lpart

        @pl.when(i > 0)
        def _():
            dg_ref[...] += part
            loss_ref[...] += lpart

    row = BS((tm, Dm), lambda i: (i, 0))
    vec = BS((1, Dm), lambda i: (0, 0))
    one = BS((1, 1), lambda i: (0, 0))
    return pl.pallas_call(
        body, grid=(S // tm,), in_specs=[row, vec, row], out_specs=[one, row, row, vec],
        out_shape=[SDS((1, 1), F32), SDS((S, Dm), F32), SDS((S, Dm), BF16), SDS((1, Dm), F32)],
        name=name, compiler_params=_cp(("arbitrary",)))(x, g, tgt)


MM_DEEP_K = 2048


def mm(a, b, *, name, ta=False, tb=False, tm=1024, tn=512, out_dtype=F32, pre_a=None, epi=None, extras=()):
    M, K = (a.shape[1], a.shape[0]) if ta else a.shape
    N = b.shape[0] if tb else b.shape[1]
    if ta:
        tm = min(tm, 512)
    elif K > MM_DEEP_K and N % 256 == 0:
        tn = min(tn, 256)
    tm, tn = min(tm, M), min(tn, N)
    assert M % tm == 0 and N % tn == 0, (name, M, N, K, tm, tn)
    ne = len(extras)
    a_spec = BS((K, tm), lambda i, j: (0, i)) if ta else BS((tm, K), lambda i, j: (i, 0))
    b_spec = BS((tn, K), lambda i, j: (j, 0)) if tb else BS((K, tn), lambda i, j: (0, j))
    o_spec = BS((tm, tn), lambda i, j: (i, j))
    dims = (((0 if ta else 1,), (1 if tb else 0,)), ((), ()))

    def body(a_ref, b_ref, *rest):
        e_refs, o_ref = rest[:ne], rest[ne]
        av = a_ref[...]
        if pre_a is not None:
            av = pre_a(av)
        acc = lax.dot_general(_bf(av), _bf(b_ref[...]), dims, preferred_element_type=F32)
        res = epi(acc, *[e[...] for e in e_refs]) if epi is not None else acc
        o_ref[...] = res.astype(o_ref.dtype)

    return pl.pallas_call(
        body, grid=(M // tm, N // tn), in_specs=[a_spec, b_spec] + [o_spec] * ne, out_specs=o_spec,
        out_shape=SDS((M, N), out_dtype), name=name, compiler_params=_cp(("parallel", "parallel")))(a, b, *extras)


def _relu(acc):
    return jnp.maximum(acc, 0.0)


def _add(acc, res):
    return acc + res


def _sq(av):
    return av * av


def _times_2r(acc, r):
    return acc * (2.0 * r.astype(F32))


def mlp_fwd(x, g, w1, w2, tag):
    h, ht = rms_fwd(x, g, f"{tag}_rms")
    r = mm(h, w1, name=f"{tag}_up", tn=1024, out_dtype=BF16, epi=_relu)
    xn = mm(r, w2, name=f"{tag}_down", pre_a=_sq, epi=_add, extras=(x,))
    return xn, (ht, r)


def mlp_bwd(x, g, w1, w2, saved, dx, dxb, tag):
    ht, r = saved
    da = mm(dxb, w2, name=f"{tag}_dact", tb=True, tn=1024, out_dtype=BF16, epi=_times_2r, extras=(r,))
    dw2 = mm(r, dxb, name=f"{tag}_dw2", ta=True, pre_a=_sq)
    dw1 = mm(ht, da, name=f"{tag}_dw1")
    dh = mm(da, w1, name=f"{tag}_dh", tb=True)
    dx, dxb, dg = rms_bwd(x, g, dh, dx, f"{tag}_rmsb")
    return dx, dxb, dg, dw1, dw2


def _conv_taps(x, S):
    t = lax.broadcasted_iota(jnp.int32, x.shape, 0)
    taps = []
    for j in range(GDN_CONV):
        sh = j - GDN_CONV // 2
        xs = x if sh == 0 else pltpu.roll(x, (-sh) % S, 0)
        taps.append(jnp.where((t + sh >= 0) & (t + sh < S), xs, 0.0))
    return taps


def _qkv_scale(c):
    is_norm = c < 2 * GDN_H
    scale = jnp.where(c < GDN_H, GDN_DK ** -0.5, 1.0)
    return is_norm, scale


def gdn_pre_fwd(proj, convw, name):
    S = proj.shape[0]

    def body(p_ref, w_ref, o_ref):
        c = pl.program_id(0)
        x = p_ref[...]
        w = w_ref[...]
        y = jnp.zeros_like(x)
        for j, xs in enumerate(_conv_taps(x, S)):
            y = y + w[j:j + 1, :] * xs
        t = y * _sigmoid(y)
        is_norm, scale = _qkv_scale(c)
        r = lax.rsqrt(jnp.sum(t * t, axis=-1, keepdims=True) + 1e-6)
        o_ref[...] = jnp.where(is_norm, t * r * scale, t)

    return pl.pallas_call(
        body, grid=(GDN_QKV // LANES,),
        in_specs=[BS((S, LANES), lambda c: (0, c)), BS((8, LANES), lambda c: (0, c))],
        out_specs=BS((S, LANES), lambda c: (0, c)),
        out_shape=SDS((S, GDN_QKV), F32), name=name, compiler_params=_cp(("parallel",)))(proj, convw)


def gdn_pre_bwd(proj, convw, dqkv, name):
    S = proj.shape[0]

    def body(p_ref, w_ref, d_ref, dp_ref, dw_ref):
        c = pl.program_id(0)
        x = p_ref[...]
        w = w_ref[...]
        taps = _conv_taps(x, S)
        y = jnp.zeros_like(x)
        for j, xs in enumerate(taps):
            y = y + w[j:j + 1, :] * xs
        sg = _sigmoid(y)
        t = y * sg
        is_norm, scale = _qkv_scale(c)
        dout = d_ref[0, 0] + d_ref[1, 0]
        r = lax.rsqrt(jnp.sum(t * t, axis=-1, keepdims=True) + 1e-6)
        n = t * r
        dn = dout * scale
        dt_norm = r * (dn - n * jnp.sum(dn * n, axis=-1, keepdims=True))
        dt = jnp.where(is_norm, dt_norm, dout)
        dy = dt * (sg * (1.0 + y * (1.0 - sg)))
        row = lax.broadcasted_iota(jnp.int32, (8, LANES), 0)
        dw = jnp.zeros((8, LANES), F32)
        for j, xs in enumerate(taps):
            dw = dw + jnp.where(row == j, jnp.sum(dy * xs, axis=0, keepdims=True), 0.0)
        dw_ref[...] = dw
        tt = lax.broadcasted_iota(jnp.int32, x.shape, 0)
        dx = jnp.zeros_like(x)
        for j in range(GDN_CONV):
            sh = j - GDN_CONV // 2
            ds = dy if sh == 0 else pltpu.roll(dy, sh % S, 0)
            dx = dx + w[j:j + 1, :] * jnp.where((tt - sh >= 0) & (tt - sh < S), ds, 0.0)
        dp_ref[...] = dx.astype(BF16)

    return pl.pallas_call(
        body, grid=(GDN_QKV // LANES,),
        in_specs=[BS((S, LANES), lambda c: (0, c)), BS((8, LANES), lambda c: (0, c)),
                  BS((2, 1, S, LANES), lambda c: (0, c // GDN_H, 0, c % GDN_H))],
        out_specs=[BS((S, LANES), lambda c: (0, c)), BS((8, LANES), lambda c: (0, c))],
        out_shape=[SDS((S, GDN_QKV), BF16), SDS((8, GDN_QKV), F32)],
        name=name, compiler_params=_cp(("parallel",)))(proj, convw, dqkv)


def _chunk_sum_matrix(n, upper):
    i = lax.broadcasted_iota(jnp.int32, (n, n), 0)
    j = lax.broadcasted_iota(jnp.int32, (n, n), 1)
    same = (i // GDN_C) == (j // GDN_C)
    tri = (i <= j) if upper else (i >= j)
    return jnp.where(same & tri, 1.0, 0.0).astype(F32)


def _gate_lanes(shape):
    lane = lax.broadcasted_iota(jnp.int32, shape, 1)
    return lane < GDN_H, (lane >= GDN_H) & (lane < 2 * GDN_H), (lane >= 2 * GDN_H) & (lane < 4 * GDN_H)


def gdn_gate_fwd(proj, prm, name):
    S = proj.shape[0]
    tm = min(512, S)
    ct = GDN_INP // LANES - 1

    def body(p_ref, prm_ref, o_ref):
        ab = p_ref[...]
        a_log = prm_ref[0:1, :]
        dtb = prm_ref[1:2, :]
        z = ab + dtb
        sp = jnp.maximum(z, 0.0) + jnp.log(1.0 + jnp.exp(-jnp.abs(z)))
        g = -jnp.exp(a_log) * sp
        is_f, is_b, is_beta = _gate_lanes(ab.shape)
        gf = _dot(_chunk_sum_matrix(tm, False), jnp.where(is_f, g, 0.0), HI)
        gbk = _dot(_chunk_sum_matrix(tm, True), jnp.where(is_b, g, 0.0), HI)
        o_ref[...] = gf + gbk + jnp.where(is_beta, _sigmoid(ab), 0.0)

    return pl.pallas_call(
        body, grid=(S // tm,),
        in_specs=[BS((tm, LANES), lambda i: (i, ct)), BS((8, LANES), lambda i: (0, 0))],
        out_specs=BS((tm, LANES), lambda i: (i, 0)),
        out_shape=SDS((S, LANES), F32), name=name, compiler_params=_cp(("parallel",)))(proj, prm)


def gdn_gate_bwd(proj, prm, dgb, name):
    S = proj.shape[0]
    tm = min(512, S)
    ct = GDN_INP // LANES - 1

    def body(p_ref, prm_ref, d_ref, dab_ref, dprm_ref):
        i = pl.program_id(0)
        ab = p_ref[...]
        a_log = prm_ref[0:1, :]
        dtb = prm_ref[1:2, :]
        z = ab + dtb
        sp = jnp.maximum(z, 0.0) + jnp.log(1.0 + jnp.exp(-jnp.abs(z)))
        ea = jnp.exp(a_log)
        g = -ea * sp
        is_f, is_b, is_beta = _gate_lanes(ab.shape)
        d = d_ref[...]
        dg = (_dot_tn(_chunk_sum_matrix(tm, False), jnp.where(is_f, d, 0.0), HI)
              + _dot_tn(_chunk_sum_matrix(tm, True), jnp.where(is_b, d, 0.0), HI))
        da = dg * (-ea) * _sigmoid(z)
        beta = _sigmoid(ab)
        dab_ref[...] = jnp.where(is_beta, d * beta * (1.0 - beta), da).astype(BF16)
        row = lax.broadcasted_iota(jnp.int32, (8, LANES), 0)
        part = (jnp.where(row == 0, jnp.sum(dg * g, axis=0, keepdims=True), 0.0)
                + jnp.where(row == 1, jnp.sum(da, axis=0, keepdims=True), 0.0))

        @pl.when(i == 0)
        def _():
            dprm_ref[...] = part

        @pl.when(i > 0)
        def _():
            dprm_ref[...] += part

    return pl.pallas_call(
        body, grid=(S // tm,),
        in_specs=[BS((tm, LANES), lambda i: (i, ct)), BS((8, LANES), lambda i: (0, 0)), BS((tm, LANES), lambda i: (i, 0))],
        out_specs=[BS((tm, LANES), lambda i: (i, 0)), BS((8, LANES), lambda i: (0, 0))],
        out_shape=[SDS((S, LANES), BF16), SDS((8, LANES), F32)],
        name=name, compiler_params=_cp(("arbitrary",)))(proj, prm, dgb)


def _tri_masks(d):
    i = lax.broadcasted_iota(jnp.int32, (GDN_C, GDN_C), 0)
    j = lax.broadcasted_iota(jnp.int32, (GDN_C, GDN_C), 1)
    s = (i - j) * (1 - 2 * d)
    return s >= 0, s > 0


def _split(a):
    hi = _bf(a)
    return hi, _bf(a - hi.astype(F32))


def _dot3(a, b):
    return _dot(a[0], b[0]) + (_dot(a[0], b[1]) + _dot(a[1], b[0]))


def _inv_unit_tri_many(mats):
    i = lax.broadcasted_iota(jnp.int32, mats[0].shape, 0)
    j = lax.broadcasted_iota(jnp.int32, mats[0].shape, 1)
    eye = jnp.where(i == j, 1.0, 0.0)
    ms = [-a for a in mats]
    ps = [eye + m for m in ms]
    for _ in range(int(math.log2(GDN_C)) - 1):
        sp = [_split(m) for m in ms]
        ms = [_dot3(s, s) for s in sp]
        sp = [_split(m) for m in ms]
        pp = [_split(p) for p in ps]
        ps = [p + _dot3(a, b) for p, a, b in zip(ps, pp, sp)]
    return ps


def _lane_col(x, lane_idx):
    lane = lax.broadcasted_iota(jnp.int32, x.shape, 1)
    return jnp.sum(jnp.where(lane == lane_idx, x, 0.0), axis=1, keepdims=True)


def _chunk_gates(gb_ref, grow_ref, hh, ci, d, head):
    gbv = gb_ref[ci * GDN_C:(ci + 1) * GDN_C, :]
    gcol = _lane_col(gbv, d * GDN_H + head)
    bcol = _lane_col(gbv, 2 * GDN_H + d * GDN_H + head)
    glast = jnp.where(d == 0, gcol[GDN_C - 1:GDN_C, :], gcol[0:1, :])
    return gcol, bcol, grow_ref[hh, ci:ci + 1, :], glast


def _chunk_base(q, k, gcol, grow, bcol, glast, d):
    incl, strict = _tri_masks(d)
    decay = jnp.where(incl, jnp.exp(jnp.where(incl, gcol - grow, 0.0)), 0.0)
    kb = k * bcol
    kk = _dot_nt(_bf(kb), _bf(k))
    qk = _dot_nt(_bf(q), _bf(k))
    eg = jnp.exp(gcol)
    ek = jnp.exp(glast - gcol)
    return dict(incl=incl, strict=strict, decay=decay, kb=kb, kk=kk, qk=qk, eg=eg, ek=ek, q_dec=q * eg, k_dec=k * ek,
                bcol=bcol, glast=glast)


def _block_terms(q_ref, k_ref, v_ref, gb_ref, grow_ref, d, h, hp):
    keys = [(hh, ci) for hh in range(hp) for ci in range(GDN_GC)]
    ts = []
    for hh, ci in keys:
        rows = slice(ci * GDN_C, (ci + 1) * GDN_C)
        cols = slice(hh * GDN_DK, (hh + 1) * GDN_DK)
        gcol, bcol, grow_v, glast = _chunk_gates(gb_ref, grow_ref, hh, ci, d, h * hp + hh)
        t = _chunk_base(q_ref[rows, cols], k_ref[rows, cols], gcol, grow_v, bcol, glast, d)
        t["v"] = v_ref[rows, cols]
        ts.append(t)
    tinvs = _inv_unit_tri_many([jnp.where(t["strict"], t["kk"] * t["decay"], 0.0) for t in ts])
    sp = [_split(x) for x in tinvs]
    us = [_dot3(s, _split(t["v"] * t["bcol"])) for s, t in zip(sp, ts)]
    ws = [_dot3(s, _split(t["kb"] * t["eg"])) for s, t in zip(sp, ts)]
    for t, tinv, u, w in zip(ts, tinvs, us, ws):
        t.update(tinv=tinv, u=u, w=w)
    return keys, ts


def _gdn_specs(S, nblk, order, hp):
    R = GDN_GC * GDN_C
    wd = hp * GDN_DK
    hb = GDN_H // hp

    def qkv_spec(part):
        return BS((R, wd), lambda d, h, n: (order(d, n), part * hb + h))

    gb_spec = BS((R, LANES), lambda d, h, n: (order(d, n), 0))
    grow_spec = BS((hp, GDN_GC, GDN_C), lambda d, h, n: (d * hb + h, order(d, n), 0))
    st_spec = BS((1, hp, GDN_GC, GDN_DK, GDN_DK), lambda d, h, n: (d, h, order(d, n), 0, 0))
    return qkv_spec, gb_spec, grow_spec, st_spec


def _lane_row(x):
    return jnp.broadcast_to(x, (1, LANES))


def _side_parts(side):
    if side is None:
        return [], [], [], [], []
    return [ANY] * len(side.ins), [ANY] * len(side.outs), list(side.outs), list(side.sems), list(side.ins)


def _side_run(side, refs, n_in, n_out, n_scr, first, last):
    if side is None:
        return
    ns, no, nm = len(side.ins), len(side.outs), len(side.sems)
    s_in = refs[n_in:n_in + ns]
    s_out = refs[n_in + ns + n_out:n_in + ns + n_out + no]
    s_sem = refs[len(refs) - nm:]

    @pl.when(first)
    def _():
        side.start(s_in, s_out, s_sem)

    @pl.when(last)
    def _():
        side.wait(s_in, s_out, s_sem)


def gdn_scan_fwd(qkv, gb, grow, name, side=None):
    S = qkv.shape[0]
    R = GDN_GC * GDN_C
    nblk = S // R
    nc = S // GDN_C
    hp = GDN_HP_FWD
    wd = hp * GDN_DK
    heads = range(hp)

    def order(d, n):
        return n + d * (nblk - 1 - 2 * n)

    qkv_spec, gb_spec, grow_spec, st_spec = _gdn_specs(S, nblk, order, hp)

    s_in, s_out, s_shape, s_scr_shapes, s_ops = _side_parts(side)
    hb = GDN_H // hp

    def body(*refs):
        q_ref, k_ref, v_ref, gb_ref, grow_ref = refs[:5]
        o_ref, st_ref = refs[5 + len(s_in):7 + len(s_in)]
        s_scr, u_scr, w_scr, qd_scr, kd_scr, in_scr, egl_scr = refs[7 + len(s_in) + len(s_out):14 + len(s_in) + len(s_out)]
        d = pl.program_id(0)
        h = pl.program_id(1)
        n = pl.program_id(2)
        _side_run(side, refs, 5, 2, 7, (d == 0) & (h == 0) & (n == 0), (d == 1) & (h == hb - 1) & (n == nblk - 1))

        @pl.when(n == 0)
        def _():
            s_scr[...] = jnp.zeros_like(s_scr)

        keys, ts = _block_terms(q_ref, k_ref, v_ref, gb_ref, grow_ref, d, h, hp)
        for (hh, ci), t in zip(keys, ts):
            u_scr[hh, ci] = t["u"]
            w_scr[hh, ci] = _bf(t["w"])
            qd_scr[hh, ci] = _bf(t["q_dec"])
            kd_scr[hh, ci] = _bf(t["k_dec"])
            in_scr[hh, ci] = _bf(jnp.where(t["incl"], t["qk"] * t["decay"], 0.0))
            egl_scr[hh, ci] = _lane_row(jnp.exp(t["glast"]))

        def chunk(cc, carry):
            ci = cc + d * (GDN_GC - 1 - 2 * cc)
            rows = pl.ds(pl.multiple_of(ci * GDN_C, GDN_C), GDN_C)
            sts = [s_scr[hh] for hh in heads]
            for hh in heads:
                st_ref[0, hh, ci] = sts[hh]
            sbs = [_bf(st) for st in sts]
            vns = [_bf(u_scr[hh, ci] - _dot(w_scr[hh, ci], sbs[hh])) for hh in heads]
            for hh in heads:
                s_scr[hh] = sts[hh] * egl_scr[hh, ci] + _dot_tn(kd_scr[hh, ci], vns[hh])
            for hh in heads:
                o_ref[0, rows, hh * GDN_DK:(hh + 1) * GDN_DK] = _dot(qd_scr[hh, ci], sbs[hh]) + _dot(in_scr[hh, ci], vns[hh])
            return carry

        lax.fori_loop(0, GDN_GC, chunk, 0)

    blk = (hp, GDN_GC, GDN_C, GDN_DK)
    return pl.pallas_call(
        body, grid=(2, GDN_H // hp, nblk),
        in_specs=[qkv_spec(0), qkv_spec(1), qkv_spec(2), gb_spec, grow_spec] + s_in,
        out_specs=[BS((1, R, wd), lambda d, h, n: (d, order(d, n), h)), st_spec] + s_out,
        out_shape=[SDS((2, S, GDN_H * GDN_DK), F32), SDS((2, GDN_H, nc, GDN_DK, GDN_DK), F32)] + s_shape,
        scratch_shapes=[pltpu.VMEM((hp, GDN_DK, GDN_DK), F32), pltpu.VMEM(blk, F32), pltpu.VMEM(blk, BF16),
                        pltpu.VMEM(blk, BF16), pltpu.VMEM(blk, BF16), pltpu.VMEM((hp, GDN_GC, GDN_C, GDN_C), BF16),
                        pltpu.VMEM((hp, GDN_GC, 1, LANES), F32)] + s_scr_shapes,
        name=name, compiler_params=_cp(("arbitrary", "arbitrary", "arbitrary")))(qkv, qkv, qkv, gb, grow, *s_ops)


def gdn_scan_bwd(qkv, gb, grow, states, do, name, side=None):
    S = qkv.shape[0]
    R = GDN_GC * GDN_C
    nblk = S // R
    hp = GDN_HP_BWD
    wd = hp * GDN_DK
    heads = range(hp)

    def order(d, n):
        return (nblk - 1 - n) - d * (nblk - 1 - 2 * n)

    qkv_spec, gb_spec, grow_spec, st_spec = _gdn_specs(S, nblk, order, hp)

    s_in, s_out, s_shape, s_scr_shapes, s_ops = _side_parts(side)
    hb = GDN_H // hp

    def body(*refs):
        q_ref, k_ref, v_ref, gb_ref, grow_ref, st_ref, do_ref = refs[:7]
        dqkv_ref, dgate_ref = refs[7 + len(s_in):9 + len(s_in)]
        (ds_scr, w_scr, kd_scr, dv1_scr, qtdo_scr, egl_scr, dsin_scr, dvn_scr,
         sdot_scr) = refs[9 + len(s_in) + len(s_out):18 + len(s_in) + len(s_out)]
        d = pl.program_id(0)
        h = pl.program_id(1)
        n = pl.program_id(2)
        _side_run(side, refs, 7, 2, 9, (d == 0) & (h == 0) & (n == 0), (d == 1) & (h == hb - 1) & (n == nblk - 1))

        @pl.when(n == 0)
        def _():
            ds_scr[...] = jnp.zeros_like(ds_scr)

        keys, ts = _block_terms(q_ref, k_ref, v_ref, gb_ref, grow_ref, d, h, hp)
        for (hh, ci), t in zip(keys, ts):
            rows = slice(ci * GDN_C, (ci + 1) * GDN_C)
            t["wb"] = _bf(t["w"])
            t["dob"] = _bf(do_ref[rows, hh * GDN_DK:(hh + 1) * GDN_DK])
            t["sb"] = _bf(st_ref[0, hh, ci])
        for (hh, ci), t in zip(keys, ts):
            t["vnb"] = _bf(t["u"] - _dot(t["wb"], t["sb"]))
            w_scr[hh, ci] = t["wb"]
            kd_scr[hh, ci] = _bf(t["k_dec"])
            dv1_scr[hh, ci] = _dot_tn(_bf(jnp.where(t["incl"], t["qk"] * t["decay"], 0.0)), t["dob"])
            qtdo_scr[hh, ci] = _dot_tn(_bf(t["q_dec"]), t["dob"])
            egl_scr[hh, ci] = _lane_row(jnp.exp(t["glast"]))

        def chunk(cc, carry):
            ci = (GDN_GC - 1 - cc) - d * (GDN_GC - 1 - 2 * cc)
            dsns = [ds_scr[hh] for hh in heads]
            dsbs = [_bf(x) for x in dsns]
            dvns = [dv1_scr[hh, ci] + _dot(kd_scr[hh, ci], dsbs[hh]) for hh in heads]
            for hh in heads:
                ds_scr[hh] = qtdo_scr[hh, ci] + egl_scr[hh, ci] * dsns[hh] - _dot_tn(w_scr[hh, ci], _bf(dvns[hh]))
            for hh in heads:
                dsin_scr[hh, ci] = dsbs[hh]
                dvn_scr[hh, ci] = dvns[hh]
                sd = jnp.sum(jnp.sum(st_ref[0, hh, ci] * dsns[hh], axis=1, keepdims=True), axis=0, keepdims=True)
                sdot_scr[hh, ci] = _lane_row(sd)
            return carry

        lax.fori_loop(0, GDN_GC, chunk, 0)

        for (hh, ci), t in zip(keys, ts):
            t["d_vnew"] = dvn_scr[hh, ci]
            t["dvb"] = _bf(t["d_vnew"])
            t["dsb"] = dsin_scr[hh, ci]
        for t in ts:
            t["d_intra"] = jnp.where(t["incl"], _dot_nt(t["dob"], t["vnb"]), 0.0)
            t["d_qdec"] = _dot_nt(t["dob"], t["sb"])
            t["d_kdec"] = _dot_nt(t["vnb"], t["dsb"])
            t["dw"] = -_dot_nt(t["dvb"], t["sb"])
        for t in ts:
            tts = _split(t["tinv"].T)
            t["d_ru"] = _dot3(tts, _split(t["d_vnew"]))
            t["d_rw"] = _dot3(tts, _split(t["dw"]))
        for t in ts:
            t["da"] = -jnp.where(t["strict"], _dot_nt(_bf(t["d_ru"]), _bf(t["u"])) + _dot_nt(_bf(t["d_rw"]), t["wb"]), 0.0)
        for (hh, ci), t in zip(keys, ts):
            rows = slice(ci * GDN_C, (ci + 1) * GDN_C)
            cols = slice(hh * GDN_DK, (hh + 1) * GDN_DK)
            q, k, v = q_ref[rows, cols], k_ref[rows, cols], t["v"]
            decay, kb, eg, ek, bcol = t["decay"], t["kb"], t["eg"], t["ek"], t["bcol"]
            d_ru, d_rw, da, d_intra, d_qdec, d_kdec = t["d_ru"], t["d_rw"], t["da"], t["d_intra"], t["d_qdec"], t["d_kdec"]
            kbf, qbf = _bf(k), _bf(q)
            dgl = egl_scr[hh, ci][:, 0:1] * sdot_scr[hh, ci][:, 0:1]
            dv = d_ru * bcol
            dbeta = jnp.sum(d_ru * v, axis=1, keepdims=True)
            dkb = d_rw * eg
            dg = jnp.sum(d_rw * kb, axis=1, keepdims=True) * eg
            dkk = _bf(da * decay)
            dqk = _bf(d_intra * decay)
            dkb = dkb + _dot(dkk, kbf)
            dk = _dot_tn(dkk, _bf(kb)) + _dot_tn(dqk, qbf)
            dq = _dot(dqk, kbf) + d_qdec * eg
            dd = (da * t["kk"] + d_intra * t["qk"]) * decay
            dg = dg + jnp.sum(dd, axis=1, keepdims=True) - jnp.sum(dd.T, axis=1, keepdims=True)
            dg = dg + jnp.sum(d_qdec * t["q_dec"], axis=1, keepdims=True)
            dk = dk + d_kdec * ek
            ee = jnp.sum(d_kdec * t["k_dec"], axis=1, keepdims=True)
            dg = dg - ee
            dgl = dgl + jnp.sum(ee, axis=0, keepdims=True)
            dk = dk + dkb * bcol
            dbeta = dbeta + jnp.sum(dkb * k, axis=1, keepdims=True)
            ridx = lax.broadcasted_iota(jnp.int32, (GDN_C, 1), 0)
            dg = dg + jnp.where(ridx == (GDN_C - 1) * (1 - d), dgl, 0.0)
            dqkv_ref[0, 0, rows, cols] = dq
            dqkv_ref[0, 1, rows, cols] = dk
            dqkv_ref[0, 2, rows, cols] = dv
            lane2 = lax.broadcasted_iota(jnp.int32, (GDN_C, 2), 1)
            dgate_ref[0, hh, rows, :] = jnp.where(lane2 == 0, dg, dbeta)

    blk = (hp, GDN_GC, GDN_C, GDN_DK)
    sq = (hp, GDN_GC, GDN_DK, GDN_DK)
    row = (hp, GDN_GC, 1, LANES)
    return pl.pallas_call(
        body, grid=(2, GDN_H // hp, nblk),
        in_specs=[qkv_spec(0), qkv_spec(1), qkv_spec(2), gb_spec, grow_spec, st_spec,
                  BS((R, wd), lambda d, h, n: (order(d, n), h))] + s_in,
        out_specs=[BS((1, 3, R, wd), lambda d, h, n: (d, 0, order(d, n), h)),
                   BS((1, hp, R, 2), lambda d, h, n: (d, h, order(d, n), 0))] + s_out,
        out_shape=[SDS((2, 3, S, GDN_H * GDN_DK), F32), SDS((2, GDN_H, S, 2), F32)] + s_shape,
        scratch_shapes=[pltpu.VMEM((hp, GDN_DK, GDN_DK), F32), pltpu.VMEM(blk, BF16), pltpu.VMEM(blk, BF16),
                        pltpu.VMEM(blk, F32), pltpu.VMEM(sq, F32), pltpu.VMEM(row, F32), pltpu.VMEM(sq, BF16),
                        pltpu.VMEM(blk, F32), pltpu.VMEM(row, F32)] + s_scr_shapes,
        name=name, compiler_params=_cp(("arbitrary", "arbitrary", "arbitrary")))(qkv, qkv, qkv, gb, grow, states, do, *s_ops)


def gdn_post_fwd(o2, proj, nw, name):
    S = proj.shape[0]
    tm = min(512, S)
    zoff = GDN_QKV // LANES

    def body(o_ref, z_ref, nw_ref, y_ref):
        o = o_ref[0] + o_ref[1]
        z = z_ref[...]
        r = lax.rsqrt(jnp.mean(o * o, axis=-1, keepdims=True) + RMS_EPS)
        y_ref[...] = (o * r * nw_ref[...] * (z * _sigmoid(z))).astype(BF16)

    return pl.pallas_call(
        body, grid=(S // tm, GDN_H),
        in_specs=[BS((2, tm, LANES), lambda i, h: (0, i, h)), BS((tm, LANES), lambda i, h: (i, zoff + h)),
                  BS((1, LANES), lambda i, h: (0, 0))],
        out_specs=BS((tm, LANES), lambda i, h: (i, h)),
        out_shape=SDS((S, GDN_H * GDN_DK), BF16), name=name, compiler_params=_cp(("parallel", "parallel")))(o2, proj, nw)


def gdn_post_bwd(o2, proj, nw, dy, name):
    S = proj.shape[0]
    tm = min(512, S)
    zoff = GDN_QKV // LANES

    def body(o_ref, z_ref, nw_ref, dy_ref, do_ref, dz_ref, dnw_ref):
        first = (pl.program_id(0) == 0) & (pl.program_id(1) == 0)
        o = o_ref[0] + o_ref[1]
        z = z_ref[...]
        nwv = nw_ref[...]
        dyv = dy_ref[...]
        r = lax.rsqrt(jnp.mean(o * o, axis=-1, keepdims=True) + RMS_EPS)
        n = o * r
        sg = _sigmoid(z)
        sz = z * sg
        dz_ref[...] = (dyv * n * nwv * (sg * (1.0 + z * (1.0 - sg)))).astype(BF16)
        dn = dyv * nwv * sz
        do_ref[...] = r * (dn - n * jnp.mean(dn * n, axis=-1, keepdims=True))
        part = jnp.sum(dyv * n * sz, axis=0, keepdims=True)

        @pl.when(first)
        def _():
            dnw_ref[...] = part

        @pl.when(jnp.logical_not(first))
        def _():
            dnw_ref[...] += part

    blk = BS((tm, LANES), lambda i, h: (i, h))
    return pl.pallas_call(
        body, grid=(S // tm, GDN_H),
        in_specs=[BS((2, tm, LANES), lambda i, h: (0, i, h)), BS((tm, LANES), lambda i, h: (i, zoff + h)),
                  BS((1, LANES), lambda i, h: (0, 0)), blk],
        out_specs=[blk, blk, BS((1, LANES), lambda i, h: (0, 0))],
        out_shape=[SDS((S, GDN_H * GDN_DK), F32), SDS((S, GDN_H * GDN_DK), BF16), SDS((1, LANES), F32)],
        name=name, compiler_params=_cp(("arbitrary", "arbitrary")))(o2, proj, nw, dy)


def _gate_prm(a_log, dt_bias):
    z = jnp.zeros((8, LANES), F32)
    z = z.at[0, :2 * GDN_H].set(a_log.reshape(-1))
    return z.at[1, :2 * GDN_H].set(dt_bias.reshape(-1))


def gdn_fwd(x, g, w_all, convw, a_log, dt_bias, nw, w_out, tag, side=None):
    S = x.shape[0]
    h, ht = rms_fwd(x, g, f"{tag}_rms")
    proj = mm(h, w_all, name=f"{tag}_proj", tn=1408)
    qkv = gdn_pre_fwd(proj, convw, f"{tag}_pre")
    prm = _gate_prm(a_log, dt_bias)
    gb = gdn_gate_fwd(proj, prm, f"{tag}_gate")
    grow = gb[:, :2 * GDN_H].T.reshape(2 * GDN_H, S // GDN_C, GDN_C)
    o2, states, *side_out = gdn_scan_fwd(qkv, gb, grow, f"{tag}_scan", side)
    y = gdn_post_fwd(o2, proj, nw, f"{tag}_post")
    xn = mm(y, w_out, name=f"{tag}_out", epi=_add, extras=(x,))
    return xn, (ht, proj, qkv, prm, gb, grow, o2, states, y), side_out


def gdn_bwd(x, g, w_all, convw, nw, w_out, saved, dx, dxb, tag, side=None):
    S = x.shape[0]
    ht, proj, qkv, prm, gb, grow, o2, states, y = saved
    dw_out = mm(y, dxb, name=f"{tag}_dwout", ta=True)
    dy = mm(dxb, w_out, name=f"{tag}_dy", tb=True)
    do, dz, dnw = gdn_post_bwd(o2, proj, nw, dy, f"{tag}_postb")
    dqkv, dgate, *side_out = gdn_scan_bwd(qkv, gb, grow, states, do, f"{tag}_scanb", side)
    dgb = jnp.transpose(dgate, (2, 3, 0, 1)).reshape(S, 4 * GDN_H)
    dgb = jnp.pad(dgb, ((0, 0), (0, LANES - 4 * GDN_H)))
    dab, dprm = gdn_gate_bwd(proj, prm, dgb, f"{tag}_gateb")
    dpq, dconvw = gdn_pre_bwd(proj, convw, dqkv, f"{tag}_preb")
    dproj = jnp.concatenate([dpq, dz, dab], axis=1)
    dw_all = mm(ht, dproj, name=f"{tag}_dwin", tn=384)
    dh = mm(dproj, w_all, name=f"{tag}_dh", tb=True)
    dx, dxb, dg = rms_bwd(x, g, dh, dx, f"{tag}_rmsb")
    da_log = dprm[0, :2 * GDN_H].reshape(2, GDN_H)
    ddt = dprm[1, :2 * GDN_H].reshape(2, GDN_H)
    return dx, dxb, dg, dw_all, dconvw, da_log, ddt, dnw, dw_out, side_out


def _rel_bucket_np(rel):
    nb = REL_BUCKETS // 2
    max_exact = nb // 2
    ret = np.where(rel > 0, nb, 0)
    n = np.abs(rel)
    nf = np.maximum(n, 1).astype(np.float32)
    large = max_exact + (np.log(nf / max_exact) / np.float32(math.log(REL_MAX_DIST / max_exact))
                         * (nb - max_exact)).astype(np.int32)
    large = np.minimum(large, nb - 1)
    return ret + np.where(n < max_exact, n, large)


def _toeplitz(f, rows, cols):
    period = rows + cols
    e = jnp.pad(f, ((0, 0), (0, period - f.shape[1])))
    y = jnp.tile(e, (1, rows))[:, :rows * (period - 1)]
    return y.reshape(f.shape[0], rows, period - 1)[:, :, :cols]


ATT_Q = DSWA_HALF
ATT_W = 3 * DSWA_HALF
ATT_TB = 1024
ATT_PAIRS = DSWA_HG // 2


def _bias_mats(rel_table, gi):
    _, dil = DSWA_CFG[gi]
    offs = np.arange(-DSWA_HALF, DSWA_HALF + 1)
    onehot = jnp.asarray(np.eye(REL_BUCKETS, dtype=np.float32)[_rel_bucket_np(offs * dil)])
    f = jnp.dot(onehot, rel_table, precision=HI)[:, gi * DSWA_HG:(gi + 1) * DSWA_HG].T
    bias = _toeplitz(f, ATT_Q, ATT_W)
    bias_t = jnp.transpose(_toeplitz(f[:, ::-1], ATT_Q, ATT_W), (0, 2, 1))
    return bias.reshape(ATT_PAIRS, 2, ATT_Q, ATT_W), bias_t.reshape(ATT_PAIRS, 2, ATT_W, ATT_Q)


def _att_specs(S, d, col):
    halo = DSWA_HALF * d
    per = ATT_TB // halo
    last = S // halo - 1
    cur = BS((ATT_TB, LANES), lambda p, tb: (tb, col(p)))
    prev = BS((halo, LANES), lambda p, tb: (jnp.maximum(tb * per - 1, 0), col(p)))
    nxt = BS((halo, LANES), lambda p, tb: (jnp.minimum((tb + 1) * per, last), col(p)))
    return prev, cur, nxt


def _att_specs3(S, d, lead):
    halo = DSWA_HALF * d
    per = ATT_TB // halo
    last = S // halo - 1
    cur = BS((1, ATT_TB, LANES), lambda p, tb: (lead(p), tb, 0))
    prev = BS((1, halo, LANES), lambda p, tb: (lead(p), jnp.maximum(tb * per - 1, 0), 0))
    nxt = BS((1, halo, LANES), lambda p, tb: (lead(p), jnp.minimum((tb + 1) * per, last), 0))
    return prev, cur, nxt


class _Pieces:
    def __init__(self, prev, cur, nxt, d, lead=None, cast=None):
        self.refs, self.d, self.lead, self.cast, self.cache = (prev, cur, nxt), d, lead, cast, {}
        self.halo = DSWA_HALF * d
        self.nsb = ATT_TB // self.halo

    def __call__(self, r, sb):
        if (r, sb) not in self.cache:
            ref = self.refs[0] if sb < 0 else self.refs[2] if sb >= self.nsb else self.refs[1]
            start = r + (self.halo * sb if 0 <= sb < self.nsb else 0)
            rows = pl.ds(start, ATT_Q, stride=self.d) if self.d > 1 else pl.ds(start, ATT_Q)
            v = ref[rows, :] if self.lead is None else ref[0, rows, :]
            self.cache[(r, sb)] = v if self.cast is None else v.astype(self.cast)
        return self.cache[(r, sb)]

    def window(self, r, sb):
        return jnp.concatenate([self(r, sb - 1), self(r, sb), self(r, sb + 1)], axis=0)


ATT_GROUP = 8


def _tile_groups(d, nsb):
    tiles = [(r, sb) for r in range(d) for sb in range(nsb)]
    return [tiles[i:i + ATT_GROUP] for i in range(0, len(tiles), ATT_GROUP)]


def _tile_rows(r, sb, d):
    start = r + DSWA_HALF * d * sb
    return pl.ds(start, ATT_Q, stride=d) if d > 1 else pl.ds(start, ATT_Q)


def _tile_valid(tb, r, sb, d, S, transposed):
    shape = (ATT_W, ATT_Q) if transposed else (ATT_Q, ATT_W)
    blk = lax.broadcasted_iota(jnp.int32, shape, 1 if transposed else 0)
    win = lax.broadcasted_iota(jnp.int32, shape, 0 if transposed else 1)
    tok = tb * ATT_TB + r + d * (DSWA_HALF * (sb - 1) + win)
    return (jnp.abs(win - DSWA_HALF - blk) <= DSWA_HALF) & (tok >= 0) & (tok < S)


def _head_masks():
    lane = lax.broadcasted_iota(jnp.int32, (1, LANES), 1)
    return [lane < DSWA_E, lane >= DSWA_E], lane


def attn_fwd(qkv, bias, gi, name):
    S = qkv.shape[0]
    d = DSWA_CFG[gi][1]
    nsb = ATT_TB // (DSWA_HALF * d)
    npair = DSWA_HEADS // 2
    q_spec = _att_specs(S, d, lambda p: gi * ATT_PAIRS + p)[1]
    k_specs = _att_specs(S, d, lambda p: npair + gi * ATT_PAIRS + p)
    v_specs = _att_specs(S, d, lambda p: 2 * npair + gi * ATT_PAIRS + p)

    def body(q_ref, kp, kc, kn, vp, vc, vn, b_ref, o_ref, lse_ref):
        tb = pl.program_id(1)
        masks, lane = _head_masks()
        kpc = _Pieces(kp, kc, kn, d, cast=BF16)
        vpc = _Pieces(vp, vc, vn, d, cast=BF16)
        scale = DSWA_E ** -0.5
        for grp in _tile_groups(d, nsb):
            rows = [_tile_rows(r, sb, d) for r, sb in grp]
            qs = [q_ref[rw, :] for rw in rows]
            kws = [kpc.window(r, sb) for r, sb in grp]
            vws = [vpc.window(r, sb) for r, sb in grp]
            valids = [_tile_valid(tb, r, sb, d, S, False) for r, sb in grp]
            both = [(t, hh) for t in range(len(grp)) for hh in range(2)]
            ss = [_dot_nt(_bf(jnp.where(masks[hh], qs[t], 0.0)), kws[t]) * scale + b_ref[0, hh] for t, hh in both]
            ss = [jnp.where(valids[t], s, NEG_INF) for (t, hh), s in zip(both, ss)]
            ms = [jnp.max(s, axis=-1, keepdims=True) for s in ss]
            ps = [jnp.exp(s - m) for s, m in zip(ss, ms)]
            ls = [jnp.sum(p, axis=-1, keepdims=True) for p in ps]
            os = [_dot(_bf(p / l), vws[t]) for (t, hh), p, l in zip(both, ps, ls)]
            for t, rw in enumerate(rows):
                o_ref[rw, :] = jnp.where(masks[0], os[2 * t], os[2 * t + 1])
                lse_ref[0, rw, :] = (jnp.where(lane == 0, ms[2 * t] + jnp.log(ls[2 * t]), 0.0)
                                     + jnp.where(lane == 1, ms[2 * t + 1] + jnp.log(ls[2 * t + 1]), 0.0))

    return pl.pallas_call(
        body, grid=(ATT_PAIRS, S // ATT_TB),
        in_specs=[q_spec, *k_specs, *v_specs, BS((1, 2, ATT_Q, ATT_W), lambda p, tb: (p, 0, 0, 0))],
        out_specs=[BS((ATT_TB, LANES), lambda p, tb: (tb, p)), BS((1, ATT_TB, LANES), lambda p, tb: (p, tb, 0))],
        out_shape=[SDS((S, DSWA_HG * DSWA_E), F32), SDS((ATT_PAIRS, S, LANES), F32)],
        name=name, compiler_params=_cp(("parallel", "parallel")))(qkv, qkv, qkv, qkv, qkv, qkv, qkv, bias)


def attn_bwd_q(qkv, bias, lse, do, dd, gi, name):
    S = qkv.shape[0]
    d = DSWA_CFG[gi][1]
    nsb = ATT_TB // (DSWA_HALF * d)
    npair = DSWA_HEADS // 2
    q_spec = _att_specs(S, d, lambda p: gi * ATT_PAIRS + p)[1]
    k_specs = _att_specs(S, d, lambda p: npair + gi * ATT_PAIRS + p)
    v_specs = _att_specs(S, d, lambda p: 2 * npair + gi * ATT_PAIRS + p)
    bspec = BS((1, 2, ATT_Q, ATT_W), lambda p, tb: (p, 0, 0, 0))

    def body(q_ref, kp, kc, kn, vp, vc, vn, b_ref, lse_ref, do_ref, dd_ref, dq_ref, db_ref):
        tb = pl.program_id(1)
        masks, lane = _head_masks()
        kpc = _Pieces(kp, kc, kn, d, cast=BF16)
        vpc = _Pieces(vp, vc, vn, d, cast=BF16)
        db = [jnp.zeros((ATT_Q, ATT_W), F32), jnp.zeros((ATT_Q, ATT_W), F32)]
        scale = DSWA_E ** -0.5
        for grp in _tile_groups(d, nsb):
            rows = [_tile_rows(r, sb, d) for r, sb in grp]
            qs = [q_ref[rw, :] for rw in rows]
            dos = [do_ref[0, rw, :] for rw in rows]
            lses = [lse_ref[0, rw, :] for rw in rows]
            dds = [dd_ref[0, 0, rw, :] for rw in rows]
            kws = [kpc.window(r, sb) for r, sb in grp]
            vws = [vpc.window(r, sb) for r, sb in grp]
            valids = [_tile_valid(tb, r, sb, d, S, False) for r, sb in grp]
            both = [(t, hh) for t in range(len(grp)) for hh in range(2)]
            ss = [_dot_nt(_bf(jnp.where(masks[hh], qs[t], 0.0)), kws[t]) * scale + b_ref[0, hh] for t, hh in both]
            dps = [_dot_nt(_bf(jnp.where(masks[hh], dos[t], 0.0)), vws[t]) for t, hh in both]
            ps = [jnp.exp(jnp.where(valids[t], s - lses[t][:, hh:hh + 1], NEG_INF)) for (t, hh), s in zip(both, ss)]
            dss = [p * (dp - dds[t][:, hh:hh + 1]) for (t, hh), p, dp in zip(both, ps, dps)]
            dqs = [_dot(_bf(ds), kws[t]) * scale for (t, hh), ds in zip(both, dss)]
            for t, rw in enumerate(rows):
                dq_ref[rw, :] = jnp.where(masks[0], dqs[2 * t], dqs[2 * t + 1])
                db[0] = db[0] + dss[2 * t]
                db[1] = db[1] + dss[2 * t + 1]

        @pl.when(tb == 0)
        def _():
            db_ref[0, 0] = db[0]
            db_ref[0, 1] = db[1]

        @pl.when(tb > 0)
        def _():
            db_ref[0, 0] += db[0]
            db_ref[0, 1] += db[1]

    return pl.pallas_call(
        body, grid=(ATT_PAIRS, S // ATT_TB),
        in_specs=[q_spec, *k_specs, *v_specs, bspec, BS((1, ATT_TB, LANES), lambda p, tb: (p, tb, 0)),
                  BS((1, ATT_TB, LANES), lambda p, tb: (gi, tb, p)), BS((1, 1, ATT_TB, LANES), lambda p, tb: (gi, p, tb, 0))],
        out_specs=[BS((ATT_TB, LANES), lambda p, tb: (tb, p)), bspec],
        out_shape=[SDS((S, DSWA_HG * DSWA_E), F32), SDS((ATT_PAIRS, 2, ATT_Q, ATT_W), F32)],
        name=name, compiler_params=_cp(("parallel", "arbitrary")))(qkv, qkv, qkv, qkv, qkv, qkv, qkv, bias, lse, do, dd)


def attn_bwd_kv(qkv, bias_t, lse, do, dd, gi, name):
    S = qkv.shape[0]
    d = DSWA_CFG[gi][1]
    nsb = ATT_TB // (DSWA_HALF * d)
    npair = DSWA_HEADS // 2
    q_specs = _att_specs(S, d, lambda p: gi * ATT_PAIRS + p)
    k_spec = _att_specs(S, d, lambda p: npair + gi * ATT_PAIRS + p)[1]
    v_spec = _att_specs(S, d, lambda p: 2 * npair + gi * ATT_PAIRS + p)[1]
    halo = DSWA_HALF * d
    per = ATT_TB // halo
    last = S // halo - 1

    def do_spec(rows, blk):
        return BS((1, rows, LANES), lambda p, tb: (gi, blk(tb), p))

    def dd_spec(rows, blk):
        return BS((1, 1, rows, LANES), lambda p, tb: (gi, p, blk(tb), 0))

    blks = [(halo, lambda tb: jnp.maximum(tb * per - 1, 0)), (ATT_TB, lambda tb: tb),
            (halo, lambda tb: jnp.minimum((tb + 1) * per, last))]
    do_specs = [do_spec(*b) for b in blks]
    dd_specs = [dd_spec(*b) for b in blks]
    lse_specs = _att_specs3(S, d, lambda p: p)

    class _Lead4:
        def __init__(self, ref):
            self.ref = ref

        def __getitem__(self, idx):
            return self.ref[(0,) + idx]

    def body(k_ref, v_ref, qp, qc, qn, dop, doc, don, lp, lc, ln, ddp, ddc, ddn, b_ref, dk_ref, dv_ref):
        tb = pl.program_id(1)
        masks, lane = _head_masks()
        qpc = _Pieces(qp, qc, qn, d)
        dopc = _Pieces(dop, doc, don, d, lead=True)
        lpc = _Pieces(lp, lc, ln, d, lead=True)
        ddpc = _Pieces(_Lead4(ddp), _Lead4(ddc), _Lead4(ddn), d, lead=True)
        scale = DSWA_E ** -0.5
        for grp in _tile_groups(d, nsb):
            rows = [_tile_rows(r, sb, d) for r, sb in grp]
            kcs = [_bf(k_ref[rw, :]) for rw in rows]
            vcs = [_bf(v_ref[rw, :]) for rw in rows]
            qws = [qpc.window(r, sb) for r, sb in grp]
            dows = [dopc.window(r, sb) for r, sb in grp]
            lws = [lpc.window(r, sb) for r, sb in grp]
            ddws = [ddpc.window(r, sb) for r, sb in grp]
            qwbs = [_bf(x) for x in qws]
            dowbs = [_bf(x) for x in dows]
            valids = [_tile_valid(tb, r, sb, d, S, True) for r, sb in grp]
            both = [(t, hh) for t in range(len(grp)) for hh in range(2)]
            ss = [_dot_nt(_bf(jnp.where(masks[hh], qws[t], 0.0)), kcs[t]) * scale + b_ref[0, hh] for t, hh in both]
            dps = [_dot_nt(_bf(jnp.where(masks[hh], dows[t], 0.0)), vcs[t]) for t, hh in both]
            ps = [jnp.exp(jnp.where(valids[t], s - lws[t][:, hh:hh + 1], NEG_INF)) for (t, hh), s in zip(both, ss)]
            dvs = [_dot_tn(_bf(p), dowbs[t]) for (t, hh), p in zip(both, ps)]
            dss = [p * (dp - ddws[t][:, hh:hh + 1]) for (t, hh), p, dp in zip(both, ps, dps)]
            dks = [_dot_tn(_bf(ds), qwbs[t]) * scale for (t, hh), ds in zip(both, dss)]
            for t, rw in enumerate(rows):
                dk_ref[rw, :] = jnp.where(masks[0], dks[2 * t], dks[2 * t + 1])
                dv_ref[rw, :] = jnp.where(masks[0], dvs[2 * t], dvs[2 * t + 1])

    out = BS((ATT_TB, LANES), lambda p, tb: (tb, p))
    return pl.pallas_call(
        body, grid=(ATT_PAIRS, S // ATT_TB),
        in_specs=[k_spec, v_spec, *q_specs, *do_specs, *lse_specs, *dd_specs,
                  BS((1, 2, ATT_W, ATT_Q), lambda p, tb: (p, 0, 0, 0))],
        out_specs=[out, out],
        out_shape=[SDS((S, DSWA_HG * DSWA_E), F32), SDS((S, DSWA_HG * DSWA_E), F32)],
        name=name, compiler_params=_cp(("parallel", "parallel")))(
            qkv, qkv, qkv, qkv, qkv, do, do, do, lse, lse, lse, dd, dd, dd, bias_t)


def _pair_alphas(lses):
    m = jnp.maximum(jnp.maximum(lses[0], lses[1]), lses[2])
    e = [jnp.exp(t - m) for t in lses]
    tot = e[0] + e[1] + e[2]
    return [t / tot for t in e]


def _pair_expand(a, lane):
    return jnp.where(lane < DSWA_E, a[:, 0:1], a[:, 1:2])


def combine_fwd(o_raw, lse, name):
    S = o_raw.shape[0]
    tm = min(1024, S)

    def body(o_ref, l_ref, y_ref):
        g = pl.program_id(2)
        lane = lax.broadcasted_iota(jnp.int32, (1, LANES), 1)
        alphas = _pair_alphas([l_ref[0, 0], l_ref[1, 0], l_ref[2, 0]])
        a = jnp.where(g == 0, alphas[0], jnp.where(g == 1, alphas[1], alphas[2]))
        y_ref[...] = (o_ref[...] * _pair_expand(a, lane)).astype(BF16)

    blk = BS((tm, LANES), lambda i, p, g: (i, g * ATT_PAIRS + p))
    return pl.pallas_call(
        body, grid=(S // tm, ATT_PAIRS, 3),
        in_specs=[blk, BS((3, 1, tm, LANES), lambda i, p, g: (0, p, i, 0))], out_specs=blk,
        out_shape=SDS((S, DSWA_W), BF16), name=name, compiler_params=_cp(("parallel", "parallel", "parallel")))(o_raw, lse)


def combine_bwd(o_raw, lse, dy, name):
    S = o_raw.shape[0]
    tm = min(512, S)

    def body(o0, o1, o2, l_ref, d0, d1, d2, do_ref, dd_ref):
        lane = lax.broadcasted_iota(jnp.int32, (1, LANES), 1)
        alphas = _pair_alphas([l_ref[0, 0], l_ref[1, 0], l_ref[2, 0]])
        c = jnp.zeros((tm, LANES), F32)
        for g, (o_ref, dy_ref) in enumerate(((o0, d0), (o1, d1), (o2, d2))):
            dyv = dy_ref[...]
            do_ref[g] = dyv * _pair_expand(alphas[g], lane)
            prod = o_ref[...] * dyv
            dal = (jnp.where(lane == 0, jnp.sum(jnp.where(lane < DSWA_E, prod, 0.0), axis=1, keepdims=True), 0.0)
                   + jnp.where(lane == 1, jnp.sum(jnp.where(lane >= DSWA_E, prod, 0.0), axis=1, keepdims=True), 0.0))
            c = c + alphas[g] * dal
        for g in range(3):
            dd_ref[g, 0] = alphas[g] * c

    def col(g):
        return BS((tm, LANES), lambda i, p: (i, g * ATT_PAIRS + p))

    return pl.pallas_call(
        body, grid=(S // tm, ATT_PAIRS),
        in_specs=[col(0), col(1), col(2), BS((3, 1, tm, LANES), lambda i, p: (0, p, i, 0)), col(0), col(1), col(2)],
        out_specs=[BS((3, tm, LANES), lambda i, p: (0, i, p)), BS((3, 1, tm, LANES), lambda i, p: (0, p, i, 0))],
        out_shape=[SDS((3, S, DSWA_HG * DSWA_E), F32), SDS((3, ATT_PAIRS, S, LANES), F32)],
        name=name, compiler_params=_cp(("parallel", "parallel")))(o_raw, o_raw, o_raw, lse, dy, dy, dy)


def dswa_fwd(x, g, w_in, w_out, rel_table, tag):
    h, ht = rms_fwd(x, g, f"{tag}_rms")
    qkv = mm(h, w_in, name=f"{tag}_qkv", tn=1152)
    outs, lses = [], []
    for gi in range(3):
        bias, _ = _bias_mats(rel_table, gi)
        o, lse = attn_fwd(qkv, bias, gi, f"{tag}_att{gi}")
        outs.append(o)
        lses.append(lse)
    o_raw = jnp.concatenate(outs, axis=1)
    lse = jnp.stack(lses)
    y = combine_fwd(o_raw, lse, f"{tag}_comb")
    xn = mm(y, w_out, name=f"{tag}_out", epi=_add, extras=(x,))
    return xn, (ht, qkv, o_raw, lse, y)


def dswa_bwd(x, g, w_in, w_out, rel_table, saved, dx, dxb, tag):
    ht, qkv, o_raw, lse, y = saved
    dw_out = mm(y, dxb, name=f"{tag}_dwout", ta=True, tm=384)
    dy = mm(dxb, w_out, name=f"{tag}_dy", tb=True, tn=384)
    do_raw, dd = combine_bwd(o_raw, lse, dy, f"{tag}_combb")
    dqs, dks, dvs = [], [], []
    drel = jnp.zeros_like(rel_table)
    for gi in range(3):
        (bias, bias_t), bias_vjp = jax.vjp(lambda tbl: _bias_mats(tbl, gi), rel_table)
        dq, dbias = attn_bwd_q(qkv, bias, lse[gi], do_raw, dd, gi, f"{tag}_attq{gi}")
        dk, dv = attn_bwd_kv(qkv, bias_t, lse[gi], do_raw, dd, gi, f"{tag}_attkv{gi}")
        drel = drel + bias_vjp((dbias, jnp.zeros_like(bias_t)))[0]
        dqs.append(dq)
        dks.append(dk)
        dvs.append(dv)
    dqkv = jnp.concatenate(dqs + dks + dvs, axis=1).astype(BF16)
    dw_in = mm(ht, dqkv, name=f"{tag}_dwin", tn=384)
    dh = mm(dqkv, w_in, name=f"{tag}_dh", tb=True)
    dx, dxb, dg = rms_bwd(x, g, dh, dx, f"{tag}_rmsb")
    return dx, dxb, dg, dw_in, dw_out, drel


def adamw(w, g, m, v, name):
    shape = w.shape
    last = shape[-1]
    w2, g2, m2, v2 = (t.reshape(-1, last) for t in (w, g, m, v))
    rows = w2.shape[0]
    tr = rows
    if rows > 512:
        tr = next(t for t in (512, 256, 192, 128, 64, 8) if rows % t == 0)
    c1 = 1.0 / (1.0 - ADAM_B1 ** ADAM_STEP)
    c2 = 1.0 / (1.0 - ADAM_B2 ** ADAM_STEP)

    def body(w_ref, g_ref, m_ref, v_ref, d_ref, nm_ref, nv_ref):
        gv = g_ref[...]
        nm = ADAM_B1 * m_ref[...] + (1.0 - ADAM_B1) * gv
        nv = ADAM_B2 * v_ref[...] + (1.0 - ADAM_B2) * (gv * gv)
        nm_ref[...] = nm
        nv_ref[...] = nv
        d_ref[...] = -ADAM_LR * ((nm * c1) / (jnp.sqrt(nv * c2) + ADAM_EPS) + ADAM_WD * w_ref[...])

    spec = BS((tr, last), lambda i: (i, 0))
    outs = pl.pallas_call(
        body, grid=(rows // tr,), in_specs=[spec] * 4, out_specs=[spec] * 3,
        out_shape=[SDS((rows, last), F32)] * 3, name=name, compiler_params=_cp(("parallel",)))(w2, g2, m2, v2)
    return tuple(o.reshape(shape) for o in outs)


def _place():
    x, y, c = lax.axis_index("x"), lax.axis_index("y"), lax.axis_index("c")
    chips = [(1 - x, y), (x, 1 - y), (1 - x, 1 - y)]
    return x, y, c, chips


def _rcopy(src, dst, ssem, rsem, dev):
    return pltpu.make_async_remote_copy(src_ref=src, dst_ref=dst, send_sem=ssem, recv_sem=rsem, device_id=dev,
                                        device_id_type=MESH)


class SideJob(NamedTuple):
    ins: list
    outs: list
    sems: list
    start: Callable
    wait: Callable


def _job(ins, outs, sems, copies):
    def start(in_refs, out_refs, sem_refs):
        for cp in copies(in_refs, out_refs, sem_refs):
            cp.start()

    def wait(in_refs, out_refs, sem_refs):
        for cp in copies(in_refs, out_refs, sem_refs):
            cp.wait()

    return SideJob(list(ins), list(outs), list(sems), start, wait)


def gather_job(packs, halved):
    n = len(packs)
    dma = pltpu.SemaphoreType.DMA

    def copies(in_refs, out_refs, sems):
        ssem, rsem = sems
        x, y, c, chips = _place()
        jme = 2 * x + y
        cps = []
        for i, (p_ref, f_ref) in enumerate(zip(in_refs, out_refs)):
            rows = p_ref.shape[0]
            mine = pl.ds(c * (rows // 2), rows // 2) if halved[i] else pl.ds(0, rows)
            for r, (cx, cy) in enumerate(chips):
                cps.append(_rcopy(p_ref.at[mine], f_ref.at[jme, mine], ssem.at[i, r], rsem.at[i, r], (cx, cy, c)))
        return cps

    return _job(packs, [SDS((4,) + p.shape, p.dtype) for p in packs], [dma((n, 3)), dma((n, 3))], copies)


def chip_exchange_job(parts):
    n = len(parts)
    dma = pltpu.SemaphoreType.DMA

    def copies(in_refs, out_refs, sems):
        ssem, rsem = sems
        x, y, c, chips = _place()
        cps = []
        for i, (p_ref, r_ref) in enumerate(zip(in_refs, out_refs)):
            for r, (cx, cy) in enumerate(chips):
                cps.append(_rcopy(p_ref.at[2 * cx + cy], r_ref.at[r], ssem.at[i, r], rsem.at[i, r], (cx, cy, c)))
        return cps

    return _job(parts, [SDS((3,) + p.shape[1:], p.dtype) for p in parts], [dma((n, 3)), dma((n, 3))], copies)


def run_job(job, name):
    ni, no = len(job.ins), len(job.outs)

    def body(*refs):
        job.start(refs[:ni], refs[ni:ni + no], refs[ni + no:])
        job.wait(refs[:ni], refs[ni:ni + no], refs[ni + no:])

    return pl.pallas_call(
        body, in_specs=[ANY] * ni, out_specs=[ANY] * no, out_shape=job.outs, scratch_shapes=job.sems, name=name,
        compiler_params=pltpu.CompilerParams(has_side_effects=True))(*job.ins)


def forward_to_sibling(fulls, name):
    n = len(fulls)

    def body(*refs):
        in_refs, out_refs, (ssem, rsem) = refs[:n], refs[n:2 * n], refs[2 * n:]
        x, y, c, chips = _place()
        cps = []
        for i in range(n):
            half = in_refs[i].shape[1] // 2
            for r, (cx, cy) in enumerate(chips):
                piece = (2 * cx + cy, pl.ds(c * half, half))
                cps.append(_rcopy(in_refs[i].at[piece], out_refs[i].at[piece], ssem.at[i, r], rsem.at[i, r], (x, y, 1 - c)))
        for cp in cps:
            cp.start()
        for cp in cps:
            cp.wait()

    dma = pltpu.SemaphoreType.DMA
    return pl.pallas_call(
        body, in_specs=[ANY] * n, out_specs=[ANY] * n, out_shape=[SDS(f.shape, f.dtype) for f in fulls],
        scratch_shapes=[dma((n, 3)), dma((n, 3))], input_output_aliases={i: i for i in range(n)}, name=name,
        compiler_params=pltpu.CompilerParams(has_side_effects=True))(*fulls)


def rs_sibling_exchange(gpack, name):
    _, rows, W = gpack.shape
    half = rows // 2

    def body(g_ref, r_ref, ssem, rsem):
        x, y, c, _ = _place()
        cps = [_rcopy(g_ref.at[j, pl.ds((1 - c) * half, half)], r_ref.at[j], ssem.at[j], rsem.at[j], (x, y, 1 - c))
               for j in range(4)]
        for cp in cps:
            cp.start()
        for cp in cps:
            cp.wait()

    dma = pltpu.SemaphoreType.DMA
    return pl.pallas_call(
        body, in_specs=[ANY], out_specs=ANY, out_shape=SDS((4, half, W), gpack.dtype),
        scratch_shapes=[dma((4,)), dma((4,))], name=name,
        compiler_params=pltpu.CompilerParams(has_side_effects=True))(gpack)


def _div_tile(n, limit):
    return next(t for t in range(limit - limit % 16, 0, -16) if n % t == 0)


def rs_add_sibling(gpack, recv, cidx, name, out_dtype=F32):
    _, rows, W = gpack.shape
    half = rows // 2
    tr = _div_tile(half, 1024)
    nb = half // tr

    def body(c_ref, g_ref, r_ref, o_ref):
        o_ref[...] = (g_ref[...].astype(F32) + r_ref[...].astype(F32)).astype(o_ref.dtype)

    gs = pltpu.PrefetchScalarGridSpec(
        num_scalar_prefetch=1, grid=(4, nb),
        in_specs=[BS((1, tr, W), lambda j, i, c: (j, c[0] * nb + i, 0)), BS((1, tr, W), lambda j, i, c: (j, i, 0))],
        out_specs=BS((1, tr, W), lambda j, i, c: (j, i, 0)))
    return pl.pallas_call(body, grid_spec=gs, out_shape=SDS((4, half, W), out_dtype), name=name,
                          compiler_params=_cp(("parallel", "parallel")))(cidx, gpack, recv)


def rs_add_chips(recv, part, place, name):
    _, half, W = recv.shape
    tr = _div_tile(half, 640)
    nb = half // tr

    def body(x_ref, y_ref, c_ref, r_ref, own_ref, o_ref):
        r0, r1, r2, own = (t.astype(F32) for t in (r_ref[0], r_ref[1], r_ref[2], own_ref[0]))
        o_ref[...] = ((r0 + r1) + r2) + own

    gs = pltpu.PrefetchScalarGridSpec(
        num_scalar_prefetch=3, grid=(nb,),
        in_specs=[BS((3, tr, W), lambda i, x, y, c: (0, i, 0)), BS((1, tr, W), lambda i, x, y, c: (2 * x[0] + y[0], i, 0))],
        out_specs=BS((tr, W), lambda i, x, y, c: (c[0] * nb + i, 0)))
    return pl.pallas_call(body, grid_spec=gs, out_shape=SDS((2 * half, W), F32), name=name,
                          compiler_params=_cp(("parallel",)))(*place, recv, part)


def rs_sibling_share(gsh, name):
    rows, W = gsh.shape
    half = rows // 2

    def body(g_ref, o_ref, ssem, rsem):
        x, y, c, _ = _place()
        mine = pl.ds(c * half, half)
        cp = _rcopy(g_ref.at[mine], o_ref.at[mine], ssem, rsem, (x, y, 1 - c))
        cp.start()
        cp.wait()

    dma = pltpu.SemaphoreType.DMA
    return pl.pallas_call(
        body, in_specs=[ANY], out_specs=ANY, out_shape=SDS(gsh.shape, gsh.dtype),
        scratch_shapes=[dma, dma], input_output_aliases={0: 0}, name=name,
        compiler_params=pltpu.CompilerParams(has_side_effects=True))(gsh)


def allreduce_small(pack):
    R = pack.shape[0]

    def body(p_ref, o_ref, all_ref, ssem, rsem):
        x, y, c, _ = _place()
        me = 4 * x + 2 * y + c
        all_ref[me] = p_ref[...]
        cps = []
        for m in range(1, 8):
            peer = (1 - x if m & 4 else x, 1 - y if m & 2 else y, 1 - c if m & 1 else c)
            cp = _rcopy(p_ref, all_ref.at[me], ssem.at[m - 1], rsem.at[m - 1], peer)
            cp.start()
            cps.append(cp)
        for cp in cps:
            cp.wait()
        acc = all_ref[0]
        for i in range(1, 8):
            acc = acc + all_ref[i]
        o_ref[...] = acc

    dma = pltpu.SemaphoreType.DMA
    vm = BS(memory_space=pltpu.VMEM)
    return pl.pallas_call(
        body, in_specs=[vm], out_specs=vm, out_shape=SDS(pack.shape, F32),
        scratch_shapes=[pltpu.VMEM((8, R, LANES), F32), dma((7,)), dma((7,))], name="allreduce_small",
        compiler_params=pltpu.CompilerParams(has_side_effects=True))(pack)


PACK_W = 1024
PACK_ALIGN = 32


def _layer_entries(l):
    if l % 2 == 0:
        mixer = [("gdn_w_in", l // 2, D_MODEL, GDN_IN // 4, True), ("gdn_w_out", l // 2, D_MODEL // 4, D_MODEL, False)]
    else:
        mixer = [("dswa_w_in", l // 2, D_MODEL, 3 * DSWA_W // 4, True), ("dswa_w_out", l // 2, DSWA_W // 4, D_MODEL, False)]
    return mixer + [("mlp_w1", l, D_MODEL, D_FF // 4, True), ("mlp_w2", l, D_FF // 4, D_MODEL, False)]


def _entries_offsets(entries):
    offs = [int(o) for o in np.cumsum([0] + [r * c // PACK_W for (_, _, r, c, _) in entries])]
    return offs, -(-offs[-1] // PACK_ALIGN) * PACK_ALIGN


def _layer_offsets(l):
    return _entries_offsets(_layer_entries(l))


def _pack_layer(l, shards, dtype):
    offs, total = _layer_offsets(l)
    parts = [shards[name][li].astype(dtype).reshape(-1, PACK_W) for (name, li, _, _, _) in _layer_entries(l)]
    parts.append(jnp.zeros((total - offs[-1], PACK_W), dtype))
    return jnp.concatenate(parts, axis=0)


def _unpack_layer(l, full, own, jme):
    offs, _ = _layer_offsets(l)
    mats = []
    for e, (_, _, r, c, by_col) in enumerate(_layer_entries(l)):
        mine = own[offs[e]:offs[e + 1]]
        sh = [jnp.where(jme == j, mine, full[j, offs[e]:offs[e + 1]]).reshape(r, c) for j in range(4)]
        mats.append(jnp.concatenate(sh, axis=1 if by_col else 0))
    return mats


def _pack_grads(entries, grads):
    offs, total = _entries_offsets(entries)
    per_chip = []
    for j in range(4):
        parts = []
        for g, (_, _, r, c, by_col) in zip(grads, entries):
            sh = g[:, c * j:c * (j + 1)] if by_col else g[r * j:r * (j + 1), :]
            parts.append(sh.astype(BF16).reshape(-1, PACK_W))
        parts.append(jnp.zeros((total - offs[-1], PACK_W), BF16))
        per_chip.append(jnp.concatenate(parts, axis=0))
    return jnp.stack(per_chip)


def _unpack_shard_grads(units):
    out = {}
    for entries, gsh in units:
        offs, _ = _entries_offsets(entries)
        for e, (name, _, r, c, _) in enumerate(entries):
            out.setdefault(name, []).append(gsh[offs[e]:offs[e + 1]].reshape(r, c))
    return {k: jnp.stack(v) for k, v in out.items()}


def _flat_pad(t, mult=8 * LANES):
    f = t.reshape(-1)
    return jnp.pad(f, (0, (-f.shape[0]) % mult))


def kernel(x, norm_mix, norm_mlp, norm_final, rel_bias, gdn_w_in, gdn_conv_w, gdn_a_log, gdn_dt_bias, gdn_norm_w, gdn_w_out, dswa_w_in, dswa_w_out, mlp_w1, mlp_w2, loss_target, m_norm_mix, m_norm_mlp, m_norm_final, m_rel_bias, m_gdn_w_in, m_gdn_conv_w, m_gdn_a_log, m_gdn_dt_bias, m_gdn_norm_w, m_gdn_w_out, m_dswa_w_in, m_dswa_w_out, m_mlp_w1, m_mlp_w2, v_norm_mix, v_norm_mlp, v_norm_final, v_rel_bias, v_gdn_w_in, v_gdn_conv_w, v_gdn_a_log, v_gdn_dt_bias, v_gdn_norm_w, v_gdn_w_out, v_dswa_w_in, v_dswa_w_out, v_mlp_w1, v_mlp_w2):
    xi, yi, ci = lax.axis_index("x"), lax.axis_index("y"), lax.axis_index("c")
    jme = 2 * xi + yi
    big = dict(gdn_w_in=gdn_w_in, gdn_w_out=gdn_w_out, dswa_w_in=dswa_w_in, dswa_w_out=dswa_w_out, mlp_w1=mlp_w1, mlp_w2=mlp_w2)
    n_gdn = gdn_w_in.shape[0]
    conv_cols = gdn_conv_w.shape[-1]

    packs = [_pack_layer(l, big, BF16) for l in range(DEPTH)]
    convp = jnp.pad(gdn_conv_w.reshape(n_gdn * GDN_CONV, conv_cols), ((0, 16 - n_gdn * GDN_CONV), (0, 0)))
    raw0, cfull = run_job(gather_job([packs[0], convp], [True, False]), "gather_l0")
    fulls = {0: forward_to_sibling([raw0], "forward_l0")[0]}
    cfull = jnp.where((jnp.arange(4) == jme)[:, None, None], convp[None], cfull)
    conv_all = jnp.transpose(cfull[:, :n_gdn * GDN_CONV], (1, 0, 2)).reshape(n_gdn, GDN_CONV, 4 * conv_cols)
    conv_all = jnp.pad(conv_all, ((0, 0), (0, 8 - GDN_CONV), (0, 0)))
    fwd_jobs = {0: [1, 2], 2: [3]}

    xs = x[0]
    saved = []
    for l in range(DEPTH):
        w_in, w_out, w1, w2 = _unpack_layer(l, fulls[l], packs[l], jme)
        gm, gp = norm_mix[l][None], norm_mlp[l][None]
        a = l // 2
        if l % 2 == 0:
            w_in = jnp.pad(w_in, ((0, 0), (0, GDN_INP - GDN_IN)))
            job = gather_job([packs[t] for t in fwd_jobs[l]], [True] * len(fwd_jobs[l]))
            x_mid, sv, raws = gdn_fwd(xs, gm, w_in, conv_all[a], gdn_a_log[a], gdn_dt_bias[a], gdn_norm_w[a][None], w_out,
                                      f"l{l}_gdn", job)
            for t, f in zip(fwd_jobs[l], forward_to_sibling(raws, f"forward_from_l{l}")):
                fulls[t] = f
        else:
            x_mid, sv = dswa_fwd(xs, gm, w_in, w_out, rel_bias, f"l{l}_att")
        x_out, sv2 = mlp_fwd(x_mid, gp, w1, w2, f"l{l}_mlp")
        saved.append((xs, x_mid, (w_in, w_out, w1, w2), sv, sv2))
        xs = x_out

    cidx = ci.astype(jnp.int32).reshape(1)
    place = [t.astype(jnp.int32).reshape(1) for t in (xi, yi, ci)]
    units = {"0a": _layer_entries(0)[:2], "0b": _layer_entries(0)[2:], **{str(l): _layer_entries(l) for l in (1, 2, 3)}}

    def chip_partial(u, grads):
        gpack = _pack_grads(units[u], grads)
        return rs_add_sibling(gpack, rs_sibling_exchange(gpack, f"rs_sibling_{u}"), cidx, f"rs_add_sibling_{u}", BF16)

    def finish(u, recv):
        return rs_sibling_share(rs_add_chips(recv, parts[u], place, f"rs_add_chips_{u}"), f"rs_share_{u}")

    loss_part, dx, dxb, d_final = loss_head(xs, norm_final[None], loss_target[0], "loss_head")
    d_mix, d_mlp = [None] * DEPTH, [None] * DEPTH
    d_conv, d_alog, d_dt, d_nw = [None] * n_gdn, [None] * n_gdn, [None] * n_gdn, [None] * n_gdn
    d_rel = jnp.zeros_like(rel_bias)
    parts, gshs = {}, {}
    bwd_jobs = {2: ["3"], 0: ["2", "1", "0b"]}
    for l in reversed(range(DEPTH)):
        x_in, x_mid, (w_in, w_out, w1, w2), sv, sv2 = saved[l]
        gm, gp = norm_mix[l][None], norm_mlp[l][None]
        a = l // 2
        dx, dxb, d_mlp[l], dw1, dw2 = mlp_bwd(x_mid, gp, w1, w2, sv2, dx, dxb, f"l{l}_mlp")
        if l == 0:
            parts["0b"] = chip_partial("0b", [dw1, dw2])
        if l % 2 == 0:
            job = chip_exchange_job([parts[u] for u in bwd_jobs[l]])
            dx, dxb, d_mix[l], dw_all, d_conv[a], d_alog[a], d_dt[a], d_nw[a], dwo, recvs = gdn_bwd(
                x_in, gm, w_in, conv_all[a], gdn_norm_w[a][None], w_out, sv, dx, dxb, f"l{l}_gdn", job)
            for u, rv in zip(bwd_jobs[l], recvs):
                gshs[u] = finish(u, rv)
            dwi = dw_all[:, :GDN_IN]
        else:
            dx, dxb, d_mix[l], dwi, dwo, drel = dswa_bwd(x_in, gm, w_in, w_out, rel_bias, sv, dx, dxb, f"l{l}_att")
            d_rel = d_rel + drel
        if l == 0:
            parts["0a"] = chip_partial("0a", [dwi, dwo])
        else:
            parts[str(l)] = chip_partial(str(l), [dwi, dwo, dw1, dw2])
    gshs["0a"] = finish("0a", run_job(chip_exchange_job([parts["0a"]]), "rs_chip_exchange_0a")[0])
    gbig = _unpack_shard_grads([(units[u], gshs[u]) for u in ("0a", "0b", "1", "2", "3")])

    small = [jnp.concatenate(d_mix, axis=0), jnp.concatenate(d_mlp, axis=0), d_final, d_rel,
             jnp.stack(d_conv), jnp.stack(d_alog), jnp.stack(d_dt), jnp.concatenate(d_nw, axis=0)]
    flat = [_flat_pad(t) for t in small]
    sizes = [f.shape[0] for f in flat]
    red = allreduce_small(jnp.concatenate(flat).reshape(-1, LANES)).reshape(-1)
    offs = np.cumsum([0] + sizes)
    red = [red[offs[i]:offs[i] + small[i].size].reshape(small[i].shape) for i in range(len(small))]
    g_conv_all = red[4][:, :GDN_CONV].reshape(n_gdn, GDN_CONV, 1, 4 * conv_cols)
    g_conv = lax.dynamic_slice_in_dim(g_conv_all, jme * conv_cols, conv_cols, axis=3)
    g = dict(norm_mix=red[0], norm_mlp=red[1], norm_final=red[2].reshape(norm_final.shape), rel_bias=red[3],
             gdn_conv_w=g_conv, gdn_a_log=red[5], gdn_dt_bias=red[6], gdn_norm_w=red[7][:, :GDN_DK], **gbig)

    w = dict(norm_mix=norm_mix, norm_mlp=norm_mlp, norm_final=norm_final, rel_bias=rel_bias, gdn_conv_w=gdn_conv_w,
             gdn_a_log=gdn_a_log, gdn_dt_bias=gdn_dt_bias, gdn_norm_w=gdn_norm_w, **big)
    m = dict(norm_mix=m_norm_mix, norm_mlp=m_norm_mlp, norm_final=m_norm_final, rel_bias=m_rel_bias, gdn_w_in=m_gdn_w_in,
             gdn_conv_w=m_gdn_conv_w, gdn_a_log=m_gdn_a_log, gdn_dt_bias=m_gdn_dt_bias, gdn_norm_w=m_gdn_norm_w,
             gdn_w_out=m_gdn_w_out, dswa_w_in=m_dswa_w_in, dswa_w_out=m_dswa_w_out, mlp_w1=m_mlp_w1, mlp_w2=m_mlp_w2)
    v = dict(norm_mix=v_norm_mix, norm_mlp=v_norm_mlp, norm_final=v_norm_final, rel_bias=v_rel_bias, gdn_w_in=v_gdn_w_in,
             gdn_conv_w=v_gdn_conv_w, gdn_a_log=v_gdn_a_log, gdn_dt_bias=v_gdn_dt_bias, gdn_norm_w=v_gdn_norm_w,
             gdn_w_out=v_gdn_w_out, dswa_w_in=v_dswa_w_in, dswa_w_out=v_dswa_w_out, mlp_w1=v_mlp_w1, mlp_w2=v_mlp_w2)
    names = ["norm_mix", "norm_mlp", "norm_final", "rel_bias", "gdn_w_in", "gdn_conv_w", "gdn_a_log", "gdn_dt_bias",
             "gdn_norm_w", "gdn_w_out", "dswa_w_in", "dswa_w_out", "mlp_w1", "mlp_w2"]
    upd = {n: adamw(w[n], g[n], m[n], v[n], f"adamw_{n}") for n in names}
    loss = lax.psum(loss_part[0, 0], ("x", "y", "c"))
    return (loss, dx[None], *[g[n] for n in names], *[upd[n][0] for n in names], *[upd[n][1] for n in names],
            *[upd[n][2] for n in names])
```

```python
import math
from typing import Callable, NamedTuple

import numpy as np
import jax
import jax.numpy as jnp
from jax import lax
from jax.experimental import pallas as pl
from jax.experimental.pallas import tpu as pltpu

F32 = jnp.float32
BF16 = jnp.bfloat16
HI = lax.Precision.HIGHEST
BS = pl.BlockSpec
SDS = jax.ShapeDtypeStruct
MESH = pl.DeviceIdType.MESH
ANY = BS(memory_space=pl.ANY)

D_MODEL = 1024
D_FF = 4096
DEPTH = 4
RMS_EPS = 1e-6
NEG_INF = -1e30
LANES = 128
VMEM_LIMIT = 56 << 20

GDN_H = 8
GDN_DK = 128
GDN_CONV = 5
GDN_C = 64
GDN_GC = 8
GDN_HP_FWD = 8
GDN_HP_BWD = 4
GDN_QKV = 3 * GDN_H * GDN_DK
GDN_IN = GDN_QKV + GDN_H * GDN_DK + 4 * GDN_H
GDN_INP = 4224

DSWA_CFG = ((128, 1), (512, 4), (2048, 16))
DSWA_HG = 6
DSWA_E = 64
DSWA_HEADS = 18
DSWA_W = DSWA_HEADS * DSWA_E
DSWA_HALF = 64
REL_BUCKETS = 32
REL_MAX_DIST = 1024

ADAM_LR = 0.001
ADAM_B1 = 0.9
ADAM_B2 = 0.999
ADAM_EPS = 1e-08
ADAM_WD = 0.01
ADAM_STEP = 10


def _cp(sem=None):
    return pltpu.CompilerParams(dimension_semantics=sem, vmem_limit_bytes=VMEM_LIMIT)


def _dot(a, b, prec=None):
    return jnp.dot(a, b, precision=prec, preferred_element_type=F32)


def _dot_nt(a, b, prec=None):
    return lax.dot_general(a, b, (((1,), (1,)), ((), ())), precision=prec, preferred_element_type=F32)


def _dot_tn(a, b, prec=None):
    return lax.dot_general(a, b, (((0,), (0,)), ((), ())), precision=prec, preferred_element_type=F32)


def _bf(a):
    return a.astype(BF16)


def _sigmoid(x):
    return 1.0 / (1.0 + jnp.exp(-x))


def rms_fwd(x, g, name):
    S, Dm = x.shape
    tm = min(512, S)

    def body(x_ref, g_ref, o_ref, ot_ref):
        xv = x_ref[...]
        r = lax.rsqrt(jnp.mean(xv * xv, axis=-1, keepdims=True) + RMS_EPS)
        hb = (xv * r * g_ref[...]).astype(o_ref.dtype)
        o_ref[...] = hb
        ot_ref[...] = hb.T

    return pl.pallas_call(
        body, grid=(S // tm,),
        in_specs=[BS((tm, Dm), lambda i: (i, 0)), BS((1, Dm), lambda i: (0, 0))],
        out_specs=[BS((tm, Dm), lambda i: (i, 0)), BS((Dm, tm), lambda i: (0, i))],
        out_shape=[SDS((S, Dm), BF16), SDS((Dm, S), BF16)], name=name, compiler_params=_cp(("parallel",)))(x, g)


def rms_bwd(x, g, dh, dres, name):
    S, Dm = x.shape
    tm = min(512, S)

    def body(x_ref, g_ref, dh_ref, dres_ref, dx_ref, dxb_ref, dg_ref):
        i = pl.program_id(0)
        xv = x_ref[...]
        r = lax.rsqrt(jnp.mean(xv * xv, axis=-1, keepdims=True) + RMS_EPS)
        n = xv * r
        dhv = dh_ref[...]
        t = dhv * g_ref[...]
        dx = dres_ref[...] + r * (t - n * jnp.mean(n * t, axis=-1, keepdims=True))
        dx_ref[...] = dx
        dxb_ref[...] = dx.astype(BF16)
        part = jnp.sum(dhv * n, axis=0, keepdims=True)

        @pl.when(i == 0)
        def _():
            dg_ref[...] = part

        @pl.when(i > 0)
        def _():
            dg_ref[...] += part

    row = BS((tm, Dm), lambda i: (i, 0))
    vec = BS((1, Dm), lambda i: (0, 0))
    return pl.pallas_call(
        body, grid=(S // tm,), in_specs=[row, vec, row, row], out_specs=[row, row, vec],
        out_shape=[SDS((S, Dm), F32), SDS((S, Dm), BF16), SDS((1, Dm), F32)],
        name=name, compiler_params=_cp(("arbitrary",)))(x, g, dh, dres)


def loss_head(x, g, tgt, name):
    S, Dm = x.shape
    tm = min(512, S)

    def body(x_ref, g_ref, t_ref, loss_ref, dx_ref, dxb_ref, dg_ref):
        i = pl.program_id(0)
        xv = x_ref[...]
        gv = g_ref[...]
        r = lax.rsqrt(jnp.mean(xv * xv, axis=-1, keepdims=True) + RMS_EPS)
        n = xv * r
        err = n * gv - t_ref[...]
        lpart = 0.5 * jnp.sum(jnp.mean(err * err, axis=-1, keepdims=True), axis=0, keepdims=True)
        dout = err * (1.0 / Dm)
        t = dout * gv
        dx = r * (t - n * jnp.mean(n * t, axis=-1, keepdims=True))
        dx_ref[...] = dx
        dxb_ref[...] = dx.astype(BF16)
        part = jnp.sum(dout * n, axis=0, keepdims=True)

        @pl.when(i == 0)
        def _():
            dg_ref[...] = part
            loss_ref[...] = lpart

        @pl.when(i > 0)
        def _():
            dg_ref[...] += part
            loss_ref[...] += lpart

    row = BS((tm, Dm), lambda i: (i, 0))
    vec = BS((1, Dm), lambda i: (0, 0))
    one = BS((1, 1), lambda i: (0, 0))
    return pl.pallas_call(
        body, grid=(S // tm,), in_specs=[row, vec, row], out_specs=[one, row, row, vec],
        out_shape=[SDS((1, 1), F32), SDS((S, Dm), F32), SDS((S, Dm), BF16), SDS((1, Dm), F32)],
        name=name, compiler_params=_cp(("arbitrary",)))(x, g, tgt)


MM_DEEP_K = 2048
MM_SHALLOW_K = 1024


def mm(a, b, *, name, ta=False, tb=False, tm=1024, tn=512, out_dtype=F32, pre_a=None, epi=None, extras=()):
    M, K = (a.shape[1], a.shape[0]) if ta else a.shape
    N = b.shape[0] if tb else b.shape[1]
    if ta:
        tm = min(tm, 512)
    elif K > MM_DEEP_K and N % 256 == 0:
        tn = min(tn, 256)
    elif K <= MM_SHALLOW_K:
        tm = 2 * tm
    tm, tn = min(tm, M), min(tn, N)
    assert M % tm == 0 and N % tn == 0, (name, M, N, K, tm, tn)
    ne = len(extras)
    a_spec = BS((K, tm), lambda i, j: (0, i)) if ta else BS((tm, K), lambda i, j: (i, 0))
    b_spec = BS((tn, K), lambda i, j: (j, 0)) if tb else BS((K, tn), lambda i, j: (0, j))
    o_spec = BS((tm, tn), lambda i, j: (i, j))
    dims = (((0 if ta else 1,), (1 if tb else 0,)), ((), ()))

    def body(a_ref, b_ref, *rest):
        e_refs, o_ref = rest[:ne], rest[ne]
        av = a_ref[...]
        if pre_a is not None:
            av = pre_a(av)
        acc = lax.dot_general(_bf(av), _bf(b_ref[...]), dims, preferred_element_type=F32)
        res = epi(acc, *[e[...] for e in e_refs]) if epi is not None else acc
        o_ref[...] = res.astype(o_ref.dtype)

    return pl.pallas_call(
        body, grid=(M // tm, N // tn), in_specs=[a_spec, b_spec] + [o_spec] * ne, out_specs=o_spec,
        out_shape=SDS((M, N), out_dtype), name=name, compiler_params=_cp(("parallel", "parallel")))(a, b, *extras)


def _relu(acc):
    return jnp.maximum(acc, 0.0)


def _add(acc, res):
    return acc + res


def _sq(av):
    return av * av


def _times_2r(acc, r):
    return acc * (2.0 * r.astype(F32))


def mlp_fwd(x, g, w1, w2, tag):
    h, ht = rms_fwd(x, g, f"{tag}_rms")
    r = mm(h, w1, name=f"{tag}_up", tn=1024, out_dtype=BF16, epi=_relu)
    xn = mm(r, w2, name=f"{tag}_down", pre_a=_sq, epi=_add, extras=(x,))
    return xn, (ht, r)


def mlp_bwd(x, g, w1, w2, saved, dx, dxb, tag):
    ht, r = saved
    da = mm(dxb, w2, name=f"{tag}_dact", tb=True, tn=1024, out_dtype=BF16, epi=_times_2r, extras=(r,))
    dw2 = mm(r, dxb, name=f"{tag}_dw2", ta=True, pre_a=_sq)
    dw1 = mm(ht, da, name=f"{tag}_dw1")
    dh = mm(da, w1, name=f"{tag}_dh", tb=True)
    dx, dxb, dg = rms_bwd(x, g, dh, dx, f"{tag}_rmsb")
    return dx, dxb, dg, dw1, dw2


def _conv_taps(x, S):
    t = lax.broadcasted_iota(jnp.int32, x.shape, 0)
    taps = []
    for j in range(GDN_CONV):
        sh = j - GDN_CONV // 2
        xs = x if sh == 0 else pltpu.roll(x, (-sh) % S, 0)
        taps.append(jnp.where((t + sh >= 0) & (t + sh < S), xs, 0.0))
    return taps


def _qkv_scale(c):
    is_norm = c < 2 * GDN_H
    scale = jnp.where(c < GDN_H, GDN_DK ** -0.5, 1.0)
    return is_norm, scale


def gdn_pre_fwd(proj, convw, name):
    S = proj.shape[0]

    def body(p_ref, w_ref, o_ref):
        c = pl.program_id(0)
        x = p_ref[...]
        w = w_ref[...]
        y = jnp.zeros_like(x)
        for j, xs in enumerate(_conv_taps(x, S)):
            y = y + w[j:j + 1, :] * xs
        t = y * _sigmoid(y)
        is_norm, scale = _qkv_scale(c)
        r = lax.rsqrt(jnp.sum(t * t, axis=-1, keepdims=True) + 1e-6)
        o_ref[...] = jnp.where(is_norm, t * r * scale, t)

    return pl.pallas_call(
        body, grid=(GDN_QKV // LANES,),
        in_specs=[BS((S, LANES), lambda c: (0, c)), BS((8, LANES), lambda c: (0, c))],
        out_specs=BS((S, LANES), lambda c: (0, c)),
        out_shape=SDS((S, GDN_QKV), F32), name=name, compiler_params=_cp(("parallel",)))(proj, convw)


def gdn_pre_bwd(proj, convw, dqkv, name):
    S = proj.shape[0]

    def body(p_ref, w_ref, d_ref, dp_ref, dw_ref):
        c = pl.program_id(0)
        x = p_ref[...]
        w = w_ref[...]
        taps = _conv_taps(x, S)
        y = jnp.zeros_like(x)
        for j, xs in enumerate(taps):
            y = y + w[j:j + 1, :] * xs
        sg = _sigmoid(y)
        t = y * sg
        is_norm, scale = _qkv_scale(c)
        dout = d_ref[0, 0] + d_ref[1, 0]
        r = lax.rsqrt(jnp.sum(t * t, axis=-1, keepdims=True) + 1e-6)
        n = t * r
        dn = dout * scale
        dt_norm = r * (dn - n * jnp.sum(dn * n, axis=-1, keepdims=True))
        dt = jnp.where(is_norm, dt_norm, dout)
        dy = dt * (sg * (1.0 + y * (1.0 - sg)))
        row = lax.broadcasted_iota(jnp.int32, (8, LANES), 0)
        dw = jnp.zeros((8, LANES), F32)
        for j, xs in enumerate(taps):
            dw = dw + jnp.where(row == j, jnp.sum(dy * xs, axis=0, keepdims=True), 0.0)
        dw_ref[...] = dw
        tt = lax.broadcasted_iota(jnp.int32, x.shape, 0)
        dx = jnp.zeros_like(x)
        for j in range(GDN_CONV):
            sh = j - GDN_CONV // 2
            ds = dy if sh == 0 else pltpu.roll(dy, sh % S, 0)
            dx = dx + w[j:j + 1, :] * jnp.where((tt - sh >= 0) & (tt - sh < S), ds, 0.0)
        dp_ref[...] = dx.astype(BF16)

    return pl.pallas_call(
        body, grid=(GDN_QKV // LANES,),
        in_specs=[BS((S, LANES), lambda c: (0, c)), BS((8, LANES), lambda c: (0, c)),
                  BS((2, 1, S, LANES), lambda c: (0, c // GDN_H, 0, c % GDN_H))],
        out_specs=[BS((S, LANES), lambda c: (0, c)), BS((8, LANES), lambda c: (0, c))],
        out_shape=[SDS((S, GDN_QKV), BF16), SDS((8, GDN_QKV), F32)],
        name=name, compiler_params=_cp(("parallel",)))(proj, convw, dqkv)


def _chunk_sum_matrix(n, upper):
    i = lax.broadcasted_iota(jnp.int32, (n, n), 0)
    j = lax.broadcasted_iota(jnp.int32, (n, n), 1)
    same = (i // GDN_C) == (j // GDN_C)
    tri = (i <= j) if upper else (i >= j)
    return jnp.where(same & tri, 1.0, 0.0).astype(F32)


def _gate_lanes(shape):
    lane = lax.broadcasted_iota(jnp.int32, shape, 1)
    return lane < GDN_H, (lane >= GDN_H) & (lane < 2 * GDN_H), (lane >= 2 * GDN_H) & (lane < 4 * GDN_H)


def gdn_gate_fwd(proj, prm, name):
    S = proj.shape[0]
    tm = min(512, S)
    ct = GDN_INP // LANES - 1

    def body(p_ref, prm_ref, o_ref):
        ab = p_ref[...]
        a_log = prm_ref[0:1, :]
        dtb = prm_ref[1:2, :]
        z = ab + dtb
        sp = jnp.maximum(z, 0.0) + jnp.log(1.0 + jnp.exp(-jnp.abs(z)))
        g = -jnp.exp(a_log) * sp
        is_f, is_b, is_beta = _gate_lanes(ab.shape)
        gf = _dot(_chunk_sum_matrix(tm, False), jnp.where(is_f, g, 0.0), HI)
        gbk = _dot(_chunk_sum_matrix(tm, True), jnp.where(is_b, g, 0.0), HI)
        o_ref[...] = gf + gbk + jnp.where(is_beta, _sigmoid(ab), 0.0)

    return pl.pallas_call(
        body, grid=(S // tm,),
        in_specs=[BS((tm, LANES), lambda i: (i, ct)), BS((8, LANES), lambda i: (0, 0))],
        out_specs=BS((tm, LANES), lambda i: (i, 0)),
        out_shape=SDS((S, LANES), F32), name=name, compiler_params=_cp(("parallel",)))(proj, prm)


def gdn_gate_bwd(proj, prm, dgb, name):
    S = proj.shape[0]
    tm = min(512, S)
    ct = GDN_INP // LANES - 1

    def body(p_ref, prm_ref, d_ref, dab_ref, dprm_ref):
        i = pl.program_id(0)
        ab = p_ref[...]
        a_log = prm_ref[0:1, :]
        dtb = prm_ref[1:2, :]
        z = ab + dtb
        sp = jnp.maximum(z, 0.0) + jnp.log(1.0 + jnp.exp(-jnp.abs(z)))
        ea = jnp.exp(a_log)
        g = -ea * sp
        is_f, is_b, is_beta = _gate_lanes(ab.shape)
        d = d_ref[...]
        dg = (_dot_tn(_chunk_sum_matrix(tm, False), jnp.where(is_f, d, 0.0), HI)
              + _dot_tn(_chunk_sum_matrix(tm, True), jnp.where(is_b, d, 0.0), HI))
        da = dg * (-ea) * _sigmoid(z)
        beta = _sigmoid(ab)
        dab_ref[...] = jnp.where(is_beta, d * beta * (1.0 - beta), da).astype(BF16)
        row = lax.broadcasted_iota(jnp.int32, (8, LANES), 0)
        part = (jnp.where(row == 0, jnp.sum(dg * g, axis=0, keepdims=True), 0.0)
                + jnp.where(row == 1, jnp.sum(da, axis=0, keepdims=True), 0.0))

        @pl.when(i == 0)
        def _():
            dprm_ref[...] = part

        @pl.when(i > 0)
        def _():
            dprm_ref[...] += part

    return pl.pallas_call(
        body, grid=(S // tm,),
        in_specs=[BS((tm, LANES), lambda i: (i, ct)), BS((8, LANES), lambda i: (0, 0)), BS((tm, LANES), lambda i: (i, 0))],
        out_specs=[BS((tm, LANES), lambda i: (i, 0)), BS((8, LANES), lambda i: (0, 0))],
        out_shape=[SDS((S, LANES), BF16), SDS((8, LANES), F32)],
        name=name, compiler_params=_cp(("arbitrary",)))(proj, prm, dgb)


def _tri_masks(d):
    i = lax.broadcasted_iota(jnp.int32, (GDN_C, GDN_C), 0)
    j = lax.broadcasted_iota(jnp.int32, (GDN_C, GDN_C), 1)
    s = (i - j) * (1 - 2 * d)
    return s >= 0, s > 0


def _split(a):
    hi = _bf(a)
    return hi, _bf(a - hi.astype(F32))


def _dot3(a, b):
    return _dot(a[0], b[0]) + (_dot(a[0], b[1]) + _dot(a[1], b[0]))


def _inv_unit_tri_many(mats):
    i = lax.broadcasted_iota(jnp.int32, mats[0].shape, 0)
    j = lax.broadcasted_iota(jnp.int32, mats[0].shape, 1)
    eye = jnp.where(i == j, 1.0, 0.0)
    ms = [-a for a in mats]
    ps = [eye + m for m in ms]
    for _ in range(int(math.log2(GDN_C)) - 1):
        sp = [_split(m) for m in ms]
        ms = [_dot3(s, s) for s in sp]
        sp = [_split(m) for m in ms]
        pp = [_split(p) for p in ps]
        ps = [p + _dot3(a, b) for p, a, b in zip(ps, pp, sp)]
    return ps


def _lane_col(x, lane_idx):
    lane = lax.broadcasted_iota(jnp.int32, x.shape, 1)
    return jnp.sum(jnp.where(lane == lane_idx, x, 0.0), axis=1, keepdims=True)


def _chunk_gates(gb_ref, grow_ref, hh, ci, d, head):
    gbv = gb_ref[ci * GDN_C:(ci + 1) * GDN_C, :]
    gcol = _lane_col(gbv, d * GDN_H + head)
    bcol = _lane_col(gbv, 2 * GDN_H + d * GDN_H + head)
    glast = jnp.where(d == 0, gcol[GDN_C - 1:GDN_C, :], gcol[0:1, :])
    return gcol, bcol, grow_ref[hh, ci:ci + 1, :], glast


def _chunk_base(q, k, gcol, grow, bcol, glast, d):
    incl, strict = _tri_masks(d)
    decay = jnp.where(incl, jnp.exp(jnp.where(incl, gcol - grow, 0.0)), 0.0)
    kb = k * bcol
    kk = _dot_nt(_bf(kb), _bf(k))
    qk = _dot_nt(_bf(q), _bf(k))
    eg = jnp.exp(gcol)
    ek = jnp.exp(glast - gcol)
    return dict(incl=incl, strict=strict, decay=decay, kb=kb, kk=kk, qk=qk, eg=eg, ek=ek, q_dec=q * eg, k_dec=k * ek,
                bcol=bcol, glast=glast)


def _block_terms(q_ref, k_ref, v_ref, gb_ref, grow_ref, d, h, hp):
    keys = [(hh, ci) for hh in range(hp) for ci in range(GDN_GC)]
    ts = []
    for hh, ci in keys:
        rows = slice(ci * GDN_C, (ci + 1) * GDN_C)
        cols = slice(hh * GDN_DK, (hh + 1) * GDN_DK)
        gcol, bcol, grow_v, glast = _chunk_gates(gb_ref, grow_ref, hh, ci, d, h * hp + hh)
        t = _chunk_base(q_ref[rows, cols], k_ref[rows, cols], gcol, grow_v, bcol, glast, d)
        t["v"] = v_ref[rows, cols]
        ts.append(t)
    tinvs = _inv_unit_tri_many([jnp.where(t["strict"], t["kk"] * t["decay"], 0.0) for t in ts])
    sp = [_split(x) for x in tinvs]
    us = [_dot3(s, _split(t["v"] * t["bcol"])) for s, t in zip(sp, ts)]
    ws = [_dot3(s, _split(t["kb"] * t["eg"])) for s, t in zip(sp, ts)]
    for t, tinv, u, w in zip(ts, tinvs, us, ws):
        t.update(tinv=tinv, u=u, w=w)
    return keys, ts


def _gdn_specs(S, nblk, order, hp):
    R = GDN_GC * GDN_C
    wd = hp * GDN_DK
    hb = GDN_H // hp

    def qkv_spec(part):
        return BS((R, wd), lambda d, h, n: (order(d, n), part * hb + h))

    gb_spec = BS((R, LANES), lambda d, h, n: (order(d, n), 0))
    grow_spec = BS((hp, GDN_GC, GDN_C), lambda d, h, n: (d * hb + h, order(d, n), 0))
    st_spec = BS((1, hp, GDN_GC, GDN_DK, GDN_DK), lambda d, h, n: (d, h, order(d, n), 0, 0))
    return qkv_spec, gb_spec, grow_spec, st_spec


def _lane_row(x):
    return jnp.broadcast_to(x, (1, LANES))


def _side_parts(side):
    if side is None:
        return [], [], [], [], []
    return [ANY] * len(side.ins), [ANY] * len(side.outs), list(side.outs), list(side.sems), list(side.ins)


def _side_run(side, refs, n_in, n_out, n_scr, first, last):
    if side is None:
        return
    ns, no, nm = len(side.ins), len(side.outs), len(side.sems)
    s_in = refs[n_in:n_in + ns]
    s_out = refs[n_in + ns + n_out:n_in + ns + n_out + no]
    s_sem = refs[len(refs) - nm:]

    @pl.when(first)
    def _():
        side.start(s_in, s_out, s_sem)

    @pl.when(last)
    def _():
        side.wait(s_in, s_out, s_sem)


def gdn_scan_fwd(qkv, gb, grow, name, side=None):
    S = qkv.shape[0]
    R = GDN_GC * GDN_C
    nblk = S // R
    nc = S // GDN_C
    hp = GDN_HP_FWD
    wd = hp * GDN_DK
    heads = range(hp)

    def order(d, n):
        return n + d * (nblk - 1 - 2 * n)

    qkv_spec, gb_spec, grow_spec, st_spec = _gdn_specs(S, nblk, order, hp)

    s_in, s_out, s_shape, s_scr_shapes, s_ops = _side_parts(side)
    hb = GDN_H // hp

    def body(*refs):
        q_ref, k_ref, v_ref, gb_ref, grow_ref = refs[:5]
        o_ref, st_ref = refs[5 + len(s_in):7 + len(s_in)]
        s_scr, u_scr, w_scr, qd_scr, kd_scr, in_scr, egl_scr = refs[7 + len(s_in) + len(s_out):14 + len(s_in) + len(s_out)]
        d = pl.program_id(0)
        h = pl.program_id(1)
        n = pl.program_id(2)
        _side_run(side, refs, 5, 2, 7, (d == 0) & (h == 0) & (n == 0), (d == 1) & (h == hb - 1) & (n == nblk - 1))

        @pl.when(n == 0)
        def _():
            s_scr[...] = jnp.zeros_like(s_scr)

        keys, ts = _block_terms(q_ref, k_ref, v_ref, gb_ref, grow_ref, d, h, hp)
        for (hh, ci), t in zip(keys, ts):
            u_scr[hh, ci] = t["u"]
            w_scr[hh, ci] = _bf(t["w"])
            qd_scr[hh, ci] = _bf(t["q_dec"])
            kd_scr[hh, ci] = _bf(t["k_dec"])
            in_scr[hh, ci] = _bf(jnp.where(t["incl"], t["qk"] * t["decay"], 0.0))
            egl_scr[hh, ci] = _lane_row(jnp.exp(t["glast"]))

        def chunk(cc, carry):
            ci = cc + d * (GDN_GC - 1 - 2 * cc)
            rows = pl.ds(pl.multiple_of(ci * GDN_C, GDN_C), GDN_C)
            sts = [s_scr[hh] for hh in heads]
            for hh in heads:
                st_ref[0, hh, ci] = sts[hh]
            sbs = [_bf(st) for st in sts]
            vns = [_bf(u_scr[hh, ci] - _dot(w_scr[hh, ci], sbs[hh])) for hh in heads]
            for hh in heads:
                s_scr[hh] = sts[hh] * egl_scr[hh, ci] + _dot_tn(kd_scr[hh, ci], vns[hh])
            for hh in heads:
                o_ref[0, rows, hh * GDN_DK:(hh + 1) * GDN_DK] = _dot(qd_scr[hh, ci], sbs[hh]) + _dot(in_scr[hh, ci], vns[hh])
            return carry

        lax.fori_loop(0, GDN_GC, chunk, 0)

    blk = (hp, GDN_GC, GDN_C, GDN_DK)
    return pl.pallas_call(
        body, grid=(2, GDN_H // hp, nblk),
        in_specs=[qkv_spec(0), qkv_spec(1), qkv_spec(2), gb_spec, grow_spec] + s_in,
        out_specs=[BS((1, R, wd), lambda d, h, n: (d, order(d, n), h)), st_spec] + s_out,
        out_shape=[SDS((2, S, GDN_H * GDN_DK), F32), SDS((2, GDN_H, nc, GDN_DK, GDN_DK), F32)] + s_shape,
        scratch_shapes=[pltpu.VMEM((hp, GDN_DK, GDN_DK), F32), pltpu.VMEM(blk, F32), pltpu.VMEM(blk, BF16),
                        pltpu.VMEM(blk, BF16), pltpu.VMEM(blk, BF16), pltpu.VMEM((hp, GDN_GC, GDN_C, GDN_C), BF16),
                        pltpu.VMEM((hp, GDN_GC, 1, LANES), F32)] + s_scr_shapes,
        name=name, compiler_params=_cp(("arbitrary", "arbitrary", "arbitrary")))(qkv, qkv, qkv, gb, grow, *s_ops)


def gdn_scan_bwd(qkv, gb, grow, states, do, name, side=None):
    S = qkv.shape[0]
    R = GDN_GC * GDN_C
    nblk = S // R
    hp = GDN_HP_BWD
    wd = hp * GDN_DK
    heads = range(hp)

    def order(d, n):
        return (nblk - 1 - n) - d * (nblk - 1 - 2 * n)

    qkv_spec, gb_spec, grow_spec, st_spec = _gdn_specs(S, nblk, order, hp)

    s_in, s_out, s_shape, s_scr_shapes, s_ops = _side_parts(side)
    hb = GDN_H // hp

    def body(*refs):
        q_ref, k_ref, v_ref, gb_ref, grow_ref, st_ref, do_ref = refs[:7]
        dqkv_ref, dgate_ref = refs[7 + len(s_in):9 + len(s_in)]
        (ds_scr, w_scr, kd_scr, dv1_scr, qtdo_scr, egl_scr, dsin_scr, dvn_scr,
         sdot_scr) = refs[9 + len(s_in) + len(s_out):18 + len(s_in) + len(s_out)]
        d = pl.program_id(0)
        h = pl.program_id(1)
        n = pl.program_id(2)
        _side_run(side, refs, 7, 2, 9, (d == 0) & (h == 0) & (n == 0), (d == 1) & (h == hb - 1) & (n == nblk - 1))

        @pl.when(n == 0)
        def _():
            ds_scr[...] = jnp.zeros_like(ds_scr)

        keys, ts = _block_terms(q_ref, k_ref, v_ref, gb_ref, grow_ref, d, h, hp)
        for (hh, ci), t in zip(keys, ts):
            rows = slice(ci * GDN_C, (ci + 1) * GDN_C)
            t["wb"] = _bf(t["w"])
            t["dob"] = _bf(do_ref[rows, hh * GDN_DK:(hh + 1) * GDN_DK])
            t["sb"] = _bf(st_ref[0, hh, ci])
        for (hh, ci), t in zip(keys, ts):
            t["vnb"] = _bf(t["u"] - _dot(t["wb"], t["sb"]))
            w_scr[hh, ci] = t["wb"]
            kd_scr[hh, ci] = _bf(t["k_dec"])
            dv1_scr[hh, ci] = _dot_tn(_bf(jnp.where(t["incl"], t["qk"] * t["decay"], 0.0)), t["dob"])
            qtdo_scr[hh, ci] = _dot_tn(_bf(t["q_dec"]), t["dob"])
            egl_scr[hh, ci] = _lane_row(jnp.exp(t["glast"]))

        def chunk(cc, carry):
            ci = (GDN_GC - 1 - cc) - d * (GDN_GC - 1 - 2 * cc)
            dsns = [ds_scr[hh] for hh in heads]
            dsbs = [_bf(x) for x in dsns]
            dvns = [dv1_scr[hh, ci] + _dot(kd_scr[hh, ci], dsbs[hh]) for hh in heads]
            for hh in heads:
                ds_scr[hh] = qtdo_scr[hh, ci] + egl_scr[hh, ci] * dsns[hh] - _dot_tn(w_scr[hh, ci], _bf(dvns[hh]))
            for hh in heads:
                dsin_scr[hh, ci] = dsbs[hh]
                dvn_scr[hh, ci] = dvns[hh]
                sd = jnp.sum(jnp.sum(st_ref[0, hh, ci] * dsns[hh], axis=1, keepdims=True), axis=0, keepdims=True)
                sdot_scr[hh, ci] = _lane_row(sd)
            return carry

        lax.fori_loop(0, GDN_GC, chunk, 0)

        for (hh, ci), t in zip(keys, ts):
            t["d_vnew"] = dvn_scr[hh, ci]
            t["dvb"] = _bf(t["d_vnew"])
            t["dsb"] = dsin_scr[hh, ci]
        for t in ts:
            t["d_intra"] = jnp.where(t["incl"], _dot_nt(t["dob"], t["vnb"]), 0.0)
            t["d_qdec"] = _dot_nt(t["dob"], t["sb"])
            t["d_kdec"] = _dot_nt(t["vnb"], t["dsb"])
            t["dw"] = -_dot_nt(t["dvb"], t["sb"])
        for t in ts:
            tts = _split(t["tinv"].T)
            t["d_ru"] = _dot3(tts, _split(t["d_vnew"]))
            t["d_rw"] = _dot3(tts, _split(t["dw"]))
        for t in ts:
            t["da"] = -jnp.where(t["strict"], _dot_nt(_bf(t["d_ru"]), _bf(t["u"])) + _dot_nt(_bf(t["d_rw"]), t["wb"]), 0.0)
        for (hh, ci), t in zip(keys, ts):
            rows = slice(ci * GDN_C, (ci + 1) * GDN_C)
            cols = slice(hh * GDN_DK, (hh + 1) * GDN_DK)
            q, k, v = q_ref[rows, cols], k_ref[rows, cols], t["v"]
            decay, kb, eg, ek, bcol = t["decay"], t["kb"], t["eg"], t["ek"], t["bcol"]
            d_ru, d_rw, da, d_intra, d_qdec, d_kdec = t["d_ru"], t["d_rw"], t["da"], t["d_intra"], t["d_qdec"], t["d_kdec"]
            kbf, qbf = _bf(k), _bf(q)
            dgl = egl_scr[hh, ci][:, 0:1] * sdot_scr[hh, ci][:, 0:1]
            dv = d_ru * bcol
            dbeta = jnp.sum(d_ru * v, axis=1, keepdims=True)
            dkb = d_rw * eg
            dg = jnp.sum(d_rw * kb, axis=1, keepdims=True) * eg
            dkk = _bf(da * decay)
            dqk = _bf(d_intra * decay)
            dkb = dkb + _dot(dkk, kbf)
            dk = _dot_tn(dkk, _bf(kb)) + _dot_tn(dqk, qbf)
            dq = _dot(dqk, kbf) + d_qdec * eg
            dd = (da * t["kk"] + d_intra * t["qk"]) * decay
            dg = dg + jnp.sum(dd, axis=1, keepdims=True) - jnp.sum(dd.T, axis=1, keepdims=True)
            dg = dg + jnp.sum(d_qdec * t["q_dec"], axis=1, keepdims=True)
            dk = dk + d_kdec * ek
            ee = jnp.sum(d_kdec * t["k_dec"], axis=1, keepdims=True)
            dg = dg - ee
            dgl = dgl + jnp.sum(ee, axis=0, keepdims=True)
            dk = dk + dkb * bcol
            dbeta = dbeta + jnp.sum(dkb * k, axis=1, keepdims=True)
            ridx = lax.broadcasted_iota(jnp.int32, (GDN_C, 1), 0)
            dg = dg + jnp.where(ridx == (GDN_C - 1) * (1 - d), dgl, 0.0)
            dqkv_ref[0, 0, rows, cols] = dq
            dqkv_ref[0, 1, rows, cols] = dk
            dqkv_ref[0, 2, rows, cols] = dv
            lane2 = lax.broadcasted_iota(jnp.int32, (GDN_C, 2), 1)
            dgate_ref[0, hh, rows, :] = jnp.where(lane2 == 0, dg, dbeta)

    blk = (hp, GDN_GC, GDN_C, GDN_DK)
    sq = (hp, GDN_GC, GDN_DK, GDN_DK)
    row = (hp, GDN_GC, 1, LANES)
    return pl.pallas_call(
        body, grid=(2, GDN_H // hp, nblk),
        in_specs=[qkv_spec(0), qkv_spec(1), qkv_spec(2), gb_spec, grow_spec, st_spec,
                  BS((R, wd), lambda d, h, n: (order(d, n), h))] + s_in,
        out_specs=[BS((1, 3, R, wd), lambda d, h, n: (d, 0, order(d, n), h)),
                   BS((1, hp, R, 2), lambda d, h, n: (d, h, order(d, n), 0))] + s_out,
        out_shape=[SDS((2, 3, S, GDN_H * GDN_DK), F32), SDS((2, GDN_H, S, 2), F32)] + s_shape,
        scratch_shapes=[pltpu.VMEM((hp, GDN_DK, GDN_DK), F32), pltpu.VMEM(blk, BF16), pltpu.VMEM(blk, BF16),
                        pltpu.VMEM(blk, F32), pltpu.VMEM(sq, F32), pltpu.VMEM(row, F32), pltpu.VMEM(sq, BF16),
                        pltpu.VMEM(blk, F32), pltpu.VMEM(row, F32)] + s_scr_shapes,
        name=name, compiler_params=_cp(("arbitrary", "arbitrary", "arbitrary")))(qkv, qkv, qkv, gb, grow, states, do, *s_ops)


def gdn_post_fwd(o2, proj, nw, name):
    S = proj.shape[0]
    tm = min(512, S)
    zoff = GDN_QKV // LANES

    def body(o_ref, z_ref, nw_ref, y_ref):
        o = o_ref[0] + o_ref[1]
        z = z_ref[...]
        r = lax.rsqrt(jnp.mean(o * o, axis=-1, keepdims=True) + RMS_EPS)
        y_ref[...] = (o * r * nw_ref[...] * (z * _sigmoid(z))).astype(BF16)

    return pl.pallas_call(
        body, grid=(S // tm, GDN_H),
        in_specs=[BS((2, tm, LANES), lambda i, h: (0, i, h)), BS((tm, LANES), lambda i, h: (i, zoff + h)),
                  BS((1, LANES), lambda i, h: (0, 0))],
        out_specs=BS((tm, LANES), lambda i, h: (i, h)),
        out_shape=SDS((S, GDN_H * GDN_DK), BF16), name=name, compiler_params=_cp(("parallel", "parallel")))(o2, proj, nw)


def gdn_post_bwd(o2, proj, nw, dy, name):
    S = proj.shape[0]
    tm = min(512, S)
    zoff = GDN_QKV // LANES

    def body(o_ref, z_ref, nw_ref, dy_ref, do_ref, dz_ref, dnw_ref):
        first = (pl.program_id(0) == 0) & (pl.program_id(1) == 0)
        o = o_ref[0] + o_ref[1]
        z = z_ref[...]
        nwv = nw_ref[...]
        dyv = dy_ref[...]
        r = lax.rsqrt(jnp.mean(o * o, axis=-1, keepdims=True) + RMS_EPS)
        n = o * r
        sg = _sigmoid(z)
        sz = z * sg
        dz_ref[...] = (dyv * n * nwv * (sg * (1.0 + z * (1.0 - sg)))).astype(BF16)
        dn = dyv * nwv * sz
        do_ref[...] = r * (dn - n * jnp.mean(dn * n, axis=-1, keepdims=True))
        part = jnp.sum(dyv * n * sz, axis=0, keepdims=True)

        @pl.when(first)
        def _():
            dnw_ref[...] = part

        @pl.when(jnp.logical_not(first))
        def _():
            dnw_ref[...] += part

    blk = BS((tm, LANES), lambda i, h: (i, h))
    return pl.pallas_call(
        body, grid=(S // tm, GDN_H),
        in_specs=[BS((2, tm, LANES), lambda i, h: (0, i, h)), BS((tm, LANES), lambda i, h: (i, zoff + h)),
                  BS((1, LANES), lambda i, h: (0, 0)), blk],
        out_specs=[blk, blk, BS((1, LANES), lambda i, h: (0, 0))],
        out_shape=[SDS((S, GDN_H * GDN_DK), F32), SDS((S, GDN_H * GDN_DK), BF16), SDS((1, LANES), F32)],
        name=name, compiler_params=_cp(("arbitrary", "arbitrary")))(o2, proj, nw, dy)


def _gate_prm(a_log, dt_bias):
    z = jnp.zeros((8, LANES), F32)
    z = z.at[0, :2 * GDN_H].set(a_log.reshape(-1))
    return z.at[1, :2 * GDN_H].set(dt_bias.reshape(-1))


def gdn_fwd(x, g, w_all, convw, a_log, dt_bias, nw, w_out, tag, side=None):
    S = x.shape[0]
    h, ht = rms_fwd(x, g, f"{tag}_rms")
    proj = mm(h, w_all, name=f"{tag}_proj", tn=1408)
    qkv = gdn_pre_fwd(proj, convw, f"{tag}_pre")
    prm = _gate_prm(a_log, dt_bias)
    gb = gdn_gate_fwd(proj, prm, f"{tag}_gate")
    grow = gb[:, :2 * GDN_H].T.reshape(2 * GDN_H, S // GDN_C, GDN_C)
    o2, states, *side_out = gdn_scan_fwd(qkv, gb, grow, f"{tag}_scan", side)
    y = gdn_post_fwd(o2, proj, nw, f"{tag}_post")
    xn = mm(y, w_out, name=f"{tag}_out", epi=_add, extras=(x,))
    return xn, (ht, proj, qkv, prm, gb, grow, o2, states, y), side_out


def gdn_bwd(x, g, w_all, convw, nw, w_out, saved, dx, dxb, tag, side=None):
    S = x.shape[0]
    ht, proj, qkv, prm, gb, grow, o2, states, y = saved
    dw_out = mm(y, dxb, name=f"{tag}_dwout", ta=True)
    dy = mm(dxb, w_out, name=f"{tag}_dy", tb=True)
    do, dz, dnw = gdn_post_bwd(o2, proj, nw, dy, f"{tag}_postb")
    dqkv, dgate, *side_out = gdn_scan_bwd(qkv, gb, grow, states, do, f"{tag}_scanb", side)
    dgb = jnp.transpose(dgate, (2, 3, 0, 1)).reshape(S, 4 * GDN_H)
    dgb = jnp.pad(dgb, ((0, 0), (0, LANES - 4 * GDN_H)))
    dab, dprm = gdn_gate_bwd(proj, prm, dgb, f"{tag}_gateb")
    dpq, dconvw = gdn_pre_bwd(proj, convw, dqkv, f"{tag}_preb")
    dproj = jnp.concatenate([dpq, dz, dab], axis=1)
    dw_all = mm(ht, dproj, name=f"{tag}_dwin", tn=384)
    dh = mm(dproj, w_all, name=f"{tag}_dh", tb=True)
    dx, dxb, dg = rms_bwd(x, g, dh, dx, f"{tag}_rmsb")
    da_log = dprm[0, :2 * GDN_H].reshape(2, GDN_H)
    ddt = dprm[1, :2 * GDN_H].reshape(2, GDN_H)
    return dx, dxb, dg, dw_all, dconvw, da_log, ddt, dnw, dw_out, side_out


def _rel_bucket_np(rel):
    nb = REL_BUCKETS // 2
    max_exact = nb // 2
    ret = np.where(rel > 0, nb, 0)
    n = np.abs(rel)
    nf = np.maximum(n, 1).astype(np.float32)
    large = max_exact + (np.log(nf / max_exact) / np.float32(math.log(REL_MAX_DIST / max_exact))
                         * (nb - max_exact)).astype(np.int32)
    large = np.minimum(large, nb - 1)
    return ret + np.where(n < max_exact, n, large)


def _toeplitz(f, rows, cols):
    period = rows + cols
    e = jnp.pad(f, ((0, 0), (0, period - f.shape[1])))
    y = jnp.tile(e, (1, rows))[:, :rows * (period - 1)]
    return y.reshape(f.shape[0], rows, period - 1)[:, :, :cols]


ATT_Q = DSWA_HALF
ATT_W = 3 * DSWA_HALF
ATT_TB = 1024
ATT_PAIRS = DSWA_HG // 2


def _bias_mats(rel_table, gi):
    _, dil = DSWA_CFG[gi]
    offs = np.arange(-DSWA_HALF, DSWA_HALF + 1)
    onehot = jnp.asarray(np.eye(REL_BUCKETS, dtype=np.float32)[_rel_bucket_np(offs * dil)])
    f = jnp.dot(onehot, rel_table, precision=HI)[:, gi * DSWA_HG:(gi + 1) * DSWA_HG].T
    bias = _toeplitz(f, ATT_Q, ATT_W)
    bias_t = jnp.transpose(_toeplitz(f[:, ::-1], ATT_Q, ATT_W), (0, 2, 1))
    return bias.reshape(ATT_PAIRS, 2, ATT_Q, ATT_W), bias_t.reshape(ATT_PAIRS, 2, ATT_W, ATT_Q)


def _att_specs(S, d, col):
    halo = DSWA_HALF * d
    per = ATT_TB // halo
    last = S // halo - 1
    cur = BS((ATT_TB, LANES), lambda p, tb: (tb, col(p)))
    prev = BS((halo, LANES), lambda p, tb: (jnp.maximum(tb * per - 1, 0), col(p)))
    nxt = BS((halo, LANES), lambda p, tb: (jnp.minimum((tb + 1) * per, last), col(p)))
    return prev, cur, nxt


def _att_specs3(S, d, lead):
    halo = DSWA_HALF * d
    per = ATT_TB // halo
    last = S // halo - 1
    cur = BS((1, ATT_TB, LANES), lambda p, tb: (lead(p), tb, 0))
    prev = BS((1, halo, LANES), lambda p, tb: (lead(p), jnp.maximum(tb * per - 1, 0), 0))
    nxt = BS((1, halo, LANES), lambda p, tb: (lead(p), jnp.minimum((tb + 1) * per, last), 0))
    return prev, cur, nxt


class _Pieces:
    def __init__(self, prev, cur, nxt, d, lead=None, cast=None):
        self.refs, self.d, self.lead, self.cast, self.cache = (prev, cur, nxt), d, lead, cast, {}
        self.halo = DSWA_HALF * d
        self.nsb = ATT_TB // self.halo

    def __call__(self, r, sb):
        if (r, sb) not in self.cache:
            ref = self.refs[0] if sb < 0 else self.refs[2] if sb >= self.nsb else self.refs[1]
            start = r + (self.halo * sb if 0 <= sb < self.nsb else 0)
            rows = pl.ds(start, ATT_Q, stride=self.d) if self.d > 1 else pl.ds(start, ATT_Q)
            v = ref[rows, :] if self.lead is None else ref[0, rows, :]
            self.cache[(r, sb)] = v if self.cast is None else v.astype(self.cast)
        return self.cache[(r, sb)]

    def window(self, r, sb):
        return jnp.concatenate([self(r, sb - 1), self(r, sb), self(r, sb + 1)], axis=0)


ATT_GROUP = 8


def _tile_groups(d, nsb):
    tiles = [(r, sb) for r in range(d) for sb in range(nsb)]
    return [tiles[i:i + ATT_GROUP] for i in range(0, len(tiles), ATT_GROUP)]


def _tile_rows(r, sb, d):
    start = r + DSWA_HALF * d * sb
    return pl.ds(start, ATT_Q, stride=d) if d > 1 else pl.ds(start, ATT_Q)


def _tile_valid(tb, r, sb, d, S, transposed):
    shape = (ATT_W, ATT_Q) if transposed else (ATT_Q, ATT_W)
    blk = lax.broadcasted_iota(jnp.int32, shape, 1 if transposed else 0)
    win = lax.broadcasted_iota(jnp.int32, shape, 0 if transposed else 1)
    tok = tb * ATT_TB + r + d * (DSWA_HALF * (sb - 1) + win)
    return (jnp.abs(win - DSWA_HALF - blk) <= DSWA_HALF) & (tok >= 0) & (tok < S)


def _head_masks():
    lane = lax.broadcasted_iota(jnp.int32, (1, LANES), 1)
    return [lane < DSWA_E, lane >= DSWA_E], lane


def attn_fwd(qkv, bias, gi, name):
    S = qkv.shape[0]
    d = DSWA_CFG[gi][1]
    nsb = ATT_TB // (DSWA_HALF * d)
    npair = DSWA_HEADS // 2
    q_spec = _att_specs(S, d, lambda p: gi * ATT_PAIRS + p)[1]
    k_specs = _att_specs(S, d, lambda p: npair + gi * ATT_PAIRS + p)
    v_specs = _att_specs(S, d, lambda p: 2 * npair + gi * ATT_PAIRS + p)

    def body(q_ref, kp, kc, kn, vp, vc, vn, b_ref, o_ref, lse_ref):
        tb = pl.program_id(1)
        masks, lane = _head_masks()
        kpc = _Pieces(kp, kc, kn, d, cast=BF16)
        vpc = _Pieces(vp, vc, vn, d, cast=BF16)
        scale = DSWA_E ** -0.5
        for grp in _tile_groups(d, nsb):
            rows = [_tile_rows(r, sb, d) for r, sb in grp]
            qs = [q_ref[rw, :] for rw in rows]
            kws = [kpc.window(r, sb) for r, sb in grp]
            vws = [vpc.window(r, sb) for r, sb in grp]
            valids = [_tile_valid(tb, r, sb, d, S, False) for r, sb in grp]
            both = [(t, hh) for t in range(len(grp)) for hh in range(2)]
            ss = [_dot_nt(_bf(jnp.where(masks[hh], qs[t], 0.0)), kws[t]) * scale + b_ref[0, hh] for t, hh in both]
            ss = [jnp.where(valids[t], s, NEG_INF) for (t, hh), s in zip(both, ss)]
            ms = [jnp.max(s, axis=-1, keepdims=True) for s in ss]
            ps = [jnp.exp(s - m) for s, m in zip(ss, ms)]
            ls = [jnp.sum(p, axis=-1, keepdims=True) for p in ps]
            os = [_dot(_bf(p / l), vws[t]) for (t, hh), p, l in zip(both, ps, ls)]
            for t, rw in enumerate(rows):
                o_ref[rw, :] = jnp.where(masks[0], os[2 * t], os[2 * t + 1])
                lse_ref[0, rw, :] = (jnp.where(lane == 0, ms[2 * t] + jnp.log(ls[2 * t]), 0.0)
                                     + jnp.where(lane == 1, ms[2 * t + 1] + jnp.log(ls[2 * t + 1]), 0.0))

    return pl.pallas_call(
        body, grid=(ATT_PAIRS, S // ATT_TB),
        in_specs=[q_spec, *k_specs, *v_specs, BS((1, 2, ATT_Q, ATT_W), lambda p, tb: (p, 0, 0, 0))],
        out_specs=[BS((ATT_TB, LANES), lambda p, tb: (tb, p)), BS((1, ATT_TB, LANES), lambda p, tb: (p, tb, 0))],
        out_shape=[SDS((S, DSWA_HG * DSWA_E), F32), SDS((ATT_PAIRS, S, LANES), F32)],
        name=name, compiler_params=_cp(("parallel", "parallel")))(qkv, qkv, qkv, qkv, qkv, qkv, qkv, bias)


def attn_bwd_q(qkv, bias, lse, do, dd, gi, name):
    S = qkv.shape[0]
    d = DSWA_CFG[gi][1]
    nsb = ATT_TB // (DSWA_HALF * d)
    npair = DSWA_HEADS // 2
    q_spec = _att_specs(S, d, lambda p: gi * ATT_PAIRS + p)[1]
    k_specs = _att_specs(S, d, lambda p: npair + gi * ATT_PAIRS + p)
    v_specs = _att_specs(S, d, lambda p: 2 * npair + gi * ATT_PAIRS + p)
    bspec = BS((1, 2, ATT_Q, ATT_W), lambda p, tb: (p, 0, 0, 0))

    def body(q_ref, kp, kc, kn, vp, vc, vn, b_ref, lse_ref, do_ref, dd_ref, dq_ref, db_ref):
        tb = pl.program_id(1)
        masks, lane = _head_masks()
        kpc = _Pieces(kp, kc, kn, d, cast=BF16)
        vpc = _Pieces(vp, vc, vn, d, cast=BF16)
        db = [jnp.zeros((ATT_Q, ATT_W), F32), jnp.zeros((ATT_Q, ATT_W), F32)]
        scale = DSWA_E ** -0.5
        for grp in _tile_groups(d, nsb):
            rows = [_tile_rows(r, sb, d) for r, sb in grp]
            qs = [q_ref[rw, :] for rw in rows]
            dos = [do_ref[0, rw, :] for rw in rows]
            lses = [lse_ref[0, rw, :] for rw in rows]
            dds = [dd_ref[0, 0, rw, :] for rw in rows]
            kws = [kpc.window(r, sb) for r, sb in grp]
            vws = [vpc.window(r, sb) for r, sb in grp]
            valids = [_tile_valid(tb, r, sb, d, S, False) for r, sb in grp]
            both = [(t, hh) for t in range(len(grp)) for hh in range(2)]
            ss = [_dot_nt(_bf(jnp.where(masks[hh], qs[t], 0.0)), kws[t]) * scale + b_ref[0, hh] for t, hh in both]
            dps = [_dot_nt(_bf(jnp.where(masks[hh], dos[t], 0.0)), vws[t]) for t, hh in both]
            ps = [jnp.exp(jnp.where(valids[t], s - lses[t][:, hh:hh + 1], NEG_INF)) for (t, hh), s in zip(both, ss)]
            dss = [p * (dp - dds[t][:, hh:hh + 1]) for (t, hh), p, dp in zip(both, ps, dps)]
            dqs = [_dot(_bf(ds), kws[t]) * scale for (t, hh), ds in zip(both, dss)]
            for t, rw in enumerate(rows):
                dq_ref[rw, :] = jnp.where(masks[0], dqs[2 * t], dqs[2 * t + 1])
                db[0] = db[0] + dss[2 * t]
                db[1] = db[1] + dss[2 * t + 1]

        @pl.when(tb == 0)
        def _():
            db_ref[0, 0] = db[0]
            db_ref[0, 1] = db[1]

        @pl.when(tb > 0)
        def _():
            db_ref[0, 0] += db[0]
            db_ref[0, 1] += db[1]

    return pl.pallas_call(
        body, grid=(ATT_PAIRS, S // ATT_TB),
        in_specs=[q_spec, *k_specs, *v_specs, bspec, BS((1, ATT_TB, LANES), lambda p, tb: (p, tb, 0)),
                  BS((1, ATT_TB, LANES), lambda p, tb: (gi, tb, p)), BS((1, 1, ATT_TB, LANES), lambda p, tb: (gi, p, tb, 0))],
        out_specs=[BS((ATT_TB, LANES), lambda p, tb: (tb, p)), bspec],
        out_shape=[SDS((S, DSWA_HG * DSWA_E), F32), SDS((ATT_PAIRS, 2, ATT_Q, ATT_W), F32)],
        name=name, compiler_params=_cp(("parallel", "arbitrary")))(qkv, qkv, qkv, qkv, qkv, qkv, qkv, bias, lse, do, dd)


def attn_bwd_kv(qkv, bias_t, lse, do, dd, gi, name):
    S = qkv.shape[0]
    d = DSWA_CFG[gi][1]
    nsb = ATT_TB // (DSWA_HALF * d)
    npair = DSWA_HEADS // 2
    q_specs = _att_specs(S, d, lambda p: gi * ATT_PAIRS + p)
    k_spec = _att_specs(S, d, lambda p: npair + gi * ATT_PAIRS + p)[1]
    v_spec = _att_specs(S, d, lambda p: 2 * npair + gi * ATT_PAIRS + p)[1]
    halo = DSWA_HALF * d
    per = ATT_TB // halo
    last = S // halo - 1

    def do_spec(rows, blk):
        return BS((1, rows, LANES), lambda p, tb: (gi, blk(tb), p))

    def dd_spec(rows, blk):
        return BS((1, 1, rows, LANES), lambda p, tb: (gi, p, blk(tb), 0))

    blks = [(halo, lambda tb: jnp.maximum(tb * per - 1, 0)), (ATT_TB, lambda tb: tb),
            (halo, lambda tb: jnp.minimum((tb + 1) * per, last))]
    do_specs = [do_spec(*b) for b in blks]
    dd_specs = [dd_spec(*b) for b in blks]
    lse_specs = _att_specs3(S, d, lambda p: p)

    class _Lead4:
        def __init__(self, ref):
            self.ref = ref

        def __getitem__(self, idx):
            return self.ref[(0,) + idx]

    def body(k_ref, v_ref, qp, qc, qn, dop, doc, don, lp, lc, ln, ddp, ddc, ddn, b_ref, dk_ref, dv_ref):
        tb = pl.program_id(1)
        masks, lane = _head_masks()
        qpc = _Pieces(qp, qc, qn, d)
        dopc = _Pieces(dop, doc, don, d, lead=True)
        lpc = _Pieces(lp, lc, ln, d, lead=True)
        ddpc = _Pieces(_Lead4(ddp), _Lead4(ddc), _Lead4(ddn), d, lead=True)
        scale = DSWA_E ** -0.5
        for grp in _tile_groups(d, nsb):
            rows = [_tile_rows(r, sb, d) for r, sb in grp]
            kcs = [_bf(k_ref[rw, :]) for rw in rows]
            vcs = [_bf(v_ref[rw, :]) for rw in rows]
            qws = [qpc.window(r, sb) for r, sb in grp]
            dows = [dopc.window(r, sb) for r, sb in grp]
            lws = [lpc.window(r, sb) for r, sb in grp]
            ddws = [ddpc.window(r, sb) for r, sb in grp]
            qwbs = [_bf(x) for x in qws]
            dowbs = [_bf(x) for x in dows]
            valids = [_tile_valid(tb, r, sb, d, S, True) for r, sb in grp]
            both = [(t, hh) for t in range(len(grp)) for hh in range(2)]
            ss = [_dot_nt(_bf(jnp.where(masks[hh], qws[t], 0.0)), kcs[t]) * scale + b_ref[0, hh] for t, hh in both]
            dps = [_dot_nt(_bf(jnp.where(masks[hh], dows[t], 0.0)), vcs[t]) for t, hh in both]
            ps = [jnp.exp(jnp.where(valids[t], s - lws[t][:, hh:hh + 1], NEG_INF)) for (t, hh), s in zip(both, ss)]
            dvs = [_dot_tn(_bf(p), dowbs[t]) for (t, hh), p in zip(both, ps)]
            dss = [p * (dp - ddws[t][:, hh:hh + 1]) for (t, hh), p, dp in zip(both, ps, dps)]
            dks = [_dot_tn(_bf(ds), qwbs[t]) * scale for (t, hh), ds in zip(both, dss)]
            for t, rw in enumerate(rows):
                dk_ref[rw, :] = jnp.where(masks[0], dks[2 * t], dks[2 * t + 1])
                dv_ref[rw, :] = jnp.where(masks[0], dvs[2 * t], dvs[2 * t + 1])

    out = BS((ATT_TB, LANES), lambda p, tb: (tb, p))
    return pl.pallas_call(
        body, grid=(ATT_PAIRS, S // ATT_TB),
        in_specs=[k_spec, v_spec, *q_specs, *do_specs, *lse_specs, *dd_specs,
                  BS((1, 2, ATT_W, ATT_Q), lambda p, tb: (p, 0, 0, 0))],
        out_specs=[out, out],
        out_shape=[SDS((S, DSWA_HG * DSWA_E), F32), SDS((S, DSWA_HG * DSWA_E), F32)],
        name=name, compiler_params=_cp(("parallel", "parallel")))(
            qkv, qkv, qkv, qkv, qkv, do, do, do, lse, lse, lse, dd, dd, dd, bias_t)


def _pair_alphas(lses):
    m = jnp.maximum(jnp.maximum(lses[0], lses[1]), lses[2])
    e = [jnp.exp(t - m) for t in lses]
    tot = e[0] + e[1] + e[2]
    return [t / tot for t in e]


def _pair_expand(a, lane):
    return jnp.where(lane < DSWA_E, a[:, 0:1], a[:, 1:2])


def combine_fwd(o_raw, lse, name):
    S = o_raw.shape[0]
    tm = min(1024, S)

    def body(o_ref, l_ref, y_ref):
        g = pl.program_id(2)
        lane = lax.broadcasted_iota(jnp.int32, (1, LANES), 1)
        alphas = _pair_alphas([l_ref[0, 0], l_ref[1, 0], l_ref[2, 0]])
        a = jnp.where(g == 0, alphas[0], jnp.where(g == 1, alphas[1], alphas[2]))
        y_ref[...] = (o_ref[...] * _pair_expand(a, lane)).astype(BF16)

    blk = BS((tm, LANES), lambda i, p, g: (i, g * ATT_PAIRS + p))
    return pl.pallas_call(
        body, grid=(S // tm, ATT_PAIRS, 3),
        in_specs=[blk, BS((3, 1, tm, LANES), lambda i, p, g: (0, p, i, 0))], out_specs=blk,
        out_shape=SDS((S, DSWA_W), BF16), name=name, compiler_params=_cp(("parallel", "parallel", "parallel")))(o_raw, lse)


def combine_bwd(o_raw, lse, dy, name):
    S = o_raw.shape[0]
    tm = min(512, S)

    def body(o0, o1, o2, l_ref, d0, d1, d2, do_ref, dd_ref):
        lane = lax.broadcasted_iota(jnp.int32, (1, LANES), 1)
        alphas = _pair_alphas([l_ref[0, 0], l_ref[1, 0], l_ref[2, 0]])
        c = jnp.zeros((tm, LANES), F32)
        for g, (o_ref, dy_ref) in enumerate(((o0, d0), (o1, d1), (o2, d2))):
            dyv = dy_ref[...]
            do_ref[g] = dyv * _pair_expand(alphas[g], lane)
            prod = o_ref[...] * dyv
            dal = (jnp.where(lane == 0, jnp.sum(jnp.where(lane < DSWA_E, prod, 0.0), axis=1, keepdims=True), 0.0)
                   + jnp.where(lane == 1, jnp.sum(jnp.where(lane >= DSWA_E, prod, 0.0), axis=1, keepdims=True), 0.0))
            c = c + alphas[g] * dal
        for g in range(3):
            dd_ref[g, 0] = alphas[g] * c

    def col(g):
        return BS((tm, LANES), lambda i, p: (i, g * ATT_PAIRS + p))

    return pl.pallas_call(
        body, grid=(S // tm, ATT_PAIRS),
        in_specs=[col(0), col(1), col(2), BS((3, 1, tm, LANES), lambda i, p: (0, p, i, 0)), col(0), col(1), col(2)],
        out_specs=[BS((3, tm, LANES), lambda i, p: (0, i, p)), BS((3, 1, tm, LANES), lambda i, p: (0, p, i, 0))],
        out_shape=[SDS((3, S, DSWA_HG * DSWA_E), F32), SDS((3, ATT_PAIRS, S, LANES), F32)],
        name=name, compiler_params=_cp(("parallel", "parallel")))(o_raw, o_raw, o_raw, lse, dy, dy, dy)


def dswa_fwd(x, g, w_in, w_out, rel_table, tag):
    h, ht = rms_fwd(x, g, f"{tag}_rms")
    qkv = mm(h, w_in, name=f"{tag}_qkv", tn=1152)
    outs, lses = [], []
    for gi in range(3):
        bias, _ = _bias_mats(rel_table, gi)
        o, lse = attn_fwd(qkv, bias, gi, f"{tag}_att{gi}")
        outs.append(o)
        lses.append(lse)
    o_raw = jnp.concatenate(outs, axis=1)
    lse = jnp.stack(lses)
    y = combine_fwd(o_raw, lse, f"{tag}_comb")
    xn = mm(y, w_out, name=f"{tag}_out", epi=_add, extras=(x,))
    return xn, (ht, qkv, o_raw, lse, y)


def dswa_bwd(x, g, w_in, w_out, rel_table, saved, dx, dxb, tag):
    ht, qkv, o_raw, lse, y = saved
    dw_out = mm(y, dxb, name=f"{tag}_dwout", ta=True, tm=384)
    dy = mm(dxb, w_out, name=f"{tag}_dy", tb=True, tn=384)
    do_raw, dd = combine_bwd(o_raw, lse, dy, f"{tag}_combb")
    dqs, dks, dvs = [], [], []
    drel = jnp.zeros_like(rel_table)
    for gi in range(3):
        (bias, bias_t), bias_vjp = jax.vjp(lambda tbl: _bias_mats(tbl, gi), rel_table)
        dq, dbias = attn_bwd_q(qkv, bias, lse[gi], do_raw, dd, gi, f"{tag}_attq{gi}")
        dk, dv = attn_bwd_kv(qkv, bias_t, lse[gi], do_raw, dd, gi, f"{tag}_attkv{gi}")
        drel = drel + bias_vjp((dbias, jnp.zeros_like(bias_t)))[0]
        dqs.append(dq)
        dks.append(dk)
        dvs.append(dv)
    dqkv = jnp.concatenate(dqs + dks + dvs, axis=1).astype(BF16)
    dw_in = mm(ht, dqkv, name=f"{tag}_dwin", tn=384)
    dh = mm(dqkv, w_in, name=f"{tag}_dh", tb=True)
    dx, dxb, dg = rms_bwd(x, g, dh, dx, f"{tag}_rmsb")
    return dx, dxb, dg, dw_in, dw_out, drel


def adamw(w, g, m, v, name):
    shape = w.shape
    last = shape[-1]
    w2, g2, m2, v2 = (t.reshape(-1, last) for t in (w, g, m, v))
    rows = w2.shape[0]
    tr = rows
    if rows > 512:
        tr = next(t for t in (512, 256, 192, 128, 64, 8) if rows % t == 0)
    c1 = 1.0 / (1.0 - ADAM_B1 ** ADAM_STEP)
    c2 = 1.0 / (1.0 - ADAM_B2 ** ADAM_STEP)

    def body(w_ref, g_ref, m_ref, v_ref, d_ref, nm_ref, nv_ref):
        gv = g_ref[...]
        nm = ADAM_B1 * m_ref[...] + (1.0 - ADAM_B1) * gv
        nv = ADAM_B2 * v_ref[...] + (1.0 - ADAM_B2) * (gv * gv)
        nm_ref[...] = nm
        nv_ref[...] = nv
        d_ref[...] = -ADAM_LR * ((nm * c1) / (jnp.sqrt(nv * c2) + ADAM_EPS) + ADAM_WD * w_ref[...])

    spec = BS((tr, last), lambda i: (i, 0))
    outs = pl.pallas_call(
        body, grid=(rows // tr,), in_specs=[spec] * 4, out_specs=[spec] * 3,
        out_shape=[SDS((rows, last), F32)] * 3, name=name, compiler_params=_cp(("parallel",)))(w2, g2, m2, v2)
    return tuple(o.reshape(shape) for o in outs)


def _place():
    x, y, c = lax.axis_index("x"), lax.axis_index("y"), lax.axis_index("c")
    chips = [(1 - x, y), (x, 1 - y), (1 - x, 1 - y)]
    return x, y, c, chips


def _rcopy(src, dst, ssem, rsem, dev):
    return pltpu.make_async_remote_copy(src_ref=src, dst_ref=dst, send_sem=ssem, recv_sem=rsem, device_id=dev,
                                        device_id_type=MESH)


class SideJob(NamedTuple):
    ins: list
    outs: list
    sems: list
    start: Callable
    wait: Callable


def _job(ins, outs, sems, copies):
    def start(in_refs, out_refs, sem_refs):
        for cp in copies(in_refs, out_refs, sem_refs):
            cp.start()

    def wait(in_refs, out_refs, sem_refs):
        for cp in copies(in_refs, out_refs, sem_refs):
            cp.wait()

    return SideJob(list(ins), list(outs), list(sems), start, wait)


def gather_job(packs, halved):
    n = len(packs)
    dma = pltpu.SemaphoreType.DMA

    def copies(in_refs, out_refs, sems):
        ssem, rsem = sems
        x, y, c, chips = _place()
        jme = 2 * x + y
        cps = []
        for i, (p_ref, f_ref) in enumerate(zip(in_refs, out_refs)):
            rows = p_ref.shape[0]
            mine = pl.ds(c * (rows // 2), rows // 2) if halved[i] else pl.ds(0, rows)
            for r, (cx, cy) in enumerate(chips):
                cps.append(_rcopy(p_ref.at[mine], f_ref.at[jme, mine], ssem.at[i, r], rsem.at[i, r], (cx, cy, c)))
        return cps

    return _job(packs, [SDS((4,) + p.shape, p.dtype) for p in packs], [dma((n, 3)), dma((n, 3))], copies)


def chip_exchange_job(parts):
    n = len(parts)
    dma = pltpu.SemaphoreType.DMA

    def copies(in_refs, out_refs, sems):
        ssem, rsem = sems
        x, y, c, chips = _place()
        cps = []
        for i, (p_ref, r_ref) in enumerate(zip(in_refs, out_refs)):
            for r, (cx, cy) in enumerate(chips):
                cps.append(_rcopy(p_ref.at[2 * cx + cy], r_ref.at[r], ssem.at[i, r], rsem.at[i, r], (cx, cy, c)))
        return cps

    return _job(parts, [SDS((3,) + p.shape[1:], p.dtype) for p in parts], [dma((n, 3)), dma((n, 3))], copies)


def run_job(job, name):
    ni, no = len(job.ins), len(job.outs)

    def body(*refs):
        job.start(refs[:ni], refs[ni:ni + no], refs[ni + no:])
        job.wait(refs[:ni], refs[ni:ni + no], refs[ni + no:])

    return pl.pallas_call(
        body, in_specs=[ANY] * ni, out_specs=[ANY] * no, out_shape=job.outs, scratch_shapes=job.sems, name=name,
        compiler_params=pltpu.CompilerParams(has_side_effects=True))(*job.ins)


def forward_to_sibling(fulls, name):
    n = len(fulls)

    def body(*refs):
        in_refs, out_refs, (ssem, rsem) = refs[:n], refs[n:2 * n], refs[2 * n:]
        x, y, c, chips = _place()
        cps = []
        for i in range(n):
            half = in_refs[i].shape[1] // 2
            for r, (cx, cy) in enumerate(chips):
                piece = (2 * cx + cy, pl.ds(c * half, half))
                cps.append(_rcopy(in_refs[i].at[piece], out_refs[i].at[piece], ssem.at[i, r], rsem.at[i, r], (x, y, 1 - c)))
        for cp in cps:
            cp.start()
        for cp in cps:
            cp.wait()

    dma = pltpu.SemaphoreType.DMA
    return pl.pallas_call(
        body, in_specs=[ANY] * n, out_specs=[ANY] * n, out_shape=[SDS(f.shape, f.dtype) for f in fulls],
        scratch_shapes=[dma((n, 3)), dma((n, 3))], input_output_aliases={i: i for i in range(n)}, name=name,
        compiler_params=pltpu.CompilerParams(has_side_effects=True))(*fulls)


def rs_sibling_exchange(gpack, name):
    _, rows, W = gpack.shape
    half = rows // 2

    def body(g_ref, r_ref, ssem, rsem):
        x, y, c, _ = _place()
        cps = [_rcopy(g_ref.at[j, pl.ds((1 - c) * half, half)], r_ref.at[j], ssem.at[j], rsem.at[j], (x, y, 1 - c))
               for j in range(4)]
        for cp in cps:
            cp.start()
        for cp in cps:
            cp.wait()

    dma = pltpu.SemaphoreType.DMA
    return pl.pallas_call(
        body, in_specs=[ANY], out_specs=ANY, out_shape=SDS((4, half, W), gpack.dtype),
        scratch_shapes=[dma((4,)), dma((4,))], name=name,
        compiler_params=pltpu.CompilerParams(has_side_effects=True))(gpack)


def _div_tile(n, limit):
    return next(t for t in range(limit - limit % 16, 0, -16) if n % t == 0)


def rs_add_sibling(gpack, recv, cidx, name, out_dtype=F32):
    _, rows, W = gpack.shape
    half = rows // 2
    tr = _div_tile(half, 1024)
    nb = half // tr

    def body(c_ref, g_ref, r_ref, o_ref):
        o_ref[...] = (g_ref[...].astype(F32) + r_ref[...].astype(F32)).astype(o_ref.dtype)

    gs = pltpu.PrefetchScalarGridSpec(
        num_scalar_prefetch=1, grid=(4, nb),
        in_specs=[BS((1, tr, W), lambda j, i, c: (j, c[0] * nb + i, 0)), BS((1, tr, W), lambda j, i, c: (j, i, 0))],
        out_specs=BS((1, tr, W), lambda j, i, c: (j, i, 0)))
    return pl.pallas_call(body, grid_spec=gs, out_shape=SDS((4, half, W), out_dtype), name=name,
                          compiler_params=_cp(("parallel", "parallel")))(cidx, gpack, recv)


def rs_add_chips(recv, part, place, name):
    _, half, W = recv.shape
    tr = _div_tile(half, 640)
    nb = half // tr

    def body(x_ref, y_ref, c_ref, r_ref, own_ref, o_ref):
        r0, r1, r2, own = (t.astype(F32) for t in (r_ref[0], r_ref[1], r_ref[2], own_ref[0]))
        o_ref[...] = ((r0 + r1) + r2) + own

    gs = pltpu.PrefetchScalarGridSpec(
        num_scalar_prefetch=3, grid=(nb,),
        in_specs=[BS((3, tr, W), lambda i, x, y, c: (0, i, 0)), BS((1, tr, W), lambda i, x, y, c: (2 * x[0] + y[0], i, 0))],
        out_specs=BS((tr, W), lambda i, x, y, c: (c[0] * nb + i, 0)))
    return pl.pallas_call(body, grid_spec=gs, out_shape=SDS((2 * half, W), F32), name=name,
                          compiler_params=_cp(("parallel",)))(*place, recv, part)


def rs_sibling_share(gsh, name):
    rows, W = gsh.shape
    half = rows // 2

    def body(g_ref, o_ref, ssem, rsem):
        x, y, c, _ = _place()
        mine = pl.ds(c * half, half)
        cp = _rcopy(g_ref.at[mine], o_ref.at[mine], ssem, rsem, (x, y, 1 - c))
        cp.start()
        cp.wait()

    dma = pltpu.SemaphoreType.DMA
    return pl.pallas_call(
        body, in_specs=[ANY], out_specs=ANY, out_shape=SDS(gsh.shape, gsh.dtype),
        scratch_shapes=[dma, dma], input_output_aliases={0: 0}, name=name,
        compiler_params=pltpu.CompilerParams(has_side_effects=True))(gsh)


def allreduce_small(pack):
    R = pack.shape[0]

    def body(p_ref, o_ref, all_ref, ssem, rsem):
        x, y, c, _ = _place()
        me = 4 * x + 2 * y + c
        all_ref[me] = p_ref[...]
        cps = []
        for m in range(1, 8):
            peer = (1 - x if m & 4 else x, 1 - y if m & 2 else y, 1 - c if m & 1 else c)
            cp = _rcopy(p_ref, all_ref.at[me], ssem.at[m - 1], rsem.at[m - 1], peer)
            cp.start()
            cps.append(cp)
        for cp in cps:
            cp.wait()
        acc = all_ref[0]
        for i in range(1, 8):
            acc = acc + all_ref[i]
        o_ref[...] = acc

    dma = pltpu.SemaphoreType.DMA
    vm = BS(memory_space=pltpu.VMEM)
    return pl.pallas_call(
        body, in_specs=[vm], out_specs=vm, out_shape=SDS(pack.shape, F32),
        scratch_shapes=[pltpu.VMEM((8, R, LANES), F32), dma((7,)), dma((7,))], name="allreduce_small",
        compiler_params=pltpu.CompilerParams(has_side_effects=True))(pack)


PACK_W = 1024
PACK_ALIGN = 32


def _layer_entries(l):
    if l % 2 == 0:
        mixer = [("gdn_w_in", l // 2, D_MODEL, GDN_IN // 4, True), ("gdn_w_out", l // 2, D_MODEL // 4, D_MODEL, False)]
    else:
        mixer = [("dswa_w_in", l // 2, D_MODEL, 3 * DSWA_W // 4, True), ("dswa_w_out", l // 2, DSWA_W // 4, D_MODEL, False)]
    return mixer + [("mlp_w1", l, D_MODEL, D_FF // 4, True), ("mlp_w2", l, D_FF // 4, D_MODEL, False)]


def _entries_offsets(entries):
    offs = [int(o) for o in np.cumsum([0] + [r * c // PACK_W for (_, _, r, c, _) in entries])]
    return offs, -(-offs[-1] // PACK_ALIGN) * PACK_ALIGN


def _layer_offsets(l):
    return _entries_offsets(_layer_entries(l))


def _pack_layer(l, shards, dtype):
    offs, total = _layer_offsets(l)
    parts = [shards[name][li].astype(dtype).reshape(-1, PACK_W) for (name, li, _, _, _) in _layer_entries(l)]
    parts.append(jnp.zeros((total - offs[-1], PACK_W), dtype))
    return jnp.concatenate(parts, axis=0)


def _unpack_layer(l, full, own, jme):
    offs, _ = _layer_offsets(l)
    mats = []
    for e, (_, _, r, c, by_col) in enumerate(_layer_entries(l)):
        mine = own[offs[e]:offs[e + 1]]
        sh = [jnp.where(jme == j, mine, full[j, offs[e]:offs[e + 1]]).reshape(r, c) for j in range(4)]
        mats.append(jnp.concatenate(sh, axis=1 if by_col else 0))
    return mats


def _pack_grads(entries, grads):
    offs, total = _entries_offsets(entries)
    per_chip = []
    for j in range(4):
        parts = []
        for g, (_, _, r, c, by_col) in zip(grads, entries):
            sh = g[:, c * j:c * (j + 1)] if by_col else g[r * j:r * (j + 1), :]
            parts.append(sh.astype(BF16).reshape(-1, PACK_W))
        parts.append(jnp.zeros((total - offs[-1], PACK_W), BF16))
        per_chip.append(jnp.concatenate(parts, axis=0))
    return jnp.stack(per_chip)


def _unpack_shard_grads(units):
    out = {}
    for entries, gsh in units:
        offs, _ = _entries_offsets(entries)
        for e, (name, _, r, c, _) in enumerate(entries):
            out.setdefault(name, []).append(gsh[offs[e]:offs[e + 1]].reshape(r, c))
    return {k: jnp.stack(v) for k, v in out.items()}


def _flat_pad(t, mult=8 * LANES):
    f = t.reshape(-1)
    return jnp.pad(f, (0, (-f.shape[0]) % mult))


def kernel(x, norm_mix, norm_mlp, norm_final, rel_bias, gdn_w_in, gdn_conv_w, gdn_a_log, gdn_dt_bias, gdn_norm_w, gdn_w_out, dswa_w_in, dswa_w_out, mlp_w1, mlp_w2, loss_target, m_norm_mix, m_norm_mlp, m_norm_final, m_rel_bias, m_gdn_w_in, m_gdn_conv_w, m_gdn_a_log, m_gdn_dt_bias, m_gdn_norm_w, m_gdn_w_out, m_dswa_w_in, m_dswa_w_out, m_mlp_w1, m_mlp_w2, v_norm_mix, v_norm_mlp, v_norm_final, v_rel_bias, v_gdn_w_in, v_gdn_conv_w, v_gdn_a_log, v_gdn_dt_bias, v_gdn_norm_w, v_gdn_w_out, v_dswa_w_in, v_dswa_w_out, v_mlp_w1, v_mlp_w2):
    xi, yi, ci = lax.axis_index("x"), lax.axis_index("y"), lax.axis_index("c")
    jme = 2 * xi + yi
    big = dict(gdn_w_in=gdn_w_in, gdn_w_out=gdn_w_out, dswa_w_in=dswa_w_in, dswa_w_out=dswa_w_out, mlp_w1=mlp_w1, mlp_w2=mlp_w2)
    n_gdn = gdn_w_in.shape[0]
    conv_cols = gdn_conv_w.shape[-1]

    packs = [_pack_layer(l, big, BF16) for l in range(DEPTH)]
    convp = jnp.pad(gdn_conv_w.reshape(n_gdn * GDN_CONV, conv_cols), ((0, 16 - n_gdn * GDN_CONV), (0, 0)))
    raw0, cfull = run_job(gather_job([packs[0], convp], [True, False]), "gather_l0")
    fulls = {0: forward_to_sibling([raw0], "forward_l0")[0]}
    cfull = jnp.where((jnp.arange(4) == jme)[:, None, None], convp[None], cfull)
    conv_all = jnp.transpose(cfull[:, :n_gdn * GDN_CONV], (1, 0, 2)).reshape(n_gdn, GDN_CONV, 4 * conv_cols)
    conv_all = jnp.pad(conv_all, ((0, 0), (0, 8 - GDN_CONV), (0, 0)))
    fwd_jobs = {0: [1, 2], 2: [3]}

    xs = x[0]
    saved = []
    for l in range(DEPTH):
        w_in, w_out, w1, w2 = _unpack_layer(l, fulls[l], packs[l], jme)
        gm, gp = norm_mix[l][None], norm_mlp[l][None]
        a = l // 2
        if l % 2 == 0:
            w_in = jnp.pad(w_in, ((0, 0), (0, GDN_INP - GDN_IN)))
            job = gather_job([packs[t] for t in fwd_jobs[l]], [True] * len(fwd_jobs[l]))
            x_mid, sv, raws = gdn_fwd(xs, gm, w_in, conv_all[a], gdn_a_log[a], gdn_dt_bias[a], gdn_norm_w[a][None], w_out,
                                      f"l{l}_gdn", job)
            for t, f in zip(fwd_jobs[l], forward_to_sibling(raws, f"forward_from_l{l}")):
                fulls[t] = f
        else:
            x_mid, sv = dswa_fwd(xs, gm, w_in, w_out, rel_bias, f"l{l}_att")
        x_out, sv2 = mlp_fwd(x_mid, gp, w1, w2, f"l{l}_mlp")
        saved.append((xs, x_mid, (w_in, w_out, w1, w2), sv, sv2))
        xs = x_out

    cidx = ci.astype(jnp.int32).reshape(1)
    place = [t.astype(jnp.int32).reshape(1) for t in (xi, yi, ci)]
    units = {"0a": _layer_entries(0)[:2], "0b": _layer_entries(0)[2:], **{str(l): _layer_entries(l) for l in (1, 2, 3)}}

    def chip_partial(u, grads):
        gpack = _pack_grads(units[u], grads)
        return rs_add_sibling(gpack, rs_sibling_exchange(gpack, f"rs_sibling_{u}"), cidx, f"rs_add_sibling_{u}", BF16)

    def finish(u, recv):
        return rs_sibling_share(rs_add_chips(recv, parts[u], place, f"rs_add_chips_{u}"), f"rs_share_{u}")

    loss_part, dx, dxb, d_final = loss_head(xs, norm_final[None], loss_target[0], "loss_head")
    d_mix, d_mlp = [None] * DEPTH, [None] * DEPTH
    d_conv, d_alog, d_dt, d_nw = [None] * n_gdn, [None] * n_gdn, [None] * n_gdn, [None] * n_gdn
    d_rel = jnp.zeros_like(rel_bias)
    parts, gshs = {}, {}
    bwd_jobs = {2: ["3"], 0: ["2", "1", "0b"]}
    for l in reversed(range(DEPTH)):
        x_in, x_mid, (w_in, w_out, w1, w2), sv, sv2 = saved[l]
        gm, gp = norm_mix[l][None], norm_mlp[l][None]
        a = l // 2
        dx, dxb, d_mlp[l], dw1, dw2 = mlp_bwd(x_mid, gp, w1, w2, sv2, dx, dxb, f"l{l}_mlp")
        if l == 0:
            parts["0b"] = chip_partial("0b", [dw1, dw2])
        if l % 2 == 0:
            job = chip_exchange_job([parts[u] for u in bwd_jobs[l]])
            dx, dxb, d_mix[l], dw_all, d_conv[a], d_alog[a], d_dt[a], d_nw[a], dwo, recvs = gdn_bwd(
                x_in, gm, w_in, conv_all[a], gdn_norm_w[a][None], w_out, sv, dx, dxb, f"l{l}_gdn", job)
            for u, rv in zip(bwd_jobs[l], recvs):
                gshs[u] = finish(u, rv)
            dwi = dw_all[:, :GDN_IN]
        else:
            dx, dxb, d_mix[l], dwi, dwo, drel = dswa_bwd(x_in, gm, w_in, w_out, rel_bias, sv, dx, dxb, f"l{l}_att")
            d_rel = d_rel + drel
        if l == 0:
            parts["0a"] = chip_partial("0a", [dwi, dwo])
        else:
            parts[str(l)] = chip_partial(str(l), [dwi, dwo, dw1, dw2])
    gshs["0a"] = finish("0a", run_job(chip_exchange_job([parts["0a"]]), "rs_chip_exchange_0a")[0])
    gbig = _unpack_shard_grads([(units[u], gshs[u]) for u in ("0a", "0b", "1", "2", "3")])

    small = [jnp.concatenate(d_mix, axis=0), jnp.concatenate(d_mlp, axis=0), d_final, d_rel,
             jnp.stack(d_conv), jnp.stack(d_alog), jnp.stack(d_dt), jnp.concatenate(d_nw, axis=0)]
    flat = [_flat_pad(t) for t in small]
    sizes = [f.shape[0] for f in flat]
    red = allreduce_small(jnp.concatenate(flat).reshape(-1, LANES)).reshape(-1)
    offs = np.cumsum([0] + sizes)
    red = [red[offs[i]:offs[i] + small[i].size].reshape(small[i].shape) for i in range(len(small))]
    g_conv_all = red[4][:, :GDN_CONV].reshape(n_gdn, GDN_CONV, 1, 4 * conv_cols)
    g_conv = lax.dynamic_slice_in_dim(g_conv_all, jme * conv_cols, conv_cols, axis=3)
    g = dict(norm_mix=red[0], norm_mlp=red[1], norm_final=red[2].reshape(norm_final.shape), rel_bias=red[3],
             gdn_conv_w=g_conv, gdn_a_log=red[5], gdn_dt_bias=red[6], gdn_norm_w=red[7][:, :GDN_DK], **gbig)

    w = dict(norm_mix=norm_mix, norm_mlp=norm_mlp, norm_final=norm_final, rel_bias=rel_bias, gdn_conv_w=gdn_conv_w,
             gdn_a_log=gdn_a_log, gdn_dt_bias=gdn_dt_bias, gdn_norm_w=gdn_norm_w, **big)
    m = dict(norm_mix=m_norm_mix, norm_mlp=m_norm_mlp, norm_final=m_norm_final, rel_bias=m_rel_bias, gdn_w_in=m_gdn_w_in,
             gdn_conv_w=m_gdn_conv_w, gdn_a_log=m_gdn_a_log, gdn_dt_bias=m_gdn_dt_bias, gdn_norm_w=m_gdn_norm_w,
             gdn_w_out=m_gdn_w_out, dswa_w_in=m_dswa_w_in, dswa_w_out=m_dswa_w_out, mlp_w1=m_mlp_w1, mlp_w2=m_mlp_w2)
    v = dict(norm_mix=v_norm_mix, norm_mlp=v_norm_mlp, norm_final=v_norm_final, rel_bias=v_rel_bias, gdn_w_in=v_gdn_w_in,
             gdn_conv_w=v_gdn_conv_w, gdn_a_log=v_gdn_a_log, gdn_dt_bias=v_gdn_dt_bias, gdn_norm_w=v_gdn_norm_w,
             gdn_w_out=v_gdn_w_out, dswa_w_in=v_dswa_w_in, dswa_w_out=v_dswa_w_out, mlp_w1=v_mlp_w1, mlp_w2=v_mlp_w2)
    names = ["norm_mix", "norm_mlp", "norm_final", "rel_bias", "gdn_w_in", "gdn_conv_w", "gdn_a_log", "gdn_dt_bias",
             "gdn_norm_w", "gdn_w_out", "dswa_w_in", "dswa_w_out", "mlp_w1", "mlp_w2"]
    upd = {n: adamw(w[n], g[n], m[n], v[n], f"adamw_{n}") for n in names}
    loss = lax.psum(loss_part[0, 0], ("x", "y", "c"))
    return (loss, dx[None], *[g[n] for n in names], *[upd[n][0] for n in names], *[upd[n][1] for n in names],
            *[upd[n][2] for n in names])
```

```python
import math
from typing import Callable, NamedTuple

import numpy as np
import jax
import jax.numpy as jnp
from jax import lax
from jax.experimental import pallas as pl
from jax.experimental.pallas import tpu as pltpu

F32 = jnp.float32
BF16 = jnp.bfloat16
HI = lax.Precision.HIGHEST
BS = pl.BlockSpec
SDS = jax.ShapeDtypeStruct
MESH = pl.DeviceIdType.MESH
ANY = BS(memory_space=pl.ANY)

D_MODEL = 1024
D_FF = 4096
DEPTH = 4
RMS_EPS = 1e-6
NEG_INF = -1e30
LANES = 128
VMEM_LIMIT = 56 << 20

GDN_H = 8
GDN_DK = 128
GDN_CONV = 5
GDN_C = 64
GDN_GC = 8
GDN_HP_FWD = 8
GDN_HP_BWD = 4
GDN_QKV = 3 * GDN_H * GDN_DK
GDN_IN = GDN_QKV + GDN_H * GDN_DK + 4 * GDN_H
GDN_INP = 4224

DSWA_CFG = ((128, 1), (512, 4), (2048, 16))
DSWA_HG = 6
DSWA_E = 64
DSWA_HEADS = 18
DSWA_W = DSWA_HEADS * DSWA_E
DSWA_HALF = 64
REL_BUCKETS = 32
REL_MAX_DIST = 1024

ADAM_LR = 0.001
ADAM_B1 = 0.9
ADAM_B2 = 0.999
ADAM_EPS = 1e-08
ADAM_WD = 0.01
ADAM_STEP = 10


def _cp(sem=None):
    return pltpu.CompilerParams(dimension_semantics=sem, vmem_limit_bytes=VMEM_LIMIT)


def _dot(a, b, prec=None):
    return jnp.dot(a, b, precision=prec, preferred_element_type=F32)


def _dot_nt(a, b, prec=None):
    return lax.dot_general(a, b, (((1,), (1,)), ((), ())), precision=prec, preferred_element_type=F32)


def _dot_tn(a, b, prec=None):
    return lax.dot_general(a, b, (((0,), (0,)), ((), ())), precision=prec, preferred_element_type=F32)


def _bf(a):
    return a.astype(BF16)


def _sigmoid(x):
    return 1.0 / (1.0 + jnp.exp(-x))


def rms_fwd(x, g, name):
    S, Dm = x.shape
    tm = min(512, S)

    def body(x_ref, g_ref, o_ref, ot_ref):
        xv = x_ref[...]
        r = lax.rsqrt(jnp.mean(xv * xv, axis=-1, keepdims=True) + RMS_EPS)
        hb = (xv * r * g_ref[...]).astype(o_ref.dtype)
        o_ref[...] = hb
        ot_ref[...] = hb.T

    return pl.pallas_call(
        body, grid=(S // tm,),
        in_specs=[BS((tm, Dm), lambda i: (i, 0)), BS((1, Dm), lambda i: (0, 0))],
        out_specs=[BS((tm, Dm), lambda i: (i, 0)), BS((Dm, tm), lambda i: (0, i))],
        out_shape=[SDS((S, Dm), BF16), SDS((Dm, S), BF16)], name=name, compiler_params=_cp(("parallel",)))(x, g)


def rms_bwd(x, g, dh, dres, name):
    S, Dm = x.shape
    tm = min(512, S)

    def body(x_ref, g_ref, dh_ref, dres_ref, dx_ref, dxb_ref, dg_ref):
        i = pl.program_id(0)
        xv = x_ref[...]
        r = lax.rsqrt(jnp.mean(xv * xv, axis=-1, keepdims=True) + RMS_EPS)
        n = xv * r
        dhv = dh_ref[...]
        t = dhv * g_ref[...]
        dx = dres_ref[...] + r * (t - n * jnp.mean(n * t, axis=-1, keepdims=True))
        dx_ref[...] = dx
        dxb_ref[...] = dx.astype(BF16)
        part = jnp.sum(dhv * n, axis=0, keepdims=True)

        @pl.when(i == 0)
        def _():
            dg_ref[...] = part

        @pl.when(i > 0)
        def _():
            dg_ref[...] += part

    row = BS((tm, Dm), lambda i: (i, 0))
    vec = BS((1, Dm), lambda i: (0, 0))
    return pl.pallas_call(
        body, grid=(S // tm,), in_specs=[row, vec, row, row], out_specs=[row, row, vec],
        out_shape=[SDS((S, Dm), F32), SDS((S, Dm), BF16), SDS((1, Dm), F32)],
        name=name, compiler_params=_cp(("arbitrary",)))(x, g, dh, dres)


def loss_head(x, g, tgt, name):
    S, Dm = x.shape
    tm = min(512, S)

    def body(x_ref, g_ref, t_ref, loss_ref, dx_ref, dxb_ref, dg_ref):
        i = pl.program_id(0)
        xv = x_ref[...]
        gv = g_ref[...]
        r = lax.rsqrt(jnp.mean(xv * xv, axis=-1, keepdims=True) + RMS_EPS)
        n = xv * r
        err = n * gv - t_ref[...]
        lpart = 0.5 * jnp.sum(jnp.mean(err * err, axis=-1, keepdims=True), axis=0, keepdims=True)
        dout = err * (1.0 / Dm)
        t = dout * gv
        dx = r * (t - n * jnp.mean(n * t, axis=-1, keepdims=True))
        dx_ref[...] = dx
        dxb_ref[...] = dx.astype(BF16)
        part = jnp.sum(dout * n, axis=0, keepdims=True)

        @pl.when(i == 0)
        def _():
            dg_ref[...] = part
            loss_ref[...] = lpart

        @pl.when(i > 0)
        def _():
            dg_ref[...] += part
            loss_ref[...] += lpart

    row = BS((tm, Dm), lambda i: (i, 0))
    vec = BS((1, Dm), lambda i: (0, 0))
    one = BS((1, 1), lambda i: (0, 0))
    return pl.pallas_call(
        body, grid=(S // tm,), in_specs=[row, vec, row], out_specs=[one, row, row, vec],
        out_shape=[SDS((1, 1), F32), SDS((S, Dm), F32), SDS((S, Dm), BF16), SDS((1, Dm), F32)],
        name=name, compiler_params=_cp(("arbitrary",)))(x, g, tgt)


MM_DEEP_K = 2048
MM_SHALLOW_K = 1024


def mm(a, b, *, name, ta=False, tb=False, tm=1024, tn=512, out_dtype=F32, pre_a=None, epi=None, extras=()):
    M, K = (a.shape[1], a.shape[0]) if ta else a.shape
    N = b.shape[0] if tb else b.shape[1]
    if ta:
        tm = min(tm, 512)
    elif K > MM_DEEP_K and N % 256 == 0:
        tn = min(tn, 256)
    elif K <= MM_SHALLOW_K:
        tm = 2 * tm
    tm, tn = min(tm, M), min(tn, N)
    assert M % tm == 0 and N % tn == 0, (name, M, N, K, tm, tn)
    ne = len(extras)
    a_spec = BS((K, tm), lambda i, j: (0, i)) if ta else BS((tm, K), lambda i, j: (i, 0))
    b_spec = BS((tn, K), lambda i, j: (j, 0)) if tb else BS((K, tn), lambda i, j: (0, j))
    o_spec = BS((tm, tn), lambda i, j: (i, j))
    dims = (((0 if ta else 1,), (1 if tb else 0,)), ((), ()))

    def body(a_ref, b_ref, *rest):
        e_refs, o_ref = rest[:ne], rest[ne]
        av = a_ref[...]
        if pre_a is not None:
            av = pre_a(av)
        acc = lax.dot_general(_bf(av), _bf(b_ref[...]), dims, preferred_element_type=F32)
        res = epi(acc, *[e[...] for e in e_refs]) if epi is not None else acc
        o_ref[...] = res.astype(o_ref.dtype)

    return pl.pallas_call(
        body, grid=(M // tm, N // tn), in_specs=[a_spec, b_spec] + [o_spec] * ne, out_specs=o_spec,
        out_shape=SDS((M, N), out_dtype), name=name, compiler_params=_cp(("parallel", "parallel")))(a, b, *extras)


def _relu(acc):
    return jnp.maximum(acc, 0.0)


def _add(acc, res):
    return acc + res


def _sq(av):
    return av * av


def _times_2r(acc, r):
    return acc * (2.0 * r.astype(F32))


def mlp_fwd(x, g, w1, w2, tag):
    h, ht = rms_fwd(x, g, f"{tag}_rms")
    r = mm(h, w1, name=f"{tag}_up", tn=1024, out_dtype=BF16, epi=_relu)
    xn = mm(r, w2, name=f"{tag}_down", pre_a=_sq, epi=_add, extras=(x,))
    return xn, (ht, r)


def mlp_bwd(x, g, w1, w2, saved, dx, dxb, tag):
    ht, r = saved
    da = mm(dxb, w2, name=f"{tag}_dact", tb=True, tn=1024, out_dtype=BF16, epi=_times_2r, extras=(r,))
    dw2 = mm(r, dxb, name=f"{tag}_dw2", ta=True, pre_a=_sq)
    dw1 = mm(ht, da, name=f"{tag}_dw1")
    dx, dxb, dg = mm_rms_bwd(da, w1, x, g, dx, f"{tag}_dh_rmsb")
    return dx, dxb, dg, dw1, dw2


def mm_rms_bwd(da, w, x, g, dres, name):
    S, K = da.shape
    Dm = w.shape[0]
    tm = min(512, S)

    def body(a_ref, w_ref, x_ref, g_ref, dres_ref, dx_ref, dxb_ref, dg_ref):
        i = pl.program_id(0)
        dhv = _dot_nt(a_ref[...], w_ref[...])
        xv = x_ref[...]
        r = lax.rsqrt(jnp.mean(xv * xv, axis=-1, keepdims=True) + RMS_EPS)
        n = xv * r
        t = dhv * g_ref[...]
        dx = dres_ref[...] + r * (t - n * jnp.mean(n * t, axis=-1, keepdims=True))
        dx_ref[...] = dx
        dxb_ref[...] = dx.astype(BF16)
        part = jnp.sum(dhv * n, axis=0, keepdims=True)

        @pl.when(i == 0)
        def _():
            dg_ref[...] = part

        @pl.when(i > 0)
        def _():
            dg_ref[...] += part

    row = BS((tm, Dm), lambda i: (i, 0))
    vec = BS((1, Dm), lambda i: (0, 0))
    return pl.pallas_call(
        body, grid=(S // tm,),
        in_specs=[BS((tm, K), lambda i: (i, 0)), BS((Dm, K), lambda i: (0, 0)), row, vec, row],
        out_specs=[row, row, vec],
        out_shape=[SDS((S, Dm), F32), SDS((S, Dm), BF16), SDS((1, Dm), F32)],
        name=name, compiler_params=_cp(("arbitrary",)))(da, w, x, g, dres)


def _conv_taps(x, S):
    t = lax.broadcasted_iota(jnp.int32, x.shape, 0)
    taps = []
    for j in range(GDN_CONV):
        sh = j - GDN_CONV // 2
        xs = x if sh == 0 else pltpu.roll(x, (-sh) % S, 0)
        taps.append(jnp.where((t + sh >= 0) & (t + sh < S), xs, 0.0))
    return taps


def _qkv_scale(c):
    is_norm = c < 2 * GDN_H
    scale = jnp.where(c < GDN_H, GDN_DK ** -0.5, 1.0)
    return is_norm, scale


def gdn_pre_fwd(proj, convw, name):
    S = proj.shape[0]

    def body(p_ref, w_ref, o_ref):
        c = pl.program_id(0)
        x = p_ref[...]
        w = w_ref[...]
        y = jnp.zeros_like(x)
        for j, xs in enumerate(_conv_taps(x, S)):
            y = y + w[j:j + 1, :] * xs
        t = y * _sigmoid(y)
        is_norm, scale = _qkv_scale(c)
        r = lax.rsqrt(jnp.sum(t * t, axis=-1, keepdims=True) + 1e-6)
        o_ref[...] = jnp.where(is_norm, t * r * scale, t)

    return pl.pallas_call(
        body, grid=(GDN_QKV // LANES,),
        in_specs=[BS((S, LANES), lambda c: (0, c)), BS((8, LANES), lambda c: (0, c))],
        out_specs=BS((S, LANES), lambda c: (0, c)),
        out_shape=SDS((S, GDN_QKV), F32), name=name, compiler_params=_cp(("parallel",)))(proj, convw)


def gdn_pre_bwd(proj, convw, dqkv, name):
    S = proj.shape[0]

    def body(p_ref, w_ref, d_ref, dp_ref, dw_ref):
        c = pl.program_id(0)
        x = p_ref[...]
        w = w_ref[...]
        taps = _conv_taps(x, S)
        y = jnp.zeros_like(x)
        for j, xs in enumerate(taps):
            y = y + w[j:j + 1, :] * xs
        sg = _sigmoid(y)
        t = y * sg
        is_norm, scale = _qkv_scale(c)
        dout = d_ref[0, 0] + d_ref[1, 0]
        r = lax.rsqrt(jnp.sum(t * t, axis=-1, keepdims=True) + 1e-6)
        n = t * r
        dn = dout * scale
        dt_norm = r * (dn - n * jnp.sum(dn * n, axis=-1, keepdims=True))
        dt = jnp.where(is_norm, dt_norm, dout)
        dy = dt * (sg * (1.0 + y * (1.0 - sg)))
        row = lax.broadcasted_iota(jnp.int32, (8, LANES), 0)
        dw = jnp.zeros((8, LANES), F32)
        for j, xs in enumerate(taps):
            dw = dw + jnp.where(row == j, jnp.sum(dy * xs, axis=0, keepdims=True), 0.0)
        dw_ref[...] = dw
        tt = lax.broadcasted_iota(jnp.int32, x.shape, 0)
        dx = jnp.zeros_like(x)
        for j in range(GDN_CONV):
            sh = j - GDN_CONV // 2
            ds = dy if sh == 0 else pltpu.roll(dy, sh % S, 0)
            dx = dx + w[j:j + 1, :] * jnp.where((tt - sh >= 0) & (tt - sh < S), ds, 0.0)
        dp_ref[...] = dx.astype(BF16)

    return pl.pallas_call(
        body, grid=(GDN_QKV // LANES,),
        in_specs=[BS((S, LANES), lambda c: (0, c)), BS((8, LANES), lambda c: (0, c)),
                  BS((2, 1, S, LANES), lambda c: (0, c // GDN_H, 0, c % GDN_H))],
        out_specs=[BS((S, LANES), lambda c: (0, c)), BS((8, LANES), lambda c: (0, c))],
        out_shape=[SDS((S, GDN_QKV), BF16), SDS((8, GDN_QKV), F32)],
        name=name, compiler_params=_cp(("parallel",)))(proj, convw, dqkv)


def _chunk_sum_matrix(n, upper):
    i = lax.broadcasted_iota(jnp.int32, (n, n), 0)
    j = lax.broadcasted_iota(jnp.int32, (n, n), 1)
    same = (i // GDN_C) == (j // GDN_C)
    tri = (i <= j) if upper else (i >= j)
    return jnp.where(same & tri, 1.0, 0.0).astype(F32)


def _gate_lanes(shape):
    lane = lax.broadcasted_iota(jnp.int32, shape, 1)
    return lane < GDN_H, (lane >= GDN_H) & (lane < 2 * GDN_H), (lane >= 2 * GDN_H) & (lane < 4 * GDN_H)


def gdn_gate_fwd(proj, prm, name):
    S = proj.shape[0]
    tm = min(512, S)
    ct = GDN_INP // LANES - 1

    def body(p_ref, prm_ref, o_ref):
        ab = p_ref[...]
        a_log = prm_ref[0:1, :]
        dtb = prm_ref[1:2, :]
        z = ab + dtb
        sp = jnp.maximum(z, 0.0) + jnp.log(1.0 + jnp.exp(-jnp.abs(z)))
        g = -jnp.exp(a_log) * sp
        is_f, is_b, is_beta = _gate_lanes(ab.shape)
        gf = _dot(_chunk_sum_matrix(tm, False), jnp.where(is_f, g, 0.0), HI)
        gbk = _dot(_chunk_sum_matrix(tm, True), jnp.where(is_b, g, 0.0), HI)
        o_ref[...] = gf + gbk + jnp.where(is_beta, _sigmoid(ab), 0.0)

    return pl.pallas_call(
        body, grid=(S // tm,),
        in_specs=[BS((tm, LANES), lambda i: (i, ct)), BS((8, LANES), lambda i: (0, 0))],
        out_specs=BS((tm, LANES), lambda i: (i, 0)),
        out_shape=SDS((S, LANES), F32), name=name, compiler_params=_cp(("parallel",)))(proj, prm)


def gdn_gate_bwd(proj, prm, dgb, name):
    S = proj.shape[0]
    tm = min(512, S)
    ct = GDN_INP // LANES - 1

    def body(p_ref, prm_ref, d_ref, dab_ref, dprm_ref):
        i = pl.program_id(0)
        ab = p_ref[...]
        a_log = prm_ref[0:1, :]
        dtb = prm_ref[1:2, :]
        z = ab + dtb
        sp = jnp.maximum(z, 0.0) + jnp.log(1.0 + jnp.exp(-jnp.abs(z)))
        ea = jnp.exp(a_log)
        g = -ea * sp
        is_f, is_b, is_beta = _gate_lanes(ab.shape)
        d = d_ref[...]
        dg = (_dot_tn(_chunk_sum_matrix(tm, False), jnp.where(is_f, d, 0.0), HI)
              + _dot_tn(_chunk_sum_matrix(tm, True), jnp.where(is_b, d, 0.0), HI))
        da = dg * (-ea) * _sigmoid(z)
        beta = _sigmoid(ab)
        dab_ref[...] = jnp.where(is_beta, d * beta * (1.0 - beta), da).astype(BF16)
        row = lax.broadcasted_iota(jnp.int32, (8, LANES), 0)
        part = (jnp.where(row == 0, jnp.sum(dg * g, axis=0, keepdims=True), 0.0)
                + jnp.where(row == 1, jnp.sum(da, axis=0, keepdims=True), 0.0))

        @pl.when(i == 0)
        def _():
            dprm_ref[...] = part

        @pl.when(i > 0)
        def _():
            dprm_ref[...] += part

    return pl.pallas_call(
        body, grid=(S // tm,),
        in_specs=[BS((tm, LANES), lambda i: (i, ct)), BS((8, LANES), lambda i: (0, 0)), BS((tm, LANES), lambda i: (i, 0))],
        out_specs=[BS((tm, LANES), lambda i: (i, 0)), BS((8, LANES), lambda i: (0, 0))],
        out_shape=[SDS((S, LANES), BF16), SDS((8, LANES), F32)],
        name=name, compiler_params=_cp(("arbitrary",)))(proj, prm, dgb)


def _tri_masks(d):
    i = lax.broadcasted_iota(jnp.int32, (GDN_C, GDN_C), 0)
    j = lax.broadcasted_iota(jnp.int32, (GDN_C, GDN_C), 1)
    s = (i - j) * (1 - 2 * d)
    return s >= 0, s > 0


def _split(a):
    hi = _bf(a)
    return hi, _bf(a - hi.astype(F32))


def _dot3(a, b):
    return _dot(a[0], b[0]) + (_dot(a[0], b[1]) + _dot(a[1], b[0]))


def _inv_unit_tri_many(mats):
    i = lax.broadcasted_iota(jnp.int32, mats[0].shape, 0)
    j = lax.broadcasted_iota(jnp.int32, mats[0].shape, 1)
    eye = jnp.where(i == j, 1.0, 0.0)
    ms = [-a for a in mats]
    ps = [eye + m for m in ms]
    for _ in range(int(math.log2(GDN_C)) - 1):
        sp = [_split(m) for m in ms]
        ms = [_dot3(s, s) for s in sp]
        sp = [_split(m) for m in ms]
        pp = [_split(p) for p in ps]
        ps = [p + _dot3(a, b) for p, a, b in zip(ps, pp, sp)]
    return ps


def _lane_col(x, lane_idx):
    lane = lax.broadcasted_iota(jnp.int32, x.shape, 1)
    return jnp.sum(jnp.where(lane == lane_idx, x, 0.0), axis=1, keepdims=True)


def _chunk_gates(gb_ref, grow_ref, hh, ci, d, head):
    gbv = gb_ref[ci * GDN_C:(ci + 1) * GDN_C, :]
    gcol = _lane_col(gbv, d * GDN_H + head)
    bcol = _lane_col(gbv, 2 * GDN_H + d * GDN_H + head)
    glast = jnp.where(d == 0, gcol[GDN_C - 1:GDN_C, :], gcol[0:1, :])
    return gcol, bcol, grow_ref[hh, ci:ci + 1, :], glast


def _chunk_base(q, k, gcol, grow, bcol, glast, d):
    incl, strict = _tri_masks(d)
    decay = jnp.where(incl, jnp.exp(jnp.where(incl, gcol - grow, 0.0)), 0.0)
    kb = k * bcol
    kk = _dot_nt(_bf(kb), _bf(k))
    qk = _dot_nt(_bf(q), _bf(k))
    eg = jnp.exp(gcol)
    ek = jnp.exp(glast - gcol)
    return dict(incl=incl, strict=strict, decay=decay, kb=kb, kk=kk, qk=qk, eg=eg, ek=ek, q_dec=q * eg, k_dec=k * ek,
                bcol=bcol, glast=glast)


def _block_terms(q_ref, k_ref, v_ref, gb_ref, grow_ref, d, h, hp):
    keys = [(hh, ci) for hh in range(hp) for ci in range(GDN_GC)]
    ts = []
    for hh, ci in keys:
        rows = slice(ci * GDN_C, (ci + 1) * GDN_C)
        cols = slice(hh * GDN_DK, (hh + 1) * GDN_DK)
        gcol, bcol, grow_v, glast = _chunk_gates(gb_ref, grow_ref, hh, ci, d, h * hp + hh)
        t = _chunk_base(q_ref[rows, cols], k_ref[rows, cols], gcol, grow_v, bcol, glast, d)
        t["v"] = v_ref[rows, cols]
        ts.append(t)
    tinvs = _inv_unit_tri_many([jnp.where(t["strict"], t["kk"] * t["decay"], 0.0) for t in ts])
    sp = [_split(x) for x in tinvs]
    us = [_dot3(s, _split(t["v"] * t["bcol"])) for s, t in zip(sp, ts)]
    ws = [_dot3(s, _split(t["kb"] * t["eg"])) for s, t in zip(sp, ts)]
    for t, tinv, u, w in zip(ts, tinvs, us, ws):
        t.update(tinv=tinv, u=u, w=w)
    return keys, ts


def _gdn_specs(S, nblk, order, hp):
    R = GDN_GC * GDN_C
    wd = hp * GDN_DK
    hb = GDN_H // hp

    def qkv_spec(part):
        return BS((R, wd), lambda d, h, n: (order(d, n), part * hb + h))

    gb_spec = BS((R, LANES), lambda d, h, n: (order(d, n), 0))
    grow_spec = BS((hp, GDN_GC, GDN_C), lambda d, h, n: (d * hb + h, order(d, n), 0))
    st_spec = BS((1, hp, GDN_GC, GDN_DK, GDN_DK), lambda d, h, n: (d, h, order(d, n), 0, 0))
    return qkv_spec, gb_spec, grow_spec, st_spec


def _lane_row(x):
    return jnp.broadcast_to(x, (1, LANES))


def _side_parts(side):
    if side is None:
        return [], [], [], [], []
    return [ANY] * len(side.ins), [ANY] * len(side.outs), list(side.outs), list(side.sems), list(side.ins)


def _side_run(side, refs, n_in, n_out, n_scr, first, last):
    if side is None:
        return
    ns, no, nm = len(side.ins), len(side.outs), len(side.sems)
    s_in = refs[n_in:n_in + ns]
    s_out = refs[n_in + ns + n_out:n_in + ns + n_out + no]
    s_sem = refs[len(refs) - nm:]

    @pl.when(first)
    def _():
        side.start(s_in, s_out, s_sem)

    @pl.when(last)
    def _():
        side.wait(s_in, s_out, s_sem)


def gdn_scan_fwd(qkv, gb, grow, name, side=None):
    S = qkv.shape[0]
    R = GDN_GC * GDN_C
    nblk = S // R
    nc = S // GDN_C
    hp = GDN_HP_FWD
    wd = hp * GDN_DK
    heads = range(hp)

    def order(d, n):
        return n + d * (nblk - 1 - 2 * n)

    qkv_spec, gb_spec, grow_spec, st_spec = _gdn_specs(S, nblk, order, hp)

    s_in, s_out, s_shape, s_scr_shapes, s_ops = _side_parts(side)
    hb = GDN_H // hp

    def body(*refs):
        q_ref, k_ref, v_ref, gb_ref, grow_ref = refs[:5]
        o_ref, st_ref = refs[5 + len(s_in):7 + len(s_in)]
        s_scr, u_scr, w_scr, qd_scr, kd_scr, in_scr, egl_scr = refs[7 + len(s_in) + len(s_out):14 + len(s_in) + len(s_out)]
        d = pl.program_id(0)
        h = pl.program_id(1)
        n = pl.program_id(2)
        _side_run(side, refs, 5, 2, 7, (d == 0) & (h == 0) & (n == 0), (d == 1) & (h == hb - 1) & (n == nblk - 1))

        @pl.when(n == 0)
        def _():
            s_scr[...] = jnp.zeros_like(s_scr)

        keys, ts = _block_terms(q_ref, k_ref, v_ref, gb_ref, grow_ref, d, h, hp)
        for (hh, ci), t in zip(keys, ts):
            u_scr[hh, ci] = t["u"]
            w_scr[hh, ci] = _bf(t["w"])
            qd_scr[hh, ci] = _bf(t["q_dec"])
            kd_scr[hh, ci] = _bf(t["k_dec"])
            in_scr[hh, ci] = _bf(jnp.where(t["incl"], t["qk"] * t["decay"], 0.0))
            egl_scr[hh, ci] = _lane_row(jnp.exp(t["glast"]))

        def chunk(cc, carry):
            ci = cc + d * (GDN_GC - 1 - 2 * cc)
            rows = pl.ds(pl.multiple_of(ci * GDN_C, GDN_C), GDN_C)
            sts = [s_scr[hh] for hh in heads]
            for hh in heads:
                st_ref[0, hh, ci] = sts[hh]
            sbs = [_bf(st) for st in sts]
            vns = [_bf(u_scr[hh, ci] - _dot(w_scr[hh, ci], sbs[hh])) for hh in heads]
            for hh in heads:
                s_scr[hh] = sts[hh] * egl_scr[hh, ci] + _dot_tn(kd_scr[hh, ci], vns[hh])
            for hh in heads:
                o_ref[0, rows, hh * GDN_DK:(hh + 1) * GDN_DK] = _dot(qd_scr[hh, ci], sbs[hh]) + _dot(in_scr[hh, ci], vns[hh])
            return carry

        lax.fori_loop(0, GDN_GC, chunk, 0)

    blk = (hp, GDN_GC, GDN_C, GDN_DK)
    return pl.pallas_call(
        body, grid=(2, GDN_H // hp, nblk),
        in_specs=[qkv_spec(0), qkv_spec(1), qkv_spec(2), gb_spec, grow_spec] + s_in,
        out_specs=[BS((1, R, wd), lambda d, h, n: (d, order(d, n), h)), st_spec] + s_out,
        out_shape=[SDS((2, S, GDN_H * GDN_DK), F32), SDS((2, GDN_H, nc, GDN_DK, GDN_DK), F32)] + s_shape,
        scratch_shapes=[pltpu.VMEM((hp, GDN_DK, GDN_DK), F32), pltpu.VMEM(blk, F32), pltpu.VMEM(blk, BF16),
                        pltpu.VMEM(blk, BF16), pltpu.VMEM(blk, BF16), pltpu.VMEM((hp, GDN_GC, GDN_C, GDN_C), BF16),
                        pltpu.VMEM((hp, GDN_GC, 1, LANES), F32)] + s_scr_shapes,
        name=name, compiler_params=_cp(("arbitrary", "arbitrary", "arbitrary")))(qkv, qkv, qkv, gb, grow, *s_ops)


def gdn_scan_bwd(qkv, gb, grow, states, do, name, side=None):
    S = qkv.shape[0]
    R = GDN_GC * GDN_C
    nblk = S // R
    hp = GDN_HP_BWD
    wd = hp * GDN_DK
    heads = range(hp)

    def order(d, n):
        return (nblk - 1 - n) - d * (nblk - 1 - 2 * n)

    qkv_spec, gb_spec, grow_spec, st_spec = _gdn_specs(S, nblk, order, hp)

    s_in, s_out, s_shape, s_scr_shapes, s_ops = _side_parts(side)
    hb = GDN_H // hp

    def body(*refs):
        q_ref, k_ref, v_ref, gb_ref, grow_ref, st_ref, do_ref = refs[:7]
        dqkv_ref, dgate_ref = refs[7 + len(s_in):9 + len(s_in)]
        (ds_scr, w_scr, kd_scr, dv1_scr, qtdo_scr, egl_scr, dsin_scr, dvn_scr,
         sdot_scr) = refs[9 + len(s_in) + len(s_out):18 + len(s_in) + len(s_out)]
        d = pl.program_id(0)
        h = pl.program_id(1)
        n = pl.program_id(2)
        _side_run(side, refs, 7, 2, 9, (d == 0) & (h == 0) & (n == 0), (d == 1) & (h == hb - 1) & (n == nblk - 1))

        @pl.when(n == 0)
        def _():
            ds_scr[...] = jnp.zeros_like(ds_scr)

        keys, ts = _block_terms(q_ref, k_ref, v_ref, gb_ref, grow_ref, d, h, hp)
        for (hh, ci), t in zip(keys, ts):
            rows = slice(ci * GDN_C, (ci + 1) * GDN_C)
            t["wb"] = _bf(t["w"])
            t["dob"] = _bf(do_ref[rows, hh * GDN_DK:(hh + 1) * GDN_DK])
            t["sb"] = _bf(st_ref[0, hh, ci])
        for (hh, ci), t in zip(keys, ts):
            t["vnb"] = _bf(t["u"] - _dot(t["wb"], t["sb"]))
            w_scr[hh, ci] = t["wb"]
            kd_scr[hh, ci] = _bf(t["k_dec"])
            dv1_scr[hh, ci] = _dot_tn(_bf(jnp.where(t["incl"], t["qk"] * t["decay"], 0.0)), t["dob"])
            qtdo_scr[hh, ci] = _dot_tn(_bf(t["q_dec"]), t["dob"])
            egl_scr[hh, ci] = _lane_row(jnp.exp(t["glast"]))

        def chunk(cc, carry):
            ci = (GDN_GC - 1 - cc) - d * (GDN_GC - 1 - 2 * cc)
            dsns = [ds_scr[hh] for hh in heads]
            dsbs = [_bf(x) for x in dsns]
            dvns = [dv1_scr[hh, ci] + _dot(kd_scr[hh, ci], dsbs[hh]) for hh in heads]
            for hh in heads:
                ds_scr[hh] = qtdo_scr[hh, ci] + egl_scr[hh, ci] * dsns[hh] - _dot_tn(w_scr[hh, ci], _bf(dvns[hh]))
            for hh in heads:
                dsin_scr[hh, ci] = dsbs[hh]
                dvn_scr[hh, ci] = dvns[hh]
                sd = jnp.sum(jnp.sum(st_ref[0, hh, ci] * dsns[hh], axis=1, keepdims=True), axis=0, keepdims=True)
                sdot_scr[hh, ci] = _lane_row(sd)
            return carry

        lax.fori_loop(0, GDN_GC, chunk, 0)

        for (hh, ci), t in zip(keys, ts):
            t["d_vnew"] = dvn_scr[hh, ci]
            t["dvb"] = _bf(t["d_vnew"])
            t["dsb"] = dsin_scr[hh, ci]
        for t in ts:
            t["d_intra"] = jnp.where(t["incl"], _dot_nt(t["dob"], t["vnb"]), 0.0)
            t["d_qdec"] = _dot_nt(t["dob"], t["sb"])
            t["d_kdec"] = _dot_nt(t["vnb"], t["dsb"])
            t["dw"] = -_dot_nt(t["dvb"], t["sb"])
        for t in ts:
            tts = _split(t["tinv"].T)
            t["d_ru"] = _dot3(tts, _split(t["d_vnew"]))
            t["d_rw"] = _dot3(tts, _split(t["dw"]))
        for t in ts:
            t["da"] = -jnp.where(t["strict"], _dot_nt(_bf(t["d_ru"]), _bf(t["u"])) + _dot_nt(_bf(t["d_rw"]), t["wb"]), 0.0)
        for (hh, ci), t in zip(keys, ts):
            rows = slice(ci * GDN_C, (ci + 1) * GDN_C)
            cols = slice(hh * GDN_DK, (hh + 1) * GDN_DK)
            q, k, v = q_ref[rows, cols], k_ref[rows, cols], t["v"]
            decay, kb, eg, ek, bcol = t["decay"], t["kb"], t["eg"], t["ek"], t["bcol"]
            d_ru, d_rw, da, d_intra, d_qdec, d_kdec = t["d_ru"], t["d_rw"], t["da"], t["d_intra"], t["d_qdec"], t["d_kdec"]
            kbf, qbf = _bf(k), _bf(q)
            dgl = egl_scr[hh, ci][:, 0:1] * sdot_scr[hh, ci][:, 0:1]
            dv = d_ru * bcol
            dbeta = jnp.sum(d_ru * v, axis=1, keepdims=True)
            dkb = d_rw * eg
            dg = jnp.sum(d_rw * kb, axis=1, keepdims=True) * eg
            dkk = _bf(da * decay)
            dqk = _bf(d_intra * decay)
            dkb = dkb + _dot(dkk, kbf)
            dk = _dot_tn(dkk, _bf(kb)) + _dot_tn(dqk, qbf)
            dq = _dot(dqk, kbf) + d_qdec * eg
            dd = (da * t["kk"] + d_intra * t["qk"]) * decay
            dg = dg + jnp.sum(dd, axis=1, keepdims=True) - jnp.sum(dd.T, axis=1, keepdims=True)
            dg = dg + jnp.sum(d_qdec * t["q_dec"], axis=1, keepdims=True)
            dk = dk + d_kdec * ek
            ee = jnp.sum(d_kdec * t["k_dec"], axis=1, keepdims=True)
            dg = dg - ee
            dgl = dgl + jnp.sum(ee, axis=0, keepdims=True)
            dk = dk + dkb * bcol
            dbeta = dbeta + jnp.sum(dkb * k, axis=1, keepdims=True)
            ridx = lax.broadcasted_iota(jnp.int32, (GDN_C, 1), 0)
            dg = dg + jnp.where(ridx == (GDN_C - 1) * (1 - d), dgl, 0.0)
            dqkv_ref[0, 0, rows, cols] = dq
            dqkv_ref[0, 1, rows, cols] = dk
            dqkv_ref[0, 2, rows, cols] = dv
            lane2 = lax.broadcasted_iota(jnp.int32, (GDN_C, 2), 1)
            dgate_ref[0, hh, rows, :] = jnp.where(lane2 == 0, dg, dbeta)

    blk = (hp, GDN_GC, GDN_C, GDN_DK)
    sq = (hp, GDN_GC, GDN_DK, GDN_DK)
    row = (hp, GDN_GC, 1, LANES)
    return pl.pallas_call(
        body, grid=(2, GDN_H // hp, nblk),
        in_specs=[qkv_spec(0), qkv_spec(1), qkv_spec(2), gb_spec, grow_spec, st_spec,
                  BS((R, wd), lambda d, h, n: (order(d, n), h))] + s_in,
        out_specs=[BS((1, 3, R, wd), lambda d, h, n: (d, 0, order(d, n), h)),
                   BS((1, hp, R, 2), lambda d, h, n: (d, h, order(d, n), 0))] + s_out,
        out_shape=[SDS((2, 3, S, GDN_H * GDN_DK), F32), SDS((2, GDN_H, S, 2), F32)] + s_shape,
        scratch_shapes=[pltpu.VMEM((hp, GDN_DK, GDN_DK), F32), pltpu.VMEM(blk, BF16), pltpu.VMEM(blk, BF16),
                        pltpu.VMEM(blk, F32), pltpu.VMEM(sq, F32), pltpu.VMEM(row, F32), pltpu.VMEM(sq, BF16),
                        pltpu.VMEM(blk, F32), pltpu.VMEM(row, F32)] + s_scr_shapes,
        name=name, compiler_params=_cp(("arbitrary", "arbitrary", "arbitrary")))(qkv, qkv, qkv, gb, grow, states, do, *s_ops)


def gdn_post_fwd(o2, proj, nw, name):
    S = proj.shape[0]
    tm = min(512, S)
    zoff = GDN_QKV // LANES

    def body(o_ref, z_ref, nw_ref, y_ref):
        o = o_ref[0] + o_ref[1]
        z = z_ref[...]
        r = lax.rsqrt(jnp.mean(o * o, axis=-1, keepdims=True) + RMS_EPS)
        y_ref[...] = (o * r * nw_ref[...] * (z * _sigmoid(z))).astype(BF16)

    return pl.pallas_call(
        body, grid=(S // tm, GDN_H),
        in_specs=[BS((2, tm, LANES), lambda i, h: (0, i, h)), BS((tm, LANES), lambda i, h: (i, zoff + h)),
                  BS((1, LANES), lambda i, h: (0, 0))],
        out_specs=BS((tm, LANES), lambda i, h: (i, h)),
        out_shape=SDS((S, GDN_H * GDN_DK), BF16), name=name, compiler_params=_cp(("parallel", "parallel")))(o2, proj, nw)


def gdn_post_bwd(o2, proj, nw, dy, name):
    S = proj.shape[0]
    tm = min(512, S)
    zoff = GDN_QKV // LANES

    def body(o_ref, z_ref, nw_ref, dy_ref, do_ref, dz_ref, dnw_ref):
        first = (pl.program_id(0) == 0) & (pl.program_id(1) == 0)
        o = o_ref[0] + o_ref[1]
        z = z_ref[...]
        nwv = nw_ref[...]
        dyv = dy_ref[...]
        r = lax.rsqrt(jnp.mean(o * o, axis=-1, keepdims=True) + RMS_EPS)
        n = o * r
        sg = _sigmoid(z)
        sz = z * sg
        dz_ref[...] = (dyv * n * nwv * (sg * (1.0 + z * (1.0 - sg)))).astype(BF16)
        dn = dyv * nwv * sz
        do_ref[...] = r * (dn - n * jnp.mean(dn * n, axis=-1, keepdims=True))
        part = jnp.sum(dyv * n * sz, axis=0, keepdims=True)

        @pl.when(first)
        def _():
            dnw_ref[...] = part

        @pl.when(jnp.logical_not(first))
        def _():
            dnw_ref[...] += part

    blk = BS((tm, LANES), lambda i, h: (i, h))
    return pl.pallas_call(
        body, grid=(S // tm, GDN_H),
        in_specs=[BS((2, tm, LANES), lambda i, h: (0, i, h)), BS((tm, LANES), lambda i, h: (i, zoff + h)),
                  BS((1, LANES), lambda i, h: (0, 0)), blk],
        out_specs=[blk, blk, BS((1, LANES), lambda i, h: (0, 0))],
        out_shape=[SDS((S, GDN_H * GDN_DK), F32), SDS((S, GDN_H * GDN_DK), BF16), SDS((1, LANES), F32)],
        name=name, compiler_params=_cp(("arbitrary", "arbitrary")))(o2, proj, nw, dy)


def _gate_prm(a_log, dt_bias):
    z = jnp.zeros((8, LANES), F32)
    z = z.at[0, :2 * GDN_H].set(a_log.reshape(-1))
    return z.at[1, :2 * GDN_H].set(dt_bias.reshape(-1))


def gdn_fwd(x, g, w_all, convw, a_log, dt_bias, nw, w_out, tag, side=None):
    S = x.shape[0]
    h, ht = rms_fwd(x, g, f"{tag}_rms")
    proj = mm(h, w_all, name=f"{tag}_proj", tn=1408)
    qkv = gdn_pre_fwd(proj, convw, f"{tag}_pre")
    prm = _gate_prm(a_log, dt_bias)
    gb = gdn_gate_fwd(proj, prm, f"{tag}_gate")
    grow = gb[:, :2 * GDN_H].T.reshape(2 * GDN_H, S // GDN_C, GDN_C)
    o2, states, *side_out = gdn_scan_fwd(qkv, gb, grow, f"{tag}_scan", side)
    y = gdn_post_fwd(o2, proj, nw, f"{tag}_post")
    xn = mm(y, w_out, name=f"{tag}_out", epi=_add, extras=(x,))
    return xn, (ht, proj, qkv, prm, gb, grow, o2, states, y), side_out


def gdn_bwd(x, g, w_all, convw, nw, w_out, saved, dx, dxb, tag, side=None):
    S = x.shape[0]
    ht, proj, qkv, prm, gb, grow, o2, states, y = saved
    dw_out = mm(y, dxb, name=f"{tag}_dwout", ta=True)
    dy = mm(dxb, w_out, name=f"{tag}_dy", tb=True)
    do, dz, dnw = gdn_post_bwd(o2, proj, nw, dy, f"{tag}_postb")
    dqkv, dgate, *side_out = gdn_scan_bwd(qkv, gb, grow, states, do, f"{tag}_scanb", side)
    dgb = jnp.transpose(dgate, (2, 3, 0, 1)).reshape(S, 4 * GDN_H)
    dgb = jnp.pad(dgb, ((0, 0), (0, LANES - 4 * GDN_H)))
    dab, dprm = gdn_gate_bwd(proj, prm, dgb, f"{tag}_gateb")
    dpq, dconvw = gdn_pre_bwd(proj, convw, dqkv, f"{tag}_preb")
    dproj = jnp.concatenate([dpq, dz, dab], axis=1)
    dw_all = mm(ht, dproj, name=f"{tag}_dwin", tn=384)
    dh = mm(dproj, w_all, name=f"{tag}_dh", tb=True)
    dx, dxb, dg = rms_bwd(x, g, dh, dx, f"{tag}_rmsb")
    da_log = dprm[0, :2 * GDN_H].reshape(2, GDN_H)
    ddt = dprm[1, :2 * GDN_H].reshape(2, GDN_H)
    return dx, dxb, dg, dw_all, dconvw, da_log, ddt, dnw, dw_out, side_out


def _rel_bucket_np(rel):
    nb = REL_BUCKETS // 2
    max_exact = nb // 2
    ret = np.where(rel > 0, nb, 0)
    n = np.abs(rel)
    nf = np.maximum(n, 1).astype(np.float32)
    large = max_exact + (np.log(nf / max_exact) / np.float32(math.log(REL_MAX_DIST / max_exact))
                         * (nb - max_exact)).astype(np.int32)
    large = np.minimum(large, nb - 1)
    return ret + np.where(n < max_exact, n, large)


def _toeplitz(f, rows, cols):
    period = rows + cols
    e = jnp.pad(f, ((0, 0), (0, period - f.shape[1])))
    y = jnp.tile(e, (1, rows))[:, :rows * (period - 1)]
    return y.reshape(f.shape[0], rows, period - 1)[:, :, :cols]


ATT_Q = DSWA_HALF
ATT_W = 3 * DSWA_HALF
ATT_TB = 1024
ATT_PAIRS = DSWA_HG // 2


def _bias_mats(rel_table, gi):
    _, dil = DSWA_CFG[gi]
    offs = np.arange(-DSWA_HALF, DSWA_HALF + 1)
    onehot = jnp.asarray(np.eye(REL_BUCKETS, dtype=np.float32)[_rel_bucket_np(offs * dil)])
    f = jnp.dot(onehot, rel_table, precision=HI)[:, gi * DSWA_HG:(gi + 1) * DSWA_HG].T
    bias = _toeplitz(f, ATT_Q, ATT_W)
    bias_t = jnp.transpose(_toeplitz(f[:, ::-1], ATT_Q, ATT_W), (0, 2, 1))
    return bias.reshape(ATT_PAIRS, 2, ATT_Q, ATT_W), bias_t.reshape(ATT_PAIRS, 2, ATT_W, ATT_Q)


def _att_specs(S, d, col):
    halo = DSWA_HALF * d
    per = ATT_TB // halo
    last = S // halo - 1
    cur = BS((ATT_TB, LANES), lambda p, tb: (tb, col(p)))
    prev = BS((halo, LANES), lambda p, tb: (jnp.maximum(tb * per - 1, 0), col(p)))
    nxt = BS((halo, LANES), lambda p, tb: (jnp.minimum((tb + 1) * per, last), col(p)))
    return prev, cur, nxt


def _att_specs3(S, d, lead):
    halo = DSWA_HALF * d
    per = ATT_TB // halo
    last = S // halo - 1
    cur = BS((1, ATT_TB, LANES), lambda p, tb: (lead(p), tb, 0))
    prev = BS((1, halo, LANES), lambda p, tb: (lead(p), jnp.maximum(tb * per - 1, 0), 0))
    nxt = BS((1, halo, LANES), lambda p, tb: (lead(p), jnp.minimum((tb + 1) * per, last), 0))
    return prev, cur, nxt


class _Pieces:
    def __init__(self, prev, cur, nxt, d, lead=None, cast=None):
        self.refs, self.d, self.lead, self.cast, self.cache = (prev, cur, nxt), d, lead, cast, {}
        self.halo = DSWA_HALF * d
        self.nsb = ATT_TB // self.halo

    def __call__(self, r, sb):
        if (r, sb) not in self.cache:
            ref = self.refs[0] if sb < 0 else self.refs[2] if sb >= self.nsb else self.refs[1]
            start = r + (self.halo * sb if 0 <= sb < self.nsb else 0)
            rows = pl.ds(start, ATT_Q, stride=self.d) if self.d > 1 else pl.ds(start, ATT_Q)
            v = ref[rows, :] if self.lead is None else ref[0, rows, :]
            self.cache[(r, sb)] = v if self.cast is None else v.astype(self.cast)
        return self.cache[(r, sb)]

    def window(self, r, sb):
        return jnp.concatenate([self(r, sb - 1), self(r, sb), self(r, sb + 1)], axis=0)


ATT_GROUP = 8


def _tile_groups(d, nsb):
    tiles = [(r, sb) for r in range(d) for sb in range(nsb)]
    return [tiles[i:i + ATT_GROUP] for i in range(0, len(tiles), ATT_GROUP)]


def _tile_rows(r, sb, d):
    start = r + DSWA_HALF * d * sb
    return pl.ds(start, ATT_Q, stride=d) if d > 1 else pl.ds(start, ATT_Q)


def _tile_valid(tb, r, sb, d, S, transposed):
    shape = (ATT_W, ATT_Q) if transposed else (ATT_Q, ATT_W)
    blk = lax.broadcasted_iota(jnp.int32, shape, 1 if transposed else 0)
    win = lax.broadcasted_iota(jnp.int32, shape, 0 if transposed else 1)
    tok = tb * ATT_TB + r + d * (DSWA_HALF * (sb - 1) + win)
    return (jnp.abs(win - DSWA_HALF - blk) <= DSWA_HALF) & (tok >= 0) & (tok < S)


def _head_masks():
    lane = lax.broadcasted_iota(jnp.int32, (1, LANES), 1)
    return [lane < DSWA_E, lane >= DSWA_E], lane


def attn_fwd(qkv, bias, gi, name):
    S = qkv.shape[0]
    d = DSWA_CFG[gi][1]
    nsb = ATT_TB // (DSWA_HALF * d)
    npair = DSWA_HEADS // 2
    q_spec = _att_specs(S, d, lambda p: gi * ATT_PAIRS + p)[1]
    k_specs = _att_specs(S, d, lambda p: npair + gi * ATT_PAIRS + p)
    v_specs = _att_specs(S, d, lambda p: 2 * npair + gi * ATT_PAIRS + p)

    def body(q_ref, kp, kc, kn, vp, vc, vn, b_ref, o_ref, lse_ref):
        tb = pl.program_id(1)
        masks, lane = _head_masks()
        kpc = _Pieces(kp, kc, kn, d, cast=BF16)
        vpc = _Pieces(vp, vc, vn, d, cast=BF16)
        scale = DSWA_E ** -0.5
        for grp in _tile_groups(d, nsb):
            rows = [_tile_rows(r, sb, d) for r, sb in grp]
            qs = [q_ref[rw, :] for rw in rows]
            kws = [kpc.window(r, sb) for r, sb in grp]
            vws = [vpc.window(r, sb) for r, sb in grp]
            valids = [_tile_valid(tb, r, sb, d, S, False) for r, sb in grp]
            both = [(t, hh) for t in range(len(grp)) for hh in range(2)]
            ss = [_dot_nt(_bf(jnp.where(masks[hh], qs[t], 0.0)), kws[t]) * scale + b_ref[0, hh] for t, hh in both]
            ss = [jnp.where(valids[t], s, NEG_INF) for (t, hh), s in zip(both, ss)]
            ms = [jnp.max(s, axis=-1, keepdims=True) for s in ss]
            ps = [jnp.exp(s - m) for s, m in zip(ss, ms)]
            ls = [jnp.sum(p, axis=-1, keepdims=True) for p in ps]
            os = [_dot(_bf(p / l), vws[t]) for (t, hh), p, l in zip(both, ps, ls)]
            for t, rw in enumerate(rows):
                o_ref[rw, :] = jnp.where(masks[0], os[2 * t], os[2 * t + 1])
                lse_ref[0, rw, :] = (jnp.where(lane == 0, ms[2 * t] + jnp.log(ls[2 * t]), 0.0)
                                     + jnp.where(lane == 1, ms[2 * t + 1] + jnp.log(ls[2 * t + 1]), 0.0))

    return pl.pallas_call(
        body, grid=(ATT_PAIRS, S // ATT_TB),
        in_specs=[q_spec, *k_specs, *v_specs, BS((1, 2, ATT_Q, ATT_W), lambda p, tb: (p, 0, 0, 0))],
        out_specs=[BS((ATT_TB, LANES), lambda p, tb: (tb, p)), BS((1, ATT_TB, LANES), lambda p, tb: (p, tb, 0))],
        out_shape=[SDS((S, DSWA_HG * DSWA_E), F32), SDS((ATT_PAIRS, S, LANES), F32)],
        name=name, compiler_params=_cp(("parallel", "parallel")))(qkv, qkv, qkv, qkv, qkv, qkv, qkv, bias)


def attn_bwd_q(qkv, bias, lse, do, dd, gi, name):
    S = qkv.shape[0]
    d = DSWA_CFG[gi][1]
    nsb = ATT_TB // (DSWA_HALF * d)
    npair = DSWA_HEADS // 2
    q_spec = _att_specs(S, d, lambda p: gi * ATT_PAIRS + p)[1]
    k_specs = _att_specs(S, d, lambda p: npair + gi * ATT_PAIRS + p)
    v_specs = _att_specs(S, d, lambda p: 2 * npair + gi * ATT_PAIRS + p)
    bspec = BS((1, 2, ATT_Q, ATT_W), lambda p, tb: (p, 0, 0, 0))

    def body(q_ref, kp, kc, kn, vp, vc, vn, b_ref, lse_ref, do_ref, dd_ref, dq_ref, db_ref):
        tb = pl.program_id(1)
        masks, lane = _head_masks()
        kpc = _Pieces(kp, kc, kn, d, cast=BF16)
        vpc = _Pieces(vp, vc, vn, d, cast=BF16)
        db = [jnp.zeros((ATT_Q, ATT_W), F32), jnp.zeros((ATT_Q, ATT_W), F32)]
        scale = DSWA_E ** -0.5
        for grp in _tile_groups(d, nsb):
            rows = [_tile_rows(r, sb, d) for r, sb in grp]
            qs = [q_ref[rw, :] for rw in rows]
            dos = [do_ref[0, rw, :] for rw in rows]
            lses = [lse_ref[0, rw, :] for rw in rows]
            dds = [dd_ref[0, 0, rw, :] for rw in rows]
            kws = [kpc.window(r, sb) for r, sb in grp]
            vws = [vpc.window(r, sb) for r, sb in grp]
            valids = [_tile_valid(tb, r, sb, d, S, False) for r, sb in grp]
            both = [(t, hh) for t in range(len(grp)) for hh in range(2)]
            ss = [_dot_nt(_bf(jnp.where(masks[hh], qs[t], 0.0)), kws[t]) * scale + b_ref[0, hh] for t, hh in both]
            dps = [_dot_nt(_bf(jnp.where(masks[hh], dos[t], 0.0)), vws[t]) for t, hh in both]
            ps = [jnp.exp(jnp.where(valids[t], s - lses[t][:, hh:hh + 1], NEG_INF)) for (t, hh), s in zip(both, ss)]
            dss = [p * (dp - dds[t][:, hh:hh + 1]) for (t, hh), p, dp in zip(both, ps, dps)]
            dqs = [_dot(_bf(ds), kws[t]) * scale for (t, hh), ds in zip(both, dss)]
            for t, rw in enumerate(rows):
                dq_ref[rw, :] = jnp.where(masks[0], dqs[2 * t], dqs[2 * t + 1])
                db[0] = db[0] + dss[2 * t]
                db[1] = db[1] + dss[2 * t + 1]

        @pl.when(tb == 0)
        def _():
            db_ref[0, 0] = db[0]
            db_ref[0, 1] = db[1]

        @pl.when(tb > 0)
        def _():
            db_ref[0, 0] += db[0]
            db_ref[0, 1] += db[1]

    return pl.pallas_call(
        body, grid=(ATT_PAIRS, S // ATT_TB),
        in_specs=[q_spec, *k_specs, *v_specs, bspec, BS((1, ATT_TB, LANES), lambda p, tb: (p, tb, 0)),
                  BS((1, ATT_TB, LANES), lambda p, tb: (gi, tb, p)), BS((1, 1, ATT_TB, LANES), lambda p, tb: (gi, p, tb, 0))],
        out_specs=[BS((ATT_TB, LANES), lambda p, tb: (tb, p)), bspec],
        out_shape=[SDS((S, DSWA_HG * DSWA_E), F32), SDS((ATT_PAIRS, 2, ATT_Q, ATT_W), F32)],
        name=name, compiler_params=_cp(("parallel", "arbitrary")))(qkv, qkv, qkv, qkv, qkv, qkv, qkv, bias, lse, do, dd)


def attn_bwd_kv(qkv, bias_t, lse, do, dd, gi, name):
    S = qkv.shape[0]
    d = DSWA_CFG[gi][1]
    nsb = ATT_TB // (DSWA_HALF * d)
    npair = DSWA_HEADS // 2
    q_specs = _att_specs(S, d, lambda p: gi * ATT_PAIRS + p)
    k_spec = _att_specs(S, d, lambda p: npair + gi * ATT_PAIRS + p)[1]
    v_spec = _att_specs(S, d, lambda p: 2 * npair + gi * ATT_PAIRS + p)[1]
    halo = DSWA_HALF * d
    per = ATT_TB // halo
    last = S // halo - 1

    def do_spec(rows, blk):
        return BS((1, rows, LANES), lambda p, tb: (gi, blk(tb), p))

    def dd_spec(rows, blk):
        return BS((1, 1, rows, LANES), lambda p, tb: (gi, p, blk(tb), 0))

    blks = [(halo, lambda tb: jnp.maximum(tb * per - 1, 0)), (ATT_TB, lambda tb: tb),
            (halo, lambda tb: jnp.minimum((tb + 1) * per, last))]
    do_specs = [do_spec(*b) for b in blks]
    dd_specs = [dd_spec(*b) for b in blks]
    lse_specs = _att_specs3(S, d, lambda p: p)

    class _Lead4:
        def __init__(self, ref):
            self.ref = ref

        def __getitem__(self, idx):
            return self.ref[(0,) + idx]

    def body(k_ref, v_ref, qp, qc, qn, dop, doc, don, lp, lc, ln, ddp, ddc, ddn, b_ref, dk_ref, dv_ref):
        tb = pl.program_id(1)
        masks, lane = _head_masks()
        qpc = _Pieces(qp, qc, qn, d)
        dopc = _Pieces(dop, doc, don, d, lead=True)
        lpc = _Pieces(lp, lc, ln, d, lead=True)
        ddpc = _Pieces(_Lead4(ddp), _Lead4(ddc), _Lead4(ddn), d, lead=True)
        scale = DSWA_E ** -0.5
        for grp in _tile_groups(d, nsb):
            rows = [_tile_rows(r, sb, d) for r, sb in grp]
            kcs = [_bf(k_ref[rw, :]) for rw in rows]
            vcs = [_bf(v_ref[rw, :]) for rw in rows]
            qws = [qpc.window(r, sb) for r, sb in grp]
            dows = [dopc.window(r, sb) for r, sb in grp]
            lws = [lpc.window(r, sb) for r, sb in grp]
            ddws = [ddpc.window(r, sb) for r, sb in grp]
            qwbs = [_bf(x) for x in qws]
            dowbs = [_bf(x) for x in dows]
            valids = [_tile_valid(tb, r, sb, d, S, True) for r, sb in grp]
            both = [(t, hh) for t in range(len(grp)) for hh in range(2)]
            ss = [_dot_nt(_bf(jnp.where(masks[hh], qws[t], 0.0)), kcs[t]) * scale + b_ref[0, hh] for t, hh in both]
            dps = [_dot_nt(_bf(jnp.where(masks[hh], dows[t], 0.0)), vcs[t]) for t, hh in both]
            ps = [jnp.exp(jnp.where(valids[t], s - lws[t][:, hh:hh + 1], NEG_INF)) for (t, hh), s in zip(both, ss)]
            dvs = [_dot_tn(_bf(p), dowbs[t]) for (t, hh), p in zip(both, ps)]
            dss = [p * (dp - ddws[t][:, hh:hh + 1]) for (t, hh), p, dp in zip(both, ps, dps)]
            dks = [_dot_tn(_bf(ds), qwbs[t]) * scale for (t, hh), ds in zip(both, dss)]
            for t, rw in enumerate(rows):
                dk_ref[rw, :] = jnp.where(masks[0], dks[2 * t], dks[2 * t + 1])
                dv_ref[rw, :] = jnp.where(masks[0], dvs[2 * t], dvs[2 * t + 1])

    out = BS((ATT_TB, LANES), lambda p, tb: (tb, p))
    return pl.pallas_call(
        body, grid=(ATT_PAIRS, S // ATT_TB),
        in_specs=[k_spec, v_spec, *q_specs, *do_specs, *lse_specs, *dd_specs,
                  BS((1, 2, ATT_W, ATT_Q), lambda p, tb: (p, 0, 0, 0))],
        out_specs=[out, out],
        out_shape=[SDS((S, DSWA_HG * DSWA_E), F32), SDS((S, DSWA_HG * DSWA_E), F32)],
        name=name, compiler_params=_cp(("parallel", "parallel")))(
            qkv, qkv, qkv, qkv, qkv, do, do, do, lse, lse, lse, dd, dd, dd, bias_t)


def _pair_alphas(lses):
    m = jnp.maximum(jnp.maximum(lses[0], lses[1]), lses[2])
    e = [jnp.exp(t - m) for t in lses]
    tot = e[0] + e[1] + e[2]
    return [t / tot for t in e]


def _pair_expand(a, lane):
    return jnp.where(lane < DSWA_E, a[:, 0:1], a[:, 1:2])


def combine_fwd(o_raw, lse, name):
    S = o_raw.shape[0]
    tm = min(1024, S)

    def body(o_ref, l_ref, y_ref):
        g = pl.program_id(2)
        lane = lax.broadcasted_iota(jnp.int32, (1, LANES), 1)
        alphas = _pair_alphas([l_ref[0, 0], l_ref[1, 0], l_ref[2, 0]])
        a = jnp.where(g == 0, alphas[0], jnp.where(g == 1, alphas[1], alphas[2]))
        y_ref[...] = (o_ref[...] * _pair_expand(a, lane)).astype(BF16)

    blk = BS((tm, LANES), lambda i, p, g: (i, g * ATT_PAIRS + p))
    return pl.pallas_call(
        body, grid=(S // tm, ATT_PAIRS, 3),
        in_specs=[blk, BS((3, 1, tm, LANES), lambda i, p, g: (0, p, i, 0))], out_specs=blk,
        out_shape=SDS((S, DSWA_W), BF16), name=name, compiler_params=_cp(("parallel", "parallel", "parallel")))(o_raw, lse)


def combine_bwd(o_raw, lse, dy, name):
    S = o_raw.shape[0]
    tm = min(512, S)

    def body(o0, o1, o2, l_ref, d0, d1, d2, do_ref, dd_ref):
        lane = lax.broadcasted_iota(jnp.int32, (1, LANES), 1)
        alphas = _pair_alphas([l_ref[0, 0], l_ref[1, 0], l_ref[2, 0]])
        c = jnp.zeros((tm, LANES), F32)
        for g, (o_ref, dy_ref) in enumerate(((o0, d0), (o1, d1), (o2, d2))):
            dyv = dy_ref[...]
            do_ref[g] = dyv * _pair_expand(alphas[g], lane)
            prod = o_ref[...] * dyv
            dal = (jnp.where(lane == 0, jnp.sum(jnp.where(lane < DSWA_E, prod, 0.0), axis=1, keepdims=True), 0.0)
                   + jnp.where(lane == 1, jnp.sum(jnp.where(lane >= DSWA_E, prod, 0.0), axis=1, keepdims=True), 0.0))
            c = c + alphas[g] * dal
        for g in range(3):
            dd_ref[g, 0] = alphas[g] * c

    def col(g):
        return BS((tm, LANES), lambda i, p: (i, g * ATT_PAIRS + p))

    return pl.pallas_call(
        body, grid=(S // tm, ATT_PAIRS),
        in_specs=[col(0), col(1), col(2), BS((3, 1, tm, LANES), lambda i, p: (0, p, i, 0)), col(0), col(1), col(2)],
        out_specs=[BS((3, tm, LANES), lambda i, p: (0, i, p)), BS((3, 1, tm, LANES), lambda i, p: (0, p, i, 0))],
        out_shape=[SDS((3, S, DSWA_HG * DSWA_E), F32), SDS((3, ATT_PAIRS, S, LANES), F32)],
        name=name, compiler_params=_cp(("parallel", "parallel")))(o_raw, o_raw, o_raw, lse, dy, dy, dy)


def dswa_fwd(x, g, w_in, w_out, rel_table, tag):
    h, ht = rms_fwd(x, g, f"{tag}_rms")
    qkv = mm(h, w_in, name=f"{tag}_qkv", tn=1152)
    outs, lses = [], []
    for gi in range(3):
        bias, _ = _bias_mats(rel_table, gi)
        o, lse = attn_fwd(qkv, bias, gi, f"{tag}_att{gi}")
        outs.append(o)
        lses.append(lse)
    o_raw = jnp.concatenate(outs, axis=1)
    lse = jnp.stack(lses)
    y = combine_fwd(o_raw, lse, f"{tag}_comb")
    xn = mm(y, w_out, name=f"{tag}_out", epi=_add, extras=(x,))
    return xn, (ht, qkv, o_raw, lse, y)


def dswa_bwd(x, g, w_in, w_out, rel_table, saved, dx, dxb, tag):
    ht, qkv, o_raw, lse, y = saved
    dw_out = mm(y, dxb, name=f"{tag}_dwout", ta=True, tm=384)
    dy = mm(dxb, w_out, name=f"{tag}_dy", tb=True, tn=384)
    do_raw, dd = combine_bwd(o_raw, lse, dy, f"{tag}_combb")
    dqs, dks, dvs = [], [], []
    drel = jnp.zeros_like(rel_table)
    for gi in range(3):
        (bias, bias_t), bias_vjp = jax.vjp(lambda tbl: _bias_mats(tbl, gi), rel_table)
        dq, dbias = attn_bwd_q(qkv, bias, lse[gi], do_raw, dd, gi, f"{tag}_attq{gi}")
        dk, dv = attn_bwd_kv(qkv, bias_t, lse[gi], do_raw, dd, gi, f"{tag}_attkv{gi}")
        drel = drel + bias_vjp((dbias, jnp.zeros_like(bias_t)))[0]
        dqs.append(dq)
        dks.append(dk)
        dvs.append(dv)
    dqkv = jnp.concatenate(dqs + dks + dvs, axis=1).astype(BF16)
    dw_in = mm(ht, dqkv, name=f"{tag}_dwin", tn=384)
    dh = mm(dqkv, w_in, name=f"{tag}_dh", tb=True)
    dx, dxb, dg = rms_bwd(x, g, dh, dx, f"{tag}_rmsb")
    return dx, dxb, dg, dw_in, dw_out, drel


def adamw(w, g, m, v, name):
    shape = w.shape
    last = shape[-1]
    w2, g2, m2, v2 = (t.reshape(-1, last) for t in (w, g, m, v))
    rows = w2.shape[0]
    tr = rows
    if rows > 512:
        tr = next(t for t in (512, 256, 192, 128, 64, 8) if rows % t == 0)
    c1 = 1.0 / (1.0 - ADAM_B1 ** ADAM_STEP)
    c2 = 1.0 / (1.0 - ADAM_B2 ** ADAM_STEP)

    def body(w_ref, g_ref, m_ref, v_ref, d_ref, nm_ref, nv_ref):
        gv = g_ref[...]
        nm = ADAM_B1 * m_ref[...] + (1.0 - ADAM_B1) * gv
        nv = ADAM_B2 * v_ref[...] + (1.0 - ADAM_B2) * (gv * gv)
        nm_ref[...] = nm
        nv_ref[...] = nv
        d_ref[...] = -ADAM_LR * ((nm * c1) / (jnp.sqrt(nv * c2) + ADAM_EPS) + ADAM_WD * w_ref[...])

    spec = BS((tr, last), lambda i: (i, 0))
    outs = pl.pallas_call(
        body, grid=(rows // tr,), in_specs=[spec] * 4, out_specs=[spec] * 3,
        out_shape=[SDS((rows, last), F32)] * 3, name=name, compiler_params=_cp(("parallel",)))(w2, g2, m2, v2)
    return tuple(o.reshape(shape) for o in outs)


def _place():
    x, y, c = lax.axis_index("x"), lax.axis_index("y"), lax.axis_index("c")
    chips = [(1 - x, y), (x, 1 - y), (1 - x, 1 - y)]
    return x, y, c, chips


def _rcopy(src, dst, ssem, rsem, dev):
    return pltpu.make_async_remote_copy(src_ref=src, dst_ref=dst, send_sem=ssem, recv_sem=rsem, device_id=dev,
                                        device_id_type=MESH)


class SideJob(NamedTuple):
    ins: list
    outs: list
    sems: list
    start: Callable
    wait: Callable


def _job(ins, outs, sems, copies):
    def start(in_refs, out_refs, sem_refs):
        for cp in copies(in_refs, out_refs, sem_refs):
            cp.start()

    def wait(in_refs, out_refs, sem_refs):
        for cp in copies(in_refs, out_refs, sem_refs):
            cp.wait()

    return SideJob(list(ins), list(outs), list(sems), start, wait)


def gather_job(packs, halved):
    n = len(packs)
    dma = pltpu.SemaphoreType.DMA

    def copies(in_refs, out_refs, sems):
        ssem, rsem = sems
        x, y, c, chips = _place()
        jme = 2 * x + y
        cps = []
        for i, (p_ref, f_ref) in enumerate(zip(in_refs, out_refs)):
            rows = p_ref.shape[0]
            mine = pl.ds(c * (rows // 2), rows // 2) if halved[i] else pl.ds(0, rows)
            for r, (cx, cy) in enumerate(chips):
                cps.append(_rcopy(p_ref.at[mine], f_ref.at[jme, mine], ssem.at[i, r], rsem.at[i, r], (cx, cy, c)))
        return cps

    return _job(packs, [SDS((4,) + p.shape, p.dtype) for p in packs], [dma((n, 3)), dma((n, 3))], copies)


def chip_exchange_job(parts):
    n = len(parts)
    dma = pltpu.SemaphoreType.DMA

    def copies(in_refs, out_refs, sems):
        ssem, rsem = sems
        x, y, c, chips = _place()
        cps = []
        for i, (p_ref, r_ref) in enumerate(zip(in_refs, out_refs)):
            for r, (cx, cy) in enumerate(chips):
                cps.append(_rcopy(p_ref.at[2 * cx + cy], r_ref.at[r], ssem.at[i, r], rsem.at[i, r], (cx, cy, c)))
        return cps

    return _job(parts, [SDS((3,) + p.shape[1:], p.dtype) for p in parts], [dma((n, 3)), dma((n, 3))], copies)


def run_job(job, name):
    ni, no = len(job.ins), len(job.outs)

    def body(*refs):
        job.start(refs[:ni], refs[ni:ni + no], refs[ni + no:])
        job.wait(refs[:ni], refs[ni:ni + no], refs[ni + no:])

    return pl.pallas_call(
        body, in_specs=[ANY] * ni, out_specs=[ANY] * no, out_shape=job.outs, scratch_shapes=job.sems, name=name,
        compiler_params=pltpu.CompilerParams(has_side_effects=True))(*job.ins)


def forward_to_sibling(fulls, name):
    n = len(fulls)

    def body(*refs):
        in_refs, out_refs, (ssem, rsem) = refs[:n], refs[n:2 * n], refs[2 * n:]
        x, y, c, chips = _place()
        cps = []
        for i in range(n):
            half = in_refs[i].shape[1] // 2
            for r, (cx, cy) in enumerate(chips):
                piece = (2 * cx + cy, pl.ds(c * half, half))
                cps.append(_rcopy(in_refs[i].at[piece], out_refs[i].at[piece], ssem.at[i, r], rsem.at[i, r], (x, y, 1 - c)))
        for cp in cps:
            cp.start()
        for cp in cps:
            cp.wait()

    dma = pltpu.SemaphoreType.DMA
    return pl.pallas_call(
        body, in_specs=[ANY] * n, out_specs=[ANY] * n, out_shape=[SDS(f.shape, f.dtype) for f in fulls],
        scratch_shapes=[dma((n, 3)), dma((n, 3))], input_output_aliases={i: i for i in range(n)}, name=name,
        compiler_params=pltpu.CompilerParams(has_side_effects=True))(*fulls)


def rs_sibling_exchange(gpack, name):
    _, rows, W = gpack.shape
    half = rows // 2

    def body(g_ref, r_ref, ssem, rsem):
        x, y, c, _ = _place()
        cps = [_rcopy(g_ref.at[j, pl.ds((1 - c) * half, half)], r_ref.at[j], ssem.at[j], rsem.at[j], (x, y, 1 - c))
               for j in range(4)]
        for cp in cps:
            cp.start()
        for cp in cps:
            cp.wait()

    dma = pltpu.SemaphoreType.DMA
    return pl.pallas_call(
        body, in_specs=[ANY], out_specs=ANY, out_shape=SDS((4, half, W), gpack.dtype),
        scratch_shapes=[dma((4,)), dma((4,))], name=name,
        compiler_params=pltpu.CompilerParams(has_side_effects=True))(gpack)


def _div_tile(n, limit):
    return next(t for t in range(limit - limit % 16, 0, -16) if n % t == 0)


def rs_add_sibling(gpack, recv, cidx, name, out_dtype=F32):
    _, rows, W = gpack.shape
    half = rows // 2
    tr = _div_tile(half, 1024)
    nb = half // tr

    def body(c_ref, g_ref, r_ref, o_ref):
        o_ref[...] = (g_ref[...].astype(F32) + r_ref[...].astype(F32)).astype(o_ref.dtype)

    gs = pltpu.PrefetchScalarGridSpec(
        num_scalar_prefetch=1, grid=(4, nb),
        in_specs=[BS((1, tr, W), lambda j, i, c: (j, c[0] * nb + i, 0)), BS((1, tr, W), lambda j, i, c: (j, i, 0))],
        out_specs=BS((1, tr, W), lambda j, i, c: (j, i, 0)))
    return pl.pallas_call(body, grid_spec=gs, out_shape=SDS((4, half, W), out_dtype), name=name,
                          compiler_params=_cp(("parallel", "parallel")))(cidx, gpack, recv)


def rs_add_chips(recv, part, place, name):
    _, half, W = recv.shape
    tr = _div_tile(half, 640)
    nb = half // tr

    def body(x_ref, y_ref, c_ref, r_ref, own_ref, o_ref):
        r0, r1, r2, own = (t.astype(F32) for t in (r_ref[0], r_ref[1], r_ref[2], own_ref[0]))
        o_ref[...] = ((r0 + r1) + r2) + own

    gs = pltpu.PrefetchScalarGridSpec(
        num_scalar_prefetch=3, grid=(nb,),
        in_specs=[BS((3, tr, W), lambda i, x, y, c: (0, i, 0)), BS((1, tr, W), lambda i, x, y, c: (2 * x[0] + y[0], i, 0))],
        out_specs=BS((tr, W), lambda i, x, y, c: (c[0] * nb + i, 0)))
    return pl.pallas_call(body, grid_spec=gs, out_shape=SDS((2 * half, W), F32), name=name,
                          compiler_params=_cp(("parallel",)))(*place, recv, part)


def rs_sibling_share(gsh, name):
    rows, W = gsh.shape
    half = rows // 2

    def body(g_ref, o_ref, ssem, rsem):
        x, y, c, _ = _place()
        mine = pl.ds(c * half, half)
        cp = _rcopy(g_ref.at[mine], o_ref.at[mine], ssem, rsem, (x, y, 1 - c))
        cp.start()
        cp.wait()

    dma = pltpu.SemaphoreType.DMA
    return pl.pallas_call(
        body, in_specs=[ANY], out_specs=ANY, out_shape=SDS(gsh.shape, gsh.dtype),
        scratch_shapes=[dma, dma], input_output_aliases={0: 0}, name=name,
        compiler_params=pltpu.CompilerParams(has_side_effects=True))(gsh)


def allreduce_small(pack):
    R = pack.shape[0]

    def body(p_ref, o_ref, all_ref, ssem, rsem):
        x, y, c, _ = _place()
        me = 4 * x + 2 * y + c
        all_ref[me] = p_ref[...]
        cps = []
        for m in range(1, 8):
            peer = (1 - x if m & 4 else x, 1 - y if m & 2 else y, 1 - c if m & 1 else c)
            cp = _rcopy(p_ref, all_ref.at[me], ssem.at[m - 1], rsem.at[m - 1], peer)
            cp.start()
            cps.append(cp)
        for cp in cps:
            cp.wait()
        acc = all_ref[0]
        for i in range(1, 8):
            acc = acc + all_ref[i]
        o_ref[...] = acc

    dma = pltpu.SemaphoreType.DMA
    vm = BS(memory_space=pltpu.VMEM)
    return pl.pallas_call(
        body, in_specs=[vm], out_specs=vm, out_shape=SDS(pack.shape, F32),
        scratch_shapes=[pltpu.VMEM((8, R, LANES), F32), dma((7,)), dma((7,))], name="allreduce_small",
        compiler_params=pltpu.CompilerParams(has_side_effects=True))(pack)


PACK_W = 1024
PACK_ALIGN = 32


def _layer_entries(l):
    if l % 2 == 0:
        mixer = [("gdn_w_in", l // 2, D_MODEL, GDN_IN // 4, True), ("gdn_w_out", l // 2, D_MODEL // 4, D_MODEL, False)]
    else:
        mixer = [("dswa_w_in", l // 2, D_MODEL, 3 * DSWA_W // 4, True), ("dswa_w_out", l // 2, DSWA_W // 4, D_MODEL, False)]
    return mixer + [("mlp_w1", l, D_MODEL, D_FF // 4, True), ("mlp_w2", l, D_FF // 4, D_MODEL, False)]


def _entries_offsets(entries):
    offs = [int(o) for o in np.cumsum([0] + [r * c // PACK_W for (_, _, r, c, _) in entries])]
    return offs, -(-offs[-1] // PACK_ALIGN) * PACK_ALIGN


def _layer_offsets(l):
    return _entries_offsets(_layer_entries(l))


def _pack_layer(l, shards, dtype):
    offs, total = _layer_offsets(l)
    parts = [shards[name][li].astype(dtype).reshape(-1, PACK_W) for (name, li, _, _, _) in _layer_entries(l)]
    parts.append(jnp.zeros((total - offs[-1], PACK_W), dtype))
    return jnp.concatenate(parts, axis=0)


def _unpack_layer(l, full, own, jme):
    offs, _ = _layer_offsets(l)
    mats = []
    for e, (_, _, r, c, by_col) in enumerate(_layer_entries(l)):
        mine = own[offs[e]:offs[e + 1]]
        sh = [jnp.where(jme == j, mine, full[j, offs[e]:offs[e + 1]]).reshape(r, c) for j in range(4)]
        mats.append(jnp.concatenate(sh, axis=1 if by_col else 0))
    return mats


def _pack_grads(entries, grads):
    offs, total = _entries_offsets(entries)
    per_chip = []
    for j in range(4):
        parts = []
        for g, (_, _, r, c, by_col) in zip(grads, entries):
            sh = g[:, c * j:c * (j + 1)] if by_col else g[r * j:r * (j + 1), :]
            parts.append(sh.astype(BF16).reshape(-1, PACK_W))
        parts.append(jnp.zeros((total - offs[-1], PACK_W), BF16))
        per_chip.append(jnp.concatenate(parts, axis=0))
    return jnp.stack(per_chip)


def _unpack_shard_grads(units):
    out = {}
    for entries, gsh in units:
        offs, _ = _entries_offsets(entries)
        for e, (name, _, r, c, _) in enumerate(entries):
            out.setdefault(name, []).append(gsh[offs[e]:offs[e + 1]].reshape(r, c))
    return {k: jnp.stack(v) for k, v in out.items()}


def _flat_pad(t, mult=8 * LANES):
    f = t.reshape(-1)
    return jnp.pad(f, (0, (-f.shape[0]) % mult))


def kernel(x, norm_mix, norm_mlp, norm_final, rel_bias, gdn_w_in, gdn_conv_w, gdn_a_log, gdn_dt_bias, gdn_norm_w, gdn_w_out, dswa_w_in, dswa_w_out, mlp_w1, mlp_w2, loss_target, m_norm_mix, m_norm_mlp, m_norm_final, m_rel_bias, m_gdn_w_in, m_gdn_conv_w, m_gdn_a_log, m_gdn_dt_bias, m_gdn_norm_w, m_gdn_w_out, m_dswa_w_in, m_dswa_w_out, m_mlp_w1, m_mlp_w2, v_norm_mix, v_norm_mlp, v_norm_final, v_rel_bias, v_gdn_w_in, v_gdn_conv_w, v_gdn_a_log, v_gdn_dt_bias, v_gdn_norm_w, v_gdn_w_out, v_dswa_w_in, v_dswa_w_out, v_mlp_w1, v_mlp_w2):
    xi, yi, ci = lax.axis_index("x"), lax.axis_index("y"), lax.axis_index("c")
    jme = 2 * xi + yi
    big = dict(gdn_w_in=gdn_w_in, gdn_w_out=gdn_w_out, dswa_w_in=dswa_w_in, dswa_w_out=dswa_w_out, mlp_w1=mlp_w1, mlp_w2=mlp_w2)
    n_gdn = gdn_w_in.shape[0]
    conv_cols = gdn_conv_w.shape[-1]

    packs = [_pack_layer(l, big, BF16) for l in range(DEPTH)]
    convp = jnp.pad(gdn_conv_w.reshape(n_gdn * GDN_CONV, conv_cols), ((0, 16 - n_gdn * GDN_CONV), (0, 0)))
    raw0, cfull = run_job(gather_job([packs[0], convp], [True, False]), "gather_l0")
    fulls = {0: forward_to_sibling([raw0], "forward_l0")[0]}
    cfull = jnp.where((jnp.arange(4) == jme)[:, None, None], convp[None], cfull)
    conv_all = jnp.transpose(cfull[:, :n_gdn * GDN_CONV], (1, 0, 2)).reshape(n_gdn, GDN_CONV, 4 * conv_cols)
    conv_all = jnp.pad(conv_all, ((0, 0), (0, 8 - GDN_CONV), (0, 0)))
    fwd_jobs = {0: [1, 2], 2: [3]}

    xs = x[0]
    saved = []
    for l in range(DEPTH):
        w_in, w_out, w1, w2 = _unpack_layer(l, fulls[l], packs[l], jme)
        gm, gp = norm_mix[l][None], norm_mlp[l][None]
        a = l // 2
        if l % 2 == 0:
            w_in = jnp.pad(w_in, ((0, 0), (0, GDN_INP - GDN_IN)))
            job = gather_job([packs[t] for t in fwd_jobs[l]], [True] * len(fwd_jobs[l]))
            x_mid, sv, raws = gdn_fwd(xs, gm, w_in, conv_all[a], gdn_a_log[a], gdn_dt_bias[a], gdn_norm_w[a][None], w_out,
                                      f"l{l}_gdn", job)
            for t, f in zip(fwd_jobs[l], forward_to_sibling(raws, f"forward_from_l{l}")):
                fulls[t] = f
        else:
            x_mid, sv = dswa_fwd(xs, gm, w_in, w_out, rel_bias, f"l{l}_att")
        x_out, sv2 = mlp_fwd(x_mid, gp, w1, w2, f"l{l}_mlp")
        saved.append((xs, x_mid, (w_in, w_out, w1, w2), sv, sv2))
        xs = x_out

    cidx = ci.astype(jnp.int32).reshape(1)
    place = [t.astype(jnp.int32).reshape(1) for t in (xi, yi, ci)]
    units = {"0a": _layer_entries(0)[:2], "0b": _layer_entries(0)[2:], **{str(l): _layer_entries(l) for l in (1, 2, 3)}}

    def chip_partial(u, grads):
        gpack = _pack_grads(units[u], grads)
        return rs_add_sibling(gpack, rs_sibling_exchange(gpack, f"rs_sibling_{u}"), cidx, f"rs_add_sibling_{u}", BF16)

    def finish(u, recv):
        return rs_sibling_share(rs_add_chips(recv, parts[u], place, f"rs_add_chips_{u}"), f"rs_share_{u}")

    loss_part, dx, dxb, d_final = loss_head(xs, norm_final[None], loss_target[0], "loss_head")
    d_mix, d_mlp = [None] * DEPTH, [None] * DEPTH
    d_conv, d_alog, d_dt, d_nw = [None] * n_gdn, [None] * n_gdn, [None] * n_gdn, [None] * n_gdn
    d_rel = jnp.zeros_like(rel_bias)
    parts, gshs = {}, {}
    bwd_jobs = {2: ["3"], 0: ["2", "1", "0b"]}
    for l in reversed(range(DEPTH)):
        x_in, x_mid, (w_in, w_out, w1, w2), sv, sv2 = saved[l]
        gm, gp = norm_mix[l][None], norm_mlp[l][None]
        a = l // 2
        dx, dxb, d_mlp[l], dw1, dw2 = mlp_bwd(x_mid, gp, w1, w2, sv2, dx, dxb, f"l{l}_mlp")
        if l == 0:
            parts["0b"] = chip_partial("0b", [dw1, dw2])
        if l % 2 == 0:
            job = chip_exchange_job([parts[u] for u in bwd_jobs[l]])
            dx, dxb, d_mix[l], dw_all, d_conv[a], d_alog[a], d_dt[a], d_nw[a], dwo, recvs = gdn_bwd(
                x_in, gm, w_in, conv_all[a], gdn_norm_w[a][None], w_out, sv, dx, dxb, f"l{l}_gdn", job)
            for u, rv in zip(bwd_jobs[l], recvs):
                gshs[u] = finish(u, rv)
            dwi = dw_all[:, :GDN_IN]
        else:
            dx, dxb, d_mix[l], dwi, dwo, drel = dswa_bwd(x_in, gm, w_in, w_out, rel_bias, sv, dx, dxb, f"l{l}_att")
            d_rel = d_rel + drel
        if l == 0:
            parts["0a"] = chip_partial("0a", [dwi, dwo])
        else:
            parts[str(l)] = chip_partial(str(l), [dwi, dwo, dw1, dw2])
    gshs["0a"] = finish("0a", run_job(chip_exchange_job([parts["0a"]]), "rs_chip_exchange_0a")[0])
    gbig = _unpack_shard_grads([(units[u], gshs[u]) for u in ("0a", "0b", "1", "2", "3")])

    small = [jnp.concatenate(d_mix, axis=0), jnp.concatenate(d_mlp, axis=0), d_final, d_rel,
             jnp.stack(d_conv), jnp.stack(d_alog), jnp.stack(d_dt), jnp.concatenate(d_nw, axis=0)]
    flat = [_flat_pad(t) for t in small]
    sizes = [f.shape[0] for f in flat]
    red = allreduce_small(jnp.concatenate(flat).reshape(-1, LANES)).reshape(-1)
    offs = np.cumsum([0] + sizes)
    red = [red[offs[i]:offs[i] + small[i].size].reshape(small[i].shape) for i in range(len(small))]
    g_conv_all = red[4][:, :GDN_CONV].reshape(n_gdn, GDN_CONV, 1, 4 * conv_cols)
    g_conv = lax.dynamic_slice_in_dim(g_conv_all, jme * conv_cols, conv_cols, axis=3)
    g = dict(norm_mix=red[0], norm_mlp=red[1], norm_final=red[2].reshape(norm_final.shape), rel_bias=red[3],
             gdn_conv_w=g_conv, gdn_a_log=red[5], gdn_dt_bias=red[6], gdn_norm_w=red[7][:, :GDN_DK], **gbig)

    w = dict(norm_mix=norm_mix, norm_mlp=norm_mlp, norm_final=norm_final, rel_bias=rel_bias, gdn_conv_w=gdn_conv_w,
             gdn_a_log=gdn_a_log, gdn_dt_bias=gdn_dt_bias, gdn_norm_w=gdn_norm_w, **big)
    m = dict(norm_mix=m_norm_mix, norm_mlp=m_norm_mlp, norm_final=m_norm_final, rel_bias=m_rel_bias, gdn_w_in=m_gdn_w_in,
             gdn_conv_w=m_gdn_conv_w, gdn_a_log=m_gdn_a_log, gdn_dt_bias=m_gdn_dt_bias, gdn_norm_w=m_gdn_norm_w,
             gdn_w_out=m_gdn_w_out, dswa_w_in=m_dswa_w_in, dswa_w_out=m_dswa_w_out, mlp_w1=m_mlp_w1, mlp_w2=m_mlp_w2)
    v = dict(norm_mix=v_norm_mix, norm_mlp=v_norm_mlp, norm_final=v_norm_final, rel_bias=v_rel_bias, gdn_w_in=v_gdn_w_in,
             gdn_conv_w=v_gdn_conv_w, gdn_a_log=v_gdn_a_log, gdn_dt_bias=v_gdn_dt_bias, gdn_norm_w=v_gdn_norm_w,
             gdn_w_out=v_gdn_w_out, dswa_w_in=v_dswa_w_in, dswa_w_out=v_dswa_w_out, mlp_w1=v_mlp_w1, mlp_w2=v_mlp_w2)
    names = ["norm_mix", "norm_mlp", "norm_final", "rel_bias", "gdn_w_in", "gdn_conv_w", "gdn_a_log", "gdn_dt_bias",
             "gdn_norm_w", "gdn_w_out", "dswa_w_in", "dswa_w_out", "mlp_w1", "mlp_w2"]
    upd = {n: adamw(w[n], g[n], m[n], v[n], f"adamw_{n}") for n in names}
    loss = lax.psum(loss_part[0, 0], ("x", "y", "c"))
    return (loss, dx[None], *[g[n] for n in names], *[upd[n][0] for n in names], *[upd[n][1] for n in names],
            *[upd[n][2] for n in names])
```

```python
import math
from typing import Callable, NamedTuple

import numpy as np
import jax
import jax.numpy as jnp
from jax import lax
from jax.experimental import pallas as pl
from jax.experimental.pallas import tpu as pltpu

F32 = jnp.float32
BF16 = jnp.bfloat16
HI = lax.Precision.HIGHEST
BS = pl.BlockSpec
SDS = jax.ShapeDtypeStruct
MESH = pl.DeviceIdType.MESH
ANY = BS(memory_space=pl.ANY)

D_MODEL = 1024
D_FF = 4096
DEPTH = 4
RMS_EPS = 1e-6
NEG_INF = -1e30
LANES = 128
VMEM_LIMIT = 56 << 20

GDN_H = 8
GDN_DK = 128
GDN_CONV = 5
GDN_C = 64
GDN_GC = 8
GDN_HP_FWD = 8
GDN_HP_BWD = 4
GDN_QKV = 3 * GDN_H * GDN_DK
GDN_IN = GDN_QKV + GDN_H * GDN_DK + 4 * GDN_H
GDN_INP = 4224

DSWA_CFG = ((128, 1), (512, 4), (2048, 16))
DSWA_HG = 6
DSWA_E = 64
DSWA_HEADS = 18
DSWA_W = DSWA_HEADS * DSWA_E
DSWA_HALF = 64
REL_BUCKETS = 32
REL_MAX_DIST = 1024

ADAM_LR = 0.001
ADAM_B1 = 0.9
ADAM_B2 = 0.999
ADAM_EPS = 1e-08
ADAM_WD = 0.01
ADAM_STEP = 10


def _cp(sem=None):
    return pltpu.CompilerParams(dimension_semantics=sem, vmem_limit_bytes=VMEM_LIMIT)


def _dot(a, b, prec=None):
    return jnp.dot(a, b, precision=prec, preferred_element_type=F32)


def _dot_nt(a, b, prec=None):
    return lax.dot_general(a, b, (((1,), (1,)), ((), ())), precision=prec, preferred_element_type=F32)


def _dot_tn(a, b, prec=None):
    return lax.dot_general(a, b, (((0,), (0,)), ((), ())), precision=prec, preferred_element_type=F32)


def _bf(a):
    return a.astype(BF16)


def _sigmoid(x):
    return 1.0 / (1.0 + jnp.exp(-x))


def rms_fwd(x, g, name):
    S, Dm = x.shape
    tm = min(512, S)

    def body(x_ref, g_ref, o_ref, ot_ref):
        xv = x_ref[...]
        r = lax.rsqrt(jnp.mean(xv * xv, axis=-1, keepdims=True) + RMS_EPS)
        hb = (xv * r * g_ref[...]).astype(o_ref.dtype)
        o_ref[...] = hb
        ot_ref[...] = hb.T

    return pl.pallas_call(
        body, grid=(S // tm,),
        in_specs=[BS((tm, Dm), lambda i: (i, 0)), BS((1, Dm), lambda i: (0, 0))],
        out_specs=[BS((tm, Dm), lambda i: (i, 0)), BS((Dm, tm), lambda i: (0, i))],
        out_shape=[SDS((S, Dm), BF16), SDS((Dm, S), BF16)], name=name, compiler_params=_cp(("parallel",)))(x, g)


def rms_bwd(x, g, dh, dres, name):
    S, Dm = x.shape
    tm = min(512, S)

    def body(x_ref, g_ref, dh_ref, dres_ref, dx_ref, dxb_ref, dg_ref):
        i = pl.program_id(0)
        xv = x_ref[...]
        r = lax.rsqrt(jnp.mean(xv * xv, axis=-1, keepdims=True) + RMS_EPS)
        n = xv * r
        dhv = dh_ref[...]
        t = dhv * g_ref[...]
        dx = dres_ref[...] + r * (t - n * jnp.mean(n * t, axis=-1, keepdims=True))
        dx_ref[...] = dx
        dxb_ref[...] = dx.astype(BF16)
        part = jnp.sum(dhv * n, axis=0, keepdims=True)

        @pl.when(i == 0)
        def _():
            dg_ref[...] = part

        @pl.when(i > 0)
        def _():
            dg_ref[...] += part

    row = BS((tm, Dm), lambda i: (i, 0))
    vec = BS((1, Dm), lambda i: (0, 0))
    return pl.pallas_call(
        body, grid=(S // tm,), in_specs=[row, vec, row, row], out_specs=[row, row, vec],
        out_shape=[SDS((S, Dm), F32), SDS((S, Dm), BF16), SDS((1, Dm), F32)],
        name=name, compiler_params=_cp(("arbitrary",)))(x, g, dh, dres)


def loss_head(x, g, tgt, name):
    S, Dm = x.shape
    tm = min(512, S)

    def body(x_ref, g_ref, t_ref, loss_ref, dx_ref, dxb_ref, dg_ref):
        i = pl.program_id(0)
        xv = x_ref[...]
        gv = g_ref[...]
        r = lax.rsqrt(jnp.mean(xv * xv, axis=-1, keepdims=True) + RMS_EPS)
        n = xv * r
        err = n * gv - t_ref[...]
        lpart = 0.5 * jnp.sum(jnp.mean(err * err, axis=-1, keepdims=True), axis=0, keepdims=True)
        dout = err * (1.0 / Dm)
        t = dout * gv
        dx = r * (t - n * jnp.mean(n * t, axis=-1, keepdims=True))
        dx_ref[...] = dx
        dxb_ref[...] = dx.astype(BF16)
        part = jnp.sum(dout * n, axis=0, keepdims=True)

        @pl.when(i == 0)
        def _():
            dg_ref[...] = part
            loss_ref[...] = lpart

        @pl.when(i > 0)
        def _():
            dg_ref[...] += part
            loss_ref[...] += lpart

    row = BS((tm, Dm), lambda i: (i, 0))
    vec = BS((1, Dm), lambda i: (0, 0))
    one = BS((1, 1), lambda i: (0, 0))
    return pl.pallas_call(
        body, grid=(S // tm,), in_specs=[row, vec, row], out_specs=[one, row, row, vec],
        out_shape=[SDS((1, 1), F32), SDS((S, Dm), F32), SDS((S, Dm), BF16), SDS((1, Dm), F32)],
        name=name, compiler_params=_cp(("arbitrary",)))(x, g, tgt)


MM_DEEP_K = 2048
MM_SHALLOW_K = 1024


def mm(a, b, *, name, ta=False, tb=False, tm=1024, tn=512, out_dtype=F32, pre_a=None, epi=None, extras=()):
    M, K = (a.shape[1], a.shape[0]) if ta else a.shape
    N = b.shape[0] if tb else b.shape[1]
    if ta:
        tm = min(tm, 512)
    elif K > MM_DEEP_K and N % 256 == 0:
        tn = min(tn, 256)
    elif K <= MM_SHALLOW_K:
        tm = 2 * tm
    tm, tn = min(tm, M), min(tn, N)
    assert M % tm == 0 and N % tn == 0, (name, M, N, K, tm, tn)
    ne = len(extras)
    a_spec = BS((K, tm), lambda i, j: (0, i)) if ta else BS((tm, K), lambda i, j: (i, 0))
    b_spec = BS((tn, K), lambda i, j: (j, 0)) if tb else BS((K, tn), lambda i, j: (0, j))
    o_spec = BS((tm, tn), lambda i, j: (i, j))
    dims = (((0 if ta else 1,), (1 if tb else 0,)), ((), ()))

    def body(a_ref, b_ref, *rest):
        e_refs, o_ref = rest[:ne], rest[ne]
        av = a_ref[...]
        if pre_a is not None:
            av = pre_a(av)
        acc = lax.dot_general(_bf(av), _bf(b_ref[...]), dims, preferred_element_type=F32)
        res = epi(acc, *[e[...] for e in e_refs]) if epi is not None else acc
        o_ref[...] = res.astype(o_ref.dtype)

    return pl.pallas_call(
        body, grid=(M // tm, N // tn), in_specs=[a_spec, b_spec] + [o_spec] * ne, out_specs=o_spec,
        out_shape=SDS((M, N), out_dtype), name=name, compiler_params=_cp(("parallel", "parallel")))(a, b, *extras)


def _relu(acc):
    return jnp.maximum(acc, 0.0)


def _add(acc, res):
    return acc + res


def _sq(av):
    return av * av


def _times_2r(acc, r):
    return acc * (2.0 * r.astype(F32))


def mlp_fwd(x, g, w1, w2, tag):
    h, ht = rms_fwd(x, g, f"{tag}_rms")
    r = mm(h, w1, name=f"{tag}_up", tn=1024, out_dtype=BF16, epi=_relu)
    xn = mm(r, w2, name=f"{tag}_down", pre_a=_sq, epi=_add, extras=(x,))
    return xn, (ht, r)


def mlp_bwd(x, g, w1, w2, saved, dx, dxb, tag):
    ht, r = saved
    da = mm(dxb, w2, name=f"{tag}_dact", tb=True, tn=1024, out_dtype=BF16, epi=_times_2r, extras=(r,))
    dw2 = mm(r, dxb, name=f"{tag}_dw2", ta=True, pre_a=_sq)
    dw1 = mm(ht, da, name=f"{tag}_dw1")
    dx, dxb, dg = mm_rms_bwd(da, w1, x, g, dx, f"{tag}_dh_rmsb")
    return dx, dxb, dg, dw1, dw2


def mm_rms_bwd(da, w, x, g, dres, name):
    S, K = da.shape
    Dm = w.shape[0]
    tm = min(512, S)

    def body(a_ref, w_ref, x_ref, g_ref, dres_ref, dx_ref, dxb_ref, dg_ref):
        i = pl.program_id(0)
        dhv = _dot_nt(a_ref[...], w_ref[...])
        xv = x_ref[...]
        r = lax.rsqrt(jnp.mean(xv * xv, axis=-1, keepdims=True) + RMS_EPS)
        n = xv * r
        t = dhv * g_ref[...]
        dx = dres_ref[...] + r * (t - n * jnp.mean(n * t, axis=-1, keepdims=True))
        dx_ref[...] = dx
        dxb_ref[...] = dx.astype(BF16)
        part = jnp.sum(dhv * n, axis=0, keepdims=True)

        @pl.when(i == 0)
        def _():
            dg_ref[...] = part

        @pl.when(i > 0)
        def _():
            dg_ref[...] += part

    row = BS((tm, Dm), lambda i: (i, 0))
    vec = BS((1, Dm), lambda i: (0, 0))
    return pl.pallas_call(
        body, grid=(S // tm,),
        in_specs=[BS((tm, K), lambda i: (i, 0)), BS((Dm, K), lambda i: (0, 0)), row, vec, row],
        out_specs=[row, row, vec],
        out_shape=[SDS((S, Dm), F32), SDS((S, Dm), BF16), SDS((1, Dm), F32)],
        name=name, compiler_params=_cp(("arbitrary",)))(da, w, x, g, dres)


def _conv_taps(x, S):
    t = lax.broadcasted_iota(jnp.int32, x.shape, 0)
    taps = []
    for j in range(GDN_CONV):
        sh = j - GDN_CONV // 2
        xs = x if sh == 0 else pltpu.roll(x, (-sh) % S, 0)
        taps.append(jnp.where((t + sh >= 0) & (t + sh < S), xs, 0.0))
    return taps


def _qkv_scale(c):
    is_norm = c < 2 * GDN_H
    scale = jnp.where(c < GDN_H, GDN_DK ** -0.5, 1.0)
    return is_norm, scale


def gdn_pre_fwd(proj, convw, name):
    S = proj.shape[0]

    def body(p_ref, w_ref, o_ref):
        c = pl.program_id(0)
        x = p_ref[...]
        w = w_ref[...]
        y = jnp.zeros_like(x)
        for j, xs in enumerate(_conv_taps(x, S)):
            y = y + w[j:j + 1, :] * xs
        t = y * _sigmoid(y)
        is_norm, scale = _qkv_scale(c)
        r = lax.rsqrt(jnp.sum(t * t, axis=-1, keepdims=True) + 1e-6)
        o_ref[...] = jnp.where(is_norm, t * r * scale, t)

    return pl.pallas_call(
        body, grid=(GDN_QKV // LANES,),
        in_specs=[BS((S, LANES), lambda c: (0, c)), BS((8, LANES), lambda c: (0, c))],
        out_specs=BS((S, LANES), lambda c: (0, c)),
        out_shape=SDS((S, GDN_QKV), F32), name=name, compiler_params=_cp(("parallel",)))(proj, convw)


def gdn_pre_bwd(proj, convw, dqkv, name):
    S = proj.shape[0]

    def body(p_ref, w_ref, d_ref, dp_ref, dw_ref):
        c = pl.program_id(0)
        x = p_ref[...]
        w = w_ref[...]
        taps = _conv_taps(x, S)
        y = jnp.zeros_like(x)
        for j, xs in enumerate(taps):
            y = y + w[j:j + 1, :] * xs
        sg = _sigmoid(y)
        t = y * sg
        is_norm, scale = _qkv_scale(c)
        dout = d_ref[0, 0] + d_ref[1, 0]
        r = lax.rsqrt(jnp.sum(t * t, axis=-1, keepdims=True) + 1e-6)
        n = t * r
        dn = dout * scale
        dt_norm = r * (dn - n * jnp.sum(dn * n, axis=-1, keepdims=True))
        dt = jnp.where(is_norm, dt_norm, dout)
        dy = dt * (sg * (1.0 + y * (1.0 - sg)))
        row = lax.broadcasted_iota(jnp.int32, (8, LANES), 0)
        dw = jnp.zeros((8, LANES), F32)
        for j, xs in enumerate(taps):
            dw = dw + jnp.where(row == j, jnp.sum(dy * xs, axis=0, keepdims=True), 0.0)
        dw_ref[...] = dw
        tt = lax.broadcasted_iota(jnp.int32, x.shape, 0)
        dx = jnp.zeros_like(x)
        for j in range(GDN_CONV):
            sh = j - GDN_CONV // 2
            ds = dy if sh == 0 else pltpu.roll(dy, sh % S, 0)
            dx = dx + w[j:j + 1, :] * jnp.where((tt - sh >= 0) & (tt - sh < S), ds, 0.0)
        dp_ref[...] = dx.astype(BF16)

    return pl.pallas_call(
        body, grid=(GDN_QKV // LANES,),
        in_specs=[BS((S, LANES), lambda c: (0, c)), BS((8, LANES), lambda c: (0, c)),
                  BS((2, 1, S, LANES), lambda c: (0, c // GDN_H, 0, c % GDN_H))],
        out_specs=[BS((S, LANES), lambda c: (0, c)), BS((8, LANES), lambda c: (0, c))],
        out_shape=[SDS((S, GDN_QKV), BF16), SDS((8, GDN_QKV), F32)],
        name=name, compiler_params=_cp(("parallel",)))(proj, convw, dqkv)


def _chunk_sum_matrix(n, upper):
    i = lax.broadcasted_iota(jnp.int32, (n, n), 0)
    j = lax.broadcasted_iota(jnp.int32, (n, n), 1)
    same = (i // GDN_C) == (j // GDN_C)
    tri = (i <= j) if upper else (i >= j)
    return jnp.where(same & tri, 1.0, 0.0).astype(F32)


def _gate_lanes(shape):
    lane = lax.broadcasted_iota(jnp.int32, shape, 1)
    return lane < GDN_H, (lane >= GDN_H) & (lane < 2 * GDN_H), (lane >= 2 * GDN_H) & (lane < 4 * GDN_H)


def gdn_gate_fwd(proj, prm, name):
    S = proj.shape[0]
    tm = min(512, S)
    ct = GDN_INP // LANES - 1

    def body(p_ref, prm_ref, o_ref):
        ab = p_ref[...]
        a_log = prm_ref[0:1, :]
        dtb = prm_ref[1:2, :]
        z = ab + dtb
        sp = jnp.maximum(z, 0.0) + jnp.log(1.0 + jnp.exp(-jnp.abs(z)))
        g = -jnp.exp(a_log) * sp
        is_f, is_b, is_beta = _gate_lanes(ab.shape)
        gf = _dot(_chunk_sum_matrix(tm, False), jnp.where(is_f, g, 0.0), HI)
        gbk = _dot(_chunk_sum_matrix(tm, True), jnp.where(is_b, g, 0.0), HI)
        o_ref[...] = gf + gbk + jnp.where(is_beta, _sigmoid(ab), 0.0)

    return pl.pallas_call(
        body, grid=(S // tm,),
        in_specs=[BS((tm, LANES), lambda i: (i, ct)), BS((8, LANES), lambda i: (0, 0))],
        out_specs=BS((tm, LANES), lambda i: (i, 0)),
        out_shape=SDS((S, LANES), F32), name=name, compiler_params=_cp(("parallel",)))(proj, prm)


def gdn_gate_bwd(proj, prm, dgb, name):
    S = proj.shape[0]
    tm = min(512, S)
    ct = GDN_INP // LANES - 1

    def body(p_ref, prm_ref, d_ref, dab_ref, dprm_ref):
        i = pl.program_id(0)
        ab = p_ref[...]
        a_log = prm_ref[0:1, :]
        dtb = prm_ref[1:2, :]
        z = ab + dtb
        sp = jnp.maximum(z, 0.0) + jnp.log(1.0 + jnp.exp(-jnp.abs(z)))
        ea = jnp.exp(a_log)
        g = -ea * sp
        is_f, is_b, is_beta = _gate_lanes(ab.shape)
        d = d_ref[...]
        dg = (_dot_tn(_chunk_sum_matrix(tm, False), jnp.where(is_f, d, 0.0), HI)
              + _dot_tn(_chunk_sum_matrix(tm, True), jnp.where(is_b, d, 0.0), HI))
        da = dg * (-ea) * _sigmoid(z)
        beta = _sigmoid(ab)
        dab_ref[...] = jnp.where(is_beta, d * beta * (1.0 - beta), da).astype(BF16)
        row = lax.broadcasted_iota(jnp.int32, (8, LANES), 0)
        part = (jnp.where(row == 0, jnp.sum(dg * g, axis=0, keepdims=True), 0.0)
                + jnp.where(row == 1, jnp.sum(da, axis=0, keepdims=True), 0.0))

        @pl.when(i == 0)
        def _():
            dprm_ref[...] = part

        @pl.when(i > 0)
        def _():
            dprm_ref[...] += part

    return pl.pallas_call(
        body, grid=(S // tm,),
        in_specs=[BS((tm, LANES), lambda i: (i, ct)), BS((8, LANES), lambda i: (0, 0)), BS((tm, LANES), lambda i: (i, 0))],
        out_specs=[BS((tm, LANES), lambda i: (i, 0)), BS((8, LANES), lambda i: (0, 0))],
        out_shape=[SDS((S, LANES), BF16), SDS((8, LANES), F32)],
        name=name, compiler_params=_cp(("arbitrary",)))(proj, prm, dgb)


def _tri_masks(d):
    i = lax.broadcasted_iota(jnp.int32, (GDN_C, GDN_C), 0)
    j = lax.broadcasted_iota(jnp.int32, (GDN_C, GDN_C), 1)
    s = (i - j) * (1 - 2 * d)
    return s >= 0, s > 0


def _split(a):
    hi = _bf(a)
    return hi, _bf(a - hi.astype(F32))


def _dot3(a, b):
    return _dot(a[0], b[0]) + (_dot(a[0], b[1]) + _dot(a[1], b[0]))


def _inv_unit_tri_many(mats):
    i = lax.broadcasted_iota(jnp.int32, mats[0].shape, 0)
    j = lax.broadcasted_iota(jnp.int32, mats[0].shape, 1)
    eye = jnp.where(i == j, 1.0, 0.0)
    ms = [-a for a in mats]
    ps = [eye + m for m in ms]
    for _ in range(int(math.log2(GDN_C)) - 1):
        sp = [_split(m) for m in ms]
        ms = [_dot3(s, s) for s in sp]
        sp = [_split(m) for m in ms]
        pp = [_split(p) for p in ps]
        ps = [p + _dot3(a, b) for p, a, b in zip(ps, pp, sp)]
    return ps


def _lane_col(x, lane_idx):
    lane = lax.broadcasted_iota(jnp.int32, x.shape, 1)
    return jnp.sum(jnp.where(lane == lane_idx, x, 0.0), axis=1, keepdims=True)


def _chunk_gates(gb_ref, grow_ref, hh, ci, d, head):
    gbv = gb_ref[ci * GDN_C:(ci + 1) * GDN_C, :]
    gcol = _lane_col(gbv, d * GDN_H + head)
    bcol = _lane_col(gbv, 2 * GDN_H + d * GDN_H + head)
    glast = jnp.where(d == 0, gcol[GDN_C - 1:GDN_C, :], gcol[0:1, :])
    return gcol, bcol, grow_ref[hh, ci:ci + 1, :], glast


def _chunk_base(q, k, gcol, grow, bcol, glast, d):
    incl, strict = _tri_masks(d)
    decay = jnp.where(incl, jnp.exp(jnp.where(incl, gcol - grow, 0.0)), 0.0)
    kb = k * bcol
    kk = _dot_nt(_bf(kb), _bf(k))
    qk = _dot_nt(_bf(q), _bf(k))
    eg = jnp.exp(gcol)
    ek = jnp.exp(glast - gcol)
    return dict(incl=incl, strict=strict, decay=decay, kb=kb, kk=kk, qk=qk, eg=eg, ek=ek, q_dec=q * eg, k_dec=k * ek,
                bcol=bcol, glast=glast)


def _block_terms(q_ref, k_ref, v_ref, gb_ref, grow_ref, d, h, hp):
    keys = [(hh, ci) for hh in range(hp) for ci in range(GDN_GC)]
    ts = []
    for hh, ci in keys:
        rows = slice(ci * GDN_C, (ci + 1) * GDN_C)
        cols = slice(hh * GDN_DK, (hh + 1) * GDN_DK)
        gcol, bcol, grow_v, glast = _chunk_gates(gb_ref, grow_ref, hh, ci, d, h * hp + hh)
        t = _chunk_base(q_ref[rows, cols], k_ref[rows, cols], gcol, grow_v, bcol, glast, d)
        t["v"] = v_ref[rows, cols]
        ts.append(t)
    tinvs = _inv_unit_tri_many([jnp.where(t["strict"], t["kk"] * t["decay"], 0.0) for t in ts])
    sp = [_split(x) for x in tinvs]
    us = [_dot3(s, _split(t["v"] * t["bcol"])) for s, t in zip(sp, ts)]
    ws = [_dot3(s, _split(t["kb"] * t["eg"])) for s, t in zip(sp, ts)]
    for t, tinv, u, w in zip(ts, tinvs, us, ws):
        t.update(tinv=tinv, u=u, w=w)
    return keys, ts


def _gdn_specs(S, nblk, order, hp):
    R = GDN_GC * GDN_C
    wd = hp * GDN_DK
    hb = GDN_H // hp

    def qkv_spec(part):
        return BS((R, wd), lambda d, h, n: (order(d, n), part * hb + h))

    gb_spec = BS((R, LANES), lambda d, h, n: (order(d, n), 0))
    grow_spec = BS((hp, GDN_GC, GDN_C), lambda d, h, n: (d * hb + h, order(d, n), 0))
    st_spec = BS((1, hp, GDN_GC, GDN_DK, GDN_DK), lambda d, h, n: (d, h, order(d, n), 0, 0))
    return qkv_spec, gb_spec, grow_spec, st_spec


def _lane_row(x):
    return jnp.broadcast_to(x, (1, LANES))


def _side_parts(side):
    if side is None:
        return [], [], [], [], []
    return [ANY] * len(side.ins), [ANY] * len(side.outs), list(side.outs), list(side.sems), list(side.ins)


def _side_run(side, refs, n_in, n_out, n_scr, first, last):
    if side is None:
        return
    ns, no, nm = len(side.ins), len(side.outs), len(side.sems)
    s_in = refs[n_in:n_in + ns]
    s_out = refs[n_in + ns + n_out:n_in + ns + n_out + no]
    s_sem = refs[len(refs) - nm:]

    @pl.when(first)
    def _():
        side.start(s_in, s_out, s_sem)

    @pl.when(last)
    def _():
        side.wait(s_in, s_out, s_sem)


def gdn_scan_fwd(qkv, gb, grow, name, side=None):
    S = qkv.shape[0]
    R = GDN_GC * GDN_C
    nblk = S // R
    nc = S // GDN_C
    hp = GDN_HP_FWD
    wd = hp * GDN_DK
    heads = range(hp)

    def order(d, n):
        return n + d * (nblk - 1 - 2 * n)

    qkv_spec, gb_spec, grow_spec, st_spec = _gdn_specs(S, nblk, order, hp)

    s_in, s_out, s_shape, s_scr_shapes, s_ops = _side_parts(side)
    hb = GDN_H // hp

    def body(*refs):
        q_ref, k_ref, v_ref, gb_ref, grow_ref = refs[:5]
        o_ref, st_ref = refs[5 + len(s_in):7 + len(s_in)]
        s_scr, u_scr, w_scr, qd_scr, kd_scr, in_scr, egl_scr = refs[7 + len(s_in) + len(s_out):14 + len(s_in) + len(s_out)]
        d = pl.program_id(0)
        h = pl.program_id(1)
        n = pl.program_id(2)
        _side_run(side, refs, 5, 2, 7, (d == 0) & (h == 0) & (n == 0), (d == 1) & (h == hb - 1) & (n == nblk - 1))

        @pl.when(n == 0)
        def _():
            s_scr[...] = jnp.zeros_like(s_scr)

        keys, ts = _block_terms(q_ref, k_ref, v_ref, gb_ref, grow_ref, d, h, hp)
        for (hh, ci), t in zip(keys, ts):
            u_scr[hh, ci] = t["u"]
            w_scr[hh, ci] = _bf(t["w"])
            qd_scr[hh, ci] = _bf(t["q_dec"])
            kd_scr[hh, ci] = _bf(t["k_dec"])
            in_scr[hh, ci] = _bf(jnp.where(t["incl"], t["qk"] * t["decay"], 0.0))
            egl_scr[hh, ci] = _lane_row(jnp.exp(t["glast"]))

        def chunk(cc, carry):
            ci = cc + d * (GDN_GC - 1 - 2 * cc)
            rows = pl.ds(pl.multiple_of(ci * GDN_C, GDN_C), GDN_C)
            sts = [s_scr[hh] for hh in heads]
            for hh in heads:
                st_ref[0, hh, ci] = sts[hh]
            sbs = [_bf(st) for st in sts]
            vns = [_bf(u_scr[hh, ci] - _dot(w_scr[hh, ci], sbs[hh])) for hh in heads]
            for hh in heads:
                s_scr[hh] = sts[hh] * egl_scr[hh, ci] + _dot_tn(kd_scr[hh, ci], vns[hh])
            for hh in heads:
                o_ref[0, rows, hh * GDN_DK:(hh + 1) * GDN_DK] = _dot(qd_scr[hh, ci], sbs[hh]) + _dot(in_scr[hh, ci], vns[hh])
            return carry

        lax.fori_loop(0, GDN_GC, chunk, 0)

    blk = (hp, GDN_GC, GDN_C, GDN_DK)
    return pl.pallas_call(
        body, grid=(2, GDN_H // hp, nblk),
        in_specs=[qkv_spec(0), qkv_spec(1), qkv_spec(2), gb_spec, grow_spec] + s_in,
        out_specs=[BS((1, R, wd), lambda d, h, n: (d, order(d, n), h)), st_spec] + s_out,
        out_shape=[SDS((2, S, GDN_H * GDN_DK), F32), SDS((2, GDN_H, nc, GDN_DK, GDN_DK), F32)] + s_shape,
        scratch_shapes=[pltpu.VMEM((hp, GDN_DK, GDN_DK), F32), pltpu.VMEM(blk, F32), pltpu.VMEM(blk, BF16),
                        pltpu.VMEM(blk, BF16), pltpu.VMEM(blk, BF16), pltpu.VMEM((hp, GDN_GC, GDN_C, GDN_C), BF16),
                        pltpu.VMEM((hp, GDN_GC, 1, LANES), F32)] + s_scr_shapes,
        name=name, compiler_params=_cp(("arbitrary", "arbitrary", "arbitrary")))(qkv, qkv, qkv, gb, grow, *s_ops)


def gdn_scan_bwd(qkv, gb, grow, states, do, name, side=None):
    S = qkv.shape[0]
    R = GDN_GC * GDN_C
    nblk = S // R
    hp = GDN_HP_BWD
    wd = hp * GDN_DK
    heads = range(hp)

    def order(d, n):
        return (nblk - 1 - n) - d * (nblk - 1 - 2 * n)

    qkv_spec, gb_spec, grow_spec, st_spec = _gdn_specs(S, nblk, order, hp)

    s_in, s_out, s_shape, s_scr_shapes, s_ops = _side_parts(side)
    hb = GDN_H // hp

    def body(*refs):
        q_ref, k_ref, v_ref, gb_ref, grow_ref, st_ref, do_ref = refs[:7]
        dqkv_ref, dgate_ref = refs[7 + len(s_in):9 + len(s_in)]
        (ds_scr, w_scr, kd_scr, dv1_scr, qtdo_scr, egl_scr, dsin_scr, dvn_scr,
         sdot_scr) = refs[9 + len(s_in) + len(s_out):18 + len(s_in) + len(s_out)]
        d = pl.program_id(0)
        h = pl.program_id(1)
        n = pl.program_id(2)
        _side_run(side, refs, 7, 2, 9, (d == 0) & (h == 0) & (n == 0), (d == 1) & (h == hb - 1) & (n == nblk - 1))

        @pl.when(n == 0)
        def _():
            ds_scr[...] = jnp.zeros_like(ds_scr)

        keys, ts = _block_terms(q_ref, k_ref, v_ref, gb_ref, grow_ref, d, h, hp)
        for (hh, ci), t in zip(keys, ts):
            rows = slice(ci * GDN_C, (ci + 1) * GDN_C)
            t["wb"] = _bf(t["w"])
            t["dob"] = _bf(do_ref[rows, hh * GDN_DK:(hh + 1) * GDN_DK])
            t["sb"] = _bf(st_ref[0, hh, ci])
        for (hh, ci), t in zip(keys, ts):
            t["vnb"] = _bf(t["u"] - _dot(t["wb"], t["sb"]))
            w_scr[hh, ci] = t["wb"]
            kd_scr[hh, ci] = _bf(t["k_dec"])
            dv1_scr[hh, ci] = _dot_tn(_bf(jnp.where(t["incl"], t["qk"] * t["decay"], 0.0)), t["dob"])
            qtdo_scr[hh, ci] = _dot_tn(_bf(t["q_dec"]), t["dob"])
            egl_scr[hh, ci] = _lane_row(jnp.exp(t["glast"]))

        def chunk(cc, carry):
            ci = (GDN_GC - 1 - cc) - d * (GDN_GC - 1 - 2 * cc)
            dsns = [ds_scr[hh] for hh in heads]
            dsbs = [_bf(x) for x in dsns]
            dvns = [dv1_scr[hh, ci] + _dot(kd_scr[hh, ci], dsbs[hh]) for hh in heads]
            for hh in heads:
                ds_scr[hh] = qtdo_scr[hh, ci] + egl_scr[hh, ci] * dsns[hh] - _dot_tn(w_scr[hh, ci], _bf(dvns[hh]))
            for hh in heads:
                dsin_scr[hh, ci] = dsbs[hh]
                dvn_scr[hh, ci] = dvns[hh]
                sd = jnp.sum(jnp.sum(st_ref[0, hh, ci] * dsns[hh], axis=1, keepdims=True), axis=0, keepdims=True)
                sdot_scr[hh, ci] = _lane_row(sd)
            return carry

        lax.fori_loop(0, GDN_GC, chunk, 0)

        for (hh, ci), t in zip(keys, ts):
            t["d_vnew"] = dvn_scr[hh, ci]
            t["dvb"] = _bf(t["d_vnew"])
            t["dsb"] = dsin_scr[hh, ci]
        for t in ts:
            t["d_intra"] = jnp.where(t["incl"], _dot_nt(t["dob"], t["vnb"]), 0.0)
            t["d_qdec"] = _dot_nt(t["dob"], t["sb"])
            t["d_kdec"] = _dot_nt(t["vnb"], t["dsb"])
            t["dw"] = -_dot_nt(t["dvb"], t["sb"])
        for t in ts:
            tts = _split(t["tinv"].T)
            t["d_ru"] = _dot3(tts, _split(t["d_vnew"]))
            t["d_rw"] = _dot3(tts, _split(t["dw"]))
        for t in ts:
            t["da"] = -jnp.where(t["strict"], _dot_nt(_bf(t["d_ru"]), _bf(t["u"])) + _dot_nt(_bf(t["d_rw"]), t["wb"]), 0.0)
        for (hh, ci), t in zip(keys, ts):
            rows = slice(ci * GDN_C, (ci + 1) * GDN_C)
            cols = slice(hh * GDN_DK, (hh + 1) * GDN_DK)
            q, k, v = q_ref[rows, cols], k_ref[rows, cols], t["v"]
            decay, kb, eg, ek, bcol = t["decay"], t["kb"], t["eg"], t["ek"], t["bcol"]
            d_ru, d_rw, da, d_intra, d_qdec, d_kdec = t["d_ru"], t["d_rw"], t["da"], t["d_intra"], t["d_qdec"], t["d_kdec"]
            kbf, qbf = _bf(k), _bf(q)
            dgl = egl_scr[hh, ci][:, 0:1] * sdot_scr[hh, ci][:, 0:1]
            dv = d_ru * bcol
            dbeta = jnp.sum(d_ru * v, axis=1, keepdims=True)
            dkb = d_rw * eg
            dg = jnp.sum(d_rw * kb, axis=1, keepdims=True) * eg
            dkk = _bf(da * decay)
            dqk = _bf(d_intra * decay)
            dkb = dkb + _dot(dkk, kbf)
            dk = _dot_tn(dkk, _bf(kb)) + _dot_tn(dqk, qbf)
            dq = _dot(dqk, kbf) + d_qdec * eg
            dd = (da * t["kk"] + d_intra * t["qk"]) * decay
            dg = dg + jnp.sum(dd, axis=1, keepdims=True) - jnp.sum(dd.T, axis=1, keepdims=True)
            dg = dg + jnp.sum(d_qdec * t["q_dec"], axis=1, keepdims=True)
            dk = dk + d_kdec * ek
            ee = jnp.sum(d_kdec * t["k_dec"], axis=1, keepdims=True)
            dg = dg - ee
            dgl = dgl + jnp.sum(ee, axis=0, keepdims=True)
            dk = dk + dkb * bcol
            dbeta = dbeta + jnp.sum(dkb * k, axis=1, keepdims=True)
            ridx = lax.broadcasted_iota(jnp.int32, (GDN_C, 1), 0)
            dg = dg + jnp.where(ridx == (GDN_C - 1) * (1 - d), dgl, 0.0)
            dqkv_ref[0, 0, rows, cols] = dq
            dqkv_ref[0, 1, rows, cols] = dk
            dqkv_ref[0, 2, rows, cols] = dv
            lane2 = lax.broadcasted_iota(jnp.int32, (GDN_C, 2), 1)
            dgate_ref[0, hh, rows, :] = jnp.where(lane2 == 0, dg, dbeta)

    blk = (hp, GDN_GC, GDN_C, GDN_DK)
    sq = (hp, GDN_GC, GDN_DK, GDN_DK)
    row = (hp, GDN_GC, 1, LANES)
    return pl.pallas_call(
        body, grid=(2, GDN_H // hp, nblk),
        in_specs=[qkv_spec(0), qkv_spec(1), qkv_spec(2), gb_spec, grow_spec, st_spec,
                  BS((R, wd), lambda d, h, n: (order(d, n), h))] + s_in,
        out_specs=[BS((1, 3, R, wd), lambda d, h, n: (d, 0, order(d, n), h)),
                   BS((1, hp, R, 2), lambda d, h, n: (d, h, order(d, n), 0))] + s_out,
        out_shape=[SDS((2, 3, S, GDN_H * GDN_DK), F32), SDS((2, GDN_H, S, 2), F32)] + s_shape,
        scratch_shapes=[pltpu.VMEM((hp, GDN_DK, GDN_DK), F32), pltpu.VMEM(blk, BF16), pltpu.VMEM(blk, BF16),
                        pltpu.VMEM(blk, F32), pltpu.VMEM(sq, F32), pltpu.VMEM(row, F32), pltpu.VMEM(sq, BF16),
                        pltpu.VMEM(blk, F32), pltpu.VMEM(row, F32)] + s_scr_shapes,
        name=name, compiler_params=_cp(("arbitrary", "arbitrary", "arbitrary")))(qkv, qkv, qkv, gb, grow, states, do, *s_ops)


def gdn_post_fwd(o2, proj, nw, name):
    S = proj.shape[0]
    tm = min(512, S)
    zoff = GDN_QKV // LANES

    def body(o_ref, z_ref, nw_ref, y_ref):
        o = o_ref[0] + o_ref[1]
        z = z_ref[...]
        r = lax.rsqrt(jnp.mean(o * o, axis=-1, keepdims=True) + RMS_EPS)
        y_ref[...] = (o * r * nw_ref[...] * (z * _sigmoid(z))).astype(BF16)

    return pl.pallas_call(
        body, grid=(S // tm, GDN_H),
        in_specs=[BS((2, tm, LANES), lambda i, h: (0, i, h)), BS((tm, LANES), lambda i, h: (i, zoff + h)),
                  BS((1, LANES), lambda i, h: (0, 0))],
        out_specs=BS((tm, LANES), lambda i, h: (i, h)),
        out_shape=SDS((S, GDN_H * GDN_DK), BF16), name=name, compiler_params=_cp(("parallel", "parallel")))(o2, proj, nw)


def gdn_post_bwd(o2, proj, nw, dy, name):
    S = proj.shape[0]
    tm = min(512, S)
    zoff = GDN_QKV // LANES

    def body(o_ref, z_ref, nw_ref, dy_ref, do_ref, dz_ref, dnw_ref):
        first = (pl.program_id(0) == 0) & (pl.program_id(1) == 0)
        o = o_ref[0] + o_ref[1]
        z = z_ref[...]
        nwv = nw_ref[...]
        dyv = dy_ref[...]
        r = lax.rsqrt(jnp.mean(o * o, axis=-1, keepdims=True) + RMS_EPS)
        n = o * r
        sg = _sigmoid(z)
        sz = z * sg
        dz_ref[...] = (dyv * n * nwv * (sg * (1.0 + z * (1.0 - sg)))).astype(BF16)
        dn = dyv * nwv * sz
        do_ref[...] = r * (dn - n * jnp.mean(dn * n, axis=-1, keepdims=True))
        part = jnp.sum(dyv * n * sz, axis=0, keepdims=True)

        @pl.when(first)
        def _():
            dnw_ref[...] = part

        @pl.when(jnp.logical_not(first))
        def _():
            dnw_ref[...] += part

    blk = BS((tm, LANES), lambda i, h: (i, h))
    return pl.pallas_call(
        body, grid=(S // tm, GDN_H),
        in_specs=[BS((2, tm, LANES), lambda i, h: (0, i, h)), BS((tm, LANES), lambda i, h: (i, zoff + h)),
                  BS((1, LANES), lambda i, h: (0, 0)), blk],
        out_specs=[blk, blk, BS((1, LANES), lambda i, h: (0, 0))],
        out_shape=[SDS((S, GDN_H * GDN_DK), F32), SDS((S, GDN_H * GDN_DK), BF16), SDS((1, LANES), F32)],
        name=name, compiler_params=_cp(("arbitrary", "arbitrary")))(o2, proj, nw, dy)


def _gate_prm(a_log, dt_bias):
    z = jnp.zeros((8, LANES), F32)
    z = z.at[0, :2 * GDN_H].set(a_log.reshape(-1))
    return z.at[1, :2 * GDN_H].set(dt_bias.reshape(-1))


def gdn_fwd(x, g, w_all, convw, a_log, dt_bias, nw, w_out, tag, side=None):
    S = x.shape[0]
    h, ht = rms_fwd(x, g, f"{tag}_rms")
    proj = mm(h, w_all, name=f"{tag}_proj", tn=1408)
    qkv = gdn_pre_fwd(proj, convw, f"{tag}_pre")
    prm = _gate_prm(a_log, dt_bias)
    gb = gdn_gate_fwd(proj, prm, f"{tag}_gate")
    grow = gb[:, :2 * GDN_H].T.reshape(2 * GDN_H, S // GDN_C, GDN_C)
    o2, states, *side_out = gdn_scan_fwd(qkv, gb, grow, f"{tag}_scan", side)
    y = gdn_post_fwd(o2, proj, nw, f"{tag}_post")
    xn = mm(y, w_out, name=f"{tag}_out", epi=_add, extras=(x,))
    return xn, (ht, proj, qkv, prm, gb, grow, o2, states, y), side_out


def gdn_bwd(x, g, w_all, convw, nw, w_out, saved, dx, dxb, tag, side=None):
    S = x.shape[0]
    ht, proj, qkv, prm, gb, grow, o2, states, y = saved
    dw_out = mm(y, dxb, name=f"{tag}_dwout", ta=True)
    dy = mm(dxb, w_out, name=f"{tag}_dy", tb=True)
    do, dz, dnw = gdn_post_bwd(o2, proj, nw, dy, f"{tag}_postb")
    dqkv, dgate, *side_out = gdn_scan_bwd(qkv, gb, grow, states, do, f"{tag}_scanb", side)
    dgb = jnp.transpose(dgate, (2, 3, 0, 1)).reshape(S, 4 * GDN_H)
    dgb = jnp.pad(dgb, ((0, 0), (0, LANES - 4 * GDN_H)))
    dab, dprm = gdn_gate_bwd(proj, prm, dgb, f"{tag}_gateb")
    dpq, dconvw = gdn_pre_bwd(proj, convw, dqkv, f"{tag}_preb")
    dproj = jnp.concatenate([dpq, dz, dab], axis=1)
    dw_all = mm(ht, dproj, name=f"{tag}_dwin", tn=384)
    dx, dxb, dg = mm_rms_bwd(dproj, w_all, x, g, dx, f"{tag}_dh_rmsb")
    da_log = dprm[0, :2 * GDN_H].reshape(2, GDN_H)
    ddt = dprm[1, :2 * GDN_H].reshape(2, GDN_H)
    return dx, dxb, dg, dw_all, dconvw, da_log, ddt, dnw, dw_out, side_out


def _rel_bucket_np(rel):
    nb = REL_BUCKETS // 2
    max_exact = nb // 2
    ret = np.where(rel > 0, nb, 0)
    n = np.abs(rel)
    nf = np.maximum(n, 1).astype(np.float32)
    large = max_exact + (np.log(nf / max_exact) / np.float32(math.log(REL_MAX_DIST / max_exact))
                         * (nb - max_exact)).astype(np.int32)
    large = np.minimum(large, nb - 1)
    return ret + np.where(n < max_exact, n, large)


def _toeplitz(f, rows, cols):
    period = rows + cols
    e = jnp.pad(f, ((0, 0), (0, period - f.shape[1])))
    y = jnp.tile(e, (1, rows))[:, :rows * (period - 1)]
    return y.reshape(f.shape[0], rows, period - 1)[:, :, :cols]


ATT_Q = DSWA_HALF
ATT_W = 3 * DSWA_HALF
ATT_TB = 1024
ATT_PAIRS = DSWA_HG // 2


def _bias_mats(rel_table, gi):
    _, dil = DSWA_CFG[gi]
    offs = np.arange(-DSWA_HALF, DSWA_HALF + 1)
    onehot = jnp.asarray(np.eye(REL_BUCKETS, dtype=np.float32)[_rel_bucket_np(offs * dil)])
    f = jnp.dot(onehot, rel_table, precision=HI)[:, gi * DSWA_HG:(gi + 1) * DSWA_HG].T
    bias = _toeplitz(f, ATT_Q, ATT_W)
    bias_t = jnp.transpose(_toeplitz(f[:, ::-1], ATT_Q, ATT_W), (0, 2, 1))
    return bias.reshape(ATT_PAIRS, 2, ATT_Q, ATT_W), bias_t.reshape(ATT_PAIRS, 2, ATT_W, ATT_Q)


def _att_specs(S, d, col):
    halo = DSWA_HALF * d
    per = ATT_TB // halo
    last = S // halo - 1
    cur = BS((ATT_TB, LANES), lambda p, tb: (tb, col(p)))
    prev = BS((halo, LANES), lambda p, tb: (jnp.maximum(tb * per - 1, 0), col(p)))
    nxt = BS((halo, LANES), lambda p, tb: (jnp.minimum((tb + 1) * per, last), col(p)))
    return prev, cur, nxt


def _att_specs3(S, d, lead):
    halo = DSWA_HALF * d
    per = ATT_TB // halo
    last = S // halo - 1
    cur = BS((1, ATT_TB, LANES), lambda p, tb: (lead(p), tb, 0))
    prev = BS((1, halo, LANES), lambda p, tb: (lead(p), jnp.maximum(tb * per - 1, 0), 0))
    nxt = BS((1, halo, LANES), lambda p, tb: (lead(p), jnp.minimum((tb + 1) * per, last), 0))
    return prev, cur, nxt


class _Pieces:
    def __init__(self, prev, cur, nxt, d, lead=None, cast=None):
        self.refs, self.d, self.lead, self.cast, self.cache = (prev, cur, nxt), d, lead, cast, {}
        self.halo = DSWA_HALF * d
        self.nsb = ATT_TB // self.halo

    def __call__(self, r, sb):
        if (r, sb) not in self.cache:
            ref = self.refs[0] if sb < 0 else self.refs[2] if sb >= self.nsb else self.refs[1]
            start = r + (self.halo * sb if 0 <= sb < self.nsb else 0)
            rows = pl.ds(start, ATT_Q, stride=self.d) if self.d > 1 else pl.ds(start, ATT_Q)
            v = ref[rows, :] if self.lead is None else ref[0, rows, :]
            self.cache[(r, sb)] = v if self.cast is None else v.astype(self.cast)
        return self.cache[(r, sb)]

    def window(self, r, sb):
        return jnp.concatenate([self(r, sb - 1), self(r, sb), self(r, sb + 1)], axis=0)


ATT_GROUP = 8


def _tile_groups(d, nsb):
    tiles = [(r, sb) for r in range(d) for sb in range(nsb)]
    return [tiles[i:i + ATT_GROUP] for i in range(0, len(tiles), ATT_GROUP)]


def _tile_rows(r, sb, d):
    start = r + DSWA_HALF * d * sb
    return pl.ds(start, ATT_Q, stride=d) if d > 1 else pl.ds(start, ATT_Q)


def _tile_valid(tb, r, sb, d, S, transposed):
    shape = (ATT_W, ATT_Q) if transposed else (ATT_Q, ATT_W)
    blk = lax.broadcasted_iota(jnp.int32, shape, 1 if transposed else 0)
    win = lax.broadcasted_iota(jnp.int32, shape, 0 if transposed else 1)
    tok = tb * ATT_TB + r + d * (DSWA_HALF * (sb - 1) + win)
    return (jnp.abs(win - DSWA_HALF - blk) <= DSWA_HALF) & (tok >= 0) & (tok < S)


def _head_masks():
    lane = lax.broadcasted_iota(jnp.int32, (1, LANES), 1)
    return [lane < DSWA_E, lane >= DSWA_E], lane


def attn_fwd(qkv, bias, gi, name):
    S = qkv.shape[0]
    d = DSWA_CFG[gi][1]
    nsb = ATT_TB // (DSWA_HALF * d)
    npair = DSWA_HEADS // 2
    q_spec = _att_specs(S, d, lambda p: gi * ATT_PAIRS + p)[1]
    k_specs = _att_specs(S, d, lambda p: npair + gi * ATT_PAIRS + p)
    v_specs = _att_specs(S, d, lambda p: 2 * npair + gi * ATT_PAIRS + p)

    def body(q_ref, kp, kc, kn, vp, vc, vn, b_ref, o_ref, lse_ref):
        tb = pl.program_id(1)
        masks, lane = _head_masks()
        kpc = _Pieces(kp, kc, kn, d, cast=BF16)
        vpc = _Pieces(vp, vc, vn, d, cast=BF16)
        scale = DSWA_E ** -0.5
        for grp in _tile_groups(d, nsb):
            rows = [_tile_rows(r, sb, d) for r, sb in grp]
            qs = [q_ref[rw, :] for rw in rows]
            kws = [kpc.window(r, sb) for r, sb in grp]
            vws = [vpc.window(r, sb) for r, sb in grp]
            valids = [_tile_valid(tb, r, sb, d, S, False) for r, sb in grp]
            both = [(t, hh) for t in range(len(grp)) for hh in range(2)]
            ss = [_dot_nt(_bf(jnp.where(masks[hh], qs[t], 0.0)), kws[t]) * scale + b_ref[0, hh] for t, hh in both]
            ss = [jnp.where(valids[t], s, NEG_INF) for (t, hh), s in zip(both, ss)]
            ms = [jnp.max(s, axis=-1, keepdims=True) for s in ss]
            ps = [jnp.exp(s - m) for s, m in zip(ss, ms)]
            ls = [jnp.sum(p, axis=-1, keepdims=True) for p in ps]
            os = [_dot(_bf(p / l), vws[t]) for (t, hh), p, l in zip(both, ps, ls)]
            for t, rw in enumerate(rows):
                o_ref[rw, :] = jnp.where(masks[0], os[2 * t], os[2 * t + 1])
                lse_ref[0, rw, :] = (jnp.where(lane == 0, ms[2 * t] + jnp.log(ls[2 * t]), 0.0)
                                     + jnp.where(lane == 1, ms[2 * t + 1] + jnp.log(ls[2 * t + 1]), 0.0))

    return pl.pallas_call(
        body, grid=(ATT_PAIRS, S // ATT_TB),
        in_specs=[q_spec, *k_specs, *v_specs, BS((1, 2, ATT_Q, ATT_W), lambda p, tb: (p, 0, 0, 0))],
        out_specs=[BS((ATT_TB, LANES), lambda p, tb: (tb, p)), BS((1, ATT_TB, LANES), lambda p, tb: (p, tb, 0))],
        out_shape=[SDS((S, DSWA_HG * DSWA_E), F32), SDS((ATT_PAIRS, S, LANES), F32)],
        name=name, compiler_params=_cp(("parallel", "parallel")))(qkv, qkv, qkv, qkv, qkv, qkv, qkv, bias)


def attn_bwd_q(qkv, bias, lse, do, dd, gi, name):
    S = qkv.shape[0]
    d = DSWA_CFG[gi][1]
    nsb = ATT_TB // (DSWA_HALF * d)
    npair = DSWA_HEADS // 2
    q_spec = _att_specs(S, d, lambda p: gi * ATT_PAIRS + p)[1]
    k_specs = _att_specs(S, d, lambda p: npair + gi * ATT_PAIRS + p)
    v_specs = _att_specs(S, d, lambda p: 2 * npair + gi * ATT_PAIRS + p)
    bspec = BS((1, 2, ATT_Q, ATT_W), lambda p, tb: (p, 0, 0, 0))

    def body(q_ref, kp, kc, kn, vp, vc, vn, b_ref, lse_ref, do_ref, dd_ref, dq_ref, db_ref):
        tb = pl.program_id(1)
        masks, lane = _head_masks()
        kpc = _Pieces(kp, kc, kn, d, cast=BF16)
        vpc = _Pieces(vp, vc, vn, d, cast=BF16)
        db = [jnp.zeros((ATT_Q, ATT_W), F32), jnp.zeros((ATT_Q, ATT_W), F32)]
        scale = DSWA_E ** -0.5
        for grp in _tile_groups(d, nsb):
            rows = [_tile_rows(r, sb, d) for r, sb in grp]
            qs = [q_ref[rw, :] for rw in rows]
            dos = [do_ref[0, rw, :] for rw in rows]
            lses = [lse_ref[0, rw, :] for rw in rows]
            dds = [dd_ref[0, 0, rw, :] for rw in rows]
            kws = [kpc.window(r, sb) for r, sb in grp]
            vws = [vpc.window(r, sb) for r, sb in grp]
            valids = [_tile_valid(tb, r, sb, d, S, False) for r, sb in grp]
            both = [(t, hh) for t in range(len(grp)) for hh in range(2)]
            ss = [_dot_nt(_bf(jnp.where(masks[hh], qs[t], 0.0)), kws[t]) * scale + b_ref[0, hh] for t, hh in both]
            dps = [_dot_nt(_bf(jnp.where(masks[hh], dos[t], 0.0)), vws[t]) for t, hh in both]
            ps = [jnp.exp(jnp.where(valids[t], s - lses[t][:, hh:hh + 1], NEG_INF)) for (t, hh), s in zip(both, ss)]
            dss = [p * (dp - dds[t][:, hh:hh + 1]) for (t, hh), p, dp in zip(both, ps, dps)]
            dqs = [_dot(_bf(ds), kws[t]) * scale for (t, hh), ds in zip(both, dss)]
            for t, rw in enumerate(rows):
                dq_ref[rw, :] = jnp.where(masks[0], dqs[2 * t], dqs[2 * t + 1])
                db[0] = db[0] + dss[2 * t]
                db[1] = db[1] + dss[2 * t + 1]

        @pl.when(tb == 0)
        def _():
            db_ref[0, 0] = db[0]
            db_ref[0, 1] = db[1]

        @pl.when(tb > 0)
        def _():
            db_ref[0, 0] += db[0]
            db_ref[0, 1] += db[1]

    return pl.pallas_call(
        body, grid=(ATT_PAIRS, S // ATT_TB),
        in_specs=[q_spec, *k_specs, *v_specs, bspec, BS((1, ATT_TB, LANES), lambda p, tb: (p, tb, 0)),
                  BS((1, ATT_TB, LANES), lambda p, tb: (gi, tb, p)), BS((1, 1, ATT_TB, LANES), lambda p, tb: (gi, p, tb, 0))],
        out_specs=[BS((ATT_TB, LANES), lambda p, tb: (tb, p)), bspec],
        out_shape=[SDS((S, DSWA_HG * DSWA_E), F32), SDS((ATT_PAIRS, 2, ATT_Q, ATT_W), F32)],
        name=name, compiler_params=_cp(("parallel", "arbitrary")))(qkv, qkv, qkv, qkv, qkv, qkv, qkv, bias, lse, do, dd)


def attn_bwd_kv(qkv, bias_t, lse, do, dd, gi, name):
    S = qkv.shape[0]
    d = DSWA_CFG[gi][1]
    nsb = ATT_TB // (DSWA_HALF * d)
    npair = DSWA_HEADS // 2
    q_specs = _att_specs(S, d, lambda p: gi * ATT_PAIRS + p)
    k_spec = _att_specs(S, d, lambda p: npair + gi * ATT_PAIRS + p)[1]
    v_spec = _att_specs(S, d, lambda p: 2 * npair + gi * ATT_PAIRS + p)[1]
    halo = DSWA_HALF * d
    per = ATT_TB // halo
    last = S // halo - 1

    def do_spec(rows, blk):
        return BS((1, rows, LANES), lambda p, tb: (gi, blk(tb), p))

    def dd_spec(rows, blk):
        return BS((1, 1, rows, LANES), lambda p, tb: (gi, p, blk(tb), 0))

    blks = [(halo, lambda tb: jnp.maximum(tb * per - 1, 0)), (ATT_TB, lambda tb: tb),
            (halo, lambda tb: jnp.minimum((tb + 1) * per, last))]
    do_specs = [do_spec(*b) for b in blks]
    dd_specs = [dd_spec(*b) for b in blks]
    lse_specs = _att_specs3(S, d, lambda p: p)

    class _Lead4:
        def __init__(self, ref):
            self.ref = ref

        def __getitem__(self, idx):
            return self.ref[(0,) + idx]

    def body(k_ref, v_ref, qp, qc, qn, dop, doc, don, lp, lc, ln, ddp, ddc, ddn, b_ref, dk_ref, dv_ref):
        tb = pl.program_id(1)
        masks, lane = _head_masks()
        qpc = _Pieces(qp, qc, qn, d)
        dopc = _Pieces(dop, doc, don, d, lead=True)
        lpc = _Pieces(lp, lc, ln, d, lead=True)
        ddpc = _Pieces(_Lead4(ddp), _Lead4(ddc), _Lead4(ddn), d, lead=True)
        scale = DSWA_E ** -0.5
        for grp in _tile_groups(d, nsb):
            rows = [_tile_rows(r, sb, d) for r, sb in grp]
            kcs = [_bf(k_ref[rw, :]) for rw in rows]
            vcs = [_bf(v_ref[rw, :]) for rw in rows]
            qws = [qpc.window(r, sb) for r, sb in grp]
            dows = [dopc.window(r, sb) for r, sb in grp]
            lws = [lpc.window(r, sb) for r, sb in grp]
            ddws = [ddpc.window(r, sb) for r, sb in grp]
            qwbs = [_bf(x) for x in qws]
            dowbs = [_bf(x) for x in dows]
            valids = [_tile_valid(tb, r, sb, d, S, True) for r, sb in grp]
            both = [(t, hh) for t in range(len(grp)) for hh in range(2)]
            ss = [_dot_nt(_bf(jnp.where(masks[hh], qws[t], 0.0)), kcs[t]) * scale + b_ref[0, hh] for t, hh in both]
            dps = [_dot_nt(_bf(jnp.where(masks[hh], dows[t], 0.0)), vcs[t]) for t, hh in both]
            ps = [jnp.exp(jnp.where(valids[t], s - lws[t][:, hh:hh + 1], NEG_INF)) for (t, hh), s in zip(both, ss)]
            dvs = [_dot_tn(_bf(p), dowbs[t]) for (t, hh), p in zip(both, ps)]
            dss = [p * (dp - ddws[t][:, hh:hh + 1]) for (t, hh), p, dp in zip(both, ps, dps)]
            dks = [_dot_tn(_bf(ds), qwbs[t]) * scale for (t, hh), ds in zip(both, dss)]
            for t, rw in enumerate(rows):
                dk_ref[rw, :] = jnp.where(masks[0], dks[2 * t], dks[2 * t + 1])
                dv_ref[rw, :] = jnp.where(masks[0], dvs[2 * t], dvs[2 * t + 1])

    out = BS((ATT_TB, LANES), lambda p, tb: (tb, p))
    return pl.pallas_call(
        body, grid=(ATT_PAIRS, S // ATT_TB),
        in_specs=[k_spec, v_spec, *q_specs, *do_specs, *lse_specs, *dd_specs,
                  BS((1, 2, ATT_W, ATT_Q), lambda p, tb: (p, 0, 0, 0))],
        out_specs=[out, out],
        out_shape=[SDS((S, DSWA_HG * DSWA_E), F32), SDS((S, DSWA_HG * DSWA_E), F32)],
        name=name, compiler_params=_cp(("parallel", "parallel")))(
            qkv, qkv, qkv, qkv, qkv, do, do, do, lse, lse, lse, dd, dd, dd, bias_t)


def _pair_alphas(lses):
    m = jnp.maximum(jnp.maximum(lses[0], lses[1]), lses[2])
    e = [jnp.exp(t - m) for t in lses]
    tot = e[0] + e[1] + e[2]
    return [t / tot for t in e]


def _pair_expand(a, lane):
    return jnp.where(lane < DSWA_E, a[:, 0:1], a[:, 1:2])


def combine_fwd(o_raw, lse, name):
    S = o_raw.shape[0]
    tm = min(1024, S)

    def body(o_ref, l_ref, y_ref):
        g = pl.program_id(2)
        lane = lax.broadcasted_iota(jnp.int32, (1, LANES), 1)
        alphas = _pair_alphas([l_ref[0, 0], l_ref[1, 0], l_ref[2, 0]])
        a = jnp.where(g == 0, alphas[0], jnp.where(g == 1, alphas[1], alphas[2]))
        y_ref[...] = (o_ref[...] * _pair_expand(a, lane)).astype(BF16)

    blk = BS((tm, LANES), lambda i, p, g: (i, g * ATT_PAIRS + p))
    return pl.pallas_call(
        body, grid=(S // tm, ATT_PAIRS, 3),
        in_specs=[blk, BS((3, 1, tm, LANES), lambda i, p, g: (0, p, i, 0))], out_specs=blk,
        out_shape=SDS((S, DSWA_W), BF16), name=name, compiler_params=_cp(("parallel", "parallel", "parallel")))(o_raw, lse)


def combine_bwd(o_raw, lse, dy, name):
    S = o_raw.shape[0]
    tm = min(512, S)

    def body(o0, o1, o2, l_ref, d0, d1, d2, do_ref, dd_ref):
        lane = lax.broadcasted_iota(jnp.int32, (1, LANES), 1)
        alphas = _pair_alphas([l_ref[0, 0], l_ref[1, 0], l_ref[2, 0]])
        c = jnp.zeros((tm, LANES), F32)
        for g, (o_ref, dy_ref) in enumerate(((o0, d0), (o1, d1), (o2, d2))):
            dyv = dy_ref[...]
            do_ref[g] = dyv * _pair_expand(alphas[g], lane)
            prod = o_ref[...] * dyv
            dal = (jnp.where(lane == 0, jnp.sum(jnp.where(lane < DSWA_E, prod, 0.0), axis=1, keepdims=True), 0.0)
                   + jnp.where(lane == 1, jnp.sum(jnp.where(lane >= DSWA_E, prod, 0.0), axis=1, keepdims=True), 0.0))
            c = c + alphas[g] * dal
        for g in range(3):
            dd_ref[g, 0] = alphas[g] * c

    def col(g):
        return BS((tm, LANES), lambda i, p: (i, g * ATT_PAIRS + p))

    return pl.pallas_call(
        body, grid=(S // tm, ATT_PAIRS),
        in_specs=[col(0), col(1), col(2), BS((3, 1, tm, LANES), lambda i, p: (0, p, i, 0)), col(0), col(1), col(2)],
        out_specs=[BS((3, tm, LANES), lambda i, p: (0, i, p)), BS((3, 1, tm, LANES), lambda i, p: (0, p, i, 0))],
        out_shape=[SDS((3, S, DSWA_HG * DSWA_E), F32), SDS((3, ATT_PAIRS, S, LANES), F32)],
        name=name, compiler_params=_cp(("parallel", "parallel")))(o_raw, o_raw, o_raw, lse, dy, dy, dy)


def dswa_fwd(x, g, w_in, w_out, rel_table, tag):
    h, ht = rms_fwd(x, g, f"{tag}_rms")
    qkv = mm(h, w_in, name=f"{tag}_qkv", tn=1152)
    outs, lses = [], []
    for gi in range(3):
        bias, _ = _bias_mats(rel_table, gi)
        o, lse = attn_fwd(qkv, bias, gi, f"{tag}_att{gi}")
        outs.append(o)
        lses.append(lse)
    o_raw = jnp.concatenate(outs, axis=1)
    lse = jnp.stack(lses)
    y = combine_fwd(o_raw, lse, f"{tag}_comb")
    xn = mm(y, w_out, name=f"{tag}_out", epi=_add, extras=(x,))
    return xn, (ht, qkv, o_raw, lse, y)


def dswa_bwd(x, g, w_in, w_out, rel_table, saved, dx, dxb, tag):
    ht, qkv, o_raw, lse, y = saved
    dw_out = mm(y, dxb, name=f"{tag}_dwout", ta=True, tm=384)
    dy = mm(dxb, w_out, name=f"{tag}_dy", tb=True, tn=384)
    do_raw, dd = combine_bwd(o_raw, lse, dy, f"{tag}_combb")
    dqs, dks, dvs = [], [], []
    drel = jnp.zeros_like(rel_table)
    for gi in range(3):
        (bias, bias_t), bias_vjp = jax.vjp(lambda tbl: _bias_mats(tbl, gi), rel_table)
        dq, dbias = attn_bwd_q(qkv, bias, lse[gi], do_raw, dd, gi, f"{tag}_attq{gi}")
        dk, dv = attn_bwd_kv(qkv, bias_t, lse[gi], do_raw, dd, gi, f"{tag}_attkv{gi}")
        drel = drel + bias_vjp((dbias, jnp.zeros_like(bias_t)))[0]
        dqs.append(dq)
        dks.append(dk)
        dvs.append(dv)
    dqkv = jnp.concatenate(dqs + dks + dvs, axis=1).astype(BF16)
    dw_in = mm(ht, dqkv, name=f"{tag}_dwin", tn=384)
    dx, dxb, dg = mm_rms_bwd(dqkv, w_in, x, g, dx, f"{tag}_dh_rmsb")
    return dx, dxb, dg, dw_in, dw_out, drel


def adamw(w, g, m, v, name):
    shape = w.shape
    last = shape[-1]
    w2, g2, m2, v2 = (t.reshape(-1, last) for t in (w, g, m, v))
    rows = w2.shape[0]
    tr = rows
    if rows > 512:
        tr = next(t for t in (512, 256, 192, 128, 64, 8) if rows % t == 0)
    c1 = 1.0 / (1.0 - ADAM_B1 ** ADAM_STEP)
    c2 = 1.0 / (1.0 - ADAM_B2 ** ADAM_STEP)

    def body(w_ref, g_ref, m_ref, v_ref, d_ref, nm_ref, nv_ref):
        gv = g_ref[...]
        nm = ADAM_B1 * m_ref[...] + (1.0 - ADAM_B1) * gv
        nv = ADAM_B2 * v_ref[...] + (1.0 - ADAM_B2) * (gv * gv)
        nm_ref[...] = nm
        nv_ref[...] = nv
        d_ref[...] = -ADAM_LR * ((nm * c1) / (jnp.sqrt(nv * c2) + ADAM_EPS) + ADAM_WD * w_ref[...])

    spec = BS((tr, last), lambda i: (i, 0))
    outs = pl.pallas_call(
        body, grid=(rows // tr,), in_specs=[spec] * 4, out_specs=[spec] * 3,
        out_shape=[SDS((rows, last), F32)] * 3, name=name, compiler_params=_cp(("parallel",)))(w2, g2, m2, v2)
    return tuple(o.reshape(shape) for o in outs)


def _place():
    x, y, c = lax.axis_index("x"), lax.axis_index("y"), lax.axis_index("c")
    chips = [(1 - x, y), (x, 1 - y), (1 - x, 1 - y)]
    return x, y, c, chips


def _rcopy(src, dst, ssem, rsem, dev):
    return pltpu.make_async_remote_copy(src_ref=src, dst_ref=dst, send_sem=ssem, recv_sem=rsem, device_id=dev,
                                        device_id_type=MESH)


class SideJob(NamedTuple):
    ins: list
    outs: list
    sems: list
    start: Callable
    wait: Callable


def _job(ins, outs, sems, copies):
    def start(in_refs, out_refs, sem_refs):
        for cp in copies(in_refs, out_refs, sem_refs):
            cp.start()

    def wait(in_refs, out_refs, sem_refs):
        for cp in copies(in_refs, out_refs, sem_refs):
            cp.wait()

    return SideJob(list(ins), list(outs), list(sems), start, wait)


def gather_job(packs, halved):
    n = len(packs)
    dma = pltpu.SemaphoreType.DMA

    def copies(in_refs, out_refs, sems):
        ssem, rsem = sems
        x, y, c, chips = _place()
        jme = 2 * x + y
        cps = []
        for i, (p_ref, f_ref) in enumerate(zip(in_refs, out_refs)):
            rows = p_ref.shape[0]
            mine = pl.ds(c * (rows // 2), rows // 2) if halved[i] else pl.ds(0, rows)
            for r, (cx, cy) in enumerate(chips):
                cps.append(_rcopy(p_ref.at[mine], f_ref.at[jme, mine], ssem.at[i, r], rsem.at[i, r], (cx, cy, c)))
        return cps

    return _job(packs, [SDS((4,) + p.shape, p.dtype) for p in packs], [dma((n, 3)), dma((n, 3))], copies)


def chip_exchange_job(parts):
    n = len(parts)
    dma = pltpu.SemaphoreType.DMA

    def copies(in_refs, out_refs, sems):
        ssem, rsem = sems
        x, y, c, chips = _place()
        cps = []
        for i, (p_ref, r_ref) in enumerate(zip(in_refs, out_refs)):
            for r, (cx, cy) in enumerate(chips):
                cps.append(_rcopy(p_ref.at[2 * cx + cy], r_ref.at[r], ssem.at[i, r], rsem.at[i, r], (cx, cy, c)))
        return cps

    return _job(parts, [SDS((3,) + p.shape[1:], p.dtype) for p in parts], [dma((n, 3)), dma((n, 3))], copies)


def run_job(job, name):
    ni, no = len(job.ins), len(job.outs)

    def body(*refs):
        job.start(refs[:ni], refs[ni:ni + no], refs[ni + no:])
        job.wait(refs[:ni], refs[ni:ni + no], refs[ni + no:])

    return pl.pallas_call(
        body, in_specs=[ANY] * ni, out_specs=[ANY] * no, out_shape=job.outs, scratch_shapes=job.sems, name=name,
        compiler_params=pltpu.CompilerParams(has_side_effects=True))(*job.ins)


def forward_to_sibling(fulls, name):
    n = len(fulls)

    def body(*refs):
        in_refs, out_refs, (ssem, rsem) = refs[:n], refs[n:2 * n], refs[2 * n:]
        x, y, c, chips = _place()
        cps = []
        for i in range(n):
            half = in_refs[i].shape[1] // 2
            for r, (cx, cy) in enumerate(chips):
                piece = (2 * cx + cy, pl.ds(c * half, half))
                cps.append(_rcopy(in_refs[i].at[piece], out_refs[i].at[piece], ssem.at[i, r], rsem.at[i, r], (x, y, 1 - c)))
        for cp in cps:
            cp.start()
        for cp in cps:
            cp.wait()

    dma = pltpu.SemaphoreType.DMA
    return pl.pallas_call(
        body, in_specs=[ANY] * n, out_specs=[ANY] * n, out_shape=[SDS(f.shape, f.dtype) for f in fulls],
        scratch_shapes=[dma((n, 3)), dma((n, 3))], input_output_aliases={i: i for i in range(n)}, name=name,
        compiler_params=pltpu.CompilerParams(has_side_effects=True))(*fulls)


def rs_sibling_exchange(gpack, name):
    _, rows, W = gpack.shape
    half = rows // 2

    def body(g_ref, r_ref, ssem, rsem):
        x, y, c, _ = _place()
        cps = [_rcopy(g_ref.at[j, pl.ds((1 - c) * half, half)], r_ref.at[j], ssem.at[j], rsem.at[j], (x, y, 1 - c))
               for j in range(4)]
        for cp in cps:
            cp.start()
        for cp in cps:
            cp.wait()

    dma = pltpu.SemaphoreType.DMA
    return pl.pallas_call(
        body, in_specs=[ANY], out_specs=ANY, out_shape=SDS((4, half, W), gpack.dtype),
        scratch_shapes=[dma((4,)), dma((4,))], name=name,
        compiler_params=pltpu.CompilerParams(has_side_effects=True))(gpack)


def _div_tile(n, limit):
    return next(t for t in range(limit - limit % 16, 0, -16) if n % t == 0)


def rs_add_sibling(gpack, recv, cidx, name, out_dtype=F32):
    _, rows, W = gpack.shape
    half = rows // 2
    tr = _div_tile(half, 1024)
    nb = half // tr

    def body(c_ref, g_ref, r_ref, o_ref):
        o_ref[...] = (g_ref[...].astype(F32) + r_ref[...].astype(F32)).astype(o_ref.dtype)

    gs = pltpu.PrefetchScalarGridSpec(
        num_scalar_prefetch=1, grid=(4, nb),
        in_specs=[BS((1, tr, W), lambda j, i, c: (j, c[0] * nb + i, 0)), BS((1, tr, W), lambda j, i, c: (j, i, 0))],
        out_specs=BS((1, tr, W), lambda j, i, c: (j, i, 0)))
    return pl.pallas_call(body, grid_spec=gs, out_shape=SDS((4, half, W), out_dtype), name=name,
                          compiler_params=_cp(("parallel", "parallel")))(cidx, gpack, recv)


def rs_add_chips(recv, part, place, name):
    _, half, W = recv.shape
    tr = _div_tile(half, 640)
    nb = half // tr

    def body(x_ref, y_ref, c_ref, r_ref, own_ref, o_ref):
        r0, r1, r2, own = (t.astype(F32) for t in (r_ref[0], r_ref[1], r_ref[2], own_ref[0]))
        o_ref[...] = ((r0 + r1) + r2) + own

    gs = pltpu.PrefetchScalarGridSpec(
        num_scalar_prefetch=3, grid=(nb,),
        in_specs=[BS((3, tr, W), lambda i, x, y, c: (0, i, 0)), BS((1, tr, W), lambda i, x, y, c: (2 * x[0] + y[0], i, 0))],
        out_specs=BS((tr, W), lambda i, x, y, c: (c[0] * nb + i, 0)))
    return pl.pallas_call(body, grid_spec=gs, out_shape=SDS((2 * half, W), F32), name=name,
                          compiler_params=_cp(("parallel",)))(*place, recv, part)


def rs_sibling_share(gsh, name):
    rows, W = gsh.shape
    half = rows // 2

    def body(g_ref, o_ref, ssem, rsem):
        x, y, c, _ = _place()
        mine = pl.ds(c * half, half)
        cp = _rcopy(g_ref.at[mine], o_ref.at[mine], ssem, rsem, (x, y, 1 - c))
        cp.start()
        cp.wait()

    dma = pltpu.SemaphoreType.DMA
    return pl.pallas_call(
        body, in_specs=[ANY], out_specs=ANY, out_shape=SDS(gsh.shape, gsh.dtype),
        scratch_shapes=[dma, dma], input_output_aliases={0: 0}, name=name,
        compiler_params=pltpu.CompilerParams(has_side_effects=True))(gsh)


def allreduce_small(pack):
    R = pack.shape[0]

    def body(p_ref, o_ref, all_ref, ssem, rsem):
        x, y, c, _ = _place()
        me = 4 * x + 2 * y + c
        all_ref[me] = p_ref[...]
        cps = []
        for m in range(1, 8):
            peer = (1 - x if m & 4 else x, 1 - y if m & 2 else y, 1 - c if m & 1 else c)
            cp = _rcopy(p_ref, all_ref.at[me], ssem.at[m - 1], rsem.at[m - 1], peer)
            cp.start()
            cps.append(cp)
        for cp in cps:
            cp.wait()
        acc = all_ref[0]
        for i in range(1, 8):
            acc = acc + all_ref[i]
        o_ref[...] = acc

    dma = pltpu.SemaphoreType.DMA
    vm = BS(memory_space=pltpu.VMEM)
    return pl.pallas_call(
        body, in_specs=[vm], out_specs=vm, out_shape=SDS(pack.shape, F32),
        scratch_shapes=[pltpu.VMEM((8, R, LANES), F32), dma((7,)), dma((7,))], name="allreduce_small",
        compiler_params=pltpu.CompilerParams(has_side_effects=True))(pack)


PACK_W = 1024
PACK_ALIGN = 32


def _layer_entries(l):
    if l % 2 == 0:
        mixer = [("gdn_w_in", l // 2, D_MODEL, GDN_IN // 4, True), ("gdn_w_out", l // 2, D_MODEL // 4, D_MODEL, False)]
    else:
        mixer = [("dswa_w_in", l // 2, D_MODEL, 3 * DSWA_W // 4, True), ("dswa_w_out", l // 2, DSWA_W // 4, D_MODEL, False)]
    return mixer + [("mlp_w1", l, D_MODEL, D_FF // 4, True), ("mlp_w2", l, D_FF // 4, D_MODEL, False)]


def _entries_offsets(entries):
    offs = [int(o) for o in np.cumsum([0] + [r * c // PACK_W for (_, _, r, c, _) in entries])]
    return offs, -(-offs[-1] // PACK_ALIGN) * PACK_ALIGN


def _layer_offsets(l):
    return _entries_offsets(_layer_entries(l))


def _pack_layer(l, shards, dtype):
    offs, total = _layer_offsets(l)
    parts = [shards[name][li].astype(dtype).reshape(-1, PACK_W) for (name, li, _, _, _) in _layer_entries(l)]
    parts.append(jnp.zeros((total - offs[-1], PACK_W), dtype))
    return jnp.concatenate(parts, axis=0)


def _unpack_layer(l, full, own, jme):
    offs, _ = _layer_offsets(l)
    mats = []
    for e, (_, _, r, c, by_col) in enumerate(_layer_entries(l)):
        mine = own[offs[e]:offs[e + 1]]
        sh = [jnp.where(jme == j, mine, full[j, offs[e]:offs[e + 1]]).reshape(r, c) for j in range(4)]
        mats.append(jnp.concatenate(sh, axis=1 if by_col else 0))
    return mats


def _pack_grads(entries, grads):
    offs, total = _entries_offsets(entries)
    per_chip = []
    for j in range(4):
        parts = []
        for g, (_, _, r, c, by_col) in zip(grads, entries):
            sh = g[:, c * j:c * (j + 1)] if by_col else g[r * j:r * (j + 1), :]
            parts.append(sh.astype(BF16).reshape(-1, PACK_W))
        parts.append(jnp.zeros((total - offs[-1], PACK_W), BF16))
        per_chip.append(jnp.concatenate(parts, axis=0))
    return jnp.stack(per_chip)


def _unpack_shard_grads(units):
    out = {}
    for entries, gsh in units:
        offs, _ = _entries_offsets(entries)
        for e, (name, _, r, c, _) in enumerate(entries):
            out.setdefault(name, []).append(gsh[offs[e]:offs[e + 1]].reshape(r, c))
    return {k: jnp.stack(v) for k, v in out.items()}


def _flat_pad(t, mult=8 * LANES):
    f = t.reshape(-1)
    return jnp.pad(f, (0, (-f.shape[0]) % mult))


def kernel(x, norm_mix, norm_mlp, norm_final, rel_bias, gdn_w_in, gdn_conv_w, gdn_a_log, gdn_dt_bias, gdn_norm_w, gdn_w_out, dswa_w_in, dswa_w_out, mlp_w1, mlp_w2, loss_target, m_norm_mix, m_norm_mlp, m_norm_final, m_rel_bias, m_gdn_w_in, m_gdn_conv_w, m_gdn_a_log, m_gdn_dt_bias, m_gdn_norm_w, m_gdn_w_out, m_dswa_w_in, m_dswa_w_out, m_mlp_w1, m_mlp_w2, v_norm_mix, v_norm_mlp, v_norm_final, v_rel_bias, v_gdn_w_in, v_gdn_conv_w, v_gdn_a_log, v_gdn_dt_bias, v_gdn_norm_w, v_gdn_w_out, v_dswa_w_in, v_dswa_w_out, v_mlp_w1, v_mlp_w2):
    xi, yi, ci = lax.axis_index("x"), lax.axis_index("y"), lax.axis_index("c")
    jme = 2 * xi + yi
    big = dict(gdn_w_in=gdn_w_in, gdn_w_out=gdn_w_out, dswa_w_in=dswa_w_in, dswa_w_out=dswa_w_out, mlp_w1=mlp_w1, mlp_w2=mlp_w2)
    n_gdn = gdn_w_in.shape[0]
    conv_cols = gdn_conv_w.shape[-1]

    packs = [_pack_layer(l, big, BF16) for l in range(DEPTH)]
    convp = jnp.pad(gdn_conv_w.reshape(n_gdn * GDN_CONV, conv_cols), ((0, 16 - n_gdn * GDN_CONV), (0, 0)))
    raw0, cfull = run_job(gather_job([packs[0], convp], [True, False]), "gather_l0")
    fulls = {0: forward_to_sibling([raw0], "forward_l0")[0]}
    cfull = jnp.where((jnp.arange(4) == jme)[:, None, None], convp[None], cfull)
    conv_all = jnp.transpose(cfull[:, :n_gdn * GDN_CONV], (1, 0, 2)).reshape(n_gdn, GDN_CONV, 4 * conv_cols)
    conv_all = jnp.pad(conv_all, ((0, 0), (0, 8 - GDN_CONV), (0, 0)))
    fwd_jobs = {0: [1, 2], 2: [3]}

    xs = x[0]
    saved = []
    for l in range(DEPTH):
        w_in, w_out, w1, w2 = _unpack_layer(l, fulls[l], packs[l], jme)
        gm, gp = norm_mix[l][None], norm_mlp[l][None]
        a = l // 2
        if l % 2 == 0:
            w_in = jnp.pad(w_in, ((0, 0), (0, GDN_INP - GDN_IN)))
            job = gather_job([packs[t] for t in fwd_jobs[l]], [True] * len(fwd_jobs[l]))
            x_mid, sv, raws = gdn_fwd(xs, gm, w_in, conv_all[a], gdn_a_log[a], gdn_dt_bias[a], gdn_norm_w[a][None], w_out,
                                      f"l{l}_gdn", job)
            for t, f in zip(fwd_jobs[l], forward_to_sibling(raws, f"forward_from_l{l}")):
                fulls[t] = f
        else:
            x_mid, sv = dswa_fwd(xs, gm, w_in, w_out, rel_bias, f"l{l}_att")
        x_out, sv2 = mlp_fwd(x_mid, gp, w1, w2, f"l{l}_mlp")
        saved.append((xs, x_mid, (w_in, w_out, w1, w2), sv, sv2))
        xs = x_out

    cidx = ci.astype(jnp.int32).reshape(1)
    place = [t.astype(jnp.int32).reshape(1) for t in (xi, yi, ci)]
    units = {"0a": _layer_entries(0)[:2], "0b": _layer_entries(0)[2:], **{str(l): _layer_entries(l) for l in (1, 2, 3)}}

    def chip_partial(u, grads):
        gpack = _pack_grads(units[u], grads)
        return rs_add_sibling(gpack, rs_sibling_exchange(gpack, f"rs_sibling_{u}"), cidx, f"rs_add_sibling_{u}", BF16)

    def finish(u, recv):
        return rs_sibling_share(rs_add_chips(recv, parts[u], place, f"rs_add_chips_{u}"), f"rs_share_{u}")

    loss_part, dx, dxb, d_final = loss_head(xs, norm_final[None], loss_target[0], "loss_head")
    d_mix, d_mlp = [None] * DEPTH, [None] * DEPTH
    d_conv, d_alog, d_dt, d_nw = [None] * n_gdn, [None] * n_gdn, [None] * n_gdn, [None] * n_gdn
    d_rel = jnp.zeros_like(rel_bias)
    parts, gshs = {}, {}
    bwd_jobs = {2: ["3"], 0: ["2", "1", "0b"]}
    for l in reversed(range(DEPTH)):
        x_in, x_mid, (w_in, w_out, w1, w2), sv, sv2 = saved[l]
        gm, gp = norm_mix[l][None], norm_mlp[l][None]
        a = l // 2
        dx, dxb, d_mlp[l], dw1, dw2 = mlp_bwd(x_mid, gp, w1, w2, sv2, dx, dxb, f"l{l}_mlp")
        if l == 0:
            parts["0b"] = chip_partial("0b", [dw1, dw2])
        if l % 2 == 0:
            job = chip_exchange_job([parts[u] for u in bwd_jobs[l]])
            dx, dxb, d_mix[l], dw_all, d_conv[a], d_alog[a], d_dt[a], d_nw[a], dwo, recvs = gdn_bwd(
                x_in, gm, w_in, conv_all[a], gdn_norm_w[a][None], w_out, sv, dx, dxb, f"l{l}_gdn", job)
            for u, rv in zip(bwd_jobs[l], recvs):
                gshs[u] = finish(u, rv)
            dwi = dw_all[:, :GDN_IN]
        else:
            dx, dxb, d_mix[l], dwi, dwo, drel = dswa_bwd(x_in, gm, w_in, w_out, rel_bias, sv, dx, dxb, f"l{l}_att")
            d_rel = d_rel + drel
        if l == 0:
            parts["0a"] = chip_partial("0a", [dwi, dwo])
        else:
            parts[str(l)] = chip_partial(str(l), [dwi, dwo, dw1, dw2])
    gshs["0a"] = finish("0a", run_job(chip_exchange_job([parts["0a"]]), "rs_chip_exchange_0a")[0])
    gbig = _unpack_shard_grads([(units[u], gshs[u]) for u in ("0a", "0b", "1", "2", "3")])

    small = [jnp.concatenate(d_mix, axis=0), jnp.concatenate(d_mlp, axis=0), d_final, d_rel,
             jnp.stack(d_conv), jnp.stack(d_alog), jnp.stack(d_dt), jnp.concatenate(d_nw, axis=0)]
    flat = [_flat_pad(t) for t in small]
    sizes = [f.shape[0] for f in flat]
    red = allreduce_small(jnp.concatenate(flat).reshape(-1, LANES)).reshape(-1)
    offs = np.cumsum([0] + sizes)
    red = [red[offs[i]:offs[i] + small[i].size].reshape(small[i].shape) for i in range(len(small))]
    g_conv_all = red[4][:, :GDN_CONV].reshape(n_gdn, GDN_CONV, 1, 4 * conv_cols)
    g_conv = lax.dynamic_slice_in_dim(g_conv_all, jme * conv_cols, conv_cols, axis=3)
    g = dict(norm_mix=red[0], norm_mlp=red[1], norm_final=red[2].reshape(norm_final.shape), rel_bias=red[3],
             gdn_conv_w=g_conv, gdn_a_log=red[5], gdn_dt_bias=red[6], gdn_norm_w=red[7][:, :GDN_DK], **gbig)

    w = dict(norm_mix=norm_mix, norm_mlp=norm_mlp, norm_final=norm_final, rel_bias=rel_bias, gdn_conv_w=gdn_conv_w,
             gdn_a_log=gdn_a_log, gdn_dt_bias=gdn_dt_bias, gdn_norm_w=gdn_norm_w, **big)
    m = dict(norm_mix=m_norm_mix, norm_mlp=m_norm_mlp, norm_final=m_norm_final, rel_bias=m_rel_bias, gdn_w_in=m_gdn_w_in,
             gdn_conv_w=m_gdn_conv_w, gdn_a_log=m_gdn_a_log, gdn_dt_bias=m_gdn_dt_bias, gdn_norm_w=m_gdn_norm_w,
             gdn_w_out=m_gdn_w_out, dswa_w_in=m_dswa_w_in, dswa_w_out=m_dswa_w_out, mlp_w1=m_mlp_w1, mlp_w2=m_mlp_w2)
    v = dict(norm_mix=v_norm_mix, norm_mlp=v_norm_mlp, norm_final=v_norm_final, rel_bias=v_rel_bias, gdn_w_in=v_gdn_w_in,
             gdn_conv_w=v_gdn_conv_w, gdn_a_log=v_gdn_a_log, gdn_dt_bias=v_gdn_dt_bias, gdn_norm_w=v_gdn_norm_w,
             gdn_w_out=v_gdn_w_out, dswa_w_in=v_dswa_w_in, dswa_w_out=v_dswa_w_out, mlp_w1=v_mlp_w1, mlp_w2=v_mlp_w2)
    names = ["norm_mix", "norm_mlp", "norm_final", "rel_bias", "gdn_w_in", "gdn_conv_w", "gdn_a_log", "gdn_dt_bias",
             "gdn_norm_w", "gdn_w_out", "dswa_w_in", "dswa_w_out", "mlp_w1", "mlp_w2"]
    upd = {n: adamw(w[n], g[n], m[n], v[n], f"adamw_{n}") for n in names}
    loss = lax.psum(loss_part[0, 0], ("x", "y", "c"))
    return (loss, dx[None], *[g[n] for n in names], *[upd[n][0] for n in names], *[upd[n][1] for n in names],
            *[upd[n][2] for n in names])
```
